```python
import math
import jax, jax.numpy as jnp
from jax import lax
import numpy as np

D_MODEL = 2048
BATCH = 8
SEQ = 2048
DEPTH = 1

D_MIX = D_MODEL
HEAD_DIM = 64
D_ATTN = D_MIX // 2
N_Q_HEADS = D_ATTN // HEAD_DIM
N_KV_HEADS = 2
Q_PER_KV = N_Q_HEADS // N_KV_HEADS
D_KV = N_KV_HEADS * HEAD_DIM
WINDOW = 128
BLOCK = WINDOW
D_SGU = D_MIX - D_ATTN
SGU_GROUPS = 8
SGU_GROUP_DIM = D_SGU // SGU_GROUPS
CHUNK = 128
D_IN = D_ATTN + 2 * D_KV + D_ATTN + 3 * D_SGU
EPS = 1e-6

kernel_name = "hybrid_swa_sink_gmlp_parallel_heads"


def rms_norm(x, g):
    xf = x.astype(jnp.float32)
    y = xf * lax.rsqrt(jnp.mean(xf * xf, axis=-1, keepdims=True) + EPS)
    return (y * g.astype(jnp.float32)).astype(x.dtype)


def layer_norm(x, g, b):
    xf = x.astype(jnp.float32)
    mu = jnp.mean(xf, axis=-1, keepdims=True)
    var = jnp.mean(jnp.square(xf - mu), axis=-1, keepdims=True)
    y = (xf - mu) * lax.rsqrt(var + EPS)
    return (y * g.astype(jnp.float32) + b.astype(jnp.float32)).astype(x.dtype)


def sliding_window_sink_attention(q, k, v, sinks):
    B, S = q.shape[0], q.shape[1]
    nb = S // BLOCK
    qb = q.reshape(B, nb, BLOCK, N_KV_HEADS, Q_PER_KV, HEAD_DIM)
    kb = k.reshape(B, nb, BLOCK, N_KV_HEADS, HEAD_DIM)
    vb = v.reshape(B, nb, BLOCK, N_KV_HEADS, HEAD_DIM)
    pad = ((0, 0), (1, 0), (0, 0), (0, 0), (0, 0))
    k_ext = jnp.concatenate([jnp.pad(kb, pad)[:, :-1], kb], axis=2)
    v_ext = jnp.concatenate([jnp.pad(vb, pad)[:, :-1], vb], axis=2)
    scale = 1.0 / math.sqrt(HEAD_DIM)
    scores = jnp.einsum('bnqhgd,bnshd->bnhgqs', qb, k_ext).astype(jnp.float32) * scale
    qpos = jnp.arange(BLOCK)[:, None] + BLOCK
    kpos = jnp.arange(2 * BLOCK)[None, :]
    dist = qpos - kpos
    band = (dist >= 0) & (dist < WINDOW)
    has_prev = (jnp.arange(nb) > 0)[:, None, None] | (kpos >= BLOCK)[None]
    valid = band[None] & has_prev
    scores = jnp.where(valid[None, :, None, None], scores, -jnp.inf)
    sink = sinks.astype(jnp.float32).reshape(N_KV_HEADS, Q_PER_KV)
    sink = jnp.broadcast_to(sink[None, None, :, :, None, None], scores.shape[:-1] + (1,))
    probs = jax.nn.softmax(jnp.concatenate([scores, sink], axis=-1), axis=-1)[..., :-1]
    out = jnp.einsum('bnhgqs,bnshd->bnqhgd', probs.astype(v.dtype), v_ext)
    return out.reshape(B, S, N_Q_HEADS * HEAD_DIM)


def chunked_spatial_gating(u, v, ln_g, ln_b, w_s, b_s):
    B, S = u.shape[0], u.shape[1]
    nc = S // CHUNK
    vn = layer_norm(v, ln_g, ln_b).reshape(B, nc, CHUNK, SGU_GROUPS, SGU_GROUP_DIM)
    causal = jnp.tril(jnp.ones((CHUNK, CHUNK), dtype=w_s.dtype))
    w = w_s * causal[None]
    mixed = jnp.einsum('gts,bnsgc->bntgc', w, vn) + b_s.T[None, None, :, :, None]
    return u * mixed.reshape(B, S, D_SGU)


def _fwd_setup_inputs(seed: int = 0) -> dict:
    key = jax.random.key(seed)
    ks = jax.random.split(key, 14)
    f32 = jnp.float32
    x = jax.random.normal(ks[0], (BATCH, SEQ, D_MODEL), f32)
    c = jax.random.normal(ks[1], (BATCH, D_MODEL), f32)
    norm_g = 1.0 + 0.05 * jax.random.normal(ks[2], (DEPTH, D_MODEL), f32)
    w_ada = 0.5 * D_MODEL ** -0.5 * jax.random.normal(ks[3], (DEPTH, D_MODEL, 3 * D_MODEL), f32)
    b_ada = 0.02 * jax.random.normal(ks[4], (DEPTH, 3 * D_MODEL), f32)
    w_in = D_MODEL ** -0.5 * jax.random.normal(ks[5], (DEPTH, D_MODEL, D_IN), f32)
    attn_sinks = 0.5 * jax.random.normal(ks[6], (DEPTH, N_Q_HEADS), f32)
    sgu_ln_g = 1.0 + 0.05 * jax.random.normal(ks[7], (DEPTH, D_SGU), f32)
    sgu_ln_b = 0.02 * jax.random.normal(ks[8], (DEPTH, D_SGU), f32)
    sgu_w = CHUNK ** -0.5 * jax.random.normal(ks[9], (DEPTH, SGU_GROUPS, CHUNK, CHUNK), f32)
    sgu_b = 1.0 + 0.1 * jax.random.normal(ks[10], (DEPTH, SGU_GROUPS, CHUNK), f32)
    w_out = D_MIX ** -0.5 * jax.random.normal(ks[11], (DEPTH, D_MIX, D_MODEL), f32)
    final_g = 1.0 + 0.05 * jax.random.normal(ks[12], (D_MODEL,), f32)
    return {"x": x, "c": c, "norm_g": norm_g, "w_ada": w_ada, "b_ada": b_ada,
            "w_in": w_in, "attn_sinks": attn_sinks, "sgu_ln_g": sgu_ln_g,
            "sgu_ln_b": sgu_ln_b, "sgu_w": sgu_w, "sgu_b": sgu_b, "w_out": w_out,
            "final_g": final_g}


def _fwd_reference(x, c, norm_g, w_ada, b_ada, w_in, attn_sinks, sgu_ln_g, sgu_ln_b,
              sgu_w, sgu_b, w_out, final_g):
    B, S = x.shape[0], x.shape[1]
    splits = np.cumsum([D_ATTN, D_KV, D_KV, D_ATTN, D_SGU, D_SGU])
    c_act = jax.nn.silu(c)
    for l in range(DEPTH):
        mod = c_act @ w_ada[l] + b_ada[l]
        shift, scale, gate = jnp.split(mod, 3, axis=-1)
        h = rms_norm(x, norm_g[l]) * (1.0 + scale[:, None, :]) + shift[:, None, :]
        z = h @ w_in[l]
        q, k, v, g_attn, u, v_s, g_sgu = jnp.split(z, splits, axis=-1)
        attn = sliding_window_sink_attention(
            q.reshape(B, S, N_Q_HEADS, HEAD_DIM),
            k.reshape(B, S, N_KV_HEADS, HEAD_DIM),
            v.reshape(B, S, N_KV_HEADS, HEAD_DIM),
            attn_sinks[l]) * jax.nn.silu(g_attn)
        sgu = chunked_spatial_gating(u, v_s, sgu_ln_g[l], sgu_ln_b[l],
                                     sgu_w[l], sgu_b[l]) * jax.nn.silu(g_sgu)
        y = jnp.concatenate([attn, sgu], axis=-1) @ w_out[l]
        x = x + gate[:, None, :] * y
    return rms_norm(x, final_g)


import jax as _jax
import jax.numpy as _jnp

TWIN_FORMAT = 'train_step'
FWD_PARAMS = ['x', 'c', 'norm_g', 'w_ada', 'b_ada', 'w_in', 'attn_sinks', 'sgu_ln_g', 'sgu_ln_b', 'sgu_w', 'sgu_b', 'w_out', 'final_g']
TWIN_WEIGHTS = ['norm_g', 'w_ada', 'b_ada', 'w_in', 'attn_sinks', 'sgu_ln_g', 'sgu_ln_b', 'sgu_w', 'sgu_b', 'w_out', 'final_g']
TWIN_DIFF_INPUT = 'x'
TWIN_INPUTS = ['x', 'c', 'norm_g', 'w_ada', 'b_ada', 'w_in', 'attn_sinks', 'sgu_ln_g', 'sgu_ln_b', 'sgu_w', 'sgu_b', 'w_out', 'final_g', 'loss_target', 'm_norm_g', 'm_w_ada', 'm_b_ada', 'm_w_in', 'm_attn_sinks', 'm_sgu_ln_g', 'm_sgu_ln_b', 'm_sgu_w', 'm_sgu_b', 'm_w_out', 'm_final_g', 'v_norm_g', 'v_w_ada', 'v_b_ada', 'v_w_in', 'v_attn_sinks', 'v_sgu_ln_g', 'v_sgu_ln_b', 'v_sgu_w', 'v_sgu_b', 'v_w_out', 'v_final_g']
TWIN_OUTPUTS = ['loss', 'grad_x', 'grad_norm_g', 'grad_w_ada', 'grad_b_ada', 'grad_w_in', 'grad_attn_sinks', 'grad_sgu_ln_g', 'grad_sgu_ln_b', 'grad_sgu_w', 'grad_sgu_b', 'grad_w_out', 'grad_final_g', 'delta_norm_g', 'delta_w_ada', 'delta_b_ada', 'delta_w_in', 'delta_attn_sinks', 'delta_sgu_ln_g', 'delta_sgu_ln_b', 'delta_sgu_w', 'delta_sgu_b', 'delta_w_out', 'delta_final_g', 'new_m_norm_g', 'new_m_w_ada', 'new_m_b_ada', 'new_m_w_in', 'new_m_attn_sinks', 'new_m_sgu_ln_g', 'new_m_sgu_ln_b', 'new_m_sgu_w', 'new_m_sgu_b', 'new_m_w_out', 'new_m_final_g', 'new_v_norm_g', 'new_v_w_ada', 'new_v_b_ada', 'new_v_w_in', 'new_v_attn_sinks', 'new_v_sgu_ln_g', 'new_v_sgu_ln_b', 'new_v_sgu_w', 'new_v_sgu_b', 'new_v_w_out', 'new_v_final_g']
TWIN_LEAF_KINDS = {'loss': 'loss', 'grad_x': 'grad_x', 'grad_norm_g': 'grad_w', 'grad_w_ada': 'grad_w', 'grad_b_ada': 'grad_w', 'grad_w_in': 'grad_w', 'grad_attn_sinks': 'grad_w', 'grad_sgu_ln_g': 'grad_w', 'grad_sgu_ln_b': 'grad_w', 'grad_sgu_w': 'grad_w', 'grad_sgu_b': 'grad_w', 'grad_w_out': 'grad_w', 'grad_final_g': 'grad_w', 'delta_norm_g': 'delta_w', 'delta_w_ada': 'delta_w', 'delta_b_ada': 'delta_w', 'delta_w_in': 'delta_w', 'delta_attn_sinks': 'delta_w', 'delta_sgu_ln_g': 'delta_w', 'delta_sgu_ln_b': 'delta_w', 'delta_sgu_w': 'delta_w', 'delta_sgu_b': 'delta_w', 'delta_w_out': 'delta_w', 'delta_final_g': 'delta_w', 'new_m_norm_g': 'new_m', 'new_m_w_ada': 'new_m', 'new_m_b_ada': 'new_m', 'new_m_w_in': 'new_m', 'new_m_attn_sinks': 'new_m', 'new_m_sgu_ln_g': 'new_m', 'new_m_sgu_ln_b': 'new_m', 'new_m_sgu_w': 'new_m', 'new_m_sgu_b': 'new_m', 'new_m_w_out': 'new_m', 'new_m_final_g': 'new_m', 'new_v_norm_g': 'new_v', 'new_v_w_ada': 'new_v', 'new_v_b_ada': 'new_v', 'new_v_w_in': 'new_v', 'new_v_attn_sinks': 'new_v', 'new_v_sgu_ln_g': 'new_v', 'new_v_sgu_ln_b': 'new_v', 'new_v_sgu_w': 'new_v', 'new_v_sgu_b': 'new_v', 'new_v_w_out': 'new_v', 'new_v_final_g': 'new_v'}


def _forward(args):
    return _fwd_reference(*[args[k] for k in FWD_PARAMS])


def _output_shape():
    out = _jax.eval_shape(lambda: _forward(_fwd_setup_inputs(0)))
    return out.shape, out.dtype

N_MICROBATCH = 1
ADAM_LR = 0.001
ADAM_B1 = 0.9
ADAM_B2 = 0.999
ADAM_EPS = 1e-08
ADAM_WD = 0.01
ADAM_STEP = 10
PER_EXAMPLE_BATCH_AXIS = {'x': 0, 'c': 0, 'loss_target': 0}
SHARED_INPUTS = []
_WEIGHT_DTYPES = {'norm_g': _jnp.float32, 'w_ada': _jnp.float32, 'b_ada': _jnp.float32, 'w_in': _jnp.float32, 'attn_sinks': _jnp.float32, 'sgu_ln_g': _jnp.float32, 'sgu_ln_b': _jnp.float32, 'sgu_w': _jnp.float32, 'sgu_b': _jnp.float32, 'w_out': _jnp.float32, 'final_g': _jnp.float32}
MOMENT_SCALE = {'norm_g': 1.818146e-02, 'w_ada': 2.027679e-02, 'b_ada': 3.572530e-02, 'w_in': 1.162140e-02, 'attn_sinks': 2.339115e-03, 'sgu_ln_g': 9.417063e-03, 'sgu_ln_b': 9.981602e-03, 'sgu_w': 9.519776e-03, 'sgu_b': 1.380077e-02, 'w_out': 1.209260e-02, 'final_g': 8.022642e+00}


def _to_microbatches(a, axis):
    t = _jnp.moveaxis(a, axis, 0)
    t = t.reshape((N_MICROBATCH, t.shape[0] // N_MICROBATCH) + t.shape[1:])
    return _jnp.moveaxis(t, 1, axis + 1)


def setup_inputs(seed: int = 0) -> dict:
    inp = _fwd_setup_inputs(seed)
    key = _jax.random.fold_in(_jax.random.key(seed), 7919)
    shape, _ = _output_shape()
    out = dict(inp)
    out["loss_target"] = _jax.random.normal(_jax.random.fold_in(key, 0), shape, _jnp.float32)
    for i, name in enumerate(TWIN_WEIGHTS):
        w = inp[name].astype(_jnp.float32)
        if MOMENT_SCALE is None:
            s = _jnp.sqrt(_jnp.mean(_jnp.square(w)) + 1e-30)
        else:
            s = MOMENT_SCALE[name]
        km, kv = _jax.random.split(_jax.random.fold_in(key, i + 1))
        out[name] = w
        out["m_" + name] = s * _jax.random.normal(km, w.shape, _jnp.float32)
        out["v_" + name] = (s * s) * _jax.random.uniform(kv, w.shape, _jnp.float32, 0.5, 1.5)
    if N_MICROBATCH > 1:
        for name, axis in PER_EXAMPLE_BATCH_AXIS.items():
            out[name] = _to_microbatches(out[name], axis)
    return {'x': out['x'], 'c': out['c'], 'norm_g': out['norm_g'], 'w_ada': out['w_ada'], 'b_ada': out['b_ada'], 'w_in': out['w_in'], 'attn_sinks': out['attn_sinks'], 'sgu_ln_g': out['sgu_ln_g'], 'sgu_ln_b': out['sgu_ln_b'], 'sgu_w': out['sgu_w'], 'sgu_b': out['sgu_b'], 'w_out': out['w_out'], 'final_g': out['final_g'], 'loss_target': out['loss_target'], 'm_norm_g': out['m_norm_g'], 'm_w_ada': out['m_w_ada'], 'm_b_ada': out['m_b_ada'], 'm_w_in': out['m_w_in'], 'm_attn_sinks': out['m_attn_sinks'], 'm_sgu_ln_g': out['m_sgu_ln_g'], 'm_sgu_ln_b': out['m_sgu_ln_b'], 'm_sgu_w': out['m_sgu_w'], 'm_sgu_b': out['m_sgu_b'], 'm_w_out': out['m_w_out'], 'm_final_g': out['m_final_g'], 'v_norm_g': out['v_norm_g'], 'v_w_ada': out['v_w_ada'], 'v_b_ada': out['v_b_ada'], 'v_w_in': out['v_w_in'], 'v_attn_sinks': out['v_attn_sinks'], 'v_sgu_ln_g': out['v_sgu_ln_g'], 'v_sgu_ln_b': out['v_sgu_ln_b'], 'v_sgu_w': out['v_sgu_w'], 'v_sgu_b': out['v_sgu_b'], 'v_w_out': out['v_w_out'], 'v_final_g': out['v_final_g']}


def _loss(weights, diff, rest, loss_target):
    with _jax.named_scope("forward"):
        args = {**rest, TWIN_DIFF_INPUT: diff, **{k: w.astype(_WEIGHT_DTYPES[k]) for k, w in weights.items()}}
        y = _forward(args)
    with _jax.named_scope("loss_head"):
        err = _jnp.square(y.astype(_jnp.float32) - loss_target)
        return 0.5 * _jnp.sum(_jnp.mean(err, axis=-1)) if err.ndim else 0.5 * err


def _adamw(w, g, m, v):
    m = ADAM_B1 * m + (1.0 - ADAM_B1) * g
    v = ADAM_B2 * v + (1.0 - ADAM_B2) * _jnp.square(g)
    m_hat = m / (1.0 - ADAM_B1 ** ADAM_STEP)
    v_hat = v / (1.0 - ADAM_B2 ** ADAM_STEP)
    delta = -ADAM_LR * (m_hat / (_jnp.sqrt(v_hat) + ADAM_EPS) + ADAM_WD * w)
    return delta, m, v


def reference(x, c, norm_g, w_ada, b_ada, w_in, attn_sinks, sgu_ln_g, sgu_ln_b, sgu_w, sgu_b, w_out, final_g, loss_target, m_norm_g, m_w_ada, m_b_ada, m_w_in, m_attn_sinks, m_sgu_ln_g, m_sgu_ln_b, m_sgu_w, m_sgu_b, m_w_out, m_final_g, v_norm_g, v_w_ada, v_b_ada, v_w_in, v_attn_sinks, v_sgu_ln_g, v_sgu_ln_b, v_sgu_w, v_sgu_b, v_w_out, v_final_g):
    given = dict(x=x, c=c, norm_g=norm_g, w_ada=w_ada, b_ada=b_ada, w_in=w_in, attn_sinks=attn_sinks, sgu_ln_g=sgu_ln_g, sgu_ln_b=sgu_ln_b, sgu_w=sgu_w, sgu_b=sgu_b, w_out=w_out, final_g=final_g, loss_target=loss_target, m_norm_g=m_norm_g, m_w_ada=m_w_ada, m_b_ada=m_b_ada, m_w_in=m_w_in, m_attn_sinks=m_attn_sinks, m_sgu_ln_g=m_sgu_ln_g, m_sgu_ln_b=m_sgu_ln_b, m_sgu_w=m_sgu_w, m_sgu_b=m_sgu_b, m_w_out=m_w_out, m_final_g=m_final_g, v_norm_g=v_norm_g, v_w_ada=v_w_ada, v_b_ada=v_b_ada, v_w_in=v_w_in, v_attn_sinks=v_attn_sinks, v_sgu_ln_g=v_sgu_ln_g, v_sgu_ln_b=v_sgu_ln_b, v_sgu_w=v_sgu_w, v_sgu_b=v_sgu_b, v_w_out=v_w_out, v_final_g=v_final_g)
    weights = {n: given[n] for n in TWIN_WEIGHTS}
    shared = {n: given[n] for n in SHARED_INPUTS}
    per_example = {n: given[n] for n in ['x', 'c']}
    grad_fn = _jax.value_and_grad(_loss, argnums=(0, 1))

    def one_microbatch(ex, loss_target):
        ex = dict(ex)
        diff = ex.pop(TWIN_DIFF_INPUT)
        return grad_fn(weights, diff, {**shared, **ex}, loss_target)

    if N_MICROBATCH == 1:
        loss, (grad_w, grad_x) = one_microbatch(per_example, given["loss_target"])
    else:
        def body(carry, xs):
            loss_sum, grad_sum = carry
            l_k, (gw_k, gx_k) = one_microbatch(xs[0], xs[1])
            with _jax.named_scope("update"):
                return (loss_sum + l_k, _jax.tree.map(_jnp.add, grad_sum, gw_k)), gx_k

        init = (_jnp.zeros((), _jnp.float32), _jax.tree.map(_jnp.zeros_like, weights))
        (loss, grad_w), grad_x = _jax.lax.scan(body, init, (per_example, given["loss_target"]))
    with _jax.named_scope("update"):
        delta_w, new_m, new_v = {}, {}, {}
        for n in TWIN_WEIGHTS:
            delta_w[n], new_m[n], new_v[n] = _adamw(weights[n], grad_w[n], given["m_" + n], given["v_" + n])
    return (loss, grad_x, *[grad_w[n] for n in TWIN_WEIGHTS], *[delta_w[n] for n in TWIN_WEIGHTS],
            *[new_m[n] for n in TWIN_WEIGHTS], *[new_v[n] for n in TWIN_WEIGHTS])
```

```python
import functools
import math

import jax
import jax.numpy as jnp
from jax import lax
from jax.experimental import pallas as pl
from jax.experimental.pallas import tpu as pltpu

F32 = jnp.float32
BF16 = jnp.bfloat16

D_MODEL = 2048
D_ATTN = 1024
HEAD_DIM = 64
N_Q_HEADS = 16
Q_PER_KV = 8
D_KV = 128
BLOCK = 128
D_SGU = 1024
SGU_GROUPS = 8
GROUP_DIM = 128
D_IN = 5376
EPS = 1e-6
ATTN_SCALE = 1.0 / math.sqrt(HEAD_DIM)
N_CHIPS = 4
N_DEV = 8
IN_SHARD = D_IN // N_CHIPS
OUT_SHARD = D_MODEL // N_CHIPS
ADA_SHARD = 3 * D_MODEL // N_CHIPS
HALF = D_MODEL // 2

Q0, K0, V0, GA0, U0, VS0, GS0 = 0, 1024, 1152, 1280, 2304, 3328, 4352

ADAM_LR = 0.001
ADAM_B1 = 0.9
ADAM_B2 = 0.999
ADAM_EPS = 1e-08
ADAM_WD = 0.01
ADAM_STEP = 10

VMEM_LIMIT_V7X = 56 * 1024 * 1024

NN = (((1,), (0,)), ((), ()))
NT = (((1,), (1,)), ((), ()))
TN = (((0,), (0,)), ((), ()))
MESH = pl.DeviceIdType.MESH
ANY = pl.BlockSpec(memory_space=pl.ANY)
VMEM = pl.BlockSpec(memory_space=pltpu.VMEM)


def _dot(a, b, dims):
    return lax.dot_general(a, b, dims, preferred_element_type=F32)


def _params(sem=None):
    return pltpu.CompilerParams(dimension_semantics=sem, vmem_limit_bytes=VMEM_LIMIT_V7X)


def _sigmoid(x):
    return jax.nn.sigmoid(x)


def _adamw(w, g, m, v):
    m = ADAM_B1 * m + (1.0 - ADAM_B1) * g
    v = ADAM_B2 * v + (1.0 - ADAM_B2) * (g * g)
    m_hat = m / (1.0 - ADAM_B1 ** ADAM_STEP)
    v_hat = v / (1.0 - ADAM_B2 ** ADAM_STEP)
    delta = -ADAM_LR * (m_hat / (jnp.sqrt(v_hat) + ADAM_EPS) + ADAM_WD * w)
    return delta, m, v


def _place():
    return lax.axis_index("x"), lax.axis_index("y"), lax.axis_index("c")


def _flip(p, bit):
    return 1 - p if bit else p


def _mod_exchange(c, w_ada, b_ada):
    kdim = w_ada.shape[0]
    kc = 256

    def body(c_ref, w_ref, b_ref, mod_ref, cact_ref, cbuf, pbuf, rbuf, send_sems, recv_sems):
        x, y, cc = _place()
        me = 4 * x + 2 * y + cc
        chip = 2 * x + y
        cbuf[me] = c_ref[...]
        gathers = []
        for k in range(1, N_DEV):
            bx, by, bc = (k >> 2) & 1, (k >> 1) & 1, k & 1
            cp = pltpu.make_async_remote_copy(
                src_ref=cbuf.at[me], dst_ref=cbuf.at[me], send_sem=send_sems.at[k - 1], recv_sem=recv_sems.at[k - 1],
                device_id=(_flip(x, bx), _flip(y, by), _flip(cc, bc)), device_id_type=MESH)
            cp.start()
            gathers.append(cp)
        for cp in gathers:
            cp.wait()
        for b in range(N_DEV):
            cv = cbuf[b]
            cact_ref[b:b + 1, :] = cv * _sigmoid(cv)
        acc = jnp.zeros((N_DEV, ADA_SHARD), F32)
        for k0 in range(0, kdim, kc):
            acc = acc + _dot(cact_ref[:, k0:k0 + kc].astype(BF16), w_ref[k0:k0 + kc, :].astype(BF16), NN)
        for b in range(N_DEV):
            pbuf[b] = acc[b:b + 1, :]
        rbuf[chip] = pbuf[me]
        sends = []
        for j, (bx, by) in enumerate(((0, 1), (1, 0), (1, 1))):
            px, py = _flip(x, bx), _flip(y, by)
            cp = pltpu.make_async_remote_copy(
                src_ref=pbuf.at[4 * px + 2 * py + cc], dst_ref=rbuf.at[chip],
                send_sem=send_sems.at[N_DEV - 1 + j], recv_sem=recv_sems.at[N_DEV - 1 + j],
                device_id=(px, py, cc), device_id_type=MESH)
            cp.start()
            sends.append(cp)
        for cp in sends:
            cp.wait()
        for k in range(N_CHIPS):
            cols = slice(ADA_SHARD * k, ADA_SHARD * (k + 1))
            mod_ref[:, cols] = rbuf[k] + b_ref[:, cols]

    return pl.pallas_call(
        body, name="mod_exchange",
        out_shape=(jax.ShapeDtypeStruct((1, 3 * D_MODEL), F32), jax.ShapeDtypeStruct((N_DEV, D_MODEL), F32)),
        in_specs=[VMEM, VMEM, VMEM], out_specs=(VMEM, VMEM),
        scratch_shapes=[pltpu.VMEM((N_DEV, 1, D_MODEL), F32), pltpu.VMEM((N_DEV, 1, ADA_SHARD), F32),
                        pltpu.VMEM((N_CHIPS, 1, ADA_SHARD), F32),
                        pltpu.SemaphoreType.DMA((N_DEV + 2,)), pltpu.SemaphoreType.DMA((N_DEV + 2,))],
        compiler_params=_params(),
    )(c, w_ada, b_ada)


def _weight_gather(win_t, wout):
    hi, ho = IN_SHARD // 2, OUT_SHARD // 2

    def body(wi_ref, wo_ref, fi_ref, fo_ref, loc_sems, send_sems, recv_sems, fsend_sems, frecv_sems):
        x, y, cc = _place()
        chip = 2 * x + y
        others = [(_flip(x, bx), _flip(y, by)) for bx, by in ((0, 1), (1, 0), (1, 1))]

        def rows(ref, shard, half, k, c_half):
            return ref.at[pl.ds(shard * k + half * c_half, half), :]

        own = [pltpu.make_async_copy(wi_ref, fi_ref.at[pl.ds(IN_SHARD * chip, IN_SHARD), :], loc_sems.at[0]),
               pltpu.make_async_copy(wo_ref, fo_ref.at[pl.ds(OUT_SHARD * chip, OUT_SHARD), :], loc_sems.at[1])]
        for cp in own:
            cp.start()
        sends = []
        for j, (px, py) in enumerate(others):
            for a, (src, dst, shard, half) in enumerate(((wi_ref, fi_ref, IN_SHARD, hi), (wo_ref, fo_ref, OUT_SHARD, ho))):
                cp = pltpu.make_async_remote_copy(
                    src_ref=src.at[pl.ds(half * cc, half), :], dst_ref=rows(dst, shard, half, chip, cc),
                    send_sem=send_sems.at[2 * j + a], recv_sem=recv_sems.at[2 * j + a],
                    device_id=(px, py, cc), device_id_type=MESH)
                cp.start()
                sends.append(cp)
        forwards = []
        for j, (px, py) in enumerate(others):
            pchip = 2 * px + py
            for a, (dst, shard, half) in enumerate(((fi_ref, IN_SHARD, hi), (fo_ref, OUT_SHARD, ho))):
                landed = rows(dst, shard, half, pchip, cc)
                pltpu.make_async_remote_copy(
                    src_ref=landed, dst_ref=landed, send_sem=send_sems.at[2 * j + a], recv_sem=recv_sems.at[2 * j + a],
                    device_id=(px, py, cc), device_id_type=MESH).wait_recv()
                cp = pltpu.make_async_remote_copy(
                    src_ref=landed, dst_ref=landed, send_sem=fsend_sems.at[2 * j + a], recv_sem=frecv_sems.at[2 * j + a],
                    device_id=(x, y, 1 - cc), device_id_type=MESH)
                cp.start()
                forwards.append(cp)
        for j, (px, py) in enumerate(others):
            pchip = 2 * px + py
            for a, (dst, shard, half) in enumerate(((fi_ref, IN_SHARD, hi), (fo_ref, OUT_SHARD, ho))):
                theirs = rows(dst, shard, half, pchip, 1 - cc)
                pltpu.make_async_remote_copy(
                    src_ref=theirs, dst_ref=theirs, send_sem=fsend_sems.at[2 * j + a], recv_sem=frecv_sems.at[2 * j + a],
                    device_id=(x, y, 1 - cc), device_id_type=MESH).wait_recv()
        for cp in sends + forwards:
            cp.wait_send()
        for cp in own:
            cp.wait()

    return pl.pallas_call(
        body, name="weight_gather",
        out_shape=(jax.ShapeDtypeStruct((D_IN, D_MODEL), BF16), jax.ShapeDtypeStruct((D_MODEL, D_MODEL), BF16)),
        in_specs=[ANY, ANY], out_specs=(ANY, ANY),
        scratch_shapes=[pltpu.SemaphoreType.DMA((2,))] + [pltpu.SemaphoreType.DMA((6,))] * 4,
        compiler_params=_params(),
    )(win_t, wout)


def _grad_to_sibling(gw, go):
    def body(gw_ref, go_ref, rw_ref, ro_ref, send_sems, recv_sems):
        x, y, cc = _place()
        cps = []
        for a, (src, dst) in enumerate(((gw_ref, rw_ref), (go_ref, ro_ref))):
            cp = pltpu.make_async_remote_copy(src_ref=src, dst_ref=dst, send_sem=send_sems.at[a], recv_sem=recv_sems.at[a],
                                              device_id=(x, y, 1 - cc), device_id_type=MESH)
            cp.start()
            cps.append(cp)
        for cp in cps:
            cp.wait()

    return pl.pallas_call(
        body, name="grad_to_sibling",
        out_shape=(jax.ShapeDtypeStruct(gw.shape, gw.dtype), jax.ShapeDtypeStruct(go.shape, go.dtype)),
        in_specs=[ANY, ANY], out_specs=(ANY, ANY),
        scratch_shapes=[pltpu.SemaphoreType.DMA((2,)), pltpu.SemaphoreType.DMA((2,))],
        compiler_params=_params(),
    )(gw, go)


def _grad_to_chips(pw, po):
    def body(pw_ref, po_ref, sw_ref, so_ref, loc_sems, send_sems, recv_sems):
        x, y, cc = _place()
        chip = 2 * x + y
        others = [(_flip(x, bx), _flip(y, by)) for bx, by in ((0, 1), (1, 0), (1, 1))]
        arrays = ((pw_ref, sw_ref, IN_SHARD), (po_ref, so_ref, OUT_SHARD))
        own = [pltpu.make_async_copy(src.at[pl.ds(shard * chip, shard), :], dst.at[chip], loc_sems.at[a])
               for a, (src, dst, shard) in enumerate(arrays)]
        for cp in own:
            cp.start()
        cps = []
        for j, (px, py) in enumerate(others):
            for a, (src, dst, shard) in enumerate(arrays):
                cp = pltpu.make_async_remote_copy(
                    src_ref=src.at[pl.ds(shard * (2 * px + py), shard), :], dst_ref=dst.at[chip],
                    send_sem=send_sems.at[2 * j + a], recv_sem=recv_sems.at[2 * j + a],
                    device_id=(px, py, cc), device_id_type=MESH)
                cp.start()
                cps.append(cp)
        for cp in cps:
            cp.wait()
        for cp in own:
            cp.wait()

    return pl.pallas_call(
        body, name="grad_to_chips",
        out_shape=(jax.ShapeDtypeStruct((N_CHIPS, IN_SHARD, HALF), BF16), jax.ShapeDtypeStruct((N_CHIPS, OUT_SHARD, HALF), BF16)),
        in_specs=[ANY, ANY], out_specs=(ANY, ANY),
        scratch_shapes=[pltpu.SemaphoreType.DMA((2,)), pltpu.SemaphoreType.DMA((6,)), pltpu.SemaphoreType.DMA((6,))],
        compiler_params=_params(),
    )(pw, po)


def _grad_halves_swap(fw, fo):
    def body(fw_in, fo_in, fw_ref, fo_ref, send_sems, recv_sems):
        x, y, cc = _place()
        cps = []
        for a, ref in enumerate((fw_ref, fo_ref)):
            mine = ref.at[:, pl.ds(pl.multiple_of(HALF * cc, 128), HALF)]
            cp = pltpu.make_async_remote_copy(src_ref=mine, dst_ref=mine, send_sem=send_sems.at[a], recv_sem=recv_sems.at[a],
                                              device_id=(x, y, 1 - cc), device_id_type=MESH)
            cp.start()
            cps.append(cp)
        for cp in cps:
            cp.wait()

    return pl.pallas_call(
        body, name="grad_halves_swap",
        out_shape=(jax.ShapeDtypeStruct(fw.shape, fw.dtype), jax.ShapeDtypeStruct(fo.shape, fo.dtype)),
        in_specs=[ANY, ANY], out_specs=(ANY, ANY), input_output_aliases={0: 0, 1: 1},
        scratch_shapes=[pltpu.SemaphoreType.DMA((2,)), pltpu.SemaphoreType.DMA((2,))],
        compiler_params=_params(),
    )(fw, fo)


ROW_NORM_G, ROW_DMOD, ROW_FINAL_G, ROW_LN, ROW_SINKS, ROW_SGU_B, MISC_ROWS = 0, 1, 4, 5, 6, 8, 16


def _small_grads_step(parts, small):
    names = ["norm_g", "b_ada", "attn_sinks", "sgu_ln_g", "sgu_ln_b", "sgu_w", "sgu_b", "final_g"]
    part_names = ["dng", "dshift", "dscale", "dgate", "dfg", "dlg", "dlb", "dsinks", "dsgu_b", "dsgu_w"]
    n_parts = len(part_names)

    def body(*refs):
        p = dict(zip(part_names, refs[:n_parts]))
        wmv = {n: refs[n_parts + 3 * i: n_parts + 3 * i + 3] for i, n in enumerate(names)}
        outs = refs[n_parts + 3 * len(names):]
        res = {n: outs[4 * i: 4 * i + 4] for i, n in enumerate(names)}
        dmod_ref = outs[4 * len(names)]
        misc, wbuf, send_sems, recv_sems = outs[4 * len(names) + 1:]
        x, y, cc = _place()
        me = 4 * x + 2 * y + cc

        misc[me] = jnp.zeros((MISC_ROWS, D_MODEL), F32)
        misc[me, ROW_NORM_G:ROW_NORM_G + 1, :] = p["dng"][...]
        misc[me, ROW_DMOD:ROW_DMOD + 1, :] = p["dshift"][...]
        misc[me, ROW_DMOD + 1:ROW_DMOD + 2, :] = p["dscale"][...]
        misc[me, ROW_DMOD + 2:ROW_DMOD + 3, :] = p["dgate"][...]
        misc[me, ROW_FINAL_G:ROW_FINAL_G + 1, :] = p["dfg"][...]
        misc[me, ROW_LN:ROW_LN + 1, 0:D_SGU] = p["dlg"][...]
        misc[me, ROW_LN:ROW_LN + 1, D_SGU:2 * D_SGU] = p["dlb"][...]
        misc[me, ROW_SINKS:ROW_SINKS + 1, 0:N_Q_HEADS] = p["dsinks"][...]
        misc[me, ROW_SGU_B:ROW_SGU_B + SGU_GROUPS, 0:BLOCK] = p["dsgu_b"][...]
        for g in range(SGU_GROUPS):
            wbuf[me, BLOCK * g:BLOCK * (g + 1), :] = p["dsgu_w"][g]

        cps = []
        for k in range(1, N_DEV):
            bx, by, bc = (k >> 2) & 1, (k >> 1) & 1, k & 1
            peer = (_flip(x, bx), _flip(y, by), _flip(cc, bc))
            for a, buf in enumerate((misc, wbuf)):
                cp = pltpu.make_async_remote_copy(
                    src_ref=buf.at[me], dst_ref=buf.at[me], send_sem=send_sems.at[2 * (k - 1) + a],
                    recv_sem=recv_sems.at[2 * (k - 1) + a], device_id=peer, device_id_type=MESH)
                cp.start()
                cps.append(cp)
        for cp in cps:
            cp.wait()

        tot = misc[0]
        for b in range(1, N_DEV):
            tot = tot + misc[b]
        for b in range(N_DEV):
            for t in range(3):
                dmod_ref[b:b + 1, D_MODEL * t:D_MODEL * (t + 1)] = misc[b, ROW_DMOD + t:ROW_DMOD + t + 1, :]

        def update(name, g, pick=lambda r: r[...], put=None):
            w_ref, m_ref, v_ref = wmv[name]
            delta, m2, v2 = _adamw(pick(w_ref), g, pick(m_ref), pick(v_ref))
            for ref, val in zip(res[name], (g, delta, m2, v2)):
                if put is None:
                    ref[...] = val
                else:
                    put(ref, val)

        update("norm_g", tot[ROW_NORM_G:ROW_NORM_G + 1, :])
        update("final_g", tot[ROW_FINAL_G:ROW_FINAL_G + 1, :])
        for t in range(3):
            cols = slice(D_MODEL * t, D_MODEL * (t + 1))

            def put_cols(ref, val, cols=cols):
                ref[:, cols] = val
            update("b_ada", tot[ROW_DMOD + t:ROW_DMOD + t + 1, :], pick=lambda r, cols=cols: r[:, cols], put=put_cols)
        update("sgu_ln_g", tot[ROW_LN:ROW_LN + 1, 0:D_SGU])
        update("sgu_ln_b", tot[ROW_LN:ROW_LN + 1, D_SGU:2 * D_SGU])
        update("attn_sinks", tot[ROW_SINKS:ROW_SINKS + 1, 0:N_Q_HEADS])
        update("sgu_b", tot[ROW_SGU_B:ROW_SGU_B + SGU_GROUPS, 0:BLOCK])
        for g in range(SGU_GROUPS):
            rows = slice(BLOCK * g, BLOCK * (g + 1))
            gw = wbuf[0, rows, :]
            for b in range(1, N_DEV):
                gw = gw + wbuf[b, rows, :]

            def put_group(ref, val, g=g):
                ref[g] = val
            update("sgu_w", gw, pick=lambda r, g=g: r[g], put=put_group)

    shapes = {"norm_g": (1, D_MODEL), "b_ada": (1, 3 * D_MODEL), "attn_sinks": (1, N_Q_HEADS), "sgu_ln_g": (1, D_SGU),
              "sgu_ln_b": (1, D_SGU), "sgu_w": (SGU_GROUPS, BLOCK, BLOCK), "sgu_b": (SGU_GROUPS, BLOCK), "final_g": (1, D_MODEL)}
    operands = [parts[n] for n in part_names]
    for n in names:
        operands += [a.reshape(shapes[n]) for a in small[n]]
    out_shape = []
    for n in names:
        out_shape += [jax.ShapeDtypeStruct(shapes[n], F32)] * 4
    out_shape.append(jax.ShapeDtypeStruct((N_DEV, 3 * D_MODEL), F32))
    outs = pl.pallas_call(
        body, name="small_grads_step", out_shape=tuple(out_shape),
        in_specs=[VMEM] * len(operands), out_specs=tuple([VMEM] * len(out_shape)),
        scratch_shapes=[pltpu.VMEM((N_DEV, MISC_ROWS, D_MODEL), F32), pltpu.VMEM((N_DEV, SGU_GROUPS * BLOCK, BLOCK), F32),
                        pltpu.SemaphoreType.DMA((2 * (N_DEV - 1),)), pltpu.SemaphoreType.DMA((2 * (N_DEV - 1),))],
        compiler_params=_params(),
    )(*operands)
    res = {n: outs[4 * i: 4 * i + 4] for i, n in enumerate(names)}
    return res, outs[4 * len(names)]


def _transpose_cast(w, tile=256):
    r, c = w.shape

    def body(w_ref, o_ref):
        o_ref[...] = w_ref[...].T.astype(BF16)

    return pl.pallas_call(
        body, name="transpose_cast", grid=(r // tile,), out_shape=jax.ShapeDtypeStruct((c, r), BF16),
        in_specs=[pl.BlockSpec((tile, c), lambda i: (i, 0))], out_specs=pl.BlockSpec((c, tile), lambda i: (0, i)),
        compiler_params=_params(("parallel",)),
    )(w)


def _cast_bf16(w, tile=256):
    r, c = w.shape

    def body(w_ref, o_ref):
        o_ref[...] = w_ref[...].astype(BF16)

    return pl.pallas_call(
        body, name="cast_bf16", grid=(r // tile,), out_shape=jax.ShapeDtypeStruct((r, c), BF16),
        in_specs=[pl.BlockSpec((tile, c), lambda i: (i, 0))], out_specs=pl.BlockSpec((tile, c), lambda i: (i, 0)),
        compiler_params=_params(("parallel",)),
    )(w)


def _norm_modulate(x, norm_g, mod):
    s, d = x.shape
    tm = min(256, s)

    def body(x_ref, g_ref, mod_ref, h_ref):
        xv = x_ref[...]
        r = lax.rsqrt(jnp.mean(xv * xv, axis=-1, keepdims=True) + EPS)
        h = (xv * r * g_ref[...]) * (1.0 + mod_ref[:, d:2 * d]) + mod_ref[:, 0:d]
        h_ref[...] = h.astype(BF16)

    return pl.pallas_call(
        body, name="norm_modulate", grid=(s // tm,), out_shape=jax.ShapeDtypeStruct((s, d), BF16),
        in_specs=[pl.BlockSpec((tm, d), lambda i: (i, 0)), pl.BlockSpec((1, d), lambda i: (0, 0)),
                  pl.BlockSpec((1, 3 * d), lambda i: (0, 0))],
        out_specs=pl.BlockSpec((tm, d), lambda i: (i, 0)),
        compiler_params=_params(("parallel",)),
    )(x, norm_g, mod)


def _matmul_nt(a, bt, tm, tn, out_dtype, name):
    m, k = a.shape
    n = bt.shape[0]
    tm, tn = min(tm, m), min(tn, n)

    def body(a_ref, b_ref, o_ref):
        o_ref[...] = _dot(a_ref[...], b_ref[...], NT).astype(out_dtype)

    return pl.pallas_call(
        body, name=name, grid=(m // tm, n // tn), out_shape=jax.ShapeDtypeStruct((m, n), out_dtype),
        in_specs=[pl.BlockSpec((tm, k), lambda i, j: (i, 0)), pl.BlockSpec((tn, k), lambda i, j: (j, 0))],
        out_specs=pl.BlockSpec((tm, tn), lambda i, j: (i, j)),
        compiler_params=_params(("parallel", "parallel")),
    )(a, bt)


def _grad_matmul(a, b, half, tm, name, add=None):
    t, m = a.shape
    tm = min(tm, m)

    def body(half_ref, a_ref, b_ref, *rest):
        if add is None:
            (o_ref,) = rest
            o_ref[...] = _dot(a_ref[...], b_ref[...], TN).astype(BF16)
        else:
            add_ref, o_ref = rest
            o_ref[...] = (_dot(a_ref[...], b_ref[...], TN) + add_ref[...].astype(F32)).astype(BF16)

    in_specs = [pl.BlockSpec((t, tm), lambda i, h: (0, i)), pl.BlockSpec((t, HALF), lambda i, h: (0, h[0]))]
    operands = [a, b]
    if add is not None:
        in_specs.append(pl.BlockSpec((tm, HALF), lambda i, h: (i, 0)))
        operands.append(add)
    return pl.pallas_call(
        body, name=name,
        grid_spec=pltpu.PrefetchScalarGridSpec(num_scalar_prefetch=1, grid=(m // tm,), in_specs=in_specs,
                                               out_specs=pl.BlockSpec((tm, HALF), lambda i, h: (i, 0))),
        out_shape=jax.ShapeDtypeStruct((m, HALF), BF16),
        compiler_params=_params(("parallel",)),
    )(half, *operands)


def _mixer_masks(n):
    row = lax.broadcasted_iota(jnp.int32, (BLOCK, 2 * BLOCK), 0)
    col = lax.broadcasted_iota(jnp.int32, (BLOCK, 2 * BLOCK), 1)
    valid = (col > row) & (col <= row + BLOCK) & ((col >= BLOCK) | (n > 0))
    r2 = lax.broadcasted_iota(jnp.int32, (BLOCK, BLOCK), 0)
    c2 = lax.broadcasted_iota(jnp.int32, (BLOCK, BLOCK), 1)
    return valid, r2 >= c2


def _layer_norm_parts(vs):
    mu = jnp.mean(vs, axis=-1, keepdims=True)
    xc = vs - mu
    rstd = lax.rsqrt(jnp.mean(xc * xc, axis=-1, keepdims=True) + EPS)
    return xc * rstd, rstd


def _attn_probs(qh, kh, valid, sk):
    s = _dot(qh, kh, NT) * ATTN_SCALE
    s = jnp.where(valid, s, -jnp.inf)
    m = jnp.maximum(jnp.max(s, axis=-1, keepdims=True), sk)
    e = jnp.exp(s - m)
    es = jnp.exp(sk - m)
    den = jnp.sum(e, axis=-1, keepdims=True) + es
    return e / den, es / den


def _mixer_fwd(z, sinks, ln_g, ln_b, sgu_w, sgu_bt):
    s = z.shape[0]
    nb = s // BLOCK

    def body(z_ref, kvp_ref, sink_ref, lg_ref, lb_ref, w_ref, bt_ref, cat_ref, o_ref):
        n = pl.program_id(0)
        valid, tril = _mixer_masks(n)
        k_ext = jnp.concatenate([kvp_ref[:, 0:D_KV], z_ref[:, K0:K0 + D_KV]], axis=0).astype(BF16)
        v_ext = jnp.concatenate([kvp_ref[:, D_KV:2 * D_KV], z_ref[:, V0:V0 + D_KV]], axis=0).astype(BF16)
        for h in range(N_Q_HEADS):
            hk = h // Q_PER_KV
            kv = slice(HEAD_DIM * hk, HEAD_DIM * (hk + 1))
            hs = slice(HEAD_DIM * h, HEAD_DIM * (h + 1))
            p, _ = _attn_probs(z_ref[:, hs].astype(BF16), k_ext[:, kv], valid, sink_ref[:, h:h + 1])
            o = _dot(p.astype(BF16), v_ext[:, kv], NN)
            o_ref[:, hs] = o
            ga = z_ref[:, GA0 + HEAD_DIM * h:GA0 + HEAD_DIM * (h + 1)]
            cat_ref[:, hs] = (o * (ga * _sigmoid(ga))).astype(BF16)
        xhat, _ = _layer_norm_parts(z_ref[:, VS0:VS0 + D_SGU])
        vn = (xhat * lg_ref[...] + lb_ref[...]).astype(BF16)
        for g in range(SGU_GROUPS):
            gs_ = slice(GROUP_DIM * g, GROUP_DIM * (g + 1))
            w = jnp.where(tril, w_ref[g], 0.0).astype(BF16)
            mixed = _dot(w, vn[:, gs_], NN) + bt_ref[:, g:g + 1]
            u = z_ref[:, U0 + GROUP_DIM * g:U0 + GROUP_DIM * (g + 1)]
            gt = z_ref[:, GS0 + GROUP_DIM * g:GS0 + GROUP_DIM * (g + 1)]
            cat_ref[:, D_ATTN + GROUP_DIM * g:D_ATTN + GROUP_DIM * (g + 1)] = ((u * mixed) * (gt * _sigmoid(gt))).astype(BF16)

    const2 = lambda n: (0, 0)
    return pl.pallas_call(
        body, name="mixer_fwd", grid=(nb,),
        out_shape=(jax.ShapeDtypeStruct((s, D_MODEL), BF16), jax.ShapeDtypeStruct((s, D_ATTN), F32)),
        in_specs=[pl.BlockSpec((BLOCK, D_IN), lambda n: (n, 0)),
                  pl.BlockSpec((BLOCK, 2 * D_KV), lambda n: (jnp.maximum(n - 1, 0), K0 // (2 * D_KV))),
                  pl.BlockSpec((1, N_Q_HEADS), const2), pl.BlockSpec((1, D_SGU), const2), pl.BlockSpec((1, D_SGU), const2),
                  pl.BlockSpec((SGU_GROUPS, BLOCK, BLOCK), lambda n: (0, 0, 0)), pl.BlockSpec((BLOCK, SGU_GROUPS), const2)],
        out_specs=(pl.BlockSpec((BLOCK, D_MODEL), lambda n: (n, 0)), pl.BlockSpec((BLOCK, D_ATTN), lambda n: (n, 0))),
        compiler_params=_params(("parallel",)),
    )(z, z, sinks, ln_g, ln_b, sgu_w, sgu_bt)


def _out_proj_loss(cat, wout, x, target, mod, final_g):
    s, d = x.shape
    tm = min(256, s)

    def body(cat_ref, w_ref, x_ref, t_ref, mod_ref, fg_ref, dx1_ref, dy_ref, dfg_ref, dgate_ref, loss_ref):
        i = pl.program_id(0)
        gate = mod_ref[:, 2 * d:3 * d]
        fg = fg_ref[...]
        yv = _dot(cat_ref[...], w_ref[...], NN)
        x1 = x_ref[...] + gate * yv
        r = lax.rsqrt(jnp.mean(x1 * x1, axis=-1, keepdims=True) + EPS)
        xr = x1 * r
        err = xr * fg - t_ref[...]
        dout = err * (1.0 / d)
        gy = dout * fg
        dx1 = r * gy - xr * (r * r * jnp.mean(gy * x1, axis=-1, keepdims=True))
        dx1_ref[...] = dx1
        dy_ref[...] = (gate * dx1).astype(BF16)

        @pl.when(i == 0)
        def _():
            dfg_ref[...] = jnp.zeros_like(dfg_ref)
            dgate_ref[...] = jnp.zeros_like(dgate_ref)
            loss_ref[...] = jnp.zeros_like(loss_ref)

        dfg_ref[...] += jnp.sum(dout * xr, axis=0, keepdims=True)
        dgate_ref[...] += jnp.sum(dx1 * yv, axis=0, keepdims=True)
        loss_ref[...] += 0.5 * jnp.sum(jnp.mean(err * err, axis=-1, keepdims=True), axis=0, keepdims=True)

    row = lambda i: (i, 0)
    const = lambda i: (0, 0)
    return pl.pallas_call(
        body, name="out_proj_loss", grid=(s // tm,),
        out_shape=(jax.ShapeDtypeStruct((s, d), F32), jax.ShapeDtypeStruct((s, d), BF16), jax.ShapeDtypeStruct((1, d), F32),
                   jax.ShapeDtypeStruct((1, d), F32), jax.ShapeDtypeStruct((1, 1), F32)),
        in_specs=[pl.BlockSpec((tm, d), row), pl.BlockSpec((d, d), const), pl.BlockSpec((tm, d), row), pl.BlockSpec((tm, d), row),
                  pl.BlockSpec((1, 3 * d), const), pl.BlockSpec((1, d), const)],
        out_specs=(pl.BlockSpec((tm, d), row), pl.BlockSpec((tm, d), row), pl.BlockSpec((1, d), const), pl.BlockSpec((1, d), const),
                   pl.BlockSpec((1, 1), const)),
        compiler_params=_params(("arbitrary",)),
    )(cat, wout, x, target, mod, final_g)


def _mixer_bwd(z, dcat, o, sinks, ln_g, ln_b, sgu_w, sgu_bt):
    s = z.shape[0]
    nb = s // BLOCK

    def body(z_ref, kvp_ref, dc_ref, o_ref, sink_ref, lg_ref, lb_ref, w_ref, bt_ref,
             dz_ref, dsink_ref, dlg_ref, dlb_ref, dw_ref, db_ref, pend, dkv, carry, dsink_acc, dbt_acc):
        n = pl.program_id(0)

        @pl.when(n == 0)
        def _():
            pend[...] = jnp.zeros_like(pend)
            carry[...] = jnp.zeros_like(carry)
            dsink_acc[...] = jnp.zeros_like(dsink_acc)
            dbt_acc[...] = jnp.zeros_like(dbt_acc)
            dlg_ref[...] = jnp.zeros_like(dlg_ref)
            dlb_ref[...] = jnp.zeros_like(dlb_ref)
            dw_ref[...] = jnp.zeros_like(dw_ref)

        dz_ref[:, 0:K0] = pend[:, 0:K0]
        dz_ref[:, GA0:D_IN] = pend[:, GA0:D_IN]

        @pl.when(n == nb)
        def _():
            dkv[...] = jnp.zeros_like(dkv)

        @pl.when(n < nb)
        def _():
            valid, tril = _mixer_masks(n)
            k_ext = jnp.concatenate([kvp_ref[:, 0:D_KV], z_ref[:, K0:K0 + D_KV]], axis=0).astype(BF16)
            v_ext = jnp.concatenate([kvp_ref[:, D_KV:2 * D_KV], z_ref[:, V0:V0 + D_KV]], axis=0).astype(BF16)
            for hk in range(2):
                kv = slice(HEAD_DIM * hk, HEAD_DIM * (hk + 1))
                kh, vh = k_ext[:, kv], v_ext[:, kv]
                dk = jnp.zeros((2 * BLOCK, HEAD_DIM), F32)
                dv = jnp.zeros((2 * BLOCK, HEAD_DIM), F32)
                for gq in range(Q_PER_KV):
                    h = hk * Q_PER_KV + gq
                    hs = slice(HEAD_DIM * h, HEAD_DIM * (h + 1))
                    gs_ = slice(GA0 + HEAD_DIM * h, GA0 + HEAD_DIM * (h + 1))
                    qh = z_ref[:, hs].astype(BF16)
                    p, p_sink = _attn_probs(qh, kh, valid, sink_ref[:, h:h + 1])
                    ga = z_ref[:, gs_]
                    sg = _sigmoid(ga)
                    datt = dc_ref[:, hs]
                    do = (datt * (ga * sg)).astype(BF16)
                    pend[:, gs_] = (datt * o_ref[:, hs] * (sg * (1.0 + ga * (1.0 - sg)))).astype(BF16)
                    dp = _dot(do, vh, NT)
                    delta = jnp.sum(p * dp, axis=-1, keepdims=True)
                    ds = (p * (dp - delta) * ATTN_SCALE).astype(BF16)
                    dsink_acc[:, h:h + 1] += jnp.sum(-p_sink * delta, axis=0, keepdims=True)
                    pend[:, hs] = _dot(ds, kh, NN).astype(BF16)
                    dk = dk + _dot(ds, qh, TN)
                    dv = dv + _dot(p.astype(BF16), do, TN)
                dkv[:, kv] = dk
                dkv[:, D_KV + HEAD_DIM * hk:D_KV + HEAD_DIM * (hk + 1)] = dv

            xhat, rstd = _layer_norm_parts(z_ref[:, VS0:VS0 + D_SGU])
            lg = lg_ref[...]
            vn = (xhat * lg + lb_ref[...]).astype(BF16)
            dvn_parts = []
            for g in range(SGU_GROUPS):
                gs_ = slice(GROUP_DIM * g, GROUP_DIM * (g + 1))
                w = jnp.where(tril, w_ref[g], 0.0).astype(BF16)
                mixed = _dot(w, vn[:, gs_], NN) + bt_ref[:, g:g + 1]
                u = z_ref[:, U0 + GROUP_DIM * g:U0 + GROUP_DIM * (g + 1)]
                gt = z_ref[:, GS0 + GROUP_DIM * g:GS0 + GROUP_DIM * (g + 1)]
                sg = _sigmoid(gt)
                dsgu = dc_ref[:, D_ATTN + GROUP_DIM * g:D_ATTN + GROUP_DIM * (g + 1)]
                dgated = dsgu * (gt * sg)
                pend[:, U0 + GROUP_DIM * g:U0 + GROUP_DIM * (g + 1)] = (dgated * mixed).astype(BF16)
                pend[:, GS0 + GROUP_DIM * g:GS0 + GROUP_DIM * (g + 1)] = (
                    dsgu * (u * mixed) * (sg * (1.0 + gt * (1.0 - sg)))).astype(BF16)
                dmixed = dgated * u
                dmb = dmixed.astype(BF16)
                dw_ref[g] += jnp.where(tril, _dot(dmb, vn[:, gs_], NT), 0.0)
                dbt_acc[:, g:g + 1] += jnp.sum(dmixed, axis=-1, keepdims=True)
                dvn_parts.append(_dot(w, dmb, TN))
            dvn = jnp.concatenate(dvn_parts, axis=1)
            dlg_ref[...] += jnp.sum(dvn * xhat, axis=0, keepdims=True)
            dlb_ref[...] += jnp.sum(dvn, axis=0, keepdims=True)
            dxh = dvn * lg
            pend[:, VS0:VS0 + D_SGU] = (rstd * (dxh - jnp.mean(dxh, axis=-1, keepdims=True)
                                                - xhat * jnp.mean(dxh * xhat, axis=-1, keepdims=True))).astype(BF16)

        dz_ref[:, K0:GA0] = (carry[...] + dkv[0:BLOCK, :]).astype(BF16)
        carry[...] = dkv[BLOCK:2 * BLOCK, :]

        @pl.when(n == nb)
        def _():
            dsink_ref[...] = dsink_acc[:, 0:N_Q_HEADS]
            db_ref[...] = dbt_acc[...].T[0:SGU_GROUPS, :]

    cur = lambda n: (jnp.minimum(n, nb - 1), 0)
    const2 = lambda n: (0, 0)
    return pl.pallas_call(
        body, name="mixer_bwd", grid=(nb + 1,),
        out_shape=(jax.ShapeDtypeStruct((s, D_IN), BF16), jax.ShapeDtypeStruct((1, N_Q_HEADS), F32),
                   jax.ShapeDtypeStruct((1, D_SGU), F32), jax.ShapeDtypeStruct((1, D_SGU), F32),
                   jax.ShapeDtypeStruct((SGU_GROUPS, BLOCK, BLOCK), F32), jax.ShapeDtypeStruct((SGU_GROUPS, BLOCK), F32)),
        in_specs=[pl.BlockSpec((BLOCK, D_IN), cur),
                  pl.BlockSpec((BLOCK, 2 * D_KV), lambda n: (jnp.maximum(jnp.minimum(n, nb - 1) - 1, 0), K0 // (2 * D_KV))),
                  pl.BlockSpec((BLOCK, D_MODEL), cur), pl.BlockSpec((BLOCK, D_ATTN), cur),
                  pl.BlockSpec((1, N_Q_HEADS), const2), pl.BlockSpec((1, D_SGU), const2), pl.BlockSpec((1, D_SGU), const2),
                  pl.BlockSpec((SGU_GROUPS, BLOCK, BLOCK), lambda n: (0, 0, 0)), pl.BlockSpec((BLOCK, SGU_GROUPS), const2)],
        out_specs=(pl.BlockSpec((BLOCK, D_IN), lambda n: (jnp.maximum(n - 1, 0), 0)),
                   pl.BlockSpec((1, N_Q_HEADS), const2), pl.BlockSpec((1, D_SGU), const2), pl.BlockSpec((1, D_SGU), const2),
                   pl.BlockSpec((SGU_GROUPS, BLOCK, BLOCK), lambda n: (0, 0, 0)), pl.BlockSpec((SGU_GROUPS, BLOCK), const2)),
        scratch_shapes=[pltpu.VMEM((BLOCK, D_IN), BF16), pltpu.VMEM((2 * BLOCK, 2 * D_KV), F32), pltpu.VMEM((BLOCK, 2 * D_KV), F32),
                        pltpu.VMEM((1, BLOCK), F32), pltpu.VMEM((BLOCK, BLOCK), F32)],
        compiler_params=_params(("arbitrary",)),
    )(z, z, dcat, o, sinks, ln_g, ln_b, sgu_w, sgu_bt)


def _in_proj_bwd(dz, win_t, x, dx1, norm_g, mod):
    s, d = x.shape
    kdim = dz.shape[1]
    tm, tk = min(512, s), 768
    nk = kdim // tk

    def body(dz_ref, w_ref, x_ref, dx1_ref, g_ref, mod_ref, gx_ref, dshift_ref, dscale_ref, dng_ref, acc):
        i, k = pl.program_id(0), pl.program_id(1)

        @pl.when(k == 0)
        def _():
            acc[...] = jnp.zeros_like(acc)

        acc[...] += _dot(dz_ref[...], w_ref[...], NN)

        @pl.when((i == 0) & (k == nk - 1))
        def _():
            dshift_ref[...] = jnp.zeros_like(dshift_ref)
            dscale_ref[...] = jnp.zeros_like(dscale_ref)
            dng_ref[...] = jnp.zeros_like(dng_ref)

        @pl.when(k == nk - 1)
        def _():
            dh = acc[...]
            xv = x_ref[...]
            ng = g_ref[...]
            r = lax.rsqrt(jnp.mean(xv * xv, axis=-1, keepdims=True) + EPS)
            xn = xv * r
            dshift_ref[...] += jnp.sum(dh, axis=0, keepdims=True)
            dscale_ref[...] += jnp.sum(dh * (xn * ng), axis=0, keepdims=True)
            dhs = dh * (1.0 + mod_ref[:, d:2 * d])
            dng_ref[...] += jnp.sum(dhs * xn, axis=0, keepdims=True)
            dxn = dhs * ng
            gx_ref[...] = dx1_ref[...] + r * dxn - xn * (r * r * jnp.mean(dxn * xv, axis=-1, keepdims=True))

    row = lambda i, k: (i, 0)
    const = lambda i, k: (0, 0)
    return pl.pallas_call(
        body, name="in_proj_bwd", grid=(s // tm, nk),
        out_shape=(jax.ShapeDtypeStruct((s, d), F32),) + (jax.ShapeDtypeStruct((1, d), F32),) * 3,
        in_specs=[pl.BlockSpec((tm, tk), lambda i, k: (i, k)), pl.BlockSpec((tk, d), lambda i, k: (k, 0)),
                  pl.BlockSpec((tm, d), row), pl.BlockSpec((tm, d), row), pl.BlockSpec((1, d), const), pl.BlockSpec((1, 3 * d), const)],
        out_specs=(pl.BlockSpec((tm, d), row),) + (pl.BlockSpec((1, d), const),) * 3,
        scratch_shapes=[pltpu.VMEM((tm, d), F32)],
        compiler_params=_params(("arbitrary", "arbitrary")),
    )(dz, win_t, x, dx1, norm_g, mod)


def _sum_slots(slots, half, rows_tile, full_cols, name):
    _, r, hcols = slots.shape

    def body(half_ref, s_ref, o_ref):
        acc = s_ref[0].astype(F32)
        for k in range(1, N_CHIPS):
            acc = acc + s_ref[k].astype(F32)
        o_ref[...] = acc

    return pl.pallas_call(
        body, name=name,
        grid_spec=pltpu.PrefetchScalarGridSpec(
            num_scalar_prefetch=1, grid=(r // rows_tile,),
            in_specs=[pl.BlockSpec((N_CHIPS, rows_tile, hcols), lambda i, h: (0, i, 0))],
            out_specs=pl.BlockSpec((rows_tile, hcols), lambda i, h: (i, h[0]))),
        out_shape=jax.ShapeDtypeStruct((r, full_cols), F32),
        compiler_params=_params(("parallel",)),
    )(half, slots)


def _adamw_rows(w, g, m, v, tile, name, g_transposed=False):
    r, c = w.shape

    def body(w_ref, g_ref, m_ref, v_ref, go_ref, d_ref, mo_ref, vo_ref):
        g_ = g_ref[...].T if g_transposed else g_ref[...]
        delta, m2, v2 = _adamw(w_ref[...], g_, m_ref[...], v_ref[...])
        go_ref[...] = g_
        d_ref[...] = delta
        mo_ref[...] = m2
        vo_ref[...] = v2

    blk = pl.BlockSpec((tile, c), lambda i: (i, 0))
    gblk = pl.BlockSpec((c, tile), lambda i: (0, i)) if g_transposed else blk
    return pl.pallas_call(
        body, name=name, grid=(r // tile,), out_shape=(jax.ShapeDtypeStruct((r, c), F32),) * 4,
        in_specs=[blk, gblk, blk, blk], out_specs=(blk,) * 4,
        compiler_params=_params(("parallel",)),
    )(w, g, m, v)


def _adamw_ada(cact, dmod, w, m, v, tile=256):
    r, c = w.shape

    def body(c_ref, d_ref, w_ref, m_ref, v_ref, go_ref, dl_ref, mo_ref, vo_ref):
        g_ = _dot(c_ref[...].astype(BF16), d_ref[...].astype(BF16), TN)
        delta, m2, v2 = _adamw(w_ref[...], g_, m_ref[...], v_ref[...])
        go_ref[...] = g_
        dl_ref[...] = delta
        mo_ref[...] = m2
        vo_ref[...] = v2

    blk = pl.BlockSpec((tile, c), lambda i: (i, 0))
    return pl.pallas_call(
        body, name="adamw_ada", grid=(r // tile,), out_shape=(jax.ShapeDtypeStruct((r, c), F32),) * 4,
        in_specs=[pl.BlockSpec((N_DEV, tile), lambda i: (0, i)), pl.BlockSpec((N_DEV, c), lambda i: (0, 0)), blk, blk, blk],
        out_specs=(blk,) * 4,
        compiler_params=_params(("parallel",)),
    )(cact, dmod, w, m, v)


def kernel(x, c, norm_g, w_ada, b_ada, w_in, attn_sinks, sgu_ln_g, sgu_ln_b, sgu_w, sgu_b, w_out, final_g, loss_target, m_norm_g, m_w_ada, m_b_ada, m_w_in, m_attn_sinks, m_sgu_ln_g, m_sgu_ln_b, m_sgu_w, m_sgu_b, m_w_out, m_final_g, v_norm_g, v_w_ada, v_b_ada, v_w_in, v_attn_sinks, v_sgu_ln_g, v_sgu_ln_b, v_sgu_w, v_sgu_b, v_w_out, v_final_g):
    xs, tgt = x[0], loss_target[0]
    core = lax.axis_index("c").astype(jnp.int32).reshape(1)
    chip = 2 * lax.axis_index("x") + lax.axis_index("y")
    sgu_bt = sgu_b[0].T

    mod, cact = _mod_exchange(c, w_ada[0], b_ada)
    win_t, wout = _weight_gather(_transpose_cast(w_in[0]), _cast_bf16(w_out[0]))
    h = _norm_modulate(xs, norm_g, mod)
    z = _matmul_nt(h, win_t, 512, 768, F32, "in_proj")
    cat, o = _mixer_fwd(z, attn_sinks, sgu_ln_g, sgu_ln_b, sgu_w[0], sgu_bt)
    dx1, dy, dfg, dgate, loss = _out_proj_loss(cat, wout, xs, tgt, mod, final_g.reshape(1, D_MODEL))

    dcat = _matmul_nt(dy, wout, 512, 1024, F32, "out_proj_bwd")
    dz, dsinks, dlg, dlb, dsgu_w, dsgu_b = _mixer_bwd(z, dcat, o, attn_sinks, sgu_ln_g, sgu_ln_b, sgu_w[0], sgu_bt)
    gx, dshift, dscale, dng = _in_proj_bwd(dz, win_t, xs, dx1, norm_g, mod)

    sw, so = _grad_to_sibling(_grad_matmul(dz, h, 1 - core, 768, "w_in_grad_sibling"),
                              _grad_matmul(cat, dy, 1 - core, 512, "w_out_grad_sibling"))
    pw = _grad_matmul(dz, h, core, 768, "w_in_grad_own", add=sw)
    po = _grad_matmul(cat, dy, core, 512, "w_out_grad_own", add=so)
    slots_w, slots_o = _grad_to_chips(pw, po)
    gw_t, go = _grad_halves_swap(_sum_slots(slots_w, core, IN_SHARD // 4, D_MODEL, "w_in_grad_sum"),
                                 _sum_slots(slots_o, core, OUT_SHARD // 2, D_MODEL, "w_out_grad_sum"))

    parts = dict(dng=dng, dshift=dshift, dscale=dscale, dgate=dgate, dfg=dfg, dlg=dlg, dlb=dlb, dsinks=dsinks,
                 dsgu_b=dsgu_b, dsgu_w=dsgu_w)
    small = dict(norm_g=(norm_g, m_norm_g, v_norm_g), b_ada=(b_ada, m_b_ada, v_b_ada),
                 attn_sinks=(attn_sinks, m_attn_sinks, v_attn_sinks), sgu_ln_g=(sgu_ln_g, m_sgu_ln_g, v_sgu_ln_g),
                 sgu_ln_b=(sgu_ln_b, m_sgu_ln_b, v_sgu_ln_b), sgu_w=(sgu_w, m_sgu_w, v_sgu_w), sgu_b=(sgu_b, m_sgu_b, v_sgu_b),
                 final_g=(final_g, m_final_g, v_final_g))
    res, dmod_all = _small_grads_step(parts, small)
    dmod_mine = lax.dynamic_slice_in_dim(dmod_all, ADA_SHARD * chip, ADA_SHARD, axis=1)

    res["w_ada"] = _adamw_ada(cact, dmod_mine, w_ada[0], m_w_ada[0], v_w_ada[0])
    res["w_in"] = _adamw_rows(w_in[0], gw_t, m_w_in[0], v_w_in[0], 256, "adamw_w_in", g_transposed=True)
    res["w_out"] = _adamw_rows(w_out[0], go, m_w_out[0], v_w_out[0], 256, "adamw_w_out")

    order = ["norm_g", "w_ada", "b_ada", "w_in", "attn_sinks", "sgu_ln_g", "sgu_ln_b", "sgu_w", "sgu_b", "w_out", "final_g"]
    like = dict(norm_g=norm_g, w_ada=w_ada, b_ada=b_ada, w_in=w_in, attn_sinks=attn_sinks, sgu_ln_g=sgu_ln_g, sgu_ln_b=sgu_ln_b,
                sgu_w=sgu_w, sgu_b=sgu_b, w_out=w_out, final_g=final_g)
    total = lax.psum(loss[0, 0], ("x", "y", "c"))
    outs = [total, gx[None]]
    for kind in range(4):
        outs += [res[n][kind].reshape(like[n].shape) for n in order]
    return tuple(outs)
```

```python
import functools
import math

import jax
import jax.numpy as jnp
from jax import lax
from jax.experimental import pallas as pl
from jax.experimental.pallas import tpu as pltpu

F32 = jnp.float32
BF16 = jnp.bfloat16

D_MODEL = 2048
D_ATTN = 1024
HEAD_DIM = 64
N_Q_HEADS = 16
Q_PER_KV = 8
D_KV = 128
BLOCK = 128
D_SGU = 1024
SGU_GROUPS = 8
GROUP_DIM = 128
D_IN = 5376
EPS = 1e-6
ATTN_SCALE = 1.0 / math.sqrt(HEAD_DIM)
N_CHIPS = 4
N_DEV = 8
IN_SHARD = D_IN // N_CHIPS
OUT_SHARD = D_MODEL // N_CHIPS
ADA_SHARD = 3 * D_MODEL // N_CHIPS
HALF = D_MODEL // 2

Q0, K0, V0, GA0, U0, VS0, GS0 = 0, 1024, 1152, 1280, 2304, 3328, 4352

ADAM_LR = 0.001
ADAM_B1 = 0.9
ADAM_B2 = 0.999
ADAM_EPS = 1e-08
ADAM_WD = 0.01
ADAM_STEP = 10

VMEM_LIMIT_V7X = 56 * 1024 * 1024

NN = (((1,), (0,)), ((), ()))
NT = (((1,), (1,)), ((), ()))
TN = (((0,), (0,)), ((), ()))
MESH = pl.DeviceIdType.MESH
ANY = pl.BlockSpec(memory_space=pl.ANY)
VMEM = pl.BlockSpec(memory_space=pltpu.VMEM)


def _dot(a, b, dims):
    return lax.dot_general(a, b, dims, preferred_element_type=F32)


def _params(sem=None):
    return pltpu.CompilerParams(dimension_semantics=sem, vmem_limit_bytes=VMEM_LIMIT_V7X)


def _sigmoid(x):
    return jax.nn.sigmoid(x)


def _adamw(w, g, m, v):
    m = ADAM_B1 * m + (1.0 - ADAM_B1) * g
    v = ADAM_B2 * v + (1.0 - ADAM_B2) * (g * g)
    m_hat = m / (1.0 - ADAM_B1 ** ADAM_STEP)
    v_hat = v / (1.0 - ADAM_B2 ** ADAM_STEP)
    delta = -ADAM_LR * (m_hat / (jnp.sqrt(v_hat) + ADAM_EPS) + ADAM_WD * w)
    return delta, m, v


def _place():
    return lax.axis_index("x"), lax.axis_index("y"), lax.axis_index("c")


def _flip(p, bit):
    return 1 - p if bit else p


def _mod_exchange(c, w_ada, b_ada):
    kdim = w_ada.shape[0]
    kc = 256

    def body(c_ref, w_ref, b_ref, mod_ref, cact_ref, cbuf, pbuf, rbuf, send_sems, recv_sems):
        x, y, cc = _place()
        me = 4 * x + 2 * y + cc
        chip = 2 * x + y
        cbuf[me] = c_ref[...]
        gathers = []
        for k in range(1, N_DEV):
            bx, by, bc = (k >> 2) & 1, (k >> 1) & 1, k & 1
            cp = pltpu.make_async_remote_copy(
                src_ref=cbuf.at[me], dst_ref=cbuf.at[me], send_sem=send_sems.at[k - 1], recv_sem=recv_sems.at[k - 1],
                device_id=(_flip(x, bx), _flip(y, by), _flip(cc, bc)), device_id_type=MESH)
            cp.start()
            gathers.append(cp)
        for cp in gathers:
            cp.wait()
        for b in range(N_DEV):
            cv = cbuf[b]
            cact_ref[b:b + 1, :] = cv * _sigmoid(cv)
        acc = jnp.zeros((N_DEV, ADA_SHARD), F32)
        for k0 in range(0, kdim, kc):
            acc = acc + _dot(cact_ref[:, k0:k0 + kc].astype(BF16), w_ref[k0:k0 + kc, :].astype(BF16), NN)
        for b in range(N_DEV):
            pbuf[b] = acc[b:b + 1, :]
        rbuf[chip] = pbuf[me]
        sends = []
        for j, (bx, by) in enumerate(((0, 1), (1, 0), (1, 1))):
            px, py = _flip(x, bx), _flip(y, by)
            cp = pltpu.make_async_remote_copy(
                src_ref=pbuf.at[4 * px + 2 * py + cc], dst_ref=rbuf.at[chip],
                send_sem=send_sems.at[N_DEV - 1 + j], recv_sem=recv_sems.at[N_DEV - 1 + j],
                device_id=(px, py, cc), device_id_type=MESH)
            cp.start()
            sends.append(cp)
        for cp in sends:
            cp.wait()
        for k in range(N_CHIPS):
            cols = slice(ADA_SHARD * k, ADA_SHARD * (k + 1))
            mod_ref[:, cols] = rbuf[k] + b_ref[:, cols]

    return pl.pallas_call(
        body, name="mod_exchange",
        out_shape=(jax.ShapeDtypeStruct((1, 3 * D_MODEL), F32), jax.ShapeDtypeStruct((N_DEV, D_MODEL), F32)),
        in_specs=[VMEM, VMEM, VMEM], out_specs=(VMEM, VMEM),
        scratch_shapes=[pltpu.VMEM((N_DEV, 1, D_MODEL), F32), pltpu.VMEM((N_DEV, 1, ADA_SHARD), F32),
                        pltpu.VMEM((N_CHIPS, 1, ADA_SHARD), F32),
                        pltpu.SemaphoreType.DMA((N_DEV + 2,)), pltpu.SemaphoreType.DMA((N_DEV + 2,))],
        compiler_params=_params(),
    )(c, w_ada, b_ada)


def _weight_gather(win_t, wout):
    hi, ho = IN_SHARD // 2, OUT_SHARD // 2

    def body(wi_ref, wo_ref, fi_ref, fo_ref, loc_sems, send_sems, recv_sems, fsend_sems, frecv_sems):
        x, y, cc = _place()
        chip = 2 * x + y
        others = [(_flip(x, bx), _flip(y, by)) for bx, by in ((0, 1), (1, 0), (1, 1))]

        def rows(ref, shard, half, k, c_half):
            return ref.at[pl.ds(shard * k + half * c_half, half), :]

        own = [pltpu.make_async_copy(wi_ref, fi_ref.at[pl.ds(IN_SHARD * chip, IN_SHARD), :], loc_sems.at[0]),
               pltpu.make_async_copy(wo_ref, fo_ref.at[pl.ds(OUT_SHARD * chip, OUT_SHARD), :], loc_sems.at[1])]
        for cp in own:
            cp.start()
        sends = []
        for j, (px, py) in enumerate(others):
            for a, (src, dst, shard, half) in enumerate(((wi_ref, fi_ref, IN_SHARD, hi), (wo_ref, fo_ref, OUT_SHARD, ho))):
                cp = pltpu.make_async_remote_copy(
                    src_ref=src.at[pl.ds(half * cc, half), :], dst_ref=rows(dst, shard, half, chip, cc),
                    send_sem=send_sems.at[2 * j + a], recv_sem=recv_sems.at[2 * j + a],
                    device_id=(px, py, cc), device_id_type=MESH)
                cp.start()
                sends.append(cp)
        forwards = []
        for j, (px, py) in enumerate(others):
            pchip = 2 * px + py
            for a, (dst, shard, half) in enumerate(((fi_ref, IN_SHARD, hi), (fo_ref, OUT_SHARD, ho))):
                landed = rows(dst, shard, half, pchip, cc)
                pltpu.make_async_remote_copy(
                    src_ref=landed, dst_ref=landed, send_sem=send_sems.at[2 * j + a], recv_sem=recv_sems.at[2 * j + a],
                    device_id=(px, py, cc), device_id_type=MESH).wait_recv()
                cp = pltpu.make_async_remote_copy(
                    src_ref=landed, dst_ref=landed, send_sem=fsend_sems.at[2 * j + a], recv_sem=frecv_sems.at[2 * j + a],
                    device_id=(x, y, 1 - cc), device_id_type=MESH)
                cp.start()
                forwards.append(cp)
        for j, (px, py) in enumerate(others):
            pchip = 2 * px + py
            for a, (dst, shard, half) in enumerate(((fi_ref, IN_SHARD, hi), (fo_ref, OUT_SHARD, ho))):
                theirs = rows(dst, shard, half, pchip, 1 - cc)
                pltpu.make_async_remote_copy(
                    src_ref=theirs, dst_ref=theirs, send_sem=fsend_sems.at[2 * j + a], recv_sem=frecv_sems.at[2 * j + a],
                    device_id=(x, y, 1 - cc), device_id_type=MESH).wait_recv()
        for cp in sends + forwards:
            cp.wait_send()
        for cp in own:
            cp.wait()

    return pl.pallas_call(
        body, name="weight_gather",
        out_shape=(jax.ShapeDtypeStruct((D_IN, D_MODEL), BF16), jax.ShapeDtypeStruct((D_MODEL, D_MODEL), BF16)),
        in_specs=[ANY, ANY], out_specs=(ANY, ANY),
        scratch_shapes=[pltpu.SemaphoreType.DMA((2,))] + [pltpu.SemaphoreType.DMA((6,))] * 4,
        compiler_params=_params(),
    )(win_t, wout)


def _grad_to_sibling(gw, go):
    def body(gw_ref, go_ref, rw_ref, ro_ref, send_sems, recv_sems):
        x, y, cc = _place()
        cps = []
        for a, (src, dst) in enumerate(((gw_ref, rw_ref), (go_ref, ro_ref))):
            cp = pltpu.make_async_remote_copy(src_ref=src, dst_ref=dst, send_sem=send_sems.at[a], recv_sem=recv_sems.at[a],
                                              device_id=(x, y, 1 - cc), device_id_type=MESH)
            cp.start()
            cps.append(cp)
        for cp in cps:
            cp.wait()

    return pl.pallas_call(
        body, name="grad_to_sibling",
        out_shape=(jax.ShapeDtypeStruct(gw.shape, gw.dtype), jax.ShapeDtypeStruct(go.shape, go.dtype)),
        in_specs=[ANY, ANY], out_specs=(ANY, ANY),
        scratch_shapes=[pltpu.SemaphoreType.DMA((2,)), pltpu.SemaphoreType.DMA((2,))],
        compiler_params=_params(),
    )(gw, go)


def _grad_to_chips(pw, po):
    def body(pw_ref, po_ref, sw_ref, so_ref, loc_sems, send_sems, recv_sems):
        x, y, cc = _place()
        chip = 2 * x + y
        others = [(_flip(x, bx), _flip(y, by)) for bx, by in ((0, 1), (1, 0), (1, 1))]
        arrays = ((pw_ref, sw_ref, IN_SHARD), (po_ref, so_ref, OUT_SHARD))
        own = [pltpu.make_async_copy(src.at[pl.ds(shard * chip, shard), :], dst.at[chip], loc_sems.at[a])
               for a, (src, dst, shard) in enumerate(arrays)]
        for cp in own:
            cp.start()
        cps = []
        for j, (px, py) in enumerate(others):
            for a, (src, dst, shard) in enumerate(arrays):
                cp = pltpu.make_async_remote_copy(
                    src_ref=src.at[pl.ds(shard * (2 * px + py), shard), :], dst_ref=dst.at[chip],
                    send_sem=send_sems.at[2 * j + a], recv_sem=recv_sems.at[2 * j + a],
                    device_id=(px, py, cc), device_id_type=MESH)
                cp.start()
                cps.append(cp)
        for cp in cps:
            cp.wait()
        for cp in own:
            cp.wait()

    return pl.pallas_call(
        body, name="grad_to_chips",
        out_shape=(jax.ShapeDtypeStruct((N_CHIPS, IN_SHARD, HALF), BF16), jax.ShapeDtypeStruct((N_CHIPS, OUT_SHARD, HALF), BF16)),
        in_specs=[ANY, ANY], out_specs=(ANY, ANY),
        scratch_shapes=[pltpu.SemaphoreType.DMA((2,)), pltpu.SemaphoreType.DMA((6,)), pltpu.SemaphoreType.DMA((6,))],
        compiler_params=_params(),
    )(pw, po)


def _grad_halves_swap(fw, fo):
    def body(fw_in, fo_in, fw_ref, fo_ref, send_sems, recv_sems):
        x, y, cc = _place()
        cps = []
        for a, ref in enumerate((fw_ref, fo_ref)):
            mine = ref.at[:, pl.ds(pl.multiple_of(HALF * cc, 128), HALF)]
            cp = pltpu.make_async_remote_copy(src_ref=mine, dst_ref=mine, send_sem=send_sems.at[a], recv_sem=recv_sems.at[a],
                                              device_id=(x, y, 1 - cc), device_id_type=MESH)
            cp.start()
            cps.append(cp)
        for cp in cps:
            cp.wait()

    return pl.pallas_call(
        body, name="grad_halves_swap",
        out_shape=(jax.ShapeDtypeStruct(fw.shape, fw.dtype), jax.ShapeDtypeStruct(fo.shape, fo.dtype)),
        in_specs=[ANY, ANY], out_specs=(ANY, ANY), input_output_aliases={0: 0, 1: 1},
        scratch_shapes=[pltpu.SemaphoreType.DMA((2,)), pltpu.SemaphoreType.DMA((2,))],
        compiler_params=_params(),
    )(fw, fo)


ROW_NORM_G, ROW_DMOD, ROW_FINAL_G, ROW_LN, ROW_SINKS, ROW_LOSS, ROW_SGU_B, MISC_ROWS = 0, 1, 4, 5, 6, 7, 8, 16


def _small_grads_step(parts, small):
    names = ["norm_g", "b_ada", "attn_sinks", "sgu_ln_g", "sgu_ln_b", "sgu_w", "sgu_b", "final_g"]
    part_names = ["dng", "dshift", "dscale", "dgate", "dfg", "dlg", "dlb", "dsinks", "dsgu_b", "dsgu_w", "loss"]
    n_parts = len(part_names)

    def body(*refs):
        p = dict(zip(part_names, refs[:n_parts]))
        wmv = {n: refs[n_parts + 3 * i: n_parts + 3 * i + 3] for i, n in enumerate(names)}
        outs = refs[n_parts + 3 * len(names):]
        res = {n: outs[4 * i: 4 * i + 4] for i, n in enumerate(names)}
        dmod_ref, loss_ref = outs[4 * len(names):4 * len(names) + 2]
        misc, wbuf, cmisc, cwbuf, dbuf, send_sems, recv_sems = outs[4 * len(names) + 2:]
        x, y, cc = _place()
        me = 4 * x + 2 * y + cc
        chip = 2 * x + y

        def remote(k, src, dst, peer):
            return pltpu.make_async_remote_copy(src_ref=src, dst_ref=dst, send_sem=send_sems.at[k], recv_sem=recv_sems.at[k],
                                                device_id=peer, device_id_type=MESH)

        dbuf[me] = jnp.zeros((8, D_MODEL), F32)
        dbuf[me, 0:1, :] = p["dshift"][...]
        dbuf[me, 1:2, :] = p["dscale"][...]
        dbuf[me, 2:3, :] = p["dgate"][...]
        cps = []
        for k in range(1, N_DEV):
            bx, by, bc = (k >> 2) & 1, (k >> 1) & 1, k & 1
            cps.append(remote(k - 1, dbuf.at[me], dbuf.at[me], (_flip(x, bx), _flip(y, by), _flip(cc, bc))))

        misc[cc] = jnp.zeros((MISC_ROWS, D_MODEL), F32)
        misc[cc, ROW_NORM_G:ROW_NORM_G + 1, :] = p["dng"][...]
        misc[cc, ROW_DMOD:ROW_DMOD + 1, :] = p["dshift"][...]
        misc[cc, ROW_DMOD + 1:ROW_DMOD + 2, :] = p["dscale"][...]
        misc[cc, ROW_DMOD + 2:ROW_DMOD + 3, :] = p["dgate"][...]
        misc[cc, ROW_FINAL_G:ROW_FINAL_G + 1, :] = p["dfg"][...]
        misc[cc, ROW_LN:ROW_LN + 1, 0:D_SGU] = p["dlg"][...]
        misc[cc, ROW_LN:ROW_LN + 1, D_SGU:2 * D_SGU] = p["dlb"][...]
        misc[cc, ROW_SINKS:ROW_SINKS + 1, 0:N_Q_HEADS] = p["dsinks"][...]
        misc[cc, ROW_LOSS:ROW_LOSS + 1, 0:1] = p["loss"][...]
        misc[cc, ROW_SGU_B:ROW_SGU_B + SGU_GROUPS, 0:BLOCK] = p["dsgu_b"][...]
        for g in range(SGU_GROUPS):
            wbuf[cc, BLOCK * g:BLOCK * (g + 1), :] = p["dsgu_w"][g]
        pair = [remote(N_DEV - 1 + a, buf.at[cc], buf.at[cc], (x, y, 1 - cc)) for a, buf in enumerate((misc, wbuf))]
        for cp in cps + pair:
            cp.start()
        for cp in pair:
            cp.wait()
        cmisc[chip] = misc[0] + misc[1]
        cwbuf[chip] = wbuf[0] + wbuf[1]
        quad = []
        for j, (bx, by) in enumerate(((0, 1), (1, 0), (1, 1))):
            for a, buf in enumerate((cmisc, cwbuf)):
                quad.append(remote(N_DEV + 1 + 2 * j + a, buf.at[chip], buf.at[chip], (_flip(x, bx), _flip(y, by), cc)))
        for cp in quad:
            cp.start()
        for cp in cps + quad:
            cp.wait()

        tot = cmisc[0]
        for k in range(1, N_CHIPS):
            tot = tot + cmisc[k]
        loss_ref[...] = tot[ROW_LOSS:ROW_LOSS + 1, 0:1]
        for b in range(N_DEV):
            dmod_ref[b:b + 1, :] = jnp.concatenate([dbuf[b, t:t + 1, :] for t in range(3)], axis=1)

        def update(name, g, pick=lambda r: r[...], put=None):
            w_ref, m_ref, v_ref = wmv[name]
            delta, m2, v2 = _adamw(pick(w_ref), g, pick(m_ref), pick(v_ref))
            for ref, val in zip(res[name], (g, delta, m2, v2)):
                if put is None:
                    ref[...] = val
                else:
                    put(ref, val)

        update("norm_g", tot[ROW_NORM_G:ROW_NORM_G + 1, :])
        update("final_g", tot[ROW_FINAL_G:ROW_FINAL_G + 1, :])
        for t in range(3):
            cols = slice(D_MODEL * t, D_MODEL * (t + 1))

            def put_cols(ref, val, cols=cols):
                ref[:, cols] = val
            update("b_ada", tot[ROW_DMOD + t:ROW_DMOD + t + 1, :], pick=lambda r, cols=cols: r[:, cols], put=put_cols)
        update("sgu_ln_g", tot[ROW_LN:ROW_LN + 1, 0:D_SGU])
        update("sgu_ln_b", tot[ROW_LN:ROW_LN + 1, D_SGU:2 * D_SGU])
        update("attn_sinks", tot[ROW_SINKS:ROW_SINKS + 1, 0:N_Q_HEADS])
        update("sgu_b", tot[ROW_SGU_B:ROW_SGU_B + SGU_GROUPS, 0:BLOCK])
        for g in range(SGU_GROUPS):
            rows = slice(BLOCK * g, BLOCK * (g + 1))
            gw = cwbuf[0, rows, :]
            for k in range(1, N_CHIPS):
                gw = gw + cwbuf[k, rows, :]

            def put_group(ref, val, g=g):
                ref[g] = val
            update("sgu_w", gw, pick=lambda r, g=g: r[g], put=put_group)

    shapes = {"norm_g": (1, D_MODEL), "b_ada": (1, 3 * D_MODEL), "attn_sinks": (1, N_Q_HEADS), "sgu_ln_g": (1, D_SGU),
              "sgu_ln_b": (1, D_SGU), "sgu_w": (SGU_GROUPS, BLOCK, BLOCK), "sgu_b": (SGU_GROUPS, BLOCK), "final_g": (1, D_MODEL)}
    operands = [parts[n] for n in part_names]
    for n in names:
        operands += [a.reshape(shapes[n]) for a in small[n]]
    out_shape = []
    for n in names:
        out_shape += [jax.ShapeDtypeStruct(shapes[n], F32)] * 4
    out_shape += [jax.ShapeDtypeStruct((N_DEV, 3 * D_MODEL), F32), jax.ShapeDtypeStruct((1, 1), F32)]
    n_sems = (N_DEV - 1) + 2 + 2 * (N_CHIPS - 1)
    outs = pl.pallas_call(
        body, name="small_grads_step", out_shape=tuple(out_shape),
        in_specs=[VMEM] * len(operands), out_specs=tuple([VMEM] * len(out_shape)),
        scratch_shapes=[pltpu.VMEM((2, MISC_ROWS, D_MODEL), F32), pltpu.VMEM((2, SGU_GROUPS * BLOCK, BLOCK), F32),
                        pltpu.VMEM((N_CHIPS, MISC_ROWS, D_MODEL), F32), pltpu.VMEM((N_CHIPS, SGU_GROUPS * BLOCK, BLOCK), F32),
                        pltpu.VMEM((N_DEV, 8, D_MODEL), F32),
                        pltpu.SemaphoreType.DMA((n_sems,)), pltpu.SemaphoreType.DMA((n_sems,))],
        compiler_params=_params(),
    )(*operands)
    res = {n: outs[4 * i: 4 * i + 4] for i, n in enumerate(names)}
    return res, outs[4 * len(names)], outs[4 * len(names) + 1]


def _cast_bf16(w, tile):
    r, c = w.shape

    def body(w_ref, o_ref):
        o_ref[...] = w_ref[...].astype(BF16)

    return pl.pallas_call(
        body, name="cast_bf16", grid=(r // tile,), out_shape=jax.ShapeDtypeStruct((r, c), BF16),
        in_specs=[pl.BlockSpec((tile, c), lambda i: (i, 0))], out_specs=pl.BlockSpec((tile, c), lambda i: (i, 0)),
        compiler_params=_params(("parallel",)),
    )(w)


def _norm_modulate(x, norm_g, mod):
    s, d = x.shape
    tm = min(256, s)

    def body(x_ref, g_ref, mod_ref, h_ref):
        xv = x_ref[...]
        r = lax.rsqrt(jnp.mean(xv * xv, axis=-1, keepdims=True) + EPS)
        h = (xv * r * g_ref[...]) * (1.0 + mod_ref[:, d:2 * d]) + mod_ref[:, 0:d]
        h_ref[...] = h.astype(BF16)

    return pl.pallas_call(
        body, name="norm_modulate", grid=(s // tm,), out_shape=jax.ShapeDtypeStruct((s, d), BF16),
        in_specs=[pl.BlockSpec((tm, d), lambda i: (i, 0)), pl.BlockSpec((1, d), lambda i: (0, 0)),
                  pl.BlockSpec((1, 3 * d), lambda i: (0, 0))],
        out_specs=pl.BlockSpec((tm, d), lambda i: (i, 0)),
        compiler_params=_params(("parallel",)),
    )(x, norm_g, mod)


def _matmul_nt(a, bt, tm, tn, out_dtype, name):
    m, k = a.shape
    n = bt.shape[0]
    tm, tn = min(tm, m), min(tn, n)

    def body(a_ref, b_ref, o_ref):
        o_ref[...] = _dot(a_ref[...], b_ref[...], NT).astype(out_dtype)

    return pl.pallas_call(
        body, name=name, grid=(m // tm, n // tn), out_shape=jax.ShapeDtypeStruct((m, n), out_dtype),
        in_specs=[pl.BlockSpec((tm, k), lambda i, j: (i, 0)), pl.BlockSpec((tn, k), lambda i, j: (j, 0))],
        out_specs=pl.BlockSpec((tm, tn), lambda i, j: (i, j)),
        compiler_params=_params(("parallel", "parallel")),
    )(a, bt)


def _grad_matmul(a, b, half, tm, name, add=None):
    t, m = a.shape
    tm = min(tm, m)

    def body(half_ref, a_ref, b_ref, *rest):
        if add is None:
            (o_ref,) = rest
            o_ref[...] = _dot(a_ref[...], b_ref[...], TN).astype(BF16)
        else:
            add_ref, o_ref = rest
            o_ref[...] = (_dot(a_ref[...], b_ref[...], TN) + add_ref[...].astype(F32)).astype(BF16)

    in_specs = [pl.BlockSpec((t, tm), lambda i, h: (0, i)), pl.BlockSpec((t, HALF), lambda i, h: (0, h[0]))]
    operands = [a, b]
    if add is not None:
        in_specs.append(pl.BlockSpec((tm, HALF), lambda i, h: (i, 0)))
        operands.append(add)
    return pl.pallas_call(
        body, name=name,
        grid_spec=pltpu.PrefetchScalarGridSpec(num_scalar_prefetch=1, grid=(m // tm,), in_specs=in_specs,
                                               out_specs=pl.BlockSpec((tm, HALF), lambda i, h: (i, 0))),
        out_shape=jax.ShapeDtypeStruct((m, HALF), BF16),
        compiler_params=_params(("parallel",)),
    )(half, *operands)


def _mixer_masks(n):
    row = lax.broadcasted_iota(jnp.int32, (Q_PER_KV * BLOCK, 2 * BLOCK), 0) & (BLOCK - 1)
    col = lax.broadcasted_iota(jnp.int32, (Q_PER_KV * BLOCK, 2 * BLOCK), 1)
    valid = (col > row) & (col <= row + BLOCK) & ((col >= BLOCK) | (n > 0))
    r2 = lax.broadcasted_iota(jnp.int32, (BLOCK, BLOCK), 0)
    c2 = lax.broadcasted_iota(jnp.int32, (BLOCK, BLOCK), 1)
    return valid, r2 >= c2


def _layer_norm_parts(vs):
    mu = jnp.mean(vs, axis=-1, keepdims=True)
    xc = vs - mu
    rstd = lax.rsqrt(jnp.mean(xc * xc, axis=-1, keepdims=True) + EPS)
    return xc * rstd, rstd


def _stack_heads(ref, base, hk):
    first = base + HEAD_DIM * Q_PER_KV * hk
    return jnp.concatenate([ref[:, first + HEAD_DIM * g:first + HEAD_DIM * (g + 1)] for g in range(Q_PER_KV)], axis=0)


def _unstack_heads(ref, base, hk, val):
    first = base + HEAD_DIM * Q_PER_KV * hk
    for g in range(Q_PER_KV):
        ref[:, first + HEAD_DIM * g:first + HEAD_DIM * (g + 1)] = val[BLOCK * g:BLOCK * (g + 1), :].astype(ref.dtype)


def _stack_sinks(sink_ref, hk):
    return jnp.concatenate([jnp.broadcast_to(sink_ref[:, h:h + 1], (BLOCK, 1))
                            for h in range(Q_PER_KV * hk, Q_PER_KV * (hk + 1))], axis=0)


def _attn_probs(qs, kh, valid, sk):
    s = _dot(qs, kh, NT) * ATTN_SCALE
    s = jnp.where(valid, s, -jnp.inf)
    m = jnp.maximum(jnp.max(s, axis=-1, keepdims=True), sk)
    e = jnp.exp(s - m)
    es = jnp.exp(sk - m)
    inv = 1.0 / (jnp.sum(e, axis=-1, keepdims=True) + es)
    return e * inv, es * inv


def _kv_ext(z_ref, kvp_ref):
    k_ext = jnp.concatenate([kvp_ref[:, 0:D_KV], z_ref[:, K0:K0 + D_KV]], axis=0).astype(BF16)
    v_ext = jnp.concatenate([kvp_ref[:, D_KV:2 * D_KV], z_ref[:, V0:V0 + D_KV]], axis=0).astype(BF16)
    return k_ext, v_ext


def _sgu_mixed(w_ref, bt_ref, vn, tril):
    ws = [jnp.where(tril, w_ref[g], 0.0).astype(BF16) for g in range(SGU_GROUPS)]
    mixed = jnp.concatenate([_dot(ws[g], vn[:, GROUP_DIM * g:GROUP_DIM * (g + 1)], NN) + bt_ref[:, g:g + 1]
                             for g in range(SGU_GROUPS)], axis=1)
    return ws, mixed


def _mixer_fwd(z, sinks, ln_g, ln_b, sgu_w, sgu_bt):
    s = z.shape[0]
    nb = s // BLOCK

    def body(z_ref, kvp_ref, sink_ref, lg_ref, lb_ref, w_ref, bt_ref, cat_ref, o_ref):
        n = pl.program_id(0)
        valid, tril = _mixer_masks(n)
        k_ext, v_ext = _kv_ext(z_ref, kvp_ref)
        for hk in range(2):
            kv = slice(HEAD_DIM * hk, HEAD_DIM * (hk + 1))
            p, _ = _attn_probs(_stack_heads(z_ref, Q0, hk).astype(BF16), k_ext[:, kv], valid, _stack_sinks(sink_ref, hk))
            o = _dot(p.astype(BF16), v_ext[:, kv], NN)
            ga = _stack_heads(z_ref, GA0, hk)
            _unstack_heads(o_ref, 0, hk, o)
            _unstack_heads(cat_ref, 0, hk, o * (ga * _sigmoid(ga)))
        xhat, _ = _layer_norm_parts(z_ref[:, VS0:VS0 + D_SGU])
        vn = (xhat * lg_ref[...] + lb_ref[...]).astype(BF16)
        _, mixed = _sgu_mixed(w_ref, bt_ref, vn, tril)
        gt = z_ref[:, GS0:GS0 + D_SGU]
        cat_ref[:, D_ATTN:D_MODEL] = ((z_ref[:, U0:U0 + D_SGU] * mixed) * (gt * _sigmoid(gt))).astype(BF16)

    const2 = lambda n: (0, 0)
    return pl.pallas_call(
        body, name="mixer_fwd", grid=(nb,),
        out_shape=(jax.ShapeDtypeStruct((s, D_MODEL), BF16), jax.ShapeDtypeStruct((s, D_ATTN), F32)),
        in_specs=[pl.BlockSpec((BLOCK, D_IN), lambda n: (n, 0)),
                  pl.BlockSpec((BLOCK, 2 * D_KV), lambda n: (jnp.maximum(n - 1, 0), K0 // (2 * D_KV))),
                  pl.BlockSpec((1, N_Q_HEADS), const2), pl.BlockSpec((1, D_SGU), const2), pl.BlockSpec((1, D_SGU), const2),
                  pl.BlockSpec((SGU_GROUPS, BLOCK, BLOCK), lambda n: (0, 0, 0)), pl.BlockSpec((BLOCK, SGU_GROUPS), const2)],
        out_specs=(pl.BlockSpec((BLOCK, D_MODEL), lambda n: (n, 0)), pl.BlockSpec((BLOCK, D_ATTN), lambda n: (n, 0))),
        compiler_params=_params(("parallel",)),
    )(z, z, sinks, ln_g, ln_b, sgu_w, sgu_bt)


def _out_proj_loss(cat, wout, x, target, mod, final_g):
    s, d = x.shape
    tm = min(256, s)

    def body(cat_ref, w_ref, x_ref, t_ref, mod_ref, fg_ref, dx1_ref, dy_ref, dfg_ref, dgate_ref, loss_ref):
        i = pl.program_id(0)
        gate = mod_ref[:, 2 * d:3 * d]
        fg = fg_ref[...]
        yv = _dot(cat_ref[...], w_ref[...], NN)
        x1 = x_ref[...] + gate * yv
        r = lax.rsqrt(jnp.mean(x1 * x1, axis=-1, keepdims=True) + EPS)
        xr = x1 * r
        err = xr * fg - t_ref[...]
        dout = err * (1.0 / d)
        gy = dout * fg
        dx1 = r * gy - xr * (r * r * jnp.mean(gy * x1, axis=-1, keepdims=True))
        dx1_ref[...] = dx1
        dy_ref[...] = (gate * dx1).astype(BF16)

        @pl.when(i == 0)
        def _():
            dfg_ref[...] = jnp.zeros_like(dfg_ref)
            dgate_ref[...] = jnp.zeros_like(dgate_ref)
            loss_ref[...] = jnp.zeros_like(loss_ref)

        dfg_ref[...] += jnp.sum(dout * xr, axis=0, keepdims=True)
        dgate_ref[...] += jnp.sum(dx1 * yv, axis=0, keepdims=True)
        loss_ref[...] += 0.5 * jnp.sum(jnp.mean(err * err, axis=-1, keepdims=True), axis=0, keepdims=True)

    row = lambda i: (i, 0)
    const = lambda i: (0, 0)
    return pl.pallas_call(
        body, name="out_proj_loss", grid=(s // tm,),
        out_shape=(jax.ShapeDtypeStruct((s, d), F32), jax.ShapeDtypeStruct((s, d), BF16), jax.ShapeDtypeStruct((1, d), F32),
                   jax.ShapeDtypeStruct((1, d), F32), jax.ShapeDtypeStruct((1, 1), F32)),
        in_specs=[pl.BlockSpec((tm, d), row), pl.BlockSpec((d, d), const), pl.BlockSpec((tm, d), row), pl.BlockSpec((tm, d), row),
                  pl.BlockSpec((1, 3 * d), const), pl.BlockSpec((1, d), const)],
        out_specs=(pl.BlockSpec((tm, d), row), pl.BlockSpec((tm, d), row), pl.BlockSpec((1, d), const), pl.BlockSpec((1, d), const),
                   pl.BlockSpec((1, 1), const)),
        compiler_params=_params(("arbitrary",)),
    )(cat, wout, x, target, mod, final_g)


def _mixer_bwd(z, dcat, o, sinks, ln_g, ln_b, sgu_w, sgu_bt):
    s = z.shape[0]
    nb = s // BLOCK

    def body(z_ref, kvp_ref, dc_ref, o_ref, sink_ref, lg_ref, lb_ref, w_ref, bt_ref,
             dz_ref, dsink_ref, dlg_ref, dlb_ref, dw_ref, db_ref, pend, dkv, carry, dsink_acc, dbt_acc):
        n = pl.program_id(0)

        @pl.when(n == 0)
        def _():
            pend[...] = jnp.zeros_like(pend)
            carry[...] = jnp.zeros_like(carry)
            dsink_acc[...] = jnp.zeros_like(dsink_acc)
            dbt_acc[...] = jnp.zeros_like(dbt_acc)
            dlg_ref[...] = jnp.zeros_like(dlg_ref)
            dlb_ref[...] = jnp.zeros_like(dlb_ref)
            dw_ref[...] = jnp.zeros_like(dw_ref)

        dz_ref[:, 0:K0] = pend[:, 0:K0]
        dz_ref[:, GA0:D_IN] = pend[:, GA0:D_IN]

        @pl.when(n == nb)
        def _():
            dkv[...] = jnp.zeros_like(dkv)

        @pl.when(n < nb)
        def _():
            valid, tril = _mixer_masks(n)
            k_ext, v_ext = _kv_ext(z_ref, kvp_ref)
            for hk in range(2):
                kv = slice(HEAD_DIM * hk, HEAD_DIM * (hk + 1))
                kh, vh = k_ext[:, kv], v_ext[:, kv]
                qs = _stack_heads(z_ref, Q0, hk).astype(BF16)
                p, p_sink = _attn_probs(qs, kh, valid, _stack_sinks(sink_ref, hk))
                ga = _stack_heads(z_ref, GA0, hk)
                sg = _sigmoid(ga)
                datt = _stack_heads(dc_ref, 0, hk)
                do = (datt * (ga * sg)).astype(BF16)
                _unstack_heads(pend, GA0, hk, datt * _stack_heads(o_ref, 0, hk) * (sg * (1.0 + ga * (1.0 - sg))))
                dp = _dot(do, vh, NT)
                delta = jnp.sum(p * dp, axis=-1, keepdims=True)
                ds = (p * (dp - delta) * ATTN_SCALE).astype(BF16)
                dsk = -p_sink * delta
                for gq in range(Q_PER_KV):
                    h = hk * Q_PER_KV + gq
                    dsink_acc[:, h:h + 1] += jnp.sum(dsk[BLOCK * gq:BLOCK * (gq + 1), :], axis=0, keepdims=True)
                _unstack_heads(pend, Q0, hk, _dot(ds, kh, NN))
                dkv[:, kv] = _dot(ds, qs, TN)
                dkv[:, D_KV + HEAD_DIM * hk:D_KV + HEAD_DIM * (hk + 1)] = _dot(p.astype(BF16), do, TN)

            xhat, rstd = _layer_norm_parts(z_ref[:, VS0:VS0 + D_SGU])
            lg = lg_ref[...]
            vn = (xhat * lg + lb_ref[...]).astype(BF16)
            ws, mixed = _sgu_mixed(w_ref, bt_ref, vn, tril)
            u = z_ref[:, U0:U0 + D_SGU]
            gt = z_ref[:, GS0:GS0 + D_SGU]
            sg = _sigmoid(gt)
            dsgu = dc_ref[:, D_ATTN:D_MODEL]
            dgated = dsgu * (gt * sg)
            pend[:, U0:U0 + D_SGU] = (dgated * mixed).astype(BF16)
            pend[:, GS0:GS0 + D_SGU] = (dsgu * (u * mixed) * (sg * (1.0 + gt * (1.0 - sg)))).astype(BF16)
            dmixed = dgated * u
            dmb = dmixed.astype(BF16)
            dvn_parts = []
            for g in range(SGU_GROUPS):
                gs_ = slice(GROUP_DIM * g, GROUP_DIM * (g + 1))
                dw_ref[g] += jnp.where(tril, _dot(dmb[:, gs_], vn[:, gs_], NT), 0.0)
                dbt_acc[:, g:g + 1] += jnp.sum(dmixed[:, gs_], axis=-1, keepdims=True)
                dvn_parts.append(_dot(ws[g], dmb[:, gs_], TN))
            dvn = jnp.concatenate(dvn_parts, axis=1)
            dlg_ref[...] += jnp.sum(dvn * xhat, axis=0, keepdims=True)
            dlb_ref[...] += jnp.sum(dvn, axis=0, keepdims=True)
            dxh = dvn * lg
            pend[:, VS0:VS0 + D_SGU] = (rstd * (dxh - jnp.mean(dxh, axis=-1, keepdims=True)
                                                - xhat * jnp.mean(dxh * xhat, axis=-1, keepdims=True))).astype(BF16)

        dz_ref[:, K0:GA0] = (carry[...] + dkv[0:BLOCK, :]).astype(BF16)
        carry[...] = dkv[BLOCK:2 * BLOCK, :]

        @pl.when(n == nb)
        def _():
            dsink_ref[...] = dsink_acc[:, 0:N_Q_HEADS]
            db_ref[...] = dbt_acc[...].T[0:SGU_GROUPS, :]

    cur = lambda n: (jnp.minimum(n, nb - 1), 0)
    const2 = lambda n: (0, 0)
    return pl.pallas_call(
        body, name="mixer_bwd", grid=(nb + 1,),
        out_shape=(jax.ShapeDtypeStruct((s, D_IN), BF16), jax.ShapeDtypeStruct((1, N_Q_HEADS), F32),
                   jax.ShapeDtypeStruct((1, D_SGU), F32), jax.ShapeDtypeStruct((1, D_SGU), F32),
                   jax.ShapeDtypeStruct((SGU_GROUPS, BLOCK, BLOCK), F32), jax.ShapeDtypeStruct((SGU_GROUPS, BLOCK), F32)),
        in_specs=[pl.BlockSpec((BLOCK, D_IN), cur),
                  pl.BlockSpec((BLOCK, 2 * D_KV), lambda n: (jnp.maximum(jnp.minimum(n, nb - 1) - 1, 0), K0 // (2 * D_KV))),
                  pl.BlockSpec((BLOCK, D_MODEL), cur), pl.BlockSpec((BLOCK, D_ATTN), cur),
                  pl.BlockSpec((1, N_Q_HEADS), const2), pl.BlockSpec((1, D_SGU), const2), pl.BlockSpec((1, D_SGU), const2),
                  pl.BlockSpec((SGU_GROUPS, BLOCK, BLOCK), lambda n: (0, 0, 0)), pl.BlockSpec((BLOCK, SGU_GROUPS), const2)],
        out_specs=(pl.BlockSpec((BLOCK, D_IN), lambda n: (jnp.maximum(n - 1, 0), 0)),
                   pl.BlockSpec((1, N_Q_HEADS), const2), pl.BlockSpec((1, D_SGU), const2), pl.BlockSpec((1, D_SGU), const2),
                   pl.BlockSpec((SGU_GROUPS, BLOCK, BLOCK), lambda n: (0, 0, 0)), pl.BlockSpec((SGU_GROUPS, BLOCK), const2)),
        scratch_shapes=[pltpu.VMEM((BLOCK, D_IN), BF16), pltpu.VMEM((2 * BLOCK, 2 * D_KV), F32), pltpu.VMEM((BLOCK, 2 * D_KV), F32),
                        pltpu.VMEM((1, BLOCK), F32), pltpu.VMEM((BLOCK, BLOCK), F32)],
        compiler_params=_params(("arbitrary",)),
    )(z, z, dcat, o, sinks, ln_g, ln_b, sgu_w, sgu_bt)


def _in_proj_bwd(dz, win_t, x, dx1, norm_g, mod):
    s, d = x.shape
    kdim = dz.shape[1]
    tm, tk = min(512, s), 768
    nk = kdim // tk

    def body(dz_ref, w_ref, x_ref, dx1_ref, g_ref, mod_ref, gx_ref, dshift_ref, dscale_ref, dng_ref, acc):
        i, k = pl.program_id(0), pl.program_id(1)

        @pl.when(k == 0)
        def _():
            acc[...] = jnp.zeros_like(acc)

        acc[...] += _dot(dz_ref[...], w_ref[...], NN)

        @pl.when((i == 0) & (k == nk - 1))
        def _():
            dshift_ref[...] = jnp.zeros_like(dshift_ref)
            dscale_ref[...] = jnp.zeros_like(dscale_ref)
            dng_ref[...] = jnp.zeros_like(dng_ref)

        @pl.when(k == nk - 1)
        def _():
            dh = acc[...]
            xv = x_ref[...]
            ng = g_ref[...]
            r = lax.rsqrt(jnp.mean(xv * xv, axis=-1, keepdims=True) + EPS)
            xn = xv * r
            dshift_ref[...] += jnp.sum(dh, axis=0, keepdims=True)
            dscale_ref[...] += jnp.sum(dh * (xn * ng), axis=0, keepdims=True)
            dhs = dh * (1.0 + mod_ref[:, d:2 * d])
            dng_ref[...] += jnp.sum(dhs * xn, axis=0, keepdims=True)
            dxn = dhs * ng
            gx_ref[...] = dx1_ref[...] + r * dxn - xn * (r * r * jnp.mean(dxn * xv, axis=-1, keepdims=True))

    row = lambda i, k: (i, 0)
    const = lambda i, k: (0, 0)
    return pl.pallas_call(
        body, name="in_proj_bwd", grid=(s // tm, nk),
        out_shape=(jax.ShapeDtypeStruct((s, d), F32),) + (jax.ShapeDtypeStruct((1, d), F32),) * 3,
        in_specs=[pl.BlockSpec((tm, tk), lambda i, k: (i, k)), pl.BlockSpec((tk, d), lambda i, k: (k, 0)),
                  pl.BlockSpec((tm, d), row), pl.BlockSpec((tm, d), row), pl.BlockSpec((1, d), const), pl.BlockSpec((1, 3 * d), const)],
        out_specs=(pl.BlockSpec((tm, d), row),) + (pl.BlockSpec((1, d), const),) * 3,
        scratch_shapes=[pltpu.VMEM((tm, d), F32)],
        compiler_params=_params(("arbitrary", "arbitrary")),
    )(dz, win_t, x, dx1, norm_g, mod)


def _sum_slots(slots, half, rows_tile, full_cols, name):
    _, r, hcols = slots.shape

    def body(half_ref, s_ref, o_ref):
        acc = s_ref[0].astype(F32)
        for k in range(1, N_CHIPS):
            acc = acc + s_ref[k].astype(F32)
        o_ref[...] = acc

    return pl.pallas_call(
        body, name=name,
        grid_spec=pltpu.PrefetchScalarGridSpec(
            num_scalar_prefetch=1, grid=(r // rows_tile,),
            in_specs=[pl.BlockSpec((N_CHIPS, rows_tile, hcols), lambda i, h: (0, i, 0))],
            out_specs=pl.BlockSpec((rows_tile, hcols), lambda i, h: (i, h[0]))),
        out_shape=jax.ShapeDtypeStruct((r, full_cols), F32),
        compiler_params=_params(("parallel",)),
    )(half, slots)


def _adamw_rows(w, g, m, v, tile, name):
    r, c = w.shape

    def body(w_ref, g_ref, m_ref, v_ref, go_ref, d_ref, mo_ref, vo_ref):
        g_ = g_ref[...]
        delta, m2, v2 = _adamw(w_ref[...], g_, m_ref[...], v_ref[...])
        go_ref[...] = g_
        d_ref[...] = delta
        mo_ref[...] = m2
        vo_ref[...] = v2

    blk = pl.BlockSpec((tile, c), lambda i: (i, 0))
    return pl.pallas_call(
        body, name=name, grid=(r // tile,), out_shape=(jax.ShapeDtypeStruct((r, c), F32),) * 4,
        in_specs=[blk] * 4, out_specs=(blk,) * 4,
        compiler_params=_params(("parallel",)),
    )(w, g, m, v)


def _adamw_ada(cact, dmod, w, m, v, tile=256):
    r, c = w.shape

    def body(c_ref, d_ref, w_ref, m_ref, v_ref, go_ref, dl_ref, mo_ref, vo_ref):
        g_ = _dot(c_ref[...].astype(BF16), d_ref[...].astype(BF16), TN)
        delta, m2, v2 = _adamw(w_ref[...], g_, m_ref[...], v_ref[...])
        go_ref[...] = g_
        dl_ref[...] = delta
        mo_ref[...] = m2
        vo_ref[...] = v2

    blk = pl.BlockSpec((tile, c), lambda i: (i, 0))
    return pl.pallas_call(
        body, name="adamw_ada", grid=(r // tile,), out_shape=(jax.ShapeDtypeStruct((r, c), F32),) * 4,
        in_specs=[pl.BlockSpec((N_DEV, tile), lambda i: (0, i)), pl.BlockSpec((N_DEV, c), lambda i: (0, 0)), blk, blk, blk],
        out_specs=(blk,) * 4,
        compiler_params=_params(("parallel",)),
    )(cact, dmod, w, m, v)


def kernel(x, c, norm_g, w_ada, b_ada, w_in, attn_sinks, sgu_ln_g, sgu_ln_b, sgu_w, sgu_b, w_out, final_g, loss_target, m_norm_g, m_w_ada, m_b_ada, m_w_in, m_attn_sinks, m_sgu_ln_g, m_sgu_ln_b, m_sgu_w, m_sgu_b, m_w_out, m_final_g, v_norm_g, v_w_ada, v_b_ada, v_w_in, v_attn_sinks, v_sgu_ln_g, v_sgu_ln_b, v_sgu_w, v_sgu_b, v_w_out, v_final_g):
    xs, tgt = x[0], loss_target[0]
    core = lax.axis_index("c").astype(jnp.int32).reshape(1)
    chip = 2 * lax.axis_index("x") + lax.axis_index("y")
    sgu_bt = sgu_b[0].T

    mod, cact = _mod_exchange(c, w_ada[0], b_ada)
    w_in_t, m_in_t, v_in_t = (jnp.swapaxes(a[0], 0, 1) for a in (w_in, m_w_in, v_w_in))
    win_t, wout = _weight_gather(_cast_bf16(w_in_t, IN_SHARD // 4), _cast_bf16(w_out[0], OUT_SHARD // 2))
    h = _norm_modulate(xs, norm_g, mod)
    z = _matmul_nt(h, win_t, 512, 768, F32, "in_proj")
    cat, o = _mixer_fwd(z, attn_sinks, sgu_ln_g, sgu_ln_b, sgu_w[0], sgu_bt)
    dx1, dy, dfg, dgate, loss = _out_proj_loss(cat, wout, xs, tgt, mod, final_g.reshape(1, D_MODEL))

    dcat = _matmul_nt(dy, wout, 512, 1024, F32, "out_proj_bwd")
    dz, dsinks, dlg, dlb, dsgu_w, dsgu_b = _mixer_bwd(z, dcat, o, attn_sinks, sgu_ln_g, sgu_ln_b, sgu_w[0], sgu_bt)
    gx, dshift, dscale, dng = _in_proj_bwd(dz, win_t, xs, dx1, norm_g, mod)

    sw, so = _grad_to_sibling(_grad_matmul(dz, h, 1 - core, 768, "w_in_grad_sibling"),
                              _grad_matmul(cat, dy, 1 - core, 512, "w_out_grad_sibling"))
    pw = _grad_matmul(dz, h, core, 768, "w_in_grad_own", add=sw)
    po = _grad_matmul(cat, dy, core, 512, "w_out_grad_own", add=so)
    slots_w, slots_o = _grad_to_chips(pw, po)
    gw_t, go = _grad_halves_swap(_sum_slots(slots_w, core, IN_SHARD // 4, D_MODEL, "w_in_grad_sum"),
                                 _sum_slots(slots_o, core, OUT_SHARD // 2, D_MODEL, "w_out_grad_sum"))

    parts = dict(dng=dng, dshift=dshift, dscale=dscale, dgate=dgate, dfg=dfg, dlg=dlg, dlb=dlb, dsinks=dsinks,
                 dsgu_b=dsgu_b, dsgu_w=dsgu_w, loss=loss)
    small = dict(norm_g=(norm_g, m_norm_g, v_norm_g), b_ada=(b_ada, m_b_ada, v_b_ada),
                 attn_sinks=(attn_sinks, m_attn_sinks, v_attn_sinks), sgu_ln_g=(sgu_ln_g, m_sgu_ln_g, v_sgu_ln_g),
                 sgu_ln_b=(sgu_ln_b, m_sgu_ln_b, v_sgu_ln_b), sgu_w=(sgu_w, m_sgu_w, v_sgu_w), sgu_b=(sgu_b, m_sgu_b, v_sgu_b),
                 final_g=(final_g, m_final_g, v_final_g))
    res, dmod_all, total = _small_grads_step(parts, small)
    dmod_mine = lax.dynamic_slice_in_dim(dmod_all, ADA_SHARD * chip, ADA_SHARD, axis=1)

    res["w_ada"] = _adamw_ada(cact, dmod_mine, w_ada[0], m_w_ada[0], v_w_ada[0])
    res["w_in"] = tuple(jnp.swapaxes(a, 0, 1) for a in _adamw_rows(w_in_t, gw_t, m_in_t, v_in_t, IN_SHARD // 4, "adamw_w_in"))
    res["w_out"] = _adamw_rows(w_out[0], go, m_w_out[0], v_w_out[0], 256, "adamw_w_out")

    order = ["norm_g", "w_ada", "b_ada", "w_in", "attn_sinks", "sgu_ln_g", "sgu_ln_b", "sgu_w", "sgu_b", "w_out", "final_g"]
    like = dict(norm_g=norm_g, w_ada=w_ada, b_ada=b_ada, w_in=w_in, attn_sinks=attn_sinks, sgu_ln_g=sgu_ln_g, sgu_ln_b=sgu_ln_b,
                sgu_w=sgu_w, sgu_b=sgu_b, w_out=w_out, final_g=final_g)
    outs = [total[0, 0], gx[None]]
    for kind in range(4):
        outs += [res[n][kind].reshape(like[n].shape) for n in order]
    return tuple(outs)
```

```python
import functools
import math

import jax
import jax.numpy as jnp
from jax import lax
from jax.experimental import pallas as pl
from jax.experimental.pallas import tpu as pltpu

F32 = jnp.float32
BF16 = jnp.bfloat16

D_MODEL = 2048
D_ATTN = 1024
HEAD_DIM = 64
N_Q_HEADS = 16
Q_PER_KV = 8
D_KV = 128
BLOCK = 128
D_SGU = 1024
SGU_GROUPS = 8
GROUP_DIM = 128
D_IN = 5376
EPS = 1e-6
ATTN_SCALE = 1.0 / math.sqrt(HEAD_DIM)
N_CHIPS = 4
N_DEV = 8
IN_SHARD = D_IN // N_CHIPS
OUT_SHARD = D_MODEL // N_CHIPS
ADA_SHARD = 3 * D_MODEL // N_CHIPS
HALF = D_MODEL // 2

Q0, K0, V0, GA0, U0, VS0, GS0 = 0, 1024, 1152, 1280, 2304, 3328, 4352

ADAM_LR = 0.001
ADAM_B1 = 0.9
ADAM_B2 = 0.999
ADAM_EPS = 1e-08
ADAM_WD = 0.01
ADAM_STEP = 10

VMEM_LIMIT_V7X = 56 * 1024 * 1024

NN = (((1,), (0,)), ((), ()))
NT = (((1,), (1,)), ((), ()))
TN = (((0,), (0,)), ((), ()))
MESH = pl.DeviceIdType.MESH
ANY = pl.BlockSpec(memory_space=pl.ANY)
VMEM = pl.BlockSpec(memory_space=pltpu.VMEM)
HBM = pl.BlockSpec(memory_space=pltpu.HBM)
SEM = pl.BlockSpec(memory_space=pltpu.SEMAPHORE)
EFFECT = pltpu.SideEffectType.DATAFLOW_SIDE_EFFECTING


def _dot(a, b, dims):
    return lax.dot_general(a, b, dims, preferred_element_type=F32)


def _params(sem=None):
    return pltpu.CompilerParams(dimension_semantics=sem, vmem_limit_bytes=VMEM_LIMIT_V7X)


def _sigmoid(x):
    return jax.nn.sigmoid(x)


def _adamw(w, g, m, v):
    m = ADAM_B1 * m + (1.0 - ADAM_B1) * g
    v = ADAM_B2 * v + (1.0 - ADAM_B2) * (g * g)
    m_hat = m / (1.0 - ADAM_B1 ** ADAM_STEP)
    v_hat = v / (1.0 - ADAM_B2 ** ADAM_STEP)
    delta = -ADAM_LR * (m_hat / (jnp.sqrt(v_hat) + ADAM_EPS) + ADAM_WD * w)
    return delta, m, v


def _place():
    return lax.axis_index("x"), lax.axis_index("y"), lax.axis_index("c")


def _flip(p, bit):
    return 1 - p if bit else p


def _mod_exchange(c, w_ada, b_ada):
    kdim = w_ada.shape[0]
    kc = 256

    def body(c_ref, w_ref, b_ref, mod_ref, cact_ref, cbuf, pbuf, rbuf, send_sems, recv_sems):
        x, y, cc = _place()
        me = 4 * x + 2 * y + cc
        chip = 2 * x + y
        cbuf[me] = c_ref[...]
        gathers = []
        for k in range(1, N_DEV):
            bx, by, bc = (k >> 2) & 1, (k >> 1) & 1, k & 1
            cp = pltpu.make_async_remote_copy(
                src_ref=cbuf.at[me], dst_ref=cbuf.at[me], send_sem=send_sems.at[k - 1], recv_sem=recv_sems.at[k - 1],
                device_id=(_flip(x, bx), _flip(y, by), _flip(cc, bc)), device_id_type=MESH)
            cp.start()
            gathers.append(cp)
        for cp in gathers:
            cp.wait()
        for b in range(N_DEV):
            cv = cbuf[b]
            cact_ref[b:b + 1, :] = cv * _sigmoid(cv)
        acc = jnp.zeros((N_DEV, ADA_SHARD), F32)
        for k0 in range(0, kdim, kc):
            acc = acc + _dot(cact_ref[:, k0:k0 + kc].astype(BF16), w_ref[k0:k0 + kc, :].astype(BF16), NN)
        for b in range(N_DEV):
            pbuf[b] = acc[b:b + 1, :]
        rbuf[chip] = pbuf[me]
        sends = []
        for j, (bx, by) in enumerate(((0, 1), (1, 0), (1, 1))):
            px, py = _flip(x, bx), _flip(y, by)
            cp = pltpu.make_async_remote_copy(
                src_ref=pbuf.at[4 * px + 2 * py + cc], dst_ref=rbuf.at[chip],
                send_sem=send_sems.at[N_DEV - 1 + j], recv_sem=recv_sems.at[N_DEV - 1 + j],
                device_id=(px, py, cc), device_id_type=MESH)
            cp.start()
            sends.append(cp)
        for cp in sends:
            cp.wait()
        for k in range(N_CHIPS):
            cols = slice(ADA_SHARD * k, ADA_SHARD * (k + 1))
            mod_ref[:, cols] = rbuf[k] + b_ref[:, cols]

    return pl.pallas_call(
        body, name="mod_exchange",
        out_shape=(jax.ShapeDtypeStruct((1, 3 * D_MODEL), F32), jax.ShapeDtypeStruct((N_DEV, D_MODEL), F32)),
        in_specs=[VMEM, VMEM, VMEM], out_specs=(VMEM, VMEM),
        scratch_shapes=[pltpu.VMEM((N_DEV, 1, D_MODEL), F32), pltpu.VMEM((N_DEV, 1, ADA_SHARD), F32),
                        pltpu.VMEM((N_CHIPS, 1, ADA_SHARD), F32),
                        pltpu.SemaphoreType.DMA((N_DEV + 2,)), pltpu.SemaphoreType.DMA((N_DEV + 2,))],
        compiler_params=_params(),
    )(c, w_ada, b_ada)


def _weight_gather(win_t, wout):
    hi, ho = IN_SHARD // 2, OUT_SHARD // 2

    def body(wi_ref, wo_ref, fi_ref, fo_ref, loc_sems, send_sems, recv_sems, fsend_sems, frecv_sems):
        x, y, cc = _place()
        chip = 2 * x + y
        others = [(_flip(x, bx), _flip(y, by)) for bx, by in ((0, 1), (1, 0), (1, 1))]

        def rows(ref, shard, half, k, c_half):
            return ref.at[pl.ds(shard * k + half * c_half, half), :]

        own = [pltpu.make_async_copy(wi_ref, fi_ref.at[pl.ds(IN_SHARD * chip, IN_SHARD), :], loc_sems.at[0]),
               pltpu.make_async_copy(wo_ref, fo_ref.at[pl.ds(OUT_SHARD * chip, OUT_SHARD), :], loc_sems.at[1])]
        for cp in own:
            cp.start()
        sends = []
        for j, (px, py) in enumerate(others):
            for a, (src, dst, shard, half) in enumerate(((wi_ref, fi_ref, IN_SHARD, hi), (wo_ref, fo_ref, OUT_SHARD, ho))):
                cp = pltpu.make_async_remote_copy(
                    src_ref=src.at[pl.ds(half * cc, half), :], dst_ref=rows(dst, shard, half, chip, cc),
                    send_sem=send_sems.at[2 * j + a], recv_sem=recv_sems.at[2 * j + a],
                    device_id=(px, py, cc), device_id_type=MESH)
                cp.start()
                sends.append(cp)
        forwards = []
        for j, (px, py) in enumerate(others):
            pchip = 2 * px + py
            for a, (dst, shard, half) in enumerate(((fi_ref, IN_SHARD, hi), (fo_ref, OUT_SHARD, ho))):
                landed = rows(dst, shard, half, pchip, cc)
                pltpu.make_async_remote_copy(
                    src_ref=landed, dst_ref=landed, send_sem=send_sems.at[2 * j + a], recv_sem=recv_sems.at[2 * j + a],
                    device_id=(px, py, cc), device_id_type=MESH).wait_recv()
                cp = pltpu.make_async_remote_copy(
                    src_ref=landed, dst_ref=landed, send_sem=fsend_sems.at[2 * j + a], recv_sem=frecv_sems.at[2 * j + a],
                    device_id=(x, y, 1 - cc), device_id_type=MESH)
                cp.start()
                forwards.append(cp)
        for j, (px, py) in enumerate(others):
            pchip = 2 * px + py
            for a, (dst, shard, half) in enumerate(((fi_ref, IN_SHARD, hi), (fo_ref, OUT_SHARD, ho))):
                theirs = rows(dst, shard, half, pchip, 1 - cc)
                pltpu.make_async_remote_copy(
                    src_ref=theirs, dst_ref=theirs, send_sem=fsend_sems.at[2 * j + a], recv_sem=frecv_sems.at[2 * j + a],
                    device_id=(x, y, 1 - cc), device_id_type=MESH).wait_recv()
        for cp in sends + forwards:
            cp.wait_send()
        for cp in own:
            cp.wait()

    return pl.pallas_call(
        body, name="weight_gather",
        out_shape=(jax.ShapeDtypeStruct((D_IN, D_MODEL), BF16), jax.ShapeDtypeStruct((D_MODEL, D_MODEL), BF16)),
        in_specs=[ANY, ANY], out_specs=(ANY, ANY),
        scratch_shapes=[pltpu.SemaphoreType.DMA((2,))] + [pltpu.SemaphoreType.DMA((6,))] * 4,
        compiler_params=_params(),
    )(win_t, wout)


def _grad_to_sibling(gw, go):
    def body(gw_ref, go_ref, rw_ref, ro_ref, send_sems, recv_sems):
        x, y, cc = _place()
        cps = []
        for a, (src, dst) in enumerate(((gw_ref, rw_ref), (go_ref, ro_ref))):
            cp = pltpu.make_async_remote_copy(src_ref=src, dst_ref=dst, send_sem=send_sems.at[a], recv_sem=recv_sems.at[a],
                                              device_id=(x, y, 1 - cc), device_id_type=MESH)
            cp.start()
            cps.append(cp)
        for cp in cps:
            cp.wait()

    return pl.pallas_call(
        body, name="grad_to_sibling",
        out_shape=(jax.ShapeDtypeStruct(gw.shape, gw.dtype), jax.ShapeDtypeStruct(go.shape, go.dtype)),
        in_specs=[ANY, ANY], out_specs=(ANY, ANY),
        scratch_shapes=[pltpu.SemaphoreType.DMA((2,)), pltpu.SemaphoreType.DMA((2,))],
        compiler_params=_params(),
    )(gw, go)


def _chip_copies(pw_ref, po_ref, lw_ref, lo_ref, send_sems, recv_sems):
    x, y, cc = _place()
    chip = 2 * x + y
    cps = []
    for j, (bx, by) in enumerate(((0, 1), (1, 0), (1, 1))):
        px, py = _flip(x, bx), _flip(y, by)
        for a, (src, dst, shard) in enumerate(((pw_ref, lw_ref, IN_SHARD), (po_ref, lo_ref, OUT_SHARD))):
            cps.append(pltpu.make_async_remote_copy(
                src_ref=src.at[pl.ds(shard * (2 * px + py), shard), :], dst_ref=dst.at[chip],
                send_sem=send_sems.at[2 * j + a], recv_sem=recv_sems.at[2 * j + a],
                device_id=(px, py, cc), device_id_type=MESH))
    return cps


def _grad_to_chips_start(pw, po):
    def body(pw_ref, po_ref, lw_ref, lo_ref, send_sems, recv_sems, pw_thru, po_thru, lw_thru, lo_thru, token):
        for cp in _chip_copies(pw_ref, po_ref, lw_ref, lo_ref, send_sems, recv_sems):
            cp.start()
        token[...] = jnp.zeros_like(token)

    lw = lax.empty((N_CHIPS, IN_SHARD, HALF), BF16)
    lo = lax.empty((N_CHIPS, OUT_SHARD, HALF), BF16)
    bufs = [pltpu.with_memory_space_constraint(a, pltpu.HBM) for a in (pw, po, lw, lo)]
    return pl.pallas_call(
        body, name="grad_to_chips_start",
        out_shape=(pltpu.SemaphoreType.DMA((6,)), pltpu.SemaphoreType.DMA((6,)))
        + tuple(pltpu.HBM(a.shape, a.dtype) for a in bufs) + (jax.ShapeDtypeStruct((8, 128), F32),),
        in_specs=[HBM] * 4, out_specs=(SEM, SEM, HBM, HBM, HBM, HBM, VMEM), input_output_aliases={0: 2, 1: 3, 2: 4, 3: 5},
        compiler_params=pltpu.CompilerParams(has_side_effects=EFFECT),
    )(*bufs)


def _grad_to_chips_wait(send_sems, recv_sems, pw, po, lw, lo, after):
    def body(pw_ref, po_ref, lw_ref, lo_ref, send_sems, recv_sems, after_ref, pw_out, po_out, lw_out, lo_out):
        for cp in _chip_copies(pw_ref, po_ref, lw_ref, lo_ref, send_sems, recv_sems):
            cp.wait_send()
            cp.wait_recv()

    outs = pl.pallas_call(
        body, name="grad_to_chips_wait", out_shape=tuple(pltpu.HBM(a.shape, a.dtype) for a in (pw, po, lw, lo)),
        in_specs=[HBM, HBM, HBM, HBM, SEM, SEM, ANY], out_specs=(HBM,) * 4, input_output_aliases={0: 0, 1: 1, 2: 2, 3: 3},
        compiler_params=pltpu.CompilerParams(has_side_effects=EFFECT),
    )(pw, po, lw, lo, send_sems, recv_sems, after)
    return outs


def _grad_halves_swap(fw, fo):
    def body(fw_in, fo_in, fw_ref, fo_ref, send_sems, recv_sems):
        x, y, cc = _place()
        cps = []
        for a, ref in enumerate((fw_ref, fo_ref)):
            mine = ref.at[:, pl.ds(pl.multiple_of(HALF * cc, 128), HALF)]
            cp = pltpu.make_async_remote_copy(src_ref=mine, dst_ref=mine, send_sem=send_sems.at[a], recv_sem=recv_sems.at[a],
                                              device_id=(x, y, 1 - cc), device_id_type=MESH)
            cp.start()
            cps.append(cp)
        for cp in cps:
            cp.wait()

    return pl.pallas_call(
        body, name="grad_halves_swap",
        out_shape=(jax.ShapeDtypeStruct(fw.shape, fw.dtype), jax.ShapeDtypeStruct(fo.shape, fo.dtype)),
        in_specs=[ANY, ANY], out_specs=(ANY, ANY), input_output_aliases={0: 0, 1: 1},
        scratch_shapes=[pltpu.SemaphoreType.DMA((2,)), pltpu.SemaphoreType.DMA((2,))],
        compiler_params=_params(),
    )(fw, fo)


ROW_NORM_G, ROW_DMOD, ROW_FINAL_G, ROW_LN, ROW_SINKS, ROW_LOSS, ROW_SGU_B, MISC_ROWS = 0, 1, 4, 5, 6, 7, 8, 16


def _small_grads_step(parts, small):
    names = ["norm_g", "b_ada", "attn_sinks", "sgu_ln_g", "sgu_ln_b", "sgu_w", "sgu_b", "final_g"]
    part_names = ["dng", "dshift", "dscale", "dgate", "dfg", "dlg", "dlb", "dsinks", "dsgu_b", "dsgu_w", "loss"]
    n_parts = len(part_names)

    def body(*refs):
        p = dict(zip(part_names, refs[:n_parts]))
        wmv = {n: refs[n_parts + 3 * i: n_parts + 3 * i + 3] for i, n in enumerate(names)}
        outs = refs[n_parts + 3 * len(names):]
        res = {n: outs[4 * i: 4 * i + 4] for i, n in enumerate(names)}
        dmod_ref, loss_ref = outs[4 * len(names):4 * len(names) + 2]
        misc, wbuf, cmisc, cwbuf, dbuf, send_sems, recv_sems = outs[4 * len(names) + 2:]
        x, y, cc = _place()
        me = 4 * x + 2 * y + cc
        chip = 2 * x + y

        def remote(k, src, dst, peer):
            return pltpu.make_async_remote_copy(src_ref=src, dst_ref=dst, send_sem=send_sems.at[k], recv_sem=recv_sems.at[k],
                                                device_id=peer, device_id_type=MESH)

        dbuf[me] = jnp.zeros((8, D_MODEL), F32)
        dbuf[me, 0:1, :] = p["dshift"][...]
        dbuf[me, 1:2, :] = p["dscale"][...]
        dbuf[me, 2:3, :] = p["dgate"][...]
        cps = []
        for k in range(1, N_DEV):
            bx, by, bc = (k >> 2) & 1, (k >> 1) & 1, k & 1
            cps.append(remote(k - 1, dbuf.at[me], dbuf.at[me], (_flip(x, bx), _flip(y, by), _flip(cc, bc))))

        misc[cc] = jnp.zeros((MISC_ROWS, D_MODEL), F32)
        misc[cc, ROW_NORM_G:ROW_NORM_G + 1, :] = p["dng"][...]
        misc[cc, ROW_DMOD:ROW_DMOD + 1, :] = p["dshift"][...]
        misc[cc, ROW_DMOD + 1:ROW_DMOD + 2, :] = p["dscale"][...]
        misc[cc, ROW_DMOD + 2:ROW_DMOD + 3, :] = p["dgate"][...]
        misc[cc, ROW_FINAL_G:ROW_FINAL_G + 1, :] = p["dfg"][...]
        misc[cc, ROW_LN:ROW_LN + 1, 0:D_SGU] = p["dlg"][...]
        misc[cc, ROW_LN:ROW_LN + 1, D_SGU:2 * D_SGU] = p["dlb"][...]
        misc[cc, ROW_SINKS:ROW_SINKS + 1, 0:N_Q_HEADS] = p["dsinks"][...]
        misc[cc, ROW_LOSS:ROW_LOSS + 1, 0:1] = p["loss"][...]
        misc[cc, ROW_SGU_B:ROW_SGU_B + SGU_GROUPS, 0:BLOCK] = p["dsgu_b"][...]
        for g in range(SGU_GROUPS):
            wbuf[cc, BLOCK * g:BLOCK * (g + 1), :] = p["dsgu_w"][g]
        pair = [remote(N_DEV - 1 + a, buf.at[cc], buf.at[cc], (x, y, 1 - cc)) for a, buf in enumerate((misc, wbuf))]
        for cp in cps + pair:
            cp.start()
        for cp in pair:
            cp.wait()
        cmisc[chip] = misc[0] + misc[1]
        cwbuf[chip] = wbuf[0] + wbuf[1]
        quad = []
        for j, (bx, by) in enumerate(((0, 1), (1, 0), (1, 1))):
            for a, buf in enumerate((cmisc, cwbuf)):
                quad.append(remote(N_DEV + 1 + 2 * j + a, buf.at[chip], buf.at[chip], (_flip(x, bx), _flip(y, by), cc)))
        for cp in quad:
            cp.start()
        for cp in cps + quad:
            cp.wait()

        tot = cmisc[0]
        for k in range(1, N_CHIPS):
            tot = tot + cmisc[k]
        loss_ref[...] = tot[ROW_LOSS:ROW_LOSS + 1, 0:1]
        for b in range(N_DEV):
            dmod_ref[b:b + 1, :] = jnp.concatenate([dbuf[b, t:t + 1, :] for t in range(3)], axis=1)

        def update(name, g, pick=lambda r: r[...], put=None):
            w_ref, m_ref, v_ref = wmv[name]
            delta, m2, v2 = _adamw(pick(w_ref), g, pick(m_ref), pick(v_ref))
            for ref, val in zip(res[name], (g, delta, m2, v2)):
                if put is None:
                    ref[...] = val
                else:
                    put(ref, val)

        update("norm_g", tot[ROW_NORM_G:ROW_NORM_G + 1, :])
        update("final_g", tot[ROW_FINAL_G:ROW_FINAL_G + 1, :])
        for t in range(3):
            cols = slice(D_MODEL * t, D_MODEL * (t + 1))

            def put_cols(ref, val, cols=cols):
                ref[:, cols] = val
            update("b_ada", tot[ROW_DMOD + t:ROW_DMOD + t + 1, :], pick=lambda r, cols=cols: r[:, cols], put=put_cols)
        update("sgu_ln_g", tot[ROW_LN:ROW_LN + 1, 0:D_SGU])
        update("sgu_ln_b", tot[ROW_LN:ROW_LN + 1, D_SGU:2 * D_SGU])
        update("attn_sinks", tot[ROW_SINKS:ROW_SINKS + 1, 0:N_Q_HEADS])
        update("sgu_b", tot[ROW_SGU_B:ROW_SGU_B + SGU_GROUPS, 0:BLOCK])
        for g in range(SGU_GROUPS):
            rows = slice(BLOCK * g, BLOCK * (g + 1))
            gw = cwbuf[0, rows, :]
            for k in range(1, N_CHIPS):
                gw = gw + cwbuf[k, rows, :]

            def put_group(ref, val, g=g):
                ref[g] = val
            update("sgu_w", gw, pick=lambda r, g=g: r[g], put=put_group)

    shapes = {"norm_g": (1, D_MODEL), "b_ada": (1, 3 * D_MODEL), "attn_sinks": (1, N_Q_HEADS), "sgu_ln_g": (1, D_SGU),
              "sgu_ln_b": (1, D_SGU), "sgu_w": (SGU_GROUPS, BLOCK, BLOCK), "sgu_b": (SGU_GROUPS, BLOCK), "final_g": (1, D_MODEL)}
    operands = [parts[n] for n in part_names]
    for n in names:
        operands += [a.reshape(shapes[n]) for a in small[n]]
    out_shape = []
    for n in names:
        out_shape += [jax.ShapeDtypeStruct(shapes[n], F32)] * 4
    out_shape += [jax.ShapeDtypeStruct((N_DEV, 3 * D_MODEL), F32), jax.ShapeDtypeStruct((1, 1), F32)]
    n_sems = (N_DEV - 1) + 2 + 2 * (N_CHIPS - 1)
    outs = pl.pallas_call(
        body, name="small_grads_step", out_shape=tuple(out_shape),
        in_specs=[VMEM] * len(operands), out_specs=tuple([VMEM] * len(out_shape)),
        scratch_shapes=[pltpu.VMEM((2, MISC_ROWS, D_MODEL), F32), pltpu.VMEM((2, SGU_GROUPS * BLOCK, BLOCK), F32),
                        pltpu.VMEM((N_CHIPS, MISC_ROWS, D_MODEL), F32), pltpu.VMEM((N_CHIPS, SGU_GROUPS * BLOCK, BLOCK), F32),
                        pltpu.VMEM((N_DEV, 8, D_MODEL), F32),
                        pltpu.SemaphoreType.DMA((n_sems,)), pltpu.SemaphoreType.DMA((n_sems,))],
        compiler_params=_params(),
    )(*operands)
    res = {n: outs[4 * i: 4 * i + 4] for i, n in enumerate(names)}
    return res, outs[4 * len(names)], outs[4 * len(names) + 1]


def _cast_bf16(w, tile):
    r, c = w.shape

    def body(w_ref, o_ref):
        o_ref[...] = w_ref[...].astype(BF16)

    return pl.pallas_call(
        body, name="cast_bf16", grid=(r // tile,), out_shape=jax.ShapeDtypeStruct((r, c), BF16),
        in_specs=[pl.BlockSpec((tile, c), lambda i: (i, 0))], out_specs=pl.BlockSpec((tile, c), lambda i: (i, 0)),
        compiler_params=_params(("parallel",)),
    )(w)


def _norm_modulate(x, norm_g, mod):
    s, d = x.shape
    tm = min(256, s)

    def body(x_ref, g_ref, mod_ref, h_ref):
        xv = x_ref[...]
        r = lax.rsqrt(jnp.mean(xv * xv, axis=-1, keepdims=True) + EPS)
        h = (xv * r * g_ref[...]) * (1.0 + mod_ref[:, d:2 * d]) + mod_ref[:, 0:d]
        h_ref[...] = h.astype(BF16)

    return pl.pallas_call(
        body, name="norm_modulate", grid=(s // tm,), out_shape=jax.ShapeDtypeStruct((s, d), BF16),
        in_specs=[pl.BlockSpec((tm, d), lambda i: (i, 0)), pl.BlockSpec((1, d), lambda i: (0, 0)),
                  pl.BlockSpec((1, 3 * d), lambda i: (0, 0))],
        out_specs=pl.BlockSpec((tm, d), lambda i: (i, 0)),
        compiler_params=_params(("parallel",)),
    )(x, norm_g, mod)


def _matmul_nt(a, bt, tm, tn, out_dtype, name):
    m, k = a.shape
    n = bt.shape[0]
    tm, tn = min(tm, m), min(tn, n)

    def body(a_ref, b_ref, o_ref):
        o_ref[...] = _dot(a_ref[...], b_ref[...], NT).astype(out_dtype)

    return pl.pallas_call(
        body, name=name, grid=(m // tm, n // tn), out_shape=jax.ShapeDtypeStruct((m, n), out_dtype),
        in_specs=[pl.BlockSpec((tm, k), lambda i, j: (i, 0)), pl.BlockSpec((tn, k), lambda i, j: (j, 0))],
        out_specs=pl.BlockSpec((tm, tn), lambda i, j: (i, j)),
        compiler_params=_params(("parallel", "parallel")),
    )(a, bt)


def _grad_matmul(a, b, half, tm, name, add=None):
    t, m = a.shape
    tm = min(tm, m)

    def body(half_ref, a_ref, b_ref, *rest):
        if add is None:
            (o_ref,) = rest
            o_ref[...] = _dot(a_ref[...], b_ref[...], TN).astype(BF16)
        else:
            add_ref, o_ref = rest
            o_ref[...] = (_dot(a_ref[...], b_ref[...], TN) + add_ref[...].astype(F32)).astype(BF16)

    in_specs = [pl.BlockSpec((t, tm), lambda i, h: (0, i)), pl.BlockSpec((t, HALF), lambda i, h: (0, h[0]))]
    operands = [a, b]
    if add is not None:
        in_specs.append(pl.BlockSpec((tm, HALF), lambda i, h: (i, 0)))
        operands.append(add)
    return pl.pallas_call(
        body, name=name,
        grid_spec=pltpu.PrefetchScalarGridSpec(num_scalar_prefetch=1, grid=(m // tm,), in_specs=in_specs,
                                               out_specs=pl.BlockSpec((tm, HALF), lambda i, h: (i, 0))),
        out_shape=jax.ShapeDtypeStruct((m, HALF), BF16),
        compiler_params=_params(("parallel",)),
    )(half, *operands)


def _mixer_masks(n):
    row = lax.broadcasted_iota(jnp.int32, (Q_PER_KV * BLOCK, 2 * BLOCK), 0) & (BLOCK - 1)
    col = lax.broadcasted_iota(jnp.int32, (Q_PER_KV * BLOCK, 2 * BLOCK), 1)
    valid = (col > row) & (col <= row + BLOCK) & ((col >= BLOCK) | (n > 0))
    r2 = lax.broadcasted_iota(jnp.int32, (BLOCK, BLOCK), 0)
    c2 = lax.broadcasted_iota(jnp.int32, (BLOCK, BLOCK), 1)
    return valid, r2 >= c2


def _layer_norm_parts(vs):
    mu = jnp.mean(vs, axis=-1, keepdims=True)
    xc = vs - mu
    rstd = lax.rsqrt(jnp.mean(xc * xc, axis=-1, keepdims=True) + EPS)
    return xc * rstd, rstd


def _stack_heads(ref, base, hk):
    first = base + HEAD_DIM * Q_PER_KV * hk
    return jnp.concatenate([ref[:, first + HEAD_DIM * g:first + HEAD_DIM * (g + 1)] for g in range(Q_PER_KV)], axis=0)


def _unstack_heads(ref, base, hk, val):
    first = base + HEAD_DIM * Q_PER_KV * hk
    for g in range(Q_PER_KV):
        ref[:, first + HEAD_DIM * g:first + HEAD_DIM * (g + 1)] = val[BLOCK * g:BLOCK * (g + 1), :].astype(ref.dtype)


def _stack_sinks(sink_ref, hk):
    return jnp.concatenate([jnp.broadcast_to(sink_ref[:, h:h + 1], (BLOCK, 1))
                            for h in range(Q_PER_KV * hk, Q_PER_KV * (hk + 1))], axis=0)


def _attn_probs(qs, kh, valid, sk):
    s = _dot(qs, kh, NT) * ATTN_SCALE
    s = jnp.where(valid, s, -jnp.inf)
    m = jnp.maximum(jnp.max(s, axis=-1, keepdims=True), sk)
    e = jnp.exp(s - m)
    es = jnp.exp(sk - m)
    inv = 1.0 / (jnp.sum(e, axis=-1, keepdims=True) + es)
    return e * inv, es * inv


def _kv_ext(z_ref, kvp_ref):
    k_ext = jnp.concatenate([kvp_ref[:, 0:D_KV], z_ref[:, K0:K0 + D_KV]], axis=0).astype(BF16)
    v_ext = jnp.concatenate([kvp_ref[:, D_KV:2 * D_KV], z_ref[:, V0:V0 + D_KV]], axis=0).astype(BF16)
    return k_ext, v_ext


def _sgu_mixed(w_ref, bt_ref, vn, tril):
    ws = [jnp.where(tril, w_ref[g], 0.0).astype(BF16) for g in range(SGU_GROUPS)]
    mixed = jnp.concatenate([_dot(ws[g], vn[:, GROUP_DIM * g:GROUP_DIM * (g + 1)], NN) + bt_ref[:, g:g + 1]
                             for g in range(SGU_GROUPS)], axis=1)
    return ws, mixed


def _mixer_fwd(z, sinks, ln_g, ln_b, sgu_w, sgu_bt):
    s = z.shape[0]
    nb = s // BLOCK

    def body(z_ref, kvp_ref, sink_ref, lg_ref, lb_ref, w_ref, bt_ref, cat_ref, o_ref):
        n = pl.program_id(0)
        valid, tril = _mixer_masks(n)
        k_ext, v_ext = _kv_ext(z_ref, kvp_ref)
        for hk in range(2):
            kv = slice(HEAD_DIM * hk, HEAD_DIM * (hk + 1))
            p, _ = _attn_probs(_stack_heads(z_ref, Q0, hk).astype(BF16), k_ext[:, kv], valid, _stack_sinks(sink_ref, hk))
            o = _dot(p.astype(BF16), v_ext[:, kv], NN)
            ga = _stack_heads(z_ref, GA0, hk)
            _unstack_heads(o_ref, 0, hk, o)
            _unstack_heads(cat_ref, 0, hk, o * (ga * _sigmoid(ga)))
        xhat, _ = _layer_norm_parts(z_ref[:, VS0:VS0 + D_SGU])
        vn = (xhat * lg_ref[...] + lb_ref[...]).astype(BF16)
        _, mixed = _sgu_mixed(w_ref, bt_ref, vn, tril)
        gt = z_ref[:, GS0:GS0 + D_SGU]
        cat_ref[:, D_ATTN:D_MODEL] = ((z_ref[:, U0:U0 + D_SGU] * mixed) * (gt * _sigmoid(gt))).astype(BF16)

    const2 = lambda n: (0, 0)
    return pl.pallas_call(
        body, name="mixer_fwd", grid=(nb,),
        out_shape=(jax.ShapeDtypeStruct((s, D_MODEL), BF16), jax.ShapeDtypeStruct((s, D_ATTN), F32)),
        in_specs=[pl.BlockSpec((BLOCK, D_IN), lambda n: (n, 0)),
                  pl.BlockSpec((BLOCK, 2 * D_KV), lambda n: (jnp.maximum(n - 1, 0), K0 // (2 * D_KV))),
                  pl.BlockSpec((1, N_Q_HEADS), const2), pl.BlockSpec((1, D_SGU), const2), pl.BlockSpec((1, D_SGU), const2),
                  pl.BlockSpec((SGU_GROUPS, BLOCK, BLOCK), lambda n: (0, 0, 0)), pl.BlockSpec((BLOCK, SGU_GROUPS), const2)],
        out_specs=(pl.BlockSpec((BLOCK, D_MODEL), lambda n: (n, 0)), pl.BlockSpec((BLOCK, D_ATTN), lambda n: (n, 0))),
        compiler_params=_params(("parallel",)),
    )(z, z, sinks, ln_g, ln_b, sgu_w, sgu_bt)


def _out_proj_loss(cat, wout, x, target, mod, final_g):
    s, d = x.shape
    tm = min(256, s)

    def body(cat_ref, w_ref, x_ref, t_ref, mod_ref, fg_ref, dx1_ref, dy_ref, dfg_ref, dgate_ref, loss_ref):
        i = pl.program_id(0)
        gate = mod_ref[:, 2 * d:3 * d]
        fg = fg_ref[...]
        yv = _dot(cat_ref[...], w_ref[...], NN)
        x1 = x_ref[...] + gate * yv
        r = lax.rsqrt(jnp.mean(x1 * x1, axis=-1, keepdims=True) + EPS)
        xr = x1 * r
        err = xr * fg - t_ref[...]
        dout = err * (1.0 / d)
        gy = dout * fg
        dx1 = r * gy - xr * (r * r * jnp.mean(gy * x1, axis=-1, keepdims=True))
        dx1_ref[...] = dx1
        dy_ref[...] = (gate * dx1).astype(BF16)

        @pl.when(i == 0)
        def _():
            dfg_ref[...] = jnp.zeros_like(dfg_ref)
            dgate_ref[...] = jnp.zeros_like(dgate_ref)
            loss_ref[...] = jnp.zeros_like(loss_ref)

        dfg_ref[...] += jnp.sum(dout * xr, axis=0, keepdims=True)
        dgate_ref[...] += jnp.sum(dx1 * yv, axis=0, keepdims=True)
        loss_ref[...] += 0.5 * jnp.sum(jnp.mean(err * err, axis=-1, keepdims=True), axis=0, keepdims=True)

    row = lambda i: (i, 0)
    const = lambda i: (0, 0)
    return pl.pallas_call(
        body, name="out_proj_loss", grid=(s // tm,),
        out_shape=(jax.ShapeDtypeStruct((s, d), F32), jax.ShapeDtypeStruct((s, d), BF16), jax.ShapeDtypeStruct((1, d), F32),
                   jax.ShapeDtypeStruct((1, d), F32), jax.ShapeDtypeStruct((1, 1), F32)),
        in_specs=[pl.BlockSpec((tm, d), row), pl.BlockSpec((d, d), const), pl.BlockSpec((tm, d), row), pl.BlockSpec((tm, d), row),
                  pl.BlockSpec((1, 3 * d), const), pl.BlockSpec((1, d), const)],
        out_specs=(pl.BlockSpec((tm, d), row), pl.BlockSpec((tm, d), row), pl.BlockSpec((1, d), const), pl.BlockSpec((1, d), const),
                   pl.BlockSpec((1, 1), const)),
        compiler_params=_params(("arbitrary",)),
    )(cat, wout, x, target, mod, final_g)


def _mixer_bwd(z, dcat, o, sinks, ln_g, ln_b, sgu_w, sgu_bt):
    s = z.shape[0]
    nb = s // BLOCK

    def body(z_ref, kvp_ref, dc_ref, o_ref, sink_ref, lg_ref, lb_ref, w_ref, bt_ref,
             dz_ref, dsink_ref, dlg_ref, dlb_ref, dw_ref, db_ref, pend, dkv, carry, dsink_acc, dbt_acc):
        n = pl.program_id(0)

        @pl.when(n == 0)
        def _():
            pend[...] = jnp.zeros_like(pend)
            carry[...] = jnp.zeros_like(carry)
            dsink_acc[...] = jnp.zeros_like(dsink_acc)
            dbt_acc[...] = jnp.zeros_like(dbt_acc)
            dlg_ref[...] = jnp.zeros_like(dlg_ref)
            dlb_ref[...] = jnp.zeros_like(dlb_ref)
            dw_ref[...] = jnp.zeros_like(dw_ref)

        dz_ref[:, 0:K0] = pend[:, 0:K0]
        dz_ref[:, GA0:D_IN] = pend[:, GA0:D_IN]

        @pl.when(n == nb)
        def _():
            dkv[...] = jnp.zeros_like(dkv)

        @pl.when(n < nb)
        def _():
            valid, tril = _mixer_masks(n)
            k_ext, v_ext = _kv_ext(z_ref, kvp_ref)
            for hk in range(2):
                kv = slice(HEAD_DIM * hk, HEAD_DIM * (hk + 1))
                kh, vh = k_ext[:, kv], v_ext[:, kv]
                qs = _stack_heads(z_ref, Q0, hk).astype(BF16)
                p, p_sink = _attn_probs(qs, kh, valid, _stack_sinks(sink_ref, hk))
                ga = _stack_heads(z_ref, GA0, hk)
                sg = _sigmoid(ga)
                datt = _stack_heads(dc_ref, 0, hk)
                do = (datt * (ga * sg)).astype(BF16)
                _unstack_heads(pend, GA0, hk, datt * _stack_heads(o_ref, 0, hk) * (sg * (1.0 + ga * (1.0 - sg))))
                dp = _dot(do, vh, NT)
                delta = jnp.sum(p * dp, axis=-1, keepdims=True)
                ds = (p * (dp - delta) * ATTN_SCALE).astype(BF16)
                dsk = -p_sink * delta
                for gq in range(Q_PER_KV):
                    h = hk * Q_PER_KV + gq
                    dsink_acc[:, h:h + 1] += jnp.sum(dsk[BLOCK * gq:BLOCK * (gq + 1), :], axis=0, keepdims=True)
                _unstack_heads(pend, Q0, hk, _dot(ds, kh, NN))
                dkv[:, kv] = _dot(ds, qs, TN)
                dkv[:, D_KV + HEAD_DIM * hk:D_KV + HEAD_DIM * (hk + 1)] = _dot(p.astype(BF16), do, TN)

            xhat, rstd = _layer_norm_parts(z_ref[:, VS0:VS0 + D_SGU])
            lg = lg_ref[...]
            vn = (xhat * lg + lb_ref[...]).astype(BF16)
            ws, mixed = _sgu_mixed(w_ref, bt_ref, vn, tril)
            u = z_ref[:, U0:U0 + D_SGU]
            gt = z_ref[:, GS0:GS0 + D_SGU]
            sg = _sigmoid(gt)
            dsgu = dc_ref[:, D_ATTN:D_MODEL]
            dgated = dsgu * (gt * sg)
            pend[:, U0:U0 + D_SGU] = (dgated * mixed).astype(BF16)
            pend[:, GS0:GS0 + D_SGU] = (dsgu * (u * mixed) * (sg * (1.0 + gt * (1.0 - sg)))).astype(BF16)
            dmixed = dgated * u
            dmb = dmixed.astype(BF16)
            dvn_parts = []
            for g in range(SGU_GROUPS):
                gs_ = slice(GROUP_DIM * g, GROUP_DIM * (g + 1))
                dw_ref[g] += jnp.where(tril, _dot(dmb[:, gs_], vn[:, gs_], NT), 0.0)
                dbt_acc[:, g:g + 1] += jnp.sum(dmixed[:, gs_], axis=-1, keepdims=True)
                dvn_parts.append(_dot(ws[g], dmb[:, gs_], TN))
            dvn = jnp.concatenate(dvn_parts, axis=1)
            dlg_ref[...] += jnp.sum(dvn * xhat, axis=0, keepdims=True)
            dlb_ref[...] += jnp.sum(dvn, axis=0, keepdims=True)
            dxh = dvn * lg
            pend[:, VS0:VS0 + D_SGU] = (rstd * (dxh - jnp.mean(dxh, axis=-1, keepdims=True)
                                                - xhat * jnp.mean(dxh * xhat, axis=-1, keepdims=True))).astype(BF16)

        dz_ref[:, K0:GA0] = (carry[...] + dkv[0:BLOCK, :]).astype(BF16)
        carry[...] = dkv[BLOCK:2 * BLOCK, :]

        @pl.when(n == nb)
        def _():
            dsink_ref[...] = dsink_acc[:, 0:N_Q_HEADS]
            db_ref[...] = dbt_acc[...].T[0:SGU_GROUPS, :]

    cur = lambda n: (jnp.minimum(n, nb - 1), 0)
    const2 = lambda n: (0, 0)
    return pl.pallas_call(
        body, name="mixer_bwd", grid=(nb + 1,),
        out_shape=(jax.ShapeDtypeStruct((s, D_IN), BF16), jax.ShapeDtypeStruct((1, N_Q_HEADS), F32),
                   jax.ShapeDtypeStruct((1, D_SGU), F32), jax.ShapeDtypeStruct((1, D_SGU), F32),
                   jax.ShapeDtypeStruct((SGU_GROUPS, BLOCK, BLOCK), F32), jax.ShapeDtypeStruct((SGU_GROUPS, BLOCK), F32)),
        in_specs=[pl.BlockSpec((BLOCK, D_IN), cur),
                  pl.BlockSpec((BLOCK, 2 * D_KV), lambda n: (jnp.maximum(jnp.minimum(n, nb - 1) - 1, 0), K0 // (2 * D_KV))),
                  pl.BlockSpec((BLOCK, D_MODEL), cur), pl.BlockSpec((BLOCK, D_ATTN), cur),
                  pl.BlockSpec((1, N_Q_HEADS), const2), pl.BlockSpec((1, D_SGU), const2), pl.BlockSpec((1, D_SGU), const2),
                  pl.BlockSpec((SGU_GROUPS, BLOCK, BLOCK), lambda n: (0, 0, 0)), pl.BlockSpec((BLOCK, SGU_GROUPS), const2)],
        out_specs=(pl.BlockSpec((BLOCK, D_IN), lambda n: (jnp.maximum(n - 1, 0), 0)),
                   pl.BlockSpec((1, N_Q_HEADS), const2), pl.BlockSpec((1, D_SGU), const2), pl.BlockSpec((1, D_SGU), const2),
                   pl.BlockSpec((SGU_GROUPS, BLOCK, BLOCK), lambda n: (0, 0, 0)), pl.BlockSpec((SGU_GROUPS, BLOCK), const2)),
        scratch_shapes=[pltpu.VMEM((BLOCK, D_IN), BF16), pltpu.VMEM((2 * BLOCK, 2 * D_KV), F32), pltpu.VMEM((BLOCK, 2 * D_KV), F32),
                        pltpu.VMEM((1, BLOCK), F32), pltpu.VMEM((BLOCK, BLOCK), F32)],
        compiler_params=_params(("arbitrary",)),
    )(z, z, dcat, o, sinks, ln_g, ln_b, sgu_w, sgu_bt)


def _in_proj_bwd(dz, win_t, x, dx1, norm_g, mod):
    s, d = x.shape
    kdim = dz.shape[1]
    tm, tk = min(512, s), 768
    nk = kdim // tk

    def body(dz_ref, w_ref, x_ref, dx1_ref, g_ref, mod_ref, gx_ref, dshift_ref, dscale_ref, dng_ref, acc):
        i, k = pl.program_id(0), pl.program_id(1)

        @pl.when(k == 0)
        def _():
            acc[...] = jnp.zeros_like(acc)

        acc[...] += _dot(dz_ref[...], w_ref[...], NN)

        @pl.when((i == 0) & (k == nk - 1))
        def _():
            dshift_ref[...] = jnp.zeros_like(dshift_ref)
            dscale_ref[...] = jnp.zeros_like(dscale_ref)
            dng_ref[...] = jnp.zeros_like(dng_ref)

        @pl.when(k == nk - 1)
        def _():
            dh = acc[...]
            xv = x_ref[...]
            ng = g_ref[...]
            r = lax.rsqrt(jnp.mean(xv * xv, axis=-1, keepdims=True) + EPS)
            xn = xv * r
            dshift_ref[...] += jnp.sum(dh, axis=0, keepdims=True)
            dscale_ref[...] += jnp.sum(dh * (xn * ng), axis=0, keepdims=True)
            dhs = dh * (1.0 + mod_ref[:, d:2 * d])
            dng_ref[...] += jnp.sum(dhs * xn, axis=0, keepdims=True)
            dxn = dhs * ng
            gx_ref[...] = dx1_ref[...] + r * dxn - xn * (r * r * jnp.mean(dxn * xv, axis=-1, keepdims=True))

    row = lambda i, k: (i, 0)
    const = lambda i, k: (0, 0)
    return pl.pallas_call(
        body, name="in_proj_bwd", grid=(s // tm, nk),
        out_shape=(jax.ShapeDtypeStruct((s, d), F32),) + (jax.ShapeDtypeStruct((1, d), F32),) * 3,
        in_specs=[pl.BlockSpec((tm, tk), lambda i, k: (i, k)), pl.BlockSpec((tk, d), lambda i, k: (k, 0)),
                  pl.BlockSpec((tm, d), row), pl.BlockSpec((tm, d), row), pl.BlockSpec((1, d), const), pl.BlockSpec((1, 3 * d), const)],
        out_specs=(pl.BlockSpec((tm, d), row),) + (pl.BlockSpec((1, d), const),) * 3,
        scratch_shapes=[pltpu.VMEM((tm, d), F32)],
        compiler_params=_params(("arbitrary", "arbitrary")),
    )(dz, win_t, x, dx1, norm_g, mod)


def _sum_slots(slots, own, place, rows_tile, full_cols, name):
    _, r, hcols = slots.shape
    nblk = r // rows_tile

    def body(place_ref, s_ref, own_ref, o_ref):
        chip = place_ref[1]
        acc = jnp.zeros((rows_tile, hcols), F32)
        for k in range(N_CHIPS):
            acc = acc + jnp.where(chip == k, own_ref[...], s_ref[k]).astype(F32)
        o_ref[...] = acc

    return pl.pallas_call(
        body, name=name,
        grid_spec=pltpu.PrefetchScalarGridSpec(
            num_scalar_prefetch=1, grid=(nblk,),
            in_specs=[pl.BlockSpec((N_CHIPS, rows_tile, hcols), lambda i, p: (0, i, 0)),
                      pl.BlockSpec((rows_tile, hcols), lambda i, p: (p[1] * nblk + i, 0))],
            out_specs=pl.BlockSpec((rows_tile, hcols), lambda i, p: (i, p[0]))),
        out_shape=jax.ShapeDtypeStruct((r, full_cols), F32),
        compiler_params=_params(("parallel",)),
    )(place, slots, own)


def _adamw_rows(w, g, m, v, tile, name):
    r, c = w.shape

    def body(w_ref, g_ref, m_ref, v_ref, go_ref, d_ref, mo_ref, vo_ref):
        g_ = g_ref[...]
        delta, m2, v2 = _adamw(w_ref[...], g_, m_ref[...], v_ref[...])
        go_ref[...] = g_
        d_ref[...] = delta
        mo_ref[...] = m2
        vo_ref[...] = v2

    blk = pl.BlockSpec((tile, c), lambda i: (i, 0))
    return pl.pallas_call(
        body, name=name, grid=(r // tile,), out_shape=(jax.ShapeDtypeStruct((r, c), F32),) * 4,
        in_specs=[blk] * 4, out_specs=(blk,) * 4,
        compiler_params=_params(("parallel",)),
    )(w, g, m, v)


def _adamw_ada(cact, dmod, w, m, v, tile=256):
    r, c = w.shape

    def body(c_ref, d_ref, w_ref, m_ref, v_ref, go_ref, dl_ref, mo_ref, vo_ref):
        g_ = _dot(c_ref[...].astype(BF16), d_ref[...].astype(BF16), TN)
        delta, m2, v2 = _adamw(w_ref[...], g_, m_ref[...], v_ref[...])
        go_ref[...] = g_
        dl_ref[...] = delta
        mo_ref[...] = m2
        vo_ref[...] = v2

    blk = pl.BlockSpec((tile, c), lambda i: (i, 0))
    return pl.pallas_call(
        body, name="adamw_ada", grid=(r // tile,), out_shape=(jax.ShapeDtypeStruct((r, c), F32),) * 4,
        in_specs=[pl.BlockSpec((N_DEV, tile), lambda i: (0, i)), pl.BlockSpec((N_DEV, c), lambda i: (0, 0)), blk, blk, blk],
        out_specs=(blk,) * 4,
        compiler_params=_params(("parallel",)),
    )(cact, dmod, w, m, v)


def kernel(x, c, norm_g, w_ada, b_ada, w_in, attn_sinks, sgu_ln_g, sgu_ln_b, sgu_w, sgu_b, w_out, final_g, loss_target, m_norm_g, m_w_ada, m_b_ada, m_w_in, m_attn_sinks, m_sgu_ln_g, m_sgu_ln_b, m_sgu_w, m_sgu_b, m_w_out, m_final_g, v_norm_g, v_w_ada, v_b_ada, v_w_in, v_attn_sinks, v_sgu_ln_g, v_sgu_ln_b, v_sgu_w, v_sgu_b, v_w_out, v_final_g):
    xs, tgt = x[0], loss_target[0]
    core = lax.axis_index("c").astype(jnp.int32).reshape(1)
    chip = 2 * lax.axis_index("x") + lax.axis_index("y")
    sgu_bt = sgu_b[0].T

    mod, cact = _mod_exchange(c, w_ada[0], b_ada)
    w_in_t, m_in_t, v_in_t = (jnp.swapaxes(a[0], 0, 1) for a in (w_in, m_w_in, v_w_in))
    win_t, wout = _weight_gather(_cast_bf16(w_in_t, IN_SHARD // 4), _cast_bf16(w_out[0], OUT_SHARD // 2))
    h = _norm_modulate(xs, norm_g, mod)
    z = _matmul_nt(h, win_t, 512, 768, F32, "in_proj")
    cat, o = _mixer_fwd(z, attn_sinks, sgu_ln_g, sgu_ln_b, sgu_w[0], sgu_bt)
    dx1, dy, dfg, dgate, loss = _out_proj_loss(cat, wout, xs, tgt, mod, final_g.reshape(1, D_MODEL))

    dcat = _matmul_nt(dy, wout, 512, 1024, F32, "out_proj_bwd")
    dz, dsinks, dlg, dlb, dsgu_w, dsgu_b = _mixer_bwd(z, dcat, o, attn_sinks, sgu_ln_g, sgu_ln_b, sgu_w[0], sgu_bt)

    sw, so = _grad_to_sibling(_grad_matmul(dz, h, 1 - core, 768, "w_in_grad_sibling"),
                              _grad_matmul(cat, dy, 1 - core, 512, "w_out_grad_sibling"))
    pw = _grad_matmul(dz, h, core, 768, "w_in_grad_own", add=sw)
    po = _grad_matmul(cat, dy, core, 512, "w_out_grad_own", add=so)
    send_sems, recv_sems, pw, po, lw, lo, token = _grad_to_chips_start(pw, po)

    gx, dshift, dscale, dng = _in_proj_bwd(dz, win_t, xs, dx1, norm_g + token[0:1, 0:1], mod)

    parts = dict(dng=dng, dshift=dshift, dscale=dscale, dgate=dgate, dfg=dfg, dlg=dlg, dlb=dlb, dsinks=dsinks,
                 dsgu_b=dsgu_b, dsgu_w=dsgu_w, loss=loss)
    small = dict(norm_g=(norm_g, m_norm_g, v_norm_g), b_ada=(b_ada, m_b_ada, v_b_ada),
                 attn_sinks=(attn_sinks, m_attn_sinks, v_attn_sinks), sgu_ln_g=(sgu_ln_g, m_sgu_ln_g, v_sgu_ln_g),
                 sgu_ln_b=(sgu_ln_b, m_sgu_ln_b, v_sgu_ln_b), sgu_w=(sgu_w, m_sgu_w, v_sgu_w), sgu_b=(sgu_b, m_sgu_b, v_sgu_b),
                 final_g=(final_g, m_final_g, v_final_g))
    res, dmod_all, total = _small_grads_step(parts, small)
    dmod_mine = lax.dynamic_slice_in_dim(dmod_all, ADA_SHARD * chip, ADA_SHARD, axis=1)

    res["w_ada"] = _adamw_ada(cact, dmod_mine, w_ada[0], m_w_ada[0], v_w_ada[0])
    pw, po, slots_w, slots_o = _grad_to_chips_wait(send_sems, recv_sems, pw, po, lw, lo, res["w_ada"][1])
    place = jnp.stack([core[0], chip.astype(jnp.int32)])
    gw_t, go = _grad_halves_swap(_sum_slots(slots_w, pw, place, IN_SHARD // 4, D_MODEL, "w_in_grad_sum"),
                                 _sum_slots(slots_o, po, place, OUT_SHARD // 2, D_MODEL, "w_out_grad_sum"))
    res["w_in"] =tuple(jnp.swapaxes(a, 0, 1) for a in _adamw_rows(w_in_t, gw_t, m_in_t, v_in_t, IN_SHARD // 4, "adamw_w_in"))
    res["w_out"] = _adamw_rows(w_out[0], go, m_w_out[0], v_w_out[0], 256, "adamw_w_out")

    order = ["norm_g", "w_ada", "b_ada", "w_in", "attn_sinks", "sgu_ln_g", "sgu_ln_b", "sgu_w", "sgu_b", "w_out", "final_g"]
    like = dict(norm_g=norm_g, w_ada=w_ada, b_ada=b_ada, w_in=w_in, attn_sinks=attn_sinks, sgu_ln_g=sgu_ln_g, sgu_ln_b=sgu_ln_b,
                sgu_w=sgu_w, sgu_b=sgu_b, w_out=w_out, final_g=final_g)
    outs = [total[0, 0], gx[None]]
    for kind in range(4):
        outs += [res[n][kind].reshape(like[n].shape) for n in order]
    return tuple(outs)
```

```python
import functools
import math

import jax
import jax.numpy as jnp
from jax import lax
from jax.experimental import pallas as pl
from jax.experimental.pallas import tpu as pltpu

F32 = jnp.float32
BF16 = jnp.bfloat16

D_MODEL = 2048
D_ATTN = 1024
HEAD_DIM = 64
N_Q_HEADS = 16
Q_PER_KV = 8
D_KV = 128
BLOCK = 128
D_SGU = 1024
SGU_GROUPS = 8
GROUP_DIM = 128
D_IN = 5376
EPS = 1e-6
ATTN_SCALE = 1.0 / math.sqrt(HEAD_DIM)
N_CHIPS = 4
N_DEV = 8
IN_SHARD = D_IN // N_CHIPS
OUT_SHARD = D_MODEL // N_CHIPS
ADA_SHARD = 3 * D_MODEL // N_CHIPS
HALF = D_MODEL // 2

Q0, K0, V0, GA0, U0, VS0, GS0 = 0, 1024, 1152, 1280, 2304, 3328, 4352

ADAM_LR = 0.001
ADAM_B1 = 0.9
ADAM_B2 = 0.999
ADAM_EPS = 1e-08
ADAM_WD = 0.01
ADAM_STEP = 10

VMEM_LIMIT_V7X = 56 * 1024 * 1024

NN = (((1,), (0,)), ((), ()))
NT = (((1,), (1,)), ((), ()))
TN = (((0,), (0,)), ((), ()))
MESH = pl.DeviceIdType.MESH
ANY = pl.BlockSpec(memory_space=pl.ANY)
VMEM = pl.BlockSpec(memory_space=pltpu.VMEM)
HBM = pl.BlockSpec(memory_space=pltpu.HBM)
SEM = pl.BlockSpec(memory_space=pltpu.SEMAPHORE)
EFFECT = pltpu.SideEffectType.DATAFLOW_SIDE_EFFECTING


def _dot(a, b, dims):
    return lax.dot_general(a, b, dims, preferred_element_type=F32)


def _params(sem=None):
    return pltpu.CompilerParams(dimension_semantics=sem, vmem_limit_bytes=VMEM_LIMIT_V7X)


def _sigmoid(x):
    return jax.nn.sigmoid(x)


def _adamw(w, g, m, v):
    m = ADAM_B1 * m + (1.0 - ADAM_B1) * g
    v = ADAM_B2 * v + (1.0 - ADAM_B2) * (g * g)
    m_hat = m / (1.0 - ADAM_B1 ** ADAM_STEP)
    v_hat = v / (1.0 - ADAM_B2 ** ADAM_STEP)
    delta = -ADAM_LR * (m_hat / (jnp.sqrt(v_hat) + ADAM_EPS) + ADAM_WD * w)
    return delta, m, v


def _place():
    return lax.axis_index("x"), lax.axis_index("y"), lax.axis_index("c")


def _flip(p, bit):
    return 1 - p if bit else p


def _mod_exchange(c, w_ada, b_ada):
    kdim = w_ada.shape[0]
    kc = 256

    def body(c_ref, w_ref, b_ref, mod_ref, cact_ref, cbuf, pbuf, rbuf, send_sems, recv_sems):
        x, y, cc = _place()
        me = 4 * x + 2 * y + cc
        chip = 2 * x + y
        cbuf[me] = c_ref[...]
        gathers = []
        for k in range(1, N_DEV):
            bx, by, bc = (k >> 2) & 1, (k >> 1) & 1, k & 1
            cp = pltpu.make_async_remote_copy(
                src_ref=cbuf.at[me], dst_ref=cbuf.at[me], send_sem=send_sems.at[k - 1], recv_sem=recv_sems.at[k - 1],
                device_id=(_flip(x, bx), _flip(y, by), _flip(cc, bc)), device_id_type=MESH)
            cp.start()
            gathers.append(cp)
        for cp in gathers:
            cp.wait()
        for b in range(N_DEV):
            cv = cbuf[b]
            cact_ref[b:b + 1, :] = cv * _sigmoid(cv)
        acc = jnp.zeros((N_DEV, ADA_SHARD), F32)
        for k0 in range(0, kdim, kc):
            acc = acc + _dot(cact_ref[:, k0:k0 + kc].astype(BF16), w_ref[k0:k0 + kc, :].astype(BF16), NN)
        for b in range(N_DEV):
            pbuf[b] = acc[b:b + 1, :]
        rbuf[chip] = pbuf[me]
        sends = []
        for j, (bx, by) in enumerate(((0, 1), (1, 0), (1, 1))):
            px, py = _flip(x, bx), _flip(y, by)
            cp = pltpu.make_async_remote_copy(
                src_ref=pbuf.at[4 * px + 2 * py + cc], dst_ref=rbuf.at[chip],
                send_sem=send_sems.at[N_DEV - 1 + j], recv_sem=recv_sems.at[N_DEV - 1 + j],
                device_id=(px, py, cc), device_id_type=MESH)
            cp.start()
            sends.append(cp)
        for cp in sends:
            cp.wait()
        for k in range(N_CHIPS):
            cols = slice(ADA_SHARD * k, ADA_SHARD * (k + 1))
            mod_ref[:, cols] = rbuf[k] + b_ref[:, cols]

    return pl.pallas_call(
        body, name="mod_exchange",
        out_shape=(jax.ShapeDtypeStruct((1, 3 * D_MODEL), F32), jax.ShapeDtypeStruct((N_DEV, D_MODEL), F32)),
        in_specs=[VMEM, VMEM, VMEM], out_specs=(VMEM, VMEM),
        scratch_shapes=[pltpu.VMEM((N_DEV, 1, D_MODEL), F32), pltpu.VMEM((N_DEV, 1, ADA_SHARD), F32),
                        pltpu.VMEM((N_CHIPS, 1, ADA_SHARD), F32),
                        pltpu.SemaphoreType.DMA((N_DEV + 2,)), pltpu.SemaphoreType.DMA((N_DEV + 2,))],
        compiler_params=_params(),
    )(c, w_ada, b_ada)


def _other_chips(x, y):
    return [(_flip(x, bx), _flip(y, by)) for bx, by in ((0, 1), (1, 0), (1, 1))]


def _split_start(copies, bufs, n_sems, name):
    n = len(bufs)

    def body(*refs):
        send_sems, recv_sems = refs[n:n + 2]
        for cp in copies(*refs[:n], send_sems, recv_sems):
            cp.start()
        refs[-1][...] = jnp.zeros_like(refs[-1])

    bufs = [pltpu.with_memory_space_constraint(a, pltpu.HBM) for a in bufs]
    outs = pl.pallas_call(
        body, name=name,
        out_shape=(pltpu.SemaphoreType.DMA((n_sems,)), pltpu.SemaphoreType.DMA((n_sems,)))
        + tuple(pltpu.HBM(a.shape, a.dtype) for a in bufs) + (jax.ShapeDtypeStruct((8, 128), F32),),
        in_specs=[HBM] * n, out_specs=(SEM, SEM) + (HBM,) * n + (VMEM,), input_output_aliases={k: k + 2 for k in range(n)},
        compiler_params=pltpu.CompilerParams(has_side_effects=EFFECT),
    )(*bufs)
    return outs[:2], list(outs[2:2 + n]), outs[-1]


def _split_wait(copies, sems, bufs, after, name):
    n = len(bufs)

    def body(*refs):
        send_sems, recv_sems = refs[n:n + 2]
        for cp in copies(*refs[:n], send_sems, recv_sems):
            cp.wait_send()
            cp.wait_recv()

    return pl.pallas_call(
        body, name=name, out_shape=tuple(pltpu.HBM(a.shape, a.dtype) for a in bufs),
        in_specs=[HBM] * n + [SEM, SEM, ANY], out_specs=(HBM,) * n, input_output_aliases={k: k for k in range(n)},
        compiler_params=pltpu.CompilerParams(has_side_effects=EFFECT),
    )(*bufs, *sems, after)


def _gather_copies(w_ref, full_ref, send_sems, recv_sems):
    x, y, cc = _place()
    chip = 2 * x + y
    shard = w_ref.shape[0]
    half = shard // 2
    return [pltpu.make_async_remote_copy(
        src_ref=w_ref.at[pl.ds(half * cc, half), :], dst_ref=full_ref.at[pl.ds(shard * chip + half * cc, half), :],
        send_sem=send_sems.at[j], recv_sem=recv_sems.at[j], device_id=(px, py, cc), device_id_type=MESH)
        for j, (px, py) in enumerate(_other_chips(x, y))]


def _gather_finish(w, full, name):
    shard = w.shape[0]
    half = shard // 2

    def body(w_ref, full_in, full_ref, loc_sem, send_sems, recv_sems):
        x, y, cc = _place()
        chip = 2 * x + y
        own = pltpu.make_async_copy(w_ref, full_ref.at[pl.ds(shard * chip, shard), :], loc_sem)
        own.start()
        cps = []
        for j, (px, py) in enumerate(_other_chips(x, y)):
            landed = full_ref.at[pl.ds(shard * (2 * px + py) + half * cc, half), :]
            cps.append(pltpu.make_async_remote_copy(src_ref=landed, dst_ref=landed, send_sem=send_sems.at[j], recv_sem=recv_sems.at[j],
                                                    device_id=(x, y, 1 - cc), device_id_type=MESH))
        for cp in cps:
            cp.start()
        for cp in cps:
            cp.wait()
        own.wait()

    return pl.pallas_call(
        body, name=name, out_shape=jax.ShapeDtypeStruct(full.shape, full.dtype),
        in_specs=[ANY, ANY], out_specs=ANY, input_output_aliases={1: 0},
        scratch_shapes=[pltpu.SemaphoreType.DMA, pltpu.SemaphoreType.DMA((3,)), pltpu.SemaphoreType.DMA((3,))],
        compiler_params=_params(),
    )(w, full)


def _to_sibling(g, name):
    def body(g_ref, r_ref, send_sem, recv_sem):
        x, y, cc = _place()
        cp = pltpu.make_async_remote_copy(src_ref=g_ref, dst_ref=r_ref, send_sem=send_sem, recv_sem=recv_sem,
                                          device_id=(x, y, 1 - cc), device_id_type=MESH)
        cp.start()
        cp.wait()

    return pl.pallas_call(
        body, name=name, out_shape=jax.ShapeDtypeStruct(g.shape, g.dtype), in_specs=[ANY], out_specs=ANY,
        scratch_shapes=[pltpu.SemaphoreType.DMA, pltpu.SemaphoreType.DMA], compiler_params=_params(),
    )(g)


def _chip_copies(p_ref, land_ref, send_sems, recv_sems):
    x, y, cc = _place()
    chip = 2 * x + y
    shard = land_ref.shape[1]
    return [pltpu.make_async_remote_copy(
        src_ref=p_ref.at[pl.ds(shard * (2 * px + py), shard), :], dst_ref=land_ref.at[chip],
        send_sem=send_sems.at[j], recv_sem=recv_sems.at[j], device_id=(px, py, cc), device_id_type=MESH)
        for j, (px, py) in enumerate(_other_chips(x, y))]


def _grad_halves_swap(fw, fo):
    def body(fw_in, fo_in, fw_ref, fo_ref, send_sems, recv_sems):
        x, y, cc = _place()
        cps = []
        for a, ref in enumerate((fw_ref, fo_ref)):
            mine = ref.at[:, pl.ds(pl.multiple_of(HALF * cc, 128), HALF)]
            cp = pltpu.make_async_remote_copy(src_ref=mine, dst_ref=mine, send_sem=send_sems.at[a], recv_sem=recv_sems.at[a],
                                              device_id=(x, y, 1 - cc), device_id_type=MESH)
            cp.start()
            cps.append(cp)
        for cp in cps:
            cp.wait()

    return pl.pallas_call(
        body, name="grad_halves_swap",
        out_shape=(jax.ShapeDtypeStruct(fw.shape, fw.dtype), jax.ShapeDtypeStruct(fo.shape, fo.dtype)),
        in_specs=[ANY, ANY], out_specs=(ANY, ANY), input_output_aliases={0: 0, 1: 1},
        scratch_shapes=[pltpu.SemaphoreType.DMA((2,)), pltpu.SemaphoreType.DMA((2,))],
        compiler_params=_params(),
    )(fw, fo)


ROW_NORM_G, ROW_DMOD, ROW_FINAL_G, ROW_LN, ROW_SINKS, ROW_LOSS, ROW_SGU_B, MISC_ROWS = 0, 1, 4, 5, 6, 7, 8, 16


def _small_grads_step(parts, small):
    names = ["norm_g", "b_ada", "attn_sinks", "sgu_ln_g", "sgu_ln_b", "sgu_w", "sgu_b", "final_g"]
    part_names = ["dng", "dshift", "dscale", "dgate", "dfg", "dlg", "dlb", "dsinks", "dsgu_b", "dsgu_w", "loss"]
    n_parts = len(part_names)

    def body(*refs):
        p = dict(zip(part_names, refs[:n_parts]))
        wmv = {n: refs[n_parts + 3 * i: n_parts + 3 * i + 3] for i, n in enumerate(names)}
        outs = refs[n_parts + 3 * len(names):]
        res = {n: outs[4 * i: 4 * i + 4] for i, n in enumerate(names)}
        dmod_ref, loss_ref = outs[4 * len(names):4 * len(names) + 2]
        misc, wbuf, cmisc, cwbuf, dbuf, send_sems, recv_sems = outs[4 * len(names) + 2:]
        x, y, cc = _place()
        me = 4 * x + 2 * y + cc
        chip = 2 * x + y

        def remote(k, src, dst, peer):
            return pltpu.make_async_remote_copy(src_ref=src, dst_ref=dst, send_sem=send_sems.at[k], recv_sem=recv_sems.at[k],
                                                device_id=peer, device_id_type=MESH)

        dbuf[me] = jnp.zeros((8, D_MODEL), F32)
        dbuf[me, 0:1, :] = p["dshift"][...]
        dbuf[me, 1:2, :] = p["dscale"][...]
        dbuf[me, 2:3, :] = p["dgate"][...]
        cps = []
        for k in range(1, N_DEV):
            bx, by, bc = (k >> 2) & 1, (k >> 1) & 1, k & 1
            cps.append(remote(k - 1, dbuf.at[me], dbuf.at[me], (_flip(x, bx), _flip(y, by), _flip(cc, bc))))

        misc[cc] = jnp.zeros((MISC_ROWS, D_MODEL), F32)
        misc[cc, ROW_NORM_G:ROW_NORM_G + 1, :] = p["dng"][...]
        misc[cc, ROW_DMOD:ROW_DMOD + 1, :] = p["dshift"][...]
        misc[cc, ROW_DMOD + 1:ROW_DMOD + 2, :] = p["dscale"][...]
        misc[cc, ROW_DMOD + 2:ROW_DMOD + 3, :] = p["dgate"][...]
        misc[cc, ROW_FINAL_G:ROW_FINAL_G + 1, :] = p["dfg"][...]
        misc[cc, ROW_LN:ROW_LN + 1, 0:D_SGU] = p["dlg"][...]
        misc[cc, ROW_LN:ROW_LN + 1, D_SGU:2 * D_SGU] = p["dlb"][...]
        misc[cc, ROW_SINKS:ROW_SINKS + 1, 0:N_Q_HEADS] = p["dsinks"][...]
        misc[cc, ROW_LOSS:ROW_LOSS + 1, 0:1] = p["loss"][...]
        misc[cc, ROW_SGU_B:ROW_SGU_B + SGU_GROUPS, 0:BLOCK] = p["dsgu_b"][...]
        for g in range(SGU_GROUPS):
            wbuf[cc, BLOCK * g:BLOCK * (g + 1), :] = p["dsgu_w"][g]
        pair = [remote(N_DEV - 1 + a, buf.at[cc], buf.at[cc], (x, y, 1 - cc)) for a, buf in enumerate((misc, wbuf))]
        for cp in cps + pair:
            cp.start()
        for cp in pair:
            cp.wait()
        cmisc[chip] = misc[0] + misc[1]
        cwbuf[chip] = wbuf[0] + wbuf[1]
        quad = []
        for j, (bx, by) in enumerate(((0, 1), (1, 0), (1, 1))):
            for a, buf in enumerate((cmisc, cwbuf)):
                quad.append(remote(N_DEV + 1 + 2 * j + a, buf.at[chip], buf.at[chip], (_flip(x, bx), _flip(y, by), cc)))
        for cp in quad:
            cp.start()
        for cp in cps + quad:
            cp.wait()

        tot = cmisc[0]
        for k in range(1, N_CHIPS):
            tot = tot + cmisc[k]
        loss_ref[...] = tot[ROW_LOSS:ROW_LOSS + 1, 0:1]
        for b in range(N_DEV):
            dmod_ref[b:b + 1, :] = jnp.concatenate([dbuf[b, t:t + 1, :] for t in range(3)], axis=1)

        def update(name, g, pick=lambda r: r[...], put=None):
            w_ref, m_ref, v_ref = wmv[name]
            delta, m2, v2 = _adamw(pick(w_ref), g, pick(m_ref), pick(v_ref))
            for ref, val in zip(res[name], (g, delta, m2, v2)):
                if put is None:
                    ref[...] = val
                else:
                    put(ref, val)

        update("norm_g", tot[ROW_NORM_G:ROW_NORM_G + 1, :])
        update("final_g", tot[ROW_FINAL_G:ROW_FINAL_G + 1, :])
        for t in range(3):
            cols = slice(D_MODEL * t, D_MODEL * (t + 1))

            def put_cols(ref, val, cols=cols):
                ref[:, cols] = val
            update("b_ada", tot[ROW_DMOD + t:ROW_DMOD + t + 1, :], pick=lambda r, cols=cols: r[:, cols], put=put_cols)
        update("sgu_ln_g", tot[ROW_LN:ROW_LN + 1, 0:D_SGU])
        update("sgu_ln_b", tot[ROW_LN:ROW_LN + 1, D_SGU:2 * D_SGU])
        update("attn_sinks", tot[ROW_SINKS:ROW_SINKS + 1, 0:N_Q_HEADS])
        update("sgu_b", tot[ROW_SGU_B:ROW_SGU_B + SGU_GROUPS, 0:BLOCK])
        for g in range(SGU_GROUPS):
            rows = slice(BLOCK * g, BLOCK * (g + 1))
            gw = cwbuf[0, rows, :]
            for k in range(1, N_CHIPS):
                gw = gw + cwbuf[k, rows, :]

            def put_group(ref, val, g=g):
                ref[g] = val
            update("sgu_w", gw, pick=lambda r, g=g: r[g], put=put_group)

    shapes = {"norm_g": (1, D_MODEL), "b_ada": (1, 3 * D_MODEL), "attn_sinks": (1, N_Q_HEADS), "sgu_ln_g": (1, D_SGU),
              "sgu_ln_b": (1, D_SGU), "sgu_w": (SGU_GROUPS, BLOCK, BLOCK), "sgu_b": (SGU_GROUPS, BLOCK), "final_g": (1, D_MODEL)}
    operands = [parts[n] for n in part_names]
    for n in names:
        operands += [a.reshape(shapes[n]) for a in small[n]]
    out_shape = []
    for n in names:
        out_shape += [jax.ShapeDtypeStruct(shapes[n], F32)] * 4
    out_shape += [jax.ShapeDtypeStruct((N_DEV, 3 * D_MODEL), F32), jax.ShapeDtypeStruct((1, 1), F32)]
    n_sems = (N_DEV - 1) + 2 + 2 * (N_CHIPS - 1)
    outs = pl.pallas_call(
        body, name="small_grads_step", out_shape=tuple(out_shape),
        in_specs=[VMEM] * len(operands), out_specs=tuple([VMEM] * len(out_shape)),
        scratch_shapes=[pltpu.VMEM((2, MISC_ROWS, D_MODEL), F32), pltpu.VMEM((2, SGU_GROUPS * BLOCK, BLOCK), F32),
                        pltpu.VMEM((N_CHIPS, MISC_ROWS, D_MODEL), F32), pltpu.VMEM((N_CHIPS, SGU_GROUPS * BLOCK, BLOCK), F32),
                        pltpu.VMEM((N_DEV, 8, D_MODEL), F32),
                        pltpu.SemaphoreType.DMA((n_sems,)), pltpu.SemaphoreType.DMA((n_sems,))],
        compiler_params=_params(),
    )(*operands)
    res = {n: outs[4 * i: 4 * i + 4] for i, n in enumerate(names)}
    return res, outs[4 * len(names)], outs[4 * len(names) + 1]


def _cast_bf16(w, tile, name, after):
    r, c = w.shape

    def body(w_ref, after_ref, o_ref):
        o_ref[...] = w_ref[...].astype(BF16)

    return pl.pallas_call(
        body, name=name, grid=(r // tile,), out_shape=jax.ShapeDtypeStruct((r, c), BF16),
        in_specs=[pl.BlockSpec((tile, c), lambda i: (i, 0)), ANY], out_specs=pl.BlockSpec((tile, c), lambda i: (i, 0)),
        compiler_params=_params(("parallel",)),
    )(w, after)


def _norm_modulate(x, norm_g, mod):
    s, d = x.shape
    tm = min(256, s)

    def body(x_ref, g_ref, mod_ref, h_ref):
        xv = x_ref[...]
        r = lax.rsqrt(jnp.mean(xv * xv, axis=-1, keepdims=True) + EPS)
        h = (xv * r * g_ref[...]) * (1.0 + mod_ref[:, d:2 * d]) + mod_ref[:, 0:d]
        h_ref[...] = h.astype(BF16)

    return pl.pallas_call(
        body, name="norm_modulate", grid=(s // tm,), out_shape=jax.ShapeDtypeStruct((s, d), BF16),
        in_specs=[pl.BlockSpec((tm, d), lambda i: (i, 0)), pl.BlockSpec((1, d), lambda i: (0, 0)),
                  pl.BlockSpec((1, 3 * d), lambda i: (0, 0))],
        out_specs=pl.BlockSpec((tm, d), lambda i: (i, 0)),
        compiler_params=_params(("parallel",)),
    )(x, norm_g, mod)


def _matmul_nt(a, bt, tm, tn, out_dtype, name, after):
    m, k = a.shape
    n = bt.shape[0]
    tm, tn = min(tm, m), min(tn, n)

    def body(a_ref, b_ref, after_ref, o_ref):
        o_ref[...] = _dot(a_ref[...], b_ref[...], NT).astype(out_dtype)

    return pl.pallas_call(
        body, name=name, grid=(m // tm, n // tn), out_shape=jax.ShapeDtypeStruct((m, n), out_dtype),
        in_specs=[pl.BlockSpec((tm, k), lambda i, j: (i, 0)), pl.BlockSpec((tn, k), lambda i, j: (j, 0)), ANY],
        out_specs=pl.BlockSpec((tm, tn), lambda i, j: (i, j)),
        compiler_params=_params(("parallel", "parallel")),
    )(a, bt, after)


def _grad_matmul(a, b, half, tm, name, add=None):
    t, m = a.shape
    tm = min(tm, m)

    def body(half_ref, a_ref, b_ref, *rest):
        if add is None:
            (o_ref,) = rest
            o_ref[...] = _dot(a_ref[...], b_ref[...], TN).astype(BF16)
        else:
            add_ref, o_ref = rest
            o_ref[...] = (_dot(a_ref[...], b_ref[...], TN) + add_ref[...].astype(F32)).astype(BF16)

    in_specs = [pl.BlockSpec((t, tm), lambda i, h: (0, i)), pl.BlockSpec((t, HALF), lambda i, h: (0, h[0]))]
    operands = [a, b]
    if add is not None:
        in_specs.append(pl.BlockSpec((tm, HALF), lambda i, h: (i, 0)))
        operands.append(add)
    return pl.pallas_call(
        body, name=name,
        grid_spec=pltpu.PrefetchScalarGridSpec(num_scalar_prefetch=1, grid=(m // tm,), in_specs=in_specs,
                                               out_specs=pl.BlockSpec((tm, HALF), lambda i, h: (i, 0))),
        out_shape=jax.ShapeDtypeStruct((m, HALF), BF16),
        compiler_params=_params(("parallel",)),
    )(half, *operands)


def _mixer_masks(n):
    row = lax.broadcasted_iota(jnp.int32, (Q_PER_KV * BLOCK, 2 * BLOCK), 0) & (BLOCK - 1)
    col = lax.broadcasted_iota(jnp.int32, (Q_PER_KV * BLOCK, 2 * BLOCK), 1)
    valid = (col > row) & (col <= row + BLOCK) & ((col >= BLOCK) | (n > 0))
    r2 = lax.broadcasted_iota(jnp.int32, (BLOCK, BLOCK), 0)
    c2 = lax.broadcasted_iota(jnp.int32, (BLOCK, BLOCK), 1)
    return valid, r2 >= c2


def _layer_norm_parts(vs):
    mu = jnp.mean(vs, axis=-1, keepdims=True)
    xc = vs - mu
    rstd = lax.rsqrt(jnp.mean(xc * xc, axis=-1, keepdims=True) + EPS)
    return xc * rstd, rstd


def _stack_heads(ref, base, hk):
    first = base + HEAD_DIM * Q_PER_KV * hk
    return jnp.concatenate([ref[:, first + HEAD_DIM * g:first + HEAD_DIM * (g + 1)] for g in range(Q_PER_KV)], axis=0)


def _unstack_heads(ref, base, hk, val):
    first = base + HEAD_DIM * Q_PER_KV * hk
    for g in range(Q_PER_KV):
        ref[:, first + HEAD_DIM * g:first + HEAD_DIM * (g + 1)] = val[BLOCK * g:BLOCK * (g + 1), :].astype(ref.dtype)


def _stack_sinks(sink_ref, hk):
    return jnp.concatenate([jnp.broadcast_to(sink_ref[:, h:h + 1], (BLOCK, 1))
                            for h in range(Q_PER_KV * hk, Q_PER_KV * (hk + 1))], axis=0)


def _attn_probs(qs, kh, valid, sk):
    s = _dot(qs, kh, NT) * ATTN_SCALE
    s = jnp.where(valid, s, -jnp.inf)
    m = jnp.maximum(jnp.max(s, axis=-1, keepdims=True), sk)
    e = jnp.exp(s - m)
    es = jnp.exp(sk - m)
    inv = 1.0 / (jnp.sum(e, axis=-1, keepdims=True) + es)
    return e * inv, es * inv


def _kv_ext(z_ref, kvp_ref):
    k_ext = jnp.concatenate([kvp_ref[:, 0:D_KV], z_ref[:, K0:K0 + D_KV]], axis=0).astype(BF16)
    v_ext = jnp.concatenate([kvp_ref[:, D_KV:2 * D_KV], z_ref[:, V0:V0 + D_KV]], axis=0).astype(BF16)
    return k_ext, v_ext


def _sgu_mixed(w_ref, bt_ref, vn, tril):
    ws = [jnp.where(tril, w_ref[g], 0.0).astype(BF16) for g in range(SGU_GROUPS)]
    mixed = jnp.concatenate([_dot(ws[g], vn[:, GROUP_DIM * g:GROUP_DIM * (g + 1)], NN) + bt_ref[:, g:g + 1]
                             for g in range(SGU_GROUPS)], axis=1)
    return ws, mixed


def _mixer_fwd(z, sinks, ln_g, ln_b, sgu_w, sgu_bt):
    s = z.shape[0]
    nb = s // BLOCK

    def body(z_ref, kvp_ref, sink_ref, lg_ref, lb_ref, w_ref, bt_ref, cat_ref, o_ref):
        n = pl.program_id(0)
        valid, tril = _mixer_masks(n)
        k_ext, v_ext = _kv_ext(z_ref, kvp_ref)
        for hk in range(2):
            kv = slice(HEAD_DIM * hk, HEAD_DIM * (hk + 1))
            p, _ = _attn_probs(_stack_heads(z_ref, Q0, hk).astype(BF16), k_ext[:, kv], valid, _stack_sinks(sink_ref, hk))
            o = _dot(p.astype(BF16), v_ext[:, kv], NN)
            ga = _stack_heads(z_ref, GA0, hk)
            _unstack_heads(o_ref, 0, hk, o)
            _unstack_heads(cat_ref, 0, hk, o * (ga * _sigmoid(ga)))
        xhat, _ = _layer_norm_parts(z_ref[:, VS0:VS0 + D_SGU])
        vn = (xhat * lg_ref[...] + lb_ref[...]).astype(BF16)
        _, mixed = _sgu_mixed(w_ref, bt_ref, vn, tril)
        gt = z_ref[:, GS0:GS0 + D_SGU]
        cat_ref[:, D_ATTN:D_MODEL] = ((z_ref[:, U0:U0 + D_SGU] * mixed) * (gt * _sigmoid(gt))).astype(BF16)

    const2 = lambda n: (0, 0)
    return pl.pallas_call(
        body, name="mixer_fwd", grid=(nb,),
        out_shape=(jax.ShapeDtypeStruct((s, D_MODEL), BF16), jax.ShapeDtypeStruct((s, D_ATTN), F32)),
        in_specs=[pl.BlockSpec((BLOCK, D_IN), lambda n: (n, 0)),
                  pl.BlockSpec((BLOCK, 2 * D_KV), lambda n: (jnp.maximum(n - 1, 0), K0 // (2 * D_KV))),
                  pl.BlockSpec((1, N_Q_HEADS), const2), pl.BlockSpec((1, D_SGU), const2), pl.BlockSpec((1, D_SGU), const2),
                  pl.BlockSpec((SGU_GROUPS, BLOCK, BLOCK), lambda n: (0, 0, 0)), pl.BlockSpec((BLOCK, SGU_GROUPS), const2)],
        out_specs=(pl.BlockSpec((BLOCK, D_MODEL), lambda n: (n, 0)), pl.BlockSpec((BLOCK, D_ATTN), lambda n: (n, 0))),
        compiler_params=_params(("parallel",)),
    )(z, z, sinks, ln_g, ln_b, sgu_w, sgu_bt)


def _out_proj_loss(cat, wout, x, target, mod, final_g):
    s, d = x.shape
    tm = min(256, s)

    def body(cat_ref, w_ref, x_ref, t_ref, mod_ref, fg_ref, dx1_ref, dy_ref, dfg_ref, dgate_ref, loss_ref):
        i = pl.program_id(0)
        gate = mod_ref[:, 2 * d:3 * d]
        fg = fg_ref[...]
        yv = _dot(cat_ref[...], w_ref[...], NN)
        x1 = x_ref[...] + gate * yv
        r = lax.rsqrt(jnp.mean(x1 * x1, axis=-1, keepdims=True) + EPS)
        xr = x1 * r
        err = xr * fg - t_ref[...]
        dout = err * (1.0 / d)
        gy = dout * fg
        dx1 = r * gy - xr * (r * r * jnp.mean(gy * x1, axis=-1, keepdims=True))
        dx1_ref[...] = dx1
        dy_ref[...] = (gate * dx1).astype(BF16)

        @pl.when(i == 0)
        def _():
            dfg_ref[...] = jnp.zeros_like(dfg_ref)
            dgate_ref[...] = jnp.zeros_like(dgate_ref)
            loss_ref[...] = jnp.zeros_like(loss_ref)

        dfg_ref[...] += jnp.sum(dout * xr, axis=0, keepdims=True)
        dgate_ref[...] += jnp.sum(dx1 * yv, axis=0, keepdims=True)
        loss_ref[...] += 0.5 * jnp.sum(jnp.mean(err * err, axis=-1, keepdims=True), axis=0, keepdims=True)

    row = lambda i: (i, 0)
    const = lambda i: (0, 0)
    return pl.pallas_call(
        body, name="out_proj_loss", grid=(s // tm,),
        out_shape=(jax.ShapeDtypeStruct((s, d), F32), jax.ShapeDtypeStruct((s, d), BF16), jax.ShapeDtypeStruct((1, d), F32),
                   jax.ShapeDtypeStruct((1, d), F32), jax.ShapeDtypeStruct((1, 1), F32)),
        in_specs=[pl.BlockSpec((tm, d), row), pl.BlockSpec((d, d), const), pl.BlockSpec((tm, d), row), pl.BlockSpec((tm, d), row),
                  pl.BlockSpec((1, 3 * d), const), pl.BlockSpec((1, d), const)],
        out_specs=(pl.BlockSpec((tm, d), row), pl.BlockSpec((tm, d), row), pl.BlockSpec((1, d), const), pl.BlockSpec((1, d), const),
                   pl.BlockSpec((1, 1), const)),
        compiler_params=_params(("arbitrary",)),
    )(cat, wout, x, target, mod, final_g)


def _mixer_bwd(z, dcat, o, sinks, ln_g, ln_b, sgu_w, sgu_bt):
    s = z.shape[0]
    nb = s // BLOCK

    def body(z_ref, kvp_ref, dc_ref, o_ref, sink_ref, lg_ref, lb_ref, w_ref, bt_ref,
             dz_ref, dsink_ref, dlg_ref, dlb_ref, dw_ref, db_ref, pend, dkv, carry, dsink_acc, dbt_acc):
        n = pl.program_id(0)

        @pl.when(n == 0)
        def _():
            pend[...] = jnp.zeros_like(pend)
            carry[...] = jnp.zeros_like(carry)
            dsink_acc[...] = jnp.zeros_like(dsink_acc)
            dbt_acc[...] = jnp.zeros_like(dbt_acc)
            dlg_ref[...] = jnp.zeros_like(dlg_ref)
            dlb_ref[...] = jnp.zeros_like(dlb_ref)
            dw_ref[...] = jnp.zeros_like(dw_ref)

        dz_ref[:, 0:K0] = pend[:, 0:K0]
        dz_ref[:, GA0:D_IN] = pend[:, GA0:D_IN]

        @pl.when(n == nb)
        def _():
            dkv[...] = jnp.zeros_like(dkv)

        @pl.when(n < nb)
        def _():
            valid, tril = _mixer_masks(n)
            k_ext, v_ext = _kv_ext(z_ref, kvp_ref)
            for hk in range(2):
                kv = slice(HEAD_DIM * hk, HEAD_DIM * (hk + 1))
                kh, vh = k_ext[:, kv], v_ext[:, kv]
                qs = _stack_heads(z_ref, Q0, hk).astype(BF16)
                p, p_sink = _attn_probs(qs, kh, valid, _stack_sinks(sink_ref, hk))
                ga = _stack_heads(z_ref, GA0, hk)
                sg = _sigmoid(ga)
                datt = _stack_heads(dc_ref, 0, hk)
                do = (datt * (ga * sg)).astype(BF16)
                _unstack_heads(pend, GA0, hk, datt * _stack_heads(o_ref, 0, hk) * (sg * (1.0 + ga * (1.0 - sg))))
                dp = _dot(do, vh, NT)
                delta = jnp.sum(p * dp, axis=-1, keepdims=True)
                ds = (p * (dp - delta) * ATTN_SCALE).astype(BF16)
                dsk = -p_sink * delta
                for gq in range(Q_PER_KV):
                    h = hk * Q_PER_KV + gq
                    dsink_acc[:, h:h + 1] += jnp.sum(dsk[BLOCK * gq:BLOCK * (gq + 1), :], axis=0, keepdims=True)
                _unstack_heads(pend, Q0, hk, _dot(ds, kh, NN))
                dkv[:, kv] = _dot(ds, qs, TN)
                dkv[:, D_KV + HEAD_DIM * hk:D_KV + HEAD_DIM * (hk + 1)] = _dot(p.astype(BF16), do, TN)

            xhat, rstd = _layer_norm_parts(z_ref[:, VS0:VS0 + D_SGU])
            lg = lg_ref[...]
            vn = (xhat * lg + lb_ref[...]).astype(BF16)
            ws, mixed = _sgu_mixed(w_ref, bt_ref, vn, tril)
            u = z_ref[:, U0:U0 + D_SGU]
            gt = z_ref[:, GS0:GS0 + D_SGU]
            sg = _sigmoid(gt)
            dsgu = dc_ref[:, D_ATTN:D_MODEL]
            dgated = dsgu * (gt * sg)
            pend[:, U0:U0 + D_SGU] = (dgated * mixed).astype(BF16)
            pend[:, GS0:GS0 + D_SGU] = (dsgu * (u * mixed) * (sg * (1.0 + gt * (1.0 - sg)))).astype(BF16)
            dmixed = dgated * u
            dmb = dmixed.astype(BF16)
            dvn_parts = []
            for g in range(SGU_GROUPS):
                gs_ = slice(GROUP_DIM * g, GROUP_DIM * (g + 1))
                dw_ref[g] += jnp.where(tril, _dot(dmb[:, gs_], vn[:, gs_], NT), 0.0)
                dbt_acc[:, g:g + 1] += jnp.sum(dmixed[:, gs_], axis=-1, keepdims=True)
                dvn_parts.append(_dot(ws[g], dmb[:, gs_], TN))
            dvn = jnp.concatenate(dvn_parts, axis=1)
            dlg_ref[...] += jnp.sum(dvn * xhat, axis=0, keepdims=True)
            dlb_ref[...] += jnp.sum(dvn, axis=0, keepdims=True)
            dxh = dvn * lg
            pend[:, VS0:VS0 + D_SGU] = (rstd * (dxh - jnp.mean(dxh, axis=-1, keepdims=True)
                                                - xhat * jnp.mean(dxh * xhat, axis=-1, keepdims=True))).astype(BF16)

        dz_ref[:, K0:GA0] = (carry[...] + dkv[0:BLOCK, :]).astype(BF16)
        carry[...] = dkv[BLOCK:2 * BLOCK, :]

        @pl.when(n == nb)
        def _():
            dsink_ref[...] = dsink_acc[:, 0:N_Q_HEADS]
            db_ref[...] = dbt_acc[...].T[0:SGU_GROUPS, :]

    cur = lambda n: (jnp.minimum(n, nb - 1), 0)
    const2 = lambda n: (0, 0)
    return pl.pallas_call(
        body, name="mixer_bwd", grid=(nb + 1,),
        out_shape=(jax.ShapeDtypeStruct((s, D_IN), BF16), jax.ShapeDtypeStruct((1, N_Q_HEADS), F32),
                   jax.ShapeDtypeStruct((1, D_SGU), F32), jax.ShapeDtypeStruct((1, D_SGU), F32),
                   jax.ShapeDtypeStruct((SGU_GROUPS, BLOCK, BLOCK), F32), jax.ShapeDtypeStruct((SGU_GROUPS, BLOCK), F32)),
        in_specs=[pl.BlockSpec((BLOCK, D_IN), cur),
                  pl.BlockSpec((BLOCK, 2 * D_KV), lambda n: (jnp.maximum(jnp.minimum(n, nb - 1) - 1, 0), K0 // (2 * D_KV))),
                  pl.BlockSpec((BLOCK, D_MODEL), cur), pl.BlockSpec((BLOCK, D_ATTN), cur),
                  pl.BlockSpec((1, N_Q_HEADS), const2), pl.BlockSpec((1, D_SGU), const2), pl.BlockSpec((1, D_SGU), const2),
                  pl.BlockSpec((SGU_GROUPS, BLOCK, BLOCK), lambda n: (0, 0, 0)), pl.BlockSpec((BLOCK, SGU_GROUPS), const2)],
        out_specs=(pl.BlockSpec((BLOCK, D_IN), lambda n: (jnp.maximum(n - 1, 0), 0)),
                   pl.BlockSpec((1, N_Q_HEADS), const2), pl.BlockSpec((1, D_SGU), const2), pl.BlockSpec((1, D_SGU), const2),
                   pl.BlockSpec((SGU_GROUPS, BLOCK, BLOCK), lambda n: (0, 0, 0)), pl.BlockSpec((SGU_GROUPS, BLOCK), const2)),
        scratch_shapes=[pltpu.VMEM((BLOCK, D_IN), BF16), pltpu.VMEM((2 * BLOCK, 2 * D_KV), F32), pltpu.VMEM((BLOCK, 2 * D_KV), F32),
                        pltpu.VMEM((1, BLOCK), F32), pltpu.VMEM((BLOCK, BLOCK), F32)],
        compiler_params=_params(("arbitrary",)),
    )(z, z, dcat, o, sinks, ln_g, ln_b, sgu_w, sgu_bt)


def _in_proj_bwd(dz, win_t, x, dx1, norm_g, mod):
    s, d = x.shape
    kdim = dz.shape[1]
    tm = min(256, s)

    def body(dz_ref, w_ref, x_ref, dx1_ref, g_ref, mod_ref, gx_ref, dshift_ref, dscale_ref, dng_ref):
        @pl.when(pl.program_id(0) == 0)
        def _():
            dshift_ref[...] = jnp.zeros_like(dshift_ref)
            dscale_ref[...] = jnp.zeros_like(dscale_ref)
            dng_ref[...] = jnp.zeros_like(dng_ref)

        dh = _dot(dz_ref[...], w_ref[...], NN)
        xv = x_ref[...]
        ng = g_ref[...]
        r = lax.rsqrt(jnp.mean(xv * xv, axis=-1, keepdims=True) + EPS)
        xn = xv * r
        dshift_ref[...] += jnp.sum(dh, axis=0, keepdims=True)
        dscale_ref[...] += jnp.sum(dh * (xn * ng), axis=0, keepdims=True)
        dhs = dh * (1.0 + mod_ref[:, d:2 * d])
        dng_ref[...] += jnp.sum(dhs * xn, axis=0, keepdims=True)
        dxn = dhs * ng
        gx_ref[...] = dx1_ref[...] + r * dxn - xn * (r * r * jnp.mean(dxn * xv, axis=-1, keepdims=True))

    row = lambda i: (i, 0)
    const = lambda i: (0, 0)
    return pl.pallas_call(
        body, name="in_proj_bwd", grid=(s // tm,),
        out_shape=(jax.ShapeDtypeStruct((s, d), F32),) + (jax.ShapeDtypeStruct((1, d), F32),) * 3,
        in_specs=[pl.BlockSpec((tm, kdim), row), pl.BlockSpec((kdim, d), const, pipeline_mode=pl.Buffered(1)),
                  pl.BlockSpec((tm, d), row), pl.BlockSpec((tm, d), row), pl.BlockSpec((1, d), const), pl.BlockSpec((1, 3 * d), const)],
        out_specs=(pl.BlockSpec((tm, d), row),) + (pl.BlockSpec((1, d), const),) * 3,
        compiler_params=_params(("arbitrary",)),
    )(dz, win_t, x, dx1, norm_g, mod)


def _sum_slots(slots, own, place, rows_tile, full_cols, name):
    _, r, hcols = slots.shape
    nblk = r // rows_tile

    def body(place_ref, s_ref, own_ref, o_ref):
        chip = place_ref[1]
        acc = jnp.zeros((rows_tile, hcols), F32)
        for k in range(N_CHIPS):
            acc = acc + jnp.where(chip == k, own_ref[...], s_ref[k]).astype(F32)
        o_ref[...] = acc

    return pl.pallas_call(
        body, name=name,
        grid_spec=pltpu.PrefetchScalarGridSpec(
            num_scalar_prefetch=1, grid=(nblk,),
            in_specs=[pl.BlockSpec((N_CHIPS, rows_tile, hcols), lambda i, p: (0, i, 0)),
                      pl.BlockSpec((rows_tile, hcols), lambda i, p: (p[1] * nblk + i, 0))],
            out_specs=pl.BlockSpec((rows_tile, hcols), lambda i, p: (i, p[0]))),
        out_shape=jax.ShapeDtypeStruct((r, full_cols), F32),
        compiler_params=_params(("parallel",)),
    )(place, slots, own)


def _adamw_rows(w, g, m, v, tile, name):
    r, c = w.shape

    def body(w_ref, g_ref, m_ref, v_ref, go_ref, d_ref, mo_ref, vo_ref):
        g_ = g_ref[...]
        delta, m2, v2 = _adamw(w_ref[...], g_, m_ref[...], v_ref[...])
        go_ref[...] = g_
        d_ref[...] = delta
        mo_ref[...] = m2
        vo_ref[...] = v2

    blk = pl.BlockSpec((tile, c), lambda i: (i, 0))
    return pl.pallas_call(
        body, name=name, grid=(r // tile,), out_shape=(jax.ShapeDtypeStruct((r, c), F32),) * 4,
        in_specs=[blk] * 4, out_specs=(blk,) * 4,
        compiler_params=_params(("parallel",)),
    )(w, g, m, v)


def _adamw_ada(cact, dmod, w, m, v, tile=256):
    r, c = w.shape

    def body(c_ref, d_ref, w_ref, m_ref, v_ref, go_ref, dl_ref, mo_ref, vo_ref):
        g_ = _dot(c_ref[...].astype(BF16), d_ref[...].astype(BF16), TN)
        delta, m2, v2 = _adamw(w_ref[...], g_, m_ref[...], v_ref[...])
        go_ref[...] = g_
        dl_ref[...] = delta
        mo_ref[...] = m2
        vo_ref[...] = v2

    blk = pl.BlockSpec((tile, c), lambda i: (i, 0))
    return pl.pallas_call(
        body, name="adamw_ada", grid=(r // tile,), out_shape=(jax.ShapeDtypeStruct((r, c), F32),) * 4,
        in_specs=[pl.BlockSpec((N_DEV, tile), lambda i: (0, i)), pl.BlockSpec((N_DEV, c), lambda i: (0, 0)), blk, blk, blk],
        out_specs=(blk,) * 4,
        compiler_params=_params(("parallel",)),
    )(cact, dmod, w, m, v)


def kernel(x, c, norm_g, w_ada, b_ada, w_in, attn_sinks, sgu_ln_g, sgu_ln_b, sgu_w, sgu_b, w_out, final_g, loss_target, m_norm_g, m_w_ada, m_b_ada, m_w_in, m_attn_sinks, m_sgu_ln_g, m_sgu_ln_b, m_sgu_w, m_sgu_b, m_w_out, m_final_g, v_norm_g, v_w_ada, v_b_ada, v_w_in, v_attn_sinks, v_sgu_ln_g, v_sgu_ln_b, v_sgu_w, v_sgu_b, v_w_out, v_final_g):
    xs, tgt = x[0], loss_target[0]
    core = lax.axis_index("c").astype(jnp.int32).reshape(1)
    chip = 2 * lax.axis_index("x") + lax.axis_index("y")
    place = jnp.stack([core[0], chip.astype(jnp.int32)])
    sgu_bt = sgu_b[0].T

    mod, cact = _mod_exchange(c, w_ada[0], b_ada)
    w_in_t, m_in_t, v_in_t = (jnp.swapaxes(a[0], 0, 1) for a in (w_in, m_w_in, v_w_in))
    wi = _cast_bf16(w_in_t, IN_SHARD // 4, "cast_w_in", mod)
    sems, bufs, token = _split_start(_gather_copies, [wi, lax.empty((D_IN, D_MODEL), BF16)], 3, "w_in_gather_start")
    h = _norm_modulate(xs, norm_g + token[0:1, 0:1], mod)
    wi, win_t = _split_wait(_gather_copies, sems, bufs, h, "w_in_gather_wait")
    win_t = _gather_finish(wi, win_t, "w_in_gather_finish")
    wo = _cast_bf16(w_out[0], OUT_SHARD // 2, "cast_w_out", win_t)
    sems, bufs, token = _split_start(_gather_copies, [wo, lax.empty((D_MODEL, D_MODEL), BF16)], 3, "w_out_gather_start")
    z = _matmul_nt(h, win_t, 1024, 768, F32, "in_proj", token)
    cat, o = _mixer_fwd(z, attn_sinks, sgu_ln_g, sgu_ln_b, sgu_w[0], sgu_bt)
    wo, wout = _split_wait(_gather_copies, sems, bufs, cat, "w_out_gather_wait")
    wout = _gather_finish(wo, wout, "w_out_gather_finish")
    dx1, dy, dfg, dgate, loss = _out_proj_loss(cat, wout, xs, tgt, mod, final_g.reshape(1, D_MODEL))

    so = _to_sibling(_grad_matmul(cat, dy, 1 - core, 512, "w_out_grad_sibling"), "w_out_grad_to_sibling")
    po = _grad_matmul(cat, dy, core, 512, "w_out_grad_own", add=so)
    sems, bufs, token = _split_start(_chip_copies, [po, lax.empty((N_CHIPS, OUT_SHARD, HALF), BF16)], 3, "w_out_grad_to_chips_start")
    dcat = _matmul_nt(dy, wout, 512, 1024, F32, "out_proj_bwd", token)
    dz, dsinks, dlg, dlb, dsgu_w, dsgu_b = _mixer_bwd(z, dcat, o, attn_sinks, sgu_ln_g, sgu_ln_b, sgu_w[0], sgu_bt)
    po, slots_o = _split_wait(_chip_copies, sems, bufs, dz, "w_out_grad_to_chips_wait")
    go = _sum_slots(slots_o, po, place, OUT_SHARD // 2, D_MODEL, "w_out_grad_sum")

    sw = _to_sibling(_grad_matmul(dz, h, 1 - core, 768, "w_in_grad_sibling"), "w_in_grad_to_sibling")
    pw = _grad_matmul(dz, h, core, 768, "w_in_grad_own", add=sw)
    sems, bufs, token = _split_start(_chip_copies, [pw, lax.empty((N_CHIPS, IN_SHARD, HALF), BF16)], 3, "w_in_grad_to_chips_start")
    gx, dshift, dscale, dng = _in_proj_bwd(dz, win_t, xs, dx1, norm_g + token[0:1, 0:1], mod)

    parts = dict(dng=dng, dshift=dshift, dscale=dscale, dgate=dgate, dfg=dfg, dlg=dlg, dlb=dlb, dsinks=dsinks,
                 dsgu_b=dsgu_b, dsgu_w=dsgu_w, loss=loss)
    small = dict(norm_g=(norm_g, m_norm_g, v_norm_g), b_ada=(b_ada, m_b_ada, v_b_ada),
                 attn_sinks=(attn_sinks, m_attn_sinks, v_attn_sinks), sgu_ln_g=(sgu_ln_g, m_sgu_ln_g, v_sgu_ln_g),
                 sgu_ln_b=(sgu_ln_b, m_sgu_ln_b, v_sgu_ln_b), sgu_w=(sgu_w, m_sgu_w, v_sgu_w), sgu_b=(sgu_b, m_sgu_b, v_sgu_b),
                 final_g=(final_g, m_final_g, v_final_g))
    res, dmod_all, total = _small_grads_step(parts, small)
    dmod_mine = lax.dynamic_slice_in_dim(dmod_all, ADA_SHARD * chip, ADA_SHARD, axis=1)
    res["w_ada"] = _adamw_ada(cact, dmod_mine, w_ada[0], m_w_ada[0], v_w_ada[0])

    pw, slots_w = _split_wait(_chip_copies, sems, bufs, res["w_ada"][1], "w_in_grad_to_chips_wait")
    gw_t, go = _grad_halves_swap(_sum_slots(slots_w, pw, place, IN_SHARD // 4, D_MODEL, "w_in_grad_sum"), go)
    res["w_in"] = tuple(jnp.swapaxes(a, 0, 1) for a in _adamw_rows(w_in_t, gw_t, m_in_t, v_in_t, IN_SHARD // 4, "adamw_w_in"))
    res["w_out"] = _adamw_rows(w_out[0], go, m_w_out[0], v_w_out[0], 256, "adamw_w_out")

    order = ["norm_g", "w_ada", "b_ada", "w_in", "attn_sinks", "sgu_ln_g", "sgu_ln_b", "sgu_w", "sgu_b", "w_out", "final_g"]
    like = dict(norm_g=norm_g, w_ada=w_ada, b_ada=b_ada, w_in=w_in, attn_sinks=attn_sinks, sgu_ln_g=sgu_ln_g, sgu_ln_b=sgu_ln_b,
                sgu_w=sgu_w, sgu_b=sgu_b, w_out=w_out, final_g=final_g)
    outs = [total[0, 0], gx[None]]
    for kind in range(4):
        outs += [res[n][kind].reshape(like[n].shape) for n in order]
    return tuple(outs)
```

```python
import functools
import math

import jax
import jax.numpy as jnp
from jax import lax
from jax.experimental import pallas as pl
from jax.experimental.pallas import tpu as pltpu

F32 = jnp.float32
BF16 = jnp.bfloat16

D_MODEL = 2048
D_ATTN = 1024
HEAD_DIM = 64
N_Q_HEADS = 16
Q_PER_KV = 8
D_KV = 128
BLOCK = 128
D_SGU = 1024
SGU_GROUPS = 8
GROUP_DIM = 128
D_IN = 5376
EPS = 1e-6
ATTN_SCALE = 1.0 / math.sqrt(HEAD_DIM)
N_CHIPS = 4
N_DEV = 8
IN_SHARD = D_IN // N_CHIPS
OUT_SHARD = D_MODEL // N_CHIPS
ADA_SHARD = 3 * D_MODEL // N_CHIPS
HALF = D_MODEL // 2

Q0, K0, V0, GA0, U0, VS0, GS0 = 0, 1024, 1152, 1280, 2304, 3328, 4352

ADAM_LR = 0.001
ADAM_B1 = 0.9
ADAM_B2 = 0.999
ADAM_EPS = 1e-08
ADAM_WD = 0.01
ADAM_STEP = 10

VMEM_LIMIT_V7X = 56 * 1024 * 1024

NN = (((1,), (0,)), ((), ()))
NT = (((1,), (1,)), ((), ()))
TN = (((0,), (0,)), ((), ()))
MESH = pl.DeviceIdType.MESH
ANY = pl.BlockSpec(memory_space=pl.ANY)
VMEM = pl.BlockSpec(memory_space=pltpu.VMEM)
HBM = pl.BlockSpec(memory_space=pltpu.HBM)
SEM = pl.BlockSpec(memory_space=pltpu.SEMAPHORE)
EFFECT = pltpu.SideEffectType.DATAFLOW_SIDE_EFFECTING


def _dot(a, b, dims):
    return lax.dot_general(a, b, dims, preferred_element_type=F32)


def _params(sem=None):
    return pltpu.CompilerParams(dimension_semantics=sem, vmem_limit_bytes=VMEM_LIMIT_V7X)


def _sigmoid(x):
    return jax.nn.sigmoid(x)


def _adamw(w, g, m, v):
    m = ADAM_B1 * m + (1.0 - ADAM_B1) * g
    v = ADAM_B2 * v + (1.0 - ADAM_B2) * (g * g)
    m_hat = m / (1.0 - ADAM_B1 ** ADAM_STEP)
    v_hat = v / (1.0 - ADAM_B2 ** ADAM_STEP)
    delta = -ADAM_LR * (m_hat / (jnp.sqrt(v_hat) + ADAM_EPS) + ADAM_WD * w)
    return delta, m, v


def _place():
    return lax.axis_index("x"), lax.axis_index("y"), lax.axis_index("c")


def _flip(p, bit):
    return 1 - p if bit else p


def _mod_exchange(c, w_ada, b_ada):
    kdim = w_ada.shape[0]
    kc = 256

    def body(c_ref, w_ref, b_ref, mod_ref, cact_ref, cbuf, pbuf, rbuf, send_sems, recv_sems):
        x, y, cc = _place()
        me = 4 * x + 2 * y + cc
        chip = 2 * x + y
        cbuf[me] = c_ref[...]
        gathers = []
        for k in range(1, N_DEV):
            bx, by, bc = (k >> 2) & 1, (k >> 1) & 1, k & 1
            cp = pltpu.make_async_remote_copy(
                src_ref=cbuf.at[me], dst_ref=cbuf.at[me], send_sem=send_sems.at[k - 1], recv_sem=recv_sems.at[k - 1],
                device_id=(_flip(x, bx), _flip(y, by), _flip(cc, bc)), device_id_type=MESH)
            cp.start()
            gathers.append(cp)
        for cp in gathers:
            cp.wait()
        for b in range(N_DEV):
            cv = cbuf[b]
            cact_ref[b:b + 1, :] = cv * _sigmoid(cv)
        acc = jnp.zeros((N_DEV, ADA_SHARD), F32)
        for k0 in range(0, kdim, kc):
            acc = acc + _dot(cact_ref[:, k0:k0 + kc].astype(BF16), w_ref[k0:k0 + kc, :].astype(BF16), NN)
        for b in range(N_DEV):
            pbuf[b] = acc[b:b + 1, :]
        rbuf[chip] = pbuf[me]
        sends = []
        for j, (bx, by) in enumerate(((0, 1), (1, 0), (1, 1))):
            px, py = _flip(x, bx), _flip(y, by)
            cp = pltpu.make_async_remote_copy(
                src_ref=pbuf.at[4 * px + 2 * py + cc], dst_ref=rbuf.at[chip],
                send_sem=send_sems.at[N_DEV - 1 + j], recv_sem=recv_sems.at[N_DEV - 1 + j],
                device_id=(px, py, cc), device_id_type=MESH)
            cp.start()
            sends.append(cp)
        for cp in sends:
            cp.wait()
        for k in range(N_CHIPS):
            cols = slice(ADA_SHARD * k, ADA_SHARD * (k + 1))
            mod_ref[:, cols] = rbuf[k] + b_ref[:, cols]

    return pl.pallas_call(
        body, name="mod_exchange",
        out_shape=(jax.ShapeDtypeStruct((1, 3 * D_MODEL), F32), jax.ShapeDtypeStruct((N_DEV, D_MODEL), F32)),
        in_specs=[VMEM, VMEM, VMEM], out_specs=(VMEM, VMEM),
        scratch_shapes=[pltpu.VMEM((N_DEV, 1, D_MODEL), F32), pltpu.VMEM((N_DEV, 1, ADA_SHARD), F32),
                        pltpu.VMEM((N_CHIPS, 1, ADA_SHARD), F32),
                        pltpu.SemaphoreType.DMA((N_DEV + 2,)), pltpu.SemaphoreType.DMA((N_DEV + 2,))],
        compiler_params=_params(),
    )(c, w_ada, b_ada)


def _other_chips(x, y):
    return [(_flip(x, bx), _flip(y, by)) for bx, by in ((0, 1), (1, 0), (1, 1))]


def _split_start(copies, bufs, n_sems, name, after):
    n = len(bufs)

    def body(*refs):
        send_sems, recv_sems = refs[n + 1:n + 3]
        for cp in copies(*refs[:n], send_sems, recv_sems):
            cp.start()
        refs[-1][...] = jnp.zeros_like(refs[-1])

    bufs = [pltpu.with_memory_space_constraint(a, pltpu.HBM) for a in bufs]
    outs = pl.pallas_call(
        body, name=name,
        out_shape=(pltpu.SemaphoreType.DMA((n_sems,)), pltpu.SemaphoreType.DMA((n_sems,)))
        + tuple(pltpu.HBM(a.shape, a.dtype) for a in bufs) + (jax.ShapeDtypeStruct((8, 128), F32),),
        in_specs=[HBM] * n + [ANY], out_specs=(SEM, SEM) + (HBM,) * n + (VMEM,), input_output_aliases={k: k + 2 for k in range(n)},
        compiler_params=pltpu.CompilerParams(has_side_effects=EFFECT),
    )(*bufs, after)
    return outs[:2], list(outs[2:2 + n]), outs[-1]


def _split_wait(copies, sems, bufs, after, name):
    n = len(bufs)

    def body(*refs):
        send_sems, recv_sems = refs[n:n + 2]
        for cp in copies(*refs[:n], send_sems, recv_sems):
            cp.wait_send()
            cp.wait_recv()

    return pl.pallas_call(
        body, name=name, out_shape=tuple(pltpu.HBM(a.shape, a.dtype) for a in bufs),
        in_specs=[HBM] * n + [SEM, SEM, ANY], out_specs=(HBM,) * n, input_output_aliases={k: k for k in range(n)},
        compiler_params=pltpu.CompilerParams(has_side_effects=EFFECT),
    )(*bufs, *sems, after)


def _gather_copies(full_ref, send_sems, recv_sems):
    x, y, cc = _place()
    shard = full_ref.shape[0] // N_CHIPS
    half = shard // 2
    mine = full_ref.at[pl.ds(shard * (2 * x + y) + half * cc, half), :]
    return [pltpu.make_async_remote_copy(src_ref=mine, dst_ref=mine, send_sem=send_sems.at[j], recv_sem=recv_sems.at[j],
                                         device_id=(px, py, cc), device_id_type=MESH)
            for j, (px, py) in enumerate(_other_chips(x, y))]


def _gather_finish(full, name):
    shard = full.shape[0] // N_CHIPS
    half = shard // 2

    def body(full_in, full_ref, send_sems, recv_sems):
        x, y, cc = _place()
        cps = []
        for j, (px, py) in enumerate(_other_chips(x, y)):
            landed = full_ref.at[pl.ds(shard * (2 * px + py) + half * cc, half), :]
            cps.append(pltpu.make_async_remote_copy(src_ref=landed, dst_ref=landed, send_sem=send_sems.at[j], recv_sem=recv_sems.at[j],
                                                    device_id=(x, y, 1 - cc), device_id_type=MESH))
        for cp in cps:
            cp.start()
        for cp in cps:
            cp.wait()

    return pl.pallas_call(
        body, name=name, out_shape=jax.ShapeDtypeStruct(full.shape, full.dtype),
        in_specs=[ANY], out_specs=ANY, input_output_aliases={0: 0},
        scratch_shapes=[pltpu.SemaphoreType.DMA((3,)), pltpu.SemaphoreType.DMA((3,))],
        compiler_params=_params(),
    )(full)


def _to_sibling(g, name):
    def body(g_ref, r_ref, send_sem, recv_sem):
        x, y, cc = _place()
        cp = pltpu.make_async_remote_copy(src_ref=g_ref, dst_ref=r_ref, send_sem=send_sem, recv_sem=recv_sem,
                                          device_id=(x, y, 1 - cc), device_id_type=MESH)
        cp.start()
        cp.wait()

    return pl.pallas_call(
        body, name=name, out_shape=jax.ShapeDtypeStruct(g.shape, g.dtype), in_specs=[ANY], out_specs=ANY,
        scratch_shapes=[pltpu.SemaphoreType.DMA, pltpu.SemaphoreType.DMA], compiler_params=_params(),
    )(g)


def _chip_copies(pieces, p_ref, land_ref, send_sems, recv_sems):
    x, y, cc = _place()
    chip = 2 * x + y
    shard = land_ref.shape[1]
    others = _other_chips(x, y)
    cps = []
    for k, (j, r0, nr) in enumerate(pieces):
        px, py = others[j]
        cps.append(pltpu.make_async_remote_copy(
            src_ref=p_ref.at[pl.ds(shard * (2 * px + py) + r0, nr), :], dst_ref=land_ref.at[chip, pl.ds(r0, nr), :],
            send_sem=send_sems.at[k], recv_sem=recv_sems.at[k], device_id=(px, py, cc), device_id_type=MESH))
    return cps


def _halves_swap(f, name):
    def body(f_in, f_ref, send_sem, recv_sem):
        x, y, cc = _place()
        mine = f_ref.at[:, pl.ds(pl.multiple_of(HALF * cc, 128), HALF)]
        cp = pltpu.make_async_remote_copy(src_ref=mine, dst_ref=mine, send_sem=send_sem, recv_sem=recv_sem,
                                          device_id=(x, y, 1 - cc), device_id_type=MESH)
        cp.start()
        cp.wait()

    return pl.pallas_call(
        body, name=name, out_shape=jax.ShapeDtypeStruct(f.shape, f.dtype),
        in_specs=[ANY], out_specs=ANY, input_output_aliases={0: 0},
        scratch_shapes=[pltpu.SemaphoreType.DMA, pltpu.SemaphoreType.DMA], compiler_params=_params(),
    )(f)


ROW_NORM_G, ROW_DMOD, ROW_FINAL_G, ROW_LN, ROW_SINKS, ROW_LOSS, ROW_SGU_B, MISC_ROWS = 0, 1, 4, 5, 6, 7, 8, 16


def _small_grads_step(parts, small, after):
    names = ["norm_g", "b_ada", "attn_sinks", "sgu_ln_g", "sgu_ln_b", "sgu_w", "sgu_b", "final_g"]
    part_names = ["dng", "dshift", "dscale", "dgate", "dfg", "dlg", "dlb", "dsinks", "dsgu_b", "dsgu_w", "loss"]
    n_parts = len(part_names)

    def body(*refs):
        p = dict(zip(part_names, refs[:n_parts]))
        wmv = {n: refs[n_parts + 3 * i: n_parts + 3 * i + 3] for i, n in enumerate(names)}
        outs = refs[n_parts + 3 * len(names) + 1:]
        res = {n: outs[4 * i: 4 * i + 4] for i, n in enumerate(names)}
        dmod_ref, loss_ref = outs[4 * len(names):4 * len(names) + 2]
        misc, wbuf, cmisc, cwbuf, dbuf, send_sems, recv_sems = outs[4 * len(names) + 2:]
        x, y, cc = _place()
        me = 4 * x + 2 * y + cc
        chip = 2 * x + y

        def remote(k, src, dst, peer):
            return pltpu.make_async_remote_copy(src_ref=src, dst_ref=dst, send_sem=send_sems.at[k], recv_sem=recv_sems.at[k],
                                                device_id=peer, device_id_type=MESH)

        dbuf[me] = jnp.zeros((8, D_MODEL), F32)
        dbuf[me, 0:1, :] = p["dshift"][...]
        dbuf[me, 1:2, :] = p["dscale"][...]
        dbuf[me, 2:3, :] = p["dgate"][...]
        cps = []
        for k in range(1, N_DEV):
            bx, by, bc = (k >> 2) & 1, (k >> 1) & 1, k & 1
            cps.append(remote(k - 1, dbuf.at[me], dbuf.at[me], (_flip(x, bx), _flip(y, by), _flip(cc, bc))))

        misc[cc] = jnp.zeros((MISC_ROWS, D_MODEL), F32)
        misc[cc, ROW_NORM_G:ROW_NORM_G + 1, :] = p["dng"][...]
        misc[cc, ROW_DMOD:ROW_DMOD + 1, :] = p["dshift"][...]
        misc[cc, ROW_DMOD + 1:ROW_DMOD + 2, :] = p["dscale"][...]
        misc[cc, ROW_DMOD + 2:ROW_DMOD + 3, :] = p["dgate"][...]
        misc[cc, ROW_FINAL_G:ROW_FINAL_G + 1, :] = p["dfg"][...]
        misc[cc, ROW_LN:ROW_LN + 1, 0:D_SGU] = p["dlg"][...]
        misc[cc, ROW_LN:ROW_LN + 1, D_SGU:2 * D_SGU] = p["dlb"][...]
        misc[cc, ROW_SINKS:ROW_SINKS + 1, 0:N_Q_HEADS] = p["dsinks"][...]
        misc[cc, ROW_LOSS:ROW_LOSS + 1, 0:1] = p["loss"][...]
        misc[cc, ROW_SGU_B:ROW_SGU_B + SGU_GROUPS, 0:BLOCK] = p["dsgu_b"][...]
        for g in range(SGU_GROUPS):
            wbuf[cc, BLOCK * g:BLOCK * (g + 1), :] = p["dsgu_w"][g]
        pair = [remote(N_DEV - 1 + a, buf.at[cc], buf.at[cc], (x, y, 1 - cc)) for a, buf in enumerate((misc, wbuf))]
        for cp in cps + pair:
            cp.start()
        for cp in pair:
            cp.wait()
        cmisc[chip] = misc[0] + misc[1]
        cwbuf[chip] = wbuf[0] + wbuf[1]
        quad = []
        for j, (bx, by) in enumerate(((0, 1), (1, 0), (1, 1))):
            for a, buf in enumerate((cmisc, cwbuf)):
                quad.append(remote(N_DEV + 1 + 2 * j + a, buf.at[chip], buf.at[chip], (_flip(x, bx), _flip(y, by), cc)))
        for cp in quad:
            cp.start()
        for cp in cps + quad:
            cp.wait()

        tot = cmisc[0]
        for k in range(1, N_CHIPS):
            tot = tot + cmisc[k]
        loss_ref[...] = tot[ROW_LOSS:ROW_LOSS + 1, 0:1]
        for b in range(N_DEV):
            dmod_ref[b:b + 1, :] = jnp.concatenate([dbuf[b, t:t + 1, :] for t in range(3)], axis=1)

        def update(name, g, pick=lambda r: r[...], put=None):
            w_ref, m_ref, v_ref = wmv[name]
            delta, m2, v2 = _adamw(pick(w_ref), g, pick(m_ref), pick(v_ref))
            for ref, val in zip(res[name], (g, delta, m2, v2)):
                if put is None:
                    ref[...] = val
                else:
                    put(ref, val)

        update("norm_g", tot[ROW_NORM_G:ROW_NORM_G + 1, :])
        update("final_g", tot[ROW_FINAL_G:ROW_FINAL_G + 1, :])
        for t in range(3):
            cols = slice(D_MODEL * t, D_MODEL * (t + 1))

            def put_cols(ref, val, cols=cols):
                ref[:, cols] = val
            update("b_ada", tot[ROW_DMOD + t:ROW_DMOD + t + 1, :], pick=lambda r, cols=cols: r[:, cols], put=put_cols)
        update("sgu_ln_g", tot[ROW_LN:ROW_LN + 1, 0:D_SGU])
        update("sgu_ln_b", tot[ROW_LN:ROW_LN + 1, D_SGU:2 * D_SGU])
        update("attn_sinks", tot[ROW_SINKS:ROW_SINKS + 1, 0:N_Q_HEADS])
        update("sgu_b", tot[ROW_SGU_B:ROW_SGU_B + SGU_GROUPS, 0:BLOCK])
        for g in range(SGU_GROUPS):
            rows = slice(BLOCK * g, BLOCK * (g + 1))
            gw = cwbuf[0, rows, :]
            for k in range(1, N_CHIPS):
                gw = gw + cwbuf[k, rows, :]

            def put_group(ref, val, g=g):
                ref[g] = val
            update("sgu_w", gw, pick=lambda r, g=g: r[g], put=put_group)

    shapes = {"norm_g": (1, D_MODEL), "b_ada": (1, 3 * D_MODEL), "attn_sinks": (1, N_Q_HEADS), "sgu_ln_g": (1, D_SGU),
              "sgu_ln_b": (1, D_SGU), "sgu_w": (SGU_GROUPS, BLOCK, BLOCK), "sgu_b": (SGU_GROUPS, BLOCK), "final_g": (1, D_MODEL)}
    operands = [parts[n] for n in part_names]
    for n in names:
        operands += [a.reshape(shapes[n]) for a in small[n]]
    out_shape = []
    for n in names:
        out_shape += [jax.ShapeDtypeStruct(shapes[n], F32)] * 4
    out_shape += [jax.ShapeDtypeStruct((N_DEV, 3 * D_MODEL), F32), jax.ShapeDtypeStruct((1, 1), F32)]
    n_sems = (N_DEV - 1) + 2 + 2 * (N_CHIPS - 1)
    outs = pl.pallas_call(
        body, name="small_grads_step", out_shape=tuple(out_shape),
        in_specs=[VMEM] * len(operands) + [ANY], out_specs=tuple([VMEM] * len(out_shape)),
        scratch_shapes=[pltpu.VMEM((2, MISC_ROWS, D_MODEL), F32), pltpu.VMEM((2, SGU_GROUPS * BLOCK, BLOCK), F32),
                        pltpu.VMEM((N_CHIPS, MISC_ROWS, D_MODEL), F32), pltpu.VMEM((N_CHIPS, SGU_GROUPS * BLOCK, BLOCK), F32),
                        pltpu.VMEM((N_DEV, 8, D_MODEL), F32),
                        pltpu.SemaphoreType.DMA((n_sems,)), pltpu.SemaphoreType.DMA((n_sems,))],
        compiler_params=_params(),
    )(*operands, after)
    res = {n: outs[4 * i: 4 * i + 4] for i, n in enumerate(names)}
    return res, outs[4 * len(names)], outs[4 * len(names) + 1]


def _cast_into_gathered(w, tile, place, name, after):
    r, c = w.shape
    nblk = r // tile

    def body(place_ref, w_ref, after_ref, o_ref):
        o_ref[...] = w_ref[...].astype(BF16)

    return pl.pallas_call(
        body, name=name,
        grid_spec=pltpu.PrefetchScalarGridSpec(
            num_scalar_prefetch=1, grid=(nblk,), in_specs=[pl.BlockSpec((tile, c), lambda i, p: (i, 0)), ANY],
            out_specs=pl.BlockSpec((tile, c), lambda i, p: (p[1] * nblk + i, 0))),
        out_shape=jax.ShapeDtypeStruct((N_CHIPS * r, c), BF16),
        compiler_params=_params(("parallel",)),
    )(place, w, after)


def _norm_modulate(x, norm_g, mod):
    s, d = x.shape
    tm = min(256, s)

    def body(x_ref, g_ref, mod_ref, h_ref):
        xv = x_ref[...]
        r = lax.rsqrt(jnp.mean(xv * xv, axis=-1, keepdims=True) + EPS)
        h = (xv * r * g_ref[...]) * (1.0 + mod_ref[:, d:2 * d]) + mod_ref[:, 0:d]
        h_ref[...] = h.astype(BF16)

    return pl.pallas_call(
        body, name="norm_modulate", grid=(s // tm,), out_shape=jax.ShapeDtypeStruct((s, d), BF16),
        in_specs=[pl.BlockSpec((tm, d), lambda i: (i, 0)), pl.BlockSpec((1, d), lambda i: (0, 0)),
                  pl.BlockSpec((1, 3 * d), lambda i: (0, 0))],
        out_specs=pl.BlockSpec((tm, d), lambda i: (i, 0)),
        compiler_params=_params(("parallel",)),
    )(x, norm_g, mod)


def _matmul_nt(a, bt, tm, tn, out_dtype, name, after):
    m, k = a.shape
    n = bt.shape[0]
    tm, tn = min(tm, m), min(tn, n)

    def body(a_ref, b_ref, after_ref, o_ref):
        o_ref[...] = _dot(a_ref[...], b_ref[...], NT).astype(out_dtype)

    return pl.pallas_call(
        body, name=name, grid=(m // tm, n // tn), out_shape=jax.ShapeDtypeStruct((m, n), out_dtype),
        in_specs=[pl.BlockSpec((tm, k), lambda i, j: (i, 0)), pl.BlockSpec((tn, k), lambda i, j: (j, 0)), ANY],
        out_specs=pl.BlockSpec((tm, tn), lambda i, j: (i, j)),
        compiler_params=_params(("parallel", "parallel")),
    )(a, bt, after)


def _grad_matmul(a, b, half, tm, name, add=None):
    t, m = a.shape
    tm = min(tm, m)

    def body(half_ref, a_ref, b_ref, *rest):
        if add is None:
            (o_ref,) = rest
            o_ref[...] = _dot(a_ref[...], b_ref[...], TN).astype(BF16)
        else:
            add_ref, o_ref = rest
            o_ref[...] = (_dot(a_ref[...], b_ref[...], TN) + add_ref[...].astype(F32)).astype(BF16)

    in_specs = [pl.BlockSpec((t, tm), lambda i, h: (0, i)), pl.BlockSpec((t, HALF), lambda i, h: (0, h[0]))]
    operands = [a, b]
    if add is not None:
        in_specs.append(pl.BlockSpec((tm, HALF), lambda i, h: (i, 0)))
        operands.append(add)
    return pl.pallas_call(
        body, name=name,
        grid_spec=pltpu.PrefetchScalarGridSpec(num_scalar_prefetch=1, grid=(m // tm,), in_specs=in_specs,
                                               out_specs=pl.BlockSpec((tm, HALF), lambda i, h: (i, 0))),
        out_shape=jax.ShapeDtypeStruct((m, HALF), BF16),
        compiler_params=_params(("parallel",)),
    )(half, *operands)


def _mixer_masks(n):
    row = lax.broadcasted_iota(jnp.int32, (Q_PER_KV * BLOCK, 2 * BLOCK), 0) & (BLOCK - 1)
    col = lax.broadcasted_iota(jnp.int32, (Q_PER_KV * BLOCK, 2 * BLOCK), 1)
    valid = (col > row) & (col <= row + BLOCK) & ((col >= BLOCK) | (n > 0))
    r2 = lax.broadcasted_iota(jnp.int32, (BLOCK, BLOCK), 0)
    c2 = lax.broadcasted_iota(jnp.int32, (BLOCK, BLOCK), 1)
    return valid, r2 >= c2


def _layer_norm_parts(vs):
    mu = jnp.mean(vs, axis=-1, keepdims=True)
    xc = vs - mu
    rstd = lax.rsqrt(jnp.mean(xc * xc, axis=-1, keepdims=True) + EPS)
    return xc * rstd, rstd


def _stack_heads(ref, base, hk):
    first = base + HEAD_DIM * Q_PER_KV * hk
    return jnp.concatenate([ref[:, first + HEAD_DIM * g:first + HEAD_DIM * (g + 1)] for g in range(Q_PER_KV)], axis=0)


def _unstack_heads(ref, base, hk, val):
    first = base + HEAD_DIM * Q_PER_KV * hk
    for g in range(Q_PER_KV):
        ref[:, first + HEAD_DIM * g:first + HEAD_DIM * (g + 1)] = val[BLOCK * g:BLOCK * (g + 1), :].astype(ref.dtype)


def _stack_sinks(sink_ref, hk):
    return jnp.concatenate([jnp.broadcast_to(sink_ref[:, h:h + 1], (BLOCK, 1))
                            for h in range(Q_PER_KV * hk, Q_PER_KV * (hk + 1))], axis=0)


def _attn_probs(qs, kh, valid, sk):
    s = _dot(qs, kh, NT) * ATTN_SCALE
    s = jnp.where(valid, s, -jnp.inf)
    m = jnp.maximum(jnp.max(s, axis=-1, keepdims=True), sk)
    e = jnp.exp(s - m)
    es = jnp.exp(sk - m)
    inv = 1.0 / (jnp.sum(e, axis=-1, keepdims=True) + es)
    return e * inv, es * inv


def _kv_ext(z_ref, kvp_ref):
    k_ext = jnp.concatenate([kvp_ref[:, 0:D_KV], z_ref[:, K0:K0 + D_KV]], axis=0).astype(BF16)
    v_ext = jnp.concatenate([kvp_ref[:, D_KV:2 * D_KV], z_ref[:, V0:V0 + D_KV]], axis=0).astype(BF16)
    return k_ext, v_ext


def _sgu_mixed(w_ref, bt_ref, vn, tril):
    ws = [jnp.where(tril, w_ref[g], 0.0).astype(BF16) for g in range(SGU_GROUPS)]
    mixed = jnp.concatenate([_dot(ws[g], vn[:, GROUP_DIM * g:GROUP_DIM * (g + 1)], NN) + bt_ref[:, g:g + 1]
                             for g in range(SGU_GROUPS)], axis=1)
    return ws, mixed


def _mixer_fwd(z, sinks, ln_g, ln_b, sgu_w, sgu_bt):
    s = z.shape[0]
    nb = s // BLOCK

    def body(z_ref, kvp_ref, sink_ref, lg_ref, lb_ref, w_ref, bt_ref, cat_ref, o_ref):
        n = pl.program_id(0)
        valid, tril = _mixer_masks(n)
        k_ext, v_ext = _kv_ext(z_ref, kvp_ref)
        for hk in range(2):
            kv = slice(HEAD_DIM * hk, HEAD_DIM * (hk + 1))
            p, _ = _attn_probs(_stack_heads(z_ref, Q0, hk).astype(BF16), k_ext[:, kv], valid, _stack_sinks(sink_ref, hk))
            o = _dot(p.astype(BF16), v_ext[:, kv], NN)
            ga = _stack_heads(z_ref, GA0, hk)
            _unstack_heads(o_ref, 0, hk, o)
            _unstack_heads(cat_ref, 0, hk, o * (ga * _sigmoid(ga)))
        xhat, _ = _layer_norm_parts(z_ref[:, VS0:VS0 + D_SGU])
        vn = (xhat * lg_ref[...] + lb_ref[...]).astype(BF16)
        _, mixed = _sgu_mixed(w_ref, bt_ref, vn, tril)
        gt = z_ref[:, GS0:GS0 + D_SGU]
        cat_ref[:, D_ATTN:D_MODEL] = ((z_ref[:, U0:U0 + D_SGU] * mixed) * (gt * _sigmoid(gt))).astype(BF16)

    const2 = lambda n: (0, 0)
    return pl.pallas_call(
        body, name="mixer_fwd", grid=(nb,),
        out_shape=(jax.ShapeDtypeStruct((s, D_MODEL), BF16), jax.ShapeDtypeStruct((s, D_ATTN), F32)),
        in_specs=[pl.BlockSpec((BLOCK, D_IN), lambda n: (n, 0)),
                  pl.BlockSpec((BLOCK, 2 * D_KV), lambda n: (jnp.maximum(n - 1, 0), K0 // (2 * D_KV))),
                  pl.BlockSpec((1, N_Q_HEADS), const2), pl.BlockSpec((1, D_SGU), const2), pl.BlockSpec((1, D_SGU), const2),
                  pl.BlockSpec((SGU_GROUPS, BLOCK, BLOCK), lambda n: (0, 0, 0)), pl.BlockSpec((BLOCK, SGU_GROUPS), const2)],
        out_specs=(pl.BlockSpec((BLOCK, D_MODEL), lambda n: (n, 0)), pl.BlockSpec((BLOCK, D_ATTN), lambda n: (n, 0))),
        compiler_params=_params(("parallel",)),
    )(z, z, sinks, ln_g, ln_b, sgu_w, sgu_bt)


def _out_proj_loss(cat, wout, x, target, mod, final_g):
    s, d = x.shape
    tm = min(256, s)

    def body(cat_ref, w_ref, x_ref, t_ref, mod_ref, fg_ref, dx1_ref, dy_ref, dfg_ref, dgate_ref, loss_ref):
        i = pl.program_id(0)
        gate = mod_ref[:, 2 * d:3 * d]
        fg = fg_ref[...]
        yv = _dot(cat_ref[...], w_ref[...], NN)
        x1 = x_ref[...] + gate * yv
        r = lax.rsqrt(jnp.mean(x1 * x1, axis=-1, keepdims=True) + EPS)
        xr = x1 * r
        err = xr * fg - t_ref[...]
        dout = err * (1.0 / d)
        gy = dout * fg
        dx1 = r * gy - xr * (r * r * jnp.mean(gy * x1, axis=-1, keepdims=True))
        dx1_ref[...] = dx1
        dy_ref[...] = (gate * dx1).astype(BF16)

        @pl.when(i == 0)
        def _():
            dfg_ref[...] = jnp.zeros_like(dfg_ref)
            dgate_ref[...] = jnp.zeros_like(dgate_ref)
            loss_ref[...] = jnp.zeros_like(loss_ref)

        dfg_ref[...] += jnp.sum(dout * xr, axis=0, keepdims=True)
        dgate_ref[...] += jnp.sum(dx1 * yv, axis=0, keepdims=True)
        loss_ref[...] += 0.5 * jnp.sum(jnp.mean(err * err, axis=-1, keepdims=True), axis=0, keepdims=True)

    row = lambda i: (i, 0)
    const = lambda i: (0, 0)
    return pl.pallas_call(
        body, name="out_proj_loss", grid=(s // tm,),
        out_shape=(jax.ShapeDtypeStruct((s, d), F32), jax.ShapeDtypeStruct((s, d), BF16), jax.ShapeDtypeStruct((1, d), F32),
                   jax.ShapeDtypeStruct((1, d), F32), jax.ShapeDtypeStruct((1, 1), F32)),
        in_specs=[pl.BlockSpec((tm, d), row), pl.BlockSpec((d, d), const), pl.BlockSpec((tm, d), row), pl.BlockSpec((tm, d), row),
                  pl.BlockSpec((1, 3 * d), const), pl.BlockSpec((1, d), const)],
        out_specs=(pl.BlockSpec((tm, d), row), pl.BlockSpec((tm, d), row), pl.BlockSpec((1, d), const), pl.BlockSpec((1, d), const),
                   pl.BlockSpec((1, 1), const)),
        compiler_params=_params(("arbitrary",)),
    )(cat, wout, x, target, mod, final_g)


def _mixer_bwd(z, dcat, o, sinks, ln_g, ln_b, sgu_w, sgu_bt):
    s = z.shape[0]
    nb = s // BLOCK

    def body(z_ref, kvp_ref, dc_ref, o_ref, sink_ref, lg_ref, lb_ref, w_ref, bt_ref,
             dz_ref, dsink_ref, dlg_ref, dlb_ref, dw_ref, db_ref, pend, dkv, carry, dsink_acc, dbt_acc):
        n = pl.program_id(0)

        @pl.when(n == 0)
        def _():
            pend[...] = jnp.zeros_like(pend)
            carry[...] = jnp.zeros_like(carry)
            dsink_acc[...] = jnp.zeros_like(dsink_acc)
            dbt_acc[...] = jnp.zeros_like(dbt_acc)
            dlg_ref[...] = jnp.zeros_like(dlg_ref)
            dlb_ref[...] = jnp.zeros_like(dlb_ref)
            dw_ref[...] = jnp.zeros_like(dw_ref)

        dz_ref[:, 0:K0] = pend[:, 0:K0]
        dz_ref[:, GA0:D_IN] = pend[:, GA0:D_IN]

        @pl.when(n == nb)
        def _():
            dkv[...] = jnp.zeros_like(dkv)

        @pl.when(n < nb)
        def _():
            valid, tril = _mixer_masks(n)
            k_ext, v_ext = _kv_ext(z_ref, kvp_ref)
            for hk in range(2):
                kv = slice(HEAD_DIM * hk, HEAD_DIM * (hk + 1))
                kh, vh = k_ext[:, kv], v_ext[:, kv]
                qs = _stack_heads(z_ref, Q0, hk).astype(BF16)
                p, p_sink = _attn_probs(qs, kh, valid, _stack_sinks(sink_ref, hk))
                ga = _stack_heads(z_ref, GA0, hk)
                sg = _sigmoid(ga)
                datt = _stack_heads(dc_ref, 0, hk)
                do = (datt * (ga * sg)).astype(BF16)
                _unstack_heads(pend, GA0, hk, datt * _stack_heads(o_ref, 0, hk) * (sg * (1.0 + ga * (1.0 - sg))))
                dp = _dot(do, vh, NT)
                delta = jnp.sum(p * dp, axis=-1, keepdims=True)
                ds = (p * (dp - delta) * ATTN_SCALE).astype(BF16)
                dsk = -p_sink * delta
                for gq in range(Q_PER_KV):
                    h = hk * Q_PER_KV + gq
                    dsink_acc[:, h:h + 1] += jnp.sum(dsk[BLOCK * gq:BLOCK * (gq + 1), :], axis=0, keepdims=True)
                _unstack_heads(pend, Q0, hk, _dot(ds, kh, NN))
                dkv[:, kv] = _dot(ds, qs, TN)
                dkv[:, D_KV + HEAD_DIM * hk:D_KV + HEAD_DIM * (hk + 1)] = _dot(p.astype(BF16), do, TN)

            xhat, rstd = _layer_norm_parts(z_ref[:, VS0:VS0 + D_SGU])
            lg = lg_ref[...]
            vn = (xhat * lg + lb_ref[...]).astype(BF16)
            ws, mixed = _sgu_mixed(w_ref, bt_ref, vn, tril)
            u = z_ref[:, U0:U0 + D_SGU]
            gt = z_ref[:, GS0:GS0 + D_SGU]
            sg = _sigmoid(gt)
            dsgu = dc_ref[:, D_ATTN:D_MODEL]
            dgated = dsgu * (gt * sg)
            pend[:, U0:U0 + D_SGU] = (dgated * mixed).astype(BF16)
            pend[:, GS0:GS0 + D_SGU] = (dsgu * (u * mixed) * (sg * (1.0 + gt * (1.0 - sg)))).astype(BF16)
            dmixed = dgated * u
            dmb = dmixed.astype(BF16)
            dvn_parts = []
            for g in range(SGU_GROUPS):
                gs_ = slice(GROUP_DIM * g, GROUP_DIM * (g + 1))
                dw_ref[g] += jnp.where(tril, _dot(dmb[:, gs_], vn[:, gs_], NT), 0.0)
                dbt_acc[:, g:g + 1] += jnp.sum(dmixed[:, gs_], axis=-1, keepdims=True)
                dvn_parts.append(_dot(ws[g], dmb[:, gs_], TN))
            dvn = jnp.concatenate(dvn_parts, axis=1)
            dlg_ref[...] += jnp.sum(dvn * xhat, axis=0, keepdims=True)
            dlb_ref[...] += jnp.sum(dvn, axis=0, keepdims=True)
            dxh = dvn * lg
            pend[:, VS0:VS0 + D_SGU] = (rstd * (dxh - jnp.mean(dxh, axis=-1, keepdims=True)
                                                - xhat * jnp.mean(dxh * xhat, axis=-1, keepdims=True))).astype(BF16)

        dz_ref[:, K0:GA0] = (carry[...] + dkv[0:BLOCK, :]).astype(BF16)
        carry[...] = dkv[BLOCK:2 * BLOCK, :]

        @pl.when(n == nb)
        def _():
            dsink_ref[...] = dsink_acc[:, 0:N_Q_HEADS]
            db_ref[...] = dbt_acc[...].T[0:SGU_GROUPS, :]

    cur = lambda n: (jnp.minimum(n, nb - 1), 0)
    const2 = lambda n: (0, 0)
    return pl.pallas_call(
        body, name="mixer_bwd", grid=(nb + 1,),
        out_shape=(jax.ShapeDtypeStruct((s, D_IN), BF16), jax.ShapeDtypeStruct((1, N_Q_HEADS), F32),
                   jax.ShapeDtypeStruct((1, D_SGU), F32), jax.ShapeDtypeStruct((1, D_SGU), F32),
                   jax.ShapeDtypeStruct((SGU_GROUPS, BLOCK, BLOCK), F32), jax.ShapeDtypeStruct((SGU_GROUPS, BLOCK), F32)),
        in_specs=[pl.BlockSpec((BLOCK, D_IN), cur),
                  pl.BlockSpec((BLOCK, 2 * D_KV), lambda n: (jnp.maximum(jnp.minimum(n, nb - 1) - 1, 0), K0 // (2 * D_KV))),
                  pl.BlockSpec((BLOCK, D_MODEL), cur), pl.BlockSpec((BLOCK, D_ATTN), cur),
                  pl.BlockSpec((1, N_Q_HEADS), const2), pl.BlockSpec((1, D_SGU), const2), pl.BlockSpec((1, D_SGU), const2),
                  pl.BlockSpec((SGU_GROUPS, BLOCK, BLOCK), lambda n: (0, 0, 0)), pl.BlockSpec((BLOCK, SGU_GROUPS), const2)],
        out_specs=(pl.BlockSpec((BLOCK, D_IN), lambda n: (jnp.maximum(n - 1, 0), 0)),
                   pl.BlockSpec((1, N_Q_HEADS), const2), pl.BlockSpec((1, D_SGU), const2), pl.BlockSpec((1, D_SGU), const2),
                   pl.BlockSpec((SGU_GROUPS, BLOCK, BLOCK), lambda n: (0, 0, 0)), pl.BlockSpec((SGU_GROUPS, BLOCK), const2)),
        scratch_shapes=[pltpu.VMEM((BLOCK, D_IN), BF16), pltpu.VMEM((2 * BLOCK, 2 * D_KV), F32), pltpu.VMEM((BLOCK, 2 * D_KV), F32),
                        pltpu.VMEM((1, BLOCK), F32), pltpu.VMEM((BLOCK, BLOCK), F32)],
        compiler_params=_params(("arbitrary",)),
    )(z, z, dcat, o, sinks, ln_g, ln_b, sgu_w, sgu_bt)


def _in_proj_bwd(dz, win_t, x, dx1, norm_g, mod):
    s, d = x.shape
    kdim = dz.shape[1]
    tm = min(256, s)

    def body(dz_ref, w_ref, x_ref, dx1_ref, g_ref, mod_ref, gx_ref, dshift_ref, dscale_ref, dng_ref):
        @pl.when(pl.program_id(0) == 0)
        def _():
            dshift_ref[...] = jnp.zeros_like(dshift_ref)
            dscale_ref[...] = jnp.zeros_like(dscale_ref)
            dng_ref[...] = jnp.zeros_like(dng_ref)

        dh = _dot(dz_ref[...], w_ref[...], NN)
        xv = x_ref[...]
        ng = g_ref[...]
        r = lax.rsqrt(jnp.mean(xv * xv, axis=-1, keepdims=True) + EPS)
        xn = xv * r
        dshift_ref[...] += jnp.sum(dh, axis=0, keepdims=True)
        dscale_ref[...] += jnp.sum(dh * (xn * ng), axis=0, keepdims=True)
        dhs = dh * (1.0 + mod_ref[:, d:2 * d])
        dng_ref[...] += jnp.sum(dhs * xn, axis=0, keepdims=True)
        dxn = dhs * ng
        gx_ref[...] = dx1_ref[...] + r * dxn - xn * (r * r * jnp.mean(dxn * xv, axis=-1, keepdims=True))

    row = lambda i: (i, 0)
    const = lambda i: (0, 0)
    return pl.pallas_call(
        body, name="in_proj_bwd", grid=(s // tm,),
        out_shape=(jax.ShapeDtypeStruct((s, d), F32),) + (jax.ShapeDtypeStruct((1, d), F32),) * 3,
        in_specs=[pl.BlockSpec((tm, kdim), row), pl.BlockSpec((kdim, d), const, pipeline_mode=pl.Buffered(1)),
                  pl.BlockSpec((tm, d), row), pl.BlockSpec((tm, d), row), pl.BlockSpec((1, d), const), pl.BlockSpec((1, 3 * d), const)],
        out_specs=(pl.BlockSpec((tm, d), row),) + (pl.BlockSpec((1, d), const),) * 3,
        compiler_params=_params(("arbitrary",)),
    )(dz, win_t, x, dx1, norm_g, mod)


def _sum_slots(slots, own, place, rows_tile, full_cols, name):
    _, r, hcols = slots.shape
    nblk = r // rows_tile

    def body(place_ref, s_ref, own_ref, o_ref):
        chip = place_ref[1]
        acc = jnp.zeros((rows_tile, hcols), F32)
        for k in range(N_CHIPS):
            acc = acc + jnp.where(chip == k, own_ref[...], s_ref[k]).astype(F32)
        o_ref[...] = acc

    return pl.pallas_call(
        body, name=name,
        grid_spec=pltpu.PrefetchScalarGridSpec(
            num_scalar_prefetch=1, grid=(nblk,),
            in_specs=[pl.BlockSpec((N_CHIPS, rows_tile, hcols), lambda i, p: (0, i, 0)),
                      pl.BlockSpec((rows_tile, hcols), lambda i, p: (p[1] * nblk + i, 0))],
            out_specs=pl.BlockSpec((rows_tile, hcols), lambda i, p: (i, p[0]))),
        out_shape=jax.ShapeDtypeStruct((r, full_cols), F32),
        compiler_params=_params(("parallel",)),
    )(place, slots, own)


def _adamw_rows(w, g, m, v, tile, name, after):
    r, c = w.shape

    def body(w_ref, g_ref, m_ref, v_ref, after_ref, go_ref, d_ref, mo_ref, vo_ref):
        g_ = g_ref[...]
        delta, m2, v2 = _adamw(w_ref[...], g_, m_ref[...], v_ref[...])
        go_ref[...] = g_
        d_ref[...] = delta
        mo_ref[...] = m2
        vo_ref[...] = v2

    blk = pl.BlockSpec((tile, c), lambda i: (i, 0))
    return pl.pallas_call(
        body, name=name, grid=(r // tile,), out_shape=(jax.ShapeDtypeStruct((r, c), F32),) * 4,
        in_specs=[blk] * 4 + [ANY], out_specs=(blk,) * 4,
        compiler_params=_params(("parallel",)),
    )(w, g, m, v, after)


def _adamw_ada(cact, dmod, w, m, v, tile=256):
    r, c = w.shape

    def body(c_ref, d_ref, w_ref, m_ref, v_ref, go_ref, dl_ref, mo_ref, vo_ref):
        g_ = _dot(c_ref[...].astype(BF16), d_ref[...].astype(BF16), TN)
        delta, m2, v2 = _adamw(w_ref[...], g_, m_ref[...], v_ref[...])
        go_ref[...] = g_
        dl_ref[...] = delta
        mo_ref[...] = m2
        vo_ref[...] = v2

    blk = pl.BlockSpec((tile, c), lambda i: (i, 0))
    return pl.pallas_call(
        body, name="adamw_ada", grid=(r // tile,), out_shape=(jax.ShapeDtypeStruct((r, c), F32),) * 4,
        in_specs=[pl.BlockSpec((N_DEV, tile), lambda i: (0, i)), pl.BlockSpec((N_DEV, c), lambda i: (0, 0)), blk, blk, blk],
        out_specs=(blk,) * 4,
        compiler_params=_params(("parallel",)),
    )(cact, dmod, w, m, v)


def kernel(x, c, norm_g, w_ada, b_ada, w_in, attn_sinks, sgu_ln_g, sgu_ln_b, sgu_w, sgu_b, w_out, final_g, loss_target, m_norm_g, m_w_ada, m_b_ada, m_w_in, m_attn_sinks, m_sgu_ln_g, m_sgu_ln_b, m_sgu_w, m_sgu_b, m_w_out, m_final_g, v_norm_g, v_w_ada, v_b_ada, v_w_in, v_attn_sinks, v_sgu_ln_g, v_sgu_ln_b, v_sgu_w, v_sgu_b, v_w_out, v_final_g):
    xs, tgt = x[0], loss_target[0]
    core = lax.axis_index("c").astype(jnp.int32).reshape(1)
    chip = 2 * lax.axis_index("x") + lax.axis_index("y")
    place = jnp.stack([core[0], chip.astype(jnp.int32)])
    sgu_bt = sgu_b[0].T

    mod, cact = _mod_exchange(c, w_ada[0], b_ada)
    w_in_t, m_in_t, v_in_t = (jnp.swapaxes(a[0], 0, 1) for a in (w_in, m_w_in, v_w_in))
    win_t = _cast_into_gathered(w_in_t, IN_SHARD // 4, place, "cast_w_in", mod)
    sems, bufs, token = _split_start(_gather_copies, [win_t], 3, "w_in_gather_start", mod)
    h = _norm_modulate(xs, norm_g + token[0:1, 0:1], mod)
    win_t = _gather_finish(_split_wait(_gather_copies, sems, bufs, h, "w_in_gather_wait")[0], "w_in_gather_finish")
    wout = _cast_into_gathered(w_out[0], OUT_SHARD // 2, place, "cast_w_out", win_t)
    sems, bufs, token = _split_start(_gather_copies, [wout], 3, "w_out_gather_start", win_t)
    z = _matmul_nt(h, win_t, 1024, 768, F32, "in_proj", token)
    cat, o = _mixer_fwd(z, attn_sinks, sgu_ln_g, sgu_ln_b, sgu_w[0], sgu_bt)
    wout = _gather_finish(_split_wait(_gather_copies, sems, bufs, cat, "w_out_gather_wait")[0], "w_out_gather_finish")
    dx1, dy, dfg, dgate, loss = _out_proj_loss(cat, wout, xs, tgt, mod, final_g.reshape(1, D_MODEL))

    so = _to_sibling(_grad_matmul(cat, dy, 1 - core, 512, "w_out_grad_sibling"), "w_out_grad_to_sibling")
    po = _grad_matmul(cat, dy, core, 512, "w_out_grad_own", add=so)
    whole_o = functools.partial(_chip_copies, [(j, 0, OUT_SHARD) for j in range(3)])
    sems, bufs, token = _split_start(whole_o, [po, lax.empty((N_CHIPS, OUT_SHARD, HALF), BF16)], 3, "w_out_grad_to_chips_start", po)
    dcat = _matmul_nt(dy, wout, 512, 1024, F32, "out_proj_bwd", token)
    dz, dsinks, dlg, dlb, dsgu_w, dsgu_b = _mixer_bwd(z, dcat, o, attn_sinks, sgu_ln_g, sgu_ln_b, sgu_w[0], sgu_bt)
    po, slots_o = _split_wait(whole_o, sems, bufs, dz, "w_out_grad_to_chips_wait")
    go = _halves_swap(_sum_slots(slots_o, po, place, OUT_SHARD // 2, D_MODEL, "w_out_grad_sum"), "w_out_grad_halves_swap")

    sw = _to_sibling(_grad_matmul(dz, h, 1 - core, 768, "w_in_grad_sibling"), "w_in_grad_to_sibling")
    pw = _grad_matmul(dz, h, core, 768, "w_in_grad_own", add=sw)
    first = functools.partial(_chip_copies, [(0, 0, IN_SHARD), (1, 0, IN_SHARD), (2, 0, IN_SHARD // 2)])
    rest = functools.partial(_chip_copies, [(2, IN_SHARD // 2, IN_SHARD // 2)])
    sems, bufs, token = _split_start(first, [pw, lax.empty((N_CHIPS, IN_SHARD, HALF), BF16)], 3, "w_in_grad_to_chips_start", pw)
    gx, dshift, dscale, dng = _in_proj_bwd(dz, win_t, xs, dx1, norm_g + token[0:1, 0:1], mod)
    bufs = _split_wait(first, sems, bufs, gx, "w_in_grad_to_chips_wait")

    parts = dict(dng=dng, dshift=dshift, dscale=dscale, dgate=dgate, dfg=dfg, dlg=dlg, dlb=dlb, dsinks=dsinks,
                 dsgu_b=dsgu_b, dsgu_w=dsgu_w, loss=loss)
    small = dict(norm_g=(norm_g, m_norm_g, v_norm_g), b_ada=(b_ada, m_b_ada, v_b_ada),
                 attn_sinks=(attn_sinks, m_attn_sinks, v_attn_sinks), sgu_ln_g=(sgu_ln_g, m_sgu_ln_g, v_sgu_ln_g),
                 sgu_ln_b=(sgu_ln_b, m_sgu_ln_b, v_sgu_ln_b), sgu_w=(sgu_w, m_sgu_w, v_sgu_w), sgu_b=(sgu_b, m_sgu_b, v_sgu_b),
                 final_g=(final_g, m_final_g, v_final_g))
    res, dmod_all, total = _small_grads_step(parts, small, bufs[1])
    sems, bufs, token = _split_start(rest, list(bufs), 1, "w_in_grad_rest_to_chips_start", total)
    dmod_mine = lax.dynamic_slice_in_dim(dmod_all, ADA_SHARD * chip, ADA_SHARD, axis=1) + token[0:1, 0:1]
    res["w_ada"] = _adamw_ada(cact, dmod_mine, w_ada[0], m_w_ada[0], v_w_ada[0])
    res["w_out"] = _adamw_rows(w_out[0], go, m_w_out[0], v_w_out[0], 256, "adamw_w_out", res["w_ada"][1])

    pw, slots_w = _split_wait(rest, sems, bufs, res["w_out"][1], "w_in_grad_rest_to_chips_wait")
    gw_t = _halves_swap(_sum_slots(slots_w, pw, place, IN_SHARD // 4, D_MODEL, "w_in_grad_sum"), "w_in_grad_halves_swap")
    res["w_in"] = tuple(jnp.swapaxes(a, 0, 1) for a in _adamw_rows(w_in_t, gw_t, m_in_t, v_in_t, IN_SHARD // 4, "adamw_w_in", gw_t))

    order = ["norm_g", "w_ada", "b_ada", "w_in", "attn_sinks", "sgu_ln_g", "sgu_ln_b", "sgu_w", "sgu_b", "w_out", "final_g"]
    like = dict(norm_g=norm_g, w_ada=w_ada, b_ada=b_ada, w_in=w_in, attn_sinks=attn_sinks, sgu_ln_g=sgu_ln_g, sgu_ln_b=sgu_ln_b,
                sgu_w=sgu_w, sgu_b=sgu_b, w_out=w_out, final_g=final_g)
    outs = [total[0, 0], gx[None]]
    for kind in range(4):
        outs += [res[n][kind].reshape(like[n].shape) for n in order]
    return tuple(outs)
```

```python
import functools
import math

import jax
import jax.numpy as jnp
from jax import lax
from jax.experimental import pallas as pl
from jax.experimental.pallas import tpu as pltpu

F32 = jnp.float32
BF16 = jnp.bfloat16

D_MODEL = 2048
D_ATTN = 1024
HEAD_DIM = 64
N_Q_HEADS = 16
Q_PER_KV = 8
D_KV = 128
BLOCK = 128
D_SGU = 1024
SGU_GROUPS = 8
GROUP_DIM = 128
D_IN = 5376
EPS = 1e-6
ATTN_SCALE = 1.0 / math.sqrt(HEAD_DIM)
N_CHIPS = 4
N_DEV = 8
IN_SHARD = D_IN // N_CHIPS
OUT_SHARD = D_MODEL // N_CHIPS
ADA_SHARD = 3 * D_MODEL // N_CHIPS
HALF = D_MODEL // 2

Q0, K0, V0, GA0, U0, VS0, GS0 = 0, 1024, 1152, 1280, 2304, 3328, 4352

ADAM_LR = 0.001
ADAM_B1 = 0.9
ADAM_B2 = 0.999
ADAM_EPS = 1e-08
ADAM_WD = 0.01
ADAM_STEP = 10

VMEM_LIMIT_V7X = 56 * 1024 * 1024

NN = (((1,), (0,)), ((), ()))
NT = (((1,), (1,)), ((), ()))
TN = (((0,), (0,)), ((), ()))
MESH = pl.DeviceIdType.MESH
ANY = pl.BlockSpec(memory_space=pl.ANY)
VMEM = pl.BlockSpec(memory_space=pltpu.VMEM)
HBM = pl.BlockSpec(memory_space=pltpu.HBM)
SEM = pl.BlockSpec(memory_space=pltpu.SEMAPHORE)
EFFECT = pltpu.SideEffectType.DATAFLOW_SIDE_EFFECTING


def _dot(a, b, dims):
    return lax.dot_general(a, b, dims, preferred_element_type=F32)


def _params(sem=None):
    return pltpu.CompilerParams(dimension_semantics=sem, vmem_limit_bytes=VMEM_LIMIT_V7X)


def _sigmoid(x):
    return jax.nn.sigmoid(x)


def _adamw(w, g, m, v):
    m = ADAM_B1 * m + (1.0 - ADAM_B1) * g
    v = ADAM_B2 * v + (1.0 - ADAM_B2) * (g * g)
    m_hat = m / (1.0 - ADAM_B1 ** ADAM_STEP)
    v_hat = v / (1.0 - ADAM_B2 ** ADAM_STEP)
    delta = -ADAM_LR * (m_hat / (jnp.sqrt(v_hat) + ADAM_EPS) + ADAM_WD * w)
    return delta, m, v


def _place():
    return lax.axis_index("x"), lax.axis_index("y"), lax.axis_index("c")


def _flip(p, bit):
    return 1 - p if bit else p


def _mod_exchange(c, w_ada, b_ada):
    kdim = w_ada.shape[0]
    kc = 256

    def body(c_ref, w_ref, b_ref, mod_ref, cact_ref, cbuf, pbuf, rbuf, send_sems, recv_sems):
        x, y, cc = _place()
        me = 4 * x + 2 * y + cc
        chip = 2 * x + y
        cbuf[me] = c_ref[...]
        gathers = []
        for k in range(1, N_DEV):
            bx, by, bc = (k >> 2) & 1, (k >> 1) & 1, k & 1
            cp = pltpu.make_async_remote_copy(
                src_ref=cbuf.at[me], dst_ref=cbuf.at[me], send_sem=send_sems.at[k - 1], recv_sem=recv_sems.at[k - 1],
                device_id=(_flip(x, bx), _flip(y, by), _flip(cc, bc)), device_id_type=MESH)
            cp.start()
            gathers.append(cp)
        for cp in gathers:
            cp.wait()
        for b in range(N_DEV):
            cv = cbuf[b]
            cact_ref[b:b + 1, :] = cv * _sigmoid(cv)
        acc = jnp.zeros((N_DEV, ADA_SHARD), F32)
        for k0 in range(0, kdim, kc):
            acc = acc + _dot(cact_ref[:, k0:k0 + kc].astype(BF16), w_ref[k0:k0 + kc, :].astype(BF16), NN)
        for b in range(N_DEV):
            pbuf[b] = acc[b:b + 1, :]
        rbuf[chip] = pbuf[me]
        sends = []
        for j, (bx, by) in enumerate(((0, 1), (1, 0), (1, 1))):
            px, py = _flip(x, bx), _flip(y, by)
            cp = pltpu.make_async_remote_copy(
                src_ref=pbuf.at[4 * px + 2 * py + cc], dst_ref=rbuf.at[chip],
                send_sem=send_sems.at[N_DEV - 1 + j], recv_sem=recv_sems.at[N_DEV - 1 + j],
                device_id=(px, py, cc), device_id_type=MESH)
            cp.start()
            sends.append(cp)
        for cp in sends:
            cp.wait()
        for k in range(N_CHIPS):
            cols = slice(ADA_SHARD * k, ADA_SHARD * (k + 1))
            mod_ref[:, cols] = rbuf[k] + b_ref[:, cols]

    return pl.pallas_call(
        body, name="mod_exchange",
        out_shape=(jax.ShapeDtypeStruct((1, 3 * D_MODEL), F32), jax.ShapeDtypeStruct((N_DEV, D_MODEL), F32)),
        in_specs=[VMEM, VMEM, VMEM], out_specs=(VMEM, VMEM),
        scratch_shapes=[pltpu.VMEM((N_DEV, 1, D_MODEL), F32), pltpu.VMEM((N_DEV, 1, ADA_SHARD), F32),
                        pltpu.VMEM((N_CHIPS, 1, ADA_SHARD), F32),
                        pltpu.SemaphoreType.DMA((N_DEV + 2,)), pltpu.SemaphoreType.DMA((N_DEV + 2,))],
        compiler_params=_params(),
    )(c, w_ada, b_ada)


def _other_chips(x, y):
    return [(_flip(x, bx), _flip(y, by)) for bx, by in ((0, 1), (1, 0), (1, 1))]


def _split_start(copies, bufs, n_sems, name, after):
    n = len(bufs)

    def body(*refs):
        send_sems, recv_sems = refs[n + 1:n + 3]
        for cp in copies(*refs[:n], send_sems, recv_sems):
            cp.start()
        refs[-1][...] = jnp.zeros_like(refs[-1])

    bufs = [pltpu.with_memory_space_constraint(a, pltpu.HBM) for a in bufs]
    outs = pl.pallas_call(
        body, name=name,
        out_shape=(pltpu.SemaphoreType.DMA((n_sems,)), pltpu.SemaphoreType.DMA((n_sems,)))
        + tuple(pltpu.HBM(a.shape, a.dtype) for a in bufs) + (jax.ShapeDtypeStruct((8, 128), F32),),
        in_specs=[HBM] * n + [ANY], out_specs=(SEM, SEM) + (HBM,) * n + (VMEM,), input_output_aliases={k: k + 2 for k in range(n)},
        compiler_params=pltpu.CompilerParams(has_side_effects=EFFECT),
    )(*bufs, after)
    return outs[:2], list(outs[2:2 + n]), outs[-1]


def _split_wait(copies, sems, bufs, after, name):
    n = len(bufs)

    def body(*refs):
        send_sems, recv_sems = refs[n:n + 2]
        for cp in copies(*refs[:n], send_sems, recv_sems):
            cp.wait_send()
            cp.wait_recv()

    return pl.pallas_call(
        body, name=name, out_shape=tuple(pltpu.HBM(a.shape, a.dtype) for a in bufs),
        in_specs=[HBM] * n + [SEM, SEM, ANY], out_specs=(HBM,) * n, input_output_aliases={k: k for k in range(n)},
        compiler_params=pltpu.CompilerParams(has_side_effects=EFFECT),
    )(*bufs, *sems, after)


def _gather_copies(full_ref, send_sems, recv_sems):
    x, y, cc = _place()
    shard = full_ref.shape[0] // N_CHIPS
    half = shard // 2
    mine = full_ref.at[pl.ds(shard * (2 * x + y) + half * cc, half), :]
    return [pltpu.make_async_remote_copy(src_ref=mine, dst_ref=mine, send_sem=send_sems.at[j], recv_sem=recv_sems.at[j],
                                         device_id=(px, py, cc), device_id_type=MESH)
            for j, (px, py) in enumerate(_other_chips(x, y))]


def _gather_routed(full, name):
    shard = full.shape[0] // N_CHIPS
    half = shard // 2
    quarter = half // 2

    def body(full_in, full_ref, send_sems, recv_sems):
        x, y, cc = _place()
        across_x, across_y, far = (1 - x, y), (x, 1 - y), (1 - x, 1 - y)

        def rows(chip_xy, core, q):
            return full_ref.at[pl.ds(shard * (2 * chip_xy[0] + chip_xy[1]) + half * core + quarter * q, quarter), :]

        def copy(k, piece, to):
            return pltpu.make_async_remote_copy(src_ref=piece, dst_ref=piece, send_sem=send_sems.at[k], recv_sem=recv_sems.at[k],
                                                device_id=to, device_id_type=MESH)

        me = (x, y)
        direct = [copy(0, rows(me, cc, 0), (*across_x, cc)), copy(2, rows(me, cc, 1), (*across_y, cc)),
                  copy(1, rows(me, cc, 1), (*across_x, cc)), copy(3, rows(me, cc, 0), (*across_y, cc))]
        for cp in direct:
            cp.start()
        landing = [(0, across_x, 0, (4, (*across_y, cc))), (2, across_y, 1, (5, (*across_x, cc))),
                   (1, across_x, 1, None), (3, across_y, 0, None), (4, far, 0, None), (5, far, 1, None)]
        passed = []
        for n, (k, whose, q, onward) in enumerate(landing):
            piece = rows(whose, cc, q)
            copy(k, piece, (x, y, cc)).wait_recv()
            if onward is not None:
                passed.append(copy(onward[0], piece, onward[1]))
                passed[-1].start()
            passed.append(copy(6 + n, piece, (x, y, 1 - cc)))
            passed[-1].start()
        for n, (k, whose, q, onward) in enumerate(landing):
            copy(6 + n, rows(whose, 1 - cc, q), (x, y, cc)).wait_recv()
        for cp in direct + passed:
            cp.wait_send()

    return pl.pallas_call(
        body, name=name, out_shape=jax.ShapeDtypeStruct(full.shape, full.dtype),
        in_specs=[ANY], out_specs=ANY, input_output_aliases={0: 0},
        scratch_shapes=[pltpu.SemaphoreType.DMA((12,)), pltpu.SemaphoreType.DMA((12,))],
        compiler_params=_params(),
    )(full)


def _gather_finish(full, name):
    shard = full.shape[0] // N_CHIPS
    half = shard // 2

    def body(full_in, full_ref, send_sems, recv_sems):
        x, y, cc = _place()
        cps = []
        for j, (px, py) in enumerate(_other_chips(x, y)):
            landed = full_ref.at[pl.ds(shard * (2 * px + py) + half * cc, half), :]
            cps.append(pltpu.make_async_remote_copy(src_ref=landed, dst_ref=landed, send_sem=send_sems.at[j], recv_sem=recv_sems.at[j],
                                                    device_id=(x, y, 1 - cc), device_id_type=MESH))
        for cp in cps:
            cp.start()
        for cp in cps:
            cp.wait()

    return pl.pallas_call(
        body, name=name, out_shape=jax.ShapeDtypeStruct(full.shape, full.dtype),
        in_specs=[ANY], out_specs=ANY, input_output_aliases={0: 0},
        scratch_shapes=[pltpu.SemaphoreType.DMA((3,)), pltpu.SemaphoreType.DMA((3,))],
        compiler_params=_params(),
    )(full)


def _to_sibling(g, name):
    def body(g_ref, r_ref, send_sem, recv_sem):
        x, y, cc = _place()
        cp = pltpu.make_async_remote_copy(src_ref=g_ref, dst_ref=r_ref, send_sem=send_sem, recv_sem=recv_sem,
                                          device_id=(x, y, 1 - cc), device_id_type=MESH)
        cp.start()
        cp.wait()

    return pl.pallas_call(
        body, name=name, out_shape=jax.ShapeDtypeStruct(g.shape, g.dtype), in_specs=[ANY], out_specs=ANY,
        scratch_shapes=[pltpu.SemaphoreType.DMA, pltpu.SemaphoreType.DMA], compiler_params=_params(),
    )(g)


def _chip_copies(pieces, p_ref, land_ref, send_sems, recv_sems):
    x, y, cc = _place()
    chip = 2 * x + y
    shard = land_ref.shape[1]
    others = _other_chips(x, y)
    cps = []
    for k, (j, r0, nr) in enumerate(pieces):
        px, py = others[j]
        cps.append(pltpu.make_async_remote_copy(
            src_ref=p_ref.at[pl.ds(shard * (2 * px + py) + r0, nr), :], dst_ref=land_ref.at[chip, pl.ds(r0, nr), :],
            send_sem=send_sems.at[k], recv_sem=recv_sems.at[k], device_id=(px, py, cc), device_id_type=MESH))
    return cps


def _halves_swap(f, name):
    def body(f_in, f_ref, send_sem, recv_sem):
        x, y, cc = _place()
        mine = f_ref.at[:, pl.ds(pl.multiple_of(HALF * cc, 128), HALF)]
        cp = pltpu.make_async_remote_copy(src_ref=mine, dst_ref=mine, send_sem=send_sem, recv_sem=recv_sem,
                                          device_id=(x, y, 1 - cc), device_id_type=MESH)
        cp.start()
        cp.wait()

    return pl.pallas_call(
        body, name=name, out_shape=jax.ShapeDtypeStruct(f.shape, f.dtype),
        in_specs=[ANY], out_specs=ANY, input_output_aliases={0: 0},
        scratch_shapes=[pltpu.SemaphoreType.DMA, pltpu.SemaphoreType.DMA], compiler_params=_params(),
    )(f)


ROW_NORM_G, ROW_DMOD, ROW_FINAL_G, ROW_LN, ROW_SINKS, ROW_LOSS, ROW_SGU_B, MISC_ROWS = 0, 1, 4, 5, 6, 7, 8, 16


def _small_grads_sum(parts, after):
    part_names = ["dng", "dshift", "dscale", "dgate", "dfg", "dlg", "dlb", "dsinks", "dsgu_b", "dsgu_w", "loss"]
    n_parts = len(part_names)
    n_peers = N_DEV - 1

    def body(*refs):
        p = dict(zip(part_names, refs[:n_parts]))
        tot_ref, gw_ref, dmod_ref = refs[n_parts + 1:n_parts + 4]
        misc, cmisc, dbuf, wland, wsum, send_sems, recv_sems = refs[n_parts + 4:]
        x, y, cc = _place()
        me = 4 * x + 2 * y + cc
        chip = 2 * x + y
        peers = [(_flip(x, (k >> 2) & 1), _flip(y, (k >> 1) & 1), _flip(cc, k & 1)) for k in range(1, N_DEV)]

        def remote(k, src, dst, peer):
            return pltpu.make_async_remote_copy(src_ref=src, dst_ref=dst, send_sem=send_sems.at[k], recv_sem=recv_sems.at[k],
                                                device_id=peer, device_id_type=MESH)

        dbuf[me] = jnp.zeros((8, D_MODEL), F32)
        dbuf[me, 0:1, :] = p["dshift"][...]
        dbuf[me, 1:2, :] = p["dscale"][...]
        dbuf[me, 2:3, :] = p["dgate"][...]
        gathers = [remote(k, dbuf.at[me], dbuf.at[me], peer) for k, peer in enumerate(peers)]
        wland[me] = p["dsgu_w"][me]
        scatters = [remote(n_peers + k, p["dsgu_w"].at[4 * px + 2 * py + pc], wland.at[me], (px, py, pc))
                    for k, (px, py, pc) in enumerate(peers)]

        misc[cc] = jnp.zeros((MISC_ROWS, D_MODEL), F32)
        misc[cc, ROW_NORM_G:ROW_NORM_G + 1, :] = p["dng"][...]
        misc[cc, ROW_DMOD:ROW_DMOD + 1, :] = p["dshift"][...]
        misc[cc, ROW_DMOD + 1:ROW_DMOD + 2, :] = p["dscale"][...]
        misc[cc, ROW_DMOD + 2:ROW_DMOD + 3, :] = p["dgate"][...]
        misc[cc, ROW_FINAL_G:ROW_FINAL_G + 1, :] = p["dfg"][...]
        misc[cc, ROW_LN:ROW_LN + 1, 0:D_SGU] = p["dlg"][...]
        misc[cc, ROW_LN:ROW_LN + 1, D_SGU:2 * D_SGU] = p["dlb"][...]
        misc[cc, ROW_SINKS:ROW_SINKS + 1, 0:N_Q_HEADS] = p["dsinks"][...]
        misc[cc, ROW_LOSS:ROW_LOSS + 1, 0:1] = p["loss"][...]
        misc[cc, ROW_SGU_B:ROW_SGU_B + SGU_GROUPS, 0:BLOCK] = p["dsgu_b"][...]
        pair = remote(3 * n_peers, misc.at[cc], misc.at[cc], (x, y, 1 - cc))
        for cp in gathers + scatters + [pair]:
            cp.start()
        pair.wait()
        cmisc[chip] = misc[0] + misc[1]
        quad = [remote(3 * n_peers + 1 + j, cmisc.at[chip], cmisc.at[chip], (_flip(x, bx), _flip(y, by), cc))
                for j, (bx, by) in enumerate(((0, 1), (1, 0), (1, 1)))]
        for cp in quad:
            cp.start()
        for cp in scatters:
            cp.wait()
        group = wland[0]
        for b in range(1, N_DEV):
            group = group + wland[b]
        wsum[me] = group
        spreads = [remote(2 * n_peers + k, wsum.at[me], wsum.at[me], peer) for k, peer in enumerate(peers)]
        for cp in spreads:
            cp.start()
        for cp in gathers + quad + spreads:
            cp.wait()

        tot = cmisc[0]
        for k in range(1, N_CHIPS):
            tot = tot + cmisc[k]
        tot_ref[...] = tot
        gw_ref[...] = wsum[...]
        for b in range(N_DEV):
            dmod_ref[b:b + 1, :] = jnp.concatenate([dbuf[b, t:t + 1, :] for t in range(3)], axis=1)

    operands = [parts[n] for n in part_names]
    n_sems = 3 * n_peers + 1 + (N_CHIPS - 1)
    return pl.pallas_call(
        body, name="small_grads_sum",
        out_shape=(jax.ShapeDtypeStruct((MISC_ROWS, D_MODEL), F32), jax.ShapeDtypeStruct((SGU_GROUPS, BLOCK, BLOCK), F32),
                   jax.ShapeDtypeStruct((N_DEV, 3 * D_MODEL), F32)),
        in_specs=[VMEM] * len(operands) + [ANY], out_specs=(VMEM, VMEM, VMEM),
        scratch_shapes=[pltpu.VMEM((2, MISC_ROWS, D_MODEL), F32), pltpu.VMEM((N_CHIPS, MISC_ROWS, D_MODEL), F32),
                        pltpu.VMEM((N_DEV, 8, D_MODEL), F32), pltpu.VMEM((N_DEV, BLOCK, BLOCK), F32),
                        pltpu.VMEM((SGU_GROUPS, BLOCK, BLOCK), F32),
                        pltpu.SemaphoreType.DMA((n_sems,)), pltpu.SemaphoreType.DMA((n_sems,))],
        compiler_params=_params(),
    )(*operands, after)


def _small_adamw(tot, gw, small):
    names = ["norm_g", "b_ada", "attn_sinks", "sgu_ln_g", "sgu_ln_b", "sgu_w", "sgu_b", "final_g"]

    def body(*refs):
        tot_ref, gw_ref = refs[:2]
        wmv = {n: refs[2 + 3 * i: 5 + 3 * i] for i, n in enumerate(names)}
        outs = refs[2 + 3 * len(names):]
        res = {n: outs[4 * i: 4 * i + 4] for i, n in enumerate(names)}
        outs[4 * len(names)][...] = tot_ref[ROW_LOSS:ROW_LOSS + 1, 0:1]

        def update(name, g, pick=lambda r: r[...], put=None):
            w_ref, m_ref, v_ref = wmv[name]
            delta, m2, v2 = _adamw(pick(w_ref), g, pick(m_ref), pick(v_ref))
            for ref, val in zip(res[name], (g, delta, m2, v2)):
                if put is None:
                    ref[...] = val
                else:
                    put(ref, val)

        update("norm_g", tot_ref[ROW_NORM_G:ROW_NORM_G + 1, :])
        update("final_g", tot_ref[ROW_FINAL_G:ROW_FINAL_G + 1, :])
        for t in range(3):
            cols = slice(D_MODEL * t, D_MODEL * (t + 1))

            def put_cols(ref, val, cols=cols):
                ref[:, cols] = val
            update("b_ada", tot_ref[ROW_DMOD + t:ROW_DMOD + t + 1, :], pick=lambda r, cols=cols: r[:, cols], put=put_cols)
        update("sgu_ln_g", tot_ref[ROW_LN:ROW_LN + 1, 0:D_SGU])
        update("sgu_ln_b", tot_ref[ROW_LN:ROW_LN + 1, D_SGU:2 * D_SGU])
        update("attn_sinks", tot_ref[ROW_SINKS:ROW_SINKS + 1, 0:N_Q_HEADS])
        update("sgu_b", tot_ref[ROW_SGU_B:ROW_SGU_B + SGU_GROUPS, 0:BLOCK])
        for g in range(SGU_GROUPS):
            def put_group(ref, val, g=g):
                ref[g] = val
            update("sgu_w", gw_ref[g], pick=lambda r, g=g: r[g], put=put_group)

    shapes = {"norm_g": (1, D_MODEL), "b_ada": (1, 3 * D_MODEL), "attn_sinks": (1, N_Q_HEADS), "sgu_ln_g": (1, D_SGU),
              "sgu_ln_b": (1, D_SGU), "sgu_w": (SGU_GROUPS, BLOCK, BLOCK), "sgu_b": (SGU_GROUPS, BLOCK), "final_g": (1, D_MODEL)}
    operands = [tot, gw]
    for n in names:
        operands += [a.reshape(shapes[n]) for a in small[n]]
    out_shape = []
    for n in names:
        out_shape += [jax.ShapeDtypeStruct(shapes[n], F32)] * 4
    out_shape.append(jax.ShapeDtypeStruct((1, 1), F32))
    outs = pl.pallas_call(
        body, name="small_adamw", out_shape=tuple(out_shape),
        in_specs=[VMEM] * len(operands), out_specs=tuple([VMEM] * len(out_shape)), compiler_params=_params(),
    )(*operands)
    return {n: outs[4 * i: 4 * i + 4] for i, n in enumerate(names)}, outs[4 * len(names)]


def _cast_into_gathered(w, tile, place, name, after):
    r, c = w.shape
    nblk = r // tile

    def body(place_ref, w_ref, after_ref, o_ref):
        o_ref[...] = w_ref[...].astype(BF16)

    return pl.pallas_call(
        body, name=name,
        grid_spec=pltpu.PrefetchScalarGridSpec(
            num_scalar_prefetch=1, grid=(nblk,), in_specs=[pl.BlockSpec((tile, c), lambda i, p: (i, 0)), ANY],
            out_specs=pl.BlockSpec((tile, c), lambda i, p: (p[1] * nblk + i, 0))),
        out_shape=jax.ShapeDtypeStruct((N_CHIPS * r, c), BF16),
        compiler_params=_params(("parallel",)),
    )(place, w, after)


def _norm_modulate(x, norm_g, mod):
    s, d = x.shape
    tm = min(256, s)

    def body(x_ref, g_ref, mod_ref, h_ref):
        xv = x_ref[...]
        r = lax.rsqrt(jnp.mean(xv * xv, axis=-1, keepdims=True) + EPS)
        h = (xv * r * g_ref[...]) * (1.0 + mod_ref[:, d:2 * d]) + mod_ref[:, 0:d]
        h_ref[...] = h.astype(BF16)

    return pl.pallas_call(
        body, name="norm_modulate", grid=(s // tm,), out_shape=jax.ShapeDtypeStruct((s, d), BF16),
        in_specs=[pl.BlockSpec((tm, d), lambda i: (i, 0)), pl.BlockSpec((1, d), lambda i: (0, 0)),
                  pl.BlockSpec((1, 3 * d), lambda i: (0, 0))],
        out_specs=pl.BlockSpec((tm, d), lambda i: (i, 0)),
        compiler_params=_params(("parallel",)),
    )(x, norm_g, mod)


def _matmul_nt(a, bt, tm, tn, out_dtype, name, after):
    m, k = a.shape
    n = bt.shape[0]
    tm, tn = min(tm, m), min(tn, n)

    def body(a_ref, b_ref, after_ref, o_ref):
        o_ref[...] = _dot(a_ref[...], b_ref[...], NT).astype(out_dtype)

    return pl.pallas_call(
        body, name=name, grid=(m // tm, n // tn), out_shape=jax.ShapeDtypeStruct((m, n), out_dtype),
        in_specs=[pl.BlockSpec((tm, k), lambda i, j: (i, 0)), pl.BlockSpec((tn, k), lambda i, j: (j, 0)), ANY],
        out_specs=pl.BlockSpec((tm, tn), lambda i, j: (i, j)),
        compiler_params=_params(("parallel", "parallel")),
    )(a, bt, after)


def _grad_matmul(a, b, half, tm, name, add=None):
    t, m = a.shape
    tm = min(tm, m)

    def body(half_ref, a_ref, b_ref, *rest):
        if add is None:
            (o_ref,) = rest
            o_ref[...] = _dot(a_ref[...], b_ref[...], TN).astype(BF16)
        else:
            add_ref, o_ref = rest
            o_ref[...] = (_dot(a_ref[...], b_ref[...], TN) + add_ref[...].astype(F32)).astype(BF16)

    in_specs = [pl.BlockSpec((t, tm), lambda i, h: (0, i)), pl.BlockSpec((t, HALF), lambda i, h: (0, h[0]))]
    operands = [a, b]
    if add is not None:
        in_specs.append(pl.BlockSpec((tm, HALF), lambda i, h: (i, 0)))
        operands.append(add)
    return pl.pallas_call(
        body, name=name,
        grid_spec=pltpu.PrefetchScalarGridSpec(num_scalar_prefetch=1, grid=(m // tm,), in_specs=in_specs,
                                               out_specs=pl.BlockSpec((tm, HALF), lambda i, h: (i, 0))),
        out_shape=jax.ShapeDtypeStruct((m, HALF), BF16),
        compiler_params=_params(("parallel",)),
    )(half, *operands)


def _mixer_masks(n):
    row = lax.broadcasted_iota(jnp.int32, (Q_PER_KV * BLOCK, 2 * BLOCK), 0) & (BLOCK - 1)
    col = lax.broadcasted_iota(jnp.int32, (Q_PER_KV * BLOCK, 2 * BLOCK), 1)
    valid = (col > row) & (col <= row + BLOCK) & ((col >= BLOCK) | (n > 0))
    r2 = lax.broadcasted_iota(jnp.int32, (BLOCK, BLOCK), 0)
    c2 = lax.broadcasted_iota(jnp.int32, (BLOCK, BLOCK), 1)
    return valid, r2 >= c2


def _layer_norm_parts(vs):
    mu = jnp.mean(vs, axis=-1, keepdims=True)
    xc = vs - mu
    rstd = lax.rsqrt(jnp.mean(xc * xc, axis=-1, keepdims=True) + EPS)
    return xc * rstd, rstd


def _stack_heads(ref, base, hk):
    first = base + HEAD_DIM * Q_PER_KV * hk
    return jnp.concatenate([ref[:, first + HEAD_DIM * g:first + HEAD_DIM * (g + 1)] for g in range(Q_PER_KV)], axis=0)


def _unstack_heads(ref, base, hk, val):
    first = base + HEAD_DIM * Q_PER_KV * hk
    for g in range(Q_PER_KV):
        ref[:, first + HEAD_DIM * g:first + HEAD_DIM * (g + 1)] = val[BLOCK * g:BLOCK * (g + 1), :].astype(ref.dtype)


def _stack_sinks(sink_ref, hk):
    return jnp.concatenate([jnp.broadcast_to(sink_ref[:, h:h + 1], (BLOCK, 1))
                            for h in range(Q_PER_KV * hk, Q_PER_KV * (hk + 1))], axis=0)


def _attn_probs(qs, kh, valid, sk):
    s = _dot(qs, kh, NT) * ATTN_SCALE
    s = jnp.where(valid, s, -jnp.inf)
    m = jnp.maximum(jnp.max(s, axis=-1, keepdims=True), sk)
    e = jnp.exp(s - m)
    es = jnp.exp(sk - m)
    inv = 1.0 / (jnp.sum(e, axis=-1, keepdims=True) + es)
    return e * inv, es * inv


def _kv_ext(z_ref, kvp_ref):
    k_ext = jnp.concatenate([kvp_ref[:, 0:D_KV], z_ref[:, K0:K0 + D_KV]], axis=0).astype(BF16)
    v_ext = jnp.concatenate([kvp_ref[:, D_KV:2 * D_KV], z_ref[:, V0:V0 + D_KV]], axis=0).astype(BF16)
    return k_ext, v_ext


def _sgu_mixed(w_ref, bt_ref, vn, tril):
    ws = [jnp.where(tril, w_ref[g], 0.0).astype(BF16) for g in range(SGU_GROUPS)]
    mixed = jnp.concatenate([_dot(ws[g], vn[:, GROUP_DIM * g:GROUP_DIM * (g + 1)], NN) + bt_ref[:, g:g + 1]
                             for g in range(SGU_GROUPS)], axis=1)
    return ws, mixed


def _mixer_fwd(z, sinks, ln_g, ln_b, sgu_w, sgu_bt):
    s = z.shape[0]
    nb = s // BLOCK

    def body(z_ref, kvp_ref, sink_ref, lg_ref, lb_ref, w_ref, bt_ref, cat_ref, o_ref):
        n = pl.program_id(0)
        valid, tril = _mixer_masks(n)
        k_ext, v_ext = _kv_ext(z_ref, kvp_ref)
        for hk in range(2):
            kv = slice(HEAD_DIM * hk, HEAD_DIM * (hk + 1))
            p, _ = _attn_probs(_stack_heads(z_ref, Q0, hk).astype(BF16), k_ext[:, kv], valid, _stack_sinks(sink_ref, hk))
            o = _dot(p.astype(BF16), v_ext[:, kv], NN)
            ga = _stack_heads(z_ref, GA0, hk)
            _unstack_heads(o_ref, 0, hk, o)
            _unstack_heads(cat_ref, 0, hk, o * (ga * _sigmoid(ga)))
        xhat, _ = _layer_norm_parts(z_ref[:, VS0:VS0 + D_SGU])
        vn = (xhat * lg_ref[...] + lb_ref[...]).astype(BF16)
        _, mixed = _sgu_mixed(w_ref, bt_ref, vn, tril)
        gt = z_ref[:, GS0:GS0 + D_SGU]
        cat_ref[:, D_ATTN:D_MODEL] = ((z_ref[:, U0:U0 + D_SGU] * mixed) * (gt * _sigmoid(gt))).astype(BF16)

    const2 = lambda n: (0, 0)
    return pl.pallas_call(
        body, name="mixer_fwd", grid=(nb,),
        out_shape=(jax.ShapeDtypeStruct((s, D_MODEL), BF16), jax.ShapeDtypeStruct((s, D_ATTN), F32)),
        in_specs=[pl.BlockSpec((BLOCK, D_IN), lambda n: (n, 0)),
                  pl.BlockSpec((BLOCK, 2 * D_KV), lambda n: (jnp.maximum(n - 1, 0), K0 // (2 * D_KV))),
                  pl.BlockSpec((1, N_Q_HEADS), const2), pl.BlockSpec((1, D_SGU), const2), pl.BlockSpec((1, D_SGU), const2),
                  pl.BlockSpec((SGU_GROUPS, BLOCK, BLOCK), lambda n: (0, 0, 0)), pl.BlockSpec((BLOCK, SGU_GROUPS), const2)],
        out_specs=(pl.BlockSpec((BLOCK, D_MODEL), lambda n: (n, 0)), pl.BlockSpec((BLOCK, D_ATTN), lambda n: (n, 0))),
        compiler_params=_params(("parallel",)),
    )(z, z, sinks, ln_g, ln_b, sgu_w, sgu_bt)


def _out_proj_loss(cat, wout, x, target, mod, final_g):
    s, d = x.shape
    tm = min(256, s)

    def body(cat_ref, w_ref, x_ref, t_ref, mod_ref, fg_ref, dx1_ref, dy_ref, dfg_ref, dgate_ref, loss_ref):
        i = pl.program_id(0)
        gate = mod_ref[:, 2 * d:3 * d]
        fg = fg_ref[...]
        yv = _dot(cat_ref[...], w_ref[...], NN)
        x1 = x_ref[...] + gate * yv
        r = lax.rsqrt(jnp.mean(x1 * x1, axis=-1, keepdims=True) + EPS)
        xr = x1 * r
        err = xr * fg - t_ref[...]
        dout = err * (1.0 / d)
        gy = dout * fg
        dx1 = r * gy - xr * (r * r * jnp.mean(gy * x1, axis=-1, keepdims=True))
        dx1_ref[...] = dx1
        dy_ref[...] = (gate * dx1).astype(BF16)

        @pl.when(i == 0)
        def _():
            dfg_ref[...] = jnp.zeros_like(dfg_ref)
            dgate_ref[...] = jnp.zeros_like(dgate_ref)
            loss_ref[...] = jnp.zeros_like(loss_ref)

        dfg_ref[...] += jnp.sum(dout * xr, axis=0, keepdims=True)
        dgate_ref[...] += jnp.sum(dx1 * yv, axis=0, keepdims=True)
        loss_ref[...] += 0.5 * jnp.sum(jnp.mean(err * err, axis=-1, keepdims=True), axis=0, keepdims=True)

    row = lambda i: (i, 0)
    const = lambda i: (0, 0)
    return pl.pallas_call(
        body, name="out_proj_loss", grid=(s // tm,),
        out_shape=(jax.ShapeDtypeStruct((s, d), F32), jax.ShapeDtypeStruct((s, d), BF16), jax.ShapeDtypeStruct((1, d), F32),
                   jax.ShapeDtypeStruct((1, d), F32), jax.ShapeDtypeStruct((1, 1), F32)),
        in_specs=[pl.BlockSpec((tm, d), row), pl.BlockSpec((d, d), const), pl.BlockSpec((tm, d), row), pl.BlockSpec((tm, d), row),
                  pl.BlockSpec((1, 3 * d), const), pl.BlockSpec((1, d), const)],
        out_specs=(pl.BlockSpec((tm, d), row), pl.BlockSpec((tm, d), row), pl.BlockSpec((1, d), const), pl.BlockSpec((1, d), const),
                   pl.BlockSpec((1, 1), const)),
        compiler_params=_params(("arbitrary",)),
    )(cat, wout, x, target, mod, final_g)


def _mixer_bwd(z, dcat, o, sinks, ln_g, ln_b, sgu_w, sgu_bt):
    s = z.shape[0]
    nb = s // BLOCK

    def body(z_ref, kvp_ref, dc_ref, o_ref, sink_ref, lg_ref, lb_ref, w_ref, bt_ref,
             dz_ref, dsink_ref, dlg_ref, dlb_ref, dw_ref, db_ref, pend, dkv, carry, dsink_acc, dbt_acc):
        n = pl.program_id(0)

        @pl.when(n == 0)
        def _():
            pend[...] = jnp.zeros_like(pend)
            carry[...] = jnp.zeros_like(carry)
            dsink_acc[...] = jnp.zeros_like(dsink_acc)
            dbt_acc[...] = jnp.zeros_like(dbt_acc)
            dlg_ref[...] = jnp.zeros_like(dlg_ref)
            dlb_ref[...] = jnp.zeros_like(dlb_ref)
            dw_ref[...] = jnp.zeros_like(dw_ref)

        dz_ref[:, 0:K0] = pend[:, 0:K0]
        dz_ref[:, GA0:D_IN] = pend[:, GA0:D_IN]

        @pl.when(n == nb)
        def _():
            dkv[...] = jnp.zeros_like(dkv)

        @pl.when(n < nb)
        def _():
            valid, tril = _mixer_masks(n)
            k_ext, v_ext = _kv_ext(z_ref, kvp_ref)
            for hk in range(2):
                kv = slice(HEAD_DIM * hk, HEAD_DIM * (hk + 1))
                kh, vh = k_ext[:, kv], v_ext[:, kv]
                qs = _stack_heads(z_ref, Q0, hk).astype(BF16)
                p, p_sink = _attn_probs(qs, kh, valid, _stack_sinks(sink_ref, hk))
                ga = _stack_heads(z_ref, GA0, hk)
                sg = _sigmoid(ga)
                datt = _stack_heads(dc_ref, 0, hk)
                do = (datt * (ga * sg)).astype(BF16)
                _unstack_heads(pend, GA0, hk, datt * _stack_heads(o_ref, 0, hk) * (sg * (1.0 + ga * (1.0 - sg))))
                dp = _dot(do, vh, NT)
                delta = jnp.sum(p * dp, axis=-1, keepdims=True)
                ds = (p * (dp - delta) * ATTN_SCALE).astype(BF16)
                dsk = -p_sink * delta
                for gq in range(Q_PER_KV):
                    h = hk * Q_PER_KV + gq
                    dsink_acc[:, h:h + 1] += jnp.sum(dsk[BLOCK * gq:BLOCK * (gq + 1), :], axis=0, keepdims=True)
                _unstack_heads(pend, Q0, hk, _dot(ds, kh, NN))
                dkv[:, kv] = _dot(ds, qs, TN)
                dkv[:, D_KV + HEAD_DIM * hk:D_KV + HEAD_DIM * (hk + 1)] = _dot(p.astype(BF16), do, TN)

            xhat, rstd = _layer_norm_parts(z_ref[:, VS0:VS0 + D_SGU])
            lg = lg_ref[...]
            vn = (xhat * lg + lb_ref[...]).astype(BF16)
            ws, mixed = _sgu_mixed(w_ref, bt_ref, vn, tril)
            u = z_ref[:, U0:U0 + D_SGU]
            gt = z_ref[:, GS0:GS0 + D_SGU]
            sg = _sigmoid(gt)
            dsgu = dc_ref[:, D_ATTN:D_MODEL]
            dgated = dsgu * (gt * sg)
            pend[:, U0:U0 + D_SGU] = (dgated * mixed).astype(BF16)
            pend[:, GS0:GS0 + D_SGU] = (dsgu * (u * mixed) * (sg * (1.0 + gt * (1.0 - sg)))).astype(BF16)
            dmixed = dgated * u
            dmb = dmixed.astype(BF16)
            dvn_parts = []
            for g in range(SGU_GROUPS):
                gs_ = slice(GROUP_DIM * g, GROUP_DIM * (g + 1))
                dw_ref[g] += jnp.where(tril, _dot(dmb[:, gs_], vn[:, gs_], NT), 0.0)
                dbt_acc[:, g:g + 1] += jnp.sum(dmixed[:, gs_], axis=-1, keepdims=True)
                dvn_parts.append(_dot(ws[g], dmb[:, gs_], TN))
            dvn = jnp.concatenate(dvn_parts, axis=1)
            dlg_ref[...] += jnp.sum(dvn * xhat, axis=0, keepdims=True)
            dlb_ref[...] += jnp.sum(dvn, axis=0, keepdims=True)
            dxh = dvn * lg
            pend[:, VS0:VS0 + D_SGU] = (rstd * (dxh - jnp.mean(dxh, axis=-1, keepdims=True)
                                                - xhat * jnp.mean(dxh * xhat, axis=-1, keepdims=True))).astype(BF16)

        dz_ref[:, K0:GA0] = (carry[...] + dkv[0:BLOCK, :]).astype(BF16)
        carry[...] = dkv[BLOCK:2 * BLOCK, :]

        @pl.when(n == nb)
        def _():
            dsink_ref[...] = dsink_acc[:, 0:N_Q_HEADS]
            db_ref[...] = dbt_acc[...].T[0:SGU_GROUPS, :]

    cur = lambda n: (jnp.minimum(n, nb - 1), 0)
    const2 = lambda n: (0, 0)
    return pl.pallas_call(
        body, name="mixer_bwd", grid=(nb + 1,),
        out_shape=(jax.ShapeDtypeStruct((s, D_IN), BF16), jax.ShapeDtypeStruct((1, N_Q_HEADS), F32),
                   jax.ShapeDtypeStruct((1, D_SGU), F32), jax.ShapeDtypeStruct((1, D_SGU), F32),
                   jax.ShapeDtypeStruct((SGU_GROUPS, BLOCK, BLOCK), F32), jax.ShapeDtypeStruct((SGU_GROUPS, BLOCK), F32)),
        in_specs=[pl.BlockSpec((BLOCK, D_IN), cur),
                  pl.BlockSpec((BLOCK, 2 * D_KV), lambda n: (jnp.maximum(jnp.minimum(n, nb - 1) - 1, 0), K0 // (2 * D_KV))),
                  pl.BlockSpec((BLOCK, D_MODEL), cur), pl.BlockSpec((BLOCK, D_ATTN), cur),
                  pl.BlockSpec((1, N_Q_HEADS), const2), pl.BlockSpec((1, D_SGU), const2), pl.BlockSpec((1, D_SGU), const2),
                  pl.BlockSpec((SGU_GROUPS, BLOCK, BLOCK), lambda n: (0, 0, 0)), pl.BlockSpec((BLOCK, SGU_GROUPS), const2)],
        out_specs=(pl.BlockSpec((BLOCK, D_IN), lambda n: (jnp.maximum(n - 1, 0), 0)),
                   pl.BlockSpec((1, N_Q_HEADS), const2), pl.BlockSpec((1, D_SGU), const2), pl.BlockSpec((1, D_SGU), const2),
                   pl.BlockSpec((SGU_GROUPS, BLOCK, BLOCK), lambda n: (0, 0, 0)), pl.BlockSpec((SGU_GROUPS, BLOCK), const2)),
        scratch_shapes=[pltpu.VMEM((BLOCK, D_IN), BF16), pltpu.VMEM((2 * BLOCK, 2 * D_KV), F32), pltpu.VMEM((BLOCK, 2 * D_KV), F32),
                        pltpu.VMEM((1, BLOCK), F32), pltpu.VMEM((BLOCK, BLOCK), F32)],
        compiler_params=_params(("arbitrary",)),
    )(z, z, dcat, o, sinks, ln_g, ln_b, sgu_w, sgu_bt)


def _in_proj_bwd(dz, win_t, x, dx1, norm_g, mod):
    s, d = x.shape
    kdim = dz.shape[1]
    tm = min(256, s)

    def body(dz_ref, w_ref, x_ref, dx1_ref, g_ref, mod_ref, gx_ref, dshift_ref, dscale_ref, dng_ref):
        @pl.when(pl.program_id(0) == 0)
        def _():
            dshift_ref[...] = jnp.zeros_like(dshift_ref)
            dscale_ref[...] = jnp.zeros_like(dscale_ref)
            dng_ref[...] = jnp.zeros_like(dng_ref)

        dh = _dot(dz_ref[...], w_ref[...], NN)
        xv = x_ref[...]
        ng = g_ref[...]
        r = lax.rsqrt(jnp.mean(xv * xv, axis=-1, keepdims=True) + EPS)
        xn = xv * r
        dshift_ref[...] += jnp.sum(dh, axis=0, keepdims=True)
        dscale_ref[...] += jnp.sum(dh * (xn * ng), axis=0, keepdims=True)
        dhs = dh * (1.0 + mod_ref[:, d:2 * d])
        dng_ref[...] += jnp.sum(dhs * xn, axis=0, keepdims=True)
        dxn = dhs * ng
        gx_ref[...] = dx1_ref[...] + r * dxn - xn * (r * r * jnp.mean(dxn * xv, axis=-1, keepdims=True))

    row = lambda i: (i, 0)
    const = lambda i: (0, 0)
    return pl.pallas_call(
        body, name="in_proj_bwd", grid=(s // tm,),
        out_shape=(jax.ShapeDtypeStruct((s, d), F32),) + (jax.ShapeDtypeStruct((1, d), F32),) * 3,
        in_specs=[pl.BlockSpec((tm, kdim), row), pl.BlockSpec((kdim, d), const, pipeline_mode=pl.Buffered(1)),
                  pl.BlockSpec((tm, d), row), pl.BlockSpec((tm, d), row), pl.BlockSpec((1, d), const), pl.BlockSpec((1, 3 * d), const)],
        out_specs=(pl.BlockSpec((tm, d), row),) + (pl.BlockSpec((1, d), const),) * 3,
        compiler_params=_params(("arbitrary",)),
    )(dz, win_t, x, dx1, norm_g, mod)


def _sum_slots(slots, own, place, rows_tile, full_cols, name):
    _, r, hcols = slots.shape
    nblk = r // rows_tile

    def body(place_ref, s_ref, own_ref, o_ref):
        chip = place_ref[1]
        acc = jnp.zeros((rows_tile, hcols), F32)
        for k in range(N_CHIPS):
            acc = acc + jnp.where(chip == k, own_ref[...], s_ref[k]).astype(F32)
        o_ref[...] = acc

    return pl.pallas_call(
        body, name=name,
        grid_spec=pltpu.PrefetchScalarGridSpec(
            num_scalar_prefetch=1, grid=(nblk,),
            in_specs=[pl.BlockSpec((N_CHIPS, rows_tile, hcols), lambda i, p: (0, i, 0)),
                      pl.BlockSpec((rows_tile, hcols), lambda i, p: (p[1] * nblk + i, 0))],
            out_specs=pl.BlockSpec((rows_tile, hcols), lambda i, p: (i, p[0]))),
        out_shape=jax.ShapeDtypeStruct((r, full_cols), F32),
        compiler_params=_params(("parallel",)),
    )(place, slots, own)


def _adamw_rows(w, g, m, v, tile, name, after):
    r, c = w.shape

    def body(w_ref, g_ref, m_ref, v_ref, after_ref, go_ref, d_ref, mo_ref, vo_ref):
        g_ = g_ref[...]
        delta, m2, v2 = _adamw(w_ref[...], g_, m_ref[...], v_ref[...])
        go_ref[...] = g_
        d_ref[...] = delta
        mo_ref[...] = m2
        vo_ref[...] = v2

    blk = pl.BlockSpec((tile, c), lambda i: (i, 0))
    return pl.pallas_call(
        body, name=name, grid=(r // tile,), out_shape=(jax.ShapeDtypeStruct((r, c), F32),) * 4,
        in_specs=[blk] * 4 + [ANY], out_specs=(blk,) * 4,
        compiler_params=_params(("parallel",)),
    )(w, g, m, v, after)


def _adamw_ada(cact, dmod, w, m, v, tile=256):
    r, c = w.shape

    def body(c_ref, d_ref, w_ref, m_ref, v_ref, go_ref, dl_ref, mo_ref, vo_ref):
        g_ = _dot(c_ref[...].astype(BF16), d_ref[...].astype(BF16), TN)
        delta, m2, v2 = _adamw(w_ref[...], g_, m_ref[...], v_ref[...])
        go_ref[...] = g_
        dl_ref[...] = delta
        mo_ref[...] = m2
        vo_ref[...] = v2

    blk = pl.BlockSpec((tile, c), lambda i: (i, 0))
    return pl.pallas_call(
        body, name="adamw_ada", grid=(r // tile,), out_shape=(jax.ShapeDtypeStruct((r, c), F32),) * 4,
        in_specs=[pl.BlockSpec((N_DEV, tile), lambda i: (0, i)), pl.BlockSpec((N_DEV, c), lambda i: (0, 0)), blk, blk, blk],
        out_specs=(blk,) * 4,
        compiler_params=_params(("parallel",)),
    )(cact, dmod, w, m, v)


def kernel(x, c, norm_g, w_ada, b_ada, w_in, attn_sinks, sgu_ln_g, sgu_ln_b, sgu_w, sgu_b, w_out, final_g, loss_target, m_norm_g, m_w_ada, m_b_ada, m_w_in, m_attn_sinks, m_sgu_ln_g, m_sgu_ln_b, m_sgu_w, m_sgu_b, m_w_out, m_final_g, v_norm_g, v_w_ada, v_b_ada, v_w_in, v_attn_sinks, v_sgu_ln_g, v_sgu_ln_b, v_sgu_w, v_sgu_b, v_w_out, v_final_g):
    xs, tgt = x[0], loss_target[0]
    core = lax.axis_index("c").astype(jnp.int32).reshape(1)
    chip = 2 * lax.axis_index("x") + lax.axis_index("y")
    place = jnp.stack([core[0], chip.astype(jnp.int32)])
    sgu_bt = sgu_b[0].T

    mod, cact = _mod_exchange(c, w_ada[0], b_ada)
    w_in_t, m_in_t, v_in_t = (jnp.swapaxes(a[0], 0, 1) for a in (w_in, m_w_in, v_w_in))
    win_t = _cast_into_gathered(w_in_t, IN_SHARD // 4, place, "cast_w_in", mod)
    win_t = _gather_routed(win_t, "w_in_gather")
    h = _norm_modulate(xs, norm_g, mod)
    wout = _cast_into_gathered(w_out[0], OUT_SHARD // 2, place, "cast_w_out", win_t)
    sems, bufs, token = _split_start(_gather_copies, [wout], 3, "w_out_gather_start", win_t)
    z = _matmul_nt(h, win_t, 1024, 768, F32, "in_proj", token)
    cat, o = _mixer_fwd(z, attn_sinks, sgu_ln_g, sgu_ln_b, sgu_w[0], sgu_bt)
    wout = _gather_finish(_split_wait(_gather_copies, sems, bufs, cat, "w_out_gather_wait")[0], "w_out_gather_finish")
    dx1, dy, dfg, dgate, loss = _out_proj_loss(cat, wout, xs, tgt, mod, final_g.reshape(1, D_MODEL))

    so = _to_sibling(_grad_matmul(cat, dy, 1 - core, 512, "w_out_grad_sibling"), "w_out_grad_to_sibling")
    po = _grad_matmul(cat, dy, core, 512, "w_out_grad_own", add=so)
    whole_o = functools.partial(_chip_copies, [(j, 0, OUT_SHARD) for j in range(3)])
    sems, bufs, token = _split_start(whole_o, [po, lax.empty((N_CHIPS, OUT_SHARD, HALF), BF16)], 3, "w_out_grad_to_chips_start", so)
    dcat = _matmul_nt(dy, wout, 512, 1024, F32, "out_proj_bwd", token)
    dz, dsinks, dlg, dlb, dsgu_w, dsgu_b = _mixer_bwd(z, dcat, o, attn_sinks, sgu_ln_g, sgu_ln_b, sgu_w[0], sgu_bt)
    po, slots_o = _split_wait(whole_o, sems, bufs, dz, "w_out_grad_to_chips_wait")
    go = _halves_swap(_sum_slots(slots_o, po, place, OUT_SHARD // 2, D_MODEL, "w_out_grad_sum"), "w_out_grad_halves_swap")

    sw = _to_sibling(_grad_matmul(dz, h, 1 - core, 768, "w_in_grad_sibling"), "w_in_grad_to_sibling")
    pw = _grad_matmul(dz, h, core, 768, "w_in_grad_own", add=sw)
    first = functools.partial(_chip_copies, [(0, 0, IN_SHARD), (1, 0, IN_SHARD), (2, 0, IN_SHARD // 4)])
    rest = functools.partial(_chip_copies, [(2, IN_SHARD // 4, 3 * IN_SHARD // 4)])
    sems, bufs, token = _split_start(first, [pw, lax.empty((N_CHIPS, IN_SHARD, HALF), BF16)], 3, "w_in_grad_to_chips_start", sw)
    gx, dshift, dscale, dng = _in_proj_bwd(dz, win_t, xs, dx1, norm_g + token[0:1, 0:1], mod)
    bufs = _split_wait(first, sems, bufs, gx, "w_in_grad_to_chips_wait")

    parts = dict(dng=dng, dshift=dshift, dscale=dscale, dgate=dgate, dfg=dfg, dlg=dlg, dlb=dlb, dsinks=dsinks,
                 dsgu_b=dsgu_b, dsgu_w=dsgu_w, loss=loss)
    small = dict(norm_g=(norm_g, m_norm_g, v_norm_g), b_ada=(b_ada, m_b_ada, v_b_ada),
                 attn_sinks=(attn_sinks, m_attn_sinks, v_attn_sinks), sgu_ln_g=(sgu_ln_g, m_sgu_ln_g, v_sgu_ln_g),
                 sgu_ln_b=(sgu_ln_b, m_sgu_ln_b, v_sgu_ln_b), sgu_w=(sgu_w, m_sgu_w, v_sgu_w), sgu_b=(sgu_b, m_sgu_b, v_sgu_b),
                 final_g=(final_g, m_final_g, v_final_g))
    tot, gsgu_w, dmod_all = _small_grads_sum(parts, bufs[1])
    sems, bufs, token = _split_start(rest, list(bufs), 1, "w_in_grad_rest_to_chips_start", tot)
    res, total = _small_adamw(tot, gsgu_w, small)
    dmod_mine = lax.dynamic_slice_in_dim(dmod_all, ADA_SHARD * chip, ADA_SHARD, axis=1) + token[0:1, 0:1]
    res["w_ada"] = _adamw_ada(cact, dmod_mine, w_ada[0], m_w_ada[0], v_w_ada[0])
    res["w_out"] = _adamw_rows(w_out[0], go, m_w_out[0], v_w_out[0], 256, "adamw_w_out", res["w_ada"][1])

    pw, slots_w = _split_wait(rest, sems, bufs, res["w_out"][1], "w_in_grad_rest_to_chips_wait")
    gw_t = _halves_swap(_sum_slots(slots_w, pw, place, IN_SHARD // 4, D_MODEL, "w_in_grad_sum"), "w_in_grad_halves_swap")
    res["w_in"] = tuple(jnp.swapaxes(a, 0, 1) for a in _adamw_rows(w_in_t, gw_t, m_in_t, v_in_t, IN_SHARD // 4, "adamw_w_in", gw_t))

    order = ["norm_g", "w_ada", "b_ada", "w_in", "attn_sinks", "sgu_ln_g", "sgu_ln_b", "sgu_w", "sgu_b", "w_out", "final_g"]
    like = dict(norm_g=norm_g, w_ada=w_ada, b_ada=b_ada, w_in=w_in, attn_sinks=attn_sinks, sgu_ln_g=sgu_ln_g, sgu_ln_b=sgu_ln_b,
                sgu_w=sgu_w, sgu_b=sgu_b, w_out=w_out, final_g=final_g)
    outs = [total[0, 0], gx[None]]
    for kind in range(4):
        outs += [res[n][kind].reshape(like[n].shape) for n in order]
    return tuple(outs)
```

```python
import functools
import math

import jax
import jax.numpy as jnp
from jax import lax
from jax.experimental import pallas as pl
from jax.experimental.pallas import tpu as pltpu

F32 = jnp.float32
BF16 = jnp.bfloat16

D_MODEL = 2048
D_ATTN = 1024
HEAD_DIM = 64
N_Q_HEADS = 16
Q_PER_KV = 8
D_KV = 128
BLOCK = 128
D_SGU = 1024
SGU_GROUPS = 8
GROUP_DIM = 128
D_IN = 5376
EPS = 1e-6
ATTN_SCALE = 1.0 / math.sqrt(HEAD_DIM)
N_CHIPS = 4
N_DEV = 8
IN_SHARD = D_IN // N_CHIPS
OUT_SHARD = D_MODEL // N_CHIPS
ADA_SHARD = 3 * D_MODEL // N_CHIPS
HALF = D_MODEL // 2

Q0, K0, V0, GA0, U0, VS0, GS0 = 0, 1024, 1152, 1280, 2304, 3328, 4352

ADAM_LR = 0.001
ADAM_B1 = 0.9
ADAM_B2 = 0.999
ADAM_EPS = 1e-08
ADAM_WD = 0.01
ADAM_STEP = 10

VMEM_LIMIT_V7X = 56 * 1024 * 1024

NN = (((1,), (0,)), ((), ()))
NT = (((1,), (1,)), ((), ()))
TN = (((0,), (0,)), ((), ()))
MESH = pl.DeviceIdType.MESH
ANY = pl.BlockSpec(memory_space=pl.ANY)
VMEM = pl.BlockSpec(memory_space=pltpu.VMEM)
HBM = pl.BlockSpec(memory_space=pltpu.HBM)
SEM = pl.BlockSpec(memory_space=pltpu.SEMAPHORE)
EFFECT = pltpu.SideEffectType.DATAFLOW_SIDE_EFFECTING


def _dot(a, b, dims):
    return lax.dot_general(a, b, dims, preferred_element_type=F32)


def _params(sem=None):
    return pltpu.CompilerParams(dimension_semantics=sem, vmem_limit_bytes=VMEM_LIMIT_V7X)


def _sigmoid(x):
    return jax.nn.sigmoid(x)


def _adamw(w, g, m, v):
    m = ADAM_B1 * m + (1.0 - ADAM_B1) * g
    v = ADAM_B2 * v + (1.0 - ADAM_B2) * (g * g)
    m_hat = m / (1.0 - ADAM_B1 ** ADAM_STEP)
    v_hat = v / (1.0 - ADAM_B2 ** ADAM_STEP)
    delta = -ADAM_LR * (m_hat / (jnp.sqrt(v_hat) + ADAM_EPS) + ADAM_WD * w)
    return delta, m, v


def _place():
    return lax.axis_index("x"), lax.axis_index("y"), lax.axis_index("c")


def _flip(p, bit):
    return 1 - p if bit else p


def _mod_exchange(c, w_ada, b_ada):
    kdim = w_ada.shape[0]
    kc = 256

    def body(c_ref, w_ref, b_ref, mod_ref, cact_ref, cbuf, pbuf, rbuf, send_sems, recv_sems):
        x, y, cc = _place()
        me = 4 * x + 2 * y + cc
        chip = 2 * x + y
        cbuf[me] = c_ref[...]
        gathers = []
        for k in range(1, N_DEV):
            bx, by, bc = (k >> 2) & 1, (k >> 1) & 1, k & 1
            cp = pltpu.make_async_remote_copy(
                src_ref=cbuf.at[me], dst_ref=cbuf.at[me], send_sem=send_sems.at[k - 1], recv_sem=recv_sems.at[k - 1],
                device_id=(_flip(x, bx), _flip(y, by), _flip(cc, bc)), device_id_type=MESH)
            cp.start()
            gathers.append(cp)
        for cp in gathers:
            cp.wait()
        for b in range(N_DEV):
            cv = cbuf[b]
            cact_ref[b:b + 1, :] = cv * _sigmoid(cv)
        acc = jnp.zeros((N_DEV, ADA_SHARD), F32)
        for k0 in range(0, kdim, kc):
            acc = acc + _dot(cact_ref[:, k0:k0 + kc].astype(BF16), w_ref[k0:k0 + kc, :].astype(BF16), NN)
        for b in range(N_DEV):
            pbuf[b] = acc[b:b + 1, :]
        rbuf[chip] = pbuf[me]
        sends = []
        for j, (bx, by) in enumerate(((0, 1), (1, 0), (1, 1))):
            px, py = _flip(x, bx), _flip(y, by)
            cp = pltpu.make_async_remote_copy(
                src_ref=pbuf.at[4 * px + 2 * py + cc], dst_ref=rbuf.at[chip],
                send_sem=send_sems.at[N_DEV - 1 + j], recv_sem=recv_sems.at[N_DEV - 1 + j],
                device_id=(px, py, cc), device_id_type=MESH)
            cp.start()
            sends.append(cp)
        for cp in sends:
            cp.wait()
        for k in range(N_CHIPS):
            cols = slice(ADA_SHARD * k, ADA_SHARD * (k + 1))
            mod_ref[:, cols] = rbuf[k] + b_ref[:, cols]

    return pl.pallas_call(
        body, name="mod_exchange",
        out_shape=(jax.ShapeDtypeStruct((1, 3 * D_MODEL), F32), jax.ShapeDtypeStruct((N_DEV, D_MODEL), F32)),
        in_specs=[VMEM, VMEM, VMEM], out_specs=(VMEM, VMEM),
        scratch_shapes=[pltpu.VMEM((N_DEV, 1, D_MODEL), F32), pltpu.VMEM((N_DEV, 1, ADA_SHARD), F32),
                        pltpu.VMEM((N_CHIPS, 1, ADA_SHARD), F32),
                        pltpu.SemaphoreType.DMA((N_DEV + 2,)), pltpu.SemaphoreType.DMA((N_DEV + 2,))],
        compiler_params=_params(),
    )(c, w_ada, b_ada)


def _other_chips(x, y):
    return [(_flip(x, bx), _flip(y, by)) for bx, by in ((0, 1), (1, 0), (1, 1))]


def _split_start(copies, bufs, n_sems, name, after):
    n = len(bufs)

    def body(*refs):
        send_sems, recv_sems = refs[n + 1:n + 3]
        for cp in copies(*refs[:n], send_sems, recv_sems):
            cp.start()
        refs[-1][...] = jnp.zeros_like(refs[-1])

    bufs = [pltpu.with_memory_space_constraint(a, pltpu.HBM) for a in bufs]
    outs = pl.pallas_call(
        body, name=name,
        out_shape=(pltpu.SemaphoreType.DMA((n_sems,)), pltpu.SemaphoreType.DMA((n_sems,)))
        + tuple(pltpu.HBM(a.shape, a.dtype) for a in bufs) + (jax.ShapeDtypeStruct((8, 128), F32),),
        in_specs=[HBM] * n + [ANY], out_specs=(SEM, SEM) + (HBM,) * n + (VMEM,), input_output_aliases={k: k + 2 for k in range(n)},
        compiler_params=pltpu.CompilerParams(has_side_effects=EFFECT),
    )(*bufs, after)
    return outs[:2], list(outs[2:2 + n]), outs[-1]


def _split_wait(copies, sems, bufs, after, name):
    n = len(bufs)

    def body(*refs):
        send_sems, recv_sems = refs[n:n + 2]
        for cp in copies(*refs[:n], send_sems, recv_sems):
            cp.wait_send()
            cp.wait_recv()

    return pl.pallas_call(
        body, name=name, out_shape=tuple(pltpu.HBM(a.shape, a.dtype) for a in bufs),
        in_specs=[HBM] * n + [SEM, SEM, ANY], out_specs=(HBM,) * n, input_output_aliases={k: k for k in range(n)},
        compiler_params=pltpu.CompilerParams(has_side_effects=EFFECT),
    )(*bufs, *sems, after)


def _gather_copies(full_ref, send_sems, recv_sems):
    x, y, cc = _place()
    shard = full_ref.shape[0] // N_CHIPS
    half = shard // 2
    mine = full_ref.at[pl.ds(shard * (2 * x + y) + half * cc, half), :]
    return [pltpu.make_async_remote_copy(src_ref=mine, dst_ref=mine, send_sem=send_sems.at[j], recv_sem=recv_sems.at[j],
                                         device_id=(px, py, cc), device_id_type=MESH)
            for j, (px, py) in enumerate(_other_chips(x, y))]


def _gather_routed(full, name):
    shard = full.shape[0] // N_CHIPS
    half = shard // 2
    quarter = half // 2

    def body(full_in, full_ref, send_sems, recv_sems):
        x, y, cc = _place()
        across_x, across_y, far = (1 - x, y), (x, 1 - y), (1 - x, 1 - y)

        def rows(chip_xy, core, q):
            return full_ref.at[pl.ds(shard * (2 * chip_xy[0] + chip_xy[1]) + half * core + quarter * q, quarter), :]

        def copy(k, piece, to):
            return pltpu.make_async_remote_copy(src_ref=piece, dst_ref=piece, send_sem=send_sems.at[k], recv_sem=recv_sems.at[k],
                                                device_id=to, device_id_type=MESH)

        me = (x, y)
        direct = [copy(0, rows(me, cc, 0), (*across_x, cc)), copy(2, rows(me, cc, 1), (*across_y, cc)),
                  copy(1, rows(me, cc, 1), (*across_x, cc)), copy(3, rows(me, cc, 0), (*across_y, cc))]
        for cp in direct:
            cp.start()
        landing = [(0, across_x, 0, (4, (*across_y, cc))), (2, across_y, 1, (5, (*across_x, cc))),
                   (1, across_x, 1, None), (3, across_y, 0, None), (4, far, 0, None), (5, far, 1, None)]
        passed = []
        for n, (k, whose, q, onward) in enumerate(landing):
            piece = rows(whose, cc, q)
            copy(k, piece, (x, y, cc)).wait_recv()
            if onward is not None:
                passed.append(copy(onward[0], piece, onward[1]))
                passed[-1].start()
            passed.append(copy(6 + n, piece, (x, y, 1 - cc)))
            passed[-1].start()
        for n, (k, whose, q, onward) in enumerate(landing):
            copy(6 + n, rows(whose, 1 - cc, q), (x, y, cc)).wait_recv()
        for cp in direct + passed:
            cp.wait_send()

    return pl.pallas_call(
        body, name=name, out_shape=jax.ShapeDtypeStruct(full.shape, full.dtype),
        in_specs=[ANY], out_specs=ANY, input_output_aliases={0: 0},
        scratch_shapes=[pltpu.SemaphoreType.DMA((12,)), pltpu.SemaphoreType.DMA((12,))],
        compiler_params=_params(),
    )(full)


def _gather_finish(full, name):
    shard = full.shape[0] // N_CHIPS
    half = shard // 2

    def body(full_in, full_ref, send_sems, recv_sems):
        x, y, cc = _place()
        cps = []
        for j, (px, py) in enumerate(_other_chips(x, y)):
            landed = full_ref.at[pl.ds(shard * (2 * px + py) + half * cc, half), :]
            cps.append(pltpu.make_async_remote_copy(src_ref=landed, dst_ref=landed, send_sem=send_sems.at[j], recv_sem=recv_sems.at[j],
                                                    device_id=(x, y, 1 - cc), device_id_type=MESH))
        for cp in cps:
            cp.start()
        for cp in cps:
            cp.wait()

    return pl.pallas_call(
        body, name=name, out_shape=jax.ShapeDtypeStruct(full.shape, full.dtype),
        in_specs=[ANY], out_specs=ANY, input_output_aliases={0: 0},
        scratch_shapes=[pltpu.SemaphoreType.DMA((3,)), pltpu.SemaphoreType.DMA((3,))],
        compiler_params=_params(),
    )(full)


def _to_sibling(g, name):
    def body(g_ref, r_ref, send_sem, recv_sem):
        x, y, cc = _place()
        cp = pltpu.make_async_remote_copy(src_ref=g_ref, dst_ref=r_ref, send_sem=send_sem, recv_sem=recv_sem,
                                          device_id=(x, y, 1 - cc), device_id_type=MESH)
        cp.start()
        cp.wait()

    return pl.pallas_call(
        body, name=name, out_shape=jax.ShapeDtypeStruct(g.shape, g.dtype), in_specs=[ANY], out_specs=ANY,
        scratch_shapes=[pltpu.SemaphoreType.DMA, pltpu.SemaphoreType.DMA], compiler_params=_params(),
    )(g)


def _chip_copies(pieces, p_ref, land_ref, send_sems, recv_sems):
    x, y, cc = _place()
    chip = 2 * x + y
    shard = land_ref.shape[1]
    others = _other_chips(x, y)
    cps = []
    for k, (j, r0, nr) in enumerate(pieces):
        px, py = others[j]
        cps.append(pltpu.make_async_remote_copy(
            src_ref=p_ref.at[pl.ds(shard * (2 * px + py) + r0, nr), :], dst_ref=land_ref.at[chip, pl.ds(r0, nr), :],
            send_sem=send_sems.at[k], recv_sem=recv_sems.at[k], device_id=(px, py, cc), device_id_type=MESH))
    return cps


def _halves_swap(f, name):
    def body(f_in, f_ref, send_sem, recv_sem):
        x, y, cc = _place()
        mine = f_ref.at[:, pl.ds(pl.multiple_of(HALF * cc, 128), HALF)]
        cp = pltpu.make_async_remote_copy(src_ref=mine, dst_ref=mine, send_sem=send_sem, recv_sem=recv_sem,
                                          device_id=(x, y, 1 - cc), device_id_type=MESH)
        cp.start()
        cp.wait()

    return pl.pallas_call(
        body, name=name, out_shape=jax.ShapeDtypeStruct(f.shape, f.dtype),
        in_specs=[ANY], out_specs=ANY, input_output_aliases={0: 0},
        scratch_shapes=[pltpu.SemaphoreType.DMA, pltpu.SemaphoreType.DMA], compiler_params=_params(),
    )(f)


ROW_NORM_G, ROW_DMOD, ROW_FINAL_G, ROW_LN, ROW_SINKS, ROW_LOSS, ROW_SGU_B, MISC_ROWS = 0, 1, 4, 5, 6, 7, 8, 16


def _small_grads_sum(parts, after):
    part_names = ["dng", "dshift", "dscale", "dgate", "dfg", "dlg", "dlb", "dsinks", "dsgu_b", "dsgu_w", "loss"]
    n_parts = len(part_names)
    n_peers = N_DEV - 1

    def body(*refs):
        p = dict(zip(part_names, refs[:n_parts]))
        tot_ref, gw_ref, dmod_ref = refs[n_parts + 1:n_parts + 4]
        misc, cmisc, dbuf, wland, wsum, send_sems, recv_sems = refs[n_parts + 4:]
        x, y, cc = _place()
        me = 4 * x + 2 * y + cc
        chip = 2 * x + y
        peers = [(_flip(x, (k >> 2) & 1), _flip(y, (k >> 1) & 1), _flip(cc, k & 1)) for k in range(1, N_DEV)]

        def remote(k, src, dst, peer):
            return pltpu.make_async_remote_copy(src_ref=src, dst_ref=dst, send_sem=send_sems.at[k], recv_sem=recv_sems.at[k],
                                                device_id=peer, device_id_type=MESH)

        dbuf[me] = jnp.zeros((8, D_MODEL), F32)
        dbuf[me, 0:1, :] = p["dshift"][...]
        dbuf[me, 1:2, :] = p["dscale"][...]
        dbuf[me, 2:3, :] = p["dgate"][...]
        gathers = [remote(k, dbuf.at[me], dbuf.at[me], peer) for k, peer in enumerate(peers)]
        wland[me] = p["dsgu_w"][me]
        scatters = [remote(n_peers + k, p["dsgu_w"].at[4 * px + 2 * py + pc], wland.at[me], (px, py, pc))
                    for k, (px, py, pc) in enumerate(peers)]

        misc[cc] = jnp.zeros((MISC_ROWS, D_MODEL), F32)
        misc[cc, ROW_NORM_G:ROW_NORM_G + 1, :] = p["dng"][...]
        misc[cc, ROW_DMOD:ROW_DMOD + 1, :] = p["dshift"][...]
        misc[cc, ROW_DMOD + 1:ROW_DMOD + 2, :] = p["dscale"][...]
        misc[cc, ROW_DMOD + 2:ROW_DMOD + 3, :] = p["dgate"][...]
        misc[cc, ROW_FINAL_G:ROW_FINAL_G + 1, :] = p["dfg"][...]
        misc[cc, ROW_LN:ROW_LN + 1, 0:D_SGU] = p["dlg"][...]
        misc[cc, ROW_LN:ROW_LN + 1, D_SGU:2 * D_SGU] = p["dlb"][...]
        misc[cc, ROW_SINKS:ROW_SINKS + 1, 0:N_Q_HEADS] = p["dsinks"][...]
        misc[cc, ROW_LOSS:ROW_LOSS + 1, 0:1] = p["loss"][...]
        misc[cc, ROW_SGU_B:ROW_SGU_B + SGU_GROUPS, 0:BLOCK] = p["dsgu_b"][...]
        pair = remote(3 * n_peers, misc.at[cc], misc.at[cc], (x, y, 1 - cc))
        for cp in gathers + scatters + [pair]:
            cp.start()
        pair.wait()
        cmisc[chip] = misc[0] + misc[1]
        quad = [remote(3 * n_peers + 1 + j, cmisc.at[chip], cmisc.at[chip], (_flip(x, bx), _flip(y, by), cc))
                for j, (bx, by) in enumerate(((0, 1), (1, 0), (1, 1)))]
        for cp in quad:
            cp.start()
        for cp in scatters:
            cp.wait()
        group = wland[0]
        for b in range(1, N_DEV):
            group = group + wland[b]
        wsum[me] = group
        spreads = [remote(2 * n_peers + k, wsum.at[me], wsum.at[me], peer) for k, peer in enumerate(peers)]
        for cp in spreads:
            cp.start()
        for cp in gathers + quad + spreads:
            cp.wait()

        tot = cmisc[0]
        for k in range(1, N_CHIPS):
            tot = tot + cmisc[k]
        tot_ref[...] = tot
        gw_ref[...] = wsum[...]
        for b in range(N_DEV):
            dmod_ref[b:b + 1, :] = jnp.concatenate([dbuf[b, t:t + 1, :] for t in range(3)], axis=1)

    operands = [parts[n] for n in part_names]
    n_sems = 3 * n_peers + 1 + (N_CHIPS - 1)
    return pl.pallas_call(
        body, name="small_grads_sum",
        out_shape=(jax.ShapeDtypeStruct((MISC_ROWS, D_MODEL), F32), jax.ShapeDtypeStruct((SGU_GROUPS, BLOCK, BLOCK), F32),
                   jax.ShapeDtypeStruct((N_DEV, 3 * D_MODEL), F32)),
        in_specs=[VMEM] * len(operands) + [ANY], out_specs=(VMEM, VMEM, VMEM),
        scratch_shapes=[pltpu.VMEM((2, MISC_ROWS, D_MODEL), F32), pltpu.VMEM((N_CHIPS, MISC_ROWS, D_MODEL), F32),
                        pltpu.VMEM((N_DEV, 8, D_MODEL), F32), pltpu.VMEM((N_DEV, BLOCK, BLOCK), F32),
                        pltpu.VMEM((SGU_GROUPS, BLOCK, BLOCK), F32),
                        pltpu.SemaphoreType.DMA((n_sems,)), pltpu.SemaphoreType.DMA((n_sems,))],
        compiler_params=_params(),
    )(*operands, after)


def _small_adamw(tot, gw, small):
    names = ["norm_g", "b_ada", "attn_sinks", "sgu_ln_g", "sgu_ln_b", "sgu_w", "sgu_b", "final_g"]

    def body(*refs):
        tot_ref, gw_ref = refs[:2]
        wmv = {n: refs[2 + 3 * i: 5 + 3 * i] for i, n in enumerate(names)}
        outs = refs[2 + 3 * len(names):]
        res = {n: outs[4 * i: 4 * i + 4] for i, n in enumerate(names)}
        outs[4 * len(names)][...] = tot_ref[ROW_LOSS:ROW_LOSS + 1, 0:1]

        def update(name, g, pick=lambda r: r[...], put=None):
            w_ref, m_ref, v_ref = wmv[name]
            delta, m2, v2 = _adamw(pick(w_ref), g, pick(m_ref), pick(v_ref))
            for ref, val in zip(res[name], (g, delta, m2, v2)):
                if put is None:
                    ref[...] = val
                else:
                    put(ref, val)

        update("norm_g", tot_ref[ROW_NORM_G:ROW_NORM_G + 1, :])
        update("final_g", tot_ref[ROW_FINAL_G:ROW_FINAL_G + 1, :])
        for t in range(3):
            cols = slice(D_MODEL * t, D_MODEL * (t + 1))

            def put_cols(ref, val, cols=cols):
                ref[:, cols] = val
            update("b_ada", tot_ref[ROW_DMOD + t:ROW_DMOD + t + 1, :], pick=lambda r, cols=cols: r[:, cols], put=put_cols)
        update("sgu_ln_g", tot_ref[ROW_LN:ROW_LN + 1, 0:D_SGU])
        update("sgu_ln_b", tot_ref[ROW_LN:ROW_LN + 1, D_SGU:2 * D_SGU])
        update("attn_sinks", tot_ref[ROW_SINKS:ROW_SINKS + 1, 0:N_Q_HEADS])
        update("sgu_b", tot_ref[ROW_SGU_B:ROW_SGU_B + SGU_GROUPS, 0:BLOCK])
        for g in range(SGU_GROUPS):
            def put_group(ref, val, g=g):
                ref[g] = val
            update("sgu_w", gw_ref[g], pick=lambda r, g=g: r[g], put=put_group)

    shapes = {"norm_g": (1, D_MODEL), "b_ada": (1, 3 * D_MODEL), "attn_sinks": (1, N_Q_HEADS), "sgu_ln_g": (1, D_SGU),
              "sgu_ln_b": (1, D_SGU), "sgu_w": (SGU_GROUPS, BLOCK, BLOCK), "sgu_b": (SGU_GROUPS, BLOCK), "final_g": (1, D_MODEL)}
    operands = [tot, gw]
    for n in names:
        operands += [a.reshape(shapes[n]) for a in small[n]]
    out_shape = []
    for n in names:
        out_shape += [jax.ShapeDtypeStruct(shapes[n], F32)] * 4
    out_shape.append(jax.ShapeDtypeStruct((1, 1), F32))
    outs = pl.pallas_call(
        body, name="small_adamw", out_shape=tuple(out_shape),
        in_specs=[VMEM] * len(operands), out_specs=tuple([VMEM] * len(out_shape)), compiler_params=_params(),
    )(*operands)
    return {n: outs[4 * i: 4 * i + 4] for i, n in enumerate(names)}, outs[4 * len(names)]


def _cast_into_gathered(w, tile, place, name, after):
    r, c = w.shape
    nblk = r // tile

    def body(place_ref, w_ref, after_ref, o_ref):
        o_ref[...] = w_ref[...].astype(BF16)

    return pl.pallas_call(
        body, name=name,
        grid_spec=pltpu.PrefetchScalarGridSpec(
            num_scalar_prefetch=1, grid=(nblk,), in_specs=[pl.BlockSpec((tile, c), lambda i, p: (i, 0)), ANY],
            out_specs=pl.BlockSpec((tile, c), lambda i, p: (p[1] * nblk + i, 0))),
        out_shape=jax.ShapeDtypeStruct((N_CHIPS * r, c), BF16),
        compiler_params=_params(("parallel",)),
    )(place, w, after)


def _norm_modulate(x, norm_g, mod):
    s, d = x.shape
    tm = min(256, s)

    def body(x_ref, g_ref, mod_ref, h_ref):
        xv = x_ref[...]
        r = lax.rsqrt(jnp.mean(xv * xv, axis=-1, keepdims=True) + EPS)
        h = (xv * r * g_ref[...]) * (1.0 + mod_ref[:, d:2 * d]) + mod_ref[:, 0:d]
        h_ref[...] = h.astype(BF16)

    return pl.pallas_call(
        body, name="norm_modulate", grid=(s // tm,), out_shape=jax.ShapeDtypeStruct((s, d), BF16),
        in_specs=[pl.BlockSpec((tm, d), lambda i: (i, 0)), pl.BlockSpec((1, d), lambda i: (0, 0)),
                  pl.BlockSpec((1, 3 * d), lambda i: (0, 0))],
        out_specs=pl.BlockSpec((tm, d), lambda i: (i, 0)),
        compiler_params=_params(("parallel",)),
    )(x, norm_g, mod)


def _matmul_nt(a, bt, tm, tn, out_dtype, name, after):
    m, k = a.shape
    n = bt.shape[0]
    tm, tn = min(tm, m), min(tn, n)

    def body(a_ref, b_ref, after_ref, o_ref):
        o_ref[...] = _dot(a_ref[...], b_ref[...], NT).astype(out_dtype)

    return pl.pallas_call(
        body, name=name, grid=(m // tm, n // tn), out_shape=jax.ShapeDtypeStruct((m, n), out_dtype),
        in_specs=[pl.BlockSpec((tm, k), lambda i, j: (i, 0)), pl.BlockSpec((tn, k), lambda i, j: (j, 0)), ANY],
        out_specs=pl.BlockSpec((tm, tn), lambda i, j: (i, j)),
        compiler_params=_params(("parallel", "parallel")),
    )(a, bt, after)


def _grad_matmul(a, b, half, tm, name, add=None):
    t, m = a.shape
    tm = min(tm, m)

    def body(half_ref, a_ref, b_ref, *rest):
        if add is None:
            (o_ref,) = rest
            o_ref[...] = _dot(a_ref[...], b_ref[...], TN).astype(BF16)
        else:
            add_ref, o_ref = rest
            o_ref[...] = (_dot(a_ref[...], b_ref[...], TN) + add_ref[...].astype(F32)).astype(BF16)

    in_specs = [pl.BlockSpec((t, tm), lambda i, h: (0, i)), pl.BlockSpec((t, HALF), lambda i, h: (0, h[0]))]
    operands = [a, b]
    if add is not None:
        in_specs.append(pl.BlockSpec((tm, HALF), lambda i, h: (i, 0)))
        operands.append(add)
    return pl.pallas_call(
        body, name=name,
        grid_spec=pltpu.PrefetchScalarGridSpec(num_scalar_prefetch=1, grid=(m // tm,), in_specs=in_specs,
                                               out_specs=pl.BlockSpec((tm, HALF), lambda i, h: (i, 0))),
        out_shape=jax.ShapeDtypeStruct((m, HALF), BF16),
        compiler_params=_params(("parallel",)),
    )(half, *operands)


def _mixer_masks(n):
    key = lax.broadcasted_iota(jnp.int32, (2 * BLOCK, Q_PER_KV * BLOCK), 0)
    qry = lax.broadcasted_iota(jnp.int32, (2 * BLOCK, Q_PER_KV * BLOCK), 1) & (BLOCK - 1)
    valid = (key > qry) & (key <= qry + BLOCK) & ((key >= BLOCK) | (n > 0))
    r2 = lax.broadcasted_iota(jnp.int32, (BLOCK, BLOCK), 0)
    c2 = lax.broadcasted_iota(jnp.int32, (BLOCK, BLOCK), 1)
    return valid, r2 >= c2


def _layer_norm_parts(vs):
    mu = jnp.mean(vs, axis=-1, keepdims=True)
    xc = vs - mu
    rstd = lax.rsqrt(jnp.mean(xc * xc, axis=-1, keepdims=True) + EPS)
    return xc * rstd, rstd


def _stack_heads(ref, base, hk):
    first = base + HEAD_DIM * Q_PER_KV * hk
    return jnp.concatenate([ref[:, first + HEAD_DIM * g:first + HEAD_DIM * (g + 1)] for g in range(Q_PER_KV)], axis=0)


def _head_cols(base, h):
    return slice(base + HEAD_DIM * h, base + HEAD_DIM * (h + 1))


def _sink_row(sink_ref, hk):
    return jnp.concatenate([jnp.broadcast_to(sink_ref[:, h:h + 1], (1, BLOCK))
                            for h in range(Q_PER_KV * hk, Q_PER_KV * (hk + 1))], axis=1)


def _attn_probs(qs, kh, valid, sk):
    s = _dot(kh, qs, NT) * ATTN_SCALE
    s = jnp.where(valid, s, -jnp.inf)
    m = jnp.maximum(jnp.max(s, axis=0, keepdims=True), sk)
    e = jnp.exp(s - m)
    es = jnp.exp(sk - m)
    inv = 1.0 / (jnp.sum(e, axis=0, keepdims=True) + es)
    return e * inv, es * inv


def _kv_ext(z_ref, kvp_ref):
    k_ext = jnp.concatenate([kvp_ref[:, 0:D_KV], z_ref[:, K0:K0 + D_KV]], axis=0).astype(BF16)
    v_ext = jnp.concatenate([kvp_ref[:, D_KV:2 * D_KV], z_ref[:, V0:V0 + D_KV]], axis=0).astype(BF16)
    return k_ext, v_ext


def _sgu_mixed(w_ref, bt_ref, vn, tril):
    ws = [jnp.where(tril, w_ref[g], 0.0).astype(BF16) for g in range(SGU_GROUPS)]
    mixed = jnp.concatenate([_dot(ws[g], vn[:, GROUP_DIM * g:GROUP_DIM * (g + 1)], NN) + bt_ref[:, g:g + 1]
                             for g in range(SGU_GROUPS)], axis=1)
    return ws, mixed


def _mixer_fwd(z, sinks, ln_g, ln_b, sgu_w, sgu_bt):
    s = z.shape[0]
    nb = s // BLOCK

    def body(z_ref, kvp_ref, sink_ref, lg_ref, lb_ref, w_ref, bt_ref, cat_ref, o_ref):
        n = pl.program_id(0)
        valid, tril = _mixer_masks(n)
        k_ext, v_ext = _kv_ext(z_ref, kvp_ref)
        for hk in range(2):
            kv = slice(HEAD_DIM * hk, HEAD_DIM * (hk + 1))
            p, _ = _attn_probs(_stack_heads(z_ref, Q0, hk).astype(BF16), k_ext[:, kv], valid, _sink_row(sink_ref, hk))
            o_t = _dot(v_ext[:, kv], p.astype(BF16), TN)
            for g in range(Q_PER_KV):
                h = Q_PER_KV * hk + g
                o = o_t[:, BLOCK * g:BLOCK * (g + 1)].T
                ga = z_ref[:, _head_cols(GA0, h)]
                o_ref[:, _head_cols(0, h)] = o
                cat_ref[:, _head_cols(0, h)] = (o * (ga * _sigmoid(ga))).astype(BF16)
        xhat, _ = _layer_norm_parts(z_ref[:, VS0:VS0 + D_SGU])
        vn = (xhat * lg_ref[...] + lb_ref[...]).astype(BF16)
        _, mixed = _sgu_mixed(w_ref, bt_ref, vn, tril)
        gt = z_ref[:, GS0:GS0 + D_SGU]
        cat_ref[:, D_ATTN:D_MODEL] = ((z_ref[:, U0:U0 + D_SGU] * mixed) * (gt * _sigmoid(gt))).astype(BF16)

    const2 = lambda n: (0, 0)
    return pl.pallas_call(
        body, name="mixer_fwd", grid=(nb,),
        out_shape=(jax.ShapeDtypeStruct((s, D_MODEL), BF16), jax.ShapeDtypeStruct((s, D_ATTN), F32)),
        in_specs=[pl.BlockSpec((BLOCK, D_IN), lambda n: (n, 0)),
                  pl.BlockSpec((BLOCK, 2 * D_KV), lambda n: (jnp.maximum(n - 1, 0), K0 // (2 * D_KV))),
                  pl.BlockSpec((1, N_Q_HEADS), const2), pl.BlockSpec((1, D_SGU), const2), pl.BlockSpec((1, D_SGU), const2),
                  pl.BlockSpec((SGU_GROUPS, BLOCK, BLOCK), lambda n: (0, 0, 0)), pl.BlockSpec((BLOCK, SGU_GROUPS), const2)],
        out_specs=(pl.BlockSpec((BLOCK, D_MODEL), lambda n: (n, 0)), pl.BlockSpec((BLOCK, D_ATTN), lambda n: (n, 0))),
        compiler_params=_params(("parallel",)),
    )(z, z, sinks, ln_g, ln_b, sgu_w, sgu_bt)


def _out_proj_loss(cat, wout, x, target, mod, final_g):
    s, d = x.shape
    tm = min(256, s)

    def body(cat_ref, w_ref, x_ref, t_ref, mod_ref, fg_ref, dx1_ref, dy_ref, dfg_ref, dgate_ref, loss_ref):
        i = pl.program_id(0)
        gate = mod_ref[:, 2 * d:3 * d]
        fg = fg_ref[...]
        yv = _dot(cat_ref[...], w_ref[...], NN)
        x1 = x_ref[...] + gate * yv
        r = lax.rsqrt(jnp.mean(x1 * x1, axis=-1, keepdims=True) + EPS)
        xr = x1 * r
        err = xr * fg - t_ref[...]
        dout = err * (1.0 / d)
        gy = dout * fg
        dx1 = r * gy - xr * (r * r * jnp.mean(gy * x1, axis=-1, keepdims=True))
        dx1_ref[...] = dx1
        dy_ref[...] = (gate * dx1).astype(BF16)

        @pl.when(i == 0)
        def _():
            dfg_ref[...] = jnp.zeros_like(dfg_ref)
            dgate_ref[...] = jnp.zeros_like(dgate_ref)
            loss_ref[...] = jnp.zeros_like(loss_ref)

        dfg_ref[...] += jnp.sum(dout * xr, axis=0, keepdims=True)
        dgate_ref[...] += jnp.sum(dx1 * yv, axis=0, keepdims=True)
        loss_ref[...] += 0.5 * jnp.sum(jnp.mean(err * err, axis=-1, keepdims=True), axis=0, keepdims=True)

    row = lambda i: (i, 0)
    const = lambda i: (0, 0)
    return pl.pallas_call(
        body, name="out_proj_loss", grid=(s // tm,),
        out_shape=(jax.ShapeDtypeStruct((s, d), F32), jax.ShapeDtypeStruct((s, d), BF16), jax.ShapeDtypeStruct((1, d), F32),
                   jax.ShapeDtypeStruct((1, d), F32), jax.ShapeDtypeStruct((1, 1), F32)),
        in_specs=[pl.BlockSpec((tm, d), row), pl.BlockSpec((d, d), const), pl.BlockSpec((tm, d), row), pl.BlockSpec((tm, d), row),
                  pl.BlockSpec((1, 3 * d), const), pl.BlockSpec((1, d), const)],
        out_specs=(pl.BlockSpec((tm, d), row), pl.BlockSpec((tm, d), row), pl.BlockSpec((1, d), const), pl.BlockSpec((1, d), const),
                   pl.BlockSpec((1, 1), const)),
        compiler_params=_params(("arbitrary",)),
    )(cat, wout, x, target, mod, final_g)


def _mixer_bwd(z, dcat, o, sinks, ln_g, ln_b, sgu_w, sgu_bt):
    s = z.shape[0]
    nb = s // BLOCK

    def body(z_ref, kvp_ref, dc_ref, o_ref, sink_ref, lg_ref, lb_ref, w_ref, bt_ref,
             dz_ref, dsink_ref, dlg_ref, dlb_ref, dw_ref, db_ref, pend, dkv, carry, dsink_acc, dbt_acc):
        n = pl.program_id(0)

        @pl.when(n == 0)
        def _():
            pend[...] = jnp.zeros_like(pend)
            carry[...] = jnp.zeros_like(carry)
            dsink_acc[...] = jnp.zeros_like(dsink_acc)
            dbt_acc[...] = jnp.zeros_like(dbt_acc)
            dlg_ref[...] = jnp.zeros_like(dlg_ref)
            dlb_ref[...] = jnp.zeros_like(dlb_ref)
            dw_ref[...] = jnp.zeros_like(dw_ref)

        dz_ref[:, 0:K0] = pend[:, 0:K0]
        dz_ref[:, GA0:D_IN] = pend[:, GA0:D_IN]

        @pl.when(n == nb)
        def _():
            dkv[...] = jnp.zeros_like(dkv)

        @pl.when(n < nb)
        def _():
            valid, tril = _mixer_masks(n)
            k_ext, v_ext = _kv_ext(z_ref, kvp_ref)
            for hk in range(2):
                kv = slice(HEAD_DIM * hk, HEAD_DIM * (hk + 1))
                kh, vh = k_ext[:, kv], v_ext[:, kv]
                qs = _stack_heads(z_ref, Q0, hk).astype(BF16)
                p, p_sink = _attn_probs(qs, kh, valid, _sink_row(sink_ref, hk))
                dos = []
                for g in range(Q_PER_KV):
                    h = Q_PER_KV * hk + g
                    ga = z_ref[:, _head_cols(GA0, h)]
                    sg = _sigmoid(ga)
                    datt = dc_ref[:, _head_cols(0, h)]
                    dos.append(datt * (ga * sg))
                    pend[:, _head_cols(GA0, h)] = (datt * o_ref[:, _head_cols(0, h)] * (sg * (1.0 + ga * (1.0 - sg)))).astype(BF16)
                do = jnp.concatenate(dos, axis=0).astype(BF16)
                dp = _dot(vh, do, NT)
                delta = jnp.sum(p * dp, axis=0, keepdims=True)
                ds = (p * (dp - delta) * ATTN_SCALE).astype(BF16)
                dsk = -p_sink * delta
                dq_t = _dot(kh, ds, TN)
                for g in range(Q_PER_KV):
                    h = Q_PER_KV * hk + g
                    dsink_acc[:, h:h + 1] += jnp.sum(dsk[:, BLOCK * g:BLOCK * (g + 1)], axis=1, keepdims=True)
                    pend[:, _head_cols(Q0, h)] = dq_t[:, BLOCK * g:BLOCK * (g + 1)].T.astype(BF16)
                dkv[:, kv] = _dot(ds, qs, NN)
                dkv[:, D_KV + HEAD_DIM * hk:D_KV + HEAD_DIM * (hk + 1)] = _dot(p.astype(BF16), do, NN)

            xhat, rstd = _layer_norm_parts(z_ref[:, VS0:VS0 + D_SGU])
            lg = lg_ref[...]
            vn = (xhat * lg + lb_ref[...]).astype(BF16)
            ws, mixed = _sgu_mixed(w_ref, bt_ref, vn, tril)
            u = z_ref[:, U0:U0 + D_SGU]
            gt = z_ref[:, GS0:GS0 + D_SGU]
            sg = _sigmoid(gt)
            dsgu = dc_ref[:, D_ATTN:D_MODEL]
            dgated = dsgu * (gt * sg)
            pend[:, U0:U0 + D_SGU] = (dgated * mixed).astype(BF16)
            pend[:, GS0:GS0 + D_SGU] = (dsgu * (u * mixed) * (sg * (1.0 + gt * (1.0 - sg)))).astype(BF16)
            dmixed = dgated * u
            dmb = dmixed.astype(BF16)
            dvn_parts = []
            for g in range(SGU_GROUPS):
                gs_ = slice(GROUP_DIM * g, GROUP_DIM * (g + 1))
                dw_ref[g] += jnp.where(tril, _dot(dmb[:, gs_], vn[:, gs_], NT), 0.0)
                dbt_acc[:, g:g + 1] += jnp.sum(dmixed[:, gs_], axis=-1, keepdims=True)
                dvn_parts.append(_dot(ws[g], dmb[:, gs_], TN))
            dvn = jnp.concatenate(dvn_parts, axis=1)
            dlg_ref[...] += jnp.sum(dvn * xhat, axis=0, keepdims=True)
            dlb_ref[...] += jnp.sum(dvn, axis=0, keepdims=True)
            dxh = dvn * lg
            pend[:, VS0:VS0 + D_SGU] = (rstd * (dxh - jnp.mean(dxh, axis=-1, keepdims=True)
                                                - xhat * jnp.mean(dxh * xhat, axis=-1, keepdims=True))).astype(BF16)

        dz_ref[:, K0:GA0] = (carry[...] + dkv[0:BLOCK, :]).astype(BF16)
        carry[...] = dkv[BLOCK:2 * BLOCK, :]

        @pl.when(n == nb)
        def _():
            dsink_ref[...] = dsink_acc[:, 0:N_Q_HEADS]
            db_ref[...] = dbt_acc[...].T[0:SGU_GROUPS, :]

    cur = lambda n: (jnp.minimum(n, nb - 1), 0)
    const2 = lambda n: (0, 0)
    return pl.pallas_call(
        body, name="mixer_bwd", grid=(nb + 1,),
        out_shape=(jax.ShapeDtypeStruct((s, D_IN), BF16), jax.ShapeDtypeStruct((1, N_Q_HEADS), F32),
                   jax.ShapeDtypeStruct((1, D_SGU), F32), jax.ShapeDtypeStruct((1, D_SGU), F32),
                   jax.ShapeDtypeStruct((SGU_GROUPS, BLOCK, BLOCK), F32), jax.ShapeDtypeStruct((SGU_GROUPS, BLOCK), F32)),
        in_specs=[pl.BlockSpec((BLOCK, D_IN), cur),
                  pl.BlockSpec((BLOCK, 2 * D_KV), lambda n: (jnp.maximum(jnp.minimum(n, nb - 1) - 1, 0), K0 // (2 * D_KV))),
                  pl.BlockSpec((BLOCK, D_MODEL), cur), pl.BlockSpec((BLOCK, D_ATTN), cur),
                  pl.BlockSpec((1, N_Q_HEADS), const2), pl.BlockSpec((1, D_SGU), const2), pl.BlockSpec((1, D_SGU), const2),
                  pl.BlockSpec((SGU_GROUPS, BLOCK, BLOCK), lambda n: (0, 0, 0)), pl.BlockSpec((BLOCK, SGU_GROUPS), const2)],
        out_specs=(pl.BlockSpec((BLOCK, D_IN), lambda n: (jnp.maximum(n - 1, 0), 0)),
                   pl.BlockSpec((1, N_Q_HEADS), const2), pl.BlockSpec((1, D_SGU), const2), pl.BlockSpec((1, D_SGU), const2),
                   pl.BlockSpec((SGU_GROUPS, BLOCK, BLOCK), lambda n: (0, 0, 0)), pl.BlockSpec((SGU_GROUPS, BLOCK), const2)),
        scratch_shapes=[pltpu.VMEM((BLOCK, D_IN), BF16), pltpu.VMEM((2 * BLOCK, 2 * D_KV), F32), pltpu.VMEM((BLOCK, 2 * D_KV), F32),
                        pltpu.VMEM((1, BLOCK), F32), pltpu.VMEM((BLOCK, BLOCK), F32)],
        compiler_params=_params(("arbitrary",)),
    )(z, z, dcat, o, sinks, ln_g, ln_b, sgu_w, sgu_bt)


def _in_proj_bwd(dz, win_t, x, dx1, norm_g, mod):
    s, d = x.shape
    kdim = dz.shape[1]
    tm = min(256, s)

    def body(dz_ref, w_ref, x_ref, dx1_ref, g_ref, mod_ref, gx_ref, dshift_ref, dscale_ref, dng_ref):
        @pl.when(pl.program_id(0) == 0)
        def _():
            dshift_ref[...] = jnp.zeros_like(dshift_ref)
            dscale_ref[...] = jnp.zeros_like(dscale_ref)
            dng_ref[...] = jnp.zeros_like(dng_ref)

        dh = _dot(dz_ref[...], w_ref[...], NN)
        xv = x_ref[...]
        ng = g_ref[...]
        r = lax.rsqrt(jnp.mean(xv * xv, axis=-1, keepdims=True) + EPS)
        xn = xv * r
        dshift_ref[...] += jnp.sum(dh, axis=0, keepdims=True)
        dscale_ref[...] += jnp.sum(dh * (xn * ng), axis=0, keepdims=True)
        dhs = dh * (1.0 + mod_ref[:, d:2 * d])
        dng_ref[...] += jnp.sum(dhs * xn, axis=0, keepdims=True)
        dxn = dhs * ng
        gx_ref[...] = dx1_ref[...] + r * dxn - xn * (r * r * jnp.mean(dxn * xv, axis=-1, keepdims=True))

    row = lambda i: (i, 0)
    const = lambda i: (0, 0)
    return pl.pallas_call(
        body, name="in_proj_bwd", grid=(s // tm,),
        out_shape=(jax.ShapeDtypeStruct((s, d), F32),) + (jax.ShapeDtypeStruct((1, d), F32),) * 3,
        in_specs=[pl.BlockSpec((tm, kdim), row), pl.BlockSpec((kdim, d), const, pipeline_mode=pl.Buffered(1)),
                  pl.BlockSpec((tm, d), row), pl.BlockSpec((tm, d), row), pl.BlockSpec((1, d), const), pl.BlockSpec((1, 3 * d), const)],
        out_specs=(pl.BlockSpec((tm, d), row),) + (pl.BlockSpec((1, d), const),) * 3,
        compiler_params=_params(("arbitrary",)),
    )(dz, win_t, x, dx1, norm_g, mod)


def _sum_slots(slots, own, place, rows_tile, full_cols, name):
    _, r, hcols = slots.shape
    nblk = r // rows_tile

    def body(place_ref, s_ref, own_ref, o_ref):
        chip = place_ref[1]
        acc = jnp.zeros((rows_tile, hcols), F32)
        for k in range(N_CHIPS):
            acc = acc + jnp.where(chip == k, own_ref[...], s_ref[k]).astype(F32)
        o_ref[...] = acc

    return pl.pallas_call(
        body, name=name,
        grid_spec=pltpu.PrefetchScalarGridSpec(
            num_scalar_prefetch=1, grid=(nblk,),
            in_specs=[pl.BlockSpec((N_CHIPS, rows_tile, hcols), lambda i, p: (0, i, 0)),
                      pl.BlockSpec((rows_tile, hcols), lambda i, p: (p[1] * nblk + i, 0))],
            out_specs=pl.BlockSpec((rows_tile, hcols), lambda i, p: (i, p[0]))),
        out_shape=jax.ShapeDtypeStruct((r, full_cols), F32),
        compiler_params=_params(("parallel",)),
    )(place, slots, own)


def _adamw_rows(w, g, m, v, tile, name, after):
    r, c = w.shape

    def body(w_ref, g_ref, m_ref, v_ref, after_ref, go_ref, d_ref, mo_ref, vo_ref):
        g_ = g_ref[...]
        delta, m2, v2 = _adamw(w_ref[...], g_, m_ref[...], v_ref[...])
        go_ref[...] = g_
        d_ref[...] = delta
        mo_ref[...] = m2
        vo_ref[...] = v2

    blk = pl.BlockSpec((tile, c), lambda i: (i, 0))
    return pl.pallas_call(
        body, name=name, grid=(r // tile,), out_shape=(jax.ShapeDtypeStruct((r, c), F32),) * 4,
        in_specs=[blk] * 4 + [ANY], out_specs=(blk,) * 4,
        compiler_params=_params(("parallel",)),
    )(w, g, m, v, after)


def _adamw_ada(cact, dmod, w, m, v, tile=256):
    r, c = w.shape

    def body(c_ref, d_ref, w_ref, m_ref, v_ref, go_ref, dl_ref, mo_ref, vo_ref):
        g_ = _dot(c_ref[...].astype(BF16), d_ref[...].astype(BF16), TN)
        delta, m2, v2 = _adamw(w_ref[...], g_, m_ref[...], v_ref[...])
        go_ref[...] = g_
        dl_ref[...] = delta
        mo_ref[...] = m2
        vo_ref[...] = v2

    blk = pl.BlockSpec((tile, c), lambda i: (i, 0))
    return pl.pallas_call(
        body, name="adamw_ada", grid=(r // tile,), out_shape=(jax.ShapeDtypeStruct((r, c), F32),) * 4,
        in_specs=[pl.BlockSpec((N_DEV, tile), lambda i: (0, i)), pl.BlockSpec((N_DEV, c), lambda i: (0, 0)), blk, blk, blk],
        out_specs=(blk,) * 4,
        compiler_params=_params(("parallel",)),
    )(cact, dmod, w, m, v)


def kernel(x, c, norm_g, w_ada, b_ada, w_in, attn_sinks, sgu_ln_g, sgu_ln_b, sgu_w, sgu_b, w_out, final_g, loss_target, m_norm_g, m_w_ada, m_b_ada, m_w_in, m_attn_sinks, m_sgu_ln_g, m_sgu_ln_b, m_sgu_w, m_sgu_b, m_w_out, m_final_g, v_norm_g, v_w_ada, v_b_ada, v_w_in, v_attn_sinks, v_sgu_ln_g, v_sgu_ln_b, v_sgu_w, v_sgu_b, v_w_out, v_final_g):
    xs, tgt = x[0], loss_target[0]
    core = lax.axis_index("c").astype(jnp.int32).reshape(1)
    chip = 2 * lax.axis_index("x") + lax.axis_index("y")
    place = jnp.stack([core[0], chip.astype(jnp.int32)])
    sgu_bt = sgu_b[0].T

    mod, cact = _mod_exchange(c, w_ada[0], b_ada)
    w_in_t, m_in_t, v_in_t = (jnp.swapaxes(a[0], 0, 1) for a in (w_in, m_w_in, v_w_in))
    win_t = _cast_into_gathered(w_in_t, IN_SHARD // 4, place, "cast_w_in", mod)
    win_t = _gather_routed(win_t, "w_in_gather")
    h = _norm_modulate(xs, norm_g, mod)
    wout = _cast_into_gathered(w_out[0], OUT_SHARD // 2, place, "cast_w_out", win_t)
    sems, bufs, token = _split_start(_gather_copies, [wout], 3, "w_out_gather_start", win_t)
    z = _matmul_nt(h, win_t, 1024, 768, F32, "in_proj", token)
    cat, o = _mixer_fwd(z, attn_sinks, sgu_ln_g, sgu_ln_b, sgu_w[0], sgu_bt)
    wout = _gather_finish(_split_wait(_gather_copies, sems, bufs, cat, "w_out_gather_wait")[0], "w_out_gather_finish")
    dx1, dy, dfg, dgate, loss = _out_proj_loss(cat, wout, xs, tgt, mod, final_g.reshape(1, D_MODEL))

    so = _to_sibling(_grad_matmul(cat, dy, 1 - core, 512, "w_out_grad_sibling"), "w_out_grad_to_sibling")
    po = _grad_matmul(cat, dy, core, 512, "w_out_grad_own", add=so)
    whole_o = functools.partial(_chip_copies, [(j, 0, OUT_SHARD) for j in range(3)])
    sems, bufs, token = _split_start(whole_o, [po, lax.empty((N_CHIPS, OUT_SHARD, HALF), BF16)], 3, "w_out_grad_to_chips_start", so)
    dcat = _matmul_nt(dy, wout, 512, 1024, F32, "out_proj_bwd", token)
    dz, dsinks, dlg, dlb, dsgu_w, dsgu_b = _mixer_bwd(z, dcat, o, attn_sinks, sgu_ln_g, sgu_ln_b, sgu_w[0], sgu_bt)
    po, slots_o = _split_wait(whole_o, sems, bufs, dz, "w_out_grad_to_chips_wait")
    go = _halves_swap(_sum_slots(slots_o, po, place, OUT_SHARD // 2, D_MODEL, "w_out_grad_sum"), "w_out_grad_halves_swap")

    sw = _to_sibling(_grad_matmul(dz, h, 1 - core, 768, "w_in_grad_sibling"), "w_in_grad_to_sibling")
    pw = _grad_matmul(dz, h, core, 768, "w_in_grad_own", add=sw)
    first = functools.partial(_chip_copies, [(0, 0, IN_SHARD), (1, 0, IN_SHARD), (2, 0, IN_SHARD // 4)])
    rest = functools.partial(_chip_copies, [(2, IN_SHARD // 4, 3 * IN_SHARD // 4)])
    sems, bufs, token = _split_start(first, [pw, lax.empty((N_CHIPS, IN_SHARD, HALF), BF16)], 3, "w_in_grad_to_chips_start", sw)
    gx, dshift, dscale, dng = _in_proj_bwd(dz, win_t, xs, dx1, norm_g + token[0:1, 0:1], mod)
    bufs = _split_wait(first, sems, bufs, gx, "w_in_grad_to_chips_wait")

    parts = dict(dng=dng, dshift=dshift, dscale=dscale, dgate=dgate, dfg=dfg, dlg=dlg, dlb=dlb, dsinks=dsinks,
                 dsgu_b=dsgu_b, dsgu_w=dsgu_w, loss=loss)
    small = dict(norm_g=(norm_g, m_norm_g, v_norm_g), b_ada=(b_ada, m_b_ada, v_b_ada),
                 attn_sinks=(attn_sinks, m_attn_sinks, v_attn_sinks), sgu_ln_g=(sgu_ln_g, m_sgu_ln_g, v_sgu_ln_g),
                 sgu_ln_b=(sgu_ln_b, m_sgu_ln_b, v_sgu_ln_b), sgu_w=(sgu_w, m_sgu_w, v_sgu_w), sgu_b=(sgu_b, m_sgu_b, v_sgu_b),
                 final_g=(final_g, m_final_g, v_final_g))
    tot, gsgu_w, dmod_all = _small_grads_sum(parts, bufs[1])
    sems, bufs, token = _split_start(rest, list(bufs), 1, "w_in_grad_rest_to_chips_start", tot)
    res, total = _small_adamw(tot, gsgu_w, small)
    dmod_mine = lax.dynamic_slice_in_dim(dmod_all, ADA_SHARD * chip, ADA_SHARD, axis=1) + token[0:1, 0:1]
    res["w_ada"] = _adamw_ada(cact, dmod_mine, w_ada[0], m_w_ada[0], v_w_ada[0])
    res["w_out"] = _adamw_rows(w_out[0], go, m_w_out[0], v_w_out[0], 256, "adamw_w_out", res["w_ada"][1])

    pw, slots_w = _split_wait(rest, sems, bufs, res["w_out"][1], "w_in_grad_rest_to_chips_wait")
    gw_t = _halves_swap(_sum_slots(slots_w, pw, place, IN_SHARD // 4, D_MODEL, "w_in_grad_sum"), "w_in_grad_halves_swap")
    res["w_in"] = tuple(jnp.swapaxes(a, 0, 1) for a in _adamw_rows(w_in_t, gw_t, m_in_t, v_in_t, IN_SHARD // 4, "adamw_w_in", gw_t))

    order = ["norm_g", "w_ada", "b_ada", "w_in", "attn_sinks", "sgu_ln_g", "sgu_ln_b", "sgu_w", "sgu_b", "w_out", "final_g"]
    like = dict(norm_g=norm_g, w_ada=w_ada, b_ada=b_ada, w_in=w_in, attn_sinks=attn_sinks, sgu_ln_g=sgu_ln_g, sgu_ln_b=sgu_ln_b,
                sgu_w=sgu_w, sgu_b=sgu_b, w_out=w_out, final_g=final_g)
    outs = [total[0, 0], gx[None]]
    for kind in range(4):
        outs += [res[n][kind].reshape(like[n].shape) for n in order]
    return tuple(outs)
```

```python
import functools
import math

import jax
import jax.numpy as jnp
from jax import lax
from jax.experimental import pallas as pl
from jax.experimental.pallas import tpu as pltpu

F32 = jnp.float32
BF16 = jnp.bfloat16

D_MODEL = 2048
D_ATTN = 1024
HEAD_DIM = 64
N_Q_HEADS = 16
Q_PER_KV = 8
D_KV = 128
BLOCK = 128
D_SGU = 1024
SGU_GROUPS = 8
GROUP_DIM = 128
D_IN = 5376
EPS = 1e-6
ATTN_SCALE = 1.0 / math.sqrt(HEAD_DIM)
N_CHIPS = 4
N_DEV = 8
IN_SHARD = D_IN // N_CHIPS
OUT_SHARD = D_MODEL // N_CHIPS
ADA_SHARD = 3 * D_MODEL // N_CHIPS
HALF = D_MODEL // 2

Q0, K0, V0, GA0, U0, VS0, GS0 = 0, 1024, 1152, 1280, 2304, 3328, 4352

ADAM_LR = 0.001
ADAM_B1 = 0.9
ADAM_B2 = 0.999
ADAM_EPS = 1e-08
ADAM_WD = 0.01
ADAM_STEP = 10

VMEM_LIMIT_V7X = 56 * 1024 * 1024

NN = (((1,), (0,)), ((), ()))
NT = (((1,), (1,)), ((), ()))
TN = (((0,), (0,)), ((), ()))
MESH = pl.DeviceIdType.MESH
ANY = pl.BlockSpec(memory_space=pl.ANY)
VMEM = pl.BlockSpec(memory_space=pltpu.VMEM)
HBM = pl.BlockSpec(memory_space=pltpu.HBM)
SEM = pl.BlockSpec(memory_space=pltpu.SEMAPHORE)
EFFECT = pltpu.SideEffectType.DATAFLOW_SIDE_EFFECTING


def _dot(a, b, dims):
    return lax.dot_general(a, b, dims, preferred_element_type=F32)


def _params(sem=None):
    return pltpu.CompilerParams(dimension_semantics=sem, vmem_limit_bytes=VMEM_LIMIT_V7X)


def _sigmoid(x):
    return jax.nn.sigmoid(x)


def _adamw(w, g, m, v):
    m = ADAM_B1 * m + (1.0 - ADAM_B1) * g
    v = ADAM_B2 * v + (1.0 - ADAM_B2) * (g * g)
    m_hat = m / (1.0 - ADAM_B1 ** ADAM_STEP)
    v_hat = v / (1.0 - ADAM_B2 ** ADAM_STEP)
    delta = -ADAM_LR * (m_hat / (jnp.sqrt(v_hat) + ADAM_EPS) + ADAM_WD * w)
    return delta, m, v


def _place():
    return lax.axis_index("x"), lax.axis_index("y"), lax.axis_index("c")


def _flip(p, bit):
    return 1 - p if bit else p


def _other_chips(x, y):
    return [(_flip(x, bx), _flip(y, by)) for bx, by in ((0, 1), (1, 0), (1, 1))]


def _split_start(copies, bufs, n_sems, name, after):
    n = len(bufs)

    def body(*refs):
        send_sems, recv_sems = refs[n + 1:n + 3]
        for cp in copies(*refs[:n], send_sems, recv_sems):
            cp.start()
        refs[-1][...] = jnp.zeros_like(refs[-1])

    bufs = [pltpu.with_memory_space_constraint(a, pltpu.HBM) for a in bufs]
    outs = pl.pallas_call(
        body, name=name,
        out_shape=(pltpu.SemaphoreType.DMA((n_sems,)), pltpu.SemaphoreType.DMA((n_sems,)))
        + tuple(pltpu.HBM(a.shape, a.dtype) for a in bufs) + (jax.ShapeDtypeStruct((8, 128), F32),),
        in_specs=[HBM] * n + [ANY], out_specs=(SEM, SEM) + (HBM,) * n + (VMEM,), input_output_aliases={k: k + 2 for k in range(n)},
        compiler_params=pltpu.CompilerParams(has_side_effects=EFFECT),
    )(*bufs, after)
    return outs[:2], list(outs[2:2 + n]), outs[-1]


def _split_wait(copies, sems, bufs, after, name):
    n = len(bufs)

    def body(*refs):
        send_sems, recv_sems = refs[n:n + 2]
        for cp in copies(*refs[:n], send_sems, recv_sems):
            cp.wait_send()
            cp.wait_recv()

    return pl.pallas_call(
        body, name=name, out_shape=tuple(pltpu.HBM(a.shape, a.dtype) for a in bufs),
        in_specs=[HBM] * n + [SEM, SEM, ANY], out_specs=(HBM,) * n, input_output_aliases={k: k for k in range(n)},
        compiler_params=pltpu.CompilerParams(has_side_effects=EFFECT),
    )(*bufs, *sems, after)


def _gather_copies(full_ref, send_sems, recv_sems):
    x, y, cc = _place()
    shard = full_ref.shape[0] // N_CHIPS
    half = shard // 2
    mine = full_ref.at[pl.ds(shard * (2 * x + y) + half * cc, half), :]
    return [pltpu.make_async_remote_copy(src_ref=mine, dst_ref=mine, send_sem=send_sems.at[j], recv_sem=recv_sems.at[j],
                                         device_id=(px, py, cc), device_id_type=MESH)
            for j, (px, py) in enumerate(_other_chips(x, y))]


def _gather_w_in_and_modulate(full, c, w_ada, b_ada):
    shard = full.shape[0] // N_CHIPS
    half = shard // 2
    quarter = half // 2
    kdim = w_ada.shape[0]
    kc = 256

    def body(full_in, c_ref, w_hbm, b_ref, full_ref, mod_ref, cact_ref, w_vmem, cbuf, pbuf, rbuf, w_sem,
             send_sems, recv_sems, msend_sems, mrecv_sems):
        x, y, cc = _place()
        me = 4 * x + 2 * y + cc
        chip = 2 * x + y
        across_x, across_y, far = (1 - x, y), (x, 1 - y), (1 - x, 1 - y)

        def rows(chip_xy, core, q):
            return full_ref.at[pl.ds(shard * (2 * chip_xy[0] + chip_xy[1]) + half * core + quarter * q, quarter), :]

        def copy(k, piece, to):
            return pltpu.make_async_remote_copy(src_ref=piece, dst_ref=piece, send_sem=send_sems.at[k], recv_sem=recv_sems.at[k],
                                                device_id=to, device_id_type=MESH)

        def small(k, src, dst, to):
            return pltpu.make_async_remote_copy(src_ref=src, dst_ref=dst, send_sem=msend_sems.at[k], recv_sem=mrecv_sems.at[k],
                                                device_id=to, device_id_type=MESH)

        cbuf[me] = c_ref[...]
        gathers = [small(k - 1, cbuf.at[me], cbuf.at[me], (_flip(x, (k >> 2) & 1), _flip(y, (k >> 1) & 1), _flip(cc, k & 1)))
                   for k in range(1, N_DEV)]
        for cp in gathers:
            cp.start()
        w_load = pltpu.make_async_copy(w_hbm, w_vmem, w_sem)
        w_load.start()
        direct = [copy(0, rows((x, y), cc, 0), (*across_x, cc)), copy(2, rows((x, y), cc, 1), (*across_y, cc)),
                  copy(1, rows((x, y), cc, 1), (*across_x, cc)), copy(3, rows((x, y), cc, 0), (*across_y, cc))]
        for cp in direct:
            cp.start()

        for cp in gathers:
            cp.wait()
        w_load.wait()
        for b in range(N_DEV):
            cv = cbuf[b]
            cact_ref[b:b + 1, :] = cv * _sigmoid(cv)
        acc = jnp.zeros((N_DEV, ADA_SHARD), F32)
        for k0 in range(0, kdim, kc):
            acc = acc + _dot(cact_ref[:, k0:k0 + kc].astype(BF16), w_vmem[k0:k0 + kc, :].astype(BF16), NN)
        for b in range(N_DEV):
            pbuf[b] = acc[b:b + 1, :]
        rbuf[chip] = pbuf[me]
        sends = [small(N_DEV - 1 + j, pbuf.at[4 * px + 2 * py + cc], rbuf.at[chip], (px, py, cc))
                 for j, (px, py) in enumerate(_other_chips(x, y))]
        for cp in sends:
            cp.start()

        landing = [(0, across_x, 0, (4, (*across_y, cc))), (2, across_y, 1, (5, (*across_x, cc))),
                   (1, across_x, 1, None), (3, across_y, 0, None), (4, far, 0, None), (5, far, 1, None)]
        passed = []
        for n, (k, whose, q, onward) in enumerate(landing):
            piece = rows(whose, cc, q)
            copy(k, piece, (x, y, cc)).wait_recv()
            if onward is not None:
                passed.append(copy(onward[0], piece, onward[1]))
                passed[-1].start()
            passed.append(copy(6 + n, piece, (x, y, 1 - cc)))
            passed[-1].start()
        for n, (k, whose, q, onward) in enumerate(landing):
            copy(6 + n, rows(whose, 1 - cc, q), (x, y, cc)).wait_recv()
        for cp in sends:
            cp.wait()
        for k in range(N_CHIPS):
            cols = slice(ADA_SHARD * k, ADA_SHARD * (k + 1))
            mod_ref[:, cols] = rbuf[k] + b_ref[:, cols]
        for cp in direct + passed:
            cp.wait_send()

    return pl.pallas_call(
        body, name="w_in_gather_and_modulate",
        out_shape=(jax.ShapeDtypeStruct(full.shape, full.dtype), jax.ShapeDtypeStruct((1, 3 * D_MODEL), F32),
                   jax.ShapeDtypeStruct((N_DEV, D_MODEL), F32)),
        in_specs=[ANY, VMEM, ANY, VMEM], out_specs=(ANY, VMEM, VMEM), input_output_aliases={0: 0},
        scratch_shapes=[pltpu.VMEM(w_ada.shape, F32), pltpu.VMEM((N_DEV, 1, D_MODEL), F32), pltpu.VMEM((N_DEV, 1, ADA_SHARD), F32),
                        pltpu.VMEM((N_CHIPS, 1, ADA_SHARD), F32), pltpu.SemaphoreType.DMA,
                        pltpu.SemaphoreType.DMA((12,)), pltpu.SemaphoreType.DMA((12,)),
                        pltpu.SemaphoreType.DMA((N_DEV + 2,)), pltpu.SemaphoreType.DMA((N_DEV + 2,))],
        compiler_params=_params(),
    )(full, c, w_ada, b_ada)


def _gather_finish(full, name):
    shard = full.shape[0] // N_CHIPS
    half = shard // 2

    def body(full_in, full_ref, send_sems, recv_sems):
        x, y, cc = _place()
        cps = []
        for j, (px, py) in enumerate(_other_chips(x, y)):
            landed = full_ref.at[pl.ds(shard * (2 * px + py) + half * cc, half), :]
            cps.append(pltpu.make_async_remote_copy(src_ref=landed, dst_ref=landed, send_sem=send_sems.at[j], recv_sem=recv_sems.at[j],
                                                    device_id=(x, y, 1 - cc), device_id_type=MESH))
        for cp in cps:
            cp.start()
        for cp in cps:
            cp.wait()

    return pl.pallas_call(
        body, name=name, out_shape=jax.ShapeDtypeStruct(full.shape, full.dtype),
        in_specs=[ANY], out_specs=ANY, input_output_aliases={0: 0},
        scratch_shapes=[pltpu.SemaphoreType.DMA((3,)), pltpu.SemaphoreType.DMA((3,))],
        compiler_params=_params(),
    )(full)


def _to_sibling(g, name):
    def body(g_ref, r_ref, send_sem, recv_sem):
        x, y, cc = _place()
        cp = pltpu.make_async_remote_copy(src_ref=g_ref, dst_ref=r_ref, send_sem=send_sem, recv_sem=recv_sem,
                                          device_id=(x, y, 1 - cc), device_id_type=MESH)
        cp.start()
        cp.wait()

    return pl.pallas_call(
        body, name=name, out_shape=jax.ShapeDtypeStruct(g.shape, g.dtype), in_specs=[ANY], out_specs=ANY,
        scratch_shapes=[pltpu.SemaphoreType.DMA, pltpu.SemaphoreType.DMA], compiler_params=_params(),
    )(g)


def _chip_copies(pieces, p_ref, land_ref, send_sems, recv_sems):
    x, y, cc = _place()
    chip = 2 * x + y
    shard = land_ref.shape[1]
    others = _other_chips(x, y)
    cps = []
    for k, (j, r0, nr) in enumerate(pieces):
        px, py = others[j]
        cps.append(pltpu.make_async_remote_copy(
            src_ref=p_ref.at[pl.ds(shard * (2 * px + py) + r0, nr), :], dst_ref=land_ref.at[chip, pl.ds(r0, nr), :],
            send_sem=send_sems.at[k], recv_sem=recv_sems.at[k], device_id=(px, py, cc), device_id_type=MESH))
    return cps


def _halves_swap(f, name):
    def body(f_in, f_ref, send_sem, recv_sem):
        x, y, cc = _place()
        mine = f_ref.at[:, pl.ds(pl.multiple_of(HALF * cc, 128), HALF)]
        cp = pltpu.make_async_remote_copy(src_ref=mine, dst_ref=mine, send_sem=send_sem, recv_sem=recv_sem,
                                          device_id=(x, y, 1 - cc), device_id_type=MESH)
        cp.start()
        cp.wait()

    return pl.pallas_call(
        body, name=name, out_shape=jax.ShapeDtypeStruct(f.shape, f.dtype),
        in_specs=[ANY], out_specs=ANY, input_output_aliases={0: 0},
        scratch_shapes=[pltpu.SemaphoreType.DMA, pltpu.SemaphoreType.DMA], compiler_params=_params(),
    )(f)


ROW_NORM_G, ROW_DMOD, ROW_FINAL_G, ROW_LN, ROW_SINKS, ROW_LOSS, ROW_SGU_B, MISC_ROWS = 0, 1, 4, 5, 6, 7, 8, 16


def _small_grads_sum(parts, after):
    part_names = ["dng", "dshift", "dscale", "dgate", "dfg", "dlg", "dlb", "dsinks", "dsgu_b", "dsgu_w", "loss"]
    n_parts = len(part_names)
    n_peers = N_DEV - 1

    def body(*refs):
        p = dict(zip(part_names, refs[:n_parts]))
        tot_ref, gw_ref, dmod_ref = refs[n_parts + 1:n_parts + 4]
        misc, cmisc, dbuf, wland, wsum, send_sems, recv_sems = refs[n_parts + 4:]
        x, y, cc = _place()
        me = 4 * x + 2 * y + cc
        chip = 2 * x + y
        peers = [(_flip(x, (k >> 2) & 1), _flip(y, (k >> 1) & 1), _flip(cc, k & 1)) for k in range(1, N_DEV)]

        def remote(k, src, dst, peer):
            return pltpu.make_async_remote_copy(src_ref=src, dst_ref=dst, send_sem=send_sems.at[k], recv_sem=recv_sems.at[k],
                                                device_id=peer, device_id_type=MESH)

        dbuf[me] = jnp.zeros((8, D_MODEL), F32)
        dbuf[me, 0:1, :] = p["dshift"][...]
        dbuf[me, 1:2, :] = p["dscale"][...]
        dbuf[me, 2:3, :] = p["dgate"][...]
        gathers = [remote(k, dbuf.at[me], dbuf.at[me], peer) for k, peer in enumerate(peers)]
        wland[me] = p["dsgu_w"][me]
        scatters = [remote(n_peers + k, p["dsgu_w"].at[4 * px + 2 * py + pc], wland.at[me], (px, py, pc))
                    for k, (px, py, pc) in enumerate(peers)]

        misc[cc] = jnp.zeros((MISC_ROWS, D_MODEL), F32)
        misc[cc, ROW_NORM_G:ROW_NORM_G + 1, :] = p["dng"][...]
        misc[cc, ROW_DMOD:ROW_DMOD + 1, :] = p["dshift"][...]
        misc[cc, ROW_DMOD + 1:ROW_DMOD + 2, :] = p["dscale"][...]
        misc[cc, ROW_DMOD + 2:ROW_DMOD + 3, :] = p["dgate"][...]
        misc[cc, ROW_FINAL_G:ROW_FINAL_G + 1, :] = p["dfg"][...]
        misc[cc, ROW_LN:ROW_LN + 1, 0:D_SGU] = p["dlg"][...]
        misc[cc, ROW_LN:ROW_LN + 1, D_SGU:2 * D_SGU] = p["dlb"][...]
        misc[cc, ROW_SINKS:ROW_SINKS + 1, 0:N_Q_HEADS] = p["dsinks"][...]
        misc[cc, ROW_LOSS:ROW_LOSS + 1, 0:1] = p["loss"][...]
        misc[cc, ROW_SGU_B:ROW_SGU_B + SGU_GROUPS, 0:BLOCK] = p["dsgu_b"][...]
        pair = remote(3 * n_peers, misc.at[cc], misc.at[cc], (x, y, 1 - cc))
        for cp in gathers + scatters + [pair]:
            cp.start()
        pair.wait()
        cmisc[chip] = misc[0] + misc[1]
        quad = [remote(3 * n_peers + 1 + j, cmisc.at[chip], cmisc.at[chip], (_flip(x, bx), _flip(y, by), cc))
                for j, (bx, by) in enumerate(((0, 1), (1, 0), (1, 1)))]
        for cp in quad:
            cp.start()
        for cp in scatters:
            cp.wait()
        group = wland[0]
        for b in range(1, N_DEV):
            group = group + wland[b]
        wsum[me] = group
        spreads = [remote(2 * n_peers + k, wsum.at[me], wsum.at[me], peer) for k, peer in enumerate(peers)]
        for cp in spreads:
            cp.start()
        for cp in gathers + quad + spreads:
            cp.wait()

        tot = cmisc[0]
        for k in range(1, N_CHIPS):
            tot = tot + cmisc[k]
        tot_ref[...] = tot
        gw_ref[...] = wsum[...]
        for b in range(N_DEV):
            dmod_ref[b:b + 1, :] = jnp.concatenate([dbuf[b, t:t + 1, :] for t in range(3)], axis=1)

    operands = [parts[n] for n in part_names]
    n_sems = 3 * n_peers + 1 + (N_CHIPS - 1)
    return pl.pallas_call(
        body, name="small_grads_sum",
        out_shape=(jax.ShapeDtypeStruct((MISC_ROWS, D_MODEL), F32), jax.ShapeDtypeStruct((SGU_GROUPS, BLOCK, BLOCK), F32),
                   jax.ShapeDtypeStruct((N_DEV, 3 * D_MODEL), F32)),
        in_specs=[VMEM] * len(operands) + [ANY], out_specs=(VMEM, VMEM, VMEM),
        scratch_shapes=[pltpu.VMEM((2, MISC_ROWS, D_MODEL), F32), pltpu.VMEM((N_CHIPS, MISC_ROWS, D_MODEL), F32),
                        pltpu.VMEM((N_DEV, 8, D_MODEL), F32), pltpu.VMEM((N_DEV, BLOCK, BLOCK), F32),
                        pltpu.VMEM((SGU_GROUPS, BLOCK, BLOCK), F32),
                        pltpu.SemaphoreType.DMA((n_sems,)), pltpu.SemaphoreType.DMA((n_sems,))],
        compiler_params=_params(),
    )(*operands, after)


def _small_adamw(tot, gw, small):
    names = ["norm_g", "b_ada", "attn_sinks", "sgu_ln_g", "sgu_ln_b", "sgu_w", "sgu_b", "final_g"]

    def body(*refs):
        tot_ref, gw_ref = refs[:2]
        wmv = {n: refs[2 + 3 * i: 5 + 3 * i] for i, n in enumerate(names)}
        outs = refs[2 + 3 * len(names):]
        res = {n: outs[4 * i: 4 * i + 4] for i, n in enumerate(names)}
        outs[4 * len(names)][...] = tot_ref[ROW_LOSS:ROW_LOSS + 1, 0:1]

        def update(name, g, pick=lambda r: r[...], put=None):
            w_ref, m_ref, v_ref = wmv[name]
            delta, m2, v2 = _adamw(pick(w_ref), g, pick(m_ref), pick(v_ref))
            for ref, val in zip(res[name], (g, delta, m2, v2)):
                if put is None:
                    ref[...] = val
                else:
                    put(ref, val)

        update("norm_g", tot_ref[ROW_NORM_G:ROW_NORM_G + 1, :])
        update("final_g", tot_ref[ROW_FINAL_G:ROW_FINAL_G + 1, :])
        for t in range(3):
            cols = slice(D_MODEL * t, D_MODEL * (t + 1))

            def put_cols(ref, val, cols=cols):
                ref[:, cols] = val
            update("b_ada", tot_ref[ROW_DMOD + t:ROW_DMOD + t + 1, :], pick=lambda r, cols=cols: r[:, cols], put=put_cols)
        update("sgu_ln_g", tot_ref[ROW_LN:ROW_LN + 1, 0:D_SGU])
        update("sgu_ln_b", tot_ref[ROW_LN:ROW_LN + 1, D_SGU:2 * D_SGU])
        update("attn_sinks", tot_ref[ROW_SINKS:ROW_SINKS + 1, 0:N_Q_HEADS])
        update("sgu_b", tot_ref[ROW_SGU_B:ROW_SGU_B + SGU_GROUPS, 0:BLOCK])
        for g in range(SGU_GROUPS):
            def put_group(ref, val, g=g):
                ref[g] = val
            update("sgu_w", gw_ref[g], pick=lambda r, g=g: r[g], put=put_group)

    shapes = {"norm_g": (1, D_MODEL), "b_ada": (1, 3 * D_MODEL), "attn_sinks": (1, N_Q_HEADS), "sgu_ln_g": (1, D_SGU),
              "sgu_ln_b": (1, D_SGU), "sgu_w": (SGU_GROUPS, BLOCK, BLOCK), "sgu_b": (SGU_GROUPS, BLOCK), "final_g": (1, D_MODEL)}
    operands = [tot, gw]
    for n in names:
        operands += [a.reshape(shapes[n]) for a in small[n]]
    out_shape = []
    for n in names:
        out_shape += [jax.ShapeDtypeStruct(shapes[n], F32)] * 4
    out_shape.append(jax.ShapeDtypeStruct((1, 1), F32))
    outs = pl.pallas_call(
        body, name="small_adamw", out_shape=tuple(out_shape),
        in_specs=[VMEM] * len(operands), out_specs=tuple([VMEM] * len(out_shape)), compiler_params=_params(),
    )(*operands)
    return {n: outs[4 * i: 4 * i + 4] for i, n in enumerate(names)}, outs[4 * len(names)]


def _cast_into_gathered(w, tile, place, name, after):
    r, c = w.shape
    nblk = r // tile

    def body(place_ref, w_ref, after_ref, o_ref):
        o_ref[...] = w_ref[...].astype(BF16)

    return pl.pallas_call(
        body, name=name,
        grid_spec=pltpu.PrefetchScalarGridSpec(
            num_scalar_prefetch=1, grid=(nblk,), in_specs=[pl.BlockSpec((tile, c), lambda i, p: (i, 0)), ANY],
            out_specs=pl.BlockSpec((tile, c), lambda i, p: (p[1] * nblk + i, 0))),
        out_shape=jax.ShapeDtypeStruct((N_CHIPS * r, c), BF16),
        compiler_params=_params(("parallel",)),
    )(place, w, after)


def _norm_modulate(x, norm_g, mod):
    s, d = x.shape
    tm = min(256, s)

    def body(x_ref, g_ref, mod_ref, h_ref):
        xv = x_ref[...]
        r = lax.rsqrt(jnp.mean(xv * xv, axis=-1, keepdims=True) + EPS)
        h = (xv * r * g_ref[...]) * (1.0 + mod_ref[:, d:2 * d]) + mod_ref[:, 0:d]
        h_ref[...] = h.astype(BF16)

    return pl.pallas_call(
        body, name="norm_modulate", grid=(s // tm,), out_shape=jax.ShapeDtypeStruct((s, d), BF16),
        in_specs=[pl.BlockSpec((tm, d), lambda i: (i, 0)), pl.BlockSpec((1, d), lambda i: (0, 0)),
                  pl.BlockSpec((1, 3 * d), lambda i: (0, 0))],
        out_specs=pl.BlockSpec((tm, d), lambda i: (i, 0)),
        compiler_params=_params(("parallel",)),
    )(x, norm_g, mod)


def _matmul_nt(a, bt, tm, tn, out_dtype, name, after):
    m, k = a.shape
    n = bt.shape[0]
    tm, tn = min(tm, m), min(tn, n)

    def body(a_ref, b_ref, after_ref, o_ref):
        o_ref[...] = _dot(a_ref[...], b_ref[...], NT).astype(out_dtype)

    return pl.pallas_call(
        body, name=name, grid=(m // tm, n // tn), out_shape=jax.ShapeDtypeStruct((m, n), out_dtype),
        in_specs=[pl.BlockSpec((tm, k), lambda i, j: (i, 0)), pl.BlockSpec((tn, k), lambda i, j: (j, 0)), ANY],
        out_specs=pl.BlockSpec((tm, tn), lambda i, j: (i, j)),
        compiler_params=_params(("parallel", "parallel")),
    )(a, bt, after)


def _grad_matmul(a, b, half, tm, name, add=None):
    t, m = a.shape
    tm = min(tm, m)

    def body(half_ref, a_ref, b_ref, *rest):
        if add is None:
            (o_ref,) = rest
            o_ref[...] = _dot(a_ref[...], b_ref[...], TN).astype(BF16)
        else:
            add_ref, o_ref = rest
            o_ref[...] = (_dot(a_ref[...], b_ref[...], TN) + add_ref[...].astype(F32)).astype(BF16)

    in_specs = [pl.BlockSpec((t, tm), lambda i, h: (0, i)), pl.BlockSpec((t, HALF), lambda i, h: (0, h[0]))]
    operands = [a, b]
    if add is not None:
        in_specs.append(pl.BlockSpec((tm, HALF), lambda i, h: (i, 0)))
        operands.append(add)
    return pl.pallas_call(
        body, name=name,
        grid_spec=pltpu.PrefetchScalarGridSpec(num_scalar_prefetch=1, grid=(m // tm,), in_specs=in_specs,
                                               out_specs=pl.BlockSpec((tm, HALF), lambda i, h: (i, 0))),
        out_shape=jax.ShapeDtypeStruct((m, HALF), BF16),
        compiler_params=_params(("parallel",)),
    )(half, *operands)


def _mixer_masks(n):
    key = lax.broadcasted_iota(jnp.int32, (2 * BLOCK, Q_PER_KV * BLOCK), 0)
    qry = lax.broadcasted_iota(jnp.int32, (2 * BLOCK, Q_PER_KV * BLOCK), 1) & (BLOCK - 1)
    valid = (key > qry) & (key <= qry + BLOCK) & ((key >= BLOCK) | (n > 0))
    r2 = lax.broadcasted_iota(jnp.int32, (BLOCK, BLOCK), 0)
    c2 = lax.broadcasted_iota(jnp.int32, (BLOCK, BLOCK), 1)
    return valid, r2 >= c2


def _layer_norm_parts(vs):
    mu = jnp.mean(vs, axis=-1, keepdims=True)
    xc = vs - mu
    rstd = lax.rsqrt(jnp.mean(xc * xc, axis=-1, keepdims=True) + EPS)
    return xc * rstd, rstd


def _stack_heads(ref, base, hk):
    first = base + HEAD_DIM * Q_PER_KV * hk
    return jnp.concatenate([ref[:, first + HEAD_DIM * g:first + HEAD_DIM * (g + 1)] for g in range(Q_PER_KV)], axis=0)


def _head_cols(base, h):
    return slice(base + HEAD_DIM * h, base + HEAD_DIM * (h + 1))


def _sink_row(sink_ref, hk):
    return jnp.concatenate([jnp.broadcast_to(sink_ref[:, h:h + 1], (1, BLOCK))
                            for h in range(Q_PER_KV * hk, Q_PER_KV * (hk + 1))], axis=1)


def _attn_probs(qs, kh, valid, sk):
    s = _dot(kh, qs, NT) * ATTN_SCALE
    s = jnp.where(valid, s, -jnp.inf)
    m = jnp.maximum(jnp.max(s, axis=0, keepdims=True), sk)
    e = jnp.exp(s - m)
    es = jnp.exp(sk - m)
    inv = 1.0 / (jnp.sum(e, axis=0, keepdims=True) + es)
    return e * inv, es * inv


def _kv_ext(z_ref, kvp_ref):
    k_ext = jnp.concatenate([kvp_ref[:, 0:D_KV], z_ref[:, K0:K0 + D_KV]], axis=0).astype(BF16)
    v_ext = jnp.concatenate([kvp_ref[:, D_KV:2 * D_KV], z_ref[:, V0:V0 + D_KV]], axis=0).astype(BF16)
    return k_ext, v_ext


def _sgu_mixed(w_ref, bt_ref, vn, tril):
    ws = [jnp.where(tril, w_ref[g], 0.0).astype(BF16) for g in range(SGU_GROUPS)]
    mixed = jnp.concatenate([_dot(ws[g], vn[:, GROUP_DIM * g:GROUP_DIM * (g + 1)], NN) + bt_ref[:, g:g + 1]
                             for g in range(SGU_GROUPS)], axis=1)
    return ws, mixed


def _mixer_fwd(z, sinks, ln_g, ln_b, sgu_w, sgu_bt):
    s = z.shape[0]
    nb = s // BLOCK

    def body(z_ref, kvp_ref, sink_ref, lg_ref, lb_ref, w_ref, bt_ref, cat_ref, o_ref):
        n = pl.program_id(0)
        valid, tril = _mixer_masks(n)
        k_ext, v_ext = _kv_ext(z_ref, kvp_ref)
        for hk in range(2):
            kv = slice(HEAD_DIM * hk, HEAD_DIM * (hk + 1))
            p, _ = _attn_probs(_stack_heads(z_ref, Q0, hk).astype(BF16), k_ext[:, kv], valid, _sink_row(sink_ref, hk))
            o_t = _dot(v_ext[:, kv], p.astype(BF16), TN)
            for g in range(Q_PER_KV):
                h = Q_PER_KV * hk + g
                o = o_t[:, BLOCK * g:BLOCK * (g + 1)].T
                ga = z_ref[:, _head_cols(GA0, h)]
                o_ref[:, _head_cols(0, h)] = o
                cat_ref[:, _head_cols(0, h)] = (o * (ga * _sigmoid(ga))).astype(BF16)
        xhat, _ = _layer_norm_parts(z_ref[:, VS0:VS0 + D_SGU])
        vn = (xhat * lg_ref[...] + lb_ref[...]).astype(BF16)
        _, mixed = _sgu_mixed(w_ref, bt_ref, vn, tril)
        gt = z_ref[:, GS0:GS0 + D_SGU]
        cat_ref[:, D_ATTN:D_MODEL] = ((z_ref[:, U0:U0 + D_SGU] * mixed) * (gt * _sigmoid(gt))).astype(BF16)

    const2 = lambda n: (0, 0)
    return pl.pallas_call(
        body, name="mixer_fwd", grid=(nb,),
        out_shape=(jax.ShapeDtypeStruct((s, D_MODEL), BF16), jax.ShapeDtypeStruct((s, D_ATTN), F32)),
        in_specs=[pl.BlockSpec((BLOCK, D_IN), lambda n: (n, 0)),
                  pl.BlockSpec((BLOCK, 2 * D_KV), lambda n: (jnp.maximum(n - 1, 0), K0 // (2 * D_KV))),
                  pl.BlockSpec((1, N_Q_HEADS), const2), pl.BlockSpec((1, D_SGU), const2), pl.BlockSpec((1, D_SGU), const2),
                  pl.BlockSpec((SGU_GROUPS, BLOCK, BLOCK), lambda n: (0, 0, 0)), pl.BlockSpec((BLOCK, SGU_GROUPS), const2)],
        out_specs=(pl.BlockSpec((BLOCK, D_MODEL), lambda n: (n, 0)), pl.BlockSpec((BLOCK, D_ATTN), lambda n: (n, 0))),
        compiler_params=_params(("parallel",)),
    )(z, z, sinks, ln_g, ln_b, sgu_w, sgu_bt)


def _out_proj_loss(cat, wout, x, target, mod, final_g):
    s, d = x.shape
    tm = min(256, s)

    def body(cat_ref, w_ref, x_ref, t_ref, mod_ref, fg_ref, dx1_ref, dy_ref, dfg_ref, dgate_ref, loss_ref):
        @pl.when(pl.program_id(0) == 0)
        def _():
            dfg_ref[...] = jnp.zeros_like(dfg_ref)
            dgate_ref[...] = jnp.zeros_like(dgate_ref)
            loss_ref[...] = jnp.zeros_like(loss_ref)

        gate = mod_ref[:, 2 * d:3 * d]
        fg = fg_ref[...]
        yv = _dot(cat_ref[...], w_ref[...], NN)
        x1 = x_ref[...] + gate * yv
        r = lax.rsqrt(jnp.mean(x1 * x1, axis=-1, keepdims=True) + EPS)
        xr = x1 * r
        err = xr * fg - t_ref[...]
        dout = err * (1.0 / d)
        gy = dout * fg
        dx1 = r * gy - xr * (r * r * jnp.mean(gy * x1, axis=-1, keepdims=True))
        dx1_ref[...] = dx1
        dy_ref[...] = (gate * dx1).astype(BF16)
        dfg_ref[...] += jnp.sum(dout * xr, axis=0, keepdims=True)
        dgate_ref[...] += jnp.sum(dx1 * yv, axis=0, keepdims=True)
        loss_ref[...] += 0.5 * jnp.sum(jnp.mean(err * err, axis=-1, keepdims=True), axis=0, keepdims=True)

    row = lambda i: (i, 0)
    const = lambda i: (0, 0)
    return pl.pallas_call(
        body, name="out_proj_loss", grid=(s // tm,),
        out_shape=(jax.ShapeDtypeStruct((s, d), F32), jax.ShapeDtypeStruct((s, d), BF16), jax.ShapeDtypeStruct((1, d), F32),
                   jax.ShapeDtypeStruct((1, d), F32), jax.ShapeDtypeStruct((1, 1), F32)),
        in_specs=[pl.BlockSpec((tm, d), row), pl.BlockSpec((d, d), const), pl.BlockSpec((tm, d), row), pl.BlockSpec((tm, d), row),
                  pl.BlockSpec((1, 3 * d), const), pl.BlockSpec((1, d), const)],
        out_specs=(pl.BlockSpec((tm, d), row), pl.BlockSpec((tm, d), row), pl.BlockSpec((1, d), const), pl.BlockSpec((1, d), const),
                   pl.BlockSpec((1, 1), const)),
        compiler_params=_params(("arbitrary",)),
    )(cat, wout, x, target, mod, final_g)


def _mixer_bwd(z, dcat, o, sinks, ln_g, ln_b, sgu_w, sgu_bt):
    s = z.shape[0]
    nb = s // BLOCK

    def body(z_ref, kvp_ref, dc_ref, o_ref, sink_ref, lg_ref, lb_ref, w_ref, bt_ref,
             dz_ref, dsink_ref, dlg_ref, dlb_ref, dw_ref, db_ref, pend, dkv, carry, dsink_acc, dbt_acc):
        n = pl.program_id(0)

        @pl.when(n == 0)
        def _():
            pend[...] = jnp.zeros_like(pend)
            carry[...] = jnp.zeros_like(carry)
            dsink_acc[...] = jnp.zeros_like(dsink_acc)
            dbt_acc[...] = jnp.zeros_like(dbt_acc)
            dlg_ref[...] = jnp.zeros_like(dlg_ref)
            dlb_ref[...] = jnp.zeros_like(dlb_ref)
            dw_ref[...] = jnp.zeros_like(dw_ref)

        dz_ref[:, 0:K0] = pend[:, 0:K0]
        dz_ref[:, GA0:D_IN] = pend[:, GA0:D_IN]

        @pl.when(n == nb)
        def _():
            dkv[...] = jnp.zeros_like(dkv)

        @pl.when(n < nb)
        def _():
            valid, tril = _mixer_masks(n)
            k_ext, v_ext = _kv_ext(z_ref, kvp_ref)
            for hk in range(2):
                kv = slice(HEAD_DIM * hk, HEAD_DIM * (hk + 1))
                kh, vh = k_ext[:, kv], v_ext[:, kv]
                qs = _stack_heads(z_ref, Q0, hk).astype(BF16)
                p, p_sink = _attn_probs(qs, kh, valid, _sink_row(sink_ref, hk))
                dos = []
                for g in range(Q_PER_KV):
                    h = Q_PER_KV * hk + g
                    ga = z_ref[:, _head_cols(GA0, h)]
                    sg = _sigmoid(ga)
                    datt = dc_ref[:, _head_cols(0, h)]
                    dos.append(datt * (ga * sg))
                    pend[:, _head_cols(GA0, h)] = (datt * o_ref[:, _head_cols(0, h)] * (sg * (1.0 + ga * (1.0 - sg)))).astype(BF16)
                do = jnp.concatenate(dos, axis=0).astype(BF16)
                dp = _dot(vh, do, NT)
                delta = jnp.sum(p * dp, axis=0, keepdims=True)
                ds = (p * (dp - delta) * ATTN_SCALE).astype(BF16)
                dsk = -p_sink * delta
                dq_t = _dot(kh, ds, TN)
                for g in range(Q_PER_KV):
                    h = Q_PER_KV * hk + g
                    dsink_acc[:, h:h + 1] += jnp.sum(dsk[:, BLOCK * g:BLOCK * (g + 1)], axis=1, keepdims=True)
                    pend[:, _head_cols(Q0, h)] = dq_t[:, BLOCK * g:BLOCK * (g + 1)].T.astype(BF16)
                dkv[:, kv] = _dot(ds, qs, NN)
                dkv[:, D_KV + HEAD_DIM * hk:D_KV + HEAD_DIM * (hk + 1)] = _dot(p.astype(BF16), do, NN)

            xhat, rstd = _layer_norm_parts(z_ref[:, VS0:VS0 + D_SGU])
            lg = lg_ref[...]
            vn = (xhat * lg + lb_ref[...]).astype(BF16)
            ws, mixed = _sgu_mixed(w_ref, bt_ref, vn, tril)
            u = z_ref[:, U0:U0 + D_SGU]
            gt = z_ref[:, GS0:GS0 + D_SGU]
            sg = _sigmoid(gt)
            dsgu = dc_ref[:, D_ATTN:D_MODEL]
            dgated = dsgu * (gt * sg)
            pend[:, U0:U0 + D_SGU] = (dgated * mixed).astype(BF16)
            pend[:, GS0:GS0 + D_SGU] = (dsgu * (u * mixed) * (sg * (1.0 + gt * (1.0 - sg)))).astype(BF16)
            dmixed = dgated * u
            dmb = dmixed.astype(BF16)
            dvn_parts = []
            for g in range(SGU_GROUPS):
                gs_ = slice(GROUP_DIM * g, GROUP_DIM * (g + 1))
                dw_ref[g] += jnp.where(tril, _dot(dmb[:, gs_], vn[:, gs_], NT), 0.0)
                dbt_acc[:, g:g + 1] += jnp.sum(dmixed[:, gs_], axis=-1, keepdims=True)
                dvn_parts.append(_dot(ws[g], dmb[:, gs_], TN))
            dvn = jnp.concatenate(dvn_parts, axis=1)
            dlg_ref[...] += jnp.sum(dvn * xhat, axis=0, keepdims=True)
            dlb_ref[...] += jnp.sum(dvn, axis=0, keepdims=True)
            dxh = dvn * lg
            pend[:, VS0:VS0 + D_SGU] = (rstd * (dxh - jnp.mean(dxh, axis=-1, keepdims=True)
                                                - xhat * jnp.mean(dxh * xhat, axis=-1, keepdims=True))).astype(BF16)

        dz_ref[:, K0:GA0] = (carry[...] + dkv[0:BLOCK, :]).astype(BF16)
        carry[...] = dkv[BLOCK:2 * BLOCK, :]

        @pl.when(n == nb)
        def _():
            dsink_ref[...] = dsink_acc[:, 0:N_Q_HEADS]
            db_ref[...] = dbt_acc[...].T[0:SGU_GROUPS, :]

    cur = lambda n: (jnp.minimum(n, nb - 1), 0)
    const2 = lambda n: (0, 0)
    return pl.pallas_call(
        body, name="mixer_bwd", grid=(nb + 1,),
        out_shape=(jax.ShapeDtypeStruct((s, D_IN), BF16), jax.ShapeDtypeStruct((1, N_Q_HEADS), F32),
                   jax.ShapeDtypeStruct((1, D_SGU), F32), jax.ShapeDtypeStruct((1, D_SGU), F32),
                   jax.ShapeDtypeStruct((SGU_GROUPS, BLOCK, BLOCK), F32), jax.ShapeDtypeStruct((SGU_GROUPS, BLOCK), F32)),
        in_specs=[pl.BlockSpec((BLOCK, D_IN), cur),
                  pl.BlockSpec((BLOCK, 2 * D_KV), lambda n: (jnp.maximum(jnp.minimum(n, nb - 1) - 1, 0), K0 // (2 * D_KV))),
                  pl.BlockSpec((BLOCK, D_MODEL), cur), pl.BlockSpec((BLOCK, D_ATTN), cur),
                  pl.BlockSpec((1, N_Q_HEADS), const2), pl.BlockSpec((1, D_SGU), const2), pl.BlockSpec((1, D_SGU), const2),
                  pl.BlockSpec((SGU_GROUPS, BLOCK, BLOCK), lambda n: (0, 0, 0)), pl.BlockSpec((BLOCK, SGU_GROUPS), const2)],
        out_specs=(pl.BlockSpec((BLOCK, D_IN), lambda n: (jnp.maximum(n - 1, 0), 0)),
                   pl.BlockSpec((1, N_Q_HEADS), const2), pl.BlockSpec((1, D_SGU), const2), pl.BlockSpec((1, D_SGU), const2),
                   pl.BlockSpec((SGU_GROUPS, BLOCK, BLOCK), lambda n: (0, 0, 0)), pl.BlockSpec((SGU_GROUPS, BLOCK), const2)),
        scratch_shapes=[pltpu.VMEM((BLOCK, D_IN), BF16), pltpu.VMEM((2 * BLOCK, 2 * D_KV), F32), pltpu.VMEM((BLOCK, 2 * D_KV), F32),
                        pltpu.VMEM((1, BLOCK), F32), pltpu.VMEM((BLOCK, BLOCK), F32)],
        compiler_params=_params(("arbitrary",)),
    )(z, z, dcat, o, sinks, ln_g, ln_b, sgu_w, sgu_bt)


def _in_proj_bwd(dz, win_t, x, dx1, norm_g, mod):
    s, d = x.shape
    kdim = dz.shape[1]
    tm = min(256, s)

    def body(dz_ref, w_ref, x_ref, dx1_ref, g_ref, mod_ref, gx_ref, dshift_ref, dscale_ref, dng_ref):
        @pl.when(pl.program_id(0) == 0)
        def _():
            dshift_ref[...] = jnp.zeros_like(dshift_ref)
            dscale_ref[...] = jnp.zeros_like(dscale_ref)
            dng_ref[...] = jnp.zeros_like(dng_ref)

        dh = _dot(dz_ref[...], w_ref[...], NN)
        xv = x_ref[...]
        ng = g_ref[...]
        r = lax.rsqrt(jnp.mean(xv * xv, axis=-1, keepdims=True) + EPS)
        xn = xv * r
        dshift_ref[...] += jnp.sum(dh, axis=0, keepdims=True)
        dscale_ref[...] += jnp.sum(dh * (xn * ng), axis=0, keepdims=True)
        dhs = dh * (1.0 + mod_ref[:, d:2 * d])
        dng_ref[...] += jnp.sum(dhs * xn, axis=0, keepdims=True)
        dxn = dhs * ng
        gx_ref[...] = dx1_ref[...] + r * dxn - xn * (r * r * jnp.mean(dxn * xv, axis=-1, keepdims=True))

    row = lambda i: (i, 0)
    const = lambda i: (0, 0)
    return pl.pallas_call(
        body, name="in_proj_bwd", grid=(s // tm,),
        out_shape=(jax.ShapeDtypeStruct((s, d), F32),) + (jax.ShapeDtypeStruct((1, d), F32),) * 3,
        in_specs=[pl.BlockSpec((tm, kdim), row), pl.BlockSpec((kdim, d), const, pipeline_mode=pl.Buffered(1)),
                  pl.BlockSpec((tm, d), row), pl.BlockSpec((tm, d), row), pl.BlockSpec((1, d), const), pl.BlockSpec((1, 3 * d), const)],
        out_specs=(pl.BlockSpec((tm, d), row),) + (pl.BlockSpec((1, d), const),) * 3,
        compiler_params=_params(("arbitrary",)),
    )(dz, win_t, x, dx1, norm_g, mod)


def _sum_slots(slots, own, place, rows_tile, full_cols, name):
    _, r, hcols = slots.shape
    nblk = r // rows_tile

    def body(place_ref, s_ref, own_ref, o_ref):
        chip = place_ref[1]
        acc = jnp.zeros((rows_tile, hcols), F32)
        for k in range(N_CHIPS):
            acc = acc + jnp.where(chip == k, own_ref[...], s_ref[k]).astype(F32)
        o_ref[...] = acc

    return pl.pallas_call(
        body, name=name,
        grid_spec=pltpu.PrefetchScalarGridSpec(
            num_scalar_prefetch=1, grid=(nblk,),
            in_specs=[pl.BlockSpec((N_CHIPS, rows_tile, hcols), lambda i, p: (0, i, 0)),
                      pl.BlockSpec((rows_tile, hcols), lambda i, p: (p[1] * nblk + i, 0))],
            out_specs=pl.BlockSpec((rows_tile, hcols), lambda i, p: (i, p[0]))),
        out_shape=jax.ShapeDtypeStruct((r, full_cols), F32),
        compiler_params=_params(("parallel",)),
    )(place, slots, own)


def _adamw_rows(w, g, m, v, tile, name, after):
    r, c = w.shape

    def body(w_ref, g_ref, m_ref, v_ref, after_ref, go_ref, d_ref, mo_ref, vo_ref):
        g_ = g_ref[...]
        delta, m2, v2 = _adamw(w_ref[...], g_, m_ref[...], v_ref[...])
        go_ref[...] = g_
        d_ref[...] = delta
        mo_ref[...] = m2
        vo_ref[...] = v2

    blk = pl.BlockSpec((tile, c), lambda i: (i, 0))
    return pl.pallas_call(
        body, name=name, grid=(r // tile,), out_shape=(jax.ShapeDtypeStruct((r, c), F32),) * 4,
        in_specs=[blk] * 4 + [ANY], out_specs=(blk,) * 4,
        compiler_params=_params(("parallel",)),
    )(w, g, m, v, after)


def _adamw_ada(cact, dmod, w, m, v, tile=256):
    r, c = w.shape

    def body(c_ref, d_ref, w_ref, m_ref, v_ref, go_ref, dl_ref, mo_ref, vo_ref):
        g_ = _dot(c_ref[...].astype(BF16), d_ref[...].astype(BF16), TN)
        delta, m2, v2 = _adamw(w_ref[...], g_, m_ref[...], v_ref[...])
        go_ref[...] = g_
        dl_ref[...] = delta
        mo_ref[...] = m2
        vo_ref[...] = v2

    blk = pl.BlockSpec((tile, c), lambda i: (i, 0))
    return pl.pallas_call(
        body, name="adamw_ada", grid=(r // tile,), out_shape=(jax.ShapeDtypeStruct((r, c), F32),) * 4,
        in_specs=[pl.BlockSpec((N_DEV, tile), lambda i: (0, i)), pl.BlockSpec((N_DEV, c), lambda i: (0, 0)), blk, blk, blk],
        out_specs=(blk,) * 4,
        compiler_params=_params(("parallel",)),
    )(cact, dmod, w, m, v)


def kernel(x, c, norm_g, w_ada, b_ada, w_in, attn_sinks, sgu_ln_g, sgu_ln_b, sgu_w, sgu_b, w_out, final_g, loss_target, m_norm_g, m_w_ada, m_b_ada, m_w_in, m_attn_sinks, m_sgu_ln_g, m_sgu_ln_b, m_sgu_w, m_sgu_b, m_w_out, m_final_g, v_norm_g, v_w_ada, v_b_ada, v_w_in, v_attn_sinks, v_sgu_ln_g, v_sgu_ln_b, v_sgu_w, v_sgu_b, v_w_out, v_final_g):
    xs, tgt = x[0], loss_target[0]
    core = lax.axis_index("c").astype(jnp.int32).reshape(1)
    chip = 2 * lax.axis_index("x") + lax.axis_index("y")
    place = jnp.stack([core[0], chip.astype(jnp.int32)])
    sgu_bt = sgu_b[0].T

    w_in_t, m_in_t, v_in_t = (jnp.swapaxes(a[0], 0, 1) for a in (w_in, m_w_in, v_w_in))
    win_t = _cast_into_gathered(w_in_t, IN_SHARD // 4, place, "cast_w_in", c)
    win_t, mod, cact = _gather_w_in_and_modulate(win_t, c, w_ada[0], b_ada)
    h = _norm_modulate(xs, norm_g, mod)
    wout = _cast_into_gathered(w_out[0], OUT_SHARD // 2, place, "cast_w_out", win_t)
    sems, bufs, token = _split_start(_gather_copies, [wout], 3, "w_out_gather_start", win_t)
    z = _matmul_nt(h, win_t, 1024, 768, F32, "in_proj", token)
    cat, o = _mixer_fwd(z, attn_sinks, sgu_ln_g, sgu_ln_b, sgu_w[0], sgu_bt)
    wout = _gather_finish(_split_wait(_gather_copies, sems, bufs, cat, "w_out_gather_wait")[0], "w_out_gather_finish")
    dx1, dy, dfg, dgate, loss = _out_proj_loss(cat, wout, xs, tgt, mod, final_g.reshape(1, D_MODEL))

    so = _to_sibling(_grad_matmul(cat, dy, 1 - core, 512, "w_out_grad_sibling"), "w_out_grad_to_sibling")
    po = _grad_matmul(cat, dy, core, 512, "w_out_grad_own", add=so)
    whole_o = functools.partial(_chip_copies, [(j, 0, OUT_SHARD) for j in range(3)])
    sems, bufs, token = _split_start(whole_o, [po, lax.empty((N_CHIPS, OUT_SHARD, HALF), BF16)], 3, "w_out_grad_to_chips_start", so)
    dcat = _matmul_nt(dy, wout, 512, 1024, F32, "out_proj_bwd", token)
    dz, dsinks, dlg, dlb, dsgu_w, dsgu_b = _mixer_bwd(z, dcat, o, attn_sinks, sgu_ln_g, sgu_ln_b, sgu_w[0], sgu_bt)
    po, slots_o = _split_wait(whole_o, sems, bufs, dz, "w_out_grad_to_chips_wait")
    go = _halves_swap(_sum_slots(slots_o, po, place, OUT_SHARD // 2, D_MODEL, "w_out_grad_sum"), "w_out_grad_halves_swap")

    sw = _to_sibling(_grad_matmul(dz, h, 1 - core, 768, "w_in_grad_sibling"), "w_in_grad_to_sibling")
    pw = _grad_matmul(dz, h, core, 768, "w_in_grad_own", add=sw)
    first = functools.partial(_chip_copies, [(0, 0, IN_SHARD), (1, 0, IN_SHARD), (2, 0, IN_SHARD // 4)])
    rest = functools.partial(_chip_copies, [(2, IN_SHARD // 4, 3 * IN_SHARD // 4)])
    sems, bufs, token = _split_start(first, [pw, lax.empty((N_CHIPS, IN_SHARD, HALF), BF16)], 3, "w_in_grad_to_chips_start", sw)
    gx, dshift, dscale, dng = _in_proj_bwd(dz, win_t, xs, dx1, norm_g + token[0:1, 0:1], mod)
    bufs = _split_wait(first, sems, bufs, gx, "w_in_grad_to_chips_wait")

    parts = dict(dng=dng, dshift=dshift, dscale=dscale, dgate=dgate, dfg=dfg, dlg=dlg, dlb=dlb, dsinks=dsinks,
                 dsgu_b=dsgu_b, dsgu_w=dsgu_w, loss=loss)
    small = dict(norm_g=(norm_g, m_norm_g, v_norm_g), b_ada=(b_ada, m_b_ada, v_b_ada),
                 attn_sinks=(attn_sinks, m_attn_sinks, v_attn_sinks), sgu_ln_g=(sgu_ln_g, m_sgu_ln_g, v_sgu_ln_g),
                 sgu_ln_b=(sgu_ln_b, m_sgu_ln_b, v_sgu_ln_b), sgu_w=(sgu_w, m_sgu_w, v_sgu_w), sgu_b=(sgu_b, m_sgu_b, v_sgu_b),
                 final_g=(final_g, m_final_g, v_final_g))
    tot, gsgu_w, dmod_all = _small_grads_sum(parts, bufs[1])
    sems, bufs, token = _split_start(rest, list(bufs), 1, "w_in_grad_rest_to_chips_start", tot)
    res, total = _small_adamw(tot, gsgu_w, small)
    dmod_mine = lax.dynamic_slice_in_dim(dmod_all, ADA_SHARD * chip, ADA_SHARD, axis=1) + token[0:1, 0:1]
    res["w_ada"] = _adamw_ada(cact, dmod_mine, w_ada[0], m_w_ada[0], v_w_ada[0])
    res["w_out"] = _adamw_rows(w_out[0], go, m_w_out[0], v_w_out[0], 256, "adamw_w_out", res["w_ada"][1])

    pw, slots_w = _split_wait(rest, sems, bufs, res["w_out"][1], "w_in_grad_rest_to_chips_wait")
    gw_t = _halves_swap(_sum_slots(slots_w, pw, place, IN_SHARD // 4, D_MODEL, "w_in_grad_sum"), "w_in_grad_halves_swap")
    res["w_in"] = tuple(jnp.swapaxes(a, 0, 1) for a in _adamw_rows(w_in_t, gw_t, m_in_t, v_in_t, IN_SHARD // 4, "adamw_w_in", gw_t))

    order = ["norm_g", "w_ada", "b_ada", "w_in", "attn_sinks", "sgu_ln_g", "sgu_ln_b", "sgu_w", "sgu_b", "w_out", "final_g"]
    like = dict(norm_g=norm_g, w_ada=w_ada, b_ada=b_ada, w_in=w_in, attn_sinks=attn_sinks, sgu_ln_g=sgu_ln_g, sgu_ln_b=sgu_ln_b,
                sgu_w=sgu_w, sgu_b=sgu_b, w_out=w_out, final_g=final_g)
    outs = [total[0, 0], gx[None]]
    for kind in range(4):
        outs += [res[n][kind].reshape(like[n].shape) for n in order]
    return tuple(outs)
```

```python
import functools
import math

import jax
import jax.numpy as jnp
from jax import lax
from jax.experimental import pallas as pl
from jax.experimental.pallas import tpu as pltpu

F32 = jnp.float32
BF16 = jnp.bfloat16

D_MODEL = 2048
D_ATTN = 1024
HEAD_DIM = 64
N_Q_HEADS = 16
Q_PER_KV = 8
D_KV = 128
BLOCK = 128
D_SGU = 1024
SGU_GROUPS = 8
GROUP_DIM = 128
D_IN = 5376
EPS = 1e-6
ATTN_SCALE = 1.0 / math.sqrt(HEAD_DIM)
N_CHIPS = 4
N_DEV = 8
IN_SHARD = D_IN // N_CHIPS
OUT_SHARD = D_MODEL // N_CHIPS
ADA_SHARD = 3 * D_MODEL // N_CHIPS
HALF = D_MODEL // 2

Q0, K0, V0, GA0, U0, VS0, GS0 = 0, 1024, 1152, 1280, 2304, 3328, 4352

ADAM_LR = 0.001
ADAM_B1 = 0.9
ADAM_B2 = 0.999
ADAM_EPS = 1e-08
ADAM_WD = 0.01
ADAM_STEP = 10

VMEM_LIMIT_V7X = 56 * 1024 * 1024

NN = (((1,), (0,)), ((), ()))
NT = (((1,), (1,)), ((), ()))
TN = (((0,), (0,)), ((), ()))
MESH = pl.DeviceIdType.MESH
ANY = pl.BlockSpec(memory_space=pl.ANY)
VMEM = pl.BlockSpec(memory_space=pltpu.VMEM)
HBM = pl.BlockSpec(memory_space=pltpu.HBM)
SEM = pl.BlockSpec(memory_space=pltpu.SEMAPHORE)
EFFECT = pltpu.SideEffectType.DATAFLOW_SIDE_EFFECTING


def _dot(a, b, dims):
    return lax.dot_general(a, b, dims, preferred_element_type=F32)


def _params(sem=None):
    return pltpu.CompilerParams(dimension_semantics=sem, vmem_limit_bytes=VMEM_LIMIT_V7X)


def _sigmoid(x):
    return jax.nn.sigmoid(x)


def _adamw(w, g, m, v):
    m = ADAM_B1 * m + (1.0 - ADAM_B1) * g
    v = ADAM_B2 * v + (1.0 - ADAM_B2) * (g * g)
    m_hat = m / (1.0 - ADAM_B1 ** ADAM_STEP)
    v_hat = v / (1.0 - ADAM_B2 ** ADAM_STEP)
    delta = -ADAM_LR * (m_hat / (jnp.sqrt(v_hat) + ADAM_EPS) + ADAM_WD * w)
    return delta, m, v


def _place():
    return lax.axis_index("x"), lax.axis_index("y"), lax.axis_index("c")


def _flip(p, bit):
    return 1 - p if bit else p


def _other_chips(x, y):
    return [(_flip(x, bx), _flip(y, by)) for bx, by in ((0, 1), (1, 0), (1, 1))]


def _split_start(copies, bufs, n_sems, name, after):
    n = len(bufs)

    def body(*refs):
        send_sems, recv_sems = refs[n + 1:n + 3]
        for cp in copies(*refs[:n], send_sems, recv_sems):
            cp.start()
        refs[-1][...] = jnp.zeros_like(refs[-1])

    bufs = [pltpu.with_memory_space_constraint(a, pltpu.HBM) for a in bufs]
    outs = pl.pallas_call(
        body, name=name,
        out_shape=(pltpu.SemaphoreType.DMA((n_sems,)), pltpu.SemaphoreType.DMA((n_sems,)))
        + tuple(pltpu.HBM(a.shape, a.dtype) for a in bufs) + (jax.ShapeDtypeStruct((8, 128), F32),),
        in_specs=[HBM] * n + [ANY], out_specs=(SEM, SEM) + (HBM,) * n + (VMEM,), input_output_aliases={k: k + 2 for k in range(n)},
        compiler_params=pltpu.CompilerParams(has_side_effects=EFFECT),
    )(*bufs, after)
    return outs[:2], list(outs[2:2 + n]), outs[-1]


def _split_wait(copies, sems, bufs, after, name):
    n = len(bufs)

    def body(*refs):
        send_sems, recv_sems = refs[n:n + 2]
        for cp in copies(*refs[:n], send_sems, recv_sems):
            cp.wait_send()
            cp.wait_recv()

    return pl.pallas_call(
        body, name=name, out_shape=tuple(pltpu.HBM(a.shape, a.dtype) for a in bufs),
        in_specs=[HBM] * n + [SEM, SEM, ANY], out_specs=(HBM,) * n, input_output_aliases={k: k for k in range(n)},
        compiler_params=pltpu.CompilerParams(has_side_effects=EFFECT),
    )(*bufs, *sems, after)


def _gather_copies(full_ref, send_sems, recv_sems):
    x, y, cc = _place()
    shard = full_ref.shape[0] // N_CHIPS
    half = shard // 2
    mine = full_ref.at[pl.ds(shard * (2 * x + y) + half * cc, half), :]
    return [pltpu.make_async_remote_copy(src_ref=mine, dst_ref=mine, send_sem=send_sems.at[j], recv_sem=recv_sems.at[j],
                                         device_id=(px, py, cc), device_id_type=MESH)
            for j, (px, py) in enumerate(_other_chips(x, y))]


def _gather_w_in_and_modulate(full, c, w_ada, b_ada):
    shard = full.shape[0] // N_CHIPS
    half = shard // 2
    quarter = half // 2
    kdim = w_ada.shape[0]
    kc = 256

    def body(full_in, c_ref, w_hbm, b_ref, full_ref, mod_ref, cact_ref, w_vmem, cbuf, pbuf, rbuf, w_sem,
             send_sems, recv_sems, msend_sems, mrecv_sems):
        x, y, cc = _place()
        me = 4 * x + 2 * y + cc
        chip = 2 * x + y
        across_x, across_y, far = (1 - x, y), (x, 1 - y), (1 - x, 1 - y)

        def rows(chip_xy, core, q):
            return full_ref.at[pl.ds(shard * (2 * chip_xy[0] + chip_xy[1]) + half * core + quarter * q, quarter), :]

        def copy(k, piece, to):
            return pltpu.make_async_remote_copy(src_ref=piece, dst_ref=piece, send_sem=send_sems.at[k], recv_sem=recv_sems.at[k],
                                                device_id=to, device_id_type=MESH)

        def small(k, src, dst, to):
            return pltpu.make_async_remote_copy(src_ref=src, dst_ref=dst, send_sem=msend_sems.at[k], recv_sem=mrecv_sems.at[k],
                                                device_id=to, device_id_type=MESH)

        cbuf[me] = c_ref[...]
        gathers = [small(k - 1, cbuf.at[me], cbuf.at[me], (_flip(x, (k >> 2) & 1), _flip(y, (k >> 1) & 1), _flip(cc, k & 1)))
                   for k in range(1, N_DEV)]
        for cp in gathers:
            cp.start()
        w_load = pltpu.make_async_copy(w_hbm, w_vmem, w_sem)
        w_load.start()
        direct = [copy(0, rows((x, y), cc, 0), (*across_x, cc)), copy(2, rows((x, y), cc, 1), (*across_y, cc)),
                  copy(1, rows((x, y), cc, 1), (*across_x, cc)), copy(3, rows((x, y), cc, 0), (*across_y, cc))]
        for cp in direct:
            cp.start()

        for cp in gathers:
            cp.wait()
        w_load.wait()
        for b in range(N_DEV):
            cv = cbuf[b]
            cact_ref[b:b + 1, :] = cv * _sigmoid(cv)
        acc = jnp.zeros((N_DEV, ADA_SHARD), F32)
        for k0 in range(0, kdim, kc):
            acc = acc + _dot(cact_ref[:, k0:k0 + kc].astype(BF16), w_vmem[k0:k0 + kc, :].astype(BF16), NN)
        for b in range(N_DEV):
            pbuf[b] = acc[b:b + 1, :]
        rbuf[chip] = pbuf[me]
        sends = [small(N_DEV - 1 + j, pbuf.at[4 * px + 2 * py + cc], rbuf.at[chip], (px, py, cc))
                 for j, (px, py) in enumerate(_other_chips(x, y))]
        for cp in sends:
            cp.start()

        landing = [(0, across_x, 0, (4, (*across_y, cc))), (2, across_y, 1, (5, (*across_x, cc))),
                   (1, across_x, 1, None), (3, across_y, 0, None), (4, far, 0, None), (5, far, 1, None)]
        passed = []
        for n, (k, whose, q, onward) in enumerate(landing):
            piece = rows(whose, cc, q)
            copy(k, piece, (x, y, cc)).wait_recv()
            if onward is not None:
                passed.append(copy(onward[0], piece, onward[1]))
                passed[-1].start()
            passed.append(copy(6 + n, piece, (x, y, 1 - cc)))
            passed[-1].start()
        for n, (k, whose, q, onward) in enumerate(landing):
            copy(6 + n, rows(whose, 1 - cc, q), (x, y, cc)).wait_recv()
        for cp in sends:
            cp.wait()
        for k in range(N_CHIPS):
            cols = slice(ADA_SHARD * k, ADA_SHARD * (k + 1))
            mod_ref[:, cols] = rbuf[k] + b_ref[:, cols]
        for cp in direct + passed:
            cp.wait_send()

    return pl.pallas_call(
        body, name="w_in_gather_and_modulate",
        out_shape=(jax.ShapeDtypeStruct(full.shape, full.dtype), jax.ShapeDtypeStruct((1, 3 * D_MODEL), F32),
                   jax.ShapeDtypeStruct((N_DEV, D_MODEL), F32)),
        in_specs=[ANY, VMEM, ANY, VMEM], out_specs=(ANY, VMEM, VMEM), input_output_aliases={0: 0},
        scratch_shapes=[pltpu.VMEM(w_ada.shape, F32), pltpu.VMEM((N_DEV, 1, D_MODEL), F32), pltpu.VMEM((N_DEV, 1, ADA_SHARD), F32),
                        pltpu.VMEM((N_CHIPS, 1, ADA_SHARD), F32), pltpu.SemaphoreType.DMA,
                        pltpu.SemaphoreType.DMA((12,)), pltpu.SemaphoreType.DMA((12,)),
                        pltpu.SemaphoreType.DMA((N_DEV + 2,)), pltpu.SemaphoreType.DMA((N_DEV + 2,))],
        compiler_params=_params(),
    )(full, c, w_ada, b_ada)


def _gather_finish(full, name):
    shard = full.shape[0] // N_CHIPS
    half = shard // 2

    def body(full_in, full_ref, send_sems, recv_sems):
        x, y, cc = _place()
        cps = []
        for j, (px, py) in enumerate(_other_chips(x, y)):
            landed = full_ref.at[pl.ds(shard * (2 * px + py) + half * cc, half), :]
            cps.append(pltpu.make_async_remote_copy(src_ref=landed, dst_ref=landed, send_sem=send_sems.at[j], recv_sem=recv_sems.at[j],
                                                    device_id=(x, y, 1 - cc), device_id_type=MESH))
        for cp in cps:
            cp.start()
        for cp in cps:
            cp.wait()

    return pl.pallas_call(
        body, name=name, out_shape=jax.ShapeDtypeStruct(full.shape, full.dtype),
        in_specs=[ANY], out_specs=ANY, input_output_aliases={0: 0},
        scratch_shapes=[pltpu.SemaphoreType.DMA((3,)), pltpu.SemaphoreType.DMA((3,))],
        compiler_params=_params(),
    )(full)


def _to_sibling(g, name):
    def body(g_ref, r_ref, send_sem, recv_sem):
        x, y, cc = _place()
        cp = pltpu.make_async_remote_copy(src_ref=g_ref, dst_ref=r_ref, send_sem=send_sem, recv_sem=recv_sem,
                                          device_id=(x, y, 1 - cc), device_id_type=MESH)
        cp.start()
        cp.wait()

    return pl.pallas_call(
        body, name=name, out_shape=jax.ShapeDtypeStruct(g.shape, g.dtype), in_specs=[ANY], out_specs=ANY,
        scratch_shapes=[pltpu.SemaphoreType.DMA, pltpu.SemaphoreType.DMA], compiler_params=_params(),
    )(g)


def _chip_copies(pieces, p_ref, land_ref, send_sems, recv_sems):
    x, y, cc = _place()
    chip = 2 * x + y
    shard = land_ref.shape[1]
    others = _other_chips(x, y)
    cps = []
    for k, (j, r0, nr) in enumerate(pieces):
        px, py = others[j]
        cps.append(pltpu.make_async_remote_copy(
            src_ref=p_ref.at[pl.ds(shard * (2 * px + py) + r0, nr), :], dst_ref=land_ref.at[chip, pl.ds(r0, nr), :],
            send_sem=send_sems.at[k], recv_sem=recv_sems.at[k], device_id=(px, py, cc), device_id_type=MESH))
    return cps


def _owner_copies(p_ref, land_ref, send_sems, recv_sems):
    x, y, cc = _place()
    me = 4 * x + 2 * y + cc
    shard = land_ref.shape[1]
    cps = []
    for k in range(1, N_DEV):
        px, py, pc = _flip(x, (k >> 2) & 1), _flip(y, (k >> 1) & 1), _flip(cc, k & 1)
        block = p_ref.at[pl.ds(shard * (2 * px + py), shard), pl.ds(pl.multiple_of(HALF * pc, 128), HALF)]
        cps.append(pltpu.make_async_remote_copy(src_ref=block, dst_ref=land_ref.at[me], send_sem=send_sems.at[k - 1],
                                                recv_sem=recv_sems.at[k - 1], device_id=(px, py, pc), device_id_type=MESH))
    return cps


def _halves_swap(f, name):
    def body(f_in, f_ref, send_sem, recv_sem):
        x, y, cc = _place()
        mine = f_ref.at[:, pl.ds(pl.multiple_of(HALF * cc, 128), HALF)]
        cp = pltpu.make_async_remote_copy(src_ref=mine, dst_ref=mine, send_sem=send_sem, recv_sem=recv_sem,
                                          device_id=(x, y, 1 - cc), device_id_type=MESH)
        cp.start()
        cp.wait()

    return pl.pallas_call(
        body, name=name, out_shape=jax.ShapeDtypeStruct(f.shape, f.dtype),
        in_specs=[ANY], out_specs=ANY, input_output_aliases={0: 0},
        scratch_shapes=[pltpu.SemaphoreType.DMA, pltpu.SemaphoreType.DMA], compiler_params=_params(),
    )(f)


ROW_NORM_G, ROW_DMOD, ROW_FINAL_G, ROW_LN, ROW_SINKS, ROW_LOSS, ROW_SGU_B, MISC_ROWS = 0, 1, 4, 5, 6, 7, 8, 16


def _small_grads_sum(parts, after):
    part_names = ["dng", "dshift", "dscale", "dgate", "dfg", "dlg", "dlb", "dsinks", "dsgu_b", "dsgu_w", "loss"]
    n_parts = len(part_names)
    n_peers = N_DEV - 1

    def body(*refs):
        p = dict(zip(part_names, refs[:n_parts]))
        tot_ref, gw_ref, dmod_ref = refs[n_parts + 1:n_parts + 4]
        misc, cmisc, dsend, dbuf, wland, wsum, own_sem, send_sems, recv_sems = refs[n_parts + 4:]
        x, y, cc = _place()
        me = 4 * x + 2 * y + cc
        chip = 2 * x + y
        peers = [(_flip(x, (k >> 2) & 1), _flip(y, (k >> 1) & 1), _flip(cc, k & 1)) for k in range(1, N_DEV)]

        def remote(k, src, dst, peer):
            return pltpu.make_async_remote_copy(src_ref=src, dst_ref=dst, send_sem=send_sems.at[k], recv_sem=recv_sems.at[k],
                                                device_id=peer, device_id_type=MESH)

        dsend[:, 0:D_MODEL] = p["dshift"][...]
        dsend[:, D_MODEL:2 * D_MODEL] = p["dscale"][...]
        dsend[:, 2 * D_MODEL:3 * D_MODEL] = p["dgate"][...]

        def shard_cols(chip_index):
            return dsend.at[:, pl.ds(pl.multiple_of(ADA_SHARD * chip_index, 128), ADA_SHARD)]

        own_cols = pltpu.make_async_copy(shard_cols(chip), dbuf.at[me], own_sem)
        own_cols.start()
        gathers = [remote(k, shard_cols(2 * px + py), dbuf.at[me], (px, py, pc)) for k, (px, py, pc) in enumerate(peers)]
        wland[me] = p["dsgu_w"][me]
        scatters = [remote(n_peers + k, p["dsgu_w"].at[4 * px + 2 * py + pc], wland.at[me], (px, py, pc))
                    for k, (px, py, pc) in enumerate(peers)]

        misc[cc] = jnp.zeros((MISC_ROWS, D_MODEL), F32)
        misc[cc, ROW_NORM_G:ROW_NORM_G + 1, :] = p["dng"][...]
        misc[cc, ROW_DMOD:ROW_DMOD + 1, :] = p["dshift"][...]
        misc[cc, ROW_DMOD + 1:ROW_DMOD + 2, :] = p["dscale"][...]
        misc[cc, ROW_DMOD + 2:ROW_DMOD + 3, :] = p["dgate"][...]
        misc[cc, ROW_FINAL_G:ROW_FINAL_G + 1, :] = p["dfg"][...]
        misc[cc, ROW_LN:ROW_LN + 1, 0:D_SGU] = p["dlg"][...]
        misc[cc, ROW_LN:ROW_LN + 1, D_SGU:2 * D_SGU] = p["dlb"][...]
        misc[cc, ROW_SINKS:ROW_SINKS + 1, 0:N_Q_HEADS] = p["dsinks"][...]
        misc[cc, ROW_LOSS:ROW_LOSS + 1, 0:1] = p["loss"][...]
        misc[cc, ROW_SGU_B:ROW_SGU_B + SGU_GROUPS, 0:BLOCK] = p["dsgu_b"][...]
        pair = remote(3 * n_peers, misc.at[cc], misc.at[cc], (x, y, 1 - cc))
        for cp in gathers + scatters + [pair]:
            cp.start()
        pair.wait()
        cmisc[chip] = misc[0] + misc[1]
        quad = [remote(3 * n_peers + 1 + j, cmisc.at[chip], cmisc.at[chip], (_flip(x, bx), _flip(y, by), cc))
                for j, (bx, by) in enumerate(((0, 1), (1, 0), (1, 1)))]
        for cp in quad:
            cp.start()
        for cp in scatters:
            cp.wait()
        group = wland[0]
        for b in range(1, N_DEV):
            group = group + wland[b]
        wsum[me] = group
        spreads = [remote(2 * n_peers + k, wsum.at[me], wsum.at[me], peer) for k, peer in enumerate(peers)]
        for cp in spreads:
            cp.start()
        for cp in gathers + quad + spreads:
            cp.wait()

        tot = cmisc[0]
        for k in range(1, N_CHIPS):
            tot = tot + cmisc[k]
        tot_ref[...] = tot
        gw_ref[...] = wsum[...]
        own_cols.wait()
        for b in range(N_DEV):
            dmod_ref[b:b + 1, :] = dbuf[b]

    operands = [parts[n] for n in part_names]
    n_sems = 3 * n_peers + 1 + (N_CHIPS - 1)
    return pl.pallas_call(
        body, name="small_grads_sum",
        out_shape=(jax.ShapeDtypeStruct((MISC_ROWS, D_MODEL), F32), jax.ShapeDtypeStruct((SGU_GROUPS, BLOCK, BLOCK), F32),
                   jax.ShapeDtypeStruct((N_DEV, ADA_SHARD), F32)),
        in_specs=[VMEM] * len(operands) + [ANY], out_specs=(VMEM, VMEM, VMEM),
        scratch_shapes=[pltpu.VMEM((2, MISC_ROWS, D_MODEL), F32), pltpu.VMEM((N_CHIPS, MISC_ROWS, D_MODEL), F32),
                        pltpu.VMEM((1, 3 * D_MODEL), F32), pltpu.VMEM((N_DEV, 1, ADA_SHARD), F32),
                        pltpu.VMEM((N_DEV, BLOCK, BLOCK), F32), pltpu.VMEM((SGU_GROUPS, BLOCK, BLOCK), F32),
                        pltpu.SemaphoreType.DMA, pltpu.SemaphoreType.DMA((n_sems,)), pltpu.SemaphoreType.DMA((n_sems,))],
        compiler_params=_params(),
    )(*operands, after)


def _small_adamw(tot, gw, small):
    names = ["norm_g", "b_ada", "attn_sinks", "sgu_ln_g", "sgu_ln_b", "sgu_w", "sgu_b", "final_g"]

    def body(*refs):
        tot_ref, gw_ref = refs[:2]
        wmv = {n: refs[2 + 3 * i: 5 + 3 * i] for i, n in enumerate(names)}
        outs = refs[2 + 3 * len(names):]
        res = {n: outs[4 * i: 4 * i + 4] for i, n in enumerate(names)}
        outs[4 * len(names)][...] = tot_ref[ROW_LOSS:ROW_LOSS + 1, 0:1]

        def update(name, g, pick=lambda r: r[...], put=None):
            w_ref, m_ref, v_ref = wmv[name]
            delta, m2, v2 = _adamw(pick(w_ref), g, pick(m_ref), pick(v_ref))
            for ref, val in zip(res[name], (g, delta, m2, v2)):
                if put is None:
                    ref[...] = val
                else:
                    put(ref, val)

        update("norm_g", tot_ref[ROW_NORM_G:ROW_NORM_G + 1, :])
        update("final_g", tot_ref[ROW_FINAL_G:ROW_FINAL_G + 1, :])
        for t in range(3):
            cols = slice(D_MODEL * t, D_MODEL * (t + 1))

            def put_cols(ref, val, cols=cols):
                ref[:, cols] = val
            update("b_ada", tot_ref[ROW_DMOD + t:ROW_DMOD + t + 1, :], pick=lambda r, cols=cols: r[:, cols], put=put_cols)
        update("sgu_ln_g", tot_ref[ROW_LN:ROW_LN + 1, 0:D_SGU])
        update("sgu_ln_b", tot_ref[ROW_LN:ROW_LN + 1, D_SGU:2 * D_SGU])
        update("attn_sinks", tot_ref[ROW_SINKS:ROW_SINKS + 1, 0:N_Q_HEADS])
        update("sgu_b", tot_ref[ROW_SGU_B:ROW_SGU_B + SGU_GROUPS, 0:BLOCK])
        for g in range(SGU_GROUPS):
            def put_group(ref, val, g=g):
                ref[g] = val
            update("sgu_w", gw_ref[g], pick=lambda r, g=g: r[g], put=put_group)

    shapes = {"norm_g": (1, D_MODEL), "b_ada": (1, 3 * D_MODEL), "attn_sinks": (1, N_Q_HEADS), "sgu_ln_g": (1, D_SGU),
              "sgu_ln_b": (1, D_SGU), "sgu_w": (SGU_GROUPS, BLOCK, BLOCK), "sgu_b": (SGU_GROUPS, BLOCK), "final_g": (1, D_MODEL)}
    operands = [tot, gw]
    for n in names:
        operands += [a.reshape(shapes[n]) for a in small[n]]
    out_shape = []
    for n in names:
        out_shape += [jax.ShapeDtypeStruct(shapes[n], F32)] * 4
    out_shape.append(jax.ShapeDtypeStruct((1, 1), F32))
    outs = pl.pallas_call(
        body, name="small_adamw", out_shape=tuple(out_shape),
        in_specs=[VMEM] * len(operands), out_specs=tuple([VMEM] * len(out_shape)), compiler_params=_params(),
    )(*operands)
    return {n: outs[4 * i: 4 * i + 4] for i, n in enumerate(names)}, outs[4 * len(names)]


def _cast_into_gathered(w, tile, place, name, after):
    r, c = w.shape
    nblk = r // tile

    def body(place_ref, w_ref, after_ref, o_ref):
        o_ref[...] = w_ref[...].astype(BF16)

    return pl.pallas_call(
        body, name=name,
        grid_spec=pltpu.PrefetchScalarGridSpec(
            num_scalar_prefetch=1, grid=(nblk,), in_specs=[pl.BlockSpec((tile, c), lambda i, p: (i, 0)), ANY],
            out_specs=pl.BlockSpec((tile, c), lambda i, p: (p[1] * nblk + i, 0))),
        out_shape=jax.ShapeDtypeStruct((N_CHIPS * r, c), BF16),
        compiler_params=_params(("parallel",)),
    )(place, w, after)


def _norm_modulate(x, norm_g, mod):
    s, d = x.shape
    tm = min(256, s)

    def body(x_ref, g_ref, mod_ref, h_ref):
        xv = x_ref[...]
        r = lax.rsqrt(jnp.mean(xv * xv, axis=-1, keepdims=True) + EPS)
        h = (xv * r * g_ref[...]) * (1.0 + mod_ref[:, d:2 * d]) + mod_ref[:, 0:d]
        h_ref[...] = h.astype(BF16)

    return pl.pallas_call(
        body, name="norm_modulate", grid=(s // tm,), out_shape=jax.ShapeDtypeStruct((s, d), BF16),
        in_specs=[pl.BlockSpec((tm, d), lambda i: (i, 0)), pl.BlockSpec((1, d), lambda i: (0, 0)),
                  pl.BlockSpec((1, 3 * d), lambda i: (0, 0))],
        out_specs=pl.BlockSpec((tm, d), lambda i: (i, 0)),
        compiler_params=_params(("parallel",)),
    )(x, norm_g, mod)


def _matmul_nt(a, bt, tm, tn, out_dtype, name, after):
    m, k = a.shape
    n = bt.shape[0]
    tm, tn = min(tm, m), min(tn, n)

    def body(a_ref, b_ref, after_ref, o_ref):
        o_ref[...] = _dot(a_ref[...], b_ref[...], NT).astype(out_dtype)

    return pl.pallas_call(
        body, name=name, grid=(m // tm, n // tn), out_shape=jax.ShapeDtypeStruct((m, n), out_dtype),
        in_specs=[pl.BlockSpec((tm, k), lambda i, j: (i, 0)), pl.BlockSpec((tn, k), lambda i, j: (j, 0)), ANY],
        out_specs=pl.BlockSpec((tm, tn), lambda i, j: (i, j)),
        compiler_params=_params(("parallel", "parallel")),
    )(a, bt, after)


def _matmul_tn(a, b, tm, name):
    t, m = a.shape
    n = b.shape[1]
    tm = min(tm, m)

    def body(a_ref, b_ref, o_ref):
        o_ref[...] = _dot(a_ref[...], b_ref[...], TN).astype(BF16)

    return pl.pallas_call(
        body, name=name, grid=(m // tm,), out_shape=jax.ShapeDtypeStruct((m, n), BF16),
        in_specs=[pl.BlockSpec((t, tm), lambda i: (0, i)), pl.BlockSpec((t, n), lambda i: (0, 0))],
        out_specs=pl.BlockSpec((tm, n), lambda i: (i, 0)),
        compiler_params=_params(("parallel",)),
    )(a, b)


def _grad_matmul(a, b, half, tm, name, add=None):
    t, m = a.shape
    tm = min(tm, m)

    def body(half_ref, a_ref, b_ref, *rest):
        if add is None:
            (o_ref,) = rest
            o_ref[...] = _dot(a_ref[...], b_ref[...], TN).astype(BF16)
        else:
            add_ref, o_ref = rest
            o_ref[...] = (_dot(a_ref[...], b_ref[...], TN) + add_ref[...].astype(F32)).astype(BF16)

    in_specs = [pl.BlockSpec((t, tm), lambda i, h: (0, i)), pl.BlockSpec((t, HALF), lambda i, h: (0, h[0]))]
    operands = [a, b]
    if add is not None:
        in_specs.append(pl.BlockSpec((tm, HALF), lambda i, h: (i, 0)))
        operands.append(add)
    return pl.pallas_call(
        body, name=name,
        grid_spec=pltpu.PrefetchScalarGridSpec(num_scalar_prefetch=1, grid=(m // tm,), in_specs=in_specs,
                                               out_specs=pl.BlockSpec((tm, HALF), lambda i, h: (i, 0))),
        out_shape=jax.ShapeDtypeStruct((m, HALF), BF16),
        compiler_params=_params(("parallel",)),
    )(half, *operands)


def _mixer_masks(n):
    key = lax.broadcasted_iota(jnp.int32, (2 * BLOCK, Q_PER_KV * BLOCK), 0)
    qry = lax.broadcasted_iota(jnp.int32, (2 * BLOCK, Q_PER_KV * BLOCK), 1) & (BLOCK - 1)
    valid = (key > qry) & (key <= qry + BLOCK) & ((key >= BLOCK) | (n > 0))
    r2 = lax.broadcasted_iota(jnp.int32, (BLOCK, BLOCK), 0)
    c2 = lax.broadcasted_iota(jnp.int32, (BLOCK, BLOCK), 1)
    return valid, r2 >= c2


def _layer_norm_parts(vs):
    mu = jnp.mean(vs, axis=-1, keepdims=True)
    xc = vs - mu
    rstd = lax.rsqrt(jnp.mean(xc * xc, axis=-1, keepdims=True) + EPS)
    return xc * rstd, rstd


def _stack_heads(ref, base, hk):
    first = base + HEAD_DIM * Q_PER_KV * hk
    return jnp.concatenate([ref[:, first + HEAD_DIM * g:first + HEAD_DIM * (g + 1)] for g in range(Q_PER_KV)], axis=0)


def _head_cols(base, h):
    return slice(base + HEAD_DIM * h, base + HEAD_DIM * (h + 1))


def _sink_row(sink_ref, hk):
    return jnp.concatenate([jnp.broadcast_to(sink_ref[:, h:h + 1], (1, BLOCK))
                            for h in range(Q_PER_KV * hk, Q_PER_KV * (hk + 1))], axis=1)


def _attn_probs(qs, kh, valid, sk):
    s = _dot(kh, qs, NT) * ATTN_SCALE
    s = jnp.where(valid, s, -jnp.inf)
    m = jnp.maximum(jnp.max(s, axis=0, keepdims=True), sk)
    e = jnp.exp(s - m)
    es = jnp.exp(sk - m)
    inv = 1.0 / (jnp.sum(e, axis=0, keepdims=True) + es)
    return e * inv, es * inv


def _kv_ext(z_ref, kvp_ref):
    k_ext = jnp.concatenate([kvp_ref[:, 0:D_KV], z_ref[:, K0:K0 + D_KV]], axis=0).astype(BF16)
    v_ext = jnp.concatenate([kvp_ref[:, D_KV:2 * D_KV], z_ref[:, V0:V0 + D_KV]], axis=0).astype(BF16)
    return k_ext, v_ext


def _sgu_mixed(w_ref, bt_ref, vn, tril):
    ws = [jnp.where(tril, w_ref[g], 0.0).astype(BF16) for g in range(SGU_GROUPS)]
    mixed = jnp.concatenate([_dot(ws[g], vn[:, GROUP_DIM * g:GROUP_DIM * (g + 1)], NN) + bt_ref[:, g:g + 1]
                             for g in range(SGU_GROUPS)], axis=1)
    return ws, mixed


def _mixer_fwd(z, sinks, ln_g, ln_b, sgu_w, sgu_bt):
    s = z.shape[0]
    nb = s // BLOCK

    def body(z_ref, kvp_ref, sink_ref, lg_ref, lb_ref, w_ref, bt_ref, cat_ref, o_ref):
        n = pl.program_id(0)
        valid, tril = _mixer_masks(n)
        k_ext, v_ext = _kv_ext(z_ref, kvp_ref)
        for hk in range(2):
            kv = slice(HEAD_DIM * hk, HEAD_DIM * (hk + 1))
            p, _ = _attn_probs(_stack_heads(z_ref, Q0, hk).astype(BF16), k_ext[:, kv], valid, _sink_row(sink_ref, hk))
            o_t = _dot(v_ext[:, kv], p.astype(BF16), TN)
            for g in range(Q_PER_KV):
                h = Q_PER_KV * hk + g
                o = o_t[:, BLOCK * g:BLOCK * (g + 1)].T
                ga = z_ref[:, _head_cols(GA0, h)]
                o_ref[:, _head_cols(0, h)] = o
                cat_ref[:, _head_cols(0, h)] = (o * (ga * _sigmoid(ga))).astype(BF16)
        xhat, _ = _layer_norm_parts(z_ref[:, VS0:VS0 + D_SGU])
        vn = (xhat * lg_ref[...] + lb_ref[...]).astype(BF16)
        _, mixed = _sgu_mixed(w_ref, bt_ref, vn, tril)
        gt = z_ref[:, GS0:GS0 + D_SGU]
        cat_ref[:, D_ATTN:D_MODEL] = ((z_ref[:, U0:U0 + D_SGU] * mixed) * (gt * _sigmoid(gt))).astype(BF16)

    const2 = lambda n: (0, 0)
    return pl.pallas_call(
        body, name="mixer_fwd", grid=(nb,),
        out_shape=(jax.ShapeDtypeStruct((s, D_MODEL), BF16), jax.ShapeDtypeStruct((s, D_ATTN), F32)),
        in_specs=[pl.BlockSpec((BLOCK, D_IN), lambda n: (n, 0)),
                  pl.BlockSpec((BLOCK, 2 * D_KV), lambda n: (jnp.maximum(n - 1, 0), K0 // (2 * D_KV))),
                  pl.BlockSpec((1, N_Q_HEADS), const2), pl.BlockSpec((1, D_SGU), const2), pl.BlockSpec((1, D_SGU), const2),
                  pl.BlockSpec((SGU_GROUPS, BLOCK, BLOCK), lambda n: (0, 0, 0)), pl.BlockSpec((BLOCK, SGU_GROUPS), const2)],
        out_specs=(pl.BlockSpec((BLOCK, D_MODEL), lambda n: (n, 0)), pl.BlockSpec((BLOCK, D_ATTN), lambda n: (n, 0))),
        compiler_params=_params(("parallel",)),
    )(z, z, sinks, ln_g, ln_b, sgu_w, sgu_bt)


def _out_proj_loss(cat, wout, x, target, mod, final_g):
    s, d = x.shape
    tm = min(256, s)

    def body(cat_ref, w_ref, x_ref, t_ref, mod_ref, fg_ref, dx1_ref, dy_ref, dfg_ref, dgate_ref, loss_ref):
        @pl.when(pl.program_id(0) == 0)
        def _():
            dfg_ref[...] = jnp.zeros_like(dfg_ref)
            dgate_ref[...] = jnp.zeros_like(dgate_ref)
            loss_ref[...] = jnp.zeros_like(loss_ref)

        gate = mod_ref[:, 2 * d:3 * d]
        fg = fg_ref[...]
        yv = _dot(cat_ref[...], w_ref[...], NN)
        x1 = x_ref[...] + gate * yv
        r = lax.rsqrt(jnp.mean(x1 * x1, axis=-1, keepdims=True) + EPS)
        xr = x1 * r
        err = xr * fg - t_ref[...]
        dout = err * (1.0 / d)
        gy = dout * fg
        dx1 = r * gy - xr * (r * r * jnp.mean(gy * x1, axis=-1, keepdims=True))
        dx1_ref[...] = dx1
        dy_ref[...] = (gate * dx1).astype(BF16)
        dfg_ref[...] += jnp.sum(dout * xr, axis=0, keepdims=True)
        dgate_ref[...] += jnp.sum(dx1 * yv, axis=0, keepdims=True)
        loss_ref[...] += 0.5 * jnp.sum(jnp.mean(err * err, axis=-1, keepdims=True), axis=0, keepdims=True)

    row = lambda i: (i, 0)
    const = lambda i: (0, 0)
    return pl.pallas_call(
        body, name="out_proj_loss", grid=(s // tm,),
        out_shape=(jax.ShapeDtypeStruct((s, d), F32), jax.ShapeDtypeStruct((s, d), BF16), jax.ShapeDtypeStruct((1, d), F32),
                   jax.ShapeDtypeStruct((1, d), F32), jax.ShapeDtypeStruct((1, 1), F32)),
        in_specs=[pl.BlockSpec((tm, d), row), pl.BlockSpec((d, d), const), pl.BlockSpec((tm, d), row), pl.BlockSpec((tm, d), row),
                  pl.BlockSpec((1, 3 * d), const), pl.BlockSpec((1, d), const)],
        out_specs=(pl.BlockSpec((tm, d), row), pl.BlockSpec((tm, d), row), pl.BlockSpec((1, d), const), pl.BlockSpec((1, d), const),
                   pl.BlockSpec((1, 1), const)),
        compiler_params=_params(("arbitrary",)),
    )(cat, wout, x, target, mod, final_g)


def _mixer_bwd(z, dcat, o, sinks, ln_g, ln_b, sgu_w, sgu_bt):
    s = z.shape[0]
    nb = s // BLOCK

    def body(z_ref, kvp_ref, dc_ref, o_ref, sink_ref, lg_ref, lb_ref, w_ref, bt_ref,
             dz_ref, dsink_ref, dlg_ref, dlb_ref, dw_ref, db_ref, pend, dkv, carry, dsink_acc, dbt_acc):
        n = pl.program_id(0)

        @pl.when(n == 0)
        def _():
            pend[...] = jnp.zeros_like(pend)
            carry[...] = jnp.zeros_like(carry)
            dsink_acc[...] = jnp.zeros_like(dsink_acc)
            dbt_acc[...] = jnp.zeros_like(dbt_acc)
            dlg_ref[...] = jnp.zeros_like(dlg_ref)
            dlb_ref[...] = jnp.zeros_like(dlb_ref)
            dw_ref[...] = jnp.zeros_like(dw_ref)

        dz_ref[:, 0:K0] = pend[:, 0:K0]
        dz_ref[:, GA0:D_IN] = pend[:, GA0:D_IN]

        @pl.when(n == nb)
        def _():
            dkv[...] = jnp.zeros_like(dkv)

        @pl.when(n < nb)
        def _():
            valid, tril = _mixer_masks(n)
            k_ext, v_ext = _kv_ext(z_ref, kvp_ref)
            for hk in range(2):
                kv = slice(HEAD_DIM * hk, HEAD_DIM * (hk + 1))
                kh, vh = k_ext[:, kv], v_ext[:, kv]
                qs = _stack_heads(z_ref, Q0, hk).astype(BF16)
                p, p_sink = _attn_probs(qs, kh, valid, _sink_row(sink_ref, hk))
                dos = []
                for g in range(Q_PER_KV):
                    h = Q_PER_KV * hk + g
                    ga = z_ref[:, _head_cols(GA0, h)]
                    sg = _sigmoid(ga)
                    datt = dc_ref[:, _head_cols(0, h)]
                    dos.append(datt * (ga * sg))
                    pend[:, _head_cols(GA0, h)] = (datt * o_ref[:, _head_cols(0, h)] * (sg * (1.0 + ga * (1.0 - sg)))).astype(BF16)
                do = jnp.concatenate(dos, axis=0).astype(BF16)
                dp = _dot(vh, do, NT)
                delta = jnp.sum(p * dp, axis=0, keepdims=True)
                ds = (p * (dp - delta) * ATTN_SCALE).astype(BF16)
                dsk = -p_sink * delta
                dq_t = _dot(kh, ds, TN)
                for g in range(Q_PER_KV):
                    h = Q_PER_KV * hk + g
                    dsink_acc[:, h:h + 1] += jnp.sum(dsk[:, BLOCK * g:BLOCK * (g + 1)], axis=1, keepdims=True)
                    pend[:, _head_cols(Q0, h)] = dq_t[:, BLOCK * g:BLOCK * (g + 1)].T.astype(BF16)
                dkv[:, kv] = _dot(ds, qs, NN)
                dkv[:, D_KV + HEAD_DIM * hk:D_KV + HEAD_DIM * (hk + 1)] = _dot(p.astype(BF16), do, NN)

            xhat, rstd = _layer_norm_parts(z_ref[:, VS0:VS0 + D_SGU])
            lg = lg_ref[...]
            vn = (xhat * lg + lb_ref[...]).astype(BF16)
            ws, mixed = _sgu_mixed(w_ref, bt_ref, vn, tril)
            u = z_ref[:, U0:U0 + D_SGU]
            gt = z_ref[:, GS0:GS0 + D_SGU]
            sg = _sigmoid(gt)
            dsgu = dc_ref[:, D_ATTN:D_MODEL]
            dgated = dsgu * (gt * sg)
            pend[:, U0:U0 + D_SGU] = (dgated * mixed).astype(BF16)
            pend[:, GS0:GS0 + D_SGU] = (dsgu * (u * mixed) * (sg * (1.0 + gt * (1.0 - sg)))).astype(BF16)
            dmixed = dgated * u
            dmb = dmixed.astype(BF16)
            dvn_parts = []
            for g in range(SGU_GROUPS):
                gs_ = slice(GROUP_DIM * g, GROUP_DIM * (g + 1))
                dw_ref[g] += jnp.where(tril, _dot(dmb[:, gs_], vn[:, gs_], NT), 0.0)
                dbt_acc[:, g:g + 1] += jnp.sum(dmixed[:, gs_], axis=-1, keepdims=True)
                dvn_parts.append(_dot(ws[g], dmb[:, gs_], TN))
            dvn = jnp.concatenate(dvn_parts, axis=1)
            dlg_ref[...] += jnp.sum(dvn * xhat, axis=0, keepdims=True)
            dlb_ref[...] += jnp.sum(dvn, axis=0, keepdims=True)
            dxh = dvn * lg
            pend[:, VS0:VS0 + D_SGU] = (rstd * (dxh - jnp.mean(dxh, axis=-1, keepdims=True)
                                                - xhat * jnp.mean(dxh * xhat, axis=-1, keepdims=True))).astype(BF16)

        dz_ref[:, K0:GA0] = (carry[...] + dkv[0:BLOCK, :]).astype(BF16)
        carry[...] = dkv[BLOCK:2 * BLOCK, :]

        @pl.when(n == nb)
        def _():
            dsink_ref[...] = dsink_acc[:, 0:N_Q_HEADS]
            db_ref[...] = dbt_acc[...].T[0:SGU_GROUPS, :]

    cur = lambda n: (jnp.minimum(n, nb - 1), 0)
    const2 = lambda n: (0, 0)
    return pl.pallas_call(
        body, name="mixer_bwd", grid=(nb + 1,),
        out_shape=(jax.ShapeDtypeStruct((s, D_IN), BF16), jax.ShapeDtypeStruct((1, N_Q_HEADS), F32),
                   jax.ShapeDtypeStruct((1, D_SGU), F32), jax.ShapeDtypeStruct((1, D_SGU), F32),
                   jax.ShapeDtypeStruct((SGU_GROUPS, BLOCK, BLOCK), F32), jax.ShapeDtypeStruct((SGU_GROUPS, BLOCK), F32)),
        in_specs=[pl.BlockSpec((BLOCK, D_IN), cur),
                  pl.BlockSpec((BLOCK, 2 * D_KV), lambda n: (jnp.maximum(jnp.minimum(n, nb - 1) - 1, 0), K0 // (2 * D_KV))),
                  pl.BlockSpec((BLOCK, D_MODEL), cur), pl.BlockSpec((BLOCK, D_ATTN), cur),
                  pl.BlockSpec((1, N_Q_HEADS), const2), pl.BlockSpec((1, D_SGU), const2), pl.BlockSpec((1, D_SGU), const2),
                  pl.BlockSpec((SGU_GROUPS, BLOCK, BLOCK), lambda n: (0, 0, 0)), pl.BlockSpec((BLOCK, SGU_GROUPS), const2)],
        out_specs=(pl.BlockSpec((BLOCK, D_IN), lambda n: (jnp.maximum(n - 1, 0), 0)),
                   pl.BlockSpec((1, N_Q_HEADS), const2), pl.BlockSpec((1, D_SGU), const2), pl.BlockSpec((1, D_SGU), const2),
                   pl.BlockSpec((SGU_GROUPS, BLOCK, BLOCK), lambda n: (0, 0, 0)), pl.BlockSpec((SGU_GROUPS, BLOCK), const2)),
        scratch_shapes=[pltpu.VMEM((BLOCK, D_IN), BF16), pltpu.VMEM((2 * BLOCK, 2 * D_KV), F32), pltpu.VMEM((BLOCK, 2 * D_KV), F32),
                        pltpu.VMEM((1, BLOCK), F32), pltpu.VMEM((BLOCK, BLOCK), F32)],
        compiler_params=_params(("arbitrary",)),
    )(z, z, dcat, o, sinks, ln_g, ln_b, sgu_w, sgu_bt)


def _in_proj_bwd(dz, win_t, x, dx1, norm_g, mod):
    s, d = x.shape
    kdim = dz.shape[1]
    tm = min(256, s)

    def body(dz_ref, w_ref, x_ref, dx1_ref, g_ref, mod_ref, gx_ref, dshift_ref, dscale_ref, dng_ref):
        @pl.when(pl.program_id(0) == 0)
        def _():
            dshift_ref[...] = jnp.zeros_like(dshift_ref)
            dscale_ref[...] = jnp.zeros_like(dscale_ref)
            dng_ref[...] = jnp.zeros_like(dng_ref)

        dh = _dot(dz_ref[...], w_ref[...], NN)
        xv = x_ref[...]
        ng = g_ref[...]
        r = lax.rsqrt(jnp.mean(xv * xv, axis=-1, keepdims=True) + EPS)
        xn = xv * r
        dshift_ref[...] += jnp.sum(dh, axis=0, keepdims=True)
        dscale_ref[...] += jnp.sum(dh * (xn * ng), axis=0, keepdims=True)
        dhs = dh * (1.0 + mod_ref[:, d:2 * d])
        dng_ref[...] += jnp.sum(dhs * xn, axis=0, keepdims=True)
        dxn = dhs * ng
        gx_ref[...] = dx1_ref[...] + r * dxn - xn * (r * r * jnp.mean(dxn * xv, axis=-1, keepdims=True))

    row = lambda i: (i, 0)
    const = lambda i: (0, 0)
    return pl.pallas_call(
        body, name="in_proj_bwd", grid=(s // tm,),
        out_shape=(jax.ShapeDtypeStruct((s, d), F32),) + (jax.ShapeDtypeStruct((1, d), F32),) * 3,
        in_specs=[pl.BlockSpec((tm, kdim), row), pl.BlockSpec((kdim, d), const, pipeline_mode=pl.Buffered(1)),
                  pl.BlockSpec((tm, d), row), pl.BlockSpec((tm, d), row), pl.BlockSpec((1, d), const), pl.BlockSpec((1, 3 * d), const)],
        out_specs=(pl.BlockSpec((tm, d), row),) + (pl.BlockSpec((1, d), const),) * 3,
        compiler_params=_params(("arbitrary",)),
    )(dz, win_t, x, dx1, norm_g, mod)


def _sum_slots(slots, own, place, rows_tile, full_cols, name):
    n_slots, r, hcols = slots.shape
    nblk = r // rows_tile
    per_device = n_slots == N_DEV

    def body(place_ref, s_ref, own_ref, o_ref):
        mine = 2 * place_ref[1] + place_ref[0] if per_device else place_ref[1]
        acc = jnp.zeros((rows_tile, hcols), F32)
        for k in range(n_slots):
            acc = acc + jnp.where(mine == k, own_ref[...], s_ref[k]).astype(F32)
        o_ref[...] = acc

    return pl.pallas_call(
        body, name=name,
        grid_spec=pltpu.PrefetchScalarGridSpec(
            num_scalar_prefetch=1, grid=(nblk,),
            in_specs=[pl.BlockSpec((n_slots, rows_tile, hcols), lambda i, p: (0, i, 0)),
                      pl.BlockSpec((rows_tile, hcols), lambda i, p: (p[1] * nblk + i, p[0] if per_device else 0))],
            out_specs=pl.BlockSpec((rows_tile, hcols), lambda i, p: (i, p[0]))),
        out_shape=jax.ShapeDtypeStruct((r, full_cols), F32),
        compiler_params=_params(("parallel",)),
    )(place, slots, own)


def _adamw_rows(w, g, m, v, tile, name, after):
    r, c = w.shape

    def body(w_ref, g_ref, m_ref, v_ref, after_ref, go_ref, d_ref, mo_ref, vo_ref):
        g_ = g_ref[...]
        delta, m2, v2 = _adamw(w_ref[...], g_, m_ref[...], v_ref[...])
        go_ref[...] = g_
        d_ref[...] = delta
        mo_ref[...] = m2
        vo_ref[...] = v2

    blk = pl.BlockSpec((tile, c), lambda i: (i, 0))
    return pl.pallas_call(
        body, name=name, grid=(r // tile,), out_shape=(jax.ShapeDtypeStruct((r, c), F32),) * 4,
        in_specs=[blk] * 4 + [ANY], out_specs=(blk,) * 4,
        compiler_params=_params(("parallel",)),
    )(w, g, m, v, after)


def _adamw_ada(cact, dmod, w, m, v, tile=256):
    r, c = w.shape

    def body(c_ref, d_ref, w_ref, m_ref, v_ref, go_ref, dl_ref, mo_ref, vo_ref):
        g_ = _dot(c_ref[...].astype(BF16), d_ref[...].astype(BF16), TN)
        delta, m2, v2 = _adamw(w_ref[...], g_, m_ref[...], v_ref[...])
        go_ref[...] = g_
        dl_ref[...] = delta
        mo_ref[...] = m2
        vo_ref[...] = v2

    blk = pl.BlockSpec((tile, c), lambda i: (i, 0))
    return pl.pallas_call(
        body, name="adamw_ada", grid=(r // tile,), out_shape=(jax.ShapeDtypeStruct((r, c), F32),) * 4,
        in_specs=[pl.BlockSpec((N_DEV, tile), lambda i: (0, i)), pl.BlockSpec((N_DEV, c), lambda i: (0, 0)), blk, blk, blk],
        out_specs=(blk,) * 4,
        compiler_params=_params(("parallel",)),
    )(cact, dmod, w, m, v)


def kernel(x, c, norm_g, w_ada, b_ada, w_in, attn_sinks, sgu_ln_g, sgu_ln_b, sgu_w, sgu_b, w_out, final_g, loss_target, m_norm_g, m_w_ada, m_b_ada, m_w_in, m_attn_sinks, m_sgu_ln_g, m_sgu_ln_b, m_sgu_w, m_sgu_b, m_w_out, m_final_g, v_norm_g, v_w_ada, v_b_ada, v_w_in, v_attn_sinks, v_sgu_ln_g, v_sgu_ln_b, v_sgu_w, v_sgu_b, v_w_out, v_final_g):
    xs, tgt = x[0], loss_target[0]
    core = lax.axis_index("c").astype(jnp.int32).reshape(1)
    chip = 2 * lax.axis_index("x") + lax.axis_index("y")
    place = jnp.stack([core[0], chip.astype(jnp.int32)])
    sgu_bt = sgu_b[0].T

    w_in_t, m_in_t, v_in_t = (jnp.swapaxes(a[0], 0, 1) for a in (w_in, m_w_in, v_w_in))
    win_t = _cast_into_gathered(w_in_t, IN_SHARD // 4, place, "cast_w_in", c)
    win_t, mod, cact = _gather_w_in_and_modulate(win_t, c, w_ada[0], b_ada)
    h = _norm_modulate(xs, norm_g, mod)
    wout = _cast_into_gathered(w_out[0], OUT_SHARD // 2, place, "cast_w_out", win_t)
    sems, bufs, token = _split_start(_gather_copies, [wout], 3, "w_out_gather_start", win_t)
    z = _matmul_nt(h, win_t, 1024, 768, F32, "in_proj", token)
    cat, o = _mixer_fwd(z, attn_sinks, sgu_ln_g, sgu_ln_b, sgu_w[0], sgu_bt)
    wout = _gather_finish(_split_wait(_gather_copies, sems, bufs, cat, "w_out_gather_wait")[0], "w_out_gather_finish")
    dx1, dy, dfg, dgate, loss = _out_proj_loss(cat, wout, xs, tgt, mod, final_g.reshape(1, D_MODEL))

    po = _matmul_tn(cat, dy, 512, "w_out_grad")
    sems, bufs, token = _split_start(_owner_copies, [po, lax.empty((N_DEV, OUT_SHARD, HALF), BF16)], N_DEV - 1,
                                     "w_out_grad_to_owners_start", cat)
    dcat = _matmul_nt(dy, wout, 512, 1024, F32, "out_proj_bwd", token)
    dz, dsinks, dlg, dlb, dsgu_w, dsgu_b = _mixer_bwd(z, dcat, o, attn_sinks, sgu_ln_g, sgu_ln_b, sgu_w[0], sgu_bt)
    po, slots_o = _split_wait(_owner_copies, sems, bufs, dz, "w_out_grad_to_owners_wait")
    go = _halves_swap(_sum_slots(slots_o, po, place, OUT_SHARD // 2, D_MODEL, "w_out_grad_sum"), "w_out_grad_halves_swap")

    sw = _to_sibling(_grad_matmul(dz, h, 1 - core, 768, "w_in_grad_sibling"), "w_in_grad_to_sibling")
    pw = _grad_matmul(dz, h, core, 768, "w_in_grad_own", add=sw)
    first = functools.partial(_chip_copies, [(0, 0, IN_SHARD), (1, 0, IN_SHARD), (2, 0, IN_SHARD // 4)])
    rest = functools.partial(_chip_copies, [(2, IN_SHARD // 4, 3 * IN_SHARD // 4)])
    sems, bufs, token = _split_start(first, [pw, lax.empty((N_CHIPS, IN_SHARD, HALF), BF16)], 3, "w_in_grad_to_chips_start", sw)
    gx, dshift, dscale, dng = _in_proj_bwd(dz, win_t, xs, dx1, norm_g + token[0:1, 0:1], mod)
    bufs = _split_wait(first, sems, bufs, gx, "w_in_grad_to_chips_wait")

    parts = dict(dng=dng, dshift=dshift, dscale=dscale, dgate=dgate, dfg=dfg, dlg=dlg, dlb=dlb, dsinks=dsinks,
                 dsgu_b=dsgu_b, dsgu_w=dsgu_w, loss=loss)
    small = dict(norm_g=(norm_g, m_norm_g, v_norm_g), b_ada=(b_ada, m_b_ada, v_b_ada),
                 attn_sinks=(attn_sinks, m_attn_sinks, v_attn_sinks), sgu_ln_g=(sgu_ln_g, m_sgu_ln_g, v_sgu_ln_g),
                 sgu_ln_b=(sgu_ln_b, m_sgu_ln_b, v_sgu_ln_b), sgu_w=(sgu_w, m_sgu_w, v_sgu_w), sgu_b=(sgu_b, m_sgu_b, v_sgu_b),
                 final_g=(final_g, m_final_g, v_final_g))
    tot, gsgu_w, dmod_mine = _small_grads_sum(parts, bufs[1])
    sems, bufs, token = _split_start(rest, list(bufs), 1, "w_in_grad_rest_to_chips_start", tot)
    res, total = _small_adamw(tot, gsgu_w, small)
    res["w_ada"] = _adamw_ada(cact, dmod_mine + token[0:1, 0:1], w_ada[0], m_w_ada[0], v_w_ada[0])
    res["w_out"] = _adamw_rows(w_out[0], go, m_w_out[0], v_w_out[0], 256, "adamw_w_out", res["w_ada"][1])

    pw, slots_w = _split_wait(rest, sems, bufs, res["w_out"][1], "w_in_grad_rest_to_chips_wait")
    gw_t = _halves_swap(_sum_slots(slots_w, pw, place, IN_SHARD // 4, D_MODEL, "w_in_grad_sum"), "w_in_grad_halves_swap")
    res["w_in"] = tuple(jnp.swapaxes(a, 0, 1) for a in _adamw_rows(w_in_t, gw_t, m_in_t, v_in_t, IN_SHARD // 4, "adamw_w_in", gw_t))

    order = ["norm_g", "w_ada", "b_ada", "w_in", "attn_sinks", "sgu_ln_g", "sgu_ln_b", "sgu_w", "sgu_b", "w_out", "final_g"]
    like = dict(norm_g=norm_g, w_ada=w_ada, b_ada=b_ada, w_in=w_in, attn_sinks=attn_sinks, sgu_ln_g=sgu_ln_g, sgu_ln_b=sgu_ln_b,
                sgu_w=sgu_w, sgu_b=sgu_b, w_out=w_out, final_g=final_g)
    outs = [total[0, 0], gx[None]]
    for kind in range(4):
        outs += [res[n][kind].reshape(like[n].shape) for n in order]
    return tuple(outs)
```

```python
import functools
import math

import jax
import jax.numpy as jnp
from jax import lax
from jax.experimental import pallas as pl
from jax.experimental.pallas import tpu as pltpu

F32 = jnp.float32
BF16 = jnp.bfloat16

D_MODEL = 2048
D_ATTN = 1024
HEAD_DIM = 64
N_Q_HEADS = 16
Q_PER_KV = 8
D_KV = 128
BLOCK = 128
D_SGU = 1024
SGU_GROUPS = 8
GROUP_DIM = 128
D_IN = 5376
EPS = 1e-6
ATTN_SCALE = 1.0 / math.sqrt(HEAD_DIM)
N_CHIPS = 4
N_DEV = 8
IN_SHARD = D_IN // N_CHIPS
OUT_SHARD = D_MODEL // N_CHIPS
ADA_SHARD = 3 * D_MODEL // N_CHIPS
HALF = D_MODEL // 2

Q0, K0, V0, GA0, U0, VS0, GS0 = 0, 1024, 1152, 1280, 2304, 3328, 4352

ADAM_LR = 0.001
ADAM_B1 = 0.9
ADAM_B2 = 0.999
ADAM_EPS = 1e-08
ADAM_WD = 0.01
ADAM_STEP = 10

VMEM_LIMIT_V7X = 56 * 1024 * 1024

NN = (((1,), (0,)), ((), ()))
NT = (((1,), (1,)), ((), ()))
TN = (((0,), (0,)), ((), ()))
MESH = pl.DeviceIdType.MESH
ANY = pl.BlockSpec(memory_space=pl.ANY)
VMEM = pl.BlockSpec(memory_space=pltpu.VMEM)
HBM = pl.BlockSpec(memory_space=pltpu.HBM)
SEM = pl.BlockSpec(memory_space=pltpu.SEMAPHORE)
EFFECT = pltpu.SideEffectType.DATAFLOW_SIDE_EFFECTING


def _dot(a, b, dims):
    return lax.dot_general(a, b, dims, preferred_element_type=F32)


def _params(sem=None):
    return pltpu.CompilerParams(dimension_semantics=sem, vmem_limit_bytes=VMEM_LIMIT_V7X)


def _sigmoid(x):
    return jax.nn.sigmoid(x)


def _adamw(w, g, m, v):
    m = ADAM_B1 * m + (1.0 - ADAM_B1) * g
    v = ADAM_B2 * v + (1.0 - ADAM_B2) * (g * g)
    m_hat = m / (1.0 - ADAM_B1 ** ADAM_STEP)
    v_hat = v / (1.0 - ADAM_B2 ** ADAM_STEP)
    delta = -ADAM_LR * (m_hat / (jnp.sqrt(v_hat) + ADAM_EPS) + ADAM_WD * w)
    return delta, m, v


def _place():
    return lax.axis_index("x"), lax.axis_index("y"), lax.axis_index("c")


def _flip(p, bit):
    return 1 - p if bit else p


def _other_chips(x, y):
    return [(_flip(x, bx), _flip(y, by)) for bx, by in ((0, 1), (1, 0), (1, 1))]


def _split_start(copies, bufs, n_sems, name, after):
    n = len(bufs)

    def body(*refs):
        send_sems, recv_sems = refs[n + 1:n + 3]
        for cp in copies(*refs[:n], send_sems, recv_sems):
            cp.start()
        refs[-1][...] = jnp.zeros_like(refs[-1])

    bufs = [pltpu.with_memory_space_constraint(a, pltpu.HBM) for a in bufs]
    outs = pl.pallas_call(
        body, name=name,
        out_shape=(pltpu.SemaphoreType.DMA((n_sems,)), pltpu.SemaphoreType.DMA((n_sems,)))
        + tuple(pltpu.HBM(a.shape, a.dtype) for a in bufs) + (jax.ShapeDtypeStruct((8, 128), F32),),
        in_specs=[HBM] * n + [ANY], out_specs=(SEM, SEM) + (HBM,) * n + (VMEM,), input_output_aliases={k: k + 2 for k in range(n)},
        compiler_params=pltpu.CompilerParams(has_side_effects=EFFECT),
    )(*bufs, after)
    return outs[:2], list(outs[2:2 + n]), outs[-1]


def _split_wait(copies, sems, bufs, after, name):
    n = len(bufs)

    def body(*refs):
        send_sems, recv_sems = refs[n:n + 2]
        for cp in copies(*refs[:n], send_sems, recv_sems):
            cp.wait_send()
            cp.wait_recv()

    return pl.pallas_call(
        body, name=name, out_shape=tuple(pltpu.HBM(a.shape, a.dtype) for a in bufs),
        in_specs=[HBM] * n + [SEM, SEM, ANY], out_specs=(HBM,) * n, input_output_aliases={k: k for k in range(n)},
        compiler_params=pltpu.CompilerParams(has_side_effects=EFFECT),
    )(*bufs, *sems, after)


def _gather_copies(full_ref, send_sems, recv_sems):
    x, y, cc = _place()
    shard = full_ref.shape[0] // N_CHIPS
    half = shard // 2
    mine = full_ref.at[pl.ds(shard * (2 * x + y) + half * cc, half), :]
    return [pltpu.make_async_remote_copy(src_ref=mine, dst_ref=mine, send_sem=send_sems.at[2 * j + pc],
                                         recv_sem=recv_sems.at[2 * j + _flip(cc, pc)], device_id=(px, py, pc), device_id_type=MESH)
            for j, (px, py) in enumerate(_other_chips(x, y)) for pc in range(2)]


def _gather_w_in_and_modulate(full, c, w_ada, b_ada):
    shard = full.shape[0] // N_CHIPS
    half = shard // 2
    quarter = half // 2
    kdim = w_ada.shape[0]
    kc = 256

    def body(full_in, c_ref, w_hbm, b_ref, full_ref, mod_ref, cact_ref, w_vmem, cbuf, pbuf, rbuf, w_sem,
             send_sems, recv_sems, msend_sems, mrecv_sems):
        x, y, cc = _place()
        me = 4 * x + 2 * y + cc
        chip = 2 * x + y
        across_x, across_y, far = (1 - x, y), (x, 1 - y), (1 - x, 1 - y)

        def rows(chip_xy, core, q):
            return full_ref.at[pl.ds(shard * (2 * chip_xy[0] + chip_xy[1]) + half * core + quarter * q, quarter), :]

        def copy(k, piece, to):
            return pltpu.make_async_remote_copy(src_ref=piece, dst_ref=piece, send_sem=send_sems.at[k], recv_sem=recv_sems.at[k],
                                                device_id=to, device_id_type=MESH)

        def small(k, src, dst, to):
            return pltpu.make_async_remote_copy(src_ref=src, dst_ref=dst, send_sem=msend_sems.at[k], recv_sem=mrecv_sems.at[k],
                                                device_id=to, device_id_type=MESH)

        cbuf[me] = c_ref[...]
        gathers = [small(k - 1, cbuf.at[me], cbuf.at[me], (_flip(x, (k >> 2) & 1), _flip(y, (k >> 1) & 1), _flip(cc, k & 1)))
                   for k in range(1, N_DEV)]
        for cp in gathers:
            cp.start()
        w_load = pltpu.make_async_copy(w_hbm, w_vmem, w_sem)
        w_load.start()
        direct = [copy(0, rows((x, y), cc, 0), (*across_x, cc)), copy(2, rows((x, y), cc, 1), (*across_y, cc)),
                  copy(1, rows((x, y), cc, 1), (*across_x, cc)), copy(3, rows((x, y), cc, 0), (*across_y, cc))]
        for cp in direct:
            cp.start()

        for cp in gathers:
            cp.wait()
        w_load.wait()
        for b in range(N_DEV):
            cv = cbuf[b]
            cact_ref[b:b + 1, :] = cv * _sigmoid(cv)
        acc = jnp.zeros((N_DEV, ADA_SHARD), F32)
        for k0 in range(0, kdim, kc):
            acc = acc + _dot(cact_ref[:, k0:k0 + kc].astype(BF16), w_vmem[k0:k0 + kc, :].astype(BF16), NN)
        for b in range(N_DEV):
            pbuf[b] = acc[b:b + 1, :]
        rbuf[chip] = pbuf[me]
        sends = [small(N_DEV - 1 + j, pbuf.at[4 * px + 2 * py + cc], rbuf.at[chip], (px, py, cc))
                 for j, (px, py) in enumerate(_other_chips(x, y))]
        for cp in sends:
            cp.start()

        landing = [(0, across_x, 0, (4, (*across_y, cc))), (2, across_y, 1, (5, (*across_x, cc))),
                   (1, across_x, 1, None), (3, across_y, 0, None), (4, far, 0, None), (5, far, 1, None)]
        passed = []
        for n, (k, whose, q, onward) in enumerate(landing):
            piece = rows(whose, cc, q)
            copy(k, piece, (x, y, cc)).wait_recv()
            if onward is not None:
                passed.append(copy(onward[0], piece, onward[1]))
                passed[-1].start()
            passed.append(copy(6 + n, piece, (x, y, 1 - cc)))
            passed[-1].start()
        for n, (k, whose, q, onward) in enumerate(landing):
            copy(6 + n, rows(whose, 1 - cc, q), (x, y, cc)).wait_recv()
        for cp in sends:
            cp.wait()
        for k in range(N_CHIPS):
            cols = slice(ADA_SHARD * k, ADA_SHARD * (k + 1))
            mod_ref[:, cols] = rbuf[k] + b_ref[:, cols]
        for cp in direct + passed:
            cp.wait_send()

    return pl.pallas_call(
        body, name="w_in_gather_and_modulate",
        out_shape=(jax.ShapeDtypeStruct(full.shape, full.dtype), jax.ShapeDtypeStruct((1, 3 * D_MODEL), F32),
                   jax.ShapeDtypeStruct((N_DEV, D_MODEL), F32)),
        in_specs=[ANY, VMEM, ANY, VMEM], out_specs=(ANY, VMEM, VMEM), input_output_aliases={0: 0},
        scratch_shapes=[pltpu.VMEM(w_ada.shape, F32), pltpu.VMEM((N_DEV, 1, D_MODEL), F32), pltpu.VMEM((N_DEV, 1, ADA_SHARD), F32),
                        pltpu.VMEM((N_CHIPS, 1, ADA_SHARD), F32), pltpu.SemaphoreType.DMA,
                        pltpu.SemaphoreType.DMA((12,)), pltpu.SemaphoreType.DMA((12,)),
                        pltpu.SemaphoreType.DMA((N_DEV + 2,)), pltpu.SemaphoreType.DMA((N_DEV + 2,))],
        compiler_params=_params(),
    )(full, c, w_ada, b_ada)


def _to_sibling(g, name):
    def body(g_ref, r_ref, send_sem, recv_sem):
        x, y, cc = _place()
        cp = pltpu.make_async_remote_copy(src_ref=g_ref, dst_ref=r_ref, send_sem=send_sem, recv_sem=recv_sem,
                                          device_id=(x, y, 1 - cc), device_id_type=MESH)
        cp.start()
        cp.wait()

    return pl.pallas_call(
        body, name=name, out_shape=jax.ShapeDtypeStruct(g.shape, g.dtype), in_specs=[ANY], out_specs=ANY,
        scratch_shapes=[pltpu.SemaphoreType.DMA, pltpu.SemaphoreType.DMA], compiler_params=_params(),
    )(g)


def _chip_copies(pieces, p_ref, land_ref, send_sems, recv_sems):
    x, y, cc = _place()
    chip = 2 * x + y
    shard = land_ref.shape[1]
    others = _other_chips(x, y)
    cps = []
    for k, (j, r0, nr) in enumerate(pieces):
        px, py = others[j]
        cps.append(pltpu.make_async_remote_copy(
            src_ref=p_ref.at[pl.ds(shard * (2 * px + py) + r0, nr), :], dst_ref=land_ref.at[chip, pl.ds(r0, nr), :],
            send_sem=send_sems.at[k], recv_sem=recv_sems.at[k], device_id=(px, py, cc), device_id_type=MESH))
    return cps


def _owner_copies(p_ref, land_ref, send_sems, recv_sems):
    x, y, cc = _place()
    me = 4 * x + 2 * y + cc
    shard = land_ref.shape[1]
    cps = []
    for k in range(1, N_DEV):
        px, py, pc = _flip(x, (k >> 2) & 1), _flip(y, (k >> 1) & 1), _flip(cc, k & 1)
        block = p_ref.at[pl.ds(shard * (2 * px + py), shard), pl.ds(pl.multiple_of(HALF * pc, 128), HALF)]
        cps.append(pltpu.make_async_remote_copy(src_ref=block, dst_ref=land_ref.at[me], send_sem=send_sems.at[k - 1],
                                                recv_sem=recv_sems.at[k - 1], device_id=(px, py, pc), device_id_type=MESH))
    return cps


def _halves_swap(f, name):
    def body(f_in, f_ref, send_sem, recv_sem):
        x, y, cc = _place()
        mine = f_ref.at[:, pl.ds(pl.multiple_of(HALF * cc, 128), HALF)]
        cp = pltpu.make_async_remote_copy(src_ref=mine, dst_ref=mine, send_sem=send_sem, recv_sem=recv_sem,
                                          device_id=(x, y, 1 - cc), device_id_type=MESH)
        cp.start()
        cp.wait()

    return pl.pallas_call(
        body, name=name, out_shape=jax.ShapeDtypeStruct(f.shape, f.dtype),
        in_specs=[ANY], out_specs=ANY, input_output_aliases={0: 0},
        scratch_shapes=[pltpu.SemaphoreType.DMA, pltpu.SemaphoreType.DMA], compiler_params=_params(),
    )(f)


ROW_NORM_G, ROW_DMOD, ROW_FINAL_G, ROW_LN, ROW_SINKS, ROW_LOSS, ROW_SGU_B, MISC_ROWS = 0, 1, 4, 5, 6, 7, 8, 16


def _small_grads_sum(parts, after):
    part_names = ["dng", "dshift", "dscale", "dgate", "dfg", "dlg", "dlb", "dsinks", "dsgu_b", "dsgu_w", "loss"]
    n_parts = len(part_names)
    n_peers = N_DEV - 1

    def body(*refs):
        p = dict(zip(part_names, refs[:n_parts]))
        tot_ref, gw_ref, dmod_ref = refs[n_parts + 1:n_parts + 4]
        misc, cmisc, dsend, dbuf, wland, wsum, own_sem, send_sems, recv_sems = refs[n_parts + 4:]
        x, y, cc = _place()
        me = 4 * x + 2 * y + cc
        chip = 2 * x + y
        peers = [(_flip(x, (k >> 2) & 1), _flip(y, (k >> 1) & 1), _flip(cc, k & 1)) for k in range(1, N_DEV)]

        def remote(k, src, dst, peer):
            return pltpu.make_async_remote_copy(src_ref=src, dst_ref=dst, send_sem=send_sems.at[k], recv_sem=recv_sems.at[k],
                                                device_id=peer, device_id_type=MESH)

        dsend[:, 0:D_MODEL] = p["dshift"][...]
        dsend[:, D_MODEL:2 * D_MODEL] = p["dscale"][...]
        dsend[:, 2 * D_MODEL:3 * D_MODEL] = p["dgate"][...]

        def shard_cols(chip_index):
            return dsend.at[:, pl.ds(pl.multiple_of(ADA_SHARD * chip_index, 128), ADA_SHARD)]

        own_cols = pltpu.make_async_copy(shard_cols(chip), dbuf.at[me], own_sem)
        own_cols.start()
        gathers = [remote(k, shard_cols(2 * px + py), dbuf.at[me], (px, py, pc)) for k, (px, py, pc) in enumerate(peers)]
        wland[me] = p["dsgu_w"][me]
        scatters = [remote(n_peers + k, p["dsgu_w"].at[4 * px + 2 * py + pc], wland.at[me], (px, py, pc))
                    for k, (px, py, pc) in enumerate(peers)]

        misc[cc] = jnp.zeros((MISC_ROWS, D_MODEL), F32)
        misc[cc, ROW_NORM_G:ROW_NORM_G + 1, :] = p["dng"][...]
        misc[cc, ROW_DMOD:ROW_DMOD + 1, :] = p["dshift"][...]
        misc[cc, ROW_DMOD + 1:ROW_DMOD + 2, :] = p["dscale"][...]
        misc[cc, ROW_DMOD + 2:ROW_DMOD + 3, :] = p["dgate"][...]
        misc[cc, ROW_FINAL_G:ROW_FINAL_G + 1, :] = p["dfg"][...]
        misc[cc, ROW_LN:ROW_LN + 1, 0:D_SGU] = p["dlg"][...]
        misc[cc, ROW_LN:ROW_LN + 1, D_SGU:2 * D_SGU] = p["dlb"][...]
        misc[cc, ROW_SINKS:ROW_SINKS + 1, 0:N_Q_HEADS] = p["dsinks"][...]
        misc[cc, ROW_LOSS:ROW_LOSS + 1, 0:1] = p["loss"][...]
        misc[cc, ROW_SGU_B:ROW_SGU_B + SGU_GROUPS, 0:BLOCK] = p["dsgu_b"][...]
        pair = remote(3 * n_peers, misc.at[cc], misc.at[cc], (x, y, 1 - cc))
        for cp in gathers + scatters + [pair]:
            cp.start()
        pair.wait()
        cmisc[chip] = misc[0] + misc[1]
        quad = [remote(3 * n_peers + 1 + j, cmisc.at[chip], cmisc.at[chip], (_flip(x, bx), _flip(y, by), cc))
                for j, (bx, by) in enumerate(((0, 1), (1, 0), (1, 1)))]
        for cp in quad:
            cp.start()
        for cp in scatters:
            cp.wait()
        group = wland[0]
        for b in range(1, N_DEV):
            group = group + wland[b]
        wsum[me] = group
        spreads = [remote(2 * n_peers + k, wsum.at[me], wsum.at[me], peer) for k, peer in enumerate(peers)]
        for cp in spreads:
            cp.start()
        for cp in gathers + quad + spreads:
            cp.wait()

        tot = cmisc[0]
        for k in range(1, N_CHIPS):
            tot = tot + cmisc[k]
        tot_ref[...] = tot
        gw_ref[...] = wsum[...]
        own_cols.wait()
        for b in range(N_DEV):
            dmod_ref[b:b + 1, :] = dbuf[b]

    operands = [parts[n] for n in part_names]
    n_sems = 3 * n_peers + 1 + (N_CHIPS - 1)
    return pl.pallas_call(
        body, name="small_grads_sum",
        out_shape=(jax.ShapeDtypeStruct((MISC_ROWS, D_MODEL), F32), jax.ShapeDtypeStruct((SGU_GROUPS, BLOCK, BLOCK), F32),
                   jax.ShapeDtypeStruct((N_DEV, ADA_SHARD), F32)),
        in_specs=[VMEM] * len(operands) + [ANY], out_specs=(VMEM, VMEM, VMEM),
        scratch_shapes=[pltpu.VMEM((2, MISC_ROWS, D_MODEL), F32), pltpu.VMEM((N_CHIPS, MISC_ROWS, D_MODEL), F32),
                        pltpu.VMEM((1, 3 * D_MODEL), F32), pltpu.VMEM((N_DEV, 1, ADA_SHARD), F32),
                        pltpu.VMEM((N_DEV, BLOCK, BLOCK), F32), pltpu.VMEM((SGU_GROUPS, BLOCK, BLOCK), F32),
                        pltpu.SemaphoreType.DMA, pltpu.SemaphoreType.DMA((n_sems,)), pltpu.SemaphoreType.DMA((n_sems,))],
        compiler_params=_params(),
    )(*operands, after)


def _small_adamw(tot, gw, small):
    names = ["norm_g", "b_ada", "attn_sinks", "sgu_ln_g", "sgu_ln_b", "sgu_w", "sgu_b", "final_g"]

    def body(*refs):
        tot_ref, gw_ref = refs[:2]
        wmv = {n: refs[2 + 3 * i: 5 + 3 * i] for i, n in enumerate(names)}
        outs = refs[2 + 3 * len(names):]
        res = {n: outs[4 * i: 4 * i + 4] for i, n in enumerate(names)}
        outs[4 * len(names)][...] = tot_ref[ROW_LOSS:ROW_LOSS + 1, 0:1]

        def update(name, g, pick=lambda r: r[...], put=None):
            w_ref, m_ref, v_ref = wmv[name]
            delta, m2, v2 = _adamw(pick(w_ref), g, pick(m_ref), pick(v_ref))
            for ref, val in zip(res[name], (g, delta, m2, v2)):
                if put is None:
                    ref[...] = val
                else:
                    put(ref, val)

        update("norm_g", tot_ref[ROW_NORM_G:ROW_NORM_G + 1, :])
        update("final_g", tot_ref[ROW_FINAL_G:ROW_FINAL_G + 1, :])
        for t in range(3):
            cols = slice(D_MODEL * t, D_MODEL * (t + 1))

            def put_cols(ref, val, cols=cols):
                ref[:, cols] = val
            update("b_ada", tot_ref[ROW_DMOD + t:ROW_DMOD + t + 1, :], pick=lambda r, cols=cols: r[:, cols], put=put_cols)
        update("sgu_ln_g", tot_ref[ROW_LN:ROW_LN + 1, 0:D_SGU])
        update("sgu_ln_b", tot_ref[ROW_LN:ROW_LN + 1, D_SGU:2 * D_SGU])
        update("attn_sinks", tot_ref[ROW_SINKS:ROW_SINKS + 1, 0:N_Q_HEADS])
        update("sgu_b", tot_ref[ROW_SGU_B:ROW_SGU_B + SGU_GROUPS, 0:BLOCK])
        for g in range(SGU_GROUPS):
            def put_group(ref, val, g=g):
                ref[g] = val
            update("sgu_w", gw_ref[g], pick=lambda r, g=g: r[g], put=put_group)

    shapes = {"norm_g": (1, D_MODEL), "b_ada": (1, 3 * D_MODEL), "attn_sinks": (1, N_Q_HEADS), "sgu_ln_g": (1, D_SGU),
              "sgu_ln_b": (1, D_SGU), "sgu_w": (SGU_GROUPS, BLOCK, BLOCK), "sgu_b": (SGU_GROUPS, BLOCK), "final_g": (1, D_MODEL)}
    operands = [tot, gw]
    for n in names:
        operands += [a.reshape(shapes[n]) for a in small[n]]
    out_shape = []
    for n in names:
        out_shape += [jax.ShapeDtypeStruct(shapes[n], F32)] * 4
    out_shape.append(jax.ShapeDtypeStruct((1, 1), F32))
    outs = pl.pallas_call(
        body, name="small_adamw", out_shape=tuple(out_shape),
        in_specs=[VMEM] * len(operands), out_specs=tuple([VMEM] * len(out_shape)), compiler_params=_params(),
    )(*operands)
    return {n: outs[4 * i: 4 * i + 4] for i, n in enumerate(names)}, outs[4 * len(names)]


def _cast_into_gathered(w, tile, place, name, after):
    r, c = w.shape
    nblk = r // tile

    def body(place_ref, w_ref, after_ref, o_ref):
        o_ref[...] = w_ref[...].astype(BF16)

    return pl.pallas_call(
        body, name=name,
        grid_spec=pltpu.PrefetchScalarGridSpec(
            num_scalar_prefetch=1, grid=(nblk,), in_specs=[pl.BlockSpec((tile, c), lambda i, p: (i, 0)), ANY],
            out_specs=pl.BlockSpec((tile, c), lambda i, p: (p[1] * nblk + i, 0))),
        out_shape=jax.ShapeDtypeStruct((N_CHIPS * r, c), BF16),
        compiler_params=_params(("parallel",)),
    )(place, w, after)


def _in_proj(x, norm_g, mod, win_t, after):
    s, d = x.shape
    n = win_t.shape[0]
    tm, tn = min(1024, s), 768
    rows_at_once = min(256, tm)

    def body(x_ref, g_ref, mod_ref, w_ref, after_ref, z_ref, h_ref):
        @pl.when(pl.program_id(1) == 0)
        def _():
            for r0 in range(0, tm, rows_at_once):
                rows = slice(r0, r0 + rows_at_once)
                xv = x_ref[rows, :]
                r = lax.rsqrt(jnp.mean(xv * xv, axis=-1, keepdims=True) + EPS)
                h_ref[rows, :] = ((xv * r * g_ref[...]) * (1.0 + mod_ref[:, d:2 * d]) + mod_ref[:, 0:d]).astype(BF16)

        z_ref[...] = _dot(h_ref[...], w_ref[...], NT)

    return pl.pallas_call(
        body, name="in_proj", grid=(s // tm, n // tn),
        out_shape=(jax.ShapeDtypeStruct((s, n), F32), jax.ShapeDtypeStruct((s, d), BF16)),
        in_specs=[pl.BlockSpec((tm, d), lambda i, j: (i, 0)), pl.BlockSpec((1, d), lambda i, j: (0, 0)),
                  pl.BlockSpec((1, 3 * d), lambda i, j: (0, 0)), pl.BlockSpec((tn, d), lambda i, j: (j, 0)), ANY],
        out_specs=(pl.BlockSpec((tm, tn), lambda i, j: (i, j)), pl.BlockSpec((tm, d), lambda i, j: (i, 0))),
        compiler_params=_params(("parallel", "arbitrary")),
    )(x, norm_g, mod, win_t, after)


def _matmul_nt(a, bt, tm, tn, out_dtype, name, after):
    m, k = a.shape
    n = bt.shape[0]
    tm, tn = min(tm, m), min(tn, n)

    def body(a_ref, b_ref, after_ref, o_ref):
        o_ref[...] = _dot(a_ref[...], b_ref[...], NT).astype(out_dtype)

    return pl.pallas_call(
        body, name=name, grid=(m // tm, n // tn), out_shape=jax.ShapeDtypeStruct((m, n), out_dtype),
        in_specs=[pl.BlockSpec((tm, k), lambda i, j: (i, 0)), pl.BlockSpec((tn, k), lambda i, j: (j, 0)), ANY],
        out_specs=pl.BlockSpec((tm, tn), lambda i, j: (i, j)),
        compiler_params=_params(("parallel", "parallel")),
    )(a, bt, after)


def _matmul_tn(a, b, tm, name):
    t, m = a.shape
    n = b.shape[1]
    tm = min(tm, m)

    def body(a_ref, b_ref, o_ref):
        o_ref[...] = _dot(a_ref[...], b_ref[...], TN).astype(BF16)

    return pl.pallas_call(
        body, name=name, grid=(m // tm,), out_shape=jax.ShapeDtypeStruct((m, n), BF16),
        in_specs=[pl.BlockSpec((t, tm), lambda i: (0, i)), pl.BlockSpec((t, n), lambda i: (0, 0))],
        out_specs=pl.BlockSpec((tm, n), lambda i: (i, 0)),
        compiler_params=_params(("parallel",)),
    )(a, b)


def _grad_matmul(a, b, half, tm, name, add=None):
    t, m = a.shape
    tm = min(tm, m)

    def body(half_ref, a_ref, b_ref, *rest):
        if add is None:
            (o_ref,) = rest
            o_ref[...] = _dot(a_ref[...], b_ref[...], TN).astype(BF16)
        else:
            add_ref, o_ref = rest
            o_ref[...] = (_dot(a_ref[...], b_ref[...], TN) + add_ref[...].astype(F32)).astype(BF16)

    in_specs = [pl.BlockSpec((t, tm), lambda i, h: (0, i)), pl.BlockSpec((t, HALF), lambda i, h: (0, h[0]))]
    operands = [a, b]
    if add is not None:
        in_specs.append(pl.BlockSpec((tm, HALF), lambda i, h: (i, 0)))
        operands.append(add)
    return pl.pallas_call(
        body, name=name,
        grid_spec=pltpu.PrefetchScalarGridSpec(num_scalar_prefetch=1, grid=(m // tm,), in_specs=in_specs,
                                               out_specs=pl.BlockSpec((tm, HALF), lambda i, h: (i, 0))),
        out_shape=jax.ShapeDtypeStruct((m, HALF), BF16),
        compiler_params=_params(("parallel",)),
    )(half, *operands)


def _mixer_masks(n):
    key = lax.broadcasted_iota(jnp.int32, (2 * BLOCK, Q_PER_KV * BLOCK), 0)
    qry = lax.broadcasted_iota(jnp.int32, (2 * BLOCK, Q_PER_KV * BLOCK), 1) & (BLOCK - 1)
    valid = (key > qry) & (key <= qry + BLOCK) & ((key >= BLOCK) | (n > 0))
    r2 = lax.broadcasted_iota(jnp.int32, (BLOCK, BLOCK), 0)
    c2 = lax.broadcasted_iota(jnp.int32, (BLOCK, BLOCK), 1)
    return valid, r2 >= c2


def _layer_norm_parts(vs):
    mu = jnp.mean(vs, axis=-1, keepdims=True)
    xc = vs - mu
    rstd = lax.rsqrt(jnp.mean(xc * xc, axis=-1, keepdims=True) + EPS)
    return xc * rstd, rstd


def _scaled_queries(z_ref, hk):
    q = jnp.concatenate([z_ref[:, _head_cols(Q0, Q_PER_KV * hk + g)] for g in range(Q_PER_KV)], axis=0)
    return (q * ATTN_SCALE).astype(BF16)


def _head_cols(base, h):
    return slice(base + HEAD_DIM * h, base + HEAD_DIM * (h + 1))


def _sink_row(sink_ref, hk):
    return jnp.concatenate([jnp.broadcast_to(sink_ref[:, h:h + 1], (1, BLOCK))
                            for h in range(Q_PER_KV * hk, Q_PER_KV * (hk + 1))], axis=1)


def _attn_probs(qs, kh, valid, sk):
    s = jnp.where(valid, _dot(kh, qs, NT), -jnp.inf)
    m = jnp.maximum(jnp.max(s, axis=0, keepdims=True), sk)
    e = jnp.exp(s - m)
    es = jnp.exp(sk - m)
    inv = 1.0 / (jnp.sum(e, axis=0, keepdims=True) + es)
    return e * inv, es * inv


def _kv_ext(z_ref, kvp_ref):
    k_ext = jnp.concatenate([kvp_ref[:, 0:D_KV], z_ref[:, K0:K0 + D_KV]], axis=0).astype(BF16)
    v_ext = jnp.concatenate([kvp_ref[:, D_KV:2 * D_KV], z_ref[:, V0:V0 + D_KV]], axis=0).astype(BF16)
    return k_ext, v_ext


def _sgu_mixed(w_ref, bt_ref, vn, tril):
    ws = [jnp.where(tril, w_ref[g], 0.0).astype(BF16) for g in range(SGU_GROUPS)]
    mixed = jnp.concatenate([_dot(ws[g], vn[:, GROUP_DIM * g:GROUP_DIM * (g + 1)], NN) + bt_ref[:, g:g + 1]
                             for g in range(SGU_GROUPS)], axis=1)
    return ws, mixed


def _mixer_fwd(z, sinks, ln_g, ln_b, sgu_w, sgu_bt):
    s = z.shape[0]
    nb = s // BLOCK

    def body(z_ref, kvp_ref, sink_ref, lg_ref, lb_ref, w_ref, bt_ref, cat_ref, o_ref):
        n = pl.program_id(0)
        valid, tril = _mixer_masks(n)
        k_ext, v_ext = _kv_ext(z_ref, kvp_ref)
        for hk in range(2):
            kv = slice(HEAD_DIM * hk, HEAD_DIM * (hk + 1))
            p, _ = _attn_probs(_scaled_queries(z_ref, hk), k_ext[:, kv], valid, _sink_row(sink_ref, hk))
            o_t = _dot(v_ext[:, kv], p.astype(BF16), TN)
            for g in range(Q_PER_KV):
                h = Q_PER_KV * hk + g
                o = o_t[:, BLOCK * g:BLOCK * (g + 1)].T
                ga = z_ref[:, _head_cols(GA0, h)]
                o_ref[:, _head_cols(0, h)] = o
                cat_ref[:, _head_cols(0, h)] = (o * (ga * _sigmoid(ga))).astype(BF16)
        xhat, _ = _layer_norm_parts(z_ref[:, VS0:VS0 + D_SGU])
        vn = (xhat * lg_ref[...] + lb_ref[...]).astype(BF16)
        _, mixed = _sgu_mixed(w_ref, bt_ref, vn, tril)
        gt = z_ref[:, GS0:GS0 + D_SGU]
        cat_ref[:, D_ATTN:D_MODEL] = ((z_ref[:, U0:U0 + D_SGU] * mixed) * (gt * _sigmoid(gt))).astype(BF16)

    const2 = lambda n: (0, 0)
    return pl.pallas_call(
        body, name="mixer_fwd", grid=(nb,),
        out_shape=(jax.ShapeDtypeStruct((s, D_MODEL), BF16), jax.ShapeDtypeStruct((s, D_ATTN), F32)),
        in_specs=[pl.BlockSpec((BLOCK, D_IN), lambda n: (n, 0)),
                  pl.BlockSpec((BLOCK, 2 * D_KV), lambda n: (jnp.maximum(n - 1, 0), K0 // (2 * D_KV))),
                  pl.BlockSpec((1, N_Q_HEADS), const2), pl.BlockSpec((1, D_SGU), const2), pl.BlockSpec((1, D_SGU), const2),
                  pl.BlockSpec((SGU_GROUPS, BLOCK, BLOCK), lambda n: (0, 0, 0)), pl.BlockSpec((BLOCK, SGU_GROUPS), const2)],
        out_specs=(pl.BlockSpec((BLOCK, D_MODEL), lambda n: (n, 0)), pl.BlockSpec((BLOCK, D_ATTN), lambda n: (n, 0))),
        compiler_params=_params(("parallel",)),
    )(z, z, sinks, ln_g, ln_b, sgu_w, sgu_bt)


def _out_proj_loss(cat, wout, x, target, mod, final_g):
    s, d = x.shape
    tm = min(256, s)

    def body(cat_ref, w_ref, x_ref, t_ref, mod_ref, fg_ref, dx1_ref, dy_ref, dfg_ref, dgate_ref, loss_ref):
        @pl.when(pl.program_id(0) == 0)
        def _():
            dfg_ref[...] = jnp.zeros_like(dfg_ref)
            dgate_ref[...] = jnp.zeros_like(dgate_ref)
            loss_ref[...] = jnp.zeros_like(loss_ref)

        gate = mod_ref[:, 2 * d:3 * d]
        fg = fg_ref[...]
        yv = _dot(cat_ref[...], w_ref[...], NN)
        x1 = x_ref[...] + gate * yv
        r = lax.rsqrt(jnp.mean(x1 * x1, axis=-1, keepdims=True) + EPS)
        xr = x1 * r
        err = xr * fg - t_ref[...]
        dout = err * (1.0 / d)
        gy = dout * fg
        dx1 = r * gy - xr * (r * r * jnp.mean(gy * x1, axis=-1, keepdims=True))
        dx1_ref[...] = dx1
        dy_ref[...] = (gate * dx1).astype(BF16)
        dfg_ref[...] += jnp.sum(dout * xr, axis=0, keepdims=True)
        dgate_ref[...] += jnp.sum(dx1 * yv, axis=0, keepdims=True)
        loss_ref[...] += 0.5 * jnp.sum(jnp.mean(err * err, axis=-1, keepdims=True), axis=0, keepdims=True)

    row = lambda i: (i, 0)
    const = lambda i: (0, 0)
    return pl.pallas_call(
        body, name="out_proj_loss", grid=(s // tm,),
        out_shape=(jax.ShapeDtypeStruct((s, d), F32), jax.ShapeDtypeStruct((s, d), BF16), jax.ShapeDtypeStruct((1, d), F32),
                   jax.ShapeDtypeStruct((1, d), F32), jax.ShapeDtypeStruct((1, 1), F32)),
        in_specs=[pl.BlockSpec((tm, d), row), pl.BlockSpec((d, d), const), pl.BlockSpec((tm, d), row), pl.BlockSpec((tm, d), row),
                  pl.BlockSpec((1, 3 * d), const), pl.BlockSpec((1, d), const)],
        out_specs=(pl.BlockSpec((tm, d), row), pl.BlockSpec((tm, d), row), pl.BlockSpec((1, d), const), pl.BlockSpec((1, d), const),
                   pl.BlockSpec((1, 1), const)),
        compiler_params=_params(("arbitrary",)),
    )(cat, wout, x, target, mod, final_g)


def _mixer_bwd(z, dcat, o, sinks, ln_g, ln_b, sgu_w, sgu_bt):
    s = z.shape[0]
    nb = s // BLOCK

    def body(z_ref, kvp_ref, dc_ref, o_ref, sink_ref, lg_ref, lb_ref, w_ref, bt_ref,
             dz_ref, dsink_ref, dlg_ref, dlb_ref, dw_ref, db_ref, pend, dkv, carry, dsink_acc, dbt_acc):
        n = pl.program_id(0)

        @pl.when(n == 0)
        def _():
            pend[...] = jnp.zeros_like(pend)
            carry[...] = jnp.zeros_like(carry)
            dsink_acc[...] = jnp.zeros_like(dsink_acc)
            dbt_acc[...] = jnp.zeros_like(dbt_acc)
            dlg_ref[...] = jnp.zeros_like(dlg_ref)
            dlb_ref[...] = jnp.zeros_like(dlb_ref)
            dw_ref[...] = jnp.zeros_like(dw_ref)

        dz_ref[:, 0:K0] = pend[:, 0:K0]
        dz_ref[:, GA0:D_IN] = pend[:, GA0:D_IN]

        @pl.when(n == nb)
        def _():
            dkv[...] = jnp.zeros_like(dkv)

        @pl.when(n < nb)
        def _():
            valid, tril = _mixer_masks(n)
            k_ext, v_ext = _kv_ext(z_ref, kvp_ref)
            for hk in range(2):
                kv = slice(HEAD_DIM * hk, HEAD_DIM * (hk + 1))
                kh, vh = k_ext[:, kv], v_ext[:, kv]
                qs = _scaled_queries(z_ref, hk)
                p, p_sink = _attn_probs(qs, kh, valid, _sink_row(sink_ref, hk))
                dos = []
                for g in range(Q_PER_KV):
                    h = Q_PER_KV * hk + g
                    ga = z_ref[:, _head_cols(GA0, h)]
                    sg = _sigmoid(ga)
                    datt = dc_ref[:, _head_cols(0, h)]
                    dos.append(datt * (ga * sg))
                    pend[:, _head_cols(GA0, h)] = (datt * o_ref[:, _head_cols(0, h)] * (sg * (1.0 + ga * (1.0 - sg)))).astype(BF16)
                do = jnp.concatenate(dos, axis=0).astype(BF16)
                dp = _dot(vh, do, NT)
                delta = jnp.sum(p * dp, axis=0, keepdims=True)
                ds = (p * (dp - delta)).astype(BF16)
                dsk = -p_sink * delta
                dq_t = _dot(kh, ds, TN) * ATTN_SCALE
                for g in range(Q_PER_KV):
                    h = Q_PER_KV * hk + g
                    dsink_acc[:, h:h + 1] += jnp.sum(dsk[:, BLOCK * g:BLOCK * (g + 1)], axis=1, keepdims=True)
                    pend[:, _head_cols(Q0, h)] = dq_t[:, BLOCK * g:BLOCK * (g + 1)].T.astype(BF16)
                dkv[:, kv] = _dot(ds, qs, NN)
                dkv[:, D_KV + HEAD_DIM * hk:D_KV + HEAD_DIM * (hk + 1)] = _dot(p.astype(BF16), do, NN)

            xhat, rstd = _layer_norm_parts(z_ref[:, VS0:VS0 + D_SGU])
            lg = lg_ref[...]
            vn = (xhat * lg + lb_ref[...]).astype(BF16)
            ws, mixed = _sgu_mixed(w_ref, bt_ref, vn, tril)
            u = z_ref[:, U0:U0 + D_SGU]
            gt = z_ref[:, GS0:GS0 + D_SGU]
            sg = _sigmoid(gt)
            dsgu = dc_ref[:, D_ATTN:D_MODEL]
            dgated = dsgu * (gt * sg)
            pend[:, U0:U0 + D_SGU] = (dgated * mixed).astype(BF16)
            pend[:, GS0:GS0 + D_SGU] = (dsgu * (u * mixed) * (sg * (1.0 + gt * (1.0 - sg)))).astype(BF16)
            dmixed = dgated * u
            dmb = dmixed.astype(BF16)
            dvn_parts = []
            for g in range(SGU_GROUPS):
                gs_ = slice(GROUP_DIM * g, GROUP_DIM * (g + 1))
                dw_ref[g] += jnp.where(tril, _dot(dmb[:, gs_], vn[:, gs_], NT), 0.0)
                dbt_acc[:, g:g + 1] += jnp.sum(dmixed[:, gs_], axis=-1, keepdims=True)
                dvn_parts.append(_dot(ws[g], dmb[:, gs_], TN))
            dvn = jnp.concatenate(dvn_parts, axis=1)
            dlg_ref[...] += jnp.sum(dvn * xhat, axis=0, keepdims=True)
            dlb_ref[...] += jnp.sum(dvn, axis=0, keepdims=True)
            dxh = dvn * lg
            pend[:, VS0:VS0 + D_SGU] = (rstd * (dxh - jnp.mean(dxh, axis=-1, keepdims=True)
                                                - xhat * jnp.mean(dxh * xhat, axis=-1, keepdims=True))).astype(BF16)

        dz_ref[:, K0:GA0] = (carry[...] + dkv[0:BLOCK, :]).astype(BF16)
        carry[...] = dkv[BLOCK:2 * BLOCK, :]

        @pl.when(n == nb)
        def _():
            dsink_ref[...] = dsink_acc[:, 0:N_Q_HEADS]
            db_ref[...] = dbt_acc[...].T[0:SGU_GROUPS, :]

    cur = lambda n: (jnp.minimum(n, nb - 1), 0)
    const2 = lambda n: (0, 0)
    return pl.pallas_call(
        body, name="mixer_bwd", grid=(nb + 1,),
        out_shape=(jax.ShapeDtypeStruct((s, D_IN), BF16), jax.ShapeDtypeStruct((1, N_Q_HEADS), F32),
                   jax.ShapeDtypeStruct((1, D_SGU), F32), jax.ShapeDtypeStruct((1, D_SGU), F32),
                   jax.ShapeDtypeStruct((SGU_GROUPS, BLOCK, BLOCK), F32), jax.ShapeDtypeStruct((SGU_GROUPS, BLOCK), F32)),
        in_specs=[pl.BlockSpec((BLOCK, D_IN), cur),
                  pl.BlockSpec((BLOCK, 2 * D_KV), lambda n: (jnp.maximum(jnp.minimum(n, nb - 1) - 1, 0), K0 // (2 * D_KV))),
                  pl.BlockSpec((BLOCK, D_MODEL), cur), pl.BlockSpec((BLOCK, D_ATTN), cur),
                  pl.BlockSpec((1, N_Q_HEADS), const2), pl.BlockSpec((1, D_SGU), const2), pl.BlockSpec((1, D_SGU), const2),
                  pl.BlockSpec((SGU_GROUPS, BLOCK, BLOCK), lambda n: (0, 0, 0)), pl.BlockSpec((BLOCK, SGU_GROUPS), const2)],
        out_specs=(pl.BlockSpec((BLOCK, D_IN), lambda n: (jnp.maximum(n - 1, 0), 0)),
                   pl.BlockSpec((1, N_Q_HEADS), const2), pl.BlockSpec((1, D_SGU), const2), pl.BlockSpec((1, D_SGU), const2),
                   pl.BlockSpec((SGU_GROUPS, BLOCK, BLOCK), lambda n: (0, 0, 0)), pl.BlockSpec((SGU_GROUPS, BLOCK), const2)),
        scratch_shapes=[pltpu.VMEM((BLOCK, D_IN), BF16), pltpu.VMEM((2 * BLOCK, 2 * D_KV), F32), pltpu.VMEM((BLOCK, 2 * D_KV), F32),
                        pltpu.VMEM((1, BLOCK), F32), pltpu.VMEM((BLOCK, BLOCK), F32)],
        compiler_params=_params(("arbitrary",)),
    )(z, z, dcat, o, sinks, ln_g, ln_b, sgu_w, sgu_bt)


def _in_proj_bwd(dz, win_t, x, dx1, norm_g, mod):
    s, d = x.shape
    kdim = dz.shape[1]
    tm = min(256, s)

    def body(dz_ref, w_ref, x_ref, dx1_ref, g_ref, mod_ref, gx_ref, dshift_ref, dscale_ref, dng_ref):
        @pl.when(pl.program_id(0) == 0)
        def _():
            dshift_ref[...] = jnp.zeros_like(dshift_ref)
            dscale_ref[...] = jnp.zeros_like(dscale_ref)
            dng_ref[...] = jnp.zeros_like(dng_ref)

        dh = _dot(dz_ref[...], w_ref[...], NN)
        xv = x_ref[...]
        ng = g_ref[...]
        r = lax.rsqrt(jnp.mean(xv * xv, axis=-1, keepdims=True) + EPS)
        xn = xv * r
        dshift_ref[...] += jnp.sum(dh, axis=0, keepdims=True)
        dscale_ref[...] += jnp.sum(dh * (xn * ng), axis=0, keepdims=True)
        dhs = dh * (1.0 + mod_ref[:, d:2 * d])
        dng_ref[...] += jnp.sum(dhs * xn, axis=0, keepdims=True)
        dxn = dhs * ng
        gx_ref[...] = dx1_ref[...] + r * dxn - xn * (r * r * jnp.mean(dxn * xv, axis=-1, keepdims=True))

    row = lambda i: (i, 0)
    const = lambda i: (0, 0)
    return pl.pallas_call(
        body, name="in_proj_bwd", grid=(s // tm,),
        out_shape=(jax.ShapeDtypeStruct((s, d), F32),) + (jax.ShapeDtypeStruct((1, d), F32),) * 3,
        in_specs=[pl.BlockSpec((tm, kdim), row), pl.BlockSpec((kdim, d), const, pipeline_mode=pl.Buffered(1)),
                  pl.BlockSpec((tm, d), row), pl.BlockSpec((tm, d), row), pl.BlockSpec((1, d), const), pl.BlockSpec((1, 3 * d), const)],
        out_specs=(pl.BlockSpec((tm, d), row),) + (pl.BlockSpec((1, d), const),) * 3,
        compiler_params=_params(("arbitrary",)),
    )(dz, win_t, x, dx1, norm_g, mod)


def _sum_slots(slots, own, place, rows_tile, full_cols, name):
    n_slots, r, hcols = slots.shape
    nblk = r // rows_tile
    per_device = n_slots == N_DEV

    def body(place_ref, s_ref, own_ref, o_ref):
        mine = 2 * place_ref[1] + place_ref[0] if per_device else place_ref[1]
        acc = jnp.zeros((rows_tile, hcols), F32)
        for k in range(n_slots):
            acc = acc + jnp.where(mine == k, own_ref[...], s_ref[k]).astype(F32)
        o_ref[...] = acc

    return pl.pallas_call(
        body, name=name,
        grid_spec=pltpu.PrefetchScalarGridSpec(
            num_scalar_prefetch=1, grid=(nblk,),
            in_specs=[pl.BlockSpec((n_slots, rows_tile, hcols), lambda i, p: (0, i, 0)),
                      pl.BlockSpec((rows_tile, hcols), lambda i, p: (p[1] * nblk + i, p[0] if per_device else 0))],
            out_specs=pl.BlockSpec((rows_tile, hcols), lambda i, p: (i, p[0]))),
        out_shape=jax.ShapeDtypeStruct((r, full_cols), F32),
        compiler_params=_params(("parallel",)),
    )(place, slots, own)


def _adamw_rows(w, g, m, v, tile, name, after):
    r, c = w.shape

    def body(w_ref, g_ref, m_ref, v_ref, after_ref, go_ref, d_ref, mo_ref, vo_ref):
        g_ = g_ref[...]
        delta, m2, v2 = _adamw(w_ref[...], g_, m_ref[...], v_ref[...])
        go_ref[...] = g_
        d_ref[...] = delta
        mo_ref[...] = m2
        vo_ref[...] = v2

    blk = pl.BlockSpec((tile, c), lambda i: (i, 0))
    return pl.pallas_call(
        body, name=name, grid=(r // tile,), out_shape=(jax.ShapeDtypeStruct((r, c), F32),) * 4,
        in_specs=[blk] * 4 + [ANY], out_specs=(blk,) * 4,
        compiler_params=_params(("parallel",)),
    )(w, g, m, v, after)


def _adamw_ada(cact, dmod, w, m, v, tile=256):
    r, c = w.shape

    def body(c_ref, d_ref, w_ref, m_ref, v_ref, go_ref, dl_ref, mo_ref, vo_ref):
        g_ = _dot(c_ref[...].astype(BF16), d_ref[...].astype(BF16), TN)
        delta, m2, v2 = _adamw(w_ref[...], g_, m_ref[...], v_ref[...])
        go_ref[...] = g_
        dl_ref[...] = delta
        mo_ref[...] = m2
        vo_ref[...] = v2

    blk = pl.BlockSpec((tile, c), lambda i: (i, 0))
    return pl.pallas_call(
        body, name="adamw_ada", grid=(r // tile,), out_shape=(jax.ShapeDtypeStruct((r, c), F32),) * 4,
        in_specs=[pl.BlockSpec((N_DEV, tile), lambda i: (0, i)), pl.BlockSpec((N_DEV, c), lambda i: (0, 0)), blk, blk, blk],
        out_specs=(blk,) * 4,
        compiler_params=_params(("parallel",)),
    )(cact, dmod, w, m, v)


def kernel(x, c, norm_g, w_ada, b_ada, w_in, attn_sinks, sgu_ln_g, sgu_ln_b, sgu_w, sgu_b, w_out, final_g, loss_target, m_norm_g, m_w_ada, m_b_ada, m_w_in, m_attn_sinks, m_sgu_ln_g, m_sgu_ln_b, m_sgu_w, m_sgu_b, m_w_out, m_final_g, v_norm_g, v_w_ada, v_b_ada, v_w_in, v_attn_sinks, v_sgu_ln_g, v_sgu_ln_b, v_sgu_w, v_sgu_b, v_w_out, v_final_g):
    xs, tgt = x[0], loss_target[0]
    core = lax.axis_index("c").astype(jnp.int32).reshape(1)
    chip = 2 * lax.axis_index("x") + lax.axis_index("y")
    place = jnp.stack([core[0], chip.astype(jnp.int32)])
    sgu_bt = sgu_b[0].T

    w_in_t, m_in_t, v_in_t = (jnp.swapaxes(a[0], 0, 1) for a in (w_in, m_w_in, v_w_in))
    win_t = _cast_into_gathered(w_in_t, IN_SHARD // 4, place, "cast_w_in", c)
    win_t, mod, cact = _gather_w_in_and_modulate(win_t, c, w_ada[0], b_ada)
    wout = _cast_into_gathered(w_out[0], OUT_SHARD // 2, place, "cast_w_out", win_t)
    sems, bufs, token = _split_start(_gather_copies, [wout], 6, "w_out_gather_start", win_t)
    z, h = _in_proj(xs, norm_g, mod, win_t, token)
    cat, o = _mixer_fwd(z, attn_sinks, sgu_ln_g, sgu_ln_b, sgu_w[0], sgu_bt)
    wout = _split_wait(_gather_copies, sems, bufs, cat, "w_out_gather_wait")[0]
    dx1, dy, dfg, dgate, loss = _out_proj_loss(cat, wout, xs, tgt, mod, final_g.reshape(1, D_MODEL))

    po = _matmul_tn(cat, dy, 512, "w_out_grad")
    sems, bufs, token = _split_start(_owner_copies, [po, lax.empty((N_DEV, OUT_SHARD, HALF), BF16)], N_DEV - 1,
                                     "w_out_grad_to_owners_start", cat)
    dcat = _matmul_nt(dy, wout, 512, 1024, F32, "out_proj_bwd", token)
    dz, dsinks, dlg, dlb, dsgu_w, dsgu_b = _mixer_bwd(z, dcat, o, attn_sinks, sgu_ln_g, sgu_ln_b, sgu_w[0], sgu_bt)
    po, slots_o = _split_wait(_owner_copies, sems, bufs, dz, "w_out_grad_to_owners_wait")
    go = _halves_swap(_sum_slots(slots_o, po, place, OUT_SHARD // 2, D_MODEL, "w_out_grad_sum"), "w_out_grad_halves_swap")

    sw = _to_sibling(_grad_matmul(dz, h, 1 - core, 768, "w_in_grad_sibling"), "w_in_grad_to_sibling")
    pw = _grad_matmul(dz, h, core, 768, "w_in_grad_own", add=sw)
    first = functools.partial(_chip_copies, [(0, 0, IN_SHARD), (1, 0, IN_SHARD), (2, 0, IN_SHARD // 4)])
    rest = functools.partial(_chip_copies, [(2, IN_SHARD // 4, 3 * IN_SHARD // 4)])
    sems, bufs, token = _split_start(first, [pw, lax.empty((N_CHIPS, IN_SHARD, HALF), BF16)], 3, "w_in_grad_to_chips_start", sw)
    gx, dshift, dscale, dng = _in_proj_bwd(dz, win_t, xs, dx1, norm_g + token[0:1, 0:1], mod)
    bufs = _split_wait(first, sems, bufs, gx, "w_in_grad_to_chips_wait")

    parts = dict(dng=dng, dshift=dshift, dscale=dscale, dgate=dgate, dfg=dfg, dlg=dlg, dlb=dlb, dsinks=dsinks,
                 dsgu_b=dsgu_b, dsgu_w=dsgu_w, loss=loss)
    small = dict(norm_g=(norm_g, m_norm_g, v_norm_g), b_ada=(b_ada, m_b_ada, v_b_ada),
                 attn_sinks=(attn_sinks, m_attn_sinks, v_attn_sinks), sgu_ln_g=(sgu_ln_g, m_sgu_ln_g, v_sgu_ln_g),
                 sgu_ln_b=(sgu_ln_b, m_sgu_ln_b, v_sgu_ln_b), sgu_w=(sgu_w, m_sgu_w, v_sgu_w), sgu_b=(sgu_b, m_sgu_b, v_sgu_b),
                 final_g=(final_g, m_final_g, v_final_g))
    tot, gsgu_w, dmod_mine = _small_grads_sum(parts, bufs[1])
    sems, bufs, token = _split_start(rest, list(bufs), 1, "w_in_grad_rest_to_chips_start", tot)
    res, total = _small_adamw(tot, gsgu_w, small)
    res["w_ada"] = _adamw_ada(cact, dmod_mine + token[0:1, 0:1], w_ada[0], m_w_ada[0], v_w_ada[0])
    res["w_out"] = _adamw_rows(w_out[0], go, m_w_out[0], v_w_out[0], 256, "adamw_w_out", res["w_ada"][1])

    pw, slots_w = _split_wait(rest, sems, bufs, res["w_out"][1], "w_in_grad_rest_to_chips_wait")
    gw_t = _halves_swap(_sum_slots(slots_w, pw, place, IN_SHARD // 4, D_MODEL, "w_in_grad_sum"), "w_in_grad_halves_swap")
    res["w_in"] = tuple(jnp.swapaxes(a, 0, 1) for a in _adamw_rows(w_in_t, gw_t, m_in_t, v_in_t, IN_SHARD // 4, "adamw_w_in", gw_t))

    order = ["norm_g", "w_ada", "b_ada", "w_in", "attn_sinks", "sgu_ln_g", "sgu_ln_b", "sgu_w", "sgu_b", "w_out", "final_g"]
    like = dict(norm_g=norm_g, w_ada=w_ada, b_ada=b_ada, w_in=w_in, attn_sinks=attn_sinks, sgu_ln_g=sgu_ln_g, sgu_ln_b=sgu_ln_b,
                sgu_w=sgu_w, sgu_b=sgu_b, w_out=w_out, final_g=final_g)
    outs = [total[0, 0], gx[None]]
    for kind in range(4):
        outs += [res[n][kind].reshape(like[n].shape) for n in order]
    return tuple(outs)
```

```python
import functools
import math

import jax
import jax.numpy as jnp
from jax import lax
from jax.experimental import pallas as pl
from jax.experimental.pallas import tpu as pltpu

F32 = jnp.float32
BF16 = jnp.bfloat16

D_MODEL = 2048
D_ATTN = 1024
HEAD_DIM = 64
N_Q_HEADS = 16
Q_PER_KV = 8
D_KV = 128
BLOCK = 128
D_SGU = 1024
SGU_GROUPS = 8
GROUP_DIM = 128
D_IN = 5376
EPS = 1e-6
ATTN_SCALE = 1.0 / math.sqrt(HEAD_DIM)
N_CHIPS = 4
N_DEV = 8
IN_SHARD = D_IN // N_CHIPS
OUT_SHARD = D_MODEL // N_CHIPS
ADA_SHARD = 3 * D_MODEL // N_CHIPS
HALF = D_MODEL // 2

Q0, K0, V0, GA0, U0, VS0, GS0 = 0, 1024, 1152, 1280, 2304, 3328, 4352

ADAM_LR = 0.001
ADAM_B1 = 0.9
ADAM_B2 = 0.999
ADAM_EPS = 1e-08
ADAM_WD = 0.01
ADAM_STEP = 10

VMEM_LIMIT_V7X = 56 * 1024 * 1024

NN = (((1,), (0,)), ((), ()))
NT = (((1,), (1,)), ((), ()))
TN = (((0,), (0,)), ((), ()))
MESH = pl.DeviceIdType.MESH
ANY = pl.BlockSpec(memory_space=pl.ANY)
VMEM = pl.BlockSpec(memory_space=pltpu.VMEM)
HBM = pl.BlockSpec(memory_space=pltpu.HBM)
SEM = pl.BlockSpec(memory_space=pltpu.SEMAPHORE)
EFFECT = pltpu.SideEffectType.DATAFLOW_SIDE_EFFECTING


def _dot(a, b, dims):
    return lax.dot_general(a, b, dims, preferred_element_type=F32)


def _params(sem=None):
    return pltpu.CompilerParams(dimension_semantics=sem, vmem_limit_bytes=VMEM_LIMIT_V7X)


def _sigmoid(x):
    return jax.nn.sigmoid(x)


def _adamw(w, g, m, v):
    m = ADAM_B1 * m + (1.0 - ADAM_B1) * g
    v = ADAM_B2 * v + (1.0 - ADAM_B2) * (g * g)
    m_hat = m / (1.0 - ADAM_B1 ** ADAM_STEP)
    v_hat = v / (1.0 - ADAM_B2 ** ADAM_STEP)
    delta = -ADAM_LR * (m_hat / (jnp.sqrt(v_hat) + ADAM_EPS) + ADAM_WD * w)
    return delta, m, v


def _place():
    return lax.axis_index("x"), lax.axis_index("y"), lax.axis_index("c")


def _flip(p, bit):
    return 1 - p if bit else p


def _other_chips(x, y):
    return [(_flip(x, bx), _flip(y, by)) for bx, by in ((0, 1), (1, 0), (1, 1))]


def _split_start(copies, bufs, n_sems, name, after):
    n = len(bufs)

    def body(*refs):
        send_sems, recv_sems = refs[n + 1:n + 3]
        for cp in copies(*refs[:n], send_sems, recv_sems):
            cp.start()
        refs[-1][...] = jnp.zeros_like(refs[-1])

    bufs = [pltpu.with_memory_space_constraint(a, pltpu.HBM) for a in bufs]
    outs = pl.pallas_call(
        body, name=name,
        out_shape=(pltpu.SemaphoreType.DMA((n_sems,)), pltpu.SemaphoreType.DMA((n_sems,)))
        + tuple(pltpu.HBM(a.shape, a.dtype) for a in bufs) + (jax.ShapeDtypeStruct((8, 128), F32),),
        in_specs=[HBM] * n + [ANY], out_specs=(SEM, SEM) + (HBM,) * n + (VMEM,), input_output_aliases={k: k + 2 for k in range(n)},
        compiler_params=pltpu.CompilerParams(has_side_effects=EFFECT),
    )(*bufs, after)
    return outs[:2], list(outs[2:2 + n]), outs[-1]


def _split_wait(copies, sems, bufs, after, name):
    n = len(bufs)

    def body(*refs):
        send_sems, recv_sems = refs[n:n + 2]
        for cp in copies(*refs[:n], send_sems, recv_sems):
            cp.wait_send()
            cp.wait_recv()

    return pl.pallas_call(
        body, name=name, out_shape=tuple(pltpu.HBM(a.shape, a.dtype) for a in bufs),
        in_specs=[HBM] * n + [SEM, SEM, ANY], out_specs=(HBM,) * n, input_output_aliases={k: k for k in range(n)},
        compiler_params=pltpu.CompilerParams(has_side_effects=EFFECT),
    )(*bufs, *sems, after)


def _gather_copies(full_ref, send_sems, recv_sems):
    x, y, cc = _place()
    shard = full_ref.shape[0] // N_CHIPS
    half = shard // 2
    mine = full_ref.at[pl.ds(shard * (2 * x + y) + half * cc, half), :]
    return [pltpu.make_async_remote_copy(src_ref=mine, dst_ref=mine, send_sem=send_sems.at[2 * j + pc],
                                         recv_sem=recv_sems.at[2 * j + _flip(cc, pc)], device_id=(px, py, pc), device_id_type=MESH)
            for j, (px, py) in enumerate(_other_chips(x, y)) for pc in range(2)]


def _gather_w_in_and_modulate(full, c, w_ada, b_ada):
    shard = full.shape[0] // N_CHIPS
    half = shard // 2
    quarter = half // 2
    kdim = w_ada.shape[0]
    kc = 256

    def body(full_in, c_ref, w_hbm, b_ref, full_ref, mod_ref, cact_ref, w_vmem, cbuf, pbuf, rbuf, w_sem,
             send_sems, recv_sems, msend_sems, mrecv_sems):
        x, y, cc = _place()
        me = 4 * x + 2 * y + cc
        chip = 2 * x + y
        across_x, across_y, far = (1 - x, y), (x, 1 - y), (1 - x, 1 - y)

        def rows(chip_xy, core, q):
            return full_ref.at[pl.ds(shard * (2 * chip_xy[0] + chip_xy[1]) + half * core + quarter * q, quarter), :]

        def copy(k, piece, to):
            return pltpu.make_async_remote_copy(src_ref=piece, dst_ref=piece, send_sem=send_sems.at[k], recv_sem=recv_sems.at[k],
                                                device_id=to, device_id_type=MESH)

        def small(k, src, dst, to):
            return pltpu.make_async_remote_copy(src_ref=src, dst_ref=dst, send_sem=msend_sems.at[k], recv_sem=mrecv_sems.at[k],
                                                device_id=to, device_id_type=MESH)

        cbuf[me] = c_ref[...]
        gathers = [small(k - 1, cbuf.at[me], cbuf.at[me], (_flip(x, (k >> 2) & 1), _flip(y, (k >> 1) & 1), _flip(cc, k & 1)))
                   for k in range(1, N_DEV)]
        for cp in gathers:
            cp.start()
        w_load = pltpu.make_async_copy(w_hbm, w_vmem, w_sem)
        w_load.start()
        direct = [copy(0, rows((x, y), cc, 0), (*across_x, cc)), copy(2, rows((x, y), cc, 1), (*across_y, cc)),
                  copy(1, rows((x, y), cc, 1), (*across_x, cc)), copy(3, rows((x, y), cc, 0), (*across_y, cc))]
        for cp in direct:
            cp.start()

        for cp in gathers:
            cp.wait()
        w_load.wait()
        for b in range(N_DEV):
            cv = cbuf[b]
            cact_ref[b:b + 1, :] = cv * _sigmoid(cv)
        acc = jnp.zeros((N_DEV, ADA_SHARD), F32)
        for k0 in range(0, kdim, kc):
            acc = acc + _dot(cact_ref[:, k0:k0 + kc].astype(BF16), w_vmem[k0:k0 + kc, :].astype(BF16), NN)
        for b in range(N_DEV):
            pbuf[b] = acc[b:b + 1, :]
        rbuf[chip] = pbuf[me]
        sends = [small(N_DEV - 1 + j, pbuf.at[4 * px + 2 * py + cc], rbuf.at[chip], (px, py, cc))
                 for j, (px, py) in enumerate(_other_chips(x, y))]
        for cp in sends:
            cp.start()

        landing = [(0, across_x, 0, (4, (*across_y, cc))), (2, across_y, 1, (5, (*across_x, cc))),
                   (1, across_x, 1, None), (3, across_y, 0, None), (4, far, 0, None), (5, far, 1, None)]
        passed = []
        for n, (k, whose, q, onward) in enumerate(landing):
            piece = rows(whose, cc, q)
            copy(k, piece, (x, y, cc)).wait_recv()
            if onward is not None:
                passed.append(copy(onward[0], piece, onward[1]))
                passed[-1].start()
            passed.append(copy(6 + n, piece, (x, y, 1 - cc)))
            passed[-1].start()
        for n, (k, whose, q, onward) in enumerate(landing):
            copy(6 + n, rows(whose, 1 - cc, q), (x, y, cc)).wait_recv()
        for cp in sends:
            cp.wait()
        for k in range(N_CHIPS):
            cols = slice(ADA_SHARD * k, ADA_SHARD * (k + 1))
            mod_ref[:, cols] = rbuf[k] + b_ref[:, cols]
        for cp in direct + passed:
            cp.wait_send()

    return pl.pallas_call(
        body, name="w_in_gather_and_modulate",
        out_shape=(jax.ShapeDtypeStruct(full.shape, full.dtype), jax.ShapeDtypeStruct((1, 3 * D_MODEL), F32),
                   jax.ShapeDtypeStruct((N_DEV, D_MODEL), F32)),
        in_specs=[ANY, VMEM, ANY, VMEM], out_specs=(ANY, VMEM, VMEM), input_output_aliases={0: 0},
        scratch_shapes=[pltpu.VMEM(w_ada.shape, F32), pltpu.VMEM((N_DEV, 1, D_MODEL), F32), pltpu.VMEM((N_DEV, 1, ADA_SHARD), F32),
                        pltpu.VMEM((N_CHIPS, 1, ADA_SHARD), F32), pltpu.SemaphoreType.DMA,
                        pltpu.SemaphoreType.DMA((12,)), pltpu.SemaphoreType.DMA((12,)),
                        pltpu.SemaphoreType.DMA((N_DEV + 2,)), pltpu.SemaphoreType.DMA((N_DEV + 2,))],
        compiler_params=_params(),
    )(full, c, w_ada, b_ada)


def _to_sibling(g, name):
    def body(g_ref, r_ref, send_sem, recv_sem):
        x, y, cc = _place()
        cp = pltpu.make_async_remote_copy(src_ref=g_ref, dst_ref=r_ref, send_sem=send_sem, recv_sem=recv_sem,
                                          device_id=(x, y, 1 - cc), device_id_type=MESH)
        cp.start()
        cp.wait()

    return pl.pallas_call(
        body, name=name, out_shape=jax.ShapeDtypeStruct(g.shape, g.dtype), in_specs=[ANY], out_specs=ANY,
        scratch_shapes=[pltpu.SemaphoreType.DMA, pltpu.SemaphoreType.DMA], compiler_params=_params(),
    )(g)


def _chip_copies(pieces, p_ref, land_ref, send_sems, recv_sems):
    x, y, cc = _place()
    chip = 2 * x + y
    shard = land_ref.shape[1]
    others = _other_chips(x, y)
    cps = []
    for k, (j, r0, nr) in enumerate(pieces):
        px, py = others[j]
        cps.append(pltpu.make_async_remote_copy(
            src_ref=p_ref.at[pl.ds(shard * (2 * px + py) + r0, nr), :], dst_ref=land_ref.at[chip, pl.ds(r0, nr), :],
            send_sem=send_sems.at[k], recv_sem=recv_sems.at[k], device_id=(px, py, cc), device_id_type=MESH))
    return cps


def _owner_copies(p_ref, land_ref, send_sems, recv_sems):
    x, y, cc = _place()
    me = 4 * x + 2 * y + cc
    shard = land_ref.shape[1]
    cps = []
    for k in range(1, N_DEV):
        px, py, pc = _flip(x, (k >> 2) & 1), _flip(y, (k >> 1) & 1), _flip(cc, k & 1)
        block = p_ref.at[pl.ds(shard * (2 * px + py), shard), pl.ds(pl.multiple_of(HALF * pc, 128), HALF)]
        cps.append(pltpu.make_async_remote_copy(src_ref=block, dst_ref=land_ref.at[me], send_sem=send_sems.at[k - 1],
                                                recv_sem=recv_sems.at[k - 1], device_id=(px, py, pc), device_id_type=MESH))
    return cps


def _halves_swap(f, name):
    def body(f_in, f_ref, send_sem, recv_sem):
        x, y, cc = _place()
        mine = f_ref.at[:, pl.ds(pl.multiple_of(HALF * cc, 128), HALF)]
        cp = pltpu.make_async_remote_copy(src_ref=mine, dst_ref=mine, send_sem=send_sem, recv_sem=recv_sem,
                                          device_id=(x, y, 1 - cc), device_id_type=MESH)
        cp.start()
        cp.wait()

    return pl.pallas_call(
        body, name=name, out_shape=jax.ShapeDtypeStruct(f.shape, f.dtype),
        in_specs=[ANY], out_specs=ANY, input_output_aliases={0: 0},
        scratch_shapes=[pltpu.SemaphoreType.DMA, pltpu.SemaphoreType.DMA], compiler_params=_params(),
    )(f)


ROW_NORM_G, ROW_DMOD, ROW_FINAL_G, ROW_LN, ROW_SINKS, ROW_LOSS, ROW_SGU_B, MISC_ROWS = 0, 1, 4, 5, 6, 7, 8, 16


def _small_grads_sum(parts, after):
    part_names = ["dng", "dshift", "dscale", "dgate", "dfg", "dlg", "dlb", "dsinks", "dsgu_b", "dsgu_w", "loss"]
    n_parts = len(part_names)
    n_peers = N_DEV - 1

    def body(*refs):
        p = dict(zip(part_names, refs[:n_parts]))
        tot_ref, gw_ref, dmod_ref = refs[n_parts + 1:n_parts + 4]
        misc, cmisc, dsend, dbuf, wland, wsum, own_sem, send_sems, recv_sems = refs[n_parts + 4:]
        x, y, cc = _place()
        me = 4 * x + 2 * y + cc
        chip = 2 * x + y
        peers = [(_flip(x, (k >> 2) & 1), _flip(y, (k >> 1) & 1), _flip(cc, k & 1)) for k in range(1, N_DEV)]

        def remote(k, src, dst, peer):
            return pltpu.make_async_remote_copy(src_ref=src, dst_ref=dst, send_sem=send_sems.at[k], recv_sem=recv_sems.at[k],
                                                device_id=peer, device_id_type=MESH)

        dsend[:, 0:D_MODEL] = p["dshift"][...]
        dsend[:, D_MODEL:2 * D_MODEL] = p["dscale"][...]
        dsend[:, 2 * D_MODEL:3 * D_MODEL] = p["dgate"][...]

        def shard_cols(chip_index):
            return dsend.at[:, pl.ds(pl.multiple_of(ADA_SHARD * chip_index, 128), ADA_SHARD)]

        own_cols = pltpu.make_async_copy(shard_cols(chip), dbuf.at[me], own_sem)
        own_cols.start()
        gathers = [remote(k, shard_cols(2 * px + py), dbuf.at[me], (px, py, pc)) for k, (px, py, pc) in enumerate(peers)]
        wland[me] = p["dsgu_w"][me]
        scatters = [remote(n_peers + k, p["dsgu_w"].at[4 * px + 2 * py + pc], wland.at[me], (px, py, pc))
                    for k, (px, py, pc) in enumerate(peers)]

        misc[cc] = jnp.zeros((MISC_ROWS, D_MODEL), F32)
        misc[cc, ROW_NORM_G:ROW_NORM_G + 1, :] = p["dng"][...]
        misc[cc, ROW_DMOD:ROW_DMOD + 1, :] = p["dshift"][...]
        misc[cc, ROW_DMOD + 1:ROW_DMOD + 2, :] = p["dscale"][...]
        misc[cc, ROW_DMOD + 2:ROW_DMOD + 3, :] = p["dgate"][...]
        misc[cc, ROW_FINAL_G:ROW_FINAL_G + 1, :] = p["dfg"][...]
        misc[cc, ROW_LN:ROW_LN + 1, 0:D_SGU] = p["dlg"][...]
        misc[cc, ROW_LN:ROW_LN + 1, D_SGU:2 * D_SGU] = p["dlb"][...]
        misc[cc, ROW_SINKS:ROW_SINKS + 1, 0:N_Q_HEADS] = p["dsinks"][...]
        misc[cc, ROW_LOSS:ROW_LOSS + 1, 0:1] = p["loss"][...]
        misc[cc, ROW_SGU_B:ROW_SGU_B + SGU_GROUPS, 0:BLOCK] = p["dsgu_b"][...]
        pair = remote(3 * n_peers, misc.at[cc], misc.at[cc], (x, y, 1 - cc))
        for cp in gathers + scatters + [pair]:
            cp.start()
        pair.wait()
        cmisc[chip] = misc[0] + misc[1]
        quad = [remote(3 * n_peers + 1 + j, cmisc.at[chip], cmisc.at[chip], (_flip(x, bx), _flip(y, by), cc))
                for j, (bx, by) in enumerate(((0, 1), (1, 0), (1, 1)))]
        for cp in quad:
            cp.start()
        for cp in scatters:
            cp.wait()
        group = wland[0]
        for b in range(1, N_DEV):
            group = group + wland[b]
        wsum[me] = group
        spreads = [remote(2 * n_peers + k, wsum.at[me], wsum.at[me], peer) for k, peer in enumerate(peers)]
        for cp in spreads:
            cp.start()
        for cp in gathers + quad + spreads:
            cp.wait()

        tot = cmisc[0]
        for k in range(1, N_CHIPS):
            tot = tot + cmisc[k]
        tot_ref[...] = tot
        gw_ref[...] = wsum[...]
        own_cols.wait()
        for b in range(N_DEV):
            dmod_ref[b:b + 1, :] = dbuf[b]

    operands = [parts[n] for n in part_names]
    n_sems = 3 * n_peers + 1 + (N_CHIPS - 1)
    return pl.pallas_call(
        body, name="small_grads_sum",
        out_shape=(jax.ShapeDtypeStruct((MISC_ROWS, D_MODEL), F32), jax.ShapeDtypeStruct((SGU_GROUPS, BLOCK, BLOCK), F32),
                   jax.ShapeDtypeStruct((N_DEV, ADA_SHARD), F32)),
        in_specs=[VMEM] * len(operands) + [ANY], out_specs=(VMEM, VMEM, VMEM),
        scratch_shapes=[pltpu.VMEM((2, MISC_ROWS, D_MODEL), F32), pltpu.VMEM((N_CHIPS, MISC_ROWS, D_MODEL), F32),
                        pltpu.VMEM((1, 3 * D_MODEL), F32), pltpu.VMEM((N_DEV, 1, ADA_SHARD), F32),
                        pltpu.VMEM((N_DEV, BLOCK, BLOCK), F32), pltpu.VMEM((SGU_GROUPS, BLOCK, BLOCK), F32),
                        pltpu.SemaphoreType.DMA, pltpu.SemaphoreType.DMA((n_sems,)), pltpu.SemaphoreType.DMA((n_sems,))],
        compiler_params=_params(),
    )(*operands, after)


def _small_adamw(tot, gw, small):
    names = ["norm_g", "b_ada", "attn_sinks", "sgu_ln_g", "sgu_ln_b", "sgu_w", "sgu_b", "final_g"]

    def body(*refs):
        tot_ref, gw_ref = refs[:2]
        wmv = {n: refs[2 + 3 * i: 5 + 3 * i] for i, n in enumerate(names)}
        outs = refs[2 + 3 * len(names):]
        res = {n: outs[4 * i: 4 * i + 4] for i, n in enumerate(names)}
        outs[4 * len(names)][...] = tot_ref[ROW_LOSS:ROW_LOSS + 1, 0:1]

        def update(name, g, pick=lambda r: r[...], put=None):
            w_ref, m_ref, v_ref = wmv[name]
            delta, m2, v2 = _adamw(pick(w_ref), g, pick(m_ref), pick(v_ref))
            for ref, val in zip(res[name], (g, delta, m2, v2)):
                if put is None:
                    ref[...] = val
                else:
                    put(ref, val)

        update("norm_g", tot_ref[ROW_NORM_G:ROW_NORM_G + 1, :])
        update("final_g", tot_ref[ROW_FINAL_G:ROW_FINAL_G + 1, :])
        for t in range(3):
            cols = slice(D_MODEL * t, D_MODEL * (t + 1))

            def put_cols(ref, val, cols=cols):
                ref[:, cols] = val
            update("b_ada", tot_ref[ROW_DMOD + t:ROW_DMOD + t + 1, :], pick=lambda r, cols=cols: r[:, cols], put=put_cols)
        update("sgu_ln_g", tot_ref[ROW_LN:ROW_LN + 1, 0:D_SGU])
        update("sgu_ln_b", tot_ref[ROW_LN:ROW_LN + 1, D_SGU:2 * D_SGU])
        update("attn_sinks", tot_ref[ROW_SINKS:ROW_SINKS + 1, 0:N_Q_HEADS])
        update("sgu_b", tot_ref[ROW_SGU_B:ROW_SGU_B + SGU_GROUPS, 0:BLOCK])
        for g in range(SGU_GROUPS):
            def put_group(ref, val, g=g):
                ref[g] = val
            update("sgu_w", gw_ref[g], pick=lambda r, g=g: r[g], put=put_group)

    shapes = {"norm_g": (1, D_MODEL), "b_ada": (1, 3 * D_MODEL), "attn_sinks": (1, N_Q_HEADS), "sgu_ln_g": (1, D_SGU),
              "sgu_ln_b": (1, D_SGU), "sgu_w": (SGU_GROUPS, BLOCK, BLOCK), "sgu_b": (SGU_GROUPS, BLOCK), "final_g": (1, D_MODEL)}
    operands = [tot, gw]
    for n in names:
        operands += [a.reshape(shapes[n]) for a in small[n]]
    out_shape = []
    for n in names:
        out_shape += [jax.ShapeDtypeStruct(shapes[n], F32)] * 4
    out_shape.append(jax.ShapeDtypeStruct((1, 1), F32))
    outs = pl.pallas_call(
        body, name="small_adamw", out_shape=tuple(out_shape),
        in_specs=[VMEM] * len(operands), out_specs=tuple([VMEM] * len(out_shape)), compiler_params=_params(),
    )(*operands)
    return {n: outs[4 * i: 4 * i + 4] for i, n in enumerate(names)}, outs[4 * len(names)]


def _cast_into_gathered(w, tile, place, name, after):
    r, c = w.shape
    nblk = r // tile

    def body(place_ref, w_ref, after_ref, o_ref):
        o_ref[...] = w_ref[...].astype(BF16)

    return pl.pallas_call(
        body, name=name,
        grid_spec=pltpu.PrefetchScalarGridSpec(
            num_scalar_prefetch=1, grid=(nblk,), in_specs=[pl.BlockSpec((tile, c), lambda i, p: (i, 0)), ANY],
            out_specs=pl.BlockSpec((tile, c), lambda i, p: (p[1] * nblk + i, 0))),
        out_shape=jax.ShapeDtypeStruct((N_CHIPS * r, c), BF16),
        compiler_params=_params(("parallel",)),
    )(place, w, after)


def _in_proj(x, norm_g, mod, win_t, after):
    s, d = x.shape
    n = win_t.shape[0]
    tm, tn = min(1024, s), 768
    rows_at_once = min(256, tm)

    def body(x_ref, g_ref, mod_ref, w_ref, after_ref, z_ref, h_ref):
        @pl.when(pl.program_id(1) == 0)
        def _():
            for r0 in range(0, tm, rows_at_once):
                rows = slice(r0, r0 + rows_at_once)
                xv = x_ref[rows, :]
                r = lax.rsqrt(jnp.mean(xv * xv, axis=-1, keepdims=True) + EPS)
                h_ref[rows, :] = ((xv * r * g_ref[...]) * (1.0 + mod_ref[:, d:2 * d]) + mod_ref[:, 0:d]).astype(BF16)

        z_ref[...] = _dot(h_ref[...], w_ref[...], NT)

    return pl.pallas_call(
        body, name="in_proj", grid=(s // tm, n // tn),
        out_shape=(jax.ShapeDtypeStruct((s, n), F32), jax.ShapeDtypeStruct((s, d), BF16)),
        in_specs=[pl.BlockSpec((tm, d), lambda i, j: (i, 0)), pl.BlockSpec((1, d), lambda i, j: (0, 0)),
                  pl.BlockSpec((1, 3 * d), lambda i, j: (0, 0)), pl.BlockSpec((tn, d), lambda i, j: (j, 0)), ANY],
        out_specs=(pl.BlockSpec((tm, tn), lambda i, j: (i, j)), pl.BlockSpec((tm, d), lambda i, j: (i, 0))),
        compiler_params=_params(("parallel", "arbitrary")),
    )(x, norm_g, mod, win_t, after)


def _matmul_nt(a, bt, tm, out_dtype, name, after):
    m, k = a.shape
    n = bt.shape[0]
    tm = min(tm, m)

    def body(a_ref, b_ref, after_ref, o_ref):
        o_ref[...] = _dot(a_ref[...], b_ref[...], NT).astype(out_dtype)

    return pl.pallas_call(
        body, name=name, grid=(m // tm,), out_shape=jax.ShapeDtypeStruct((m, n), out_dtype),
        in_specs=[pl.BlockSpec((tm, k), lambda i: (i, 0)), pl.BlockSpec((n, k), lambda i: (0, 0), pipeline_mode=pl.Buffered(1)), ANY],
        out_specs=pl.BlockSpec((tm, n), lambda i: (i, 0)),
        compiler_params=_params(("parallel",)),
    )(a, bt, after)


def _matmul_tn(a, b, tm, name):
    t, m = a.shape
    n = b.shape[1]
    tm = min(tm, m)

    def body(a_ref, b_ref, o_ref):
        o_ref[...] = _dot(a_ref[...], b_ref[...], TN).astype(BF16)

    return pl.pallas_call(
        body, name=name, grid=(m // tm,), out_shape=jax.ShapeDtypeStruct((m, n), BF16),
        in_specs=[pl.BlockSpec((t, tm), lambda i: (0, i)), pl.BlockSpec((t, n), lambda i: (0, 0))],
        out_specs=pl.BlockSpec((tm, n), lambda i: (i, 0)),
        compiler_params=_params(("parallel",)),
    )(a, b)


def _grad_matmul(a, b, half, tm, name, add=None):
    t, m = a.shape
    tm = min(tm, m)

    def body(half_ref, a_ref, b_ref, *rest):
        if add is None:
            (o_ref,) = rest
            o_ref[...] = _dot(a_ref[...], b_ref[...], TN).astype(BF16)
        else:
            add_ref, o_ref = rest
            o_ref[...] = (_dot(a_ref[...], b_ref[...], TN) + add_ref[...].astype(F32)).astype(BF16)

    in_specs = [pl.BlockSpec((t, tm), lambda i, h: (0, i)), pl.BlockSpec((t, HALF), lambda i, h: (0, h[0]))]
    operands = [a, b]
    if add is not None:
        in_specs.append(pl.BlockSpec((tm, HALF), lambda i, h: (i, 0)))
        operands.append(add)
    return pl.pallas_call(
        body, name=name,
        grid_spec=pltpu.PrefetchScalarGridSpec(num_scalar_prefetch=1, grid=(m // tm,), in_specs=in_specs,
                                               out_specs=pl.BlockSpec((tm, HALF), lambda i, h: (i, 0))),
        out_shape=jax.ShapeDtypeStruct((m, HALF), BF16),
        compiler_params=_params(("parallel",)),
    )(half, *operands)


def _mixer_masks(n):
    key = lax.broadcasted_iota(jnp.int32, (2 * BLOCK, Q_PER_KV * BLOCK), 0)
    qry = lax.broadcasted_iota(jnp.int32, (2 * BLOCK, Q_PER_KV * BLOCK), 1) & (BLOCK - 1)
    valid = (key > qry) & (key <= qry + BLOCK) & ((key >= BLOCK) | (n > 0))
    r2 = lax.broadcasted_iota(jnp.int32, (BLOCK, BLOCK), 0)
    c2 = lax.broadcasted_iota(jnp.int32, (BLOCK, BLOCK), 1)
    return valid, r2 >= c2


def _layer_norm_parts(vs):
    mu = jnp.mean(vs, axis=-1, keepdims=True)
    xc = vs - mu
    rstd = lax.rsqrt(jnp.mean(xc * xc, axis=-1, keepdims=True) + EPS)
    return xc * rstd, rstd


def _stacked_queries(z_ref, hk, scale=None):
    q = jnp.concatenate([z_ref[:, _head_cols(Q0, Q_PER_KV * hk + g)] for g in range(Q_PER_KV)], axis=0)
    return (q if scale is None else q * scale).astype(BF16)


def _head_cols(base, h):
    return slice(base + HEAD_DIM * h, base + HEAD_DIM * (h + 1))


def _sink_row(sink_ref, hk):
    return jnp.concatenate([jnp.broadcast_to(sink_ref[:, h:h + 1], (1, BLOCK))
                            for h in range(Q_PER_KV * hk, Q_PER_KV * (hk + 1))], axis=1)


def _attn_probs(qs, kh, valid, sk, scale=None):
    s = _dot(kh, qs, NT)
    s = jnp.where(valid, s if scale is None else s * scale, -jnp.inf)
    m = jnp.maximum(jnp.max(s, axis=0, keepdims=True), sk)
    e = jnp.exp(s - m)
    es = jnp.exp(sk - m)
    inv = 1.0 / (jnp.sum(e, axis=0, keepdims=True) + es)
    return e * inv, es * inv


def _kv_ext(z_ref, kvp_ref):
    k_ext = jnp.concatenate([kvp_ref[:, 0:D_KV], z_ref[:, K0:K0 + D_KV]], axis=0).astype(BF16)
    v_ext = jnp.concatenate([kvp_ref[:, D_KV:2 * D_KV], z_ref[:, V0:V0 + D_KV]], axis=0).astype(BF16)
    return k_ext, v_ext


def _sgu_mixed(w_ref, bt_ref, vn, tril):
    ws = [jnp.where(tril, w_ref[g], 0.0).astype(BF16) for g in range(SGU_GROUPS)]
    mixed = jnp.concatenate([_dot(ws[g], vn[:, GROUP_DIM * g:GROUP_DIM * (g + 1)], NN) + bt_ref[:, g:g + 1]
                             for g in range(SGU_GROUPS)], axis=1)
    return ws, mixed


def _mixer_fwd(z, sinks, ln_g, ln_b, sgu_w, sgu_bt):
    s = z.shape[0]
    nb = s // BLOCK

    def body(z_ref, kvp_ref, sink_ref, lg_ref, lb_ref, w_ref, bt_ref, cat_ref, o_ref):
        n = pl.program_id(0)
        valid, tril = _mixer_masks(n)
        k_ext, v_ext = _kv_ext(z_ref, kvp_ref)
        for hk in range(2):
            kv = slice(HEAD_DIM * hk, HEAD_DIM * (hk + 1))
            p, _ = _attn_probs(_stacked_queries(z_ref, hk), k_ext[:, kv], valid, _sink_row(sink_ref, hk), ATTN_SCALE)
            o_t = _dot(v_ext[:, kv], p.astype(BF16), TN)
            for g in range(Q_PER_KV):
                o_ref[:, _head_cols(0, Q_PER_KV * hk + g)] = o_t[:, BLOCK * g:BLOCK * (g + 1)].T
        ga = z_ref[:, GA0:GA0 + D_ATTN]
        cat_ref[:, 0:D_ATTN] = (o_ref[...] * (ga * _sigmoid(ga))).astype(BF16)
        xhat, _ = _layer_norm_parts(z_ref[:, VS0:VS0 + D_SGU])
        vn = (xhat * lg_ref[...] + lb_ref[...]).astype(BF16)
        _, mixed = _sgu_mixed(w_ref, bt_ref, vn, tril)
        gt = z_ref[:, GS0:GS0 + D_SGU]
        cat_ref[:, D_ATTN:D_MODEL] = ((z_ref[:, U0:U0 + D_SGU] * mixed) * (gt * _sigmoid(gt))).astype(BF16)

    const2 = lambda n: (0, 0)
    return pl.pallas_call(
        body, name="mixer_fwd", grid=(nb,),
        out_shape=(jax.ShapeDtypeStruct((s, D_MODEL), BF16), jax.ShapeDtypeStruct((s, D_ATTN), F32)),
        in_specs=[pl.BlockSpec((BLOCK, D_IN), lambda n: (n, 0)),
                  pl.BlockSpec((BLOCK, 2 * D_KV), lambda n: (jnp.maximum(n - 1, 0), K0 // (2 * D_KV))),
                  pl.BlockSpec((1, N_Q_HEADS), const2), pl.BlockSpec((1, D_SGU), const2), pl.BlockSpec((1, D_SGU), const2),
                  pl.BlockSpec((SGU_GROUPS, BLOCK, BLOCK), lambda n: (0, 0, 0)), pl.BlockSpec((BLOCK, SGU_GROUPS), const2)],
        out_specs=(pl.BlockSpec((BLOCK, D_MODEL), lambda n: (n, 0)), pl.BlockSpec((BLOCK, D_ATTN), lambda n: (n, 0))),
        compiler_params=_params(("parallel",)),
    )(z, z, sinks, ln_g, ln_b, sgu_w, sgu_bt)


def _out_proj_loss(cat, wout, x, target, mod, final_g):
    s, d = x.shape
    tm = min(256, s)

    def body(cat_ref, w_ref, x_ref, t_ref, mod_ref, fg_ref, dx1_ref, dy_ref, dfg_ref, dgate_ref, loss_ref):
        @pl.when(pl.program_id(0) == 0)
        def _():
            dfg_ref[...] = jnp.zeros_like(dfg_ref)
            dgate_ref[...] = jnp.zeros_like(dgate_ref)
            loss_ref[...] = jnp.zeros_like(loss_ref)

        gate = mod_ref[:, 2 * d:3 * d]
        fg = fg_ref[...]
        yv = _dot(cat_ref[...], w_ref[...], NN)
        x1 = x_ref[...] + gate * yv
        r = lax.rsqrt(jnp.mean(x1 * x1, axis=-1, keepdims=True) + EPS)
        xr = x1 * r
        err = xr * fg - t_ref[...]
        dout = err * (1.0 / d)
        gy = dout * fg
        dx1 = r * gy - xr * (r * r * jnp.mean(gy * x1, axis=-1, keepdims=True))
        dx1_ref[...] = dx1
        dy_ref[...] = (gate * dx1).astype(BF16)
        dfg_ref[...] += jnp.sum(dout * xr, axis=0, keepdims=True)
        dgate_ref[...] += jnp.sum(dx1 * yv, axis=0, keepdims=True)
        loss_ref[...] += 0.5 * jnp.sum(jnp.mean(err * err, axis=-1, keepdims=True), axis=0, keepdims=True)

    row = lambda i: (i, 0)
    const = lambda i: (0, 0)
    return pl.pallas_call(
        body, name="out_proj_loss", grid=(s // tm,),
        out_shape=(jax.ShapeDtypeStruct((s, d), F32), jax.ShapeDtypeStruct((s, d), BF16), jax.ShapeDtypeStruct((1, d), F32),
                   jax.ShapeDtypeStruct((1, d), F32), jax.ShapeDtypeStruct((1, 1), F32)),
        in_specs=[pl.BlockSpec((tm, d), row), pl.BlockSpec((d, d), const), pl.BlockSpec((tm, d), row), pl.BlockSpec((tm, d), row),
                  pl.BlockSpec((1, 3 * d), const), pl.BlockSpec((1, d), const)],
        out_specs=(pl.BlockSpec((tm, d), row), pl.BlockSpec((tm, d), row), pl.BlockSpec((1, d), const), pl.BlockSpec((1, d), const),
                   pl.BlockSpec((1, 1), const)),
        compiler_params=_params(("arbitrary",)),
    )(cat, wout, x, target, mod, final_g)


def _mixer_bwd(z, dcat, o, sinks, ln_g, ln_b, sgu_w, sgu_bt):
    s = z.shape[0]
    nb = s // BLOCK

    def body(z_ref, kvp_ref, dc_ref, o_ref, sink_ref, lg_ref, lb_ref, w_ref, bt_ref,
             dz_ref, dsink_ref, dlg_ref, dlb_ref, dw_ref, db_ref, pend, dkv, carry, dsink_acc, dbt_acc):
        n = pl.program_id(0)

        @pl.when(n == 0)
        def _():
            pend[...] = jnp.zeros_like(pend)
            carry[...] = jnp.zeros_like(carry)
            dsink_acc[...] = jnp.zeros_like(dsink_acc)
            dbt_acc[...] = jnp.zeros_like(dbt_acc)
            dlg_ref[...] = jnp.zeros_like(dlg_ref)
            dlb_ref[...] = jnp.zeros_like(dlb_ref)
            dw_ref[...] = jnp.zeros_like(dw_ref)

        dz_ref[:, 0:K0] = pend[:, 0:K0]
        dz_ref[:, GA0:D_IN] = pend[:, GA0:D_IN]

        @pl.when(n == nb)
        def _():
            dkv[...] = jnp.zeros_like(dkv)

        @pl.when(n < nb)
        def _():
            valid, tril = _mixer_masks(n)
            k_ext, v_ext = _kv_ext(z_ref, kvp_ref)
            ga = z_ref[:, GA0:GA0 + D_ATTN]
            sg = _sigmoid(ga)
            datt = dc_ref[:, 0:D_ATTN]
            pend[:, GA0:GA0 + D_ATTN] = (datt * o_ref[...] * (sg * (1.0 + ga * (1.0 - sg)))).astype(BF16)
            do_all = (datt * (ga * sg)).astype(BF16)
            for hk in range(2):
                kv = slice(HEAD_DIM * hk, HEAD_DIM * (hk + 1))
                kh, vh = k_ext[:, kv], v_ext[:, kv]
                qs = _stacked_queries(z_ref, hk, ATTN_SCALE)
                p, p_sink = _attn_probs(qs, kh, valid, _sink_row(sink_ref, hk))
                do = jnp.concatenate([do_all[:, _head_cols(0, Q_PER_KV * hk + g)] for g in range(Q_PER_KV)], axis=0)
                dp = _dot(vh, do, NT)
                delta = jnp.sum(p * dp, axis=0, keepdims=True)
                ds = (p * (dp - delta)).astype(BF16)
                dsk = -p_sink * delta
                dq_t = _dot(kh, ds, TN) * ATTN_SCALE
                for g in range(Q_PER_KV):
                    h = Q_PER_KV * hk + g
                    dsink_acc[:, h:h + 1] += jnp.sum(dsk[:, BLOCK * g:BLOCK * (g + 1)], axis=1, keepdims=True)
                    pend[:, _head_cols(Q0, h)] = dq_t[:, BLOCK * g:BLOCK * (g + 1)].T.astype(BF16)
                dkv[:, kv] = _dot(ds, qs, NN)
                dkv[:, D_KV + HEAD_DIM * hk:D_KV + HEAD_DIM * (hk + 1)] = _dot(p.astype(BF16), do, NN)

            xhat, rstd = _layer_norm_parts(z_ref[:, VS0:VS0 + D_SGU])
            lg = lg_ref[...]
            vn = (xhat * lg + lb_ref[...]).astype(BF16)
            ws, mixed = _sgu_mixed(w_ref, bt_ref, vn, tril)
            u = z_ref[:, U0:U0 + D_SGU]
            gt = z_ref[:, GS0:GS0 + D_SGU]
            sg = _sigmoid(gt)
            dsgu = dc_ref[:, D_ATTN:D_MODEL]
            dgated = dsgu * (gt * sg)
            pend[:, U0:U0 + D_SGU] = (dgated * mixed).astype(BF16)
            pend[:, GS0:GS0 + D_SGU] = (dsgu * (u * mixed) * (sg * (1.0 + gt * (1.0 - sg)))).astype(BF16)
            dmixed = dgated * u
            dmb = dmixed.astype(BF16)
            dvn_parts = []
            for g in range(SGU_GROUPS):
                gs_ = slice(GROUP_DIM * g, GROUP_DIM * (g + 1))
                dw_ref[g] += jnp.where(tril, _dot(dmb[:, gs_], vn[:, gs_], NT), 0.0)
                dbt_acc[:, g:g + 1] += jnp.sum(dmixed[:, gs_], axis=-1, keepdims=True)
                dvn_parts.append(_dot(ws[g], dmb[:, gs_], TN))
            dvn = jnp.concatenate(dvn_parts, axis=1)
            dlg_ref[...] += jnp.sum(dvn * xhat, axis=0, keepdims=True)
            dlb_ref[...] += jnp.sum(dvn, axis=0, keepdims=True)
            dxh = dvn * lg
            pend[:, VS0:VS0 + D_SGU] = (rstd * (dxh - jnp.mean(dxh, axis=-1, keepdims=True)
                                                - xhat * jnp.mean(dxh * xhat, axis=-1, keepdims=True))).astype(BF16)

        dz_ref[:, K0:GA0] = (carry[...] + dkv[0:BLOCK, :]).astype(BF16)
        carry[...] = dkv[BLOCK:2 * BLOCK, :]

        @pl.when(n == nb)
        def _():
            dsink_ref[...] = dsink_acc[:, 0:N_Q_HEADS]
            db_ref[...] = dbt_acc[...].T[0:SGU_GROUPS, :]

    cur = lambda n: (jnp.minimum(n, nb - 1), 0)
    const2 = lambda n: (0, 0)
    return pl.pallas_call(
        body, name="mixer_bwd", grid=(nb + 1,),
        out_shape=(jax.ShapeDtypeStruct((s, D_IN), BF16), jax.ShapeDtypeStruct((1, N_Q_HEADS), F32),
                   jax.ShapeDtypeStruct((1, D_SGU), F32), jax.ShapeDtypeStruct((1, D_SGU), F32),
                   jax.ShapeDtypeStruct((SGU_GROUPS, BLOCK, BLOCK), F32), jax.ShapeDtypeStruct((SGU_GROUPS, BLOCK), F32)),
        in_specs=[pl.BlockSpec((BLOCK, D_IN), cur),
                  pl.BlockSpec((BLOCK, 2 * D_KV), lambda n: (jnp.maximum(jnp.minimum(n, nb - 1) - 1, 0), K0 // (2 * D_KV))),
                  pl.BlockSpec((BLOCK, D_MODEL), cur), pl.BlockSpec((BLOCK, D_ATTN), cur),
                  pl.BlockSpec((1, N_Q_HEADS), const2), pl.BlockSpec((1, D_SGU), const2), pl.BlockSpec((1, D_SGU), const2),
                  pl.BlockSpec((SGU_GROUPS, BLOCK, BLOCK), lambda n: (0, 0, 0)), pl.BlockSpec((BLOCK, SGU_GROUPS), const2)],
        out_specs=(pl.BlockSpec((BLOCK, D_IN), lambda n: (jnp.maximum(n - 1, 0), 0)),
                   pl.BlockSpec((1, N_Q_HEADS), const2), pl.BlockSpec((1, D_SGU), const2), pl.BlockSpec((1, D_SGU), const2),
                   pl.BlockSpec((SGU_GROUPS, BLOCK, BLOCK), lambda n: (0, 0, 0)), pl.BlockSpec((SGU_GROUPS, BLOCK), const2)),
        scratch_shapes=[pltpu.VMEM((BLOCK, D_IN), BF16), pltpu.VMEM((2 * BLOCK, 2 * D_KV), F32), pltpu.VMEM((BLOCK, 2 * D_KV), F32),
                        pltpu.VMEM((1, BLOCK), F32), pltpu.VMEM((BLOCK, BLOCK), F32)],
        compiler_params=_params(("arbitrary",)),
    )(z, z, dcat, o, sinks, ln_g, ln_b, sgu_w, sgu_bt)


def _in_proj_bwd(dz, win_t, x, dx1, norm_g, mod):
    s, d = x.shape
    kdim = dz.shape[1]
    tm = min(256, s)

    def body(dz_ref, w_ref, x_ref, dx1_ref, g_ref, mod_ref, gx_ref, dshift_ref, dscale_ref, dng_ref):
        @pl.when(pl.program_id(0) == 0)
        def _():
            dshift_ref[...] = jnp.zeros_like(dshift_ref)
            dscale_ref[...] = jnp.zeros_like(dscale_ref)
            dng_ref[...] = jnp.zeros_like(dng_ref)

        dh = _dot(dz_ref[...], w_ref[...], NN)
        xv = x_ref[...]
        ng = g_ref[...]
        r = lax.rsqrt(jnp.mean(xv * xv, axis=-1, keepdims=True) + EPS)
        xn = xv * r
        dshift_ref[...] += jnp.sum(dh, axis=0, keepdims=True)
        dscale_ref[...] += jnp.sum(dh * (xn * ng), axis=0, keepdims=True)
        dhs = dh * (1.0 + mod_ref[:, d:2 * d])
        dng_ref[...] += jnp.sum(dhs * xn, axis=0, keepdims=True)
        dxn = dhs * ng
        gx_ref[...] = dx1_ref[...] + r * dxn - xn * (r * r * jnp.mean(dxn * xv, axis=-1, keepdims=True))

    row = lambda i: (i, 0)
    const = lambda i: (0, 0)
    return pl.pallas_call(
        body, name="in_proj_bwd", grid=(s // tm,),
        out_shape=(jax.ShapeDtypeStruct((s, d), F32),) + (jax.ShapeDtypeStruct((1, d), F32),) * 3,
        in_specs=[pl.BlockSpec((tm, kdim), row), pl.BlockSpec((kdim, d), const, pipeline_mode=pl.Buffered(1)),
                  pl.BlockSpec((tm, d), row), pl.BlockSpec((tm, d), row), pl.BlockSpec((1, d), const), pl.BlockSpec((1, 3 * d), const)],
        out_specs=(pl.BlockSpec((tm, d), row),) + (pl.BlockSpec((1, d), const),) * 3,
        compiler_params=_params(("arbitrary",)),
    )(dz, win_t, x, dx1, norm_g, mod)


def _sum_slots(slots, own, place, rows_tile, full_cols, name):
    n_slots, r, hcols = slots.shape
    nblk = r // rows_tile
    per_device = n_slots == N_DEV

    def body(place_ref, s_ref, own_ref, o_ref):
        mine = 2 * place_ref[1] + place_ref[0] if per_device else place_ref[1]
        acc = jnp.zeros((rows_tile, hcols), F32)
        for k in range(n_slots):
            acc = acc + jnp.where(mine == k, own_ref[...], s_ref[k]).astype(F32)
        o_ref[...] = acc

    return pl.pallas_call(
        body, name=name,
        grid_spec=pltpu.PrefetchScalarGridSpec(
            num_scalar_prefetch=1, grid=(nblk,),
            in_specs=[pl.BlockSpec((n_slots, rows_tile, hcols), lambda i, p: (0, i, 0)),
                      pl.BlockSpec((rows_tile, hcols), lambda i, p: (p[1] * nblk + i, p[0] if per_device else 0))],
            out_specs=pl.BlockSpec((rows_tile, hcols), lambda i, p: (i, p[0]))),
        out_shape=jax.ShapeDtypeStruct((r, full_cols), F32),
        compiler_params=_params(("parallel",)),
    )(place, slots, own)


def _adamw_rows(w, g, m, v, tile, name, after):
    r, c = w.shape

    def body(w_ref, g_ref, m_ref, v_ref, after_ref, go_ref, d_ref, mo_ref, vo_ref):
        g_ = g_ref[...]
        delta, m2, v2 = _adamw(w_ref[...], g_, m_ref[...], v_ref[...])
        go_ref[...] = g_
        d_ref[...] = delta
        mo_ref[...] = m2
        vo_ref[...] = v2

    blk = pl.BlockSpec((tile, c), lambda i: (i, 0))
    return pl.pallas_call(
        body, name=name, grid=(r // tile,), out_shape=(jax.ShapeDtypeStruct((r, c), F32),) * 4,
        in_specs=[blk] * 4 + [ANY], out_specs=(blk,) * 4,
        compiler_params=_params(("parallel",)),
    )(w, g, m, v, after)


def _adamw_ada(cact, dmod, w, m, v, tile=256):
    r, c = w.shape

    def body(c_ref, d_ref, w_ref, m_ref, v_ref, go_ref, dl_ref, mo_ref, vo_ref):
        g_ = _dot(c_ref[...].astype(BF16), d_ref[...].astype(BF16), TN)
        delta, m2, v2 = _adamw(w_ref[...], g_, m_ref[...], v_ref[...])
        go_ref[...] = g_
        dl_ref[...] = delta
        mo_ref[...] = m2
        vo_ref[...] = v2

    blk = pl.BlockSpec((tile, c), lambda i: (i, 0))
    return pl.pallas_call(
        body, name="adamw_ada", grid=(r // tile,), out_shape=(jax.ShapeDtypeStruct((r, c), F32),) * 4,
        in_specs=[pl.BlockSpec((N_DEV, tile), lambda i: (0, i)), pl.BlockSpec((N_DEV, c), lambda i: (0, 0)), blk, blk, blk],
        out_specs=(blk,) * 4,
        compiler_params=_params(("parallel",)),
    )(cact, dmod, w, m, v)


def kernel(x, c, norm_g, w_ada, b_ada, w_in, attn_sinks, sgu_ln_g, sgu_ln_b, sgu_w, sgu_b, w_out, final_g, loss_target, m_norm_g, m_w_ada, m_b_ada, m_w_in, m_attn_sinks, m_sgu_ln_g, m_sgu_ln_b, m_sgu_w, m_sgu_b, m_w_out, m_final_g, v_norm_g, v_w_ada, v_b_ada, v_w_in, v_attn_sinks, v_sgu_ln_g, v_sgu_ln_b, v_sgu_w, v_sgu_b, v_w_out, v_final_g):
    xs, tgt = x[0], loss_target[0]
    core = lax.axis_index("c").astype(jnp.int32).reshape(1)
    chip = 2 * lax.axis_index("x") + lax.axis_index("y")
    place = jnp.stack([core[0], chip.astype(jnp.int32)])
    sgu_bt = sgu_b[0].T

    w_in_t, m_in_t, v_in_t = (jnp.swapaxes(a[0], 0, 1) for a in (w_in, m_w_in, v_w_in))
    win_t = _cast_into_gathered(w_in_t, IN_SHARD // 4, place, "cast_w_in", c)
    win_t, mod, cact = _gather_w_in_and_modulate(win_t, c, w_ada[0], b_ada)
    wout = _cast_into_gathered(w_out[0], OUT_SHARD // 2, place, "cast_w_out", win_t)
    sems, bufs, token = _split_start(_gather_copies, [wout], 6, "w_out_gather_start", win_t)
    z, h = _in_proj(xs, norm_g, mod, win_t, token)
    cat, o = _mixer_fwd(z, attn_sinks, sgu_ln_g, sgu_ln_b, sgu_w[0], sgu_bt)
    wout = _split_wait(_gather_copies, sems, bufs, cat, "w_out_gather_wait")[0]
    dx1, dy, dfg, dgate, loss = _out_proj_loss(cat, wout, xs, tgt, mod, final_g.reshape(1, D_MODEL))

    po = _matmul_tn(cat, dy, 512, "w_out_grad")
    sems, bufs, token = _split_start(_owner_copies, [po, lax.empty((N_DEV, OUT_SHARD, HALF), BF16)], N_DEV - 1,
                                     "w_out_grad_to_owners_start", cat)
    dcat = _matmul_nt(dy, wout, 256, F32, "out_proj_bwd", token)
    dz, dsinks, dlg, dlb, dsgu_w, dsgu_b = _mixer_bwd(z, dcat, o, attn_sinks, sgu_ln_g, sgu_ln_b, sgu_w[0], sgu_bt)
    po, slots_o = _split_wait(_owner_copies, sems, bufs, dz, "w_out_grad_to_owners_wait")
    go = _halves_swap(_sum_slots(slots_o, po, place, OUT_SHARD // 2, D_MODEL, "w_out_grad_sum"), "w_out_grad_halves_swap")

    sw = _to_sibling(_grad_matmul(dz, h, 1 - core, 768, "w_in_grad_sibling"), "w_in_grad_to_sibling")
    pw = _grad_matmul(dz, h, core, 768, "w_in_grad_own", add=sw)
    first = functools.partial(_chip_copies, [(0, 0, IN_SHARD), (1, 0, IN_SHARD), (2, 0, IN_SHARD // 4)])
    rest = functools.partial(_chip_copies, [(2, IN_SHARD // 4, 3 * IN_SHARD // 4)])
    sems, bufs, token = _split_start(first, [pw, lax.empty((N_CHIPS, IN_SHARD, HALF), BF16)], 3, "w_in_grad_to_chips_start", sw)
    gx, dshift, dscale, dng = _in_proj_bwd(dz, win_t, xs, dx1, norm_g + token[0:1, 0:1], mod)
    bufs = _split_wait(first, sems, bufs, gx, "w_in_grad_to_chips_wait")

    parts = dict(dng=dng, dshift=dshift, dscale=dscale, dgate=dgate, dfg=dfg, dlg=dlg, dlb=dlb, dsinks=dsinks,
                 dsgu_b=dsgu_b, dsgu_w=dsgu_w, loss=loss)
    small = dict(norm_g=(norm_g, m_norm_g, v_norm_g), b_ada=(b_ada, m_b_ada, v_b_ada),
                 attn_sinks=(attn_sinks, m_attn_sinks, v_attn_sinks), sgu_ln_g=(sgu_ln_g, m_sgu_ln_g, v_sgu_ln_g),
                 sgu_ln_b=(sgu_ln_b, m_sgu_ln_b, v_sgu_ln_b), sgu_w=(sgu_w, m_sgu_w, v_sgu_w), sgu_b=(sgu_b, m_sgu_b, v_sgu_b),
                 final_g=(final_g, m_final_g, v_final_g))
    tot, gsgu_w, dmod_mine = _small_grads_sum(parts, bufs[1])
    sems, bufs, token = _split_start(rest, list(bufs), 1, "w_in_grad_rest_to_chips_start", tot)
    res, total = _small_adamw(tot, gsgu_w, small)
    res["w_ada"] = _adamw_ada(cact, dmod_mine + token[0:1, 0:1], w_ada[0], m_w_ada[0], v_w_ada[0])
    res["w_out"] = _adamw_rows(w_out[0], go, m_w_out[0], v_w_out[0], 256, "adamw_w_out", res["w_ada"][1])

    pw, slots_w = _split_wait(rest, sems, bufs, res["w_out"][1], "w_in_grad_rest_to_chips_wait")
    gw_t = _halves_swap(_sum_slots(slots_w, pw, place, IN_SHARD // 4, D_MODEL, "w_in_grad_sum"), "w_in_grad_halves_swap")
    res["w_in"] = tuple(jnp.swapaxes(a, 0, 1) for a in _adamw_rows(w_in_t, gw_t, m_in_t, v_in_t, IN_SHARD // 4, "adamw_w_in", gw_t))

    order = ["norm_g", "w_ada", "b_ada", "w_in", "attn_sinks", "sgu_ln_g", "sgu_ln_b", "sgu_w", "sgu_b", "w_out", "final_g"]
    like = dict(norm_g=norm_g, w_ada=w_ada, b_ada=b_ada, w_in=w_in, attn_sinks=attn_sinks, sgu_ln_g=sgu_ln_g, sgu_ln_b=sgu_ln_b,
                sgu_w=sgu_w, sgu_b=sgu_b, w_out=w_out, final_g=final_g)
    outs = [total[0, 0], gx[None]]
    for kind in range(4):
        outs += [res[n][kind].reshape(like[n].shape) for n in order]
    return tuple(outs)
```

```python
import functools
import math

import jax
import jax.numpy as jnp
from jax import lax
from jax.experimental import pallas as pl
from jax.experimental.pallas import tpu as pltpu

F32 = jnp.float32
BF16 = jnp.bfloat16

D_MODEL = 2048
D_ATTN = 1024
HEAD_DIM = 64
N_Q_HEADS = 16
Q_PER_KV = 8
D_KV = 128
BLOCK = 128
D_SGU = 1024
SGU_GROUPS = 8
GROUP_DIM = 128
D_IN = 5376
EPS = 1e-6
ATTN_SCALE = 1.0 / math.sqrt(HEAD_DIM)
N_CHIPS = 4
N_DEV = 8
IN_SHARD = D_IN // N_CHIPS
OUT_SHARD = D_MODEL // N_CHIPS
ADA_SHARD = 3 * D_MODEL // N_CHIPS
HALF = D_MODEL // 2
IN_TILE = 768

Q0, K0, V0, GA0, U0, VS0, GS0 = 0, 1024, 1152, 1280, 2304, 3328, 4352

ADAM_LR = 0.001
ADAM_B1 = 0.9
ADAM_B2 = 0.999
ADAM_EPS = 1e-08
ADAM_WD = 0.01
ADAM_STEP = 10

VMEM_LIMIT_V7X = 56 * 1024 * 1024

NN = (((1,), (0,)), ((), ()))
NT = (((1,), (1,)), ((), ()))
TN = (((0,), (0,)), ((), ()))
MESH = pl.DeviceIdType.MESH
ANY = pl.BlockSpec(memory_space=pl.ANY)
VMEM = pl.BlockSpec(memory_space=pltpu.VMEM)
HBM = pl.BlockSpec(memory_space=pltpu.HBM)
SEM = pl.BlockSpec(memory_space=pltpu.SEMAPHORE)
EFFECT = pltpu.SideEffectType.DATAFLOW_SIDE_EFFECTING


def _dot(a, b, dims):
    return lax.dot_general(a, b, dims, preferred_element_type=F32)


def _params(sem=None):
    return pltpu.CompilerParams(dimension_semantics=sem, vmem_limit_bytes=VMEM_LIMIT_V7X)


def _sigmoid(x):
    return jax.nn.sigmoid(x)


def _adamw(w, g, m, v):
    m = ADAM_B1 * m + (1.0 - ADAM_B1) * g
    v = ADAM_B2 * v + (1.0 - ADAM_B2) * (g * g)
    m_hat = m / (1.0 - ADAM_B1 ** ADAM_STEP)
    v_hat = v / (1.0 - ADAM_B2 ** ADAM_STEP)
    delta = -ADAM_LR * (m_hat / (jnp.sqrt(v_hat) + ADAM_EPS) + ADAM_WD * w)
    return delta, m, v


def _place():
    return lax.axis_index("x"), lax.axis_index("y"), lax.axis_index("c")


def _flip(p, bit):
    return 1 - p if bit else p


def _other_chips(x, y):
    return [(_flip(x, bx), _flip(y, by)) for bx, by in ((0, 1), (1, 0), (1, 1))]


def _split_start(copies, bufs, n_sems, name, after):
    n = len(bufs)

    def body(*refs):
        send_sems, recv_sems = refs[n + 1:n + 3]
        for cp in copies(*refs[:n], send_sems, recv_sems):
            cp.start()
        refs[-1][...] = jnp.zeros_like(refs[-1])

    bufs = [pltpu.with_memory_space_constraint(a, pltpu.HBM) for a in bufs]
    outs = pl.pallas_call(
        body, name=name,
        out_shape=(pltpu.SemaphoreType.DMA((n_sems,)), pltpu.SemaphoreType.DMA((n_sems,)))
        + tuple(pltpu.HBM(a.shape, a.dtype) for a in bufs) + (jax.ShapeDtypeStruct((8, 128), F32),),
        in_specs=[HBM] * n + [ANY], out_specs=(SEM, SEM) + (HBM,) * n + (VMEM,), input_output_aliases={k: k + 2 for k in range(n)},
        compiler_params=pltpu.CompilerParams(has_side_effects=EFFECT),
    )(*bufs, after)
    return outs[:2], list(outs[2:2 + n]), outs[-1]


def _split_wait(copies, sems, bufs, after, name):
    n = len(bufs)

    def body(*refs):
        send_sems, recv_sems = refs[n:n + 2]
        for cp in copies(*refs[:n], send_sems, recv_sems):
            cp.wait_send()
            cp.wait_recv()

    return pl.pallas_call(
        body, name=name, out_shape=tuple(pltpu.HBM(a.shape, a.dtype) for a in bufs),
        in_specs=[HBM] * n + [SEM, SEM, ANY], out_specs=(HBM,) * n, input_output_aliases={k: k for k in range(n)},
        compiler_params=pltpu.CompilerParams(has_side_effects=EFFECT),
    )(*bufs, *sems, after)


def _gather_copies(full_ref, send_sems, recv_sems):
    x, y, cc = _place()
    shard = full_ref.shape[0] // N_CHIPS
    half = shard // 2
    mine = full_ref.at[pl.ds(shard * (2 * x + y) + half * cc, half), :]
    return [pltpu.make_async_remote_copy(src_ref=mine, dst_ref=mine, send_sem=send_sems.at[2 * j + pc],
                                         recv_sem=recv_sems.at[2 * j + _flip(cc, pc)], device_id=(px, py, pc), device_id_type=MESH)
            for j, (px, py) in enumerate(_other_chips(x, y)) for pc in range(2)]


def _gather_w_in_modulate_project(full, c, w_ada, b_ada, xs, norm_g):
    shard = full.shape[0] // N_CHIPS
    half = shard // 2
    quarter = half // 2
    kdim = w_ada.shape[0]
    kc = 256
    s, d = xs.shape
    rows_at_once = min(256, s)
    n_chunks = s // rows_at_once
    m_rows = min(512, s)
    n_m = s // m_rows

    def body(full_in, c_ref, w_hbm, b_ref, x_hbm, g_ref, full_ref, mod_ref, cact_ref, h_hbm, z_hbm,
             w_vmem, cbuf, pbuf, rbuf, h_vmem, xbuf, bbuf, zbuf, w_sem, x_sems, b_sem, z_sems, h_sem,
             send_sems, recv_sems, msend_sems, mrecv_sems):
        x, y, cc = _place()
        me = 4 * x + 2 * y + cc
        chip = 2 * x + y
        across_x, across_y, far = (1 - x, y), (x, 1 - y), (1 - x, 1 - y)

        def rows(chip_xy, core, q):
            return full_ref.at[pl.ds(shard * (2 * chip_xy[0] + chip_xy[1]) + half * core + quarter * q, quarter), :]

        def copy(k, piece, to):
            return pltpu.make_async_remote_copy(src_ref=piece, dst_ref=piece, send_sem=send_sems.at[k], recv_sem=recv_sems.at[k],
                                                device_id=to, device_id_type=MESH)

        def small(k, src, dst, to):
            return pltpu.make_async_remote_copy(src_ref=src, dst_ref=dst, send_sem=msend_sems.at[k], recv_sem=mrecv_sems.at[k],
                                                device_id=to, device_id_type=MESH)

        def x_chunk(i):
            return pltpu.make_async_copy(x_hbm.at[pl.ds(rows_at_once * i, rows_at_once), :], xbuf.at[i % 2], x_sems.at[i % 2])

        cbuf[me] = c_ref[...]
        gathers = [small(k - 1, cbuf.at[me], cbuf.at[me], (_flip(x, (k >> 2) & 1), _flip(y, (k >> 1) & 1), _flip(cc, k & 1)))
                   for k in range(1, N_DEV)]
        for cp in gathers:
            cp.start()
        w_load = pltpu.make_async_copy(w_hbm, w_vmem, w_sem)
        w_load.start()
        x_chunk(0).start()
        for cp in gathers:
            cp.wait()
        w_load.wait()
        for b in range(N_DEV):
            cv = cbuf[b]
            cact_ref[b:b + 1, :] = cv * _sigmoid(cv)
        acc = jnp.zeros((N_DEV, ADA_SHARD), F32)
        for k0 in range(0, kdim, kc):
            acc = acc + _dot(cact_ref[:, k0:k0 + kc].astype(BF16), w_vmem[k0:k0 + kc, :].astype(BF16), NN)
        for b in range(N_DEV):
            pbuf[b] = acc[b:b + 1, :]
        rbuf[chip] = pbuf[me]
        sends = [small(N_DEV - 1 + j, pbuf.at[4 * px + 2 * py + cc], rbuf.at[chip], (px, py, cc))
                 for j, (px, py) in enumerate(_other_chips(x, y))]
        for cp in sends:
            cp.start()
        direct = [copy(0, rows((x, y), cc, 0), (*across_x, cc)), copy(2, rows((x, y), cc, 1), (*across_y, cc)),
                  copy(1, rows((x, y), cc, 1), (*across_x, cc)), copy(3, rows((x, y), cc, 0), (*across_y, cc))]
        for cp in direct:
            cp.start()
        for cp in sends:
            cp.wait()
        for k in range(N_CHIPS):
            cols = slice(ADA_SHARD * k, ADA_SHARD * (k + 1))
            mod_ref[:, cols] = rbuf[k] + b_ref[:, cols]

        for i in range(n_chunks):
            x_chunk(i).wait()
            if i + 1 < n_chunks:
                x_chunk(i + 1).start()
            xv = xbuf[i % 2]
            r = lax.rsqrt(jnp.mean(xv * xv, axis=-1, keepdims=True) + EPS)
            h_vmem[rows_at_once * i:rows_at_once * (i + 1), :] = (
                (xv * r * g_ref[...]) * (1.0 + mod_ref[:, d:2 * d]) + mod_ref[:, 0:d]).astype(BF16)
        h_out = pltpu.make_async_copy(h_vmem, h_hbm, h_sem)
        h_out.start()

        def project(shard_index):
            first = pl.multiple_of(2 * IN_TILE * shard_index, 128)
            b_load = pltpu.make_async_copy(full_ref.at[pl.ds(first, IN_TILE), :], bbuf, b_sem)
            b_load.start()
            b_load.wait()
            outs = []
            for m in range(n_m):
                if m >= 2:
                    outs[m - 2].wait()
                zbuf[m % 2] = _dot(h_vmem[m_rows * m:m_rows * (m + 1), :], bbuf[...], NT)
                outs.append(pltpu.make_async_copy(zbuf.at[m % 2], z_hbm.at[pl.ds(m_rows * m, m_rows), pl.ds(first, IN_TILE)],
                                                  z_sems.at[m % 2]))
                outs[m].start()
            for cp in outs[max(n_m - 2, 0):]:
                cp.wait()

        project(chip)

        landing = [(0, across_x, 0, (4, (*across_y, cc))), (2, across_y, 1, (5, (*across_x, cc))),
                   (1, across_x, 1, None), (3, across_y, 0, None), (4, far, 0, None), (5, far, 1, None)]
        passed = []

        def land(n):
            k, whose, q, onward = landing[n]
            piece = rows(whose, cc, q)
            copy(k, piece, (x, y, cc)).wait_recv()
            if onward is not None:
                passed.append(copy(onward[0], piece, onward[1]))
                passed[-1].start()
            passed.append(copy(6 + n, piece, (x, y, 1 - cc)))
            passed[-1].start()

        def from_sibling(n):
            k, whose, q, onward = landing[n]
            copy(6 + n, rows(whose, 1 - cc, q), (x, y, cc)).wait_recv()

        for n in range(4):
            land(n)
        for n in range(4):
            from_sibling(n)
        land(4)
        project(2 * across_x[0] + across_x[1])
        land(5)
        project(2 * across_y[0] + across_y[1])
        from_sibling(4)
        from_sibling(5)
        project(2 * far[0] + far[1])
        h_out.wait()
        for cp in direct + passed:
            cp.wait_send()

    return pl.pallas_call(
        body, name="w_in_gather_modulate_project",
        out_shape=(jax.ShapeDtypeStruct(full.shape, full.dtype), jax.ShapeDtypeStruct((1, 3 * D_MODEL), F32),
                   jax.ShapeDtypeStruct((N_DEV, D_MODEL), F32), jax.ShapeDtypeStruct((s, d), BF16),
                   jax.ShapeDtypeStruct((s, full.shape[0]), F32)),
        in_specs=[ANY, VMEM, ANY, VMEM, ANY, VMEM], out_specs=(ANY, VMEM, VMEM, ANY, ANY), input_output_aliases={0: 0},
        scratch_shapes=[pltpu.VMEM(w_ada.shape, F32), pltpu.VMEM((N_DEV, 1, D_MODEL), F32), pltpu.VMEM((N_DEV, 1, ADA_SHARD), F32),
                        pltpu.VMEM((N_CHIPS, 1, ADA_SHARD), F32), pltpu.VMEM((s, d), BF16), pltpu.VMEM((2, rows_at_once, d), F32),
                        pltpu.VMEM((IN_TILE, d), BF16), pltpu.VMEM((2, m_rows, IN_TILE), F32),
                        pltpu.SemaphoreType.DMA, pltpu.SemaphoreType.DMA((2,)), pltpu.SemaphoreType.DMA, pltpu.SemaphoreType.DMA((2,)),
                        pltpu.SemaphoreType.DMA,
                        pltpu.SemaphoreType.DMA((12,)), pltpu.SemaphoreType.DMA((12,)),
                        pltpu.SemaphoreType.DMA((N_DEV + 2,)), pltpu.SemaphoreType.DMA((N_DEV + 2,))],
        compiler_params=_params(),
    )(full, c, w_ada, b_ada, xs, norm_g)


def _to_sibling(g, name):
    def body(g_ref, r_ref, send_sem, recv_sem):
        x, y, cc = _place()
        cp = pltpu.make_async_remote_copy(src_ref=g_ref, dst_ref=r_ref, send_sem=send_sem, recv_sem=recv_sem,
                                          device_id=(x, y, 1 - cc), device_id_type=MESH)
        cp.start()
        cp.wait()

    return pl.pallas_call(
        body, name=name, out_shape=jax.ShapeDtypeStruct(g.shape, g.dtype), in_specs=[ANY], out_specs=ANY,
        scratch_shapes=[pltpu.SemaphoreType.DMA, pltpu.SemaphoreType.DMA], compiler_params=_params(),
    )(g)


def _chip_copies(pieces, p_ref, land_ref, send_sems, recv_sems):
    x, y, cc = _place()
    chip = 2 * x + y
    shard = land_ref.shape[1]
    others = _other_chips(x, y)
    cps = []
    for k, (j, r0, nr) in enumerate(pieces):
        px, py = others[j]
        cps.append(pltpu.make_async_remote_copy(
            src_ref=p_ref.at[pl.ds(shard * (2 * px + py) + r0, nr), :], dst_ref=land_ref.at[chip, pl.ds(r0, nr), :],
            send_sem=send_sems.at[k], recv_sem=recv_sems.at[k], device_id=(px, py, cc), device_id_type=MESH))
    return cps


def _owner_copies(p_ref, land_ref, send_sems, recv_sems):
    x, y, cc = _place()
    me = 4 * x + 2 * y + cc
    shard = land_ref.shape[1]
    cps = []
    for k in range(1, N_DEV):
        px, py, pc = _flip(x, (k >> 2) & 1), _flip(y, (k >> 1) & 1), _flip(cc, k & 1)
        block = p_ref.at[pl.ds(shard * (2 * px + py), shard), pl.ds(pl.multiple_of(HALF * pc, 128), HALF)]
        cps.append(pltpu.make_async_remote_copy(src_ref=block, dst_ref=land_ref.at[me], send_sem=send_sems.at[k - 1],
                                                recv_sem=recv_sems.at[k - 1], device_id=(px, py, pc), device_id_type=MESH))
    return cps


def _halves_swap(f, name):
    def body(f_in, f_ref, send_sem, recv_sem):
        x, y, cc = _place()
        mine = f_ref.at[:, pl.ds(pl.multiple_of(HALF * cc, 128), HALF)]
        cp = pltpu.make_async_remote_copy(src_ref=mine, dst_ref=mine, send_sem=send_sem, recv_sem=recv_sem,
                                          device_id=(x, y, 1 - cc), device_id_type=MESH)
        cp.start()
        cp.wait()

    return pl.pallas_call(
        body, name=name, out_shape=jax.ShapeDtypeStruct(f.shape, f.dtype),
        in_specs=[ANY], out_specs=ANY, input_output_aliases={0: 0},
        scratch_shapes=[pltpu.SemaphoreType.DMA, pltpu.SemaphoreType.DMA], compiler_params=_params(),
    )(f)


ROW_NORM_G, ROW_DMOD, ROW_FINAL_G, ROW_LN, ROW_SINKS, ROW_LOSS, ROW_SGU_B, MISC_ROWS = 0, 1, 4, 5, 6, 7, 8, 16


def _small_grads_sum(parts, after):
    part_names = ["dng", "dshift", "dscale", "dgate", "dfg", "dlg", "dlb", "dsinks", "dsgu_b", "dsgu_w", "loss"]
    n_parts = len(part_names)
    n_peers = N_DEV - 1

    def body(*refs):
        p = dict(zip(part_names, refs[:n_parts]))
        tot_ref, gw_ref, dmod_ref = refs[n_parts + 1:n_parts + 4]
        misc, cmisc, dsend, dbuf, wland, wsum, own_sem, send_sems, recv_sems = refs[n_parts + 4:]
        x, y, cc = _place()
        me = 4 * x + 2 * y + cc
        chip = 2 * x + y
        peers = [(_flip(x, (k >> 2) & 1), _flip(y, (k >> 1) & 1), _flip(cc, k & 1)) for k in range(1, N_DEV)]

        def remote(k, src, dst, peer):
            return pltpu.make_async_remote_copy(src_ref=src, dst_ref=dst, send_sem=send_sems.at[k], recv_sem=recv_sems.at[k],
                                                device_id=peer, device_id_type=MESH)

        dsend[:, 0:D_MODEL] = p["dshift"][...]
        dsend[:, D_MODEL:2 * D_MODEL] = p["dscale"][...]
        dsend[:, 2 * D_MODEL:3 * D_MODEL] = p["dgate"][...]

        def shard_cols(chip_index):
            return dsend.at[:, pl.ds(pl.multiple_of(ADA_SHARD * chip_index, 128), ADA_SHARD)]

        own_cols = pltpu.make_async_copy(shard_cols(chip), dbuf.at[me], own_sem)
        own_cols.start()
        gathers = [remote(k, shard_cols(2 * px + py), dbuf.at[me], (px, py, pc)) for k, (px, py, pc) in enumerate(peers)]
        wland[me] = p["dsgu_w"][me]
        scatters = [remote(n_peers + k, p["dsgu_w"].at[4 * px + 2 * py + pc], wland.at[me], (px, py, pc))
                    for k, (px, py, pc) in enumerate(peers)]

        misc[cc] = jnp.zeros((MISC_ROWS, D_MODEL), F32)
        misc[cc, ROW_NORM_G:ROW_NORM_G + 1, :] = p["dng"][...]
        misc[cc, ROW_DMOD:ROW_DMOD + 1, :] = p["dshift"][...]
        misc[cc, ROW_DMOD + 1:ROW_DMOD + 2, :] = p["dscale"][...]
        misc[cc, ROW_DMOD + 2:ROW_DMOD + 3, :] = p["dgate"][...]
        misc[cc, ROW_FINAL_G:ROW_FINAL_G + 1, :] = p["dfg"][...]
        misc[cc, ROW_LN:ROW_LN + 1, 0:D_SGU] = p["dlg"][...]
        misc[cc, ROW_LN:ROW_LN + 1, D_SGU:2 * D_SGU] = p["dlb"][...]
        misc[cc, ROW_SINKS:ROW_SINKS + 1, 0:N_Q_HEADS] = p["dsinks"][...]
        misc[cc, ROW_LOSS:ROW_LOSS + 1, 0:1] = p["loss"][...]
        misc[cc, ROW_SGU_B:ROW_SGU_B + SGU_GROUPS, 0:BLOCK] = p["dsgu_b"][...]
        pair = remote(3 * n_peers, misc.at[cc], misc.at[cc], (x, y, 1 - cc))
        for cp in gathers + scatters + [pair]:
            cp.start()
        pair.wait()
        cmisc[chip] = misc[0] + misc[1]
        quad = [remote(3 * n_peers + 1 + j, cmisc.at[chip], cmisc.at[chip], (_flip(x, bx), _flip(y, by), cc))
                for j, (bx, by) in enumerate(((0, 1), (1, 0), (1, 1)))]
        for cp in quad:
            cp.start()
        for cp in scatters:
            cp.wait()
        group = wland[0]
        for b in range(1, N_DEV):
            group = group + wland[b]
        wsum[me] = group
        spreads = [remote(2 * n_peers + k, wsum.at[me], wsum.at[me], peer) for k, peer in enumerate(peers)]
        for cp in spreads:
            cp.start()
        for cp in gathers + quad + spreads:
            cp.wait()

        tot = cmisc[0]
        for k in range(1, N_CHIPS):
            tot = tot + cmisc[k]
        tot_ref[...] = tot
        gw_ref[...] = wsum[...]
        own_cols.wait()
        for b in range(N_DEV):
            dmod_ref[b:b + 1, :] = dbuf[b]

    operands = [parts[n] for n in part_names]
    n_sems = 3 * n_peers + 1 + (N_CHIPS - 1)
    return pl.pallas_call(
        body, name="small_grads_sum",
        out_shape=(jax.ShapeDtypeStruct((MISC_ROWS, D_MODEL), F32), jax.ShapeDtypeStruct((SGU_GROUPS, BLOCK, BLOCK), F32),
                   jax.ShapeDtypeStruct((N_DEV, ADA_SHARD), F32)),
        in_specs=[VMEM] * len(operands) + [ANY], out_specs=(VMEM, VMEM, VMEM),
        scratch_shapes=[pltpu.VMEM((2, MISC_ROWS, D_MODEL), F32), pltpu.VMEM((N_CHIPS, MISC_ROWS, D_MODEL), F32),
                        pltpu.VMEM((1, 3 * D_MODEL), F32), pltpu.VMEM((N_DEV, 1, ADA_SHARD), F32),
                        pltpu.VMEM((N_DEV, BLOCK, BLOCK), F32), pltpu.VMEM((SGU_GROUPS, BLOCK, BLOCK), F32),
                        pltpu.SemaphoreType.DMA, pltpu.SemaphoreType.DMA((n_sems,)), pltpu.SemaphoreType.DMA((n_sems,))],
        compiler_params=_params(),
    )(*operands, after)


def _small_adamw(tot, gw, small):
    names = ["norm_g", "b_ada", "attn_sinks", "sgu_ln_g", "sgu_ln_b", "sgu_w", "sgu_b", "final_g"]

    def body(*refs):
        tot_ref, gw_ref = refs[:2]
        wmv = {n: refs[2 + 3 * i: 5 + 3 * i] for i, n in enumerate(names)}
        outs = refs[2 + 3 * len(names):]
        res = {n: outs[4 * i: 4 * i + 4] for i, n in enumerate(names)}
        outs[4 * len(names)][...] = tot_ref[ROW_LOSS:ROW_LOSS + 1, 0:1]

        def update(name, g, pick=lambda r: r[...], put=None):
            w_ref, m_ref, v_ref = wmv[name]
            delta, m2, v2 = _adamw(pick(w_ref), g, pick(m_ref), pick(v_ref))
            for ref, val in zip(res[name], (g, delta, m2, v2)):
                if put is None:
                    ref[...] = val
                else:
                    put(ref, val)

        update("norm_g", tot_ref[ROW_NORM_G:ROW_NORM_G + 1, :])
        update("final_g", tot_ref[ROW_FINAL_G:ROW_FINAL_G + 1, :])
        for t in range(3):
            cols = slice(D_MODEL * t, D_MODEL * (t + 1))

            def put_cols(ref, val, cols=cols):
                ref[:, cols] = val
            update("b_ada", tot_ref[ROW_DMOD + t:ROW_DMOD + t + 1, :], pick=lambda r, cols=cols: r[:, cols], put=put_cols)
        update("sgu_ln_g", tot_ref[ROW_LN:ROW_LN + 1, 0:D_SGU])
        update("sgu_ln_b", tot_ref[ROW_LN:ROW_LN + 1, D_SGU:2 * D_SGU])
        update("attn_sinks", tot_ref[ROW_SINKS:ROW_SINKS + 1, 0:N_Q_HEADS])
        update("sgu_b", tot_ref[ROW_SGU_B:ROW_SGU_B + SGU_GROUPS, 0:BLOCK])
        for g in range(SGU_GROUPS):
            def put_group(ref, val, g=g):
                ref[g] = val
            update("sgu_w", gw_ref[g], pick=lambda r, g=g: r[g], put=put_group)

    shapes = {"norm_g": (1, D_MODEL), "b_ada": (1, 3 * D_MODEL), "attn_sinks": (1, N_Q_HEADS), "sgu_ln_g": (1, D_SGU),
              "sgu_ln_b": (1, D_SGU), "sgu_w": (SGU_GROUPS, BLOCK, BLOCK), "sgu_b": (SGU_GROUPS, BLOCK), "final_g": (1, D_MODEL)}
    operands = [tot, gw]
    for n in names:
        operands += [a.reshape(shapes[n]) for a in small[n]]
    out_shape = []
    for n in names:
        out_shape += [jax.ShapeDtypeStruct(shapes[n], F32)] * 4
    out_shape.append(jax.ShapeDtypeStruct((1, 1), F32))
    outs = pl.pallas_call(
        body, name="small_adamw", out_shape=tuple(out_shape),
        in_specs=[VMEM] * len(operands), out_specs=tuple([VMEM] * len(out_shape)), compiler_params=_params(),
    )(*operands)
    return {n: outs[4 * i: 4 * i + 4] for i, n in enumerate(names)}, outs[4 * len(names)]


def _cast_into_gathered(w, tile, place, name, after):
    r, c = w.shape
    nblk = r // tile

    def body(place_ref, w_ref, after_ref, o_ref):
        o_ref[...] = w_ref[...].astype(BF16)

    return pl.pallas_call(
        body, name=name,
        grid_spec=pltpu.PrefetchScalarGridSpec(
            num_scalar_prefetch=1, grid=(nblk,), in_specs=[pl.BlockSpec((tile, c), lambda i, p: (i, 0)), ANY],
            out_specs=pl.BlockSpec((tile, c), lambda i, p: (p[1] * nblk + i, 0))),
        out_shape=jax.ShapeDtypeStruct((N_CHIPS * r, c), BF16),
        compiler_params=_params(("parallel",)),
    )(place, w, after)


def _in_proj_odd(h, win_t, z, after):
    s, d = h.shape
    tm = min(1024, s)
    n_odd = win_t.shape[0] // IN_TILE // 2

    def body(h_ref, w_ref, z_in, after_ref, z_ref):
        z_ref[...] = _dot(h_ref[...], w_ref[...], NT)

    return pl.pallas_call(
        body, name="in_proj_odd", grid=(s // tm, n_odd), out_shape=jax.ShapeDtypeStruct(z.shape, z.dtype),
        in_specs=[pl.BlockSpec((tm, d), lambda i, j: (i, 0)), pl.BlockSpec((IN_TILE, d), lambda i, j: (2 * j + 1, 0)), ANY, ANY],
        out_specs=pl.BlockSpec((tm, IN_TILE), lambda i, j: (i, 2 * j + 1)), input_output_aliases={2: 0},
        compiler_params=_params(("parallel", "parallel")),
    )(h, win_t, z, after)


def _matmul_nt(a, bt, tm, out_dtype, name, after):
    m, k = a.shape
    n = bt.shape[0]
    tm = min(tm, m)

    def body(a_ref, b_ref, after_ref, o_ref):
        o_ref[...] = _dot(a_ref[...], b_ref[...], NT).astype(out_dtype)

    return pl.pallas_call(
        body, name=name, grid=(m // tm,), out_shape=jax.ShapeDtypeStruct((m, n), out_dtype),
        in_specs=[pl.BlockSpec((tm, k), lambda i: (i, 0)), pl.BlockSpec((n, k), lambda i: (0, 0), pipeline_mode=pl.Buffered(1)), ANY],
        out_specs=pl.BlockSpec((tm, n), lambda i: (i, 0)),
        compiler_params=_params(("parallel",)),
    )(a, bt, after)


def _matmul_tn(a, b, tm, name):
    t, m = a.shape
    n = b.shape[1]
    tm = min(tm, m)

    def body(a_ref, b_ref, o_ref):
        o_ref[...] = _dot(a_ref[...], b_ref[...], TN).astype(BF16)

    return pl.pallas_call(
        body, name=name, grid=(m // tm,), out_shape=jax.ShapeDtypeStruct((m, n), BF16),
        in_specs=[pl.BlockSpec((t, tm), lambda i: (0, i)), pl.BlockSpec((t, n), lambda i: (0, 0))],
        out_specs=pl.BlockSpec((tm, n), lambda i: (i, 0)),
        compiler_params=_params(("parallel",)),
    )(a, b)


def _grad_matmul(a, b, half, tm, name, add=None):
    t, m = a.shape
    tm = min(tm, m)

    def body(half_ref, a_ref, b_ref, *rest):
        if add is None:
            (o_ref,) = rest
            o_ref[...] = _dot(a_ref[...], b_ref[...], TN).astype(BF16)
        else:
            add_ref, o_ref = rest
            o_ref[...] = (_dot(a_ref[...], b_ref[...], TN) + add_ref[...].astype(F32)).astype(BF16)

    in_specs = [pl.BlockSpec((t, tm), lambda i, h: (0, i)), pl.BlockSpec((t, HALF), lambda i, h: (0, h[0]))]
    operands = [a, b]
    if add is not None:
        in_specs.append(pl.BlockSpec((tm, HALF), lambda i, h: (i, 0)))
        operands.append(add)
    return pl.pallas_call(
        body, name=name,
        grid_spec=pltpu.PrefetchScalarGridSpec(num_scalar_prefetch=1, grid=(m // tm,), in_specs=in_specs,
                                               out_specs=pl.BlockSpec((tm, HALF), lambda i, h: (i, 0))),
        out_shape=jax.ShapeDtypeStruct((m, HALF), BF16),
        compiler_params=_params(("parallel",)),
    )(half, *operands)


def _mixer_masks(n):
    key = lax.broadcasted_iota(jnp.int32, (2 * BLOCK, Q_PER_KV * BLOCK), 0)
    qry = lax.broadcasted_iota(jnp.int32, (2 * BLOCK, Q_PER_KV * BLOCK), 1) & (BLOCK - 1)
    valid = (key > qry) & (key <= qry + BLOCK) & ((key >= BLOCK) | (n > 0))
    r2 = lax.broadcasted_iota(jnp.int32, (BLOCK, BLOCK), 0)
    c2 = lax.broadcasted_iota(jnp.int32, (BLOCK, BLOCK), 1)
    return valid, r2 >= c2


def _layer_norm_parts(vs):
    mu = jnp.mean(vs, axis=-1, keepdims=True)
    xc = vs - mu
    rstd = lax.rsqrt(jnp.mean(xc * xc, axis=-1, keepdims=True) + EPS)
    return xc * rstd, rstd


def _stacked_queries(z_ref, hk, scale=None):
    q = jnp.concatenate([z_ref[:, _head_cols(Q0, Q_PER_KV * hk + g)] for g in range(Q_PER_KV)], axis=0)
    return (q if scale is None else q * scale).astype(BF16)


def _head_cols(base, h):
    return slice(base + HEAD_DIM * h, base + HEAD_DIM * (h + 1))


def _sink_row(sink_ref, hk):
    return jnp.concatenate([jnp.broadcast_to(sink_ref[:, h:h + 1], (1, BLOCK))
                            for h in range(Q_PER_KV * hk, Q_PER_KV * (hk + 1))], axis=1)


def _attn_probs(qs, kh, valid, sk, scale=None):
    s = _dot(kh, qs, NT)
    s = jnp.where(valid, s if scale is None else s * scale, -jnp.inf)
    m = jnp.maximum(jnp.max(s, axis=0, keepdims=True), sk)
    e = jnp.exp(s - m)
    es = jnp.exp(sk - m)
    inv = 1.0 / (jnp.sum(e, axis=0, keepdims=True) + es)
    return e * inv, es * inv


def _kv_ext(z_ref, kvp_ref):
    k_ext = jnp.concatenate([kvp_ref[:, 0:D_KV], z_ref[:, K0:K0 + D_KV]], axis=0).astype(BF16)
    v_ext = jnp.concatenate([kvp_ref[:, D_KV:2 * D_KV], z_ref[:, V0:V0 + D_KV]], axis=0).astype(BF16)
    return k_ext, v_ext


def _sgu_mixed(w_ref, bt_ref, vn, tril):
    ws = [jnp.where(tril, w_ref[g], 0.0).astype(BF16) for g in range(SGU_GROUPS)]
    mixed = jnp.concatenate([_dot(ws[g], vn[:, GROUP_DIM * g:GROUP_DIM * (g + 1)], NN) + bt_ref[:, g:g + 1]
                             for g in range(SGU_GROUPS)], axis=1)
    return ws, mixed


def _mixer_fwd(z, sinks, ln_g, ln_b, sgu_w, sgu_bt):
    s = z.shape[0]
    nb = s // BLOCK

    def body(z_ref, kvp_ref, sink_ref, lg_ref, lb_ref, w_ref, bt_ref, cat_ref, o_ref):
        n = pl.program_id(0)
        valid, tril = _mixer_masks(n)
        k_ext, v_ext = _kv_ext(z_ref, kvp_ref)
        for hk in range(2):
            kv = slice(HEAD_DIM * hk, HEAD_DIM * (hk + 1))
            p, _ = _attn_probs(_stacked_queries(z_ref, hk), k_ext[:, kv], valid, _sink_row(sink_ref, hk), ATTN_SCALE)
            o_t = _dot(v_ext[:, kv], p.astype(BF16), TN)
            for g in range(Q_PER_KV):
                o_ref[:, _head_cols(0, Q_PER_KV * hk + g)] = o_t[:, BLOCK * g:BLOCK * (g + 1)].T
        ga = z_ref[:, GA0:GA0 + D_ATTN]
        cat_ref[:, 0:D_ATTN] = (o_ref[...] * (ga * _sigmoid(ga))).astype(BF16)
        xhat, _ = _layer_norm_parts(z_ref[:, VS0:VS0 + D_SGU])
        vn = (xhat * lg_ref[...] + lb_ref[...]).astype(BF16)
        _, mixed = _sgu_mixed(w_ref, bt_ref, vn, tril)
        gt = z_ref[:, GS0:GS0 + D_SGU]
        cat_ref[:, D_ATTN:D_MODEL] = ((z_ref[:, U0:U0 + D_SGU] * mixed) * (gt * _sigmoid(gt))).astype(BF16)

    const2 = lambda n: (0, 0)
    return pl.pallas_call(
        body, name="mixer_fwd", grid=(nb,),
        out_shape=(jax.ShapeDtypeStruct((s, D_MODEL), BF16), jax.ShapeDtypeStruct((s, D_ATTN), F32)),
        in_specs=[pl.BlockSpec((BLOCK, D_IN), lambda n: (n, 0)),
                  pl.BlockSpec((BLOCK, 2 * D_KV), lambda n: (jnp.maximum(n - 1, 0), K0 // (2 * D_KV))),
                  pl.BlockSpec((1, N_Q_HEADS), const2), pl.BlockSpec((1, D_SGU), const2), pl.BlockSpec((1, D_SGU), const2),
                  pl.BlockSpec((SGU_GROUPS, BLOCK, BLOCK), lambda n: (0, 0, 0)), pl.BlockSpec((BLOCK, SGU_GROUPS), const2)],
        out_specs=(pl.BlockSpec((BLOCK, D_MODEL), lambda n: (n, 0)), pl.BlockSpec((BLOCK, D_ATTN), lambda n: (n, 0))),
        compiler_params=_params(("parallel",)),
    )(z, z, sinks, ln_g, ln_b, sgu_w, sgu_bt)


def _out_proj_loss(cat, wout, x, target, mod, final_g):
    s, d = x.shape
    tm = min(256, s)

    def body(cat_ref, w_ref, x_ref, t_ref, mod_ref, fg_ref, dx1_ref, dy_ref, dfg_ref, dgate_ref, loss_ref):
        @pl.when(pl.program_id(0) == 0)
        def _():
            dfg_ref[...] = jnp.zeros_like(dfg_ref)
            dgate_ref[...] = jnp.zeros_like(dgate_ref)
            loss_ref[...] = jnp.zeros_like(loss_ref)

        gate = mod_ref[:, 2 * d:3 * d]
        fg = fg_ref[...]
        yv = _dot(cat_ref[...], w_ref[...], NN)
        x1 = x_ref[...] + gate * yv
        r = lax.rsqrt(jnp.mean(x1 * x1, axis=-1, keepdims=True) + EPS)
        xr = x1 * r
        err = xr * fg - t_ref[...]
        dout = err * (1.0 / d)
        gy = dout * fg
        dx1 = r * gy - xr * (r * r * jnp.mean(gy * x1, axis=-1, keepdims=True))
        dx1_ref[...] = dx1
        dy_ref[...] = (gate * dx1).astype(BF16)
        dfg_ref[...] += jnp.sum(dout * xr, axis=0, keepdims=True)
        dgate_ref[...] += jnp.sum(dx1 * yv, axis=0, keepdims=True)
        loss_ref[...] += 0.5 * jnp.sum(jnp.mean(err * err, axis=-1, keepdims=True), axis=0, keepdims=True)

    row = lambda i: (i, 0)
    const = lambda i: (0, 0)
    return pl.pallas_call(
        body, name="out_proj_loss", grid=(s // tm,),
        out_shape=(jax.ShapeDtypeStruct((s, d), F32), jax.ShapeDtypeStruct((s, d), BF16), jax.ShapeDtypeStruct((1, d), F32),
                   jax.ShapeDtypeStruct((1, d), F32), jax.ShapeDtypeStruct((1, 1), F32)),
        in_specs=[pl.BlockSpec((tm, d), row), pl.BlockSpec((d, d), const), pl.BlockSpec((tm, d), row), pl.BlockSpec((tm, d), row),
                  pl.BlockSpec((1, 3 * d), const), pl.BlockSpec((1, d), const)],
        out_specs=(pl.BlockSpec((tm, d), row), pl.BlockSpec((tm, d), row), pl.BlockSpec((1, d), const), pl.BlockSpec((1, d), const),
                   pl.BlockSpec((1, 1), const)),
        compiler_params=_params(("arbitrary",)),
    )(cat, wout, x, target, mod, final_g)


def _mixer_bwd(z, dcat, o, sinks, ln_g, ln_b, sgu_w, sgu_bt):
    s = z.shape[0]
    nb = s // BLOCK

    def body(z_ref, kvp_ref, dc_ref, o_ref, sink_ref, lg_ref, lb_ref, w_ref, bt_ref,
             dz_ref, dsink_ref, dlg_ref, dlb_ref, dw_ref, db_ref, pend, dkv, carry, dsink_acc, dbt_acc):
        n = pl.program_id(0)

        @pl.when(n == 0)
        def _():
            pend[...] = jnp.zeros_like(pend)
            carry[...] = jnp.zeros_like(carry)
            dsink_acc[...] = jnp.zeros_like(dsink_acc)
            dbt_acc[...] = jnp.zeros_like(dbt_acc)
            dlg_ref[...] = jnp.zeros_like(dlg_ref)
            dlb_ref[...] = jnp.zeros_like(dlb_ref)
            dw_ref[...] = jnp.zeros_like(dw_ref)

        dz_ref[:, 0:K0] = pend[:, 0:K0]
        dz_ref[:, GA0:D_IN] = pend[:, GA0:D_IN]

        @pl.when(n == nb)
        def _():
            dkv[...] = jnp.zeros_like(dkv)

        @pl.when(n < nb)
        def _():
            valid, tril = _mixer_masks(n)
            k_ext, v_ext = _kv_ext(z_ref, kvp_ref)
            ga = z_ref[:, GA0:GA0 + D_ATTN]
            sg = _sigmoid(ga)
            datt = dc_ref[:, 0:D_ATTN]
            pend[:, GA0:GA0 + D_ATTN] = (datt * o_ref[...] * (sg * (1.0 + ga * (1.0 - sg)))).astype(BF16)
            do_all = (datt * (ga * sg)).astype(BF16)
            for hk in range(2):
                kv = slice(HEAD_DIM * hk, HEAD_DIM * (hk + 1))
                kh, vh = k_ext[:, kv], v_ext[:, kv]
                qs = _stacked_queries(z_ref, hk, ATTN_SCALE)
                p, p_sink = _attn_probs(qs, kh, valid, _sink_row(sink_ref, hk))
                do = jnp.concatenate([do_all[:, _head_cols(0, Q_PER_KV * hk + g)] for g in range(Q_PER_KV)], axis=0)
                dp = _dot(vh, do, NT)
                delta = jnp.sum(p * dp, axis=0, keepdims=True)
                ds = (p * (dp - delta)).astype(BF16)
                dsk = -p_sink * delta
                dq_t = _dot(kh, ds, TN) * ATTN_SCALE
                for g in range(Q_PER_KV):
                    h = Q_PER_KV * hk + g
                    dsink_acc[:, h:h + 1] += jnp.sum(dsk[:, BLOCK * g:BLOCK * (g + 1)], axis=1, keepdims=True)
                    pend[:, _head_cols(Q0, h)] = dq_t[:, BLOCK * g:BLOCK * (g + 1)].T.astype(BF16)
                dkv[:, kv] = _dot(ds, qs, NN)
                dkv[:, D_KV + HEAD_DIM * hk:D_KV + HEAD_DIM * (hk + 1)] = _dot(p.astype(BF16), do, NN)

            xhat, rstd = _layer_norm_parts(z_ref[:, VS0:VS0 + D_SGU])
            lg = lg_ref[...]
            vn = (xhat * lg + lb_ref[...]).astype(BF16)
            ws, mixed = _sgu_mixed(w_ref, bt_ref, vn, tril)
            u = z_ref[:, U0:U0 + D_SGU]
            gt = z_ref[:, GS0:GS0 + D_SGU]
            sg = _sigmoid(gt)
            dsgu = dc_ref[:, D_ATTN:D_MODEL]
            dgated = dsgu * (gt * sg)
            pend[:, U0:U0 + D_SGU] = (dgated * mixed).astype(BF16)
            pend[:, GS0:GS0 + D_SGU] = (dsgu * (u * mixed) * (sg * (1.0 + gt * (1.0 - sg)))).astype(BF16)
            dmixed = dgated * u
            dmb = dmixed.astype(BF16)
            dvn_parts = []
            for g in range(SGU_GROUPS):
                gs_ = slice(GROUP_DIM * g, GROUP_DIM * (g + 1))
                dw_ref[g] += jnp.where(tril, _dot(dmb[:, gs_], vn[:, gs_], NT), 0.0)
                dbt_acc[:, g:g + 1] += jnp.sum(dmixed[:, gs_], axis=-1, keepdims=True)
                dvn_parts.append(_dot(ws[g], dmb[:, gs_], TN))
            dvn = jnp.concatenate(dvn_parts, axis=1)
            dlg_ref[...] += jnp.sum(dvn * xhat, axis=0, keepdims=True)
            dlb_ref[...] += jnp.sum(dvn, axis=0, keepdims=True)
            dxh = dvn * lg
            pend[:, VS0:VS0 + D_SGU] = (rstd * (dxh - jnp.mean(dxh, axis=-1, keepdims=True)
                                                - xhat * jnp.mean(dxh * xhat, axis=-1, keepdims=True))).astype(BF16)

        dz_ref[:, K0:GA0] = (carry[...] + dkv[0:BLOCK, :]).astype(BF16)
        carry[...] = dkv[BLOCK:2 * BLOCK, :]

        @pl.when(n == nb)
        def _():
            dsink_ref[...] = dsink_acc[:, 0:N_Q_HEADS]
            db_ref[...] = dbt_acc[...].T[0:SGU_GROUPS, :]

    cur = lambda n: (jnp.minimum(n, nb - 1), 0)
    const2 = lambda n: (0, 0)
    return pl.pallas_call(
        body, name="mixer_bwd", grid=(nb + 1,),
        out_shape=(jax.ShapeDtypeStruct((s, D_IN), BF16), jax.ShapeDtypeStruct((1, N_Q_HEADS), F32),
                   jax.ShapeDtypeStruct((1, D_SGU), F32), jax.ShapeDtypeStruct((1, D_SGU), F32),
                   jax.ShapeDtypeStruct((SGU_GROUPS, BLOCK, BLOCK), F32), jax.ShapeDtypeStruct((SGU_GROUPS, BLOCK), F32)),
        in_specs=[pl.BlockSpec((BLOCK, D_IN), cur),
                  pl.BlockSpec((BLOCK, 2 * D_KV), lambda n: (jnp.maximum(jnp.minimum(n, nb - 1) - 1, 0), K0 // (2 * D_KV))),
                  pl.BlockSpec((BLOCK, D_MODEL), cur), pl.BlockSpec((BLOCK, D_ATTN), cur),
                  pl.BlockSpec((1, N_Q_HEADS), const2), pl.BlockSpec((1, D_SGU), const2), pl.BlockSpec((1, D_SGU), const2),
                  pl.BlockSpec((SGU_GROUPS, BLOCK, BLOCK), lambda n: (0, 0, 0)), pl.BlockSpec((BLOCK, SGU_GROUPS), const2)],
        out_specs=(pl.BlockSpec((BLOCK, D_IN), lambda n: (jnp.maximum(n - 1, 0), 0)),
                   pl.BlockSpec((1, N_Q_HEADS), const2), pl.BlockSpec((1, D_SGU), const2), pl.BlockSpec((1, D_SGU), const2),
                   pl.BlockSpec((SGU_GROUPS, BLOCK, BLOCK), lambda n: (0, 0, 0)), pl.BlockSpec((SGU_GROUPS, BLOCK), const2)),
        scratch_shapes=[pltpu.VMEM((BLOCK, D_IN), BF16), pltpu.VMEM((2 * BLOCK, 2 * D_KV), F32), pltpu.VMEM((BLOCK, 2 * D_KV), F32),
                        pltpu.VMEM((1, BLOCK), F32), pltpu.VMEM((BLOCK, BLOCK), F32)],
        compiler_params=_params(("arbitrary",)),
    )(z, z, dcat, o, sinks, ln_g, ln_b, sgu_w, sgu_bt)


def _in_proj_bwd(dz, win_t, x, dx1, norm_g, mod):
    s, d = x.shape
    kdim = dz.shape[1]
    tm = min(256, s)

    def body(dz_ref, w_ref, x_ref, dx1_ref, g_ref, mod_ref, gx_ref, dshift_ref, dscale_ref, dng_ref):
        @pl.when(pl.program_id(0) == 0)
        def _():
            dshift_ref[...] = jnp.zeros_like(dshift_ref)
            dscale_ref[...] = jnp.zeros_like(dscale_ref)
            dng_ref[...] = jnp.zeros_like(dng_ref)

        dh = _dot(dz_ref[...], w_ref[...], NN)
        xv = x_ref[...]
        ng = g_ref[...]
        r = lax.rsqrt(jnp.mean(xv * xv, axis=-1, keepdims=True) + EPS)
        xn = xv * r
        dshift_ref[...] += jnp.sum(dh, axis=0, keepdims=True)
        dscale_ref[...] += jnp.sum(dh * (xn * ng), axis=0, keepdims=True)
        dhs = dh * (1.0 + mod_ref[:, d:2 * d])
        dng_ref[...] += jnp.sum(dhs * xn, axis=0, keepdims=True)
        dxn = dhs * ng
        gx_ref[...] = dx1_ref[...] + r * dxn - xn * (r * r * jnp.mean(dxn * xv, axis=-1, keepdims=True))

    row = lambda i: (i, 0)
    const = lambda i: (0, 0)
    return pl.pallas_call(
        body, name="in_proj_bwd", grid=(s // tm,),
        out_shape=(jax.ShapeDtypeStruct((s, d), F32),) + (jax.ShapeDtypeStruct((1, d), F32),) * 3,
        in_specs=[pl.BlockSpec((tm, kdim), row), pl.BlockSpec((kdim, d), const, pipeline_mode=pl.Buffered(1)),
                  pl.BlockSpec((tm, d), row), pl.BlockSpec((tm, d), row), pl.BlockSpec((1, d), const), pl.BlockSpec((1, 3 * d), const)],
        out_specs=(pl.BlockSpec((tm, d), row),) + (pl.BlockSpec((1, d), const),) * 3,
        compiler_params=_params(("arbitrary",)),
    )(dz, win_t, x, dx1, norm_g, mod)


def _sum_slots(slots, own, place, rows_tile, full_cols, name):
    n_slots, r, hcols = slots.shape
    nblk = r // rows_tile
    per_device = n_slots == N_DEV

    def body(place_ref, s_ref, own_ref, o_ref):
        mine = 2 * place_ref[1] + place_ref[0] if per_device else place_ref[1]
        acc = jnp.zeros((rows_tile, hcols), F32)
        for k in range(n_slots):
            acc = acc + jnp.where(mine == k, own_ref[...], s_ref[k]).astype(F32)
        o_ref[...] = acc

    return pl.pallas_call(
        body, name=name,
        grid_spec=pltpu.PrefetchScalarGridSpec(
            num_scalar_prefetch=1, grid=(nblk,),
            in_specs=[pl.BlockSpec((n_slots, rows_tile, hcols), lambda i, p: (0, i, 0)),
                      pl.BlockSpec((rows_tile, hcols), lambda i, p: (p[1] * nblk + i, p[0] if per_device else 0))],
            out_specs=pl.BlockSpec((rows_tile, hcols), lambda i, p: (i, p[0]))),
        out_shape=jax.ShapeDtypeStruct((r, full_cols), F32),
        compiler_params=_params(("parallel",)),
    )(place, slots, own)


def _adamw_rows(w, g, m, v, tile, name, after):
    r, c = w.shape

    def body(w_ref, g_ref, m_ref, v_ref, after_ref, go_ref, d_ref, mo_ref, vo_ref):
        g_ = g_ref[...]
        delta, m2, v2 = _adamw(w_ref[...], g_, m_ref[...], v_ref[...])
        go_ref[...] = g_
        d_ref[...] = delta
        mo_ref[...] = m2
        vo_ref[...] = v2

    blk = pl.BlockSpec((tile, c), lambda i: (i, 0))
    return pl.pallas_call(
        body, name=name, grid=(r // tile,), out_shape=(jax.ShapeDtypeStruct((r, c), F32),) * 4,
        in_specs=[blk] * 4 + [ANY], out_specs=(blk,) * 4,
        compiler_params=_params(("parallel",)),
    )(w, g, m, v, after)


def _adamw_ada(cact, dmod, w, m, v, tile=256):
    r, c = w.shape

    def body(c_ref, d_ref, w_ref, m_ref, v_ref, go_ref, dl_ref, mo_ref, vo_ref):
        g_ = _dot(c_ref[...].astype(BF16), d_ref[...].astype(BF16), TN)
        delta, m2, v2 = _adamw(w_ref[...], g_, m_ref[...], v_ref[...])
        go_ref[...] = g_
        dl_ref[...] = delta
        mo_ref[...] = m2
        vo_ref[...] = v2

    blk = pl.BlockSpec((tile, c), lambda i: (i, 0))
    return pl.pallas_call(
        body, name="adamw_ada", grid=(r // tile,), out_shape=(jax.ShapeDtypeStruct((r, c), F32),) * 4,
        in_specs=[pl.BlockSpec((N_DEV, tile), lambda i: (0, i)), pl.BlockSpec((N_DEV, c), lambda i: (0, 0)), blk, blk, blk],
        out_specs=(blk,) * 4,
        compiler_params=_params(("parallel",)),
    )(cact, dmod, w, m, v)


def kernel(x, c, norm_g, w_ada, b_ada, w_in, attn_sinks, sgu_ln_g, sgu_ln_b, sgu_w, sgu_b, w_out, final_g, loss_target, m_norm_g, m_w_ada, m_b_ada, m_w_in, m_attn_sinks, m_sgu_ln_g, m_sgu_ln_b, m_sgu_w, m_sgu_b, m_w_out, m_final_g, v_norm_g, v_w_ada, v_b_ada, v_w_in, v_attn_sinks, v_sgu_ln_g, v_sgu_ln_b, v_sgu_w, v_sgu_b, v_w_out, v_final_g):
    xs, tgt = x[0], loss_target[0]
    core = lax.axis_index("c").astype(jnp.int32).reshape(1)
    chip = 2 * lax.axis_index("x") + lax.axis_index("y")
    place = jnp.stack([core[0], chip.astype(jnp.int32)])
    sgu_bt = sgu_b[0].T

    w_in_t, m_in_t, v_in_t = (jnp.swapaxes(a[0], 0, 1) for a in (w_in, m_w_in, v_w_in))
    win_t = _cast_into_gathered(w_in_t, IN_SHARD // 4, place, "cast_w_in", c)
    win_t, mod, cact, h, z = _gather_w_in_modulate_project(win_t, c, w_ada[0], b_ada, xs, norm_g)
    wout = _cast_into_gathered(w_out[0], OUT_SHARD // 2, place, "cast_w_out", win_t)
    sems, bufs, token = _split_start(_gather_copies, [wout], 6, "w_out_gather_start", win_t)
    z = _in_proj_odd(h, win_t, z, token)
    cat, o = _mixer_fwd(z, attn_sinks, sgu_ln_g, sgu_ln_b, sgu_w[0], sgu_bt)
    wout = _split_wait(_gather_copies, sems, bufs, cat, "w_out_gather_wait")[0]
    dx1, dy, dfg, dgate, loss = _out_proj_loss(cat, wout, xs, tgt, mod, final_g.reshape(1, D_MODEL))

    po = _matmul_tn(cat, dy, 512, "w_out_grad")
    sems, bufs, token = _split_start(_owner_copies, [po, lax.empty((N_DEV, OUT_SHARD, HALF), BF16)], N_DEV - 1,
                                     "w_out_grad_to_owners_start", cat)
    dcat = _matmul_nt(dy, wout, 256, F32, "out_proj_bwd", token)
    dz, dsinks, dlg, dlb, dsgu_w, dsgu_b = _mixer_bwd(z, dcat, o, attn_sinks, sgu_ln_g, sgu_ln_b, sgu_w[0], sgu_bt)
    po, slots_o = _split_wait(_owner_copies, sems, bufs, dz, "w_out_grad_to_owners_wait")
    go = _halves_swap(_sum_slots(slots_o, po, place, OUT_SHARD // 2, D_MODEL, "w_out_grad_sum"), "w_out_grad_halves_swap")

    sw = _to_sibling(_grad_matmul(dz, h, 1 - core, 768, "w_in_grad_sibling"), "w_in_grad_to_sibling")
    pw = _grad_matmul(dz, h, core, 768, "w_in_grad_own", add=sw)
    first = functools.partial(_chip_copies, [(0, 0, IN_SHARD), (1, 0, IN_SHARD), (2, 0, IN_SHARD // 4)])
    rest = functools.partial(_chip_copies, [(2, IN_SHARD // 4, 3 * IN_SHARD // 4)])
    sems, bufs, token = _split_start(first, [pw, lax.empty((N_CHIPS, IN_SHARD, HALF), BF16)], 3, "w_in_grad_to_chips_start", sw)
    gx, dshift, dscale, dng = _in_proj_bwd(dz, win_t, xs, dx1, norm_g + token[0:1, 0:1], mod)
    bufs = _split_wait(first, sems, bufs, gx, "w_in_grad_to_chips_wait")

    parts = dict(dng=dng, dshift=dshift, dscale=dscale, dgate=dgate, dfg=dfg, dlg=dlg, dlb=dlb, dsinks=dsinks,
                 dsgu_b=dsgu_b, dsgu_w=dsgu_w, loss=loss)
    small = dict(norm_g=(norm_g, m_norm_g, v_norm_g), b_ada=(b_ada, m_b_ada, v_b_ada),
                 attn_sinks=(attn_sinks, m_attn_sinks, v_attn_sinks), sgu_ln_g=(sgu_ln_g, m_sgu_ln_g, v_sgu_ln_g),
                 sgu_ln_b=(sgu_ln_b, m_sgu_ln_b, v_sgu_ln_b), sgu_w=(sgu_w, m_sgu_w, v_sgu_w), sgu_b=(sgu_b, m_sgu_b, v_sgu_b),
                 final_g=(final_g, m_final_g, v_final_g))
    tot, gsgu_w, dmod_mine = _small_grads_sum(parts, bufs[1])
    sems, bufs, token = _split_start(rest, list(bufs), 1, "w_in_grad_rest_to_chips_start", tot)
    res, total = _small_adamw(tot, gsgu_w, small)
    res["w_ada"] = _adamw_ada(cact, dmod_mine + token[0:1, 0:1], w_ada[0], m_w_ada[0], v_w_ada[0])
    res["w_out"] = _adamw_rows(w_out[0], go, m_w_out[0], v_w_out[0], 256, "adamw_w_out", res["w_ada"][1])

    pw, slots_w = _split_wait(rest, sems, bufs, res["w_out"][1], "w_in_grad_rest_to_chips_wait")
    gw_t = _halves_swap(_sum_slots(slots_w, pw, place, IN_SHARD // 4, D_MODEL, "w_in_grad_sum"), "w_in_grad_halves_swap")
    res["w_in"] = tuple(jnp.swapaxes(a, 0, 1) for a in _adamw_rows(w_in_t, gw_t, m_in_t, v_in_t, IN_SHARD // 4, "adamw_w_in", gw_t))

    order = ["norm_g", "w_ada", "b_ada", "w_in", "attn_sinks", "sgu_ln_g", "sgu_ln_b", "sgu_w", "sgu_b", "w_out", "final_g"]
    like = dict(norm_g=norm_g, w_ada=w_ada, b_ada=b_ada, w_in=w_in, attn_sinks=attn_sinks, sgu_ln_g=sgu_ln_g, sgu_ln_b=sgu_ln_b,
                sgu_w=sgu_w, sgu_b=sgu_b, w_out=w_out, final_g=final_g)
    outs = [total[0, 0], gx[None]]
    for kind in range(4):
        outs += [res[n][kind].reshape(like[n].shape) for n in order]
    return tuple(outs)
```

```python
import functools
import math

import jax
import jax.numpy as jnp
from jax import lax
from jax.experimental import pallas as pl
from jax.experimental.pallas import tpu as pltpu

F32 = jnp.float32
BF16 = jnp.bfloat16

D_MODEL = 2048
D_ATTN = 1024
HEAD_DIM = 64
N_Q_HEADS = 16
Q_PER_KV = 8
D_KV = 128
BLOCK = 128
D_SGU = 1024
SGU_GROUPS = 8
GROUP_DIM = 128
D_IN = 5376
EPS = 1e-6
ATTN_SCALE = 1.0 / math.sqrt(HEAD_DIM)
N_CHIPS = 4
N_DEV = 8
IN_SHARD = D_IN // N_CHIPS
OUT_SHARD = D_MODEL // N_CHIPS
ADA_SHARD = 3 * D_MODEL // N_CHIPS
HALF = D_MODEL // 2
IN_TILE = 768

Q0, K0, V0, GA0, U0, VS0, GS0 = 0, 1024, 1152, 1280, 2304, 3328, 4352

ADAM_LR = 0.001
ADAM_B1 = 0.9
ADAM_B2 = 0.999
ADAM_EPS = 1e-08
ADAM_WD = 0.01
ADAM_STEP = 10

VMEM_LIMIT_V7X = 56 * 1024 * 1024

NN = (((1,), (0,)), ((), ()))
NT = (((1,), (1,)), ((), ()))
TN = (((0,), (0,)), ((), ()))
MESH = pl.DeviceIdType.MESH
ANY = pl.BlockSpec(memory_space=pl.ANY)
VMEM = pl.BlockSpec(memory_space=pltpu.VMEM)
HBM = pl.BlockSpec(memory_space=pltpu.HBM)
SEM = pl.BlockSpec(memory_space=pltpu.SEMAPHORE)
EFFECT = pltpu.SideEffectType.DATAFLOW_SIDE_EFFECTING


def _dot(a, b, dims):
    return lax.dot_general(a, b, dims, preferred_element_type=F32)


def _params(sem=None):
    return pltpu.CompilerParams(dimension_semantics=sem, vmem_limit_bytes=VMEM_LIMIT_V7X)


def _sigmoid(x):
    return jax.nn.sigmoid(x)


def _adamw(w, g, m, v):
    m = ADAM_B1 * m + (1.0 - ADAM_B1) * g
    v = ADAM_B2 * v + (1.0 - ADAM_B2) * (g * g)
    m_hat = m / (1.0 - ADAM_B1 ** ADAM_STEP)
    v_hat = v / (1.0 - ADAM_B2 ** ADAM_STEP)
    delta = -ADAM_LR * (m_hat / (jnp.sqrt(v_hat) + ADAM_EPS) + ADAM_WD * w)
    return delta, m, v


def _place():
    return lax.axis_index("x"), lax.axis_index("y"), lax.axis_index("c")


def _flip(p, bit):
    return 1 - p if bit else p


def _other_chips(x, y):
    return [(_flip(x, bx), _flip(y, by)) for bx, by in ((0, 1), (1, 0), (1, 1))]


def _split_start(copies, bufs, n_sems, name, after):
    n = len(bufs)

    def body(*refs):
        send_sems, recv_sems = refs[n + 1:n + 3]
        for cp in copies(*refs[:n], send_sems, recv_sems):
            cp.start()
        refs[-1][...] = jnp.zeros_like(refs[-1])

    bufs = [pltpu.with_memory_space_constraint(a, pltpu.HBM) for a in bufs]
    outs = pl.pallas_call(
        body, name=name,
        out_shape=(pltpu.SemaphoreType.DMA((n_sems,)), pltpu.SemaphoreType.DMA((n_sems,)))
        + tuple(pltpu.HBM(a.shape, a.dtype) for a in bufs) + (jax.ShapeDtypeStruct((8, 128), F32),),
        in_specs=[HBM] * n + [ANY], out_specs=(SEM, SEM) + (HBM,) * n + (VMEM,), input_output_aliases={k: k + 2 for k in range(n)},
        compiler_params=pltpu.CompilerParams(has_side_effects=EFFECT),
    )(*bufs, after)
    return outs[:2], list(outs[2:2 + n]), outs[-1]


def _split_wait(copies, sems, bufs, after, name):
    n = len(bufs)

    def body(*refs):
        send_sems, recv_sems = refs[n:n + 2]
        for cp in copies(*refs[:n], send_sems, recv_sems):
            cp.wait_send()
            cp.wait_recv()

    return pl.pallas_call(
        body, name=name, out_shape=tuple(pltpu.HBM(a.shape, a.dtype) for a in bufs),
        in_specs=[HBM] * n + [SEM, SEM, ANY], out_specs=(HBM,) * n, input_output_aliases={k: k for k in range(n)},
        compiler_params=pltpu.CompilerParams(has_side_effects=EFFECT),
    )(*bufs, *sems, after)


def _gather_copies(full_ref, send_sems, recv_sems):
    x, y, cc = _place()
    shard = full_ref.shape[0] // N_CHIPS
    half = shard // 2
    mine = full_ref.at[pl.ds(shard * (2 * x + y) + half * cc, half), :]
    return [pltpu.make_async_remote_copy(src_ref=mine, dst_ref=mine, send_sem=send_sems.at[j], recv_sem=recv_sems.at[j],
                                         device_id=(px, py, cc), device_id_type=MESH)
            for j, (px, py) in enumerate(_other_chips(x, y))]


def _gather_finish(full, name):
    shard = full.shape[0] // N_CHIPS
    half = shard // 2

    def body(full_in, full_ref, send_sems, recv_sems):
        x, y, cc = _place()
        cps = []
        for j, (px, py) in enumerate(_other_chips(x, y)):
            landed = full_ref.at[pl.ds(shard * (2 * px + py) + half * cc, half), :]
            cps.append(pltpu.make_async_remote_copy(src_ref=landed, dst_ref=landed, send_sem=send_sems.at[j], recv_sem=recv_sems.at[j],
                                                    device_id=(x, y, 1 - cc), device_id_type=MESH))
        for cp in cps:
            cp.start()
        for cp in cps:
            cp.wait()

    return pl.pallas_call(
        body, name=name, out_shape=jax.ShapeDtypeStruct(full.shape, full.dtype),
        in_specs=[ANY], out_specs=ANY, input_output_aliases={0: 0},
        scratch_shapes=[pltpu.SemaphoreType.DMA((3,)), pltpu.SemaphoreType.DMA((3,))],
        compiler_params=_params(),
    )(full)


def _gather_w_in_modulate_project(full, c, w_ada, b_ada, xs, norm_g):
    shard = full.shape[0] // N_CHIPS
    half = shard // 2
    quarter = half // 2
    kdim = w_ada.shape[0]
    kc = 256
    s, d = xs.shape
    rows_at_once = min(256, s)
    n_chunks = s // rows_at_once
    m_rows = min(512, s)
    n_m = s // m_rows

    def body(full_in, c_ref, w_hbm, b_ref, x_hbm, g_ref, full_ref, mod_ref, cact_ref, h_hbm, z_hbm,
             w_vmem, cbuf, pbuf, rbuf, h_vmem, xbuf, bbuf, zbuf, w_sem, x_sems, b_sem, z_sems, h_sem,
             send_sems, recv_sems, msend_sems, mrecv_sems):
        x, y, cc = _place()
        me = 4 * x + 2 * y + cc
        chip = 2 * x + y
        across_x, across_y, far = (1 - x, y), (x, 1 - y), (1 - x, 1 - y)

        def rows(chip_xy, core, q):
            return full_ref.at[pl.ds(shard * (2 * chip_xy[0] + chip_xy[1]) + half * core + quarter * q, quarter), :]

        def copy(k, piece, to):
            return pltpu.make_async_remote_copy(src_ref=piece, dst_ref=piece, send_sem=send_sems.at[k], recv_sem=recv_sems.at[k],
                                                device_id=to, device_id_type=MESH)

        def small(k, src, dst, to):
            return pltpu.make_async_remote_copy(src_ref=src, dst_ref=dst, send_sem=msend_sems.at[k], recv_sem=mrecv_sems.at[k],
                                                device_id=to, device_id_type=MESH)

        def x_chunk(i):
            return pltpu.make_async_copy(x_hbm.at[pl.ds(rows_at_once * i, rows_at_once), :], xbuf.at[i % 2], x_sems.at[i % 2])

        cbuf[me] = c_ref[...]
        gathers = [small(k - 1, cbuf.at[me], cbuf.at[me], (_flip(x, (k >> 2) & 1), _flip(y, (k >> 1) & 1), _flip(cc, k & 1)))
                   for k in range(1, N_DEV)]
        for cp in gathers:
            cp.start()
        w_load = pltpu.make_async_copy(w_hbm, w_vmem, w_sem)
        w_load.start()
        x_chunk(0).start()
        for cp in gathers:
            cp.wait()
        w_load.wait()
        for b in range(N_DEV):
            cv = cbuf[b]
            cact_ref[b:b + 1, :] = cv * _sigmoid(cv)
        acc = jnp.zeros((N_DEV, ADA_SHARD), F32)
        for k0 in range(0, kdim, kc):
            acc = acc + _dot(cact_ref[:, k0:k0 + kc].astype(BF16), w_vmem[k0:k0 + kc, :].astype(BF16), NN)
        for b in range(N_DEV):
            pbuf[b] = acc[b:b + 1, :]
        rbuf[chip] = pbuf[me]
        sends = [small(N_DEV - 1 + j, pbuf.at[4 * px + 2 * py + cc], rbuf.at[chip], (px, py, cc))
                 for j, (px, py) in enumerate(_other_chips(x, y))]
        for cp in sends:
            cp.start()
        direct = [copy(0, rows((x, y), cc, 0), (*across_x, cc)), copy(2, rows((x, y), cc, 1), (*across_y, cc)),
                  copy(1, rows((x, y), cc, 1), (*across_x, cc)), copy(3, rows((x, y), cc, 0), (*across_y, cc))]
        for cp in direct:
            cp.start()
        for cp in sends:
            cp.wait()
        for k in range(N_CHIPS):
            cols = slice(ADA_SHARD * k, ADA_SHARD * (k + 1))
            mod_ref[:, cols] = rbuf[k] + b_ref[:, cols]

        for i in range(n_chunks):
            x_chunk(i).wait()
            if i + 1 < n_chunks:
                x_chunk(i + 1).start()
            xv = xbuf[i % 2]
            r = lax.rsqrt(jnp.mean(xv * xv, axis=-1, keepdims=True) + EPS)
            h_vmem[rows_at_once * i:rows_at_once * (i + 1), :] = (
                (xv * r * g_ref[...]) * (1.0 + mod_ref[:, d:2 * d]) + mod_ref[:, 0:d]).astype(BF16)
        h_out = pltpu.make_async_copy(h_vmem, h_hbm, h_sem)
        h_out.start()

        def project(shard_index):
            first = pl.multiple_of(2 * IN_TILE * shard_index, 128)
            b_load = pltpu.make_async_copy(full_ref.at[pl.ds(first, IN_TILE), :], bbuf, b_sem)
            b_load.start()
            b_load.wait()
            outs = []
            for m in range(n_m):
                if m >= 2:
                    outs[m - 2].wait()
                zbuf[m % 2] = _dot(h_vmem[m_rows * m:m_rows * (m + 1), :], bbuf[...], NT)
                outs.append(pltpu.make_async_copy(zbuf.at[m % 2], z_hbm.at[pl.ds(m_rows * m, m_rows), pl.ds(first, IN_TILE)],
                                                  z_sems.at[m % 2]))
                outs[m].start()
            for cp in outs[max(n_m - 2, 0):]:
                cp.wait()

        project(chip)

        landing = [(0, across_x, 0, (4, (*across_y, cc))), (2, across_y, 1, (5, (*across_x, cc))),
                   (1, across_x, 1, None), (3, across_y, 0, None), (4, far, 0, None), (5, far, 1, None)]
        passed = []

        def land(n):
            k, whose, q, onward = landing[n]
            piece = rows(whose, cc, q)
            copy(k, piece, (x, y, cc)).wait_recv()
            if onward is not None:
                passed.append(copy(onward[0], piece, onward[1]))
                passed[-1].start()
            passed.append(copy(6 + n, piece, (x, y, 1 - cc)))
            passed[-1].start()

        def from_sibling(n):
            k, whose, q, onward = landing[n]
            copy(6 + n, rows(whose, 1 - cc, q), (x, y, cc)).wait_recv()

        for n in range(4):
            land(n)
        for n in range(4):
            from_sibling(n)
        project(2 * across_x[0] + across_x[1])
        project(2 * across_y[0] + across_y[1])
        land(4)
        land(5)
        from_sibling(4)
        from_sibling(5)
        project(2 * far[0] + far[1])
        h_out.wait()
        for cp in direct + passed:
            cp.wait_send()

    return pl.pallas_call(
        body, name="w_in_gather_modulate_project",
        out_shape=(jax.ShapeDtypeStruct(full.shape, full.dtype), jax.ShapeDtypeStruct((1, 3 * D_MODEL), F32),
                   jax.ShapeDtypeStruct((N_DEV, D_MODEL), F32), jax.ShapeDtypeStruct((s, d), BF16),
                   jax.ShapeDtypeStruct((s, full.shape[0]), F32)),
        in_specs=[ANY, VMEM, ANY, VMEM, ANY, VMEM], out_specs=(ANY, VMEM, VMEM, ANY, ANY), input_output_aliases={0: 0},
        scratch_shapes=[pltpu.VMEM(w_ada.shape, F32), pltpu.VMEM((N_DEV, 1, D_MODEL), F32), pltpu.VMEM((N_DEV, 1, ADA_SHARD), F32),
                        pltpu.VMEM((N_CHIPS, 1, ADA_SHARD), F32), pltpu.VMEM((s, d), BF16), pltpu.VMEM((2, rows_at_once, d), F32),
                        pltpu.VMEM((IN_TILE, d), BF16), pltpu.VMEM((2, m_rows, IN_TILE), F32),
                        pltpu.SemaphoreType.DMA, pltpu.SemaphoreType.DMA((2,)), pltpu.SemaphoreType.DMA, pltpu.SemaphoreType.DMA((2,)),
                        pltpu.SemaphoreType.DMA,
                        pltpu.SemaphoreType.DMA((12,)), pltpu.SemaphoreType.DMA((12,)),
                        pltpu.SemaphoreType.DMA((N_DEV + 2,)), pltpu.SemaphoreType.DMA((N_DEV + 2,))],
        compiler_params=_params(),
    )(full, c, w_ada, b_ada, xs, norm_g)


def _to_sibling(g, name):
    def body(g_ref, r_ref, send_sem, recv_sem):
        x, y, cc = _place()
        cp = pltpu.make_async_remote_copy(src_ref=g_ref, dst_ref=r_ref, send_sem=send_sem, recv_sem=recv_sem,
                                          device_id=(x, y, 1 - cc), device_id_type=MESH)
        cp.start()
        cp.wait()

    return pl.pallas_call(
        body, name=name, out_shape=jax.ShapeDtypeStruct(g.shape, g.dtype), in_specs=[ANY], out_specs=ANY,
        scratch_shapes=[pltpu.SemaphoreType.DMA, pltpu.SemaphoreType.DMA], compiler_params=_params(),
    )(g)


def _chip_copies(pieces, p_ref, land_ref, send_sems, recv_sems):
    x, y, cc = _place()
    chip = 2 * x + y
    shard = land_ref.shape[1]
    others = _other_chips(x, y)
    cps = []
    for k, (j, r0, nr) in enumerate(pieces):
        px, py = others[j]
        cps.append(pltpu.make_async_remote_copy(
            src_ref=p_ref.at[pl.ds(shard * (2 * px + py) + r0, nr), :], dst_ref=land_ref.at[chip, pl.ds(r0, nr), :],
            send_sem=send_sems.at[k], recv_sem=recv_sems.at[k], device_id=(px, py, cc), device_id_type=MESH))
    return cps


def _owner_copies(p_ref, land_ref, send_sems, recv_sems):
    x, y, cc = _place()
    me = 4 * x + 2 * y + cc
    shard = land_ref.shape[1]
    cps = []
    for k in range(1, N_DEV):
        px, py, pc = _flip(x, (k >> 2) & 1), _flip(y, (k >> 1) & 1), _flip(cc, k & 1)
        block = p_ref.at[pl.ds(shard * (2 * px + py), shard), pl.ds(pl.multiple_of(HALF * pc, 128), HALF)]
        cps.append(pltpu.make_async_remote_copy(src_ref=block, dst_ref=land_ref.at[me], send_sem=send_sems.at[k - 1],
                                                recv_sem=recv_sems.at[k - 1], device_id=(px, py, pc), device_id_type=MESH))
    return cps


def _halves_swap(f, name):
    def body(f_in, f_ref, send_sem, recv_sem):
        x, y, cc = _place()
        mine = f_ref.at[:, pl.ds(pl.multiple_of(HALF * cc, 128), HALF)]
        cp = pltpu.make_async_remote_copy(src_ref=mine, dst_ref=mine, send_sem=send_sem, recv_sem=recv_sem,
                                          device_id=(x, y, 1 - cc), device_id_type=MESH)
        cp.start()
        cp.wait()

    return pl.pallas_call(
        body, name=name, out_shape=jax.ShapeDtypeStruct(f.shape, f.dtype),
        in_specs=[ANY], out_specs=ANY, input_output_aliases={0: 0},
        scratch_shapes=[pltpu.SemaphoreType.DMA, pltpu.SemaphoreType.DMA], compiler_params=_params(),
    )(f)


ROW_NORM_G, ROW_DMOD, ROW_FINAL_G, ROW_LN, ROW_SINKS, ROW_LOSS, ROW_SGU_B, MISC_ROWS = 0, 1, 4, 5, 6, 7, 8, 16


def _small_grads_sum(parts, after):
    part_names = ["dng", "dshift", "dscale", "dgate", "dfg", "dlg", "dlb", "dsinks", "dsgu_b", "dsgu_w", "loss"]
    n_parts = len(part_names)
    n_peers = N_DEV - 1

    def body(*refs):
        p = dict(zip(part_names, refs[:n_parts]))
        tot_ref, gw_ref, dmod_ref = refs[n_parts + 1:n_parts + 4]
        misc, cmisc, dsend, dbuf, wland, wsum, own_sem, send_sems, recv_sems = refs[n_parts + 4:]
        x, y, cc = _place()
        me = 4 * x + 2 * y + cc
        chip = 2 * x + y
        peers = [(_flip(x, (k >> 2) & 1), _flip(y, (k >> 1) & 1), _flip(cc, k & 1)) for k in range(1, N_DEV)]

        def remote(k, src, dst, peer):
            return pltpu.make_async_remote_copy(src_ref=src, dst_ref=dst, send_sem=send_sems.at[k], recv_sem=recv_sems.at[k],
                                                device_id=peer, device_id_type=MESH)

        dsend[:, 0:D_MODEL] = p["dshift"][...]
        dsend[:, D_MODEL:2 * D_MODEL] = p["dscale"][...]
        dsend[:, 2 * D_MODEL:3 * D_MODEL] = p["dgate"][...]

        def shard_cols(chip_index):
            return dsend.at[:, pl.ds(pl.multiple_of(ADA_SHARD * chip_index, 128), ADA_SHARD)]

        own_cols = pltpu.make_async_copy(shard_cols(chip), dbuf.at[me], own_sem)
        own_cols.start()
        gathers = [remote(k, shard_cols(2 * px + py), dbuf.at[me], (px, py, pc)) for k, (px, py, pc) in enumerate(peers)]
        wland[me] = p["dsgu_w"][me]
        scatters = [remote(n_peers + k, p["dsgu_w"].at[4 * px + 2 * py + pc], wland.at[me], (px, py, pc))
                    for k, (px, py, pc) in enumerate(peers)]

        misc[cc] = jnp.zeros((MISC_ROWS, D_MODEL), F32)
        misc[cc, ROW_NORM_G:ROW_NORM_G + 1, :] = p["dng"][...]
        misc[cc, ROW_DMOD:ROW_DMOD + 1, :] = p["dshift"][...]
        misc[cc, ROW_DMOD + 1:ROW_DMOD + 2, :] = p["dscale"][...]
        misc[cc, ROW_DMOD + 2:ROW_DMOD + 3, :] = p["dgate"][...]
        misc[cc, ROW_FINAL_G:ROW_FINAL_G + 1, :] = p["dfg"][...]
        misc[cc, ROW_LN:ROW_LN + 1, 0:D_SGU] = p["dlg"][...]
        misc[cc, ROW_LN:ROW_LN + 1, D_SGU:2 * D_SGU] = p["dlb"][...]
        misc[cc, ROW_SINKS:ROW_SINKS + 1, 0:N_Q_HEADS] = p["dsinks"][...]
        misc[cc, ROW_LOSS:ROW_LOSS + 1, 0:1] = p["loss"][...]
        misc[cc, ROW_SGU_B:ROW_SGU_B + SGU_GROUPS, 0:BLOCK] = p["dsgu_b"][...]
        pair = remote(3 * n_peers, misc.at[cc], misc.at[cc], (x, y, 1 - cc))
        for cp in gathers + scatters + [pair]:
            cp.start()
        pair.wait()
        cmisc[chip] = misc[0] + misc[1]
        quad = [remote(3 * n_peers + 1 + j, cmisc.at[chip], cmisc.at[chip], (_flip(x, bx), _flip(y, by), cc))
                for j, (bx, by) in enumerate(((0, 1), (1, 0), (1, 1)))]
        for cp in quad:
            cp.start()
        for cp in scatters:
            cp.wait()
        group = wland[0]
        for b in range(1, N_DEV):
            group = group + wland[b]
        wsum[me] = group
        spreads = [remote(2 * n_peers + k, wsum.at[me], wsum.at[me], peer) for k, peer in enumerate(peers)]
        for cp in spreads:
            cp.start()
        for cp in gathers + quad + spreads:
            cp.wait()

        tot = cmisc[0]
        for k in range(1, N_CHIPS):
            tot = tot + cmisc[k]
        tot_ref[...] = tot
        gw_ref[...] = wsum[...]
        own_cols.wait()
        for b in range(N_DEV):
            dmod_ref[b:b + 1, :] = dbuf[b]

    operands = [parts[n] for n in part_names]
    n_sems = 3 * n_peers + 1 + (N_CHIPS - 1)
    return pl.pallas_call(
        body, name="small_grads_sum",
        out_shape=(jax.ShapeDtypeStruct((MISC_ROWS, D_MODEL), F32), jax.ShapeDtypeStruct((SGU_GROUPS, BLOCK, BLOCK), F32),
                   jax.ShapeDtypeStruct((N_DEV, ADA_SHARD), F32)),
        in_specs=[VMEM] * len(operands) + [ANY], out_specs=(VMEM, VMEM, VMEM),
        scratch_shapes=[pltpu.VMEM((2, MISC_ROWS, D_MODEL), F32), pltpu.VMEM((N_CHIPS, MISC_ROWS, D_MODEL), F32),
                        pltpu.VMEM((1, 3 * D_MODEL), F32), pltpu.VMEM((N_DEV, 1, ADA_SHARD), F32),
                        pltpu.VMEM((N_DEV, BLOCK, BLOCK), F32), pltpu.VMEM((SGU_GROUPS, BLOCK, BLOCK), F32),
                        pltpu.SemaphoreType.DMA, pltpu.SemaphoreType.DMA((n_sems,)), pltpu.SemaphoreType.DMA((n_sems,))],
        compiler_params=_params(),
    )(*operands, after)


def _small_adamw(tot, gw, small):
    names = ["norm_g", "b_ada", "attn_sinks", "sgu_ln_g", "sgu_ln_b", "sgu_w", "sgu_b", "final_g"]

    def body(*refs):
        tot_ref, gw_ref = refs[:2]
        wmv = {n: refs[2 + 3 * i: 5 + 3 * i] for i, n in enumerate(names)}
        outs = refs[2 + 3 * len(names):]
        res = {n: outs[4 * i: 4 * i + 4] for i, n in enumerate(names)}
        outs[4 * len(names)][...] = tot_ref[ROW_LOSS:ROW_LOSS + 1, 0:1]

        def update(name, g, pick=lambda r: r[...], put=None):
            w_ref, m_ref, v_ref = wmv[name]
            delta, m2, v2 = _adamw(pick(w_ref), g, pick(m_ref), pick(v_ref))
            for ref, val in zip(res[name], (g, delta, m2, v2)):
                if put is None:
                    ref[...] = val
                else:
                    put(ref, val)

        update("norm_g", tot_ref[ROW_NORM_G:ROW_NORM_G + 1, :])
        update("final_g", tot_ref[ROW_FINAL_G:ROW_FINAL_G + 1, :])
        for t in range(3):
            cols = slice(D_MODEL * t, D_MODEL * (t + 1))

            def put_cols(ref, val, cols=cols):
                ref[:, cols] = val
            update("b_ada", tot_ref[ROW_DMOD + t:ROW_DMOD + t + 1, :], pick=lambda r, cols=cols: r[:, cols], put=put_cols)
        update("sgu_ln_g", tot_ref[ROW_LN:ROW_LN + 1, 0:D_SGU])
        update("sgu_ln_b", tot_ref[ROW_LN:ROW_LN + 1, D_SGU:2 * D_SGU])
        update("attn_sinks", tot_ref[ROW_SINKS:ROW_SINKS + 1, 0:N_Q_HEADS])
        update("sgu_b", tot_ref[ROW_SGU_B:ROW_SGU_B + SGU_GROUPS, 0:BLOCK])
        for g in range(SGU_GROUPS):
            def put_group(ref, val, g=g):
                ref[g] = val
            update("sgu_w", gw_ref[g], pick=lambda r, g=g: r[g], put=put_group)

    shapes = {"norm_g": (1, D_MODEL), "b_ada": (1, 3 * D_MODEL), "attn_sinks": (1, N_Q_HEADS), "sgu_ln_g": (1, D_SGU),
              "sgu_ln_b": (1, D_SGU), "sgu_w": (SGU_GROUPS, BLOCK, BLOCK), "sgu_b": (SGU_GROUPS, BLOCK), "final_g": (1, D_MODEL)}
    operands = [tot, gw]
    for n in names:
        operands += [a.reshape(shapes[n]) for a in small[n]]
    out_shape = []
    for n in names:
        out_shape += [jax.ShapeDtypeStruct(shapes[n], F32)] * 4
    out_shape.append(jax.ShapeDtypeStruct((1, 1), F32))
    outs = pl.pallas_call(
        body, name="small_adamw", out_shape=tuple(out_shape),
        in_specs=[VMEM] * len(operands), out_specs=tuple([VMEM] * len(out_shape)), compiler_params=_params(),
    )(*operands)
    return {n: outs[4 * i: 4 * i + 4] for i, n in enumerate(names)}, outs[4 * len(names)]


def _cast_into_gathered(w, tile, place, name, after):
    r, c = w.shape
    nblk = r // tile

    def body(place_ref, w_ref, after_ref, o_ref):
        o_ref[...] = w_ref[...].astype(BF16)

    return pl.pallas_call(
        body, name=name,
        grid_spec=pltpu.PrefetchScalarGridSpec(
            num_scalar_prefetch=1, grid=(nblk,), in_specs=[pl.BlockSpec((tile, c), lambda i, p: (i, 0)), ANY],
            out_specs=pl.BlockSpec((tile, c), lambda i, p: (p[1] * nblk + i, 0))),
        out_shape=jax.ShapeDtypeStruct((N_CHIPS * r, c), BF16),
        compiler_params=_params(("parallel",)),
    )(place, w, after)


def _in_proj_odd(h, win_t, z, after):
    s, d = h.shape
    tm = min(1024, s)
    n_odd = win_t.shape[0] // IN_TILE // 2

    def body(h_ref, w_ref, z_in, after_ref, z_ref):
        z_ref[...] = _dot(h_ref[...], w_ref[...], NT)

    return pl.pallas_call(
        body, name="in_proj_odd", grid=(s // tm, n_odd), out_shape=jax.ShapeDtypeStruct(z.shape, z.dtype),
        in_specs=[pl.BlockSpec((tm, d), lambda i, j: (i, 0)), pl.BlockSpec((IN_TILE, d), lambda i, j: (2 * j + 1, 0)), ANY, ANY],
        out_specs=pl.BlockSpec((tm, IN_TILE), lambda i, j: (i, 2 * j + 1)), input_output_aliases={2: 0},
        compiler_params=_params(("parallel", "parallel")),
    )(h, win_t, z, after)


def _matmul_nt(a, bt, tm, out_dtype, name, after):
    m, k = a.shape
    n = bt.shape[0]
    tm = min(tm, m)

    def body(a_ref, b_ref, after_ref, o_ref):
        o_ref[...] = _dot(a_ref[...], b_ref[...], NT).astype(out_dtype)

    return pl.pallas_call(
        body, name=name, grid=(m // tm,), out_shape=jax.ShapeDtypeStruct((m, n), out_dtype),
        in_specs=[pl.BlockSpec((tm, k), lambda i: (i, 0)), pl.BlockSpec((n, k), lambda i: (0, 0), pipeline_mode=pl.Buffered(1)), ANY],
        out_specs=pl.BlockSpec((tm, n), lambda i: (i, 0)),
        compiler_params=_params(("parallel",)),
    )(a, bt, after)


def _matmul_tn(a, b, tm, name):
    t, m = a.shape
    n = b.shape[1]
    tm = min(tm, m)

    def body(a_ref, b_ref, o_ref):
        o_ref[...] = _dot(a_ref[...], b_ref[...], TN).astype(BF16)

    return pl.pallas_call(
        body, name=name, grid=(m // tm,), out_shape=jax.ShapeDtypeStruct((m, n), BF16),
        in_specs=[pl.BlockSpec((t, tm), lambda i: (0, i)), pl.BlockSpec((t, n), lambda i: (0, 0))],
        out_specs=pl.BlockSpec((tm, n), lambda i: (i, 0)),
        compiler_params=_params(("parallel",)),
    )(a, b)


def _grad_matmul(a, b, half, tm, name, add=None):
    t, m = a.shape
    tm = min(tm, m)

    def body(half_ref, a_ref, b_ref, *rest):
        if add is None:
            (o_ref,) = rest
            o_ref[...] = _dot(a_ref[...], b_ref[...], TN).astype(BF16)
        else:
            add_ref, o_ref = rest
            o_ref[...] = (_dot(a_ref[...], b_ref[...], TN) + add_ref[...].astype(F32)).astype(BF16)

    in_specs = [pl.BlockSpec((t, tm), lambda i, h: (0, i)), pl.BlockSpec((t, HALF), lambda i, h: (0, h[0]))]
    operands = [a, b]
    if add is not None:
        in_specs.append(pl.BlockSpec((tm, HALF), lambda i, h: (i, 0)))
        operands.append(add)
    return pl.pallas_call(
        body, name=name,
        grid_spec=pltpu.PrefetchScalarGridSpec(num_scalar_prefetch=1, grid=(m // tm,), in_specs=in_specs,
                                               out_specs=pl.BlockSpec((tm, HALF), lambda i, h: (i, 0))),
        out_shape=jax.ShapeDtypeStruct((m, HALF), BF16),
        compiler_params=_params(("parallel",)),
    )(half, *operands)


def _mixer_masks(n):
    key = lax.broadcasted_iota(jnp.int32, (2 * BLOCK, Q_PER_KV * BLOCK), 0)
    qry = lax.broadcasted_iota(jnp.int32, (2 * BLOCK, Q_PER_KV * BLOCK), 1) & (BLOCK - 1)
    valid = (key > qry) & (key <= qry + BLOCK) & ((key >= BLOCK) | (n > 0))
    r2 = lax.broadcasted_iota(jnp.int32, (BLOCK, BLOCK), 0)
    c2 = lax.broadcasted_iota(jnp.int32, (BLOCK, BLOCK), 1)
    return valid, r2 >= c2


def _layer_norm_parts(vs):
    mu = jnp.mean(vs, axis=-1, keepdims=True)
    xc = vs - mu
    rstd = lax.rsqrt(jnp.mean(xc * xc, axis=-1, keepdims=True) + EPS)
    return xc * rstd, rstd


def _stacked_queries(z_ref, hk, scale=None):
    q = jnp.concatenate([z_ref[:, _head_cols(Q0, Q_PER_KV * hk + g)] for g in range(Q_PER_KV)], axis=0)
    return (q if scale is None else q * scale).astype(BF16)


def _head_cols(base, h):
    return slice(base + HEAD_DIM * h, base + HEAD_DIM * (h + 1))


def _sink_row(sink_ref, hk):
    return jnp.concatenate([jnp.broadcast_to(sink_ref[:, h:h + 1], (1, BLOCK))
                            for h in range(Q_PER_KV * hk, Q_PER_KV * (hk + 1))], axis=1)


def _attn_probs(qs, kh, valid, sk, scale=None):
    s = _dot(kh, qs, NT)
    s = jnp.where(valid, s if scale is None else s * scale, -jnp.inf)
    m = jnp.maximum(jnp.max(s, axis=0, keepdims=True), sk)
    e = jnp.exp(s - m)
    es = jnp.exp(sk - m)
    inv = 1.0 / (jnp.sum(e, axis=0, keepdims=True) + es)
    return e * inv, es * inv


def _kv_ext(z_ref, kvp_ref):
    k_ext = jnp.concatenate([kvp_ref[:, 0:D_KV], z_ref[:, K0:K0 + D_KV]], axis=0).astype(BF16)
    v_ext = jnp.concatenate([kvp_ref[:, D_KV:2 * D_KV], z_ref[:, V0:V0 + D_KV]], axis=0).astype(BF16)
    return k_ext, v_ext


def _sgu_mixed(w_ref, bt_ref, vn, tril):
    ws = [jnp.where(tril, w_ref[g], 0.0).astype(BF16) for g in range(SGU_GROUPS)]
    mixed = jnp.concatenate([_dot(ws[g], vn[:, GROUP_DIM * g:GROUP_DIM * (g + 1)], NN) + bt_ref[:, g:g + 1]
                             for g in range(SGU_GROUPS)], axis=1)
    return ws, mixed


def _mixer_fwd(z, sinks, ln_g, ln_b, sgu_w, sgu_bt):
    s = z.shape[0]
    nb = s // BLOCK

    def body(z_ref, kvp_ref, sink_ref, lg_ref, lb_ref, w_ref, bt_ref, cat_ref, o_ref):
        n = pl.program_id(0)
        valid, tril = _mixer_masks(n)
        k_ext, v_ext = _kv_ext(z_ref, kvp_ref)
        for hk in range(2):
            kv = slice(HEAD_DIM * hk, HEAD_DIM * (hk + 1))
            p, _ = _attn_probs(_stacked_queries(z_ref, hk), k_ext[:, kv], valid, _sink_row(sink_ref, hk), ATTN_SCALE)
            o_t = _dot(v_ext[:, kv], p.astype(BF16), TN)
            for g in range(Q_PER_KV):
                o_ref[:, _head_cols(0, Q_PER_KV * hk + g)] = o_t[:, BLOCK * g:BLOCK * (g + 1)].T
        ga = z_ref[:, GA0:GA0 + D_ATTN]
        cat_ref[:, 0:D_ATTN] = (o_ref[...] * (ga * _sigmoid(ga))).astype(BF16)
        xhat, _ = _layer_norm_parts(z_ref[:, VS0:VS0 + D_SGU])
        vn = (xhat * lg_ref[...] + lb_ref[...]).astype(BF16)
        _, mixed = _sgu_mixed(w_ref, bt_ref, vn, tril)
        gt = z_ref[:, GS0:GS0 + D_SGU]
        cat_ref[:, D_ATTN:D_MODEL] = ((z_ref[:, U0:U0 + D_SGU] * mixed) * (gt * _sigmoid(gt))).astype(BF16)

    const2 = lambda n: (0, 0)
    return pl.pallas_call(
        body, name="mixer_fwd", grid=(nb,),
        out_shape=(jax.ShapeDtypeStruct((s, D_MODEL), BF16), jax.ShapeDtypeStruct((s, D_ATTN), F32)),
        in_specs=[pl.BlockSpec((BLOCK, D_IN), lambda n: (n, 0)),
                  pl.BlockSpec((BLOCK, 2 * D_KV), lambda n: (jnp.maximum(n - 1, 0), K0 // (2 * D_KV))),
                  pl.BlockSpec((1, N_Q_HEADS), const2), pl.BlockSpec((1, D_SGU), const2), pl.BlockSpec((1, D_SGU), const2),
                  pl.BlockSpec((SGU_GROUPS, BLOCK, BLOCK), lambda n: (0, 0, 0)), pl.BlockSpec((BLOCK, SGU_GROUPS), const2)],
        out_specs=(pl.BlockSpec((BLOCK, D_MODEL), lambda n: (n, 0)), pl.BlockSpec((BLOCK, D_ATTN), lambda n: (n, 0))),
        compiler_params=_params(("parallel",)),
    )(z, z, sinks, ln_g, ln_b, sgu_w, sgu_bt)


def _out_proj_loss(cat, wout, x, target, mod, final_g):
    s, d = x.shape
    tm = min(256, s)

    def body(cat_ref, w_ref, x_ref, t_ref, mod_ref, fg_ref, dx1_ref, dy_ref, dfg_ref, dgate_ref, loss_ref):
        @pl.when(pl.program_id(0) == 0)
        def _():
            dfg_ref[...] = jnp.zeros_like(dfg_ref)
            dgate_ref[...] = jnp.zeros_like(dgate_ref)
            loss_ref[...] = jnp.zeros_like(loss_ref)

        gate = mod_ref[:, 2 * d:3 * d]
        fg = fg_ref[...]
        yv = _dot(cat_ref[...], w_ref[...], NN)
        x1 = x_ref[...] + gate * yv
        r = lax.rsqrt(jnp.mean(x1 * x1, axis=-1, keepdims=True) + EPS)
        xr = x1 * r
        err = xr * fg - t_ref[...]
        dout = err * (1.0 / d)
        gy = dout * fg
        dx1 = r * gy - xr * (r * r * jnp.mean(gy * x1, axis=-1, keepdims=True))
        dx1_ref[...] = dx1
        dy_ref[...] = (gate * dx1).astype(BF16)
        dfg_ref[...] += jnp.sum(dout * xr, axis=0, keepdims=True)
        dgate_ref[...] += jnp.sum(dx1 * yv, axis=0, keepdims=True)
        loss_ref[...] += 0.5 * jnp.sum(jnp.mean(err * err, axis=-1, keepdims=True), axis=0, keepdims=True)

    row = lambda i: (i, 0)
    const = lambda i: (0, 0)
    return pl.pallas_call(
        body, name="out_proj_loss", grid=(s // tm,),
        out_shape=(jax.ShapeDtypeStruct((s, d), F32), jax.ShapeDtypeStruct((s, d), BF16), jax.ShapeDtypeStruct((1, d), F32),
                   jax.ShapeDtypeStruct((1, d), F32), jax.ShapeDtypeStruct((1, 1), F32)),
        in_specs=[pl.BlockSpec((tm, d), row), pl.BlockSpec((d, d), const), pl.BlockSpec((tm, d), row), pl.BlockSpec((tm, d), row),
                  pl.BlockSpec((1, 3 * d), const), pl.BlockSpec((1, d), const)],
        out_specs=(pl.BlockSpec((tm, d), row), pl.BlockSpec((tm, d), row), pl.BlockSpec((1, d), const), pl.BlockSpec((1, d), const),
                   pl.BlockSpec((1, 1), const)),
        compiler_params=_params(("arbitrary",)),
    )(cat, wout, x, target, mod, final_g)


def _mixer_bwd(z, dcat, o, sinks, ln_g, ln_b, sgu_w, sgu_bt):
    s = z.shape[0]
    nb = s // BLOCK

    def body(z_ref, kvp_ref, dc_ref, o_ref, sink_ref, lg_ref, lb_ref, w_ref, bt_ref,
             dz_ref, dsink_ref, dlg_ref, dlb_ref, dw_ref, db_ref, pend, dkv, carry, dsink_acc, dbt_acc):
        n = pl.program_id(0)

        @pl.when(n == 0)
        def _():
            pend[...] = jnp.zeros_like(pend)
            carry[...] = jnp.zeros_like(carry)
            dsink_acc[...] = jnp.zeros_like(dsink_acc)
            dbt_acc[...] = jnp.zeros_like(dbt_acc)
            dlg_ref[...] = jnp.zeros_like(dlg_ref)
            dlb_ref[...] = jnp.zeros_like(dlb_ref)
            dw_ref[...] = jnp.zeros_like(dw_ref)

        dz_ref[:, 0:K0] = pend[:, 0:K0]
        dz_ref[:, GA0:D_IN] = pend[:, GA0:D_IN]

        @pl.when(n == nb)
        def _():
            dkv[...] = jnp.zeros_like(dkv)

        @pl.when(n < nb)
        def _():
            valid, tril = _mixer_masks(n)
            k_ext, v_ext = _kv_ext(z_ref, kvp_ref)
            ga = z_ref[:, GA0:GA0 + D_ATTN]
            sg = _sigmoid(ga)
            datt = dc_ref[:, 0:D_ATTN]
            pend[:, GA0:GA0 + D_ATTN] = (datt * o_ref[...] * (sg * (1.0 + ga * (1.0 - sg)))).astype(BF16)
            do_all = (datt * (ga * sg)).astype(BF16)
            for hk in range(2):
                kv = slice(HEAD_DIM * hk, HEAD_DIM * (hk + 1))
                kh, vh = k_ext[:, kv], v_ext[:, kv]
                qs = _stacked_queries(z_ref, hk, ATTN_SCALE)
                p, p_sink = _attn_probs(qs, kh, valid, _sink_row(sink_ref, hk))
                do = jnp.concatenate([do_all[:, _head_cols(0, Q_PER_KV * hk + g)] for g in range(Q_PER_KV)], axis=0)
                dp = _dot(vh, do, NT)
                delta = jnp.sum(p * dp, axis=0, keepdims=True)
                ds = (p * (dp - delta)).astype(BF16)
                dsk = -p_sink * delta
                dq_t = _dot(kh, ds, TN) * ATTN_SCALE
                for g in range(Q_PER_KV):
                    h = Q_PER_KV * hk + g
                    dsink_acc[:, h:h + 1] += jnp.sum(dsk[:, BLOCK * g:BLOCK * (g + 1)], axis=1, keepdims=True)
                    pend[:, _head_cols(Q0, h)] = dq_t[:, BLOCK * g:BLOCK * (g + 1)].T.astype(BF16)
                dkv[:, kv] = _dot(ds, qs, NN)
                dkv[:, D_KV + HEAD_DIM * hk:D_KV + HEAD_DIM * (hk + 1)] = _dot(p.astype(BF16), do, NN)

            xhat, rstd = _layer_norm_parts(z_ref[:, VS0:VS0 + D_SGU])
            lg = lg_ref[...]
            vn = (xhat * lg + lb_ref[...]).astype(BF16)
            ws, mixed = _sgu_mixed(w_ref, bt_ref, vn, tril)
            u = z_ref[:, U0:U0 + D_SGU]
            gt = z_ref[:, GS0:GS0 + D_SGU]
            sg = _sigmoid(gt)
            dsgu = dc_ref[:, D_ATTN:D_MODEL]
            dgated = dsgu * (gt * sg)
            pend[:, U0:U0 + D_SGU] = (dgated * mixed).astype(BF16)
            pend[:, GS0:GS0 + D_SGU] = (dsgu * (u * mixed) * (sg * (1.0 + gt * (1.0 - sg)))).astype(BF16)
            dmixed = dgated * u
            dmb = dmixed.astype(BF16)
            dvn_parts = []
            for g in range(SGU_GROUPS):
                gs_ = slice(GROUP_DIM * g, GROUP_DIM * (g + 1))
                dw_ref[g] += jnp.where(tril, _dot(dmb[:, gs_], vn[:, gs_], NT), 0.0)
                dbt_acc[:, g:g + 1] += jnp.sum(dmixed[:, gs_], axis=-1, keepdims=True)
                dvn_parts.append(_dot(ws[g], dmb[:, gs_], TN))
            dvn = jnp.concatenate(dvn_parts, axis=1)
            dlg_ref[...] += jnp.sum(dvn * xhat, axis=0, keepdims=True)
            dlb_ref[...] += jnp.sum(dvn, axis=0, keepdims=True)
            dxh = dvn * lg
            pend[:, VS0:VS0 + D_SGU] = (rstd * (dxh - jnp.mean(dxh, axis=-1, keepdims=True)
                                                - xhat * jnp.mean(dxh * xhat, axis=-1, keepdims=True))).astype(BF16)

        dz_ref[:, K0:GA0] = (carry[...] + dkv[0:BLOCK, :]).astype(BF16)
        carry[...] = dkv[BLOCK:2 * BLOCK, :]

        @pl.when(n == nb)
        def _():
            dsink_ref[...] = dsink_acc[:, 0:N_Q_HEADS]
            db_ref[...] = dbt_acc[...].T[0:SGU_GROUPS, :]

    cur = lambda n: (jnp.minimum(n, nb - 1), 0)
    const2 = lambda n: (0, 0)
    return pl.pallas_call(
        body, name="mixer_bwd", grid=(nb + 1,),
        out_shape=(jax.ShapeDtypeStruct((s, D_IN), BF16), jax.ShapeDtypeStruct((1, N_Q_HEADS), F32),
                   jax.ShapeDtypeStruct((1, D_SGU), F32), jax.ShapeDtypeStruct((1, D_SGU), F32),
                   jax.ShapeDtypeStruct((SGU_GROUPS, BLOCK, BLOCK), F32), jax.ShapeDtypeStruct((SGU_GROUPS, BLOCK), F32)),
        in_specs=[pl.BlockSpec((BLOCK, D_IN), cur),
                  pl.BlockSpec((BLOCK, 2 * D_KV), lambda n: (jnp.maximum(jnp.minimum(n, nb - 1) - 1, 0), K0 // (2 * D_KV))),
                  pl.BlockSpec((BLOCK, D_MODEL), cur), pl.BlockSpec((BLOCK, D_ATTN), cur),
                  pl.BlockSpec((1, N_Q_HEADS), const2), pl.BlockSpec((1, D_SGU), const2), pl.BlockSpec((1, D_SGU), const2),
                  pl.BlockSpec((SGU_GROUPS, BLOCK, BLOCK), lambda n: (0, 0, 0)), pl.BlockSpec((BLOCK, SGU_GROUPS), const2)],
        out_specs=(pl.BlockSpec((BLOCK, D_IN), lambda n: (jnp.maximum(n - 1, 0), 0)),
                   pl.BlockSpec((1, N_Q_HEADS), const2), pl.BlockSpec((1, D_SGU), const2), pl.BlockSpec((1, D_SGU), const2),
                   pl.BlockSpec((SGU_GROUPS, BLOCK, BLOCK), lambda n: (0, 0, 0)), pl.BlockSpec((SGU_GROUPS, BLOCK), const2)),
        scratch_shapes=[pltpu.VMEM((BLOCK, D_IN), BF16), pltpu.VMEM((2 * BLOCK, 2 * D_KV), F32), pltpu.VMEM((BLOCK, 2 * D_KV), F32),
                        pltpu.VMEM((1, BLOCK), F32), pltpu.VMEM((BLOCK, BLOCK), F32)],
        compiler_params=_params(("arbitrary",)),
    )(z, z, dcat, o, sinks, ln_g, ln_b, sgu_w, sgu_bt)


def _in_proj_bwd(dz, win_t, x, dx1, norm_g, mod):
    s, d = x.shape
    kdim = dz.shape[1]
    tm = min(256, s)

    def body(dz_ref, w_ref, x_ref, dx1_ref, g_ref, mod_ref, gx_ref, dshift_ref, dscale_ref, dng_ref):
        @pl.when(pl.program_id(0) == 0)
        def _():
            dshift_ref[...] = jnp.zeros_like(dshift_ref)
            dscale_ref[...] = jnp.zeros_like(dscale_ref)
            dng_ref[...] = jnp.zeros_like(dng_ref)

        dh = _dot(dz_ref[...], w_ref[...], NN)
        xv = x_ref[...]
        ng = g_ref[...]
        r = lax.rsqrt(jnp.mean(xv * xv, axis=-1, keepdims=True) + EPS)
        xn = xv * r
        dshift_ref[...] += jnp.sum(dh, axis=0, keepdims=True)
        dscale_ref[...] += jnp.sum(dh * (xn * ng), axis=0, keepdims=True)
        dhs = dh * (1.0 + mod_ref[:, d:2 * d])
        dng_ref[...] += jnp.sum(dhs * xn, axis=0, keepdims=True)
        dxn = dhs * ng
        gx_ref[...] = dx1_ref[...] + r * dxn - xn * (r * r * jnp.mean(dxn * xv, axis=-1, keepdims=True))

    row = lambda i: (i, 0)
    const = lambda i: (0, 0)
    return pl.pallas_call(
        body, name="in_proj_bwd", grid=(s // tm,),
        out_shape=(jax.ShapeDtypeStruct((s, d), F32),) + (jax.ShapeDtypeStruct((1, d), F32),) * 3,
        in_specs=[pl.BlockSpec((tm, kdim), row), pl.BlockSpec((kdim, d), const, pipeline_mode=pl.Buffered(1)),
                  pl.BlockSpec((tm, d), row), pl.BlockSpec((tm, d), row), pl.BlockSpec((1, d), const), pl.BlockSpec((1, 3 * d), const)],
        out_specs=(pl.BlockSpec((tm, d), row),) + (pl.BlockSpec((1, d), const),) * 3,
        compiler_params=_params(("arbitrary",)),
    )(dz, win_t, x, dx1, norm_g, mod)


def _sum_slots(slots, own, place, rows_tile, full_cols, name):
    n_slots, r, hcols = slots.shape
    nblk = r // rows_tile
    per_device = n_slots == N_DEV

    def body(place_ref, s_ref, own_ref, o_ref):
        mine = 2 * place_ref[1] + place_ref[0] if per_device else place_ref[1]
        acc = jnp.zeros((rows_tile, hcols), F32)
        for k in range(n_slots):
            acc = acc + jnp.where(mine == k, own_ref[...], s_ref[k]).astype(F32)
        o_ref[...] = acc

    return pl.pallas_call(
        body, name=name,
        grid_spec=pltpu.PrefetchScalarGridSpec(
            num_scalar_prefetch=1, grid=(nblk,),
            in_specs=[pl.BlockSpec((n_slots, rows_tile, hcols), lambda i, p: (0, i, 0)),
                      pl.BlockSpec((rows_tile, hcols), lambda i, p: (p[1] * nblk + i, p[0] if per_device else 0))],
            out_specs=pl.BlockSpec((rows_tile, hcols), lambda i, p: (i, p[0]))),
        out_shape=jax.ShapeDtypeStruct((r, full_cols), F32),
        compiler_params=_params(("parallel",)),
    )(place, slots, own)


def _adamw_rows(w, g, m, v, tile, name, after):
    r, c = w.shape

    def body(w_ref, g_ref, m_ref, v_ref, after_ref, go_ref, d_ref, mo_ref, vo_ref):
        g_ = g_ref[...]
        delta, m2, v2 = _adamw(w_ref[...], g_, m_ref[...], v_ref[...])
        go_ref[...] = g_
        d_ref[...] = delta
        mo_ref[...] = m2
        vo_ref[...] = v2

    blk = pl.BlockSpec((tile, c), lambda i: (i, 0))
    return pl.pallas_call(
        body, name=name, grid=(r // tile,), out_shape=(jax.ShapeDtypeStruct((r, c), F32),) * 4,
        in_specs=[blk] * 4 + [ANY], out_specs=(blk,) * 4,
        compiler_params=_params(("parallel",)),
    )(w, g, m, v, after)


def _adamw_ada(cact, dmod, w, m, v, tile=256):
    r, c = w.shape

    def body(c_ref, d_ref, w_ref, m_ref, v_ref, go_ref, dl_ref, mo_ref, vo_ref):
        g_ = _dot(c_ref[...].astype(BF16), d_ref[...].astype(BF16), TN)
        delta, m2, v2 = _adamw(w_ref[...], g_, m_ref[...], v_ref[...])
        go_ref[...] = g_
        dl_ref[...] = delta
        mo_ref[...] = m2
        vo_ref[...] = v2

    blk = pl.BlockSpec((tile, c), lambda i: (i, 0))
    return pl.pallas_call(
        body, name="adamw_ada", grid=(r // tile,), out_shape=(jax.ShapeDtypeStruct((r, c), F32),) * 4,
        in_specs=[pl.BlockSpec((N_DEV, tile), lambda i: (0, i)), pl.BlockSpec((N_DEV, c), lambda i: (0, 0)), blk, blk, blk],
        out_specs=(blk,) * 4,
        compiler_params=_params(("parallel",)),
    )(cact, dmod, w, m, v)


def kernel(x, c, norm_g, w_ada, b_ada, w_in, attn_sinks, sgu_ln_g, sgu_ln_b, sgu_w, sgu_b, w_out, final_g, loss_target, m_norm_g, m_w_ada, m_b_ada, m_w_in, m_attn_sinks, m_sgu_ln_g, m_sgu_ln_b, m_sgu_w, m_sgu_b, m_w_out, m_final_g, v_norm_g, v_w_ada, v_b_ada, v_w_in, v_attn_sinks, v_sgu_ln_g, v_sgu_ln_b, v_sgu_w, v_sgu_b, v_w_out, v_final_g):
    xs, tgt = x[0], loss_target[0]
    core = lax.axis_index("c").astype(jnp.int32).reshape(1)
    chip = 2 * lax.axis_index("x") + lax.axis_index("y")
    place = jnp.stack([core[0], chip.astype(jnp.int32)])
    sgu_bt = sgu_b[0].T

    w_in_t, m_in_t, v_in_t = (jnp.swapaxes(a[0], 0, 1) for a in (w_in, m_w_in, v_w_in))
    win_t = _cast_into_gathered(w_in_t, IN_SHARD // 4, place, "cast_w_in", c)
    win_t, mod, cact, h, z = _gather_w_in_modulate_project(win_t, c, w_ada[0], b_ada, xs, norm_g)
    wout = _cast_into_gathered(w_out[0], OUT_SHARD // 2, place, "cast_w_out", win_t)
    sems, bufs, token = _split_start(_gather_copies, [wout], 3, "w_out_gather_start", win_t)
    z = _in_proj_odd(h, win_t, z, token)
    cat, o = _mixer_fwd(z, attn_sinks, sgu_ln_g, sgu_ln_b, sgu_w[0], sgu_bt)
    wout = _gather_finish(_split_wait(_gather_copies, sems, bufs, cat, "w_out_gather_wait")[0], "w_out_gather_finish")
    dx1, dy, dfg, dgate, loss = _out_proj_loss(cat, wout, xs, tgt, mod, final_g.reshape(1, D_MODEL))

    po = _matmul_tn(cat, dy, 512, "w_out_grad")
    sems, bufs, token = _split_start(_owner_copies, [po, lax.empty((N_DEV, OUT_SHARD, HALF), BF16)], N_DEV - 1,
                                     "w_out_grad_to_owners_start", cat)
    dcat = _matmul_nt(dy, wout, 256, F32, "out_proj_bwd", token)
    dz, dsinks, dlg, dlb, dsgu_w, dsgu_b = _mixer_bwd(z, dcat, o, attn_sinks, sgu_ln_g, sgu_ln_b, sgu_w[0], sgu_bt)
    po, slots_o = _split_wait(_owner_copies, sems, bufs, dz, "w_out_grad_to_owners_wait")
    go = _halves_swap(_sum_slots(slots_o, po, place, OUT_SHARD // 2, D_MODEL, "w_out_grad_sum"), "w_out_grad_halves_swap")

    sw = _to_sibling(_grad_matmul(dz, h, 1 - core, 768, "w_in_grad_sibling"), "w_in_grad_to_sibling")
    pw = _grad_matmul(dz, h, core, 768, "w_in_grad_own", add=sw)
    first = functools.partial(_chip_copies, [(0, 0, IN_SHARD), (1, 0, IN_SHARD), (2, 0, IN_SHARD // 4)])
    rest = functools.partial(_chip_copies, [(2, IN_SHARD // 4, 3 * IN_SHARD // 4)])
    sems, bufs, token = _split_start(first, [pw, lax.empty((N_CHIPS, IN_SHARD, HALF), BF16)], 3, "w_in_grad_to_chips_start", sw)
    gx, dshift, dscale, dng = _in_proj_bwd(dz, win_t, xs, dx1, norm_g + token[0:1, 0:1], mod)
    bufs = _split_wait(first, sems, bufs, gx, "w_in_grad_to_chips_wait")

    parts = dict(dng=dng, dshift=dshift, dscale=dscale, dgate=dgate, dfg=dfg, dlg=dlg, dlb=dlb, dsinks=dsinks,
                 dsgu_b=dsgu_b, dsgu_w=dsgu_w, loss=loss)
    small = dict(norm_g=(norm_g, m_norm_g, v_norm_g), b_ada=(b_ada, m_b_ada, v_b_ada),
                 attn_sinks=(attn_sinks, m_attn_sinks, v_attn_sinks), sgu_ln_g=(sgu_ln_g, m_sgu_ln_g, v_sgu_ln_g),
                 sgu_ln_b=(sgu_ln_b, m_sgu_ln_b, v_sgu_ln_b), sgu_w=(sgu_w, m_sgu_w, v_sgu_w), sgu_b=(sgu_b, m_sgu_b, v_sgu_b),
                 final_g=(final_g, m_final_g, v_final_g))
    tot, gsgu_w, dmod_mine = _small_grads_sum(parts, bufs[1])
    sems, bufs, token = _split_start(rest, list(bufs), 1, "w_in_grad_rest_to_chips_start", tot)
    res, total = _small_adamw(tot, gsgu_w, small)
    res["w_ada"] = _adamw_ada(cact, dmod_mine + token[0:1, 0:1], w_ada[0], m_w_ada[0], v_w_ada[0])
    res["w_out"] = _adamw_rows(w_out[0], go, m_w_out[0], v_w_out[0], 256, "adamw_w_out", res["w_ada"][1])

    pw, slots_w = _split_wait(rest, sems, bufs, res["w_out"][1], "w_in_grad_rest_to_chips_wait")
    gw_t = _halves_swap(_sum_slots(slots_w, pw, place, IN_SHARD // 4, D_MODEL, "w_in_grad_sum"), "w_in_grad_halves_swap")
    res["w_in"] = tuple(jnp.swapaxes(a, 0, 1) for a in _adamw_rows(w_in_t, gw_t, m_in_t, v_in_t, IN_SHARD // 4, "adamw_w_in", gw_t))

    order = ["norm_g", "w_ada", "b_ada", "w_in", "attn_sinks", "sgu_ln_g", "sgu_ln_b", "sgu_w", "sgu_b", "w_out", "final_g"]
    like = dict(norm_g=norm_g, w_ada=w_ada, b_ada=b_ada, w_in=w_in, attn_sinks=attn_sinks, sgu_ln_g=sgu_ln_g, sgu_ln_b=sgu_ln_b,
                sgu_w=sgu_w, sgu_b=sgu_b, w_out=w_out, final_g=final_g)
    outs = [total[0, 0], gx[None]]
    for kind in range(4):
        outs += [res[n][kind].reshape(like[n].shape) for n in order]
    return tuple(outs)
```

```python
import functools
import math

import jax
import jax.numpy as jnp
from jax import lax
from jax.experimental import pallas as pl
from jax.experimental.pallas import tpu as pltpu

F32 = jnp.float32
BF16 = jnp.bfloat16

D_MODEL = 2048
D_ATTN = 1024
HEAD_DIM = 64
N_Q_HEADS = 16
Q_PER_KV = 8
D_KV = 128
BLOCK = 128
D_SGU = 1024
SGU_GROUPS = 8
GROUP_DIM = 128
D_IN = 5376
EPS = 1e-6
ATTN_SCALE = 1.0 / math.sqrt(HEAD_DIM)
N_CHIPS = 4
N_DEV = 8
IN_SHARD = D_IN // N_CHIPS
OUT_SHARD = D_MODEL // N_CHIPS
ADA_SHARD = 3 * D_MODEL // N_CHIPS
HALF = D_MODEL // 2
IN_TILE = 768

Q0, K0, V0, GA0, U0, VS0, GS0 = 0, 1024, 1152, 1280, 2304, 3328, 4352

ADAM_LR = 0.001
ADAM_B1 = 0.9
ADAM_B2 = 0.999
ADAM_EPS = 1e-08
ADAM_WD = 0.01
ADAM_STEP = 10

VMEM_LIMIT_V7X = 56 * 1024 * 1024

NN = (((1,), (0,)), ((), ()))
NT = (((1,), (1,)), ((), ()))
TN = (((0,), (0,)), ((), ()))
MESH = pl.DeviceIdType.MESH
ANY = pl.BlockSpec(memory_space=pl.ANY)
VMEM = pl.BlockSpec(memory_space=pltpu.VMEM)
HBM = pl.BlockSpec(memory_space=pltpu.HBM)
SEM = pl.BlockSpec(memory_space=pltpu.SEMAPHORE)
EFFECT = pltpu.SideEffectType.DATAFLOW_SIDE_EFFECTING


def _dot(a, b, dims):
    return lax.dot_general(a, b, dims, preferred_element_type=F32)


def _params(sem=None):
    return pltpu.CompilerParams(dimension_semantics=sem, vmem_limit_bytes=VMEM_LIMIT_V7X)


def _sigmoid(x):
    return jax.nn.sigmoid(x)


def _adamw(w, g, m, v):
    m = ADAM_B1 * m + (1.0 - ADAM_B1) * g
    v = ADAM_B2 * v + (1.0 - ADAM_B2) * (g * g)
    m_hat = m / (1.0 - ADAM_B1 ** ADAM_STEP)
    v_hat = v / (1.0 - ADAM_B2 ** ADAM_STEP)
    delta = -ADAM_LR * (m_hat / (jnp.sqrt(v_hat) + ADAM_EPS) + ADAM_WD * w)
    return delta, m, v


def _place():
    return lax.axis_index("x"), lax.axis_index("y"), lax.axis_index("c")


def _flip(p, bit):
    return 1 - p if bit else p


def _other_chips(x, y):
    return [(_flip(x, bx), _flip(y, by)) for bx, by in ((0, 1), (1, 0), (1, 1))]


def _split_start(copies, bufs, n_sems, name, after):
    n = len(bufs)

    def body(*refs):
        send_sems, recv_sems = refs[n + 1:n + 3]
        for cp in copies(*refs[:n], send_sems, recv_sems):
            cp.start()
        refs[-1][...] = jnp.zeros_like(refs[-1])

    bufs = [pltpu.with_memory_space_constraint(a, pltpu.HBM) for a in bufs]
    outs = pl.pallas_call(
        body, name=name,
        out_shape=(pltpu.SemaphoreType.DMA((n_sems,)), pltpu.SemaphoreType.DMA((n_sems,)))
        + tuple(pltpu.HBM(a.shape, a.dtype) for a in bufs) + (jax.ShapeDtypeStruct((8, 128), F32),),
        in_specs=[HBM] * n + [ANY], out_specs=(SEM, SEM) + (HBM,) * n + (VMEM,), input_output_aliases={k: k + 2 for k in range(n)},
        compiler_params=pltpu.CompilerParams(has_side_effects=EFFECT),
    )(*bufs, after)
    return outs[:2], list(outs[2:2 + n]), outs[-1]


def _split_wait(copies, sems, bufs, after, name):
    n = len(bufs)

    def body(*refs):
        send_sems, recv_sems = refs[n:n + 2]
        for cp in copies(*refs[:n], send_sems, recv_sems):
            cp.wait_send()
            cp.wait_recv()

    return pl.pallas_call(
        body, name=name, out_shape=tuple(pltpu.HBM(a.shape, a.dtype) for a in bufs),
        in_specs=[HBM] * n + [SEM, SEM, ANY], out_specs=(HBM,) * n, input_output_aliases={k: k for k in range(n)},
        compiler_params=pltpu.CompilerParams(has_side_effects=EFFECT),
    )(*bufs, *sems, after)


def _gather_copies(full_ref, send_sems, recv_sems):
    x, y, cc = _place()
    shard = full_ref.shape[0] // N_CHIPS
    half = shard // 2
    mine = full_ref.at[pl.ds(shard * (2 * x + y) + half * cc, half), :]
    return [pltpu.make_async_remote_copy(src_ref=mine, dst_ref=mine, send_sem=send_sems.at[j], recv_sem=recv_sems.at[j],
                                         device_id=(px, py, cc), device_id_type=MESH)
            for j, (px, py) in enumerate(_other_chips(x, y))]


def _gather_finish(full, name):
    shard = full.shape[0] // N_CHIPS
    half = shard // 2

    def body(full_in, full_ref, send_sems, recv_sems):
        x, y, cc = _place()
        cps = []
        for j, (px, py) in enumerate(_other_chips(x, y)):
            landed = full_ref.at[pl.ds(shard * (2 * px + py) + half * cc, half), :]
            cps.append(pltpu.make_async_remote_copy(src_ref=landed, dst_ref=landed, send_sem=send_sems.at[j], recv_sem=recv_sems.at[j],
                                                    device_id=(x, y, 1 - cc), device_id_type=MESH))
        for cp in cps:
            cp.start()
        for cp in cps:
            cp.wait()

    return pl.pallas_call(
        body, name=name, out_shape=jax.ShapeDtypeStruct(full.shape, full.dtype),
        in_specs=[ANY], out_specs=ANY, input_output_aliases={0: 0},
        scratch_shapes=[pltpu.SemaphoreType.DMA((3,)), pltpu.SemaphoreType.DMA((3,))],
        compiler_params=_params(),
    )(full)


def _gather_w_in_modulate_project(full, c, w_ada, b_ada, xs, norm_g):
    shard = full.shape[0] // N_CHIPS
    half = shard // 2
    quarter = half // 2
    kdim = w_ada.shape[0]
    kc = 256
    s, d = xs.shape
    rows_at_once = min(256, s)
    n_chunks = s // rows_at_once
    m_rows = min(512, s)
    n_m = s // m_rows

    def body(full_in, c_ref, w_hbm, b_ref, x_hbm, g_ref, full_ref, mod_ref, cact_ref, h_hbm, z_hbm,
             w_vmem, cbuf, pbuf, rbuf, h_vmem, xbuf, bbuf, zbuf, w_sem, x_sems, b_sem, z_sems, h_sem,
             send_sems, recv_sems, msend_sems, mrecv_sems):
        x, y, cc = _place()
        me = 4 * x + 2 * y + cc
        chip = 2 * x + y
        across_x, across_y, far = (1 - x, y), (x, 1 - y), (1 - x, 1 - y)

        def rows(chip_xy, core, q):
            return full_ref.at[pl.ds(shard * (2 * chip_xy[0] + chip_xy[1]) + half * core + quarter * q, quarter), :]

        def copy(k, piece, to):
            return pltpu.make_async_remote_copy(src_ref=piece, dst_ref=piece, send_sem=send_sems.at[k], recv_sem=recv_sems.at[k],
                                                device_id=to, device_id_type=MESH)

        def small(k, src, dst, to):
            return pltpu.make_async_remote_copy(src_ref=src, dst_ref=dst, send_sem=msend_sems.at[k], recv_sem=mrecv_sems.at[k],
                                                device_id=to, device_id_type=MESH)

        def x_chunk(i):
            return pltpu.make_async_copy(x_hbm.at[pl.ds(rows_at_once * i, rows_at_once), :], xbuf.at[i % 2], x_sems.at[i % 2])

        direct = [copy(0, rows((x, y), cc, 0), (*across_x, cc)), copy(2, rows((x, y), cc, 1), (*across_y, cc)),
                  copy(1, rows((x, y), cc, 1), (*across_x, cc)), copy(3, rows((x, y), cc, 0), (*across_y, cc))]
        landing = [(0, across_x, 0, (4, (*across_y, cc))), (2, across_y, 1, (5, (*across_x, cc))),
                   (1, across_x, 1, None), (3, across_y, 0, None), (4, far, 0, None), (5, far, 1, None)]
        passed = []

        def land(n):
            k, whose, q, onward = landing[n]
            piece = rows(whose, cc, q)
            copy(k, piece, (x, y, cc)).wait_recv()
            if onward is not None:
                passed.append(copy(onward[0], piece, onward[1]))
                passed[-1].start()
            passed.append(copy(6 + n, piece, (x, y, 1 - cc)))
            passed[-1].start()

        def from_sibling(n):
            k, whose, q, onward = landing[n]
            copy(6 + n, rows(whose, 1 - cc, q), (x, y, cc)).wait_recv()

        cbuf[me] = c_ref[...]
        gathers = [small(k - 1, cbuf.at[me], cbuf.at[me], (_flip(x, (k >> 2) & 1), _flip(y, (k >> 1) & 1), _flip(cc, k & 1)))
                   for k in range(1, N_DEV)]
        for cp in gathers:
            cp.start()
        direct[0].start()
        direct[1].start()
        w_load = pltpu.make_async_copy(w_hbm, w_vmem, w_sem)
        w_load.start()
        x_chunk(0).start()
        for cp in gathers:
            cp.wait()
        w_load.wait()
        for b in range(N_DEV):
            cv = cbuf[b]
            cact_ref[b:b + 1, :] = cv * _sigmoid(cv)
        acc = jnp.zeros((N_DEV, ADA_SHARD), F32)
        for k0 in range(0, kdim, kc):
            acc = acc + _dot(cact_ref[:, k0:k0 + kc].astype(BF16), w_vmem[k0:k0 + kc, :].astype(BF16), NN)
        for b in range(N_DEV):
            pbuf[b] = acc[b:b + 1, :]
        rbuf[chip] = pbuf[me]
        sends = [small(N_DEV - 1 + j, pbuf.at[4 * px + 2 * py + cc], rbuf.at[chip], (px, py, cc))
                 for j, (px, py) in enumerate(_other_chips(x, y))]
        for cp in sends:
            cp.start()
        direct[2].start()
        direct[3].start()
        land(0)
        land(1)
        for cp in sends:
            cp.wait()
        for k in range(N_CHIPS):
            cols = slice(ADA_SHARD * k, ADA_SHARD * (k + 1))
            mod_ref[:, cols] = rbuf[k] + b_ref[:, cols]

        for i in range(n_chunks):
            x_chunk(i).wait()
            if i + 1 < n_chunks:
                x_chunk(i + 1).start()
            xv = xbuf[i % 2]
            r = lax.rsqrt(jnp.mean(xv * xv, axis=-1, keepdims=True) + EPS)
            h_vmem[rows_at_once * i:rows_at_once * (i + 1), :] = (
                (xv * r * g_ref[...]) * (1.0 + mod_ref[:, d:2 * d]) + mod_ref[:, 0:d]).astype(BF16)
        h_out = pltpu.make_async_copy(h_vmem, h_hbm, h_sem)
        h_out.start()

        def project(shard_index):
            first = pl.multiple_of(2 * IN_TILE * shard_index, 128)
            b_load = pltpu.make_async_copy(full_ref.at[pl.ds(first, IN_TILE), :], bbuf, b_sem)
            b_load.start()
            b_load.wait()
            outs = []
            for m in range(n_m):
                if m >= 2:
                    outs[m - 2].wait()
                zbuf[m % 2] = _dot(h_vmem[m_rows * m:m_rows * (m + 1), :], bbuf[...], NT)
                outs.append(pltpu.make_async_copy(zbuf.at[m % 2], z_hbm.at[pl.ds(m_rows * m, m_rows), pl.ds(first, IN_TILE)],
                                                  z_sems.at[m % 2]))
                outs[m].start()
            for cp in outs[max(n_m - 2, 0):]:
                cp.wait()

        project(chip)
        land(2)
        land(3)
        for n in range(4):
            from_sibling(n)
        project(2 * across_x[0] + across_x[1])
        project(2 * across_y[0] + across_y[1])
        land(4)
        land(5)
        from_sibling(4)
        from_sibling(5)
        project(2 * far[0] + far[1])
        h_out.wait()
        for cp in direct + passed:
            cp.wait_send()

    return pl.pallas_call(
        body, name="w_in_gather_modulate_project",
        out_shape=(jax.ShapeDtypeStruct(full.shape, full.dtype), jax.ShapeDtypeStruct((1, 3 * D_MODEL), F32),
                   jax.ShapeDtypeStruct((N_DEV, D_MODEL), F32), jax.ShapeDtypeStruct((s, d), BF16),
                   jax.ShapeDtypeStruct((s, full.shape[0]), F32)),
        in_specs=[ANY, VMEM, ANY, VMEM, ANY, VMEM], out_specs=(ANY, VMEM, VMEM, ANY, ANY), input_output_aliases={0: 0},
        scratch_shapes=[pltpu.VMEM(w_ada.shape, F32), pltpu.VMEM((N_DEV, 1, D_MODEL), F32), pltpu.VMEM((N_DEV, 1, ADA_SHARD), F32),
                        pltpu.VMEM((N_CHIPS, 1, ADA_SHARD), F32), pltpu.VMEM((s, d), BF16), pltpu.VMEM((2, rows_at_once, d), F32),
                        pltpu.VMEM((IN_TILE, d), BF16), pltpu.VMEM((2, m_rows, IN_TILE), F32),
                        pltpu.SemaphoreType.DMA, pltpu.SemaphoreType.DMA((2,)), pltpu.SemaphoreType.DMA, pltpu.SemaphoreType.DMA((2,)),
                        pltpu.SemaphoreType.DMA,
                        pltpu.SemaphoreType.DMA((12,)), pltpu.SemaphoreType.DMA((12,)),
                        pltpu.SemaphoreType.DMA((N_DEV + 2,)), pltpu.SemaphoreType.DMA((N_DEV + 2,))],
        compiler_params=_params(),
    )(full, c, w_ada, b_ada, xs, norm_g)


def _to_sibling(g, name):
    def body(g_ref, r_ref, send_sem, recv_sem):
        x, y, cc = _place()
        cp = pltpu.make_async_remote_copy(src_ref=g_ref, dst_ref=r_ref, send_sem=send_sem, recv_sem=recv_sem,
                                          device_id=(x, y, 1 - cc), device_id_type=MESH)
        cp.start()
        cp.wait()

    return pl.pallas_call(
        body, name=name, out_shape=jax.ShapeDtypeStruct(g.shape, g.dtype), in_specs=[ANY], out_specs=ANY,
        scratch_shapes=[pltpu.SemaphoreType.DMA, pltpu.SemaphoreType.DMA], compiler_params=_params(),
    )(g)


def _chip_copies(pieces, p_ref, land_ref, send_sems, recv_sems):
    x, y, cc = _place()
    chip = 2 * x + y
    shard = land_ref.shape[1]
    others = _other_chips(x, y)
    cps = []
    for k, (j, r0, nr) in enumerate(pieces):
        px, py = others[j]
        cps.append(pltpu.make_async_remote_copy(
            src_ref=p_ref.at[pl.ds(shard * (2 * px + py) + r0, nr), :], dst_ref=land_ref.at[chip, pl.ds(r0, nr), :],
            send_sem=send_sems.at[k], recv_sem=recv_sems.at[k], device_id=(px, py, cc), device_id_type=MESH))
    return cps


def _owner_copies(p_ref, land_ref, send_sems, recv_sems):
    x, y, cc = _place()
    me = 4 * x + 2 * y + cc
    shard = land_ref.shape[1]
    cps = []
    for k in range(1, N_DEV):
        px, py, pc = _flip(x, (k >> 2) & 1), _flip(y, (k >> 1) & 1), _flip(cc, k & 1)
        block = p_ref.at[pl.ds(shard * (2 * px + py), shard), pl.ds(pl.multiple_of(HALF * pc, 128), HALF)]
        cps.append(pltpu.make_async_remote_copy(src_ref=block, dst_ref=land_ref.at[me], send_sem=send_sems.at[k - 1],
                                                recv_sem=recv_sems.at[k - 1], device_id=(px, py, pc), device_id_type=MESH))
    return cps


def _halves_swap(f, name):
    def body(f_in, f_ref, send_sem, recv_sem):
        x, y, cc = _place()
        mine = f_ref.at[:, pl.ds(pl.multiple_of(HALF * cc, 128), HALF)]
        cp = pltpu.make_async_remote_copy(src_ref=mine, dst_ref=mine, send_sem=send_sem, recv_sem=recv_sem,
                                          device_id=(x, y, 1 - cc), device_id_type=MESH)
        cp.start()
        cp.wait()

    return pl.pallas_call(
        body, name=name, out_shape=jax.ShapeDtypeStruct(f.shape, f.dtype),
        in_specs=[ANY], out_specs=ANY, input_output_aliases={0: 0},
        scratch_shapes=[pltpu.SemaphoreType.DMA, pltpu.SemaphoreType.DMA], compiler_params=_params(),
    )(f)


ROW_NORM_G, ROW_DMOD, ROW_FINAL_G, ROW_LN, ROW_SINKS, ROW_LOSS, ROW_SGU_B, MISC_ROWS = 0, 1, 4, 5, 6, 7, 8, 16


def _small_grads_sum(parts, after):
    part_names = ["dng", "dshift", "dscale", "dgate", "dfg", "dlg", "dlb", "dsinks", "dsgu_b", "dsgu_w", "loss"]
    n_parts = len(part_names)
    n_peers = N_DEV - 1

    def body(*refs):
        p = dict(zip(part_names, refs[:n_parts]))
        tot_ref, gw_ref, dmod_ref = refs[n_parts + 1:n_parts + 4]
        misc, cmisc, dsend, dbuf, wland, wsum, own_sem, send_sems, recv_sems = refs[n_parts + 4:]
        x, y, cc = _place()
        me = 4 * x + 2 * y + cc
        chip = 2 * x + y
        peers = [(_flip(x, (k >> 2) & 1), _flip(y, (k >> 1) & 1), _flip(cc, k & 1)) for k in range(1, N_DEV)]

        def remote(k, src, dst, peer):
            return pltpu.make_async_remote_copy(src_ref=src, dst_ref=dst, send_sem=send_sems.at[k], recv_sem=recv_sems.at[k],
                                                device_id=peer, device_id_type=MESH)

        dsend[:, 0:D_MODEL] = p["dshift"][...]
        dsend[:, D_MODEL:2 * D_MODEL] = p["dscale"][...]
        dsend[:, 2 * D_MODEL:3 * D_MODEL] = p["dgate"][...]

        def shard_cols(chip_index):
            return dsend.at[:, pl.ds(pl.multiple_of(ADA_SHARD * chip_index, 128), ADA_SHARD)]

        own_cols = pltpu.make_async_copy(shard_cols(chip), dbuf.at[me], own_sem)
        own_cols.start()
        gathers = [remote(k, shard_cols(2 * px + py), dbuf.at[me], (px, py, pc)) for k, (px, py, pc) in enumerate(peers)]
        wland[me] = p["dsgu_w"][me]
        scatters = [remote(n_peers + k, p["dsgu_w"].at[4 * px + 2 * py + pc], wland.at[me], (px, py, pc))
                    for k, (px, py, pc) in enumerate(peers)]

        misc[cc] = jnp.zeros((MISC_ROWS, D_MODEL), F32)
        misc[cc, ROW_NORM_G:ROW_NORM_G + 1, :] = p["dng"][...]
        misc[cc, ROW_DMOD:ROW_DMOD + 1, :] = p["dshift"][...]
        misc[cc, ROW_DMOD + 1:ROW_DMOD + 2, :] = p["dscale"][...]
        misc[cc, ROW_DMOD + 2:ROW_DMOD + 3, :] = p["dgate"][...]
        misc[cc, ROW_FINAL_G:ROW_FINAL_G + 1, :] = p["dfg"][...]
        misc[cc, ROW_LN:ROW_LN + 1, 0:D_SGU] = p["dlg"][...]
        misc[cc, ROW_LN:ROW_LN + 1, D_SGU:2 * D_SGU] = p["dlb"][...]
        misc[cc, ROW_SINKS:ROW_SINKS + 1, 0:N_Q_HEADS] = p["dsinks"][...]
        misc[cc, ROW_LOSS:ROW_LOSS + 1, 0:1] = p["loss"][...]
        misc[cc, ROW_SGU_B:ROW_SGU_B + SGU_GROUPS, 0:BLOCK] = p["dsgu_b"][...]
        pair = remote(3 * n_peers, misc.at[cc], misc.at[cc], (x, y, 1 - cc))
        for cp in gathers + scatters + [pair]:
            cp.start()
        pair.wait()
        cmisc[chip] = misc[0] + misc[1]
        quad = [remote(3 * n_peers + 1 + j, cmisc.at[chip], cmisc.at[chip], (_flip(x, bx), _flip(y, by), cc))
                for j, (bx, by) in enumerate(((0, 1), (1, 0), (1, 1)))]
        for cp in quad:
            cp.start()
        for cp in scatters:
            cp.wait()
        group = wland[0]
        for b in range(1, N_DEV):
            group = group + wland[b]
        wsum[me] = group
        spreads = [remote(2 * n_peers + k, wsum.at[me], wsum.at[me], peer) for k, peer in enumerate(peers)]
        for cp in spreads:
            cp.start()
        for cp in gathers + quad + spreads:
            cp.wait()

        tot = cmisc[0]
        for k in range(1, N_CHIPS):
            tot = tot + cmisc[k]
        tot_ref[...] = tot
        gw_ref[...] = wsum[...]
        own_cols.wait()
        for b in range(N_DEV):
            dmod_ref[b:b + 1, :] = dbuf[b]

    operands = [parts[n] for n in part_names]
    n_sems = 3 * n_peers + 1 + (N_CHIPS - 1)
    return pl.pallas_call(
        body, name="small_grads_sum",
        out_shape=(jax.ShapeDtypeStruct((MISC_ROWS, D_MODEL), F32), jax.ShapeDtypeStruct((SGU_GROUPS, BLOCK, BLOCK), F32),
                   jax.ShapeDtypeStruct((N_DEV, ADA_SHARD), F32)),
        in_specs=[VMEM] * len(operands) + [ANY], out_specs=(VMEM, VMEM, VMEM),
        scratch_shapes=[pltpu.VMEM((2, MISC_ROWS, D_MODEL), F32), pltpu.VMEM((N_CHIPS, MISC_ROWS, D_MODEL), F32),
                        pltpu.VMEM((1, 3 * D_MODEL), F32), pltpu.VMEM((N_DEV, 1, ADA_SHARD), F32),
                        pltpu.VMEM((N_DEV, BLOCK, BLOCK), F32), pltpu.VMEM((SGU_GROUPS, BLOCK, BLOCK), F32),
                        pltpu.SemaphoreType.DMA, pltpu.SemaphoreType.DMA((n_sems,)), pltpu.SemaphoreType.DMA((n_sems,))],
        compiler_params=_params(),
    )(*operands, after)


def _small_adamw(tot, gw, small):
    names = ["norm_g", "b_ada", "attn_sinks", "sgu_ln_g", "sgu_ln_b", "sgu_w", "sgu_b", "final_g"]

    def body(*refs):
        tot_ref, gw_ref = refs[:2]
        wmv = {n: refs[2 + 3 * i: 5 + 3 * i] for i, n in enumerate(names)}
        outs = refs[2 + 3 * len(names):]
        res = {n: outs[4 * i: 4 * i + 4] for i, n in enumerate(names)}
        outs[4 * len(names)][...] = tot_ref[ROW_LOSS:ROW_LOSS + 1, 0:1]

        def update(name, g, pick=lambda r: r[...], put=None):
            w_ref, m_ref, v_ref = wmv[name]
            delta, m2, v2 = _adamw(pick(w_ref), g, pick(m_ref), pick(v_ref))
            for ref, val in zip(res[name], (g, delta, m2, v2)):
                if put is None:
                    ref[...] = val
                else:
                    put(ref, val)

        update("norm_g", tot_ref[ROW_NORM_G:ROW_NORM_G + 1, :])
        update("final_g", tot_ref[ROW_FINAL_G:ROW_FINAL_G + 1, :])
        for t in range(3):
            cols = slice(D_MODEL * t, D_MODEL * (t + 1))

            def put_cols(ref, val, cols=cols):
                ref[:, cols] = val
            update("b_ada", tot_ref[ROW_DMOD + t:ROW_DMOD + t + 1, :], pick=lambda r, cols=cols: r[:, cols], put=put_cols)
        update("sgu_ln_g", tot_ref[ROW_LN:ROW_LN + 1, 0:D_SGU])
        update("sgu_ln_b", tot_ref[ROW_LN:ROW_LN + 1, D_SGU:2 * D_SGU])
        update("attn_sinks", tot_ref[ROW_SINKS:ROW_SINKS + 1, 0:N_Q_HEADS])
        update("sgu_b", tot_ref[ROW_SGU_B:ROW_SGU_B + SGU_GROUPS, 0:BLOCK])
        for g in range(SGU_GROUPS):
            def put_group(ref, val, g=g):
                ref[g] = val
            update("sgu_w", gw_ref[g], pick=lambda r, g=g: r[g], put=put_group)

    shapes = {"norm_g": (1, D_MODEL), "b_ada": (1, 3 * D_MODEL), "attn_sinks": (1, N_Q_HEADS), "sgu_ln_g": (1, D_SGU),
              "sgu_ln_b": (1, D_SGU), "sgu_w": (SGU_GROUPS, BLOCK, BLOCK), "sgu_b": (SGU_GROUPS, BLOCK), "final_g": (1, D_MODEL)}
    operands = [tot, gw]
    for n in names:
        operands += [a.reshape(shapes[n]) for a in small[n]]
    out_shape = []
    for n in names:
        out_shape += [jax.ShapeDtypeStruct(shapes[n], F32)] * 4
    out_shape.append(jax.ShapeDtypeStruct((1, 1), F32))
    outs = pl.pallas_call(
        body, name="small_adamw", out_shape=tuple(out_shape),
        in_specs=[VMEM] * len(operands), out_specs=tuple([VMEM] * len(out_shape)), compiler_params=_params(),
    )(*operands)
    return {n: outs[4 * i: 4 * i + 4] for i, n in enumerate(names)}, outs[4 * len(names)]


def _cast_into_gathered(w, tile, place, name, after):
    r, c = w.shape
    nblk = r // tile

    def body(place_ref, w_ref, after_ref, o_ref):
        o_ref[...] = w_ref[...].astype(BF16)

    return pl.pallas_call(
        body, name=name,
        grid_spec=pltpu.PrefetchScalarGridSpec(
            num_scalar_prefetch=1, grid=(nblk,), in_specs=[pl.BlockSpec((tile, c), lambda i, p: (i, 0)), ANY],
            out_specs=pl.BlockSpec((tile, c), lambda i, p: (p[1] * nblk + i, 0))),
        out_shape=jax.ShapeDtypeStruct((N_CHIPS * r, c), BF16),
        compiler_params=_params(("parallel",)),
    )(place, w, after)


def _in_proj_odd(h, win_t, z, after):
    s, d = h.shape
    tm = min(1024, s)
    n_odd = win_t.shape[0] // IN_TILE // 2

    def body(h_ref, w_ref, z_in, after_ref, z_ref):
        z_ref[...] = _dot(h_ref[...], w_ref[...], NT)

    return pl.pallas_call(
        body, name="in_proj_odd", grid=(s // tm, n_odd), out_shape=jax.ShapeDtypeStruct(z.shape, z.dtype),
        in_specs=[pl.BlockSpec((tm, d), lambda i, j: (i, 0)), pl.BlockSpec((IN_TILE, d), lambda i, j: (2 * j + 1, 0)), ANY, ANY],
        out_specs=pl.BlockSpec((tm, IN_TILE), lambda i, j: (i, 2 * j + 1)), input_output_aliases={2: 0},
        compiler_params=_params(("parallel", "parallel")),
    )(h, win_t, z, after)


def _matmul_nt(a, bt, tm, out_dtype, name, after):
    m, k = a.shape
    n = bt.shape[0]
    tm = min(tm, m)

    def body(a_ref, b_ref, after_ref, o_ref):
        o_ref[...] = _dot(a_ref[...], b_ref[...], NT).astype(out_dtype)

    return pl.pallas_call(
        body, name=name, grid=(m // tm,), out_shape=jax.ShapeDtypeStruct((m, n), out_dtype),
        in_specs=[pl.BlockSpec((tm, k), lambda i: (i, 0)), pl.BlockSpec((n, k), lambda i: (0, 0), pipeline_mode=pl.Buffered(1)), ANY],
        out_specs=pl.BlockSpec((tm, n), lambda i: (i, 0)),
        compiler_params=_params(("parallel",)),
    )(a, bt, after)


def _matmul_tn(a, b, tm, name):
    t, m = a.shape
    n = b.shape[1]
    tm = min(tm, m)

    def body(a_ref, b_ref, o_ref):
        o_ref[...] = _dot(a_ref[...], b_ref[...], TN).astype(BF16)

    return pl.pallas_call(
        body, name=name, grid=(m // tm,), out_shape=jax.ShapeDtypeStruct((m, n), BF16),
        in_specs=[pl.BlockSpec((t, tm), lambda i: (0, i)), pl.BlockSpec((t, n), lambda i: (0, 0))],
        out_specs=pl.BlockSpec((tm, n), lambda i: (i, 0)),
        compiler_params=_params(("parallel",)),
    )(a, b)


def _grad_matmul(a, b, half, tm, name, add=None):
    t, m = a.shape
    tm = min(tm, m)

    def body(half_ref, a_ref, b_ref, *rest):
        if add is None:
            (o_ref,) = rest
            o_ref[...] = _dot(a_ref[...], b_ref[...], TN).astype(BF16)
        else:
            add_ref, o_ref = rest
            o_ref[...] = (_dot(a_ref[...], b_ref[...], TN) + add_ref[...].astype(F32)).astype(BF16)

    in_specs = [pl.BlockSpec((t, tm), lambda i, h: (0, i)), pl.BlockSpec((t, HALF), lambda i, h: (0, h[0]))]
    operands = [a, b]
    if add is not None:
        in_specs.append(pl.BlockSpec((tm, HALF), lambda i, h: (i, 0)))
        operands.append(add)
    return pl.pallas_call(
        body, name=name,
        grid_spec=pltpu.PrefetchScalarGridSpec(num_scalar_prefetch=1, grid=(m // tm,), in_specs=in_specs,
                                               out_specs=pl.BlockSpec((tm, HALF), lambda i, h: (i, 0))),
        out_shape=jax.ShapeDtypeStruct((m, HALF), BF16),
        compiler_params=_params(("parallel",)),
    )(half, *operands)


def _mixer_masks(n):
    key = lax.broadcasted_iota(jnp.int32, (2 * BLOCK, Q_PER_KV * BLOCK), 0)
    qry = lax.broadcasted_iota(jnp.int32, (2 * BLOCK, Q_PER_KV * BLOCK), 1) & (BLOCK - 1)
    valid = (key > qry) & (key <= qry + BLOCK) & ((key >= BLOCK) | (n > 0))
    r2 = lax.broadcasted_iota(jnp.int32, (BLOCK, BLOCK), 0)
    c2 = lax.broadcasted_iota(jnp.int32, (BLOCK, BLOCK), 1)
    return valid, r2 >= c2


def _layer_norm_parts(vs):
    mu = jnp.mean(vs, axis=-1, keepdims=True)
    xc = vs - mu
    rstd = lax.rsqrt(jnp.mean(xc * xc, axis=-1, keepdims=True) + EPS)
    return xc * rstd, rstd


def _stacked_queries(z_ref, hk, scale=None):
    q = jnp.concatenate([z_ref[:, _head_cols(Q0, Q_PER_KV * hk + g)] for g in range(Q_PER_KV)], axis=0)
    return (q if scale is None else q * scale).astype(BF16)


def _head_cols(base, h):
    return slice(base + HEAD_DIM * h, base + HEAD_DIM * (h + 1))


def _sink_row(sink_ref, hk):
    return jnp.concatenate([jnp.broadcast_to(sink_ref[:, h:h + 1], (1, BLOCK))
                            for h in range(Q_PER_KV * hk, Q_PER_KV * (hk + 1))], axis=1)


def _attn_probs(qs, kh, valid, sk, scale=None):
    s = _dot(kh, qs, NT)
    s = jnp.where(valid, s if scale is None else s * scale, -jnp.inf)
    m = jnp.maximum(jnp.max(s, axis=0, keepdims=True), sk)
    e = jnp.exp(s - m)
    es = jnp.exp(sk - m)
    inv = 1.0 / (jnp.sum(e, axis=0, keepdims=True) + es)
    return e * inv, es * inv


def _kv_ext(z_ref, kvp_ref):
    k_ext = jnp.concatenate([kvp_ref[:, 0:D_KV], z_ref[:, K0:K0 + D_KV]], axis=0).astype(BF16)
    v_ext = jnp.concatenate([kvp_ref[:, D_KV:2 * D_KV], z_ref[:, V0:V0 + D_KV]], axis=0).astype(BF16)
    return k_ext, v_ext


def _sgu_mixed(w_ref, bt_ref, vn, tril):
    ws = [jnp.where(tril, w_ref[g], 0.0).astype(BF16) for g in range(SGU_GROUPS)]
    mixed = jnp.concatenate([_dot(ws[g], vn[:, GROUP_DIM * g:GROUP_DIM * (g + 1)], NN) + bt_ref[:, g:g + 1]
                             for g in range(SGU_GROUPS)], axis=1)
    return ws, mixed


def _mixer_fwd(z, sinks, ln_g, ln_b, sgu_w, sgu_bt):
    s = z.shape[0]
    nb = s // BLOCK

    def body(z_ref, kvp_ref, sink_ref, lg_ref, lb_ref, w_ref, bt_ref, cat_ref, o_ref):
        n = pl.program_id(0)
        valid, tril = _mixer_masks(n)
        k_ext, v_ext = _kv_ext(z_ref, kvp_ref)
        for hk in range(2):
            kv = slice(HEAD_DIM * hk, HEAD_DIM * (hk + 1))
            p, _ = _attn_probs(_stacked_queries(z_ref, hk), k_ext[:, kv], valid, _sink_row(sink_ref, hk), ATTN_SCALE)
            o_t = _dot(v_ext[:, kv], p.astype(BF16), TN)
            for g in range(Q_PER_KV):
                o_ref[:, _head_cols(0, Q_PER_KV * hk + g)] = o_t[:, BLOCK * g:BLOCK * (g + 1)].T
        ga = z_ref[:, GA0:GA0 + D_ATTN]
        cat_ref[:, 0:D_ATTN] = (o_ref[...] * (ga * _sigmoid(ga))).astype(BF16)
        xhat, _ = _layer_norm_parts(z_ref[:, VS0:VS0 + D_SGU])
        vn = (xhat * lg_ref[...] + lb_ref[...]).astype(BF16)
        _, mixed = _sgu_mixed(w_ref, bt_ref, vn, tril)
        gt = z_ref[:, GS0:GS0 + D_SGU]
        cat_ref[:, D_ATTN:D_MODEL] = ((z_ref[:, U0:U0 + D_SGU] * mixed) * (gt * _sigmoid(gt))).astype(BF16)

    const2 = lambda n: (0, 0)
    return pl.pallas_call(
        body, name="mixer_fwd", grid=(nb,),
        out_shape=(jax.ShapeDtypeStruct((s, D_MODEL), BF16), jax.ShapeDtypeStruct((s, D_ATTN), F32)),
        in_specs=[pl.BlockSpec((BLOCK, D_IN), lambda n: (n, 0)),
                  pl.BlockSpec((BLOCK, 2 * D_KV), lambda n: (jnp.maximum(n - 1, 0), K0 // (2 * D_KV))),
                  pl.BlockSpec((1, N_Q_HEADS), const2), pl.BlockSpec((1, D_SGU), const2), pl.BlockSpec((1, D_SGU), const2),
                  pl.BlockSpec((SGU_GROUPS, BLOCK, BLOCK), lambda n: (0, 0, 0)), pl.BlockSpec((BLOCK, SGU_GROUPS), const2)],
        out_specs=(pl.BlockSpec((BLOCK, D_MODEL), lambda n: (n, 0)), pl.BlockSpec((BLOCK, D_ATTN), lambda n: (n, 0))),
        compiler_params=_params(("parallel",)),
    )(z, z, sinks, ln_g, ln_b, sgu_w, sgu_bt)


def _out_proj_loss(cat, wout, x, target, mod, final_g):
    s, d = x.shape
    tm = min(256, s)

    def body(cat_ref, w_ref, x_ref, t_ref, mod_ref, fg_ref, dx1_ref, dy_ref, dfg_ref, dgate_ref, loss_ref):
        @pl.when(pl.program_id(0) == 0)
        def _():
            dfg_ref[...] = jnp.zeros_like(dfg_ref)
            dgate_ref[...] = jnp.zeros_like(dgate_ref)
            loss_ref[...] = jnp.zeros_like(loss_ref)

        gate = mod_ref[:, 2 * d:3 * d]
        fg = fg_ref[...]
        yv = _dot(cat_ref[...], w_ref[...], NN)
        x1 = x_ref[...] + gate * yv
        r = lax.rsqrt(jnp.mean(x1 * x1, axis=-1, keepdims=True) + EPS)
        xr = x1 * r
        err = xr * fg - t_ref[...]
        dout = err * (1.0 / d)
        gy = dout * fg
        dx1 = r * gy - xr * (r * r * jnp.mean(gy * x1, axis=-1, keepdims=True))
        dx1_ref[...] = dx1
        dy_ref[...] = (gate * dx1).astype(BF16)
        dfg_ref[...] += jnp.sum(dout * xr, axis=0, keepdims=True)
        dgate_ref[...] += jnp.sum(dx1 * yv, axis=0, keepdims=True)
        loss_ref[...] += 0.5 * jnp.sum(jnp.mean(err * err, axis=-1, keepdims=True), axis=0, keepdims=True)

    row = lambda i: (i, 0)
    const = lambda i: (0, 0)
    return pl.pallas_call(
        body, name="out_proj_loss", grid=(s // tm,),
        out_shape=(jax.ShapeDtypeStruct((s, d), F32), jax.ShapeDtypeStruct((s, d), BF16), jax.ShapeDtypeStruct((1, d), F32),
                   jax.ShapeDtypeStruct((1, d), F32), jax.ShapeDtypeStruct((1, 1), F32)),
        in_specs=[pl.BlockSpec((tm, d), row), pl.BlockSpec((d, d), const), pl.BlockSpec((tm, d), row), pl.BlockSpec((tm, d), row),
                  pl.BlockSpec((1, 3 * d), const), pl.BlockSpec((1, d), const)],
        out_specs=(pl.BlockSpec((tm, d), row), pl.BlockSpec((tm, d), row), pl.BlockSpec((1, d), const), pl.BlockSpec((1, d), const),
                   pl.BlockSpec((1, 1), const)),
        compiler_params=_params(("arbitrary",)),
    )(cat, wout, x, target, mod, final_g)


def _mixer_bwd(z, dcat, o, sinks, ln_g, ln_b, sgu_w, sgu_bt):
    s = z.shape[0]
    nb = s // BLOCK

    def body(z_ref, kvp_ref, dc_ref, o_ref, sink_ref, lg_ref, lb_ref, w_ref, bt_ref,
             dz_ref, dsink_ref, dlg_ref, dlb_ref, dw_ref, db_ref, pend, dkv, carry, dsink_acc, dbt_acc):
        n = pl.program_id(0)

        @pl.when(n == 0)
        def _():
            pend[...] = jnp.zeros_like(pend)
            carry[...] = jnp.zeros_like(carry)
            dsink_acc[...] = jnp.zeros_like(dsink_acc)
            dbt_acc[...] = jnp.zeros_like(dbt_acc)
            dlg_ref[...] = jnp.zeros_like(dlg_ref)
            dlb_ref[...] = jnp.zeros_like(dlb_ref)
            dw_ref[...] = jnp.zeros_like(dw_ref)

        dz_ref[:, 0:K0] = pend[:, 0:K0]
        dz_ref[:, GA0:D_IN] = pend[:, GA0:D_IN]

        @pl.when(n == nb)
        def _():
            dkv[...] = jnp.zeros_like(dkv)

        @pl.when(n < nb)
        def _():
            valid, tril = _mixer_masks(n)
            k_ext, v_ext = _kv_ext(z_ref, kvp_ref)
            ga = z_ref[:, GA0:GA0 + D_ATTN]
            sg = _sigmoid(ga)
            datt = dc_ref[:, 0:D_ATTN]
            pend[:, GA0:GA0 + D_ATTN] = (datt * o_ref[...] * (sg * (1.0 + ga * (1.0 - sg)))).astype(BF16)
            do_all = (datt * (ga * sg)).astype(BF16)
            for hk in range(2):
                kv = slice(HEAD_DIM * hk, HEAD_DIM * (hk + 1))
                kh, vh = k_ext[:, kv], v_ext[:, kv]
                qs = _stacked_queries(z_ref, hk, ATTN_SCALE)
                p, p_sink = _attn_probs(qs, kh, valid, _sink_row(sink_ref, hk))
                do = jnp.concatenate([do_all[:, _head_cols(0, Q_PER_KV * hk + g)] for g in range(Q_PER_KV)], axis=0)
                dp = _dot(vh, do, NT)
                delta = jnp.sum(p * dp, axis=0, keepdims=True)
                ds = (p * (dp - delta)).astype(BF16)
                dsk = -p_sink * delta
                dq_t = _dot(kh, ds, TN) * ATTN_SCALE
                for g in range(Q_PER_KV):
                    h = Q_PER_KV * hk + g
                    dsink_acc[:, h:h + 1] += jnp.sum(dsk[:, BLOCK * g:BLOCK * (g + 1)], axis=1, keepdims=True)
                    pend[:, _head_cols(Q0, h)] = dq_t[:, BLOCK * g:BLOCK * (g + 1)].T.astype(BF16)
                dkv[:, kv] = _dot(ds, qs, NN)
                dkv[:, D_KV + HEAD_DIM * hk:D_KV + HEAD_DIM * (hk + 1)] = _dot(p.astype(BF16), do, NN)

            xhat, rstd = _layer_norm_parts(z_ref[:, VS0:VS0 + D_SGU])
            lg = lg_ref[...]
            vn = (xhat * lg + lb_ref[...]).astype(BF16)
            ws, mixed = _sgu_mixed(w_ref, bt_ref, vn, tril)
            u = z_ref[:, U0:U0 + D_SGU]
            gt = z_ref[:, GS0:GS0 + D_SGU]
            sg = _sigmoid(gt)
            dsgu = dc_ref[:, D_ATTN:D_MODEL]
            dgated = dsgu * (gt * sg)
            pend[:, U0:U0 + D_SGU] = (dgated * mixed).astype(BF16)
            pend[:, GS0:GS0 + D_SGU] = (dsgu * (u * mixed) * (sg * (1.0 + gt * (1.0 - sg)))).astype(BF16)
            dmixed = dgated * u
            dmb = dmixed.astype(BF16)
            dvn_parts = []
            for g in range(SGU_GROUPS):
                gs_ = slice(GROUP_DIM * g, GROUP_DIM * (g + 1))
                dw_ref[g] += jnp.where(tril, _dot(dmb[:, gs_], vn[:, gs_], NT), 0.0)
                dbt_acc[:, g:g + 1] += jnp.sum(dmixed[:, gs_], axis=-1, keepdims=True)
                dvn_parts.append(_dot(ws[g], dmb[:, gs_], TN))
            dvn = jnp.concatenate(dvn_parts, axis=1)
            dlg_ref[...] += jnp.sum(dvn * xhat, axis=0, keepdims=True)
            dlb_ref[...] += jnp.sum(dvn, axis=0, keepdims=True)
            dxh = dvn * lg
            pend[:, VS0:VS0 + D_SGU] = (rstd * (dxh - jnp.mean(dxh, axis=-1, keepdims=True)
                                                - xhat * jnp.mean(dxh * xhat, axis=-1, keepdims=True))).astype(BF16)

        dz_ref[:, K0:GA0] = (carry[...] + dkv[0:BLOCK, :]).astype(BF16)
        carry[...] = dkv[BLOCK:2 * BLOCK, :]

        @pl.when(n == nb)
        def _():
            dsink_ref[...] = dsink_acc[:, 0:N_Q_HEADS]
            db_ref[...] = dbt_acc[...].T[0:SGU_GROUPS, :]

    cur = lambda n: (jnp.minimum(n, nb - 1), 0)
    const2 = lambda n: (0, 0)
    return pl.pallas_call(
        body, name="mixer_bwd", grid=(nb + 1,),
        out_shape=(jax.ShapeDtypeStruct((s, D_IN), BF16), jax.ShapeDtypeStruct((1, N_Q_HEADS), F32),
                   jax.ShapeDtypeStruct((1, D_SGU), F32), jax.ShapeDtypeStruct((1, D_SGU), F32),
                   jax.ShapeDtypeStruct((SGU_GROUPS, BLOCK, BLOCK), F32), jax.ShapeDtypeStruct((SGU_GROUPS, BLOCK), F32)),
        in_specs=[pl.BlockSpec((BLOCK, D_IN), cur),
                  pl.BlockSpec((BLOCK, 2 * D_KV), lambda n: (jnp.maximum(jnp.minimum(n, nb - 1) - 1, 0), K0 // (2 * D_KV))),
                  pl.BlockSpec((BLOCK, D_MODEL), cur), pl.BlockSpec((BLOCK, D_ATTN), cur),
                  pl.BlockSpec((1, N_Q_HEADS), const2), pl.BlockSpec((1, D_SGU), const2), pl.BlockSpec((1, D_SGU), const2),
                  pl.BlockSpec((SGU_GROUPS, BLOCK, BLOCK), lambda n: (0, 0, 0)), pl.BlockSpec((BLOCK, SGU_GROUPS), const2)],
        out_specs=(pl.BlockSpec((BLOCK, D_IN), lambda n: (jnp.maximum(n - 1, 0), 0)),
                   pl.BlockSpec((1, N_Q_HEADS), const2), pl.BlockSpec((1, D_SGU), const2), pl.BlockSpec((1, D_SGU), const2),
                   pl.BlockSpec((SGU_GROUPS, BLOCK, BLOCK), lambda n: (0, 0, 0)), pl.BlockSpec((SGU_GROUPS, BLOCK), const2)),
        scratch_shapes=[pltpu.VMEM((BLOCK, D_IN), BF16), pltpu.VMEM((2 * BLOCK, 2 * D_KV), F32), pltpu.VMEM((BLOCK, 2 * D_KV), F32),
                        pltpu.VMEM((1, BLOCK), F32), pltpu.VMEM((BLOCK, BLOCK), F32)],
        compiler_params=_params(("arbitrary",)),
    )(z, z, dcat, o, sinks, ln_g, ln_b, sgu_w, sgu_bt)


def _in_proj_bwd(dz, win_t, x, dx1, norm_g, mod):
    s, d = x.shape
    kdim = dz.shape[1]
    tm = min(256, s)

    def body(dz_ref, w_ref, x_ref, dx1_ref, g_ref, mod_ref, gx_ref, dshift_ref, dscale_ref, dng_ref):
        @pl.when(pl.program_id(0) == 0)
        def _():
            dshift_ref[...] = jnp.zeros_like(dshift_ref)
            dscale_ref[...] = jnp.zeros_like(dscale_ref)
            dng_ref[...] = jnp.zeros_like(dng_ref)

        dh = _dot(dz_ref[...], w_ref[...], NN)
        xv = x_ref[...]
        ng = g_ref[...]
        r = lax.rsqrt(jnp.mean(xv * xv, axis=-1, keepdims=True) + EPS)
        xn = xv * r
        dshift_ref[...] += jnp.sum(dh, axis=0, keepdims=True)
        dscale_ref[...] += jnp.sum(dh * (xn * ng), axis=0, keepdims=True)
        dhs = dh * (1.0 + mod_ref[:, d:2 * d])
        dng_ref[...] += jnp.sum(dhs * xn, axis=0, keepdims=True)
        dxn = dhs * ng
        gx_ref[...] = dx1_ref[...] + r * dxn - xn * (r * r * jnp.mean(dxn * xv, axis=-1, keepdims=True))

    row = lambda i: (i, 0)
    const = lambda i: (0, 0)
    return pl.pallas_call(
        body, name="in_proj_bwd", grid=(s // tm,),
        out_shape=(jax.ShapeDtypeStruct((s, d), F32),) + (jax.ShapeDtypeStruct((1, d), F32),) * 3,
        in_specs=[pl.BlockSpec((tm, kdim), row), pl.BlockSpec((kdim, d), const, pipeline_mode=pl.Buffered(1)),
                  pl.BlockSpec((tm, d), row), pl.BlockSpec((tm, d), row), pl.BlockSpec((1, d), const), pl.BlockSpec((1, 3 * d), const)],
        out_specs=(pl.BlockSpec((tm, d), row),) + (pl.BlockSpec((1, d), const),) * 3,
        compiler_params=_params(("arbitrary",)),
    )(dz, win_t, x, dx1, norm_g, mod)


def _sum_slots(slots, own, place, rows_tile, full_cols, name):
    n_slots, r, hcols = slots.shape
    nblk = r // rows_tile
    per_device = n_slots == N_DEV

    def body(place_ref, s_ref, own_ref, o_ref):
        mine = 2 * place_ref[1] + place_ref[0] if per_device else place_ref[1]
        acc = jnp.zeros((rows_tile, hcols), F32)
        for k in range(n_slots):
            acc = acc + jnp.where(mine == k, own_ref[...], s_ref[k]).astype(F32)
        o_ref[...] = acc

    return pl.pallas_call(
        body, name=name,
        grid_spec=pltpu.PrefetchScalarGridSpec(
            num_scalar_prefetch=1, grid=(nblk,),
            in_specs=[pl.BlockSpec((n_slots, rows_tile, hcols), lambda i, p: (0, i, 0)),
                      pl.BlockSpec((rows_tile, hcols), lambda i, p: (p[1] * nblk + i, p[0] if per_device else 0))],
            out_specs=pl.BlockSpec((rows_tile, hcols), lambda i, p: (i, p[0]))),
        out_shape=jax.ShapeDtypeStruct((r, full_cols), F32),
        compiler_params=_params(("parallel",)),
    )(place, slots, own)


def _adamw_rows(w, g, m, v, tile, name, after):
    r, c = w.shape

    def body(w_ref, g_ref, m_ref, v_ref, after_ref, go_ref, d_ref, mo_ref, vo_ref):
        g_ = g_ref[...]
        delta, m2, v2 = _adamw(w_ref[...], g_, m_ref[...], v_ref[...])
        go_ref[...] = g_
        d_ref[...] = delta
        mo_ref[...] = m2
        vo_ref[...] = v2

    blk = pl.BlockSpec((tile, c), lambda i: (i, 0))
    return pl.pallas_call(
        body, name=name, grid=(r // tile,), out_shape=(jax.ShapeDtypeStruct((r, c), F32),) * 4,
        in_specs=[blk] * 4 + [ANY], out_specs=(blk,) * 4,
        compiler_params=_params(("parallel",)),
    )(w, g, m, v, after)


def _adamw_ada(cact, dmod, w, m, v, tile=256):
    r, c = w.shape

    def body(c_ref, d_ref, w_ref, m_ref, v_ref, go_ref, dl_ref, mo_ref, vo_ref):
        g_ = _dot(c_ref[...].astype(BF16), d_ref[...].astype(BF16), TN)
        delta, m2, v2 = _adamw(w_ref[...], g_, m_ref[...], v_ref[...])
        go_ref[...] = g_
        dl_ref[...] = delta
        mo_ref[...] = m2
        vo_ref[...] = v2

    blk = pl.BlockSpec((tile, c), lambda i: (i, 0))
    return pl.pallas_call(
        body, name="adamw_ada", grid=(r // tile,), out_shape=(jax.ShapeDtypeStruct((r, c), F32),) * 4,
        in_specs=[pl.BlockSpec((N_DEV, tile), lambda i: (0, i)), pl.BlockSpec((N_DEV, c), lambda i: (0, 0)), blk, blk, blk],
        out_specs=(blk,) * 4,
        compiler_params=_params(("parallel",)),
    )(cact, dmod, w, m, v)


def kernel(x, c, norm_g, w_ada, b_ada, w_in, attn_sinks, sgu_ln_g, sgu_ln_b, sgu_w, sgu_b, w_out, final_g, loss_target, m_norm_g, m_w_ada, m_b_ada, m_w_in, m_attn_sinks, m_sgu_ln_g, m_sgu_ln_b, m_sgu_w, m_sgu_b, m_w_out, m_final_g, v_norm_g, v_w_ada, v_b_ada, v_w_in, v_attn_sinks, v_sgu_ln_g, v_sgu_ln_b, v_sgu_w, v_sgu_b, v_w_out, v_final_g):
    xs, tgt = x[0], loss_target[0]
    core = lax.axis_index("c").astype(jnp.int32).reshape(1)
    chip = 2 * lax.axis_index("x") + lax.axis_index("y")
    place = jnp.stack([core[0], chip.astype(jnp.int32)])
    sgu_bt = sgu_b[0].T

    w_in_t, m_in_t, v_in_t = (jnp.swapaxes(a[0], 0, 1) for a in (w_in, m_w_in, v_w_in))
    win_t = _cast_into_gathered(w_in_t, IN_SHARD // 4, place, "cast_w_in", c)
    win_t, mod, cact, h, z = _gather_w_in_modulate_project(win_t, c, w_ada[0], b_ada, xs, norm_g)
    wout = _cast_into_gathered(w_out[0], OUT_SHARD // 2, place, "cast_w_out", win_t)
    sems, bufs, token = _split_start(_gather_copies, [wout], 3, "w_out_gather_start", win_t)
    z = _in_proj_odd(h, win_t, z, token)
    cat, o = _mixer_fwd(z, attn_sinks, sgu_ln_g, sgu_ln_b, sgu_w[0], sgu_bt)
    wout = _gather_finish(_split_wait(_gather_copies, sems, bufs, cat, "w_out_gather_wait")[0], "w_out_gather_finish")
    dx1, dy, dfg, dgate, loss = _out_proj_loss(cat, wout, xs, tgt, mod, final_g.reshape(1, D_MODEL))

    po = _matmul_tn(cat, dy, 512, "w_out_grad")
    sems, bufs, token = _split_start(_owner_copies, [po, lax.empty((N_DEV, OUT_SHARD, HALF), BF16)], N_DEV - 1,
                                     "w_out_grad_to_owners_start", cat)
    dcat = _matmul_nt(dy, wout, 256, F32, "out_proj_bwd", token)
    dz, dsinks, dlg, dlb, dsgu_w, dsgu_b = _mixer_bwd(z, dcat, o, attn_sinks, sgu_ln_g, sgu_ln_b, sgu_w[0], sgu_bt)
    po, slots_o = _split_wait(_owner_copies, sems, bufs, dz, "w_out_grad_to_owners_wait")
    go = _halves_swap(_sum_slots(slots_o, po, place, OUT_SHARD // 2, D_MODEL, "w_out_grad_sum"), "w_out_grad_halves_swap")

    sw = _to_sibling(_grad_matmul(dz, h, 1 - core, 768, "w_in_grad_sibling"), "w_in_grad_to_sibling")
    pw = _grad_matmul(dz, h, core, 768, "w_in_grad_own", add=sw)
    first = functools.partial(_chip_copies, [(0, 0, IN_SHARD), (1, 0, IN_SHARD), (2, 0, IN_SHARD // 4)])
    rest = functools.partial(_chip_copies, [(2, IN_SHARD // 4, 3 * IN_SHARD // 4)])
    sems, bufs, token = _split_start(first, [pw, lax.empty((N_CHIPS, IN_SHARD, HALF), BF16)], 3, "w_in_grad_to_chips_start", sw)
    gx, dshift, dscale, dng = _in_proj_bwd(dz, win_t, xs, dx1, norm_g + token[0:1, 0:1], mod)
    bufs = _split_wait(first, sems, bufs, gx, "w_in_grad_to_chips_wait")

    parts = dict(dng=dng, dshift=dshift, dscale=dscale, dgate=dgate, dfg=dfg, dlg=dlg, dlb=dlb, dsinks=dsinks,
                 dsgu_b=dsgu_b, dsgu_w=dsgu_w, loss=loss)
    small = dict(norm_g=(norm_g, m_norm_g, v_norm_g), b_ada=(b_ada, m_b_ada, v_b_ada),
                 attn_sinks=(attn_sinks, m_attn_sinks, v_attn_sinks), sgu_ln_g=(sgu_ln_g, m_sgu_ln_g, v_sgu_ln_g),
                 sgu_ln_b=(sgu_ln_b, m_sgu_ln_b, v_sgu_ln_b), sgu_w=(sgu_w, m_sgu_w, v_sgu_w), sgu_b=(sgu_b, m_sgu_b, v_sgu_b),
                 final_g=(final_g, m_final_g, v_final_g))
    tot, gsgu_w, dmod_mine = _small_grads_sum(parts, bufs[1])
    sems, bufs, token = _split_start(rest, list(bufs), 1, "w_in_grad_rest_to_chips_start", tot)
    res, total = _small_adamw(tot, gsgu_w, small)
    res["w_ada"] = _adamw_ada(cact, dmod_mine + token[0:1, 0:1], w_ada[0], m_w_ada[0], v_w_ada[0])
    res["w_out"] = _adamw_rows(w_out[0], go, m_w_out[0], v_w_out[0], 256, "adamw_w_out", res["w_ada"][1])

    pw, slots_w = _split_wait(rest, sems, bufs, res["w_out"][1], "w_in_grad_rest_to_chips_wait")
    gw_t = _halves_swap(_sum_slots(slots_w, pw, place, IN_SHARD // 4, D_MODEL, "w_in_grad_sum"), "w_in_grad_halves_swap")
    res["w_in"] = tuple(jnp.swapaxes(a, 0, 1) for a in _adamw_rows(w_in_t, gw_t, m_in_t, v_in_t, IN_SHARD // 4, "adamw_w_in", gw_t))

    order = ["norm_g", "w_ada", "b_ada", "w_in", "attn_sinks", "sgu_ln_g", "sgu_ln_b", "sgu_w", "sgu_b", "w_out", "final_g"]
    like = dict(norm_g=norm_g, w_ada=w_ada, b_ada=b_ada, w_in=w_in, attn_sinks=attn_sinks, sgu_ln_g=sgu_ln_g, sgu_ln_b=sgu_ln_b,
                sgu_w=sgu_w, sgu_b=sgu_b, w_out=w_out, final_g=final_g)
    outs = [total[0, 0], gx[None]]
    for kind in range(4):
        outs += [res[n][kind].reshape(like[n].shape) for n in order]
    return tuple(outs)
```

```python
import functools
import math

import jax
import jax.numpy as jnp
from jax import lax
from jax.experimental import pallas as pl
from jax.experimental.pallas import tpu as pltpu

F32 = jnp.float32
BF16 = jnp.bfloat16

D_MODEL = 2048
D_ATTN = 1024
HEAD_DIM = 64
N_Q_HEADS = 16
Q_PER_KV = 8
D_KV = 128
BLOCK = 128
D_SGU = 1024
SGU_GROUPS = 8
GROUP_DIM = 128
D_IN = 5376
EPS = 1e-6
ATTN_SCALE = 1.0 / math.sqrt(HEAD_DIM)
N_CHIPS = 4
N_DEV = 8
IN_SHARD = D_IN // N_CHIPS
OUT_SHARD = D_MODEL // N_CHIPS
ADA_SHARD = 3 * D_MODEL // N_CHIPS
HALF = D_MODEL // 2
IN_TILE = 768

Q0, K0, V0, GA0, U0, VS0, GS0 = 0, 1024, 1152, 1280, 2304, 3328, 4352

ADAM_LR = 0.001
ADAM_B1 = 0.9
ADAM_B2 = 0.999
ADAM_EPS = 1e-08
ADAM_WD = 0.01
ADAM_STEP = 10

VMEM_LIMIT_V7X = 56 * 1024 * 1024

NN = (((1,), (0,)), ((), ()))
NT = (((1,), (1,)), ((), ()))
TN = (((0,), (0,)), ((), ()))
MESH = pl.DeviceIdType.MESH
ANY = pl.BlockSpec(memory_space=pl.ANY)
VMEM = pl.BlockSpec(memory_space=pltpu.VMEM)
HBM = pl.BlockSpec(memory_space=pltpu.HBM)
SEM = pl.BlockSpec(memory_space=pltpu.SEMAPHORE)
EFFECT = pltpu.SideEffectType.DATAFLOW_SIDE_EFFECTING


def _dot(a, b, dims):
    return lax.dot_general(a, b, dims, preferred_element_type=F32)


def _params(sem=None):
    return pltpu.CompilerParams(dimension_semantics=sem, vmem_limit_bytes=VMEM_LIMIT_V7X)


def _sigmoid(x):
    return jax.nn.sigmoid(x)


def _adamw(w, g, m, v):
    m = ADAM_B1 * m + (1.0 - ADAM_B1) * g
    v = ADAM_B2 * v + (1.0 - ADAM_B2) * (g * g)
    m_hat = m / (1.0 - ADAM_B1 ** ADAM_STEP)
    v_hat = v / (1.0 - ADAM_B2 ** ADAM_STEP)
    delta = -ADAM_LR * (m_hat / (jnp.sqrt(v_hat) + ADAM_EPS) + ADAM_WD * w)
    return delta, m, v


def _place():
    return lax.axis_index("x"), lax.axis_index("y"), lax.axis_index("c")


def _flip(p, bit):
    return 1 - p if bit else p


def _other_chips(x, y):
    return [(_flip(x, bx), _flip(y, by)) for bx, by in ((0, 1), (1, 0), (1, 1))]


def _split_start(copies, bufs, n_sems, name, after):
    n = len(bufs)

    def body(*refs):
        send_sems, recv_sems = refs[n + 1:n + 3]
        for cp in copies(*refs[:n], send_sems, recv_sems):
            cp.start()
        refs[-1][...] = jnp.zeros_like(refs[-1])

    bufs = [pltpu.with_memory_space_constraint(a, pltpu.HBM) for a in bufs]
    outs = pl.pallas_call(
        body, name=name,
        out_shape=(pltpu.SemaphoreType.DMA((n_sems,)), pltpu.SemaphoreType.DMA((n_sems,)))
        + tuple(pltpu.HBM(a.shape, a.dtype) for a in bufs) + (jax.ShapeDtypeStruct((8, 128), F32),),
        in_specs=[HBM] * n + [ANY], out_specs=(SEM, SEM) + (HBM,) * n + (VMEM,), input_output_aliases={k: k + 2 for k in range(n)},
        compiler_params=pltpu.CompilerParams(has_side_effects=EFFECT),
    )(*bufs, after)
    return outs[:2], list(outs[2:2 + n]), outs[-1]


def _split_wait(copies, sems, bufs, after, name):
    n = len(bufs)

    def body(*refs):
        send_sems, recv_sems = refs[n:n + 2]
        for cp in copies(*refs[:n], send_sems, recv_sems):
            cp.wait_send()
            cp.wait_recv()

    return pl.pallas_call(
        body, name=name, out_shape=tuple(pltpu.HBM(a.shape, a.dtype) for a in bufs),
        in_specs=[HBM] * n + [SEM, SEM, ANY], out_specs=(HBM,) * n, input_output_aliases={k: k for k in range(n)},
        compiler_params=pltpu.CompilerParams(has_side_effects=EFFECT),
    )(*bufs, *sems, after)


def _gather_copies(full_ref, send_sems, recv_sems):
    x, y, cc = _place()
    shard = full_ref.shape[0] // N_CHIPS
    half = shard // 2
    mine = full_ref.at[pl.ds(shard * (2 * x + y) + half * cc, half), :]
    return [pltpu.make_async_remote_copy(src_ref=mine, dst_ref=mine, send_sem=send_sems.at[j], recv_sem=recv_sems.at[j],
                                         device_id=(px, py, cc), device_id_type=MESH)
            for j, (px, py) in enumerate(_other_chips(x, y))]


def _gather_finish(full, name):
    shard = full.shape[0] // N_CHIPS
    half = shard // 2

    def body(full_in, full_ref, send_sems, recv_sems):
        x, y, cc = _place()
        cps = []
        for j, (px, py) in enumerate(_other_chips(x, y)):
            landed = full_ref.at[pl.ds(shard * (2 * px + py) + half * cc, half), :]
            cps.append(pltpu.make_async_remote_copy(src_ref=landed, dst_ref=landed, send_sem=send_sems.at[j], recv_sem=recv_sems.at[j],
                                                    device_id=(x, y, 1 - cc), device_id_type=MESH))
        for cp in cps:
            cp.start()
        for cp in cps:
            cp.wait()

    return pl.pallas_call(
        body, name=name, out_shape=jax.ShapeDtypeStruct(full.shape, full.dtype),
        in_specs=[ANY], out_specs=ANY, input_output_aliases={0: 0},
        scratch_shapes=[pltpu.SemaphoreType.DMA((3,)), pltpu.SemaphoreType.DMA((3,))],
        compiler_params=_params(),
    )(full)


def _gather_w_in_modulate_project(full, c, w_ada, b_ada, xs, norm_g):
    shard = full.shape[0] // N_CHIPS
    half = shard // 2
    quarter = half // 2
    kdim = w_ada.shape[0]
    kc = 256
    s, d = xs.shape
    rows_at_once = min(256, s)
    n_chunks = s // rows_at_once
    m_rows = min(512, s)
    n_m = s // m_rows

    def body(full_in, c_ref, w_hbm, b_ref, x_hbm, g_ref, full_ref, mod_ref, cact_ref, h_hbm, z_hbm,
             w_vmem, cbuf, pbuf, rbuf, h_vmem, xbuf, bbuf, zbuf, w_sem, x_sems, b_sem, z_sems, h_sem,
             send_sems, recv_sems, msend_sems, mrecv_sems):
        x, y, cc = _place()
        me = 4 * x + 2 * y + cc
        chip = 2 * x + y
        across_x, across_y, far = (1 - x, y), (x, 1 - y), (1 - x, 1 - y)

        def rows(chip_xy, core, q):
            return full_ref.at[pl.ds(shard * (2 * chip_xy[0] + chip_xy[1]) + half * core + quarter * q, quarter), :]

        def copy(k, piece, to):
            return pltpu.make_async_remote_copy(src_ref=piece, dst_ref=piece, send_sem=send_sems.at[k], recv_sem=recv_sems.at[k],
                                                device_id=to, device_id_type=MESH)

        def small(k, src, dst, to):
            return pltpu.make_async_remote_copy(src_ref=src, dst_ref=dst, send_sem=msend_sems.at[k], recv_sem=mrecv_sems.at[k],
                                                device_id=to, device_id_type=MESH)

        def x_chunk(i):
            return pltpu.make_async_copy(x_hbm.at[pl.ds(rows_at_once * i, rows_at_once), :], xbuf.at[i % 2], x_sems.at[i % 2])

        direct = [copy(0, rows((x, y), cc, 0), (*across_x, cc)), copy(2, rows((x, y), cc, 1), (*across_y, cc)),
                  copy(1, rows((x, y), cc, 1), (*across_x, cc)), copy(3, rows((x, y), cc, 0), (*across_y, cc))]
        landing = [(0, across_x, 0, (4, (*across_y, cc))), (2, across_y, 1, (5, (*across_x, cc))),
                   (1, across_x, 1, None), (3, across_y, 0, None), (4, far, 0, None), (5, far, 1, None)]
        passed = []

        def land(n):
            k, whose, q, onward = landing[n]
            piece = rows(whose, cc, q)
            copy(k, piece, (x, y, cc)).wait_recv()
            if onward is not None:
                passed.append(copy(onward[0], piece, onward[1]))
                passed[-1].start()
            passed.append(copy(6 + n, piece, (x, y, 1 - cc)))
            passed[-1].start()

        def from_sibling(n):
            k, whose, q, onward = landing[n]
            copy(6 + n, rows(whose, 1 - cc, q), (x, y, cc)).wait_recv()

        cbuf[me] = c_ref[...]
        gathers = [small(k - 1, cbuf.at[me], cbuf.at[me], (_flip(x, (k >> 2) & 1), _flip(y, (k >> 1) & 1), _flip(cc, k & 1)))
                   for k in range(1, N_DEV)]
        for cp in gathers:
            cp.start()
        direct[0].start()
        direct[1].start()
        w_load = pltpu.make_async_copy(w_hbm, w_vmem, w_sem)
        w_load.start()
        x_chunk(0).start()
        for cp in gathers:
            cp.wait()
        w_load.wait()
        for b in range(N_DEV):
            cv = cbuf[b]
            cact_ref[b:b + 1, :] = cv * _sigmoid(cv)
        acc = jnp.zeros((N_DEV, ADA_SHARD), F32)
        for k0 in range(0, kdim, kc):
            acc = acc + _dot(cact_ref[:, k0:k0 + kc].astype(BF16), w_vmem[k0:k0 + kc, :].astype(BF16), NN)
        for b in range(N_DEV):
            pbuf[b] = acc[b:b + 1, :]
        rbuf[chip] = pbuf[me]
        sends = [small(N_DEV - 1 + j, pbuf.at[4 * px + 2 * py + cc], rbuf.at[chip], (px, py, cc))
                 for j, (px, py) in enumerate(_other_chips(x, y))]
        for cp in sends:
            cp.start()
        direct[2].start()
        direct[3].start()
        land(0)
        land(1)
        for cp in sends:
            cp.wait()
        for k in range(N_CHIPS):
            cols = slice(ADA_SHARD * k, ADA_SHARD * (k + 1))
            mod_ref[:, cols] = rbuf[k] + b_ref[:, cols]

        for i in range(n_chunks):
            x_chunk(i).wait()
            if i + 1 < n_chunks:
                x_chunk(i + 1).start()
            xv = xbuf[i % 2]
            r = lax.rsqrt(jnp.mean(xv * xv, axis=-1, keepdims=True) + EPS)
            h_vmem[rows_at_once * i:rows_at_once * (i + 1), :] = (
                (xv * r * g_ref[...]) * (1.0 + mod_ref[:, d:2 * d]) + mod_ref[:, 0:d]).astype(BF16)
        h_out = pltpu.make_async_copy(h_vmem, h_hbm, h_sem)
        h_out.start()

        def project(tile):
            first = pl.multiple_of(IN_TILE * tile, 128)
            b_load = pltpu.make_async_copy(full_ref.at[pl.ds(first, IN_TILE), :], bbuf, b_sem)
            b_load.start()
            b_load.wait()
            outs = []
            for m in range(n_m):
                if m >= 2:
                    outs[m - 2].wait()
                zbuf[m % 2] = _dot(h_vmem[m_rows * m:m_rows * (m + 1), :], bbuf[...], NT)
                outs.append(pltpu.make_async_copy(zbuf.at[m % 2], z_hbm.at[pl.ds(m_rows * m, m_rows), pl.ds(first, IN_TILE)],
                                                  z_sems.at[m % 2]))
                outs[m].start()
            for cp in outs[max(n_m - 2, 0):]:
                cp.wait()

        project(2 * chip)
        land(2)
        land(3)
        for n in range(4):
            from_sibling(n)
        project(2 * (2 * across_x[0] + across_x[1]))
        project(2 * (2 * across_y[0] + across_y[1]))
        project(4 * x + 1)
        land(4)
        land(5)
        from_sibling(4)
        from_sibling(5)
        project(2 * (2 * far[0] + far[1]))
        h_out.wait()
        for cp in direct + passed:
            cp.wait_send()

    return pl.pallas_call(
        body, name="w_in_gather_modulate_project",
        out_shape=(jax.ShapeDtypeStruct(full.shape, full.dtype), jax.ShapeDtypeStruct((1, 3 * D_MODEL), F32),
                   jax.ShapeDtypeStruct((N_DEV, D_MODEL), F32), jax.ShapeDtypeStruct((s, d), BF16),
                   jax.ShapeDtypeStruct((s, full.shape[0]), F32)),
        in_specs=[ANY, VMEM, ANY, VMEM, ANY, VMEM], out_specs=(ANY, VMEM, VMEM, ANY, ANY), input_output_aliases={0: 0},
        scratch_shapes=[pltpu.VMEM(w_ada.shape, F32), pltpu.VMEM((N_DEV, 1, D_MODEL), F32), pltpu.VMEM((N_DEV, 1, ADA_SHARD), F32),
                        pltpu.VMEM((N_CHIPS, 1, ADA_SHARD), F32), pltpu.VMEM((s, d), BF16), pltpu.VMEM((2, rows_at_once, d), F32),
                        pltpu.VMEM((IN_TILE, d), BF16), pltpu.VMEM((2, m_rows, IN_TILE), F32),
                        pltpu.SemaphoreType.DMA, pltpu.SemaphoreType.DMA((2,)), pltpu.SemaphoreType.DMA, pltpu.SemaphoreType.DMA((2,)),
                        pltpu.SemaphoreType.DMA,
                        pltpu.SemaphoreType.DMA((12,)), pltpu.SemaphoreType.DMA((12,)),
                        pltpu.SemaphoreType.DMA((N_DEV + 2,)), pltpu.SemaphoreType.DMA((N_DEV + 2,))],
        compiler_params=_params(),
    )(full, c, w_ada, b_ada, xs, norm_g)


def _w_in_grad_chip_partial(dz, h):
    t_rows, m = dz.shape
    n_tiles = m // IN_TILE

    def body(dz_hbm, h_hbm, p_hbm, h_theirs, h_mine, abuf, sendbuf, recvbuf, obuf, h_sems, a_sems, o_sems, send_sems, recv_sems):
        x, y, cc = _place()

        def a_tile(t, n):
            return pltpu.make_async_copy(dz_hbm.at[:, pl.ds(IN_TILE * t, IN_TILE)], abuf.at[n % 2], a_sems.at[n % 2])

        def half_cols(core):
            return h_hbm.at[:, pl.ds(pl.multiple_of(HALF * core, 128), HALF)]

        loads = [pltpu.make_async_copy(half_cols(1 - cc), h_theirs, h_sems.at[0]),
                 pltpu.make_async_copy(half_cols(cc), h_mine, h_sems.at[1])]
        for cp in loads:
            cp.start()
        a_tile(0, 0).start()
        loads[0].wait()
        sent = []
        for t in range(n_tiles):
            a_tile(t, t).wait()
            a_tile((t + 1) % n_tiles, t + 1).start()
            sendbuf[t] = _dot(abuf[t % 2], h_theirs[...], TN).astype(BF16)
            sent.append(pltpu.make_async_remote_copy(src_ref=sendbuf.at[t], dst_ref=recvbuf.at[t], send_sem=send_sems.at[t],
                                                     recv_sem=recv_sems.at[t], device_id=(x, y, 1 - cc), device_id_type=MESH))
            sent[t].start()
        loads[1].wait()
        outs = []
        for t in range(n_tiles):
            n = n_tiles + t
            a_tile(t, n).wait()
            if t + 1 < n_tiles:
                a_tile(t + 1, n + 1).start()
            sent[t].wait_recv()
            if t >= 2:
                outs[t - 2].wait()
            obuf[t % 2] = (_dot(abuf[n % 2], h_mine[...], TN) + recvbuf[t].astype(F32)).astype(BF16)
            outs.append(pltpu.make_async_copy(obuf.at[t % 2], p_hbm.at[pl.ds(IN_TILE * t, IN_TILE), :], o_sems.at[t % 2]))
            outs[t].start()
        for cp in outs[max(n_tiles - 2, 0):]:
            cp.wait()
        for cp in sent:
            cp.wait_send()

    return pl.pallas_call(
        body, name="w_in_grad_chip_partial", out_shape=jax.ShapeDtypeStruct((m, HALF), BF16),
        in_specs=[ANY, ANY], out_specs=ANY,
        scratch_shapes=[pltpu.VMEM((t_rows, HALF), BF16), pltpu.VMEM((t_rows, HALF), BF16), pltpu.VMEM((2, t_rows, IN_TILE), BF16),
                        pltpu.VMEM((n_tiles, IN_TILE, HALF), BF16), pltpu.VMEM((n_tiles, IN_TILE, HALF), BF16),
                        pltpu.VMEM((2, IN_TILE, HALF), BF16),
                        pltpu.SemaphoreType.DMA((2,)), pltpu.SemaphoreType.DMA((2,)), pltpu.SemaphoreType.DMA((2,)),
                        pltpu.SemaphoreType.DMA((n_tiles,)), pltpu.SemaphoreType.DMA((n_tiles,))],
        compiler_params=_params(),
    )(dz, h)


def _chip_copies(pieces, p_ref, land_ref, send_sems, recv_sems):
    x, y, cc = _place()
    chip = 2 * x + y
    shard = land_ref.shape[1]
    others = _other_chips(x, y)
    cps = []
    for k, (j, r0, nr) in enumerate(pieces):
        px, py = others[j]
        cps.append(pltpu.make_async_remote_copy(
            src_ref=p_ref.at[pl.ds(shard * (2 * px + py) + r0, nr), :], dst_ref=land_ref.at[chip, pl.ds(r0, nr), :],
            send_sem=send_sems.at[k], recv_sem=recv_sems.at[k], device_id=(px, py, cc), device_id_type=MESH))
    return cps


def _owner_copies(p_ref, land_ref, send_sems, recv_sems):
    x, y, cc = _place()
    me = 4 * x + 2 * y + cc
    shard = land_ref.shape[1]
    cps = []
    for k in range(1, N_DEV):
        px, py, pc = _flip(x, (k >> 2) & 1), _flip(y, (k >> 1) & 1), _flip(cc, k & 1)
        block = p_ref.at[pl.ds(shard * (2 * px + py), shard), pl.ds(pl.multiple_of(HALF * pc, 128), HALF)]
        cps.append(pltpu.make_async_remote_copy(src_ref=block, dst_ref=land_ref.at[me], send_sem=send_sems.at[k - 1],
                                                recv_sem=recv_sems.at[k - 1], device_id=(px, py, pc), device_id_type=MESH))
    return cps


def _halves_swap(f, name):
    def body(f_in, f_ref, send_sem, recv_sem):
        x, y, cc = _place()
        mine = f_ref.at[:, pl.ds(pl.multiple_of(HALF * cc, 128), HALF)]
        cp = pltpu.make_async_remote_copy(src_ref=mine, dst_ref=mine, send_sem=send_sem, recv_sem=recv_sem,
                                          device_id=(x, y, 1 - cc), device_id_type=MESH)
        cp.start()
        cp.wait()

    return pl.pallas_call(
        body, name=name, out_shape=jax.ShapeDtypeStruct(f.shape, f.dtype),
        in_specs=[ANY], out_specs=ANY, input_output_aliases={0: 0},
        scratch_shapes=[pltpu.SemaphoreType.DMA, pltpu.SemaphoreType.DMA], compiler_params=_params(),
    )(f)


ROW_NORM_G, ROW_DMOD, ROW_FINAL_G, ROW_LN, ROW_SINKS, ROW_LOSS, ROW_SGU_B, MISC_ROWS = 0, 1, 4, 5, 6, 7, 8, 16


def _small_grads_sum(parts, after):
    part_names = ["dng", "dshift", "dscale", "dgate", "dfg", "dlg", "dlb", "dsinks", "dsgu_b", "dsgu_w", "loss"]
    n_parts = len(part_names)
    n_peers = N_DEV - 1

    def body(*refs):
        p = dict(zip(part_names, refs[:n_parts]))
        tot_ref, gw_ref, dmod_ref = refs[n_parts + 1:n_parts + 4]
        misc, cmisc, dsend, dbuf, wland, wsum, own_sem, send_sems, recv_sems = refs[n_parts + 4:]
        x, y, cc = _place()
        me = 4 * x + 2 * y + cc
        chip = 2 * x + y
        peers = [(_flip(x, (k >> 2) & 1), _flip(y, (k >> 1) & 1), _flip(cc, k & 1)) for k in range(1, N_DEV)]

        def remote(k, src, dst, peer):
            return pltpu.make_async_remote_copy(src_ref=src, dst_ref=dst, send_sem=send_sems.at[k], recv_sem=recv_sems.at[k],
                                                device_id=peer, device_id_type=MESH)

        dsend[:, 0:D_MODEL] = p["dshift"][...]
        dsend[:, D_MODEL:2 * D_MODEL] = p["dscale"][...]
        dsend[:, 2 * D_MODEL:3 * D_MODEL] = p["dgate"][...]

        def shard_cols(chip_index):
            return dsend.at[:, pl.ds(pl.multiple_of(ADA_SHARD * chip_index, 128), ADA_SHARD)]

        own_cols = pltpu.make_async_copy(shard_cols(chip), dbuf.at[me], own_sem)
        own_cols.start()
        gathers = [remote(k, shard_cols(2 * px + py), dbuf.at[me], (px, py, pc)) for k, (px, py, pc) in enumerate(peers)]
        wland[me] = p["dsgu_w"][me]
        scatters = [remote(n_peers + k, p["dsgu_w"].at[4 * px + 2 * py + pc], wland.at[me], (px, py, pc))
                    for k, (px, py, pc) in enumerate(peers)]

        misc[cc] = jnp.zeros((MISC_ROWS, D_MODEL), F32)
        misc[cc, ROW_NORM_G:ROW_NORM_G + 1, :] = p["dng"][...]
        misc[cc, ROW_DMOD:ROW_DMOD + 1, :] = p["dshift"][...]
        misc[cc, ROW_DMOD + 1:ROW_DMOD + 2, :] = p["dscale"][...]
        misc[cc, ROW_DMOD + 2:ROW_DMOD + 3, :] = p["dgate"][...]
        misc[cc, ROW_FINAL_G:ROW_FINAL_G + 1, :] = p["dfg"][...]
        misc[cc, ROW_LN:ROW_LN + 1, 0:D_SGU] = p["dlg"][...]
        misc[cc, ROW_LN:ROW_LN + 1, D_SGU:2 * D_SGU] = p["dlb"][...]
        misc[cc, ROW_SINKS:ROW_SINKS + 1, 0:N_Q_HEADS] = p["dsinks"][...]
        misc[cc, ROW_LOSS:ROW_LOSS + 1, 0:1] = p["loss"][...]
        misc[cc, ROW_SGU_B:ROW_SGU_B + SGU_GROUPS, 0:BLOCK] = p["dsgu_b"][...]
        pair = remote(3 * n_peers, misc.at[cc], misc.at[cc], (x, y, 1 - cc))
        for cp in gathers + scatters + [pair]:
            cp.start()
        pair.wait()
        cmisc[chip] = misc[0] + misc[1]
        quad = [remote(3 * n_peers + 1 + j, cmisc.at[chip], cmisc.at[chip], (_flip(x, bx), _flip(y, by), cc))
                for j, (bx, by) in enumerate(((0, 1), (1, 0), (1, 1)))]
        for cp in quad:
            cp.start()
        for cp in scatters:
            cp.wait()
        group = wland[0]
        for b in range(1, N_DEV):
            group = group + wland[b]
        wsum[me] = group
        spreads = [remote(2 * n_peers + k, wsum.at[me], wsum.at[me], peer) for k, peer in enumerate(peers)]
        for cp in spreads:
            cp.start()
        for cp in gathers + quad + spreads:
            cp.wait()

        tot = cmisc[0]
        for k in range(1, N_CHIPS):
            tot = tot + cmisc[k]
        tot_ref[...] = tot
        gw_ref[...] = wsum[...]
        own_cols.wait()
        for b in range(N_DEV):
            dmod_ref[b:b + 1, :] = dbuf[b]

    operands = [parts[n] for n in part_names]
    n_sems = 3 * n_peers + 1 + (N_CHIPS - 1)
    return pl.pallas_call(
        body, name="small_grads_sum",
        out_shape=(jax.ShapeDtypeStruct((MISC_ROWS, D_MODEL), F32), jax.ShapeDtypeStruct((SGU_GROUPS, BLOCK, BLOCK), F32),
                   jax.ShapeDtypeStruct((N_DEV, ADA_SHARD), F32)),
        in_specs=[VMEM] * len(operands) + [ANY], out_specs=(VMEM, VMEM, VMEM),
        scratch_shapes=[pltpu.VMEM((2, MISC_ROWS, D_MODEL), F32), pltpu.VMEM((N_CHIPS, MISC_ROWS, D_MODEL), F32),
                        pltpu.VMEM((1, 3 * D_MODEL), F32), pltpu.VMEM((N_DEV, 1, ADA_SHARD), F32),
                        pltpu.VMEM((N_DEV, BLOCK, BLOCK), F32), pltpu.VMEM((SGU_GROUPS, BLOCK, BLOCK), F32),
                        pltpu.SemaphoreType.DMA, pltpu.SemaphoreType.DMA((n_sems,)), pltpu.SemaphoreType.DMA((n_sems,))],
        compiler_params=_params(),
    )(*operands, after)


def _small_adamw(tot, gw, small):
    names = ["norm_g", "b_ada", "attn_sinks", "sgu_ln_g", "sgu_ln_b", "sgu_w", "sgu_b", "final_g"]

    def body(*refs):
        tot_ref, gw_ref = refs[:2]
        wmv = {n: refs[2 + 3 * i: 5 + 3 * i] for i, n in enumerate(names)}
        outs = refs[2 + 3 * len(names):]
        res = {n: outs[4 * i: 4 * i + 4] for i, n in enumerate(names)}
        outs[4 * len(names)][...] = tot_ref[ROW_LOSS:ROW_LOSS + 1, 0:1]

        def update(name, g, pick=lambda r: r[...], put=None):
            w_ref, m_ref, v_ref = wmv[name]
            delta, m2, v2 = _adamw(pick(w_ref), g, pick(m_ref), pick(v_ref))
            for ref, val in zip(res[name], (g, delta, m2, v2)):
                if put is None:
                    ref[...] = val
                else:
                    put(ref, val)

        update("norm_g", tot_ref[ROW_NORM_G:ROW_NORM_G + 1, :])
        update("final_g", tot_ref[ROW_FINAL_G:ROW_FINAL_G + 1, :])
        for t in range(3):
            cols = slice(D_MODEL * t, D_MODEL * (t + 1))

            def put_cols(ref, val, cols=cols):
                ref[:, cols] = val
            update("b_ada", tot_ref[ROW_DMOD + t:ROW_DMOD + t + 1, :], pick=lambda r, cols=cols: r[:, cols], put=put_cols)
        update("sgu_ln_g", tot_ref[ROW_LN:ROW_LN + 1, 0:D_SGU])
        update("sgu_ln_b", tot_ref[ROW_LN:ROW_LN + 1, D_SGU:2 * D_SGU])
        update("attn_sinks", tot_ref[ROW_SINKS:ROW_SINKS + 1, 0:N_Q_HEADS])
        update("sgu_b", tot_ref[ROW_SGU_B:ROW_SGU_B + SGU_GROUPS, 0:BLOCK])
        for g in range(SGU_GROUPS):
            def put_group(ref, val, g=g):
                ref[g] = val
            update("sgu_w", gw_ref[g], pick=lambda r, g=g: r[g], put=put_group)

    shapes = {"norm_g": (1, D_MODEL), "b_ada": (1, 3 * D_MODEL), "attn_sinks": (1, N_Q_HEADS), "sgu_ln_g": (1, D_SGU),
              "sgu_ln_b": (1, D_SGU), "sgu_w": (SGU_GROUPS, BLOCK, BLOCK), "sgu_b": (SGU_GROUPS, BLOCK), "final_g": (1, D_MODEL)}
    operands = [tot, gw]
    for n in names:
        operands += [a.reshape(shapes[n]) for a in small[n]]
    out_shape = []
    for n in names:
        out_shape += [jax.ShapeDtypeStruct(shapes[n], F32)] * 4
    out_shape.append(jax.ShapeDtypeStruct((1, 1), F32))
    outs = pl.pallas_call(
        body, name="small_adamw", out_shape=tuple(out_shape),
        in_specs=[VMEM] * len(operands), out_specs=tuple([VMEM] * len(out_shape)), compiler_params=_params(),
    )(*operands)
    return {n: outs[4 * i: 4 * i + 4] for i, n in enumerate(names)}, outs[4 * len(names)]


def _cast_into_gathered(w, tile, place, name, after):
    r, c = w.shape
    nblk = r // tile

    def body(place_ref, w_ref, after_ref, o_ref):
        o_ref[...] = w_ref[...].astype(BF16)

    return pl.pallas_call(
        body, name=name,
        grid_spec=pltpu.PrefetchScalarGridSpec(
            num_scalar_prefetch=1, grid=(nblk,), in_specs=[pl.BlockSpec((tile, c), lambda i, p: (i, 0)), ANY],
            out_specs=pl.BlockSpec((tile, c), lambda i, p: (p[1] * nblk + i, 0))),
        out_shape=jax.ShapeDtypeStruct((N_CHIPS * r, c), BF16),
        compiler_params=_params(("parallel",)),
    )(place, w, after)


def _in_proj_odd(h, win_t, z, place, after):
    s, d = h.shape
    tm = min(1024, s)

    def body(place_ref, h_ref, w_ref, z_in, after_ref, z_ref):
        z_ref[...] = _dot(h_ref[...], w_ref[...], NT)

    def tile(j, p):
        return 2 * j + 3 - 2 * (p[1] // 2)

    return pl.pallas_call(
        body, name="in_proj_odd",
        grid_spec=pltpu.PrefetchScalarGridSpec(
            num_scalar_prefetch=1, grid=(s // tm, 2),
            in_specs=[pl.BlockSpec((tm, d), lambda i, j, p: (i, 0)), pl.BlockSpec((IN_TILE, d), lambda i, j, p: (tile(j, p), 0)),
                      ANY, ANY],
            out_specs=pl.BlockSpec((tm, IN_TILE), lambda i, j, p: (i, tile(j, p)))),
        out_shape=jax.ShapeDtypeStruct(z.shape, z.dtype), input_output_aliases={3: 0},
        compiler_params=_params(("parallel", "parallel")),
    )(place, h, win_t, z, after)


def _matmul_nt(a, bt, tm, out_dtype, name, after):
    m, k = a.shape
    n = bt.shape[0]
    tm = min(tm, m)

    def body(a_ref, b_ref, after_ref, o_ref):
        o_ref[...] = _dot(a_ref[...], b_ref[...], NT).astype(out_dtype)

    return pl.pallas_call(
        body, name=name, grid=(m // tm,), out_shape=jax.ShapeDtypeStruct((m, n), out_dtype),
        in_specs=[pl.BlockSpec((tm, k), lambda i: (i, 0)), pl.BlockSpec((n, k), lambda i: (0, 0), pipeline_mode=pl.Buffered(1)), ANY],
        out_specs=pl.BlockSpec((tm, n), lambda i: (i, 0)),
        compiler_params=_params(("parallel",)),
    )(a, bt, after)


def _matmul_tn(a, b, tm, name):
    t, m = a.shape
    n = b.shape[1]
    tm = min(tm, m)

    def body(a_ref, b_ref, o_ref):
        o_ref[...] = _dot(a_ref[...], b_ref[...], TN).astype(BF16)

    return pl.pallas_call(
        body, name=name, grid=(m // tm,), out_shape=jax.ShapeDtypeStruct((m, n), BF16),
        in_specs=[pl.BlockSpec((t, tm), lambda i: (0, i)), pl.BlockSpec((t, n), lambda i: (0, 0))],
        out_specs=pl.BlockSpec((tm, n), lambda i: (i, 0)),
        compiler_params=_params(("parallel",)),
    )(a, b)


def _mixer_masks(n):
    key = lax.broadcasted_iota(jnp.int32, (2 * BLOCK, Q_PER_KV * BLOCK), 0)
    qry = lax.broadcasted_iota(jnp.int32, (2 * BLOCK, Q_PER_KV * BLOCK), 1) & (BLOCK - 1)
    valid = (key > qry) & (key <= qry + BLOCK) & ((key >= BLOCK) | (n > 0))
    r2 = lax.broadcasted_iota(jnp.int32, (BLOCK, BLOCK), 0)
    c2 = lax.broadcasted_iota(jnp.int32, (BLOCK, BLOCK), 1)
    return valid, r2 >= c2


def _layer_norm_parts(vs):
    mu = jnp.mean(vs, axis=-1, keepdims=True)
    xc = vs - mu
    rstd = lax.rsqrt(jnp.mean(xc * xc, axis=-1, keepdims=True) + EPS)
    return xc * rstd, rstd


def _stacked_queries(z_ref, hk, scale=None):
    q = jnp.concatenate([z_ref[:, _head_cols(Q0, Q_PER_KV * hk + g)] for g in range(Q_PER_KV)], axis=0)
    return (q if scale is None else q * scale).astype(BF16)


def _head_cols(base, h):
    return slice(base + HEAD_DIM * h, base + HEAD_DIM * (h + 1))


def _sink_row(sink_ref, hk):
    return jnp.concatenate([jnp.broadcast_to(sink_ref[:, h:h + 1], (1, BLOCK))
                            for h in range(Q_PER_KV * hk, Q_PER_KV * (hk + 1))], axis=1)


def _attn_probs(qs, kh, valid, sk, scale=None):
    s = _dot(kh, qs, NT)
    s = jnp.where(valid, s if scale is None else s * scale, -jnp.inf)
    m = jnp.maximum(jnp.max(s, axis=0, keepdims=True), sk)
    e = jnp.exp(s - m)
    es = jnp.exp(sk - m)
    inv = 1.0 / (jnp.sum(e, axis=0, keepdims=True) + es)
    return e * inv, es * inv


def _kv_ext(z_ref, kvp_ref):
    k_ext = jnp.concatenate([kvp_ref[:, 0:D_KV], z_ref[:, K0:K0 + D_KV]], axis=0).astype(BF16)
    v_ext = jnp.concatenate([kvp_ref[:, D_KV:2 * D_KV], z_ref[:, V0:V0 + D_KV]], axis=0).astype(BF16)
    return k_ext, v_ext


def _sgu_mixed(w_ref, bt_ref, vn, tril):
    ws = [jnp.where(tril, w_ref[g], 0.0).astype(BF16) for g in range(SGU_GROUPS)]
    mixed = jnp.concatenate([_dot(ws[g], vn[:, GROUP_DIM * g:GROUP_DIM * (g + 1)], NN) + bt_ref[:, g:g + 1]
                             for g in range(SGU_GROUPS)], axis=1)
    return ws, mixed


def _mixer_fwd(z, sinks, ln_g, ln_b, sgu_w, sgu_bt):
    s = z.shape[0]
    nb = s // BLOCK

    def body(z_ref, kvp_ref, sink_ref, lg_ref, lb_ref, w_ref, bt_ref, cat_ref, o_ref):
        n = pl.program_id(0)
        valid, tril = _mixer_masks(n)
        k_ext, v_ext = _kv_ext(z_ref, kvp_ref)
        for hk in range(2):
            kv = slice(HEAD_DIM * hk, HEAD_DIM * (hk + 1))
            p, _ = _attn_probs(_stacked_queries(z_ref, hk), k_ext[:, kv], valid, _sink_row(sink_ref, hk), ATTN_SCALE)
            o_t = _dot(v_ext[:, kv], p.astype(BF16), TN)
            for g in range(Q_PER_KV):
                o_ref[:, _head_cols(0, Q_PER_KV * hk + g)] = o_t[:, BLOCK * g:BLOCK * (g + 1)].T
        ga = z_ref[:, GA0:GA0 + D_ATTN]
        cat_ref[:, 0:D_ATTN] = (o_ref[...] * (ga * _sigmoid(ga))).astype(BF16)
        xhat, _ = _layer_norm_parts(z_ref[:, VS0:VS0 + D_SGU])
        vn = (xhat * lg_ref[...] + lb_ref[...]).astype(BF16)
        _, mixed = _sgu_mixed(w_ref, bt_ref, vn, tril)
        gt = z_ref[:, GS0:GS0 + D_SGU]
        cat_ref[:, D_ATTN:D_MODEL] = ((z_ref[:, U0:U0 + D_SGU] * mixed) * (gt * _sigmoid(gt))).astype(BF16)

    const2 = lambda n: (0, 0)
    return pl.pallas_call(
        body, name="mixer_fwd", grid=(nb,),
        out_shape=(jax.ShapeDtypeStruct((s, D_MODEL), BF16), jax.ShapeDtypeStruct((s, D_ATTN), F32)),
        in_specs=[pl.BlockSpec((BLOCK, D_IN), lambda n: (n, 0)),
                  pl.BlockSpec((BLOCK, 2 * D_KV), lambda n: (jnp.maximum(n - 1, 0), K0 // (2 * D_KV))),
                  pl.BlockSpec((1, N_Q_HEADS), const2), pl.BlockSpec((1, D_SGU), const2), pl.BlockSpec((1, D_SGU), const2),
                  pl.BlockSpec((SGU_GROUPS, BLOCK, BLOCK), lambda n: (0, 0, 0)), pl.BlockSpec((BLOCK, SGU_GROUPS), const2)],
        out_specs=(pl.BlockSpec((BLOCK, D_MODEL), lambda n: (n, 0)), pl.BlockSpec((BLOCK, D_ATTN), lambda n: (n, 0))),
        compiler_params=_params(("parallel",)),
    )(z, z, sinks, ln_g, ln_b, sgu_w, sgu_bt)


def _out_proj_loss(cat, wout, x, target, mod, final_g):
    s, d = x.shape
    tm = min(256, s)

    def body(cat_ref, w_ref, x_ref, t_ref, mod_ref, fg_ref, dx1_ref, dy_ref, dfg_ref, dgate_ref, loss_ref):
        @pl.when(pl.program_id(0) == 0)
        def _():
            dfg_ref[...] = jnp.zeros_like(dfg_ref)
            dgate_ref[...] = jnp.zeros_like(dgate_ref)
            loss_ref[...] = jnp.zeros_like(loss_ref)

        gate = mod_ref[:, 2 * d:3 * d]
        fg = fg_ref[...]
        yv = _dot(cat_ref[...], w_ref[...], NN)
        x1 = x_ref[...] + gate * yv
        r = lax.rsqrt(jnp.mean(x1 * x1, axis=-1, keepdims=True) + EPS)
        xr = x1 * r
        err = xr * fg - t_ref[...]
        dout = err * (1.0 / d)
        gy = dout * fg
        dx1 = r * gy - xr * (r * r * jnp.mean(gy * x1, axis=-1, keepdims=True))
        dx1_ref[...] = dx1
        dy_ref[...] = (gate * dx1).astype(BF16)
        dfg_ref[...] += jnp.sum(dout * xr, axis=0, keepdims=True)
        dgate_ref[...] += jnp.sum(dx1 * yv, axis=0, keepdims=True)
        loss_ref[...] += 0.5 * jnp.sum(jnp.mean(err * err, axis=-1, keepdims=True), axis=0, keepdims=True)

    row = lambda i: (i, 0)
    const = lambda i: (0, 0)
    return pl.pallas_call(
        body, name="out_proj_loss", grid=(s // tm,),
        out_shape=(jax.ShapeDtypeStruct((s, d), F32), jax.ShapeDtypeStruct((s, d), BF16), jax.ShapeDtypeStruct((1, d), F32),
                   jax.ShapeDtypeStruct((1, d), F32), jax.ShapeDtypeStruct((1, 1), F32)),
        in_specs=[pl.BlockSpec((tm, d), row), pl.BlockSpec((d, d), const), pl.BlockSpec((tm, d), row), pl.BlockSpec((tm, d), row),
                  pl.BlockSpec((1, 3 * d), const), pl.BlockSpec((1, d), const)],
        out_specs=(pl.BlockSpec((tm, d), row), pl.BlockSpec((tm, d), row), pl.BlockSpec((1, d), const), pl.BlockSpec((1, d), const),
                   pl.BlockSpec((1, 1), const)),
        compiler_params=_params(("arbitrary",)),
    )(cat, wout, x, target, mod, final_g)


def _mixer_bwd(z, dcat, o, sinks, ln_g, ln_b, sgu_w, sgu_bt):
    s = z.shape[0]
    nb = s // BLOCK

    def body(z_ref, kvp_ref, dc_ref, o_ref, sink_ref, lg_ref, lb_ref, w_ref, bt_ref,
             dz_ref, dsink_ref, dlg_ref, dlb_ref, dw_ref, db_ref, pend, dkv, carry, dsink_acc, dbt_acc):
        n = pl.program_id(0)

        @pl.when(n == 0)
        def _():
            pend[...] = jnp.zeros_like(pend)
            carry[...] = jnp.zeros_like(carry)
            dsink_acc[...] = jnp.zeros_like(dsink_acc)
            dbt_acc[...] = jnp.zeros_like(dbt_acc)
            dlg_ref[...] = jnp.zeros_like(dlg_ref)
            dlb_ref[...] = jnp.zeros_like(dlb_ref)
            dw_ref[...] = jnp.zeros_like(dw_ref)

        dz_ref[:, 0:K0] = pend[:, 0:K0]
        dz_ref[:, GA0:D_IN] = pend[:, GA0:D_IN]

        @pl.when(n == nb)
        def _():
            dkv[...] = jnp.zeros_like(dkv)

        @pl.when(n < nb)
        def _():
            valid, tril = _mixer_masks(n)
            k_ext, v_ext = _kv_ext(z_ref, kvp_ref)
            ga = z_ref[:, GA0:GA0 + D_ATTN]
            sg = _sigmoid(ga)
            datt = dc_ref[:, 0:D_ATTN]
            pend[:, GA0:GA0 + D_ATTN] = (datt * o_ref[...] * (sg * (1.0 + ga * (1.0 - sg)))).astype(BF16)
            do_all = (datt * (ga * sg)).astype(BF16)
            for hk in range(2):
                kv = slice(HEAD_DIM * hk, HEAD_DIM * (hk + 1))
                kh, vh = k_ext[:, kv], v_ext[:, kv]
                qs = _stacked_queries(z_ref, hk, ATTN_SCALE)
                p, p_sink = _attn_probs(qs, kh, valid, _sink_row(sink_ref, hk))
                do = jnp.concatenate([do_all[:, _head_cols(0, Q_PER_KV * hk + g)] for g in range(Q_PER_KV)], axis=0)
                dp = _dot(vh, do, NT)
                delta = jnp.sum(p * dp, axis=0, keepdims=True)
                ds = (p * (dp - delta)).astype(BF16)
                dsk = -p_sink * delta
                dq_t = _dot(kh, ds, TN) * ATTN_SCALE
                for g in range(Q_PER_KV):
                    h = Q_PER_KV * hk + g
                    dsink_acc[:, h:h + 1] += jnp.sum(dsk[:, BLOCK * g:BLOCK * (g + 1)], axis=1, keepdims=True)
                    pend[:, _head_cols(Q0, h)] = dq_t[:, BLOCK * g:BLOCK * (g + 1)].T.astype(BF16)
                dkv[:, kv] = _dot(ds, qs, NN)
                dkv[:, D_KV + HEAD_DIM * hk:D_KV + HEAD_DIM * (hk + 1)] = _dot(p.astype(BF16), do, NN)

            xhat, rstd = _layer_norm_parts(z_ref[:, VS0:VS0 + D_SGU])
            lg = lg_ref[...]
            vn = (xhat * lg + lb_ref[...]).astype(BF16)
            ws, mixed = _sgu_mixed(w_ref, bt_ref, vn, tril)
            u = z_ref[:, U0:U0 + D_SGU]
            gt = z_ref[:, GS0:GS0 + D_SGU]
            sg = _sigmoid(gt)
            dsgu = dc_ref[:, D_ATTN:D_MODEL]
            dgated = dsgu * (gt * sg)
            pend[:, U0:U0 + D_SGU] = (dgated * mixed).astype(BF16)
            pend[:, GS0:GS0 + D_SGU] = (dsgu * (u * mixed) * (sg * (1.0 + gt * (1.0 - sg)))).astype(BF16)
            dmixed = dgated * u
            dmb = dmixed.astype(BF16)
            dvn_parts = []
            for g in range(SGU_GROUPS):
                gs_ = slice(GROUP_DIM * g, GROUP_DIM * (g + 1))
                dw_ref[g] += jnp.where(tril, _dot(dmb[:, gs_], vn[:, gs_], NT), 0.0)
                dbt_acc[:, g:g + 1] += jnp.sum(dmixed[:, gs_], axis=-1, keepdims=True)
                dvn_parts.append(_dot(ws[g], dmb[:, gs_], TN))
            dvn = jnp.concatenate(dvn_parts, axis=1)
            dlg_ref[...] += jnp.sum(dvn * xhat, axis=0, keepdims=True)
            dlb_ref[...] += jnp.sum(dvn, axis=0, keepdims=True)
            dxh = dvn * lg
            pend[:, VS0:VS0 + D_SGU] = (rstd * (dxh - jnp.mean(dxh, axis=-1, keepdims=True)
                                                - xhat * jnp.mean(dxh * xhat, axis=-1, keepdims=True))).astype(BF16)

        dz_ref[:, K0:GA0] = (carry[...] + dkv[0:BLOCK, :]).astype(BF16)
        carry[...] = dkv[BLOCK:2 * BLOCK, :]

        @pl.when(n == nb)
        def _():
            dsink_ref[...] = dsink_acc[:, 0:N_Q_HEADS]
            db_ref[...] = dbt_acc[...].T[0:SGU_GROUPS, :]

    cur = lambda n: (jnp.minimum(n, nb - 1), 0)
    const2 = lambda n: (0, 0)
    return pl.pallas_call(
        body, name="mixer_bwd", grid=(nb + 1,),
        out_shape=(jax.ShapeDtypeStruct((s, D_IN), BF16), jax.ShapeDtypeStruct((1, N_Q_HEADS), F32),
                   jax.ShapeDtypeStruct((1, D_SGU), F32), jax.ShapeDtypeStruct((1, D_SGU), F32),
                   jax.ShapeDtypeStruct((SGU_GROUPS, BLOCK, BLOCK), F32), jax.ShapeDtypeStruct((SGU_GROUPS, BLOCK), F32)),
        in_specs=[pl.BlockSpec((BLOCK, D_IN), cur),
                  pl.BlockSpec((BLOCK, 2 * D_KV), lambda n: (jnp.maximum(jnp.minimum(n, nb - 1) - 1, 0), K0 // (2 * D_KV))),
                  pl.BlockSpec((BLOCK, D_MODEL), cur), pl.BlockSpec((BLOCK, D_ATTN), cur),
                  pl.BlockSpec((1, N_Q_HEADS), const2), pl.BlockSpec((1, D_SGU), const2), pl.BlockSpec((1, D_SGU), const2),
                  pl.BlockSpec((SGU_GROUPS, BLOCK, BLOCK), lambda n: (0, 0, 0)), pl.BlockSpec((BLOCK, SGU_GROUPS), const2)],
        out_specs=(pl.BlockSpec((BLOCK, D_IN), lambda n: (jnp.maximum(n - 1, 0), 0)),
                   pl.BlockSpec((1, N_Q_HEADS), const2), pl.BlockSpec((1, D_SGU), const2), pl.BlockSpec((1, D_SGU), const2),
                   pl.BlockSpec((SGU_GROUPS, BLOCK, BLOCK), lambda n: (0, 0, 0)), pl.BlockSpec((SGU_GROUPS, BLOCK), const2)),
        scratch_shapes=[pltpu.VMEM((BLOCK, D_IN), BF16), pltpu.VMEM((2 * BLOCK, 2 * D_KV), F32), pltpu.VMEM((BLOCK, 2 * D_KV), F32),
                        pltpu.VMEM((1, BLOCK), F32), pltpu.VMEM((BLOCK, BLOCK), F32)],
        compiler_params=_params(("arbitrary",)),
    )(z, z, dcat, o, sinks, ln_g, ln_b, sgu_w, sgu_bt)


def _in_proj_bwd(dz, win_t, x, dx1, norm_g, mod):
    s, d = x.shape
    kdim = dz.shape[1]
    tm = min(256, s)

    def body(dz_ref, w_ref, x_ref, dx1_ref, g_ref, mod_ref, gx_ref, dshift_ref, dscale_ref, dng_ref):
        @pl.when(pl.program_id(0) == 0)
        def _():
            dshift_ref[...] = jnp.zeros_like(dshift_ref)
            dscale_ref[...] = jnp.zeros_like(dscale_ref)
            dng_ref[...] = jnp.zeros_like(dng_ref)

        dh = _dot(dz_ref[...], w_ref[...], NN)
        xv = x_ref[...]
        ng = g_ref[...]
        r = lax.rsqrt(jnp.mean(xv * xv, axis=-1, keepdims=True) + EPS)
        xn = xv * r
        dshift_ref[...] += jnp.sum(dh, axis=0, keepdims=True)
        dscale_ref[...] += jnp.sum(dh * (xn * ng), axis=0, keepdims=True)
        dhs = dh * (1.0 + mod_ref[:, d:2 * d])
        dng_ref[...] += jnp.sum(dhs * xn, axis=0, keepdims=True)
        dxn = dhs * ng
        gx_ref[...] = dx1_ref[...] + r * dxn - xn * (r * r * jnp.mean(dxn * xv, axis=-1, keepdims=True))

    row = lambda i: (i, 0)
    const = lambda i: (0, 0)
    return pl.pallas_call(
        body, name="in_proj_bwd", grid=(s // tm,),
        out_shape=(jax.ShapeDtypeStruct((s, d), F32),) + (jax.ShapeDtypeStruct((1, d), F32),) * 3,
        in_specs=[pl.BlockSpec((tm, kdim), row), pl.BlockSpec((kdim, d), const, pipeline_mode=pl.Buffered(1)),
                  pl.BlockSpec((tm, d), row), pl.BlockSpec((tm, d), row), pl.BlockSpec((1, d), const), pl.BlockSpec((1, 3 * d), const)],
        out_specs=(pl.BlockSpec((tm, d), row),) + (pl.BlockSpec((1, d), const),) * 3,
        compiler_params=_params(("arbitrary",)),
    )(dz, win_t, x, dx1, norm_g, mod)


def _sum_slots(slots, own, place, rows_tile, full_cols, name):
    n_slots, r, hcols = slots.shape
    nblk = r // rows_tile
    per_device = n_slots == N_DEV

    def body(place_ref, s_ref, own_ref, o_ref):
        mine = 2 * place_ref[1] + place_ref[0] if per_device else place_ref[1]
        acc = jnp.zeros((rows_tile, hcols), F32)
        for k in range(n_slots):
            acc = acc + jnp.where(mine == k, own_ref[...], s_ref[k]).astype(F32)
        o_ref[...] = acc

    return pl.pallas_call(
        body, name=name,
        grid_spec=pltpu.PrefetchScalarGridSpec(
            num_scalar_prefetch=1, grid=(nblk,),
            in_specs=[pl.BlockSpec((n_slots, rows_tile, hcols), lambda i, p: (0, i, 0)),
                      pl.BlockSpec((rows_tile, hcols), lambda i, p: (p[1] * nblk + i, p[0] if per_device else 0))],
            out_specs=pl.BlockSpec((rows_tile, hcols), lambda i, p: (i, p[0]))),
        out_shape=jax.ShapeDtypeStruct((r, full_cols), F32),
        compiler_params=_params(("parallel",)),
    )(place, slots, own)


def _adamw_rows(w, g, m, v, tile, name, after):
    r, c = w.shape

    def body(w_ref, g_ref, m_ref, v_ref, after_ref, go_ref, d_ref, mo_ref, vo_ref):
        g_ = g_ref[...]
        delta, m2, v2 = _adamw(w_ref[...], g_, m_ref[...], v_ref[...])
        go_ref[...] = g_
        d_ref[...] = delta
        mo_ref[...] = m2
        vo_ref[...] = v2

    blk = pl.BlockSpec((tile, c), lambda i: (i, 0))
    return pl.pallas_call(
        body, name=name, grid=(r // tile,), out_shape=(jax.ShapeDtypeStruct((r, c), F32),) * 4,
        in_specs=[blk] * 4 + [ANY], out_specs=(blk,) * 4,
        compiler_params=_params(("parallel",)),
    )(w, g, m, v, after)


def _adamw_ada(cact, dmod, w, m, v, tile=256):
    r, c = w.shape

    def body(c_ref, d_ref, w_ref, m_ref, v_ref, go_ref, dl_ref, mo_ref, vo_ref):
        g_ = _dot(c_ref[...].astype(BF16), d_ref[...].astype(BF16), TN)
        delta, m2, v2 = _adamw(w_ref[...], g_, m_ref[...], v_ref[...])
        go_ref[...] = g_
        dl_ref[...] = delta
        mo_ref[...] = m2
        vo_ref[...] = v2

    blk = pl.BlockSpec((tile, c), lambda i: (i, 0))
    return pl.pallas_call(
        body, name="adamw_ada", grid=(r // tile,), out_shape=(jax.ShapeDtypeStruct((r, c), F32),) * 4,
        in_specs=[pl.BlockSpec((N_DEV, tile), lambda i: (0, i)), pl.BlockSpec((N_DEV, c), lambda i: (0, 0)), blk, blk, blk],
        out_specs=(blk,) * 4,
        compiler_params=_params(("parallel",)),
    )(cact, dmod, w, m, v)


def kernel(x, c, norm_g, w_ada, b_ada, w_in, attn_sinks, sgu_ln_g, sgu_ln_b, sgu_w, sgu_b, w_out, final_g, loss_target, m_norm_g, m_w_ada, m_b_ada, m_w_in, m_attn_sinks, m_sgu_ln_g, m_sgu_ln_b, m_sgu_w, m_sgu_b, m_w_out, m_final_g, v_norm_g, v_w_ada, v_b_ada, v_w_in, v_attn_sinks, v_sgu_ln_g, v_sgu_ln_b, v_sgu_w, v_sgu_b, v_w_out, v_final_g):
    xs, tgt = x[0], loss_target[0]
    core = lax.axis_index("c").astype(jnp.int32).reshape(1)
    chip = 2 * lax.axis_index("x") + lax.axis_index("y")
    place = jnp.stack([core[0], chip.astype(jnp.int32)])
    sgu_bt = sgu_b[0].T

    w_in_t, m_in_t, v_in_t = (jnp.swapaxes(a[0], 0, 1) for a in (w_in, m_w_in, v_w_in))
    win_t = _cast_into_gathered(w_in_t, IN_SHARD // 4, place, "cast_w_in", c)
    win_t, mod, cact, h, z = _gather_w_in_modulate_project(win_t, c, w_ada[0], b_ada, xs, norm_g)
    wout = _cast_into_gathered(w_out[0], OUT_SHARD // 2, place, "cast_w_out", win_t)
    sems, bufs, token = _split_start(_gather_copies, [wout], 3, "w_out_gather_start", win_t)
    z = _in_proj_odd(h, win_t, z, place, token)
    cat, o = _mixer_fwd(z, attn_sinks, sgu_ln_g, sgu_ln_b, sgu_w[0], sgu_bt)
    wout = _gather_finish(_split_wait(_gather_copies, sems, bufs, cat, "w_out_gather_wait")[0], "w_out_gather_finish")
    dx1, dy, dfg, dgate, loss = _out_proj_loss(cat, wout, xs, tgt, mod, final_g.reshape(1, D_MODEL))

    po = _matmul_tn(cat, dy, 512, "w_out_grad")
    sems, bufs, token = _split_start(_owner_copies, [po, lax.empty((N_DEV, OUT_SHARD, HALF), BF16)], N_DEV - 1,
                                     "w_out_grad_to_owners_start", cat)
    dcat = _matmul_nt(dy, wout, 256, F32, "out_proj_bwd", token)
    dz, dsinks, dlg, dlb, dsgu_w, dsgu_b = _mixer_bwd(z, dcat, o, attn_sinks, sgu_ln_g, sgu_ln_b, sgu_w[0], sgu_bt)
    po, slots_o = _split_wait(_owner_copies, sems, bufs, dz, "w_out_grad_to_owners_wait")
    go = _halves_swap(_sum_slots(slots_o, po, place, OUT_SHARD // 2, D_MODEL, "w_out_grad_sum"), "w_out_grad_halves_swap")

    pw = _w_in_grad_chip_partial(dz, h)
    first = functools.partial(_chip_copies, [(0, 0, IN_SHARD), (1, 0, IN_SHARD), (2, 0, IN_SHARD // 4)])
    rest = functools.partial(_chip_copies, [(2, IN_SHARD // 4, 3 * IN_SHARD // 4)])
    sems, bufs, token = _split_start(first, [pw, lax.empty((N_CHIPS, IN_SHARD, HALF), BF16)], 3, "w_in_grad_to_chips_start", dz)
    gx, dshift, dscale, dng = _in_proj_bwd(dz, win_t, xs, dx1, norm_g + token[0:1, 0:1], mod)
    bufs = _split_wait(first, sems, bufs, gx, "w_in_grad_to_chips_wait")

    parts = dict(dng=dng, dshift=dshift, dscale=dscale, dgate=dgate, dfg=dfg, dlg=dlg, dlb=dlb, dsinks=dsinks,
                 dsgu_b=dsgu_b, dsgu_w=dsgu_w, loss=loss)
    small = dict(norm_g=(norm_g, m_norm_g, v_norm_g), b_ada=(b_ada, m_b_ada, v_b_ada),
                 attn_sinks=(attn_sinks, m_attn_sinks, v_attn_sinks), sgu_ln_g=(sgu_ln_g, m_sgu_ln_g, v_sgu_ln_g),
                 sgu_ln_b=(sgu_ln_b, m_sgu_ln_b, v_sgu_ln_b), sgu_w=(sgu_w, m_sgu_w, v_sgu_w), sgu_b=(sgu_b, m_sgu_b, v_sgu_b),
                 final_g=(final_g, m_final_g, v_final_g))
    tot, gsgu_w, dmod_mine = _small_grads_sum(parts, bufs[1])
    sems, bufs, token = _split_start(rest, list(bufs), 1, "w_in_grad_rest_to_chips_start", tot)
    res, total = _small_adamw(tot, gsgu_w, small)
    res["w_ada"] = _adamw_ada(cact, dmod_mine + token[0:1, 0:1], w_ada[0], m_w_ada[0], v_w_ada[0])
    res["w_out"] = _adamw_rows(w_out[0], go, m_w_out[0], v_w_out[0], 256, "adamw_w_out", res["w_ada"][1])

    pw, slots_w = _split_wait(rest, sems, bufs, res["w_out"][1], "w_in_grad_rest_to_chips_wait")
    gw_t = _halves_swap(_sum_slots(slots_w, pw, place, IN_SHARD // 4, D_MODEL, "w_in_grad_sum"), "w_in_grad_halves_swap")
    res["w_in"] = tuple(jnp.swapaxes(a, 0, 1) for a in _adamw_rows(w_in_t, gw_t, m_in_t, v_in_t, IN_SHARD // 4, "adamw_w_in", gw_t))

    order = ["norm_g", "w_ada", "b_ada", "w_in", "attn_sinks", "sgu_ln_g", "sgu_ln_b", "sgu_w", "sgu_b", "w_out", "final_g"]
    like = dict(norm_g=norm_g, w_ada=w_ada, b_ada=b_ada, w_in=w_in, attn_sinks=attn_sinks, sgu_ln_g=sgu_ln_g, sgu_ln_b=sgu_ln_b,
                sgu_w=sgu_w, sgu_b=sgu_b, w_out=w_out, final_g=final_g)
    outs = [total[0, 0], gx[None]]
    for kind in range(4):
        outs += [res[n][kind].reshape(like[n].shape) for n in order]
    return tuple(outs)
```

```python
import functools
import math

import jax
import jax.numpy as jnp
from jax import lax
from jax.experimental import pallas as pl
from jax.experimental.pallas import tpu as pltpu

F32 = jnp.float32
BF16 = jnp.bfloat16

D_MODEL = 2048
D_ATTN = 1024
HEAD_DIM = 64
N_Q_HEADS = 16
Q_PER_KV = 8
D_KV = 128
BLOCK = 128
D_SGU = 1024
SGU_GROUPS = 8
GROUP_DIM = 128
D_IN = 5376
EPS = 1e-6
ATTN_SCALE = 1.0 / math.sqrt(HEAD_DIM)
N_CHIPS = 4
N_DEV = 8
IN_SHARD = D_IN // N_CHIPS
OUT_SHARD = D_MODEL // N_CHIPS
ADA_SHARD = 3 * D_MODEL // N_CHIPS
HALF = D_MODEL // 2
IN_TILE = 768

Q0, K0, V0, GA0, U0, VS0, GS0 = 0, 1024, 1152, 1280, 2304, 3328, 4352

ADAM_LR = 0.001
ADAM_B1 = 0.9
ADAM_B2 = 0.999
ADAM_EPS = 1e-08
ADAM_WD = 0.01
ADAM_STEP = 10

VMEM_LIMIT_V7X = 56 * 1024 * 1024

NN = (((1,), (0,)), ((), ()))
NT = (((1,), (1,)), ((), ()))
TN = (((0,), (0,)), ((), ()))
MESH = pl.DeviceIdType.MESH
ANY = pl.BlockSpec(memory_space=pl.ANY)
VMEM = pl.BlockSpec(memory_space=pltpu.VMEM)
HBM = pl.BlockSpec(memory_space=pltpu.HBM)
SEM = pl.BlockSpec(memory_space=pltpu.SEMAPHORE)
EFFECT = pltpu.SideEffectType.DATAFLOW_SIDE_EFFECTING


def _dot(a, b, dims):
    return lax.dot_general(a, b, dims, preferred_element_type=F32)


def _params(sem=None):
    return pltpu.CompilerParams(dimension_semantics=sem, vmem_limit_bytes=VMEM_LIMIT_V7X)


def _sigmoid(x):
    return jax.nn.sigmoid(x)


def _adamw(w, g, m, v):
    m = ADAM_B1 * m + (1.0 - ADAM_B1) * g
    v = ADAM_B2 * v + (1.0 - ADAM_B2) * (g * g)
    m_hat = m / (1.0 - ADAM_B1 ** ADAM_STEP)
    v_hat = v / (1.0 - ADAM_B2 ** ADAM_STEP)
    delta = -ADAM_LR * (m_hat / (jnp.sqrt(v_hat) + ADAM_EPS) + ADAM_WD * w)
    return delta, m, v


def _place():
    return lax.axis_index("x"), lax.axis_index("y"), lax.axis_index("c")


def _flip(p, bit):
    return 1 - p if bit else p


def _other_chips(x, y):
    return [(_flip(x, bx), _flip(y, by)) for bx, by in ((0, 1), (1, 0), (1, 1))]


def _split_start(copies, bufs, n_sems, name, after):
    n = len(bufs)

    def body(*refs):
        send_sems, recv_sems = refs[n + 1:n + 3]
        for cp in copies(*refs[:n], send_sems, recv_sems):
            cp.start()
        refs[-1][...] = jnp.zeros_like(refs[-1])

    bufs = [pltpu.with_memory_space_constraint(a, pltpu.HBM) for a in bufs]
    outs = pl.pallas_call(
        body, name=name,
        out_shape=(pltpu.SemaphoreType.DMA((n_sems,)), pltpu.SemaphoreType.DMA((n_sems,)))
        + tuple(pltpu.HBM(a.shape, a.dtype) for a in bufs) + (jax.ShapeDtypeStruct((8, 128), F32),),
        in_specs=[HBM] * n + [ANY], out_specs=(SEM, SEM) + (HBM,) * n + (VMEM,), input_output_aliases={k: k + 2 for k in range(n)},
        compiler_params=pltpu.CompilerParams(has_side_effects=EFFECT),
    )(*bufs, after)
    return outs[:2], list(outs[2:2 + n]), outs[-1]


def _split_wait(copies, sems, bufs, after, name):
    n = len(bufs)

    def body(*refs):
        send_sems, recv_sems = refs[n:n + 2]
        for cp in copies(*refs[:n], send_sems, recv_sems):
            cp.wait_send()
            cp.wait_recv()

    return pl.pallas_call(
        body, name=name, out_shape=tuple(pltpu.HBM(a.shape, a.dtype) for a in bufs),
        in_specs=[HBM] * n + [SEM, SEM, ANY], out_specs=(HBM,) * n, input_output_aliases={k: k for k in range(n)},
        compiler_params=pltpu.CompilerParams(has_side_effects=EFFECT),
    )(*bufs, *sems, after)


def _gather_copies(full_ref, send_sems, recv_sems):
    x, y, cc = _place()
    shard = full_ref.shape[0] // N_CHIPS
    half = shard // 2
    mine = full_ref.at[pl.ds(shard * (2 * x + y) + half * cc, half), :]
    return [pltpu.make_async_remote_copy(src_ref=mine, dst_ref=mine, send_sem=send_sems.at[j], recv_sem=recv_sems.at[j],
                                         device_id=(px, py, cc), device_id_type=MESH)
            for j, (px, py) in enumerate(_other_chips(x, y))]


def _gather_finish(full, name):
    shard = full.shape[0] // N_CHIPS
    half = shard // 2

    def body(full_in, full_ref, send_sems, recv_sems):
        x, y, cc = _place()
        cps = []
        for j, (px, py) in enumerate(_other_chips(x, y)):
            landed = full_ref.at[pl.ds(shard * (2 * px + py) + half * cc, half), :]
            cps.append(pltpu.make_async_remote_copy(src_ref=landed, dst_ref=landed, send_sem=send_sems.at[j], recv_sem=recv_sems.at[j],
                                                    device_id=(x, y, 1 - cc), device_id_type=MESH))
        for cp in cps:
            cp.start()
        for cp in cps:
            cp.wait()

    return pl.pallas_call(
        body, name=name, out_shape=jax.ShapeDtypeStruct(full.shape, full.dtype),
        in_specs=[ANY], out_specs=ANY, input_output_aliases={0: 0},
        scratch_shapes=[pltpu.SemaphoreType.DMA((3,)), pltpu.SemaphoreType.DMA((3,))],
        compiler_params=_params(),
    )(full)


def _gather_w_in_modulate_project(full, c, w_ada, b_ada, xs, norm_g):
    shard = full.shape[0] // N_CHIPS
    half = shard // 2
    quarter = half // 2
    kdim = w_ada.shape[0]
    kc = 256
    s, d = xs.shape
    rows_at_once = min(256, s)
    n_chunks = s // rows_at_once
    m_rows = min(512, s)
    n_m = s // m_rows

    def body(full_in, c_ref, w_hbm, b_ref, x_hbm, g_ref, full_ref, mod_ref, cact_ref, h_hbm, z_hbm,
             w_vmem, cbuf, pbuf, rbuf, h_vmem, xbuf, bbuf, zbuf, w_sem, x_sems, b_sem, z_sems, h_sem,
             send_sems, recv_sems, msend_sems, mrecv_sems):
        x, y, cc = _place()
        me = 4 * x + 2 * y + cc
        chip = 2 * x + y
        across_x, across_y, far = (1 - x, y), (x, 1 - y), (1 - x, 1 - y)

        def rows(chip_xy, core, q):
            return full_ref.at[pl.ds(shard * (2 * chip_xy[0] + chip_xy[1]) + half * core + quarter * q, quarter), :]

        def copy(k, piece, to):
            return pltpu.make_async_remote_copy(src_ref=piece, dst_ref=piece, send_sem=send_sems.at[k], recv_sem=recv_sems.at[k],
                                                device_id=to, device_id_type=MESH)

        def small(k, src, dst, to):
            return pltpu.make_async_remote_copy(src_ref=src, dst_ref=dst, send_sem=msend_sems.at[k], recv_sem=mrecv_sems.at[k],
                                                device_id=to, device_id_type=MESH)

        def x_chunk(i):
            return pltpu.make_async_copy(x_hbm.at[pl.ds(rows_at_once * i, rows_at_once), :], xbuf.at[i % 2], x_sems.at[i % 2])

        direct = [copy(0, rows((x, y), cc, 0), (*across_x, cc)), copy(2, rows((x, y), cc, 1), (*across_y, cc)),
                  copy(1, rows((x, y), cc, 1), (*across_x, cc)), copy(3, rows((x, y), cc, 0), (*across_y, cc))]
        landing = [(0, across_x, 0, (4, (*across_y, cc))), (2, across_y, 1, (5, (*across_x, cc))),
                   (1, across_x, 1, None), (3, across_y, 0, None), (4, far, 0, None), (5, far, 1, None)]
        passed = []

        def land(n):
            k, whose, q, onward = landing[n]
            piece = rows(whose, cc, q)
            copy(k, piece, (x, y, cc)).wait_recv()
            if onward is not None:
                passed.append(copy(onward[0], piece, onward[1]))
                passed[-1].start()
            passed.append(copy(6 + n, piece, (x, y, 1 - cc)))
            passed[-1].start()

        def from_sibling(n):
            k, whose, q, onward = landing[n]
            copy(6 + n, rows(whose, 1 - cc, q), (x, y, cc)).wait_recv()

        cbuf[me] = c_ref[...]
        gathers = [small(k - 1, cbuf.at[me], cbuf.at[me], (_flip(x, (k >> 2) & 1), _flip(y, (k >> 1) & 1), _flip(cc, k & 1)))
                   for k in range(1, N_DEV)]
        for cp in gathers:
            cp.start()
        direct[0].start()
        direct[1].start()
        w_load = pltpu.make_async_copy(w_hbm, w_vmem, w_sem)
        w_load.start()
        x_chunk(0).start()
        for cp in gathers:
            cp.wait()
        w_load.wait()
        for b in range(N_DEV):
            cv = cbuf[b]
            cact_ref[b:b + 1, :] = cv * _sigmoid(cv)
        acc = jnp.zeros((N_DEV, ADA_SHARD), F32)
        for k0 in range(0, kdim, kc):
            acc = acc + _dot(cact_ref[:, k0:k0 + kc].astype(BF16), w_vmem[k0:k0 + kc, :].astype(BF16), NN)
        for b in range(N_DEV):
            pbuf[b] = acc[b:b + 1, :]
        rbuf[chip] = pbuf[me]
        sends = [small(N_DEV - 1 + j, pbuf.at[4 * px + 2 * py + cc], rbuf.at[chip], (px, py, cc))
                 for j, (px, py) in enumerate(_other_chips(x, y))]
        for cp in sends:
            cp.start()
        direct[2].start()
        direct[3].start()
        land(0)
        land(1)
        for cp in sends:
            cp.wait()
        for k in range(N_CHIPS):
            cols = slice(ADA_SHARD * k, ADA_SHARD * (k + 1))
            mod_ref[:, cols] = rbuf[k] + b_ref[:, cols]

        for i in range(n_chunks):
            x_chunk(i).wait()
            if i + 1 < n_chunks:
                x_chunk(i + 1).start()
            xv = xbuf[i % 2]
            r = lax.rsqrt(jnp.mean(xv * xv, axis=-1, keepdims=True) + EPS)
            h_vmem[rows_at_once * i:rows_at_once * (i + 1), :] = (
                (xv * r * g_ref[...]) * (1.0 + mod_ref[:, d:2 * d]) + mod_ref[:, 0:d]).astype(BF16)
        h_out = pltpu.make_async_copy(h_vmem, h_hbm, h_sem)
        h_out.start()

        def project(tile):
            first = pl.multiple_of(IN_TILE * tile, 128)
            b_load = pltpu.make_async_copy(full_ref.at[pl.ds(first, IN_TILE), :], bbuf, b_sem)
            b_load.start()
            b_load.wait()
            outs = []
            for m in range(n_m):
                if m >= 2:
                    outs[m - 2].wait()
                zbuf[m % 2] = _dot(h_vmem[m_rows * m:m_rows * (m + 1), :], bbuf[...], NT)
                outs.append(pltpu.make_async_copy(zbuf.at[m % 2], z_hbm.at[pl.ds(m_rows * m, m_rows), pl.ds(first, IN_TILE)],
                                                  z_sems.at[m % 2]))
                outs[m].start()
            for cp in outs[max(n_m - 2, 0):]:
                cp.wait()

        project(2 * chip)
        land(2)
        land(3)
        for n in range(4):
            from_sibling(n)
        project(2 * (2 * across_x[0] + across_x[1]))
        project(2 * (2 * across_y[0] + across_y[1]))
        land(4)
        land(5)
        from_sibling(4)
        from_sibling(5)
        project(2 * (2 * far[0] + far[1]))
        h_out.wait()
        for cp in direct + passed:
            cp.wait_send()

    return pl.pallas_call(
        body, name="w_in_gather_modulate_project",
        out_shape=(jax.ShapeDtypeStruct(full.shape, full.dtype), jax.ShapeDtypeStruct((1, 3 * D_MODEL), F32),
                   jax.ShapeDtypeStruct((N_DEV, D_MODEL), F32), jax.ShapeDtypeStruct((s, d), BF16),
                   jax.ShapeDtypeStruct((s, full.shape[0]), F32)),
        in_specs=[ANY, VMEM, ANY, VMEM, ANY, VMEM], out_specs=(ANY, VMEM, VMEM, ANY, ANY), input_output_aliases={0: 0},
        scratch_shapes=[pltpu.VMEM(w_ada.shape, F32), pltpu.VMEM((N_DEV, 1, D_MODEL), F32), pltpu.VMEM((N_DEV, 1, ADA_SHARD), F32),
                        pltpu.VMEM((N_CHIPS, 1, ADA_SHARD), F32), pltpu.VMEM((s, d), BF16), pltpu.VMEM((2, rows_at_once, d), F32),
                        pltpu.VMEM((IN_TILE, d), BF16), pltpu.VMEM((2, m_rows, IN_TILE), F32),
                        pltpu.SemaphoreType.DMA, pltpu.SemaphoreType.DMA((2,)), pltpu.SemaphoreType.DMA, pltpu.SemaphoreType.DMA((2,)),
                        pltpu.SemaphoreType.DMA,
                        pltpu.SemaphoreType.DMA((12,)), pltpu.SemaphoreType.DMA((12,)),
                        pltpu.SemaphoreType.DMA((N_DEV + 2,)), pltpu.SemaphoreType.DMA((N_DEV + 2,))],
        compiler_params=_params(),
    )(full, c, w_ada, b_ada, xs, norm_g)


def _w_in_grad_chip_partial(dz, h):
    t_rows, m = dz.shape
    n_tiles = m // IN_TILE

    def body(dz_hbm, h_hbm, p_hbm, h_theirs, h_mine, abuf, sendbuf, recvbuf, obuf, h_sems, a_sems, o_sems, send_sems, recv_sems):
        x, y, cc = _place()

        def a_tile(t, n):
            return pltpu.make_async_copy(dz_hbm.at[:, pl.ds(IN_TILE * t, IN_TILE)], abuf.at[n % 2], a_sems.at[n % 2])

        def half_cols(core):
            return h_hbm.at[:, pl.ds(pl.multiple_of(HALF * core, 128), HALF)]

        loads = [pltpu.make_async_copy(half_cols(1 - cc), h_theirs, h_sems.at[0]),
                 pltpu.make_async_copy(half_cols(cc), h_mine, h_sems.at[1])]
        for cp in loads:
            cp.start()
        a_tile(0, 0).start()
        loads[0].wait()
        sent = []
        for t in range(n_tiles):
            a_tile(t, t).wait()
            a_tile((t + 1) % n_tiles, t + 1).start()
            sendbuf[t] = _dot(abuf[t % 2], h_theirs[...], TN).astype(BF16)
            sent.append(pltpu.make_async_remote_copy(src_ref=sendbuf.at[t], dst_ref=recvbuf.at[t], send_sem=send_sems.at[t],
                                                     recv_sem=recv_sems.at[t], device_id=(x, y, 1 - cc), device_id_type=MESH))
            sent[t].start()
        loads[1].wait()
        outs = []
        for t in range(n_tiles):
            n = n_tiles + t
            a_tile(t, n).wait()
            if t + 1 < n_tiles:
                a_tile(t + 1, n + 1).start()
            sent[t].wait_recv()
            if t >= 2:
                outs[t - 2].wait()
            obuf[t % 2] = (_dot(abuf[n % 2], h_mine[...], TN) + recvbuf[t].astype(F32)).astype(BF16)
            outs.append(pltpu.make_async_copy(obuf.at[t % 2], p_hbm.at[pl.ds(IN_TILE * t, IN_TILE), :], o_sems.at[t % 2]))
            outs[t].start()
        for cp in outs[max(n_tiles - 2, 0):]:
            cp.wait()
        for cp in sent:
            cp.wait_send()

    return pl.pallas_call(
        body, name="w_in_grad_chip_partial", out_shape=jax.ShapeDtypeStruct((m, HALF), BF16),
        in_specs=[ANY, ANY], out_specs=ANY,
        scratch_shapes=[pltpu.VMEM((t_rows, HALF), BF16), pltpu.VMEM((t_rows, HALF), BF16), pltpu.VMEM((2, t_rows, IN_TILE), BF16),
                        pltpu.VMEM((n_tiles, IN_TILE, HALF), BF16), pltpu.VMEM((n_tiles, IN_TILE, HALF), BF16),
                        pltpu.VMEM((2, IN_TILE, HALF), BF16),
                        pltpu.SemaphoreType.DMA((2,)), pltpu.SemaphoreType.DMA((2,)), pltpu.SemaphoreType.DMA((2,)),
                        pltpu.SemaphoreType.DMA((n_tiles,)), pltpu.SemaphoreType.DMA((n_tiles,))],
        compiler_params=_params(),
    )(dz, h)


def _chip_copies(pieces, p_ref, land_ref, send_sems, recv_sems):
    x, y, cc = _place()
    chip = 2 * x + y
    shard = land_ref.shape[1]
    others = _other_chips(x, y)
    cps = []
    for k, (j, r0, nr) in enumerate(pieces):
        px, py = others[j]
        cps.append(pltpu.make_async_remote_copy(
            src_ref=p_ref.at[pl.ds(shard * (2 * px + py) + r0, nr), :], dst_ref=land_ref.at[chip, pl.ds(r0, nr), :],
            send_sem=send_sems.at[k], recv_sem=recv_sems.at[k], device_id=(px, py, cc), device_id_type=MESH))
    return cps


def _owner_copies(p_ref, land_ref, send_sems, recv_sems):
    x, y, cc = _place()
    me = 4 * x + 2 * y + cc
    shard = land_ref.shape[1]
    cps = []
    for k in range(1, N_DEV):
        px, py, pc = _flip(x, (k >> 2) & 1), _flip(y, (k >> 1) & 1), _flip(cc, k & 1)
        block = p_ref.at[pl.ds(shard * (2 * px + py), shard), pl.ds(pl.multiple_of(HALF * pc, 128), HALF)]
        cps.append(pltpu.make_async_remote_copy(src_ref=block, dst_ref=land_ref.at[me], send_sem=send_sems.at[k - 1],
                                                recv_sem=recv_sems.at[k - 1], device_id=(px, py, pc), device_id_type=MESH))
    return cps


def _reduce_and_swap(slots, own, rows_tile, name):
    n_slots, r, hcols = slots.shape
    n_tiles = r // rows_tile
    per_device = n_slots == N_DEV

    def body(slots_hbm, own_hbm, f_hbm, sbuf, obuf, fbuf, in_sems, out_sems, send_sems, recv_sems):
        x, y, cc = _place()
        chip = 2 * x + y
        mine = 2 * chip + cc if per_device else chip
        my_cols = pl.ds(pl.multiple_of(hcols * cc, 128), hcols)

        def loads(t):
            rows = pl.ds(rows_tile * t, rows_tile)
            own_rows = pl.ds(r * chip + rows_tile * t, rows_tile)
            own_src = own_hbm.at[own_rows, my_cols] if per_device else own_hbm.at[own_rows, :]
            return ([pltpu.make_async_copy(slots_hbm.at[k, rows, :], sbuf.at[t % 2, k], in_sems.at[t % 2, k]) for k in range(n_slots)]
                    + [pltpu.make_async_copy(own_src, obuf.at[t % 2], in_sems.at[t % 2, n_slots])])

        def mine_out(t):
            return f_hbm.at[pl.ds(rows_tile * t, rows_tile), my_cols]

        for cp in loads(0):
            cp.start()
        local, remote = [], []
        for t in range(n_tiles):
            for cp in loads(t):
                cp.wait()
            if t + 1 < n_tiles:
                for cp in loads(t + 1):
                    cp.start()
            if t >= 2:
                local[t - 2].wait()
                remote[t - 2].wait_send()
            acc = jnp.zeros((rows_tile, hcols), F32)
            for k in range(n_slots):
                acc = acc + jnp.where(mine == k, obuf[t % 2], sbuf[t % 2, k]).astype(F32)
            fbuf[t % 2] = acc
            local.append(pltpu.make_async_copy(fbuf.at[t % 2], mine_out(t), out_sems.at[t % 2]))
            remote.append(pltpu.make_async_remote_copy(src_ref=fbuf.at[t % 2], dst_ref=mine_out(t), send_sem=send_sems.at[t],
                                                       recv_sem=recv_sems.at[t], device_id=(x, y, 1 - cc), device_id_type=MESH))
            local[t].start()
            remote[t].start()
        for t in range(max(n_tiles - 2, 0), n_tiles):
            local[t].wait()
            remote[t].wait_send()
        for t in range(n_tiles):
            remote[t].wait_recv()

    return pl.pallas_call(
        body, name=name, out_shape=jax.ShapeDtypeStruct((r, 2 * hcols), F32), in_specs=[ANY, ANY], out_specs=ANY,
        scratch_shapes=[pltpu.VMEM((2, n_slots, rows_tile, hcols), BF16), pltpu.VMEM((2, rows_tile, hcols), BF16),
                        pltpu.VMEM((2, rows_tile, hcols), F32), pltpu.SemaphoreType.DMA((2, n_slots + 1)),
                        pltpu.SemaphoreType.DMA((2,)), pltpu.SemaphoreType.DMA((n_tiles,)), pltpu.SemaphoreType.DMA((n_tiles,))],
        compiler_params=_params(),
    )(slots, own)


ROW_NORM_G, ROW_DMOD, ROW_FINAL_G, ROW_LN, ROW_SINKS, ROW_LOSS, ROW_SGU_B, MISC_ROWS = 0, 1, 4, 5, 6, 7, 8, 16


def _small_grads_sum(parts, after):
    part_names = ["dng", "dshift", "dscale", "dgate", "dfg", "dlg", "dlb", "dsinks", "dsgu_b", "dsgu_w", "loss"]
    n_parts = len(part_names)
    n_peers = N_DEV - 1

    def body(*refs):
        p = dict(zip(part_names, refs[:n_parts]))
        tot_ref, gw_ref, dmod_ref = refs[n_parts + 1:n_parts + 4]
        misc, cmisc, dsend, dbuf, wland, wsum, own_sem, send_sems, recv_sems = refs[n_parts + 4:]
        x, y, cc = _place()
        me = 4 * x + 2 * y + cc
        chip = 2 * x + y
        peers = [(_flip(x, (k >> 2) & 1), _flip(y, (k >> 1) & 1), _flip(cc, k & 1)) for k in range(1, N_DEV)]

        def remote(k, src, dst, peer):
            return pltpu.make_async_remote_copy(src_ref=src, dst_ref=dst, send_sem=send_sems.at[k], recv_sem=recv_sems.at[k],
                                                device_id=peer, device_id_type=MESH)

        dsend[:, 0:D_MODEL] = p["dshift"][...]
        dsend[:, D_MODEL:2 * D_MODEL] = p["dscale"][...]
        dsend[:, 2 * D_MODEL:3 * D_MODEL] = p["dgate"][...]

        def shard_cols(chip_index):
            return dsend.at[:, pl.ds(pl.multiple_of(ADA_SHARD * chip_index, 128), ADA_SHARD)]

        own_cols = pltpu.make_async_copy(shard_cols(chip), dbuf.at[me], own_sem)
        own_cols.start()
        gathers = [remote(k, shard_cols(2 * px + py), dbuf.at[me], (px, py, pc)) for k, (px, py, pc) in enumerate(peers)]
        wland[me] = p["dsgu_w"][me]
        scatters = [remote(n_peers + k, p["dsgu_w"].at[4 * px + 2 * py + pc], wland.at[me], (px, py, pc))
                    for k, (px, py, pc) in enumerate(peers)]

        misc[cc] = jnp.zeros((MISC_ROWS, D_MODEL), F32)
        misc[cc, ROW_NORM_G:ROW_NORM_G + 1, :] = p["dng"][...]
        misc[cc, ROW_DMOD:ROW_DMOD + 1, :] = p["dshift"][...]
        misc[cc, ROW_DMOD + 1:ROW_DMOD + 2, :] = p["dscale"][...]
        misc[cc, ROW_DMOD + 2:ROW_DMOD + 3, :] = p["dgate"][...]
        misc[cc, ROW_FINAL_G:ROW_FINAL_G + 1, :] = p["dfg"][...]
        misc[cc, ROW_LN:ROW_LN + 1, 0:D_SGU] = p["dlg"][...]
        misc[cc, ROW_LN:ROW_LN + 1, D_SGU:2 * D_SGU] = p["dlb"][...]
        misc[cc, ROW_SINKS:ROW_SINKS + 1, 0:N_Q_HEADS] = p["dsinks"][...]
        misc[cc, ROW_LOSS:ROW_LOSS + 1, 0:1] = p["loss"][...]
        misc[cc, ROW_SGU_B:ROW_SGU_B + SGU_GROUPS, 0:BLOCK] = p["dsgu_b"][...]
        pair = remote(3 * n_peers, misc.at[cc], misc.at[cc], (x, y, 1 - cc))
        for cp in gathers + scatters + [pair]:
            cp.start()
        pair.wait()
        cmisc[chip] = misc[0] + misc[1]
        quad = [remote(3 * n_peers + 1 + j, cmisc.at[chip], cmisc.at[chip], (_flip(x, bx), _flip(y, by), cc))
                for j, (bx, by) in enumerate(((0, 1), (1, 0), (1, 1)))]
        for cp in quad:
            cp.start()
        for cp in scatters:
            cp.wait()
        group = wland[0]
        for b in range(1, N_DEV):
            group = group + wland[b]
        wsum[me] = group
        spreads = [remote(2 * n_peers + k, wsum.at[me], wsum.at[me], peer) for k, peer in enumerate(peers)]
        for cp in spreads:
            cp.start()
        for cp in gathers + quad + spreads:
            cp.wait()

        tot = cmisc[0]
        for k in range(1, N_CHIPS):
            tot = tot + cmisc[k]
        tot_ref[...] = tot
        gw_ref[...] = wsum[...]
        own_cols.wait()
        for b in range(N_DEV):
            dmod_ref[b:b + 1, :] = dbuf[b]

    operands = [parts[n] for n in part_names]
    n_sems = 3 * n_peers + 1 + (N_CHIPS - 1)
    return pl.pallas_call(
        body, name="small_grads_sum",
        out_shape=(jax.ShapeDtypeStruct((MISC_ROWS, D_MODEL), F32), jax.ShapeDtypeStruct((SGU_GROUPS, BLOCK, BLOCK), F32),
                   jax.ShapeDtypeStruct((N_DEV, ADA_SHARD), F32)),
        in_specs=[VMEM] * len(operands) + [ANY], out_specs=(VMEM, VMEM, VMEM),
        scratch_shapes=[pltpu.VMEM((2, MISC_ROWS, D_MODEL), F32), pltpu.VMEM((N_CHIPS, MISC_ROWS, D_MODEL), F32),
                        pltpu.VMEM((1, 3 * D_MODEL), F32), pltpu.VMEM((N_DEV, 1, ADA_SHARD), F32),
                        pltpu.VMEM((N_DEV, BLOCK, BLOCK), F32), pltpu.VMEM((SGU_GROUPS, BLOCK, BLOCK), F32),
                        pltpu.SemaphoreType.DMA, pltpu.SemaphoreType.DMA((n_sems,)), pltpu.SemaphoreType.DMA((n_sems,))],
        compiler_params=_params(),
    )(*operands, after)


def _small_adamw(tot, gw, small):
    names = ["norm_g", "b_ada", "attn_sinks", "sgu_ln_g", "sgu_ln_b", "sgu_w", "sgu_b", "final_g"]

    def body(*refs):
        tot_ref, gw_ref = refs[:2]
        wmv = {n: refs[2 + 3 * i: 5 + 3 * i] for i, n in enumerate(names)}
        outs = refs[2 + 3 * len(names):]
        res = {n: outs[4 * i: 4 * i + 4] for i, n in enumerate(names)}
        outs[4 * len(names)][...] = tot_ref[ROW_LOSS:ROW_LOSS + 1, 0:1]

        def update(name, g, pick=lambda r: r[...], put=None):
            w_ref, m_ref, v_ref = wmv[name]
            delta, m2, v2 = _adamw(pick(w_ref), g, pick(m_ref), pick(v_ref))
            for ref, val in zip(res[name], (g, delta, m2, v2)):
                if put is None:
                    ref[...] = val
                else:
                    put(ref, val)

        update("norm_g", tot_ref[ROW_NORM_G:ROW_NORM_G + 1, :])
        update("final_g", tot_ref[ROW_FINAL_G:ROW_FINAL_G + 1, :])
        for t in range(3):
            cols = slice(D_MODEL * t, D_MODEL * (t + 1))

            def put_cols(ref, val, cols=cols):
                ref[:, cols] = val
            update("b_ada", tot_ref[ROW_DMOD + t:ROW_DMOD + t + 1, :], pick=lambda r, cols=cols: r[:, cols], put=put_cols)
        update("sgu_ln_g", tot_ref[ROW_LN:ROW_LN + 1, 0:D_SGU])
        update("sgu_ln_b", tot_ref[ROW_LN:ROW_LN + 1, D_SGU:2 * D_SGU])
        update("attn_sinks", tot_ref[ROW_SINKS:ROW_SINKS + 1, 0:N_Q_HEADS])
        update("sgu_b", tot_ref[ROW_SGU_B:ROW_SGU_B + SGU_GROUPS, 0:BLOCK])
        for g in range(SGU_GROUPS):
            def put_group(ref, val, g=g):
                ref[g] = val
            update("sgu_w", gw_ref[g], pick=lambda r, g=g: r[g], put=put_group)

    shapes = {"norm_g": (1, D_MODEL), "b_ada": (1, 3 * D_MODEL), "attn_sinks": (1, N_Q_HEADS), "sgu_ln_g": (1, D_SGU),
              "sgu_ln_b": (1, D_SGU), "sgu_w": (SGU_GROUPS, BLOCK, BLOCK), "sgu_b": (SGU_GROUPS, BLOCK), "final_g": (1, D_MODEL)}
    operands = [tot, gw]
    for n in names:
        operands += [a.reshape(shapes[n]) for a in small[n]]
    out_shape = []
    for n in names:
        out_shape += [jax.ShapeDtypeStruct(shapes[n], F32)] * 4
    out_shape.append(jax.ShapeDtypeStruct((1, 1), F32))
    outs = pl.pallas_call(
        body, name="small_adamw", out_shape=tuple(out_shape),
        in_specs=[VMEM] * len(operands), out_specs=tuple([VMEM] * len(out_shape)), compiler_params=_params(),
    )(*operands)
    return {n: outs[4 * i: 4 * i + 4] for i, n in enumerate(names)}, outs[4 * len(names)]


def _cast_into_gathered(w, tile, place, name, after):
    r, c = w.shape
    nblk = r // tile

    def body(place_ref, w_ref, after_ref, o_ref):
        o_ref[...] = w_ref[...].astype(BF16)

    return pl.pallas_call(
        body, name=name,
        grid_spec=pltpu.PrefetchScalarGridSpec(
            num_scalar_prefetch=1, grid=(nblk,), in_specs=[pl.BlockSpec((tile, c), lambda i, p: (i, 0)), ANY],
            out_specs=pl.BlockSpec((tile, c), lambda i, p: (p[1] * nblk + i, 0))),
        out_shape=jax.ShapeDtypeStruct((N_CHIPS * r, c), BF16),
        compiler_params=_params(("parallel",)),
    )(place, w, after)


def _in_proj_odd(h, win_t, z, after):
    s, d = h.shape
    tm = min(1024, s)
    n_odd = win_t.shape[0] // IN_TILE // 2

    def body(h_ref, w_ref, z_in, after_ref, z_ref):
        z_ref[...] = _dot(h_ref[...], w_ref[...], NT)

    return pl.pallas_call(
        body, name="in_proj_odd", grid=(s // tm, n_odd), out_shape=jax.ShapeDtypeStruct(z.shape, z.dtype),
        in_specs=[pl.BlockSpec((tm, d), lambda i, j: (i, 0)), pl.BlockSpec((IN_TILE, d), lambda i, j: (2 * j + 1, 0)), ANY, ANY],
        out_specs=pl.BlockSpec((tm, IN_TILE), lambda i, j: (i, 2 * j + 1)), input_output_aliases={2: 0},
        compiler_params=_params(("parallel", "parallel")),
    )(h, win_t, z, after)


def _matmul_nt(a, bt, tm, out_dtype, name, after):
    m, k = a.shape
    n = bt.shape[0]
    tm = min(tm, m)

    def body(a_ref, b_ref, after_ref, o_ref):
        o_ref[...] = _dot(a_ref[...], b_ref[...], NT).astype(out_dtype)

    return pl.pallas_call(
        body, name=name, grid=(m // tm,), out_shape=jax.ShapeDtypeStruct((m, n), out_dtype),
        in_specs=[pl.BlockSpec((tm, k), lambda i: (i, 0)), pl.BlockSpec((n, k), lambda i: (0, 0), pipeline_mode=pl.Buffered(1)), ANY],
        out_specs=pl.BlockSpec((tm, n), lambda i: (i, 0)),
        compiler_params=_params(("parallel",)),
    )(a, bt, after)


def _matmul_tn(a, b, tm, name):
    t, m = a.shape
    n = b.shape[1]
    tm = min(tm, m)

    def body(a_ref, b_ref, o_ref):
        o_ref[...] = _dot(a_ref[...], b_ref[...], TN).astype(BF16)

    return pl.pallas_call(
        body, name=name, grid=(m // tm,), out_shape=jax.ShapeDtypeStruct((m, n), BF16),
        in_specs=[pl.BlockSpec((t, tm), lambda i: (0, i)), pl.BlockSpec((t, n), lambda i: (0, 0))],
        out_specs=pl.BlockSpec((tm, n), lambda i: (i, 0)),
        compiler_params=_params(("parallel",)),
    )(a, b)


def _mixer_masks(n):
    key = lax.broadcasted_iota(jnp.int32, (2 * BLOCK, Q_PER_KV * BLOCK), 0)
    qry = lax.broadcasted_iota(jnp.int32, (2 * BLOCK, Q_PER_KV * BLOCK), 1) & (BLOCK - 1)
    valid = (key > qry) & (key <= qry + BLOCK) & ((key >= BLOCK) | (n > 0))
    r2 = lax.broadcasted_iota(jnp.int32, (BLOCK, BLOCK), 0)
    c2 = lax.broadcasted_iota(jnp.int32, (BLOCK, BLOCK), 1)
    return valid, r2 >= c2


def _layer_norm_parts(vs):
    mu = jnp.mean(vs, axis=-1, keepdims=True)
    xc = vs - mu
    rstd = lax.rsqrt(jnp.mean(xc * xc, axis=-1, keepdims=True) + EPS)
    return xc * rstd, rstd


def _stacked_queries(z_ref, hk, scale=None):
    q = jnp.concatenate([z_ref[:, _head_cols(Q0, Q_PER_KV * hk + g)] for g in range(Q_PER_KV)], axis=0)
    return (q if scale is None else q * scale).astype(BF16)


def _head_cols(base, h):
    return slice(base + HEAD_DIM * h, base + HEAD_DIM * (h + 1))


def _sink_row(sink_ref, hk):
    return jnp.concatenate([jnp.broadcast_to(sink_ref[:, h:h + 1], (1, BLOCK))
                            for h in range(Q_PER_KV * hk, Q_PER_KV * (hk + 1))], axis=1)


def _attn_probs(qs, kh, valid, sk, scale=None):
    s = _dot(kh, qs, NT)
    s = jnp.where(valid, s if scale is None else s * scale, -jnp.inf)
    m = jnp.maximum(jnp.max(s, axis=0, keepdims=True), sk)
    e = jnp.exp(s - m)
    es = jnp.exp(sk - m)
    inv = 1.0 / (jnp.sum(e, axis=0, keepdims=True) + es)
    return e * inv, es * inv


def _kv_ext(z_ref, kvp_ref):
    k_ext = jnp.concatenate([kvp_ref[:, 0:D_KV], z_ref[:, K0:K0 + D_KV]], axis=0).astype(BF16)
    v_ext = jnp.concatenate([kvp_ref[:, D_KV:2 * D_KV], z_ref[:, V0:V0 + D_KV]], axis=0).astype(BF16)
    return k_ext, v_ext


def _sgu_mixed(w_ref, bt_ref, vn, tril):
    ws = [jnp.where(tril, w_ref[g], 0.0).astype(BF16) for g in range(SGU_GROUPS)]
    mixed = jnp.concatenate([_dot(ws[g], vn[:, GROUP_DIM * g:GROUP_DIM * (g + 1)], NN) + bt_ref[:, g:g + 1]
                             for g in range(SGU_GROUPS)], axis=1)
    return ws, mixed


def _mixer_fwd(z, sinks, ln_g, ln_b, sgu_w, sgu_bt):
    s = z.shape[0]
    nb = s // BLOCK

    def body(z_ref, kvp_ref, sink_ref, lg_ref, lb_ref, w_ref, bt_ref, cat_ref, o_ref):
        n = pl.program_id(0)
        valid, tril = _mixer_masks(n)
        k_ext, v_ext = _kv_ext(z_ref, kvp_ref)
        for hk in range(2):
            kv = slice(HEAD_DIM * hk, HEAD_DIM * (hk + 1))
            p, _ = _attn_probs(_stacked_queries(z_ref, hk), k_ext[:, kv], valid, _sink_row(sink_ref, hk), ATTN_SCALE)
            o_t = _dot(v_ext[:, kv], p.astype(BF16), TN)
            for g in range(Q_PER_KV):
                o_ref[:, _head_cols(0, Q_PER_KV * hk + g)] = o_t[:, BLOCK * g:BLOCK * (g + 1)].T
        ga = z_ref[:, GA0:GA0 + D_ATTN]
        cat_ref[:, 0:D_ATTN] = (o_ref[...] * (ga * _sigmoid(ga))).astype(BF16)
        xhat, _ = _layer_norm_parts(z_ref[:, VS0:VS0 + D_SGU])
        vn = (xhat * lg_ref[...] + lb_ref[...]).astype(BF16)
        _, mixed = _sgu_mixed(w_ref, bt_ref, vn, tril)
        gt = z_ref[:, GS0:GS0 + D_SGU]
        cat_ref[:, D_ATTN:D_MODEL] = ((z_ref[:, U0:U0 + D_SGU] * mixed) * (gt * _sigmoid(gt))).astype(BF16)

    const2 = lambda n: (0, 0)
    return pl.pallas_call(
        body, name="mixer_fwd", grid=(nb,),
        out_shape=(jax.ShapeDtypeStruct((s, D_MODEL), BF16), jax.ShapeDtypeStruct((s, D_ATTN), F32)),
        in_specs=[pl.BlockSpec((BLOCK, D_IN), lambda n: (n, 0)),
                  pl.BlockSpec((BLOCK, 2 * D_KV), lambda n: (jnp.maximum(n - 1, 0), K0 // (2 * D_KV))),
                  pl.BlockSpec((1, N_Q_HEADS), const2), pl.BlockSpec((1, D_SGU), const2), pl.BlockSpec((1, D_SGU), const2),
                  pl.BlockSpec((SGU_GROUPS, BLOCK, BLOCK), lambda n: (0, 0, 0)), pl.BlockSpec((BLOCK, SGU_GROUPS), const2)],
        out_specs=(pl.BlockSpec((BLOCK, D_MODEL), lambda n: (n, 0)), pl.BlockSpec((BLOCK, D_ATTN), lambda n: (n, 0))),
        compiler_params=_params(("parallel",)),
    )(z, z, sinks, ln_g, ln_b, sgu_w, sgu_bt)


def _out_proj_loss(cat, wout, x, target, mod, final_g):
    s, d = x.shape
    tm = min(256, s)

    def body(cat_ref, w_ref, x_ref, t_ref, mod_ref, fg_ref, dx1_ref, dy_ref, dfg_ref, dgate_ref, loss_ref):
        @pl.when(pl.program_id(0) == 0)
        def _():
            dfg_ref[...] = jnp.zeros_like(dfg_ref)
            dgate_ref[...] = jnp.zeros_like(dgate_ref)
            loss_ref[...] = jnp.zeros_like(loss_ref)

        gate = mod_ref[:, 2 * d:3 * d]
        fg = fg_ref[...]
        yv = _dot(cat_ref[...], w_ref[...], NN)
        x1 = x_ref[...] + gate * yv
        r = lax.rsqrt(jnp.mean(x1 * x1, axis=-1, keepdims=True) + EPS)
        xr = x1 * r
        err = xr * fg - t_ref[...]
        dout = err * (1.0 / d)
        gy = dout * fg
        dx1 = r * gy - xr * (r * r * jnp.mean(gy * x1, axis=-1, keepdims=True))
        dx1_ref[...] = dx1
        dy_ref[...] = (gate * dx1).astype(BF16)
        dfg_ref[...] += jnp.sum(dout * xr, axis=0, keepdims=True)
        dgate_ref[...] += jnp.sum(dx1 * yv, axis=0, keepdims=True)
        loss_ref[...] += 0.5 * jnp.sum(jnp.mean(err * err, axis=-1, keepdims=True), axis=0, keepdims=True)

    row = lambda i: (i, 0)
    const = lambda i: (0, 0)
    return pl.pallas_call(
        body, name="out_proj_loss", grid=(s // tm,),
        out_shape=(jax.ShapeDtypeStruct((s, d), F32), jax.ShapeDtypeStruct((s, d), BF16), jax.ShapeDtypeStruct((1, d), F32),
                   jax.ShapeDtypeStruct((1, d), F32), jax.ShapeDtypeStruct((1, 1), F32)),
        in_specs=[pl.BlockSpec((tm, d), row), pl.BlockSpec((d, d), const), pl.BlockSpec((tm, d), row), pl.BlockSpec((tm, d), row),
                  pl.BlockSpec((1, 3 * d), const), pl.BlockSpec((1, d), const)],
        out_specs=(pl.BlockSpec((tm, d), row), pl.BlockSpec((tm, d), row), pl.BlockSpec((1, d), const), pl.BlockSpec((1, d), const),
                   pl.BlockSpec((1, 1), const)),
        compiler_params=_params(("arbitrary",)),
    )(cat, wout, x, target, mod, final_g)


def _mixer_bwd(z, dcat, o, sinks, ln_g, ln_b, sgu_w, sgu_bt):
    s = z.shape[0]
    nb = s // BLOCK

    def body(z_ref, kvp_ref, dc_ref, o_ref, sink_ref, lg_ref, lb_ref, w_ref, bt_ref,
             dz_ref, dsink_ref, dlg_ref, dlb_ref, dw_ref, db_ref, pend, dkv, carry, dsink_acc, dbt_acc):
        n = pl.program_id(0)

        @pl.when(n == 0)
        def _():
            pend[...] = jnp.zeros_like(pend)
            carry[...] = jnp.zeros_like(carry)
            dsink_acc[...] = jnp.zeros_like(dsink_acc)
            dbt_acc[...] = jnp.zeros_like(dbt_acc)
            dlg_ref[...] = jnp.zeros_like(dlg_ref)
            dlb_ref[...] = jnp.zeros_like(dlb_ref)
            dw_ref[...] = jnp.zeros_like(dw_ref)

        dz_ref[:, 0:K0] = pend[:, 0:K0]
        dz_ref[:, GA0:D_IN] = pend[:, GA0:D_IN]

        @pl.when(n == nb)
        def _():
            dkv[...] = jnp.zeros_like(dkv)

        @pl.when(n < nb)
        def _():
            valid, tril = _mixer_masks(n)
            k_ext, v_ext = _kv_ext(z_ref, kvp_ref)
            ga = z_ref[:, GA0:GA0 + D_ATTN]
            sg = _sigmoid(ga)
            datt = dc_ref[:, 0:D_ATTN]
            pend[:, GA0:GA0 + D_ATTN] = (datt * o_ref[...] * (sg * (1.0 + ga * (1.0 - sg)))).astype(BF16)
            do_all = (datt * (ga * sg)).astype(BF16)
            for hk in range(2):
                kv = slice(HEAD_DIM * hk, HEAD_DIM * (hk + 1))
                kh, vh = k_ext[:, kv], v_ext[:, kv]
                qs = _stacked_queries(z_ref, hk, ATTN_SCALE)
                p, p_sink = _attn_probs(qs, kh, valid, _sink_row(sink_ref, hk))
                do = jnp.concatenate([do_all[:, _head_cols(0, Q_PER_KV * hk + g)] for g in range(Q_PER_KV)], axis=0)
                dp = _dot(vh, do, NT)
                delta = jnp.sum(p * dp, axis=0, keepdims=True)
                ds = (p * (dp - delta)).astype(BF16)
                dsk = -p_sink * delta
                dq_t = _dot(kh, ds, TN) * ATTN_SCALE
                for g in range(Q_PER_KV):
                    h = Q_PER_KV * hk + g
                    dsink_acc[:, h:h + 1] += jnp.sum(dsk[:, BLOCK * g:BLOCK * (g + 1)], axis=1, keepdims=True)
                    pend[:, _head_cols(Q0, h)] = dq_t[:, BLOCK * g:BLOCK * (g + 1)].T.astype(BF16)
                dkv[:, kv] = _dot(ds, qs, NN)
                dkv[:, D_KV + HEAD_DIM * hk:D_KV + HEAD_DIM * (hk + 1)] = _dot(p.astype(BF16), do, NN)

            xhat, rstd = _layer_norm_parts(z_ref[:, VS0:VS0 + D_SGU])
            lg = lg_ref[...]
            vn = (xhat * lg + lb_ref[...]).astype(BF16)
            ws, mixed = _sgu_mixed(w_ref, bt_ref, vn, tril)
            u = z_ref[:, U0:U0 + D_SGU]
            gt = z_ref[:, GS0:GS0 + D_SGU]
            sg = _sigmoid(gt)
            dsgu = dc_ref[:, D_ATTN:D_MODEL]
            dgated = dsgu * (gt * sg)
            pend[:, U0:U0 + D_SGU] = (dgated * mixed).astype(BF16)
            pend[:, GS0:GS0 + D_SGU] = (dsgu * (u * mixed) * (sg * (1.0 + gt * (1.0 - sg)))).astype(BF16)
            dmixed = dgated * u
            dmb = dmixed.astype(BF16)
            dvn_parts = []
            for g in range(SGU_GROUPS):
                gs_ = slice(GROUP_DIM * g, GROUP_DIM * (g + 1))
                dw_ref[g] += jnp.where(tril, _dot(dmb[:, gs_], vn[:, gs_], NT), 0.0)
                dbt_acc[:, g:g + 1] += jnp.sum(dmixed[:, gs_], axis=-1, keepdims=True)
                dvn_parts.append(_dot(ws[g], dmb[:, gs_], TN))
            dvn = jnp.concatenate(dvn_parts, axis=1)
            dlg_ref[...] += jnp.sum(dvn * xhat, axis=0, keepdims=True)
            dlb_ref[...] += jnp.sum(dvn, axis=0, keepdims=True)
            dxh = dvn * lg
            pend[:, VS0:VS0 + D_SGU] = (rstd * (dxh - jnp.mean(dxh, axis=-1, keepdims=True)
                                                - xhat * jnp.mean(dxh * xhat, axis=-1, keepdims=True))).astype(BF16)

        dz_ref[:, K0:GA0] = (carry[...] + dkv[0:BLOCK, :]).astype(BF16)
        carry[...] = dkv[BLOCK:2 * BLOCK, :]

        @pl.when(n == nb)
        def _():
            dsink_ref[...] = dsink_acc[:, 0:N_Q_HEADS]
            db_ref[...] = dbt_acc[...].T[0:SGU_GROUPS, :]

    cur = lambda n: (jnp.minimum(n, nb - 1), 0)
    const2 = lambda n: (0, 0)
    return pl.pallas_call(
        body, name="mixer_bwd", grid=(nb + 1,),
        out_shape=(jax.ShapeDtypeStruct((s, D_IN), BF16), jax.ShapeDtypeStruct((1, N_Q_HEADS), F32),
                   jax.ShapeDtypeStruct((1, D_SGU), F32), jax.ShapeDtypeStruct((1, D_SGU), F32),
                   jax.ShapeDtypeStruct((SGU_GROUPS, BLOCK, BLOCK), F32), jax.ShapeDtypeStruct((SGU_GROUPS, BLOCK), F32)),
        in_specs=[pl.BlockSpec((BLOCK, D_IN), cur),
                  pl.BlockSpec((BLOCK, 2 * D_KV), lambda n: (jnp.maximum(jnp.minimum(n, nb - 1) - 1, 0), K0 // (2 * D_KV))),
                  pl.BlockSpec((BLOCK, D_MODEL), cur), pl.BlockSpec((BLOCK, D_ATTN), cur),
                  pl.BlockSpec((1, N_Q_HEADS), const2), pl.BlockSpec((1, D_SGU), const2), pl.BlockSpec((1, D_SGU), const2),
                  pl.BlockSpec((SGU_GROUPS, BLOCK, BLOCK), lambda n: (0, 0, 0)), pl.BlockSpec((BLOCK, SGU_GROUPS), const2)],
        out_specs=(pl.BlockSpec((BLOCK, D_IN), lambda n: (jnp.maximum(n - 1, 0), 0)),
                   pl.BlockSpec((1, N_Q_HEADS), const2), pl.BlockSpec((1, D_SGU), const2), pl.BlockSpec((1, D_SGU), const2),
                   pl.BlockSpec((SGU_GROUPS, BLOCK, BLOCK), lambda n: (0, 0, 0)), pl.BlockSpec((SGU_GROUPS, BLOCK), const2)),
        scratch_shapes=[pltpu.VMEM((BLOCK, D_IN), BF16), pltpu.VMEM((2 * BLOCK, 2 * D_KV), F32), pltpu.VMEM((BLOCK, 2 * D_KV), F32),
                        pltpu.VMEM((1, BLOCK), F32), pltpu.VMEM((BLOCK, BLOCK), F32)],
        compiler_params=_params(("arbitrary",)),
    )(z, z, dcat, o, sinks, ln_g, ln_b, sgu_w, sgu_bt)


def _in_proj_bwd(dz, win_t, x, dx1, norm_g, mod):
    s, d = x.shape
    kdim = dz.shape[1]
    tm = min(256, s)

    def body(dz_ref, w_ref, x_ref, dx1_ref, g_ref, mod_ref, gx_ref, dshift_ref, dscale_ref, dng_ref):
        @pl.when(pl.program_id(0) == 0)
        def _():
            dshift_ref[...] = jnp.zeros_like(dshift_ref)
            dscale_ref[...] = jnp.zeros_like(dscale_ref)
            dng_ref[...] = jnp.zeros_like(dng_ref)

        dh = _dot(dz_ref[...], w_ref[...], NN)
        xv = x_ref[...]
        ng = g_ref[...]
        r = lax.rsqrt(jnp.mean(xv * xv, axis=-1, keepdims=True) + EPS)
        xn = xv * r
        dshift_ref[...] += jnp.sum(dh, axis=0, keepdims=True)
        dscale_ref[...] += jnp.sum(dh * (xn * ng), axis=0, keepdims=True)
        dhs = dh * (1.0 + mod_ref[:, d:2 * d])
        dng_ref[...] += jnp.sum(dhs * xn, axis=0, keepdims=True)
        dxn = dhs * ng
        gx_ref[...] = dx1_ref[...] + r * dxn - xn * (r * r * jnp.mean(dxn * xv, axis=-1, keepdims=True))

    row = lambda i: (i, 0)
    const = lambda i: (0, 0)
    return pl.pallas_call(
        body, name="in_proj_bwd", grid=(s // tm,),
        out_shape=(jax.ShapeDtypeStruct((s, d), F32),) + (jax.ShapeDtypeStruct((1, d), F32),) * 3,
        in_specs=[pl.BlockSpec((tm, kdim), row), pl.BlockSpec((kdim, d), const, pipeline_mode=pl.Buffered(1)),
                  pl.BlockSpec((tm, d), row), pl.BlockSpec((tm, d), row), pl.BlockSpec((1, d), const), pl.BlockSpec((1, 3 * d), const)],
        out_specs=(pl.BlockSpec((tm, d), row),) + (pl.BlockSpec((1, d), const),) * 3,
        compiler_params=_params(("arbitrary",)),
    )(dz, win_t, x, dx1, norm_g, mod)


def _adamw_rows(w, g, m, v, tile, name, after):
    r, c = w.shape

    def body(w_ref, g_ref, m_ref, v_ref, after_ref, go_ref, d_ref, mo_ref, vo_ref):
        g_ = g_ref[...]
        delta, m2, v2 = _adamw(w_ref[...], g_, m_ref[...], v_ref[...])
        go_ref[...] = g_
        d_ref[...] = delta
        mo_ref[...] = m2
        vo_ref[...] = v2

    blk = pl.BlockSpec((tile, c), lambda i: (i, 0))
    return pl.pallas_call(
        body, name=name, grid=(r // tile,), out_shape=(jax.ShapeDtypeStruct((r, c), F32),) * 4,
        in_specs=[blk] * 4 + [ANY], out_specs=(blk,) * 4,
        compiler_params=_params(("parallel",)),
    )(w, g, m, v, after)


def _adamw_ada(cact, dmod, w, m, v, tile=256):
    r, c = w.shape

    def body(c_ref, d_ref, w_ref, m_ref, v_ref, go_ref, dl_ref, mo_ref, vo_ref):
        g_ = _dot(c_ref[...].astype(BF16), d_ref[...].astype(BF16), TN)
        delta, m2, v2 = _adamw(w_ref[...], g_, m_ref[...], v_ref[...])
        go_ref[...] = g_
        dl_ref[...] = delta
        mo_ref[...] = m2
        vo_ref[...] = v2

    blk = pl.BlockSpec((tile, c), lambda i: (i, 0))
    return pl.pallas_call(
        body, name="adamw_ada", grid=(r // tile,), out_shape=(jax.ShapeDtypeStruct((r, c), F32),) * 4,
        in_specs=[pl.BlockSpec((N_DEV, tile), lambda i: (0, i)), pl.BlockSpec((N_DEV, c), lambda i: (0, 0)), blk, blk, blk],
        out_specs=(blk,) * 4,
        compiler_params=_params(("parallel",)),
    )(cact, dmod, w, m, v)


def kernel(x, c, norm_g, w_ada, b_ada, w_in, attn_sinks, sgu_ln_g, sgu_ln_b, sgu_w, sgu_b, w_out, final_g, loss_target, m_norm_g, m_w_ada, m_b_ada, m_w_in, m_attn_sinks, m_sgu_ln_g, m_sgu_ln_b, m_sgu_w, m_sgu_b, m_w_out, m_final_g, v_norm_g, v_w_ada, v_b_ada, v_w_in, v_attn_sinks, v_sgu_ln_g, v_sgu_ln_b, v_sgu_w, v_sgu_b, v_w_out, v_final_g):
    xs, tgt = x[0], loss_target[0]
    core = lax.axis_index("c").astype(jnp.int32).reshape(1)
    chip = 2 * lax.axis_index("x") + lax.axis_index("y")
    place = jnp.stack([core[0], chip.astype(jnp.int32)])
    sgu_bt = sgu_b[0].T

    w_in_t, m_in_t, v_in_t = (jnp.swapaxes(a[0], 0, 1) for a in (w_in, m_w_in, v_w_in))
    win_t = _cast_into_gathered(w_in_t, IN_SHARD // 4, place, "cast_w_in", c)
    win_t, mod, cact, h, z = _gather_w_in_modulate_project(win_t, c, w_ada[0], b_ada, xs, norm_g)
    wout = _cast_into_gathered(w_out[0], OUT_SHARD // 2, place, "cast_w_out", win_t)
    sems, bufs, token = _split_start(_gather_copies, [wout], 3, "w_out_gather_start", win_t)
    z = _in_proj_odd(h, win_t, z, token)
    cat, o = _mixer_fwd(z, attn_sinks, sgu_ln_g, sgu_ln_b, sgu_w[0], sgu_bt)
    wout = _gather_finish(_split_wait(_gather_copies, sems, bufs, cat, "w_out_gather_wait")[0], "w_out_gather_finish")
    dx1, dy, dfg, dgate, loss = _out_proj_loss(cat, wout, xs, tgt, mod, final_g.reshape(1, D_MODEL))

    po = _matmul_tn(cat, dy, 512, "w_out_grad")
    sems, bufs, token = _split_start(_owner_copies, [po, lax.empty((N_DEV, OUT_SHARD, HALF), BF16)], N_DEV - 1,
                                     "w_out_grad_to_owners_start", cat)
    dcat = _matmul_nt(dy, wout, 256, F32, "out_proj_bwd", token)
    dz, dsinks, dlg, dlb, dsgu_w, dsgu_b = _mixer_bwd(z, dcat, o, attn_sinks, sgu_ln_g, sgu_ln_b, sgu_w[0], sgu_bt)
    po, slots_o = _split_wait(_owner_copies, sems, bufs, dz, "w_out_grad_to_owners_wait")
    go = _reduce_and_swap(slots_o, po, OUT_SHARD // 2, "w_out_grad_reduce")

    pw = _w_in_grad_chip_partial(dz, h)
    first = functools.partial(_chip_copies, [(0, 0, IN_SHARD), (1, 0, IN_SHARD), (2, 0, IN_SHARD // 4)])
    rest = functools.partial(_chip_copies, [(2, IN_SHARD // 4, 3 * IN_SHARD // 4)])
    sems, bufs, token = _split_start(first, [pw, lax.empty((N_CHIPS, IN_SHARD, HALF), BF16)], 3, "w_in_grad_to_chips_start", dz)
    gx, dshift, dscale, dng = _in_proj_bwd(dz, win_t, xs, dx1, norm_g + token[0:1, 0:1], mod)
    bufs = _split_wait(first, sems, bufs, gx, "w_in_grad_to_chips_wait")

    parts = dict(dng=dng, dshift=dshift, dscale=dscale, dgate=dgate, dfg=dfg, dlg=dlg, dlb=dlb, dsinks=dsinks,
                 dsgu_b=dsgu_b, dsgu_w=dsgu_w, loss=loss)
    small = dict(norm_g=(norm_g, m_norm_g, v_norm_g), b_ada=(b_ada, m_b_ada, v_b_ada),
                 attn_sinks=(attn_sinks, m_attn_sinks, v_attn_sinks), sgu_ln_g=(sgu_ln_g, m_sgu_ln_g, v_sgu_ln_g),
                 sgu_ln_b=(sgu_ln_b, m_sgu_ln_b, v_sgu_ln_b), sgu_w=(sgu_w, m_sgu_w, v_sgu_w), sgu_b=(sgu_b, m_sgu_b, v_sgu_b),
                 final_g=(final_g, m_final_g, v_final_g))
    tot, gsgu_w, dmod_mine = _small_grads_sum(parts, bufs[1])
    sems, bufs, token = _split_start(rest, list(bufs), 1, "w_in_grad_rest_to_chips_start", tot)
    res, total = _small_adamw(tot, gsgu_w, small)
    res["w_ada"] = _adamw_ada(cact, dmod_mine + token[0:1, 0:1], w_ada[0], m_w_ada[0], v_w_ada[0])
    res["w_out"] = _adamw_rows(w_out[0], go, m_w_out[0], v_w_out[0], 256, "adamw_w_out", res["w_ada"][1])

    pw, slots_w = _split_wait(rest, sems, bufs, res["w_out"][1], "w_in_grad_rest_to_chips_wait")
    gw_t = _reduce_and_swap(slots_w, pw, IN_SHARD // 4, "w_in_grad_reduce")
    res["w_in"] = tuple(jnp.swapaxes(a, 0, 1) for a in _adamw_rows(w_in_t, gw_t, m_in_t, v_in_t, IN_SHARD // 4, "adamw_w_in", gw_t))

    order = ["norm_g", "w_ada", "b_ada", "w_in", "attn_sinks", "sgu_ln_g", "sgu_ln_b", "sgu_w", "sgu_b", "w_out", "final_g"]
    like = dict(norm_g=norm_g, w_ada=w_ada, b_ada=b_ada, w_in=w_in, attn_sinks=attn_sinks, sgu_ln_g=sgu_ln_g, sgu_ln_b=sgu_ln_b,
                sgu_w=sgu_w, sgu_b=sgu_b, w_out=w_out, final_g=final_g)
    outs = [total[0, 0], gx[None]]
    for kind in range(4):
        outs += [res[n][kind].reshape(like[n].shape) for n in order]
    return tuple(outs)
```

```python
import functools
import math

import jax
import jax.numpy as jnp
from jax import lax
from jax.experimental import pallas as pl
from jax.experimental.pallas import tpu as pltpu

F32 = jnp.float32
BF16 = jnp.bfloat16

D_MODEL = 2048
D_ATTN = 1024
HEAD_DIM = 64
N_Q_HEADS = 16
Q_PER_KV = 8
D_KV = 128
BLOCK = 128
D_SGU = 1024
SGU_GROUPS = 8
GROUP_DIM = 128
D_IN = 5376
EPS = 1e-6
ATTN_SCALE = 1.0 / math.sqrt(HEAD_DIM)
N_CHIPS = 4
N_DEV = 8
IN_SHARD = D_IN // N_CHIPS
OUT_SHARD = D_MODEL // N_CHIPS
ADA_SHARD = 3 * D_MODEL // N_CHIPS
HALF = D_MODEL // 2
IN_TILE = 768

Q0, K0, V0, GA0, U0, VS0, GS0 = 0, 1024, 1152, 1280, 2304, 3328, 4352

ADAM_LR = 0.001
ADAM_B1 = 0.9
ADAM_B2 = 0.999
ADAM_EPS = 1e-08
ADAM_WD = 0.01
ADAM_STEP = 10

VMEM_LIMIT_V7X = 56 * 1024 * 1024

NN = (((1,), (0,)), ((), ()))
NT = (((1,), (1,)), ((), ()))
TN = (((0,), (0,)), ((), ()))
MESH = pl.DeviceIdType.MESH
ANY = pl.BlockSpec(memory_space=pl.ANY)
VMEM = pl.BlockSpec(memory_space=pltpu.VMEM)
HBM = pl.BlockSpec(memory_space=pltpu.HBM)
SEM = pl.BlockSpec(memory_space=pltpu.SEMAPHORE)
EFFECT = pltpu.SideEffectType.DATAFLOW_SIDE_EFFECTING


def _dot(a, b, dims):
    return lax.dot_general(a, b, dims, preferred_element_type=F32)


def _params(sem=None):
    return pltpu.CompilerParams(dimension_semantics=sem, vmem_limit_bytes=VMEM_LIMIT_V7X)


def _sigmoid(x):
    return jax.nn.sigmoid(x)


def _adamw(w, g, m, v):
    m = ADAM_B1 * m + (1.0 - ADAM_B1) * g
    v = ADAM_B2 * v + (1.0 - ADAM_B2) * (g * g)
    m_hat = m / (1.0 - ADAM_B1 ** ADAM_STEP)
    v_hat = v / (1.0 - ADAM_B2 ** ADAM_STEP)
    delta = -ADAM_LR * (m_hat / (jnp.sqrt(v_hat) + ADAM_EPS) + ADAM_WD * w)
    return delta, m, v


def _place():
    return lax.axis_index("x"), lax.axis_index("y"), lax.axis_index("c")


def _flip(p, bit):
    return 1 - p if bit else p


def _other_chips(x, y):
    return [(_flip(x, bx), _flip(y, by)) for bx, by in ((0, 1), (1, 0), (1, 1))]


def _split_start(copies, bufs, n_sems, name, after):
    n = len(bufs)

    def body(*refs):
        send_sems, recv_sems = refs[n + 1:n + 3]
        for cp in copies(*refs[:n], send_sems, recv_sems):
            cp.start()
        refs[-1][...] = jnp.zeros_like(refs[-1])

    bufs = [pltpu.with_memory_space_constraint(a, pltpu.HBM) for a in bufs]
    outs = pl.pallas_call(
        body, name=name,
        out_shape=(pltpu.SemaphoreType.DMA((n_sems,)), pltpu.SemaphoreType.DMA((n_sems,)))
        + tuple(pltpu.HBM(a.shape, a.dtype) for a in bufs) + (jax.ShapeDtypeStruct((8, 128), F32),),
        in_specs=[HBM] * n + [ANY], out_specs=(SEM, SEM) + (HBM,) * n + (VMEM,), input_output_aliases={k: k + 2 for k in range(n)},
        compiler_params=pltpu.CompilerParams(has_side_effects=EFFECT),
    )(*bufs, after)
    return outs[:2], list(outs[2:2 + n]), outs[-1]


def _split_wait(copies, sems, bufs, after, name):
    n = len(bufs)

    def body(*refs):
        send_sems, recv_sems = refs[n:n + 2]
        for cp in copies(*refs[:n], send_sems, recv_sems):
            cp.wait_send()
            cp.wait_recv()

    return pl.pallas_call(
        body, name=name, out_shape=tuple(pltpu.HBM(a.shape, a.dtype) for a in bufs),
        in_specs=[HBM] * n + [SEM, SEM, ANY], out_specs=(HBM,) * n, input_output_aliases={k: k for k in range(n)},
        compiler_params=pltpu.CompilerParams(has_side_effects=EFFECT),
    )(*bufs, *sems, after)


def _gather_copies(full_ref, send_sems, recv_sems):
    x, y, cc = _place()
    shard = full_ref.shape[0] // N_CHIPS
    half = shard // 2
    mine = full_ref.at[pl.ds(shard * (2 * x + y) + half * cc, half), :]
    return [pltpu.make_async_remote_copy(src_ref=mine, dst_ref=mine, send_sem=send_sems.at[j], recv_sem=recv_sems.at[j],
                                         device_id=(px, py, cc), device_id_type=MESH)
            for j, (px, py) in enumerate(_other_chips(x, y))]


def _gather_finish(full, name):
    shard = full.shape[0] // N_CHIPS
    half = shard // 2

    def body(full_in, full_ref, send_sems, recv_sems):
        x, y, cc = _place()
        cps = []
        for j, (px, py) in enumerate(_other_chips(x, y)):
            landed = full_ref.at[pl.ds(shard * (2 * px + py) + half * cc, half), :]
            cps.append(pltpu.make_async_remote_copy(src_ref=landed, dst_ref=landed, send_sem=send_sems.at[j], recv_sem=recv_sems.at[j],
                                                    device_id=(x, y, 1 - cc), device_id_type=MESH))
        for cp in cps:
            cp.start()
        for cp in cps:
            cp.wait()

    return pl.pallas_call(
        body, name=name, out_shape=jax.ShapeDtypeStruct(full.shape, full.dtype),
        in_specs=[ANY], out_specs=ANY, input_output_aliases={0: 0},
        scratch_shapes=[pltpu.SemaphoreType.DMA((3,)), pltpu.SemaphoreType.DMA((3,))],
        compiler_params=_params(),
    )(full)


def _gather_w_in_modulate_project(full, c, w_ada, b_ada, xs, norm_g):
    shard = full.shape[0] // N_CHIPS
    half = shard // 2
    quarter = half // 2
    kdim = w_ada.shape[0]
    kc = 256
    s, d = xs.shape
    rows_at_once = min(256, s)
    n_chunks = s // rows_at_once
    m_rows = min(512, s)
    n_m = s // m_rows

    def body(full_in, c_ref, w_hbm, b_ref, x_hbm, g_ref, full_ref, mod_ref, cact_ref, h_hbm, z_hbm,
             w_vmem, cbuf, pbuf, rbuf, h_vmem, xbuf, bbuf, zbuf, w_sem, x_sems, b_sem, z_sems, h_sem,
             send_sems, recv_sems, msend_sems, mrecv_sems):
        x, y, cc = _place()
        me = 4 * x + 2 * y + cc
        chip = 2 * x + y
        across_x, across_y, far = (1 - x, y), (x, 1 - y), (1 - x, 1 - y)

        def rows(chip_xy, core, q):
            return full_ref.at[pl.ds(shard * (2 * chip_xy[0] + chip_xy[1]) + half * core + quarter * q, quarter), :]

        def copy(k, piece, to):
            return pltpu.make_async_remote_copy(src_ref=piece, dst_ref=piece, send_sem=send_sems.at[k], recv_sem=recv_sems.at[k],
                                                device_id=to, device_id_type=MESH)

        def small(k, src, dst, to):
            return pltpu.make_async_remote_copy(src_ref=src, dst_ref=dst, send_sem=msend_sems.at[k], recv_sem=mrecv_sems.at[k],
                                                device_id=to, device_id_type=MESH)

        def x_chunk(i):
            return pltpu.make_async_copy(x_hbm.at[pl.ds(rows_at_once * i, rows_at_once), :], xbuf.at[i % 2], x_sems.at[i % 2])

        direct = [copy(0, rows((x, y), cc, 0), (*across_x, cc)), copy(2, rows((x, y), cc, 1), (*across_y, cc)),
                  copy(1, rows((x, y), cc, 1), (*across_x, cc)), copy(3, rows((x, y), cc, 0), (*across_y, cc))]
        landing = [(0, across_x, 0, (4, (*across_y, cc))), (2, across_y, 1, (5, (*across_x, cc))),
                   (1, across_x, 1, None), (3, across_y, 0, None), (4, far, 0, None), (5, far, 1, None)]
        passed = []

        def land(n):
            k, whose, q, onward = landing[n]
            piece = rows(whose, cc, q)
            copy(k, piece, (x, y, cc)).wait_recv()
            if onward is not None:
                passed.append(copy(onward[0], piece, onward[1]))
                passed[-1].start()
            passed.append(copy(6 + n, piece, (x, y, 1 - cc)))
            passed[-1].start()

        def from_sibling(n):
            k, whose, q, onward = landing[n]
            copy(6 + n, rows(whose, 1 - cc, q), (x, y, cc)).wait_recv()

        cbuf[me] = c_ref[...]
        gathers = [small(k - 1, cbuf.at[me], cbuf.at[me], (_flip(x, (k >> 2) & 1), _flip(y, (k >> 1) & 1), _flip(cc, k & 1)))
                   for k in range(1, N_DEV)]
        for cp in gathers:
            cp.start()
        direct[0].start()
        direct[1].start()
        w_load = pltpu.make_async_copy(w_hbm, w_vmem, w_sem)
        w_load.start()
        x_chunk(0).start()
        for cp in gathers:
            cp.wait()
        w_load.wait()
        for b in range(N_DEV):
            cv = cbuf[b]
            cact_ref[b:b + 1, :] = cv * _sigmoid(cv)
        acc = jnp.zeros((N_DEV, ADA_SHARD), F32)
        for k0 in range(0, kdim, kc):
            acc = acc + _dot(cact_ref[:, k0:k0 + kc].astype(BF16), w_vmem[k0:k0 + kc, :].astype(BF16), NN)
        for b in range(N_DEV):
            pbuf[b] = acc[b:b + 1, :]
        rbuf[chip] = pbuf[me]
        sends = [small(N_DEV - 1 + j, pbuf.at[4 * px + 2 * py + cc], rbuf.at[chip], (px, py, cc))
                 for j, (px, py) in enumerate(_other_chips(x, y))]
        for cp in sends:
            cp.start()
        direct[2].start()
        direct[3].start()
        land(0)
        land(1)
        for cp in sends:
            cp.wait()
        for k in range(N_CHIPS):
            cols = slice(ADA_SHARD * k, ADA_SHARD * (k + 1))
            mod_ref[:, cols] = rbuf[k] + b_ref[:, cols]

        for i in range(n_chunks):
            x_chunk(i).wait()
            if i + 1 < n_chunks:
                x_chunk(i + 1).start()
            xv = xbuf[i % 2]
            r = lax.rsqrt(jnp.mean(xv * xv, axis=-1, keepdims=True) + EPS)
            h_vmem[rows_at_once * i:rows_at_once * (i + 1), :] = (
                (xv * r * g_ref[...]) * (1.0 + mod_ref[:, d:2 * d]) + mod_ref[:, 0:d]).astype(BF16)
        h_out = pltpu.make_async_copy(h_vmem, h_hbm, h_sem)
        h_out.start()

        def project(tile):
            first = pl.multiple_of(IN_TILE * tile, 128)
            b_load = pltpu.make_async_copy(full_ref.at[pl.ds(first, IN_TILE), :], bbuf, b_sem)
            b_load.start()
            b_load.wait()
            outs = []
            for m in range(n_m):
                if m >= 2:
                    outs[m - 2].wait()
                zbuf[m % 2] = _dot(h_vmem[m_rows * m:m_rows * (m + 1), :], bbuf[...], NT)
                outs.append(pltpu.make_async_copy(zbuf.at[m % 2], z_hbm.at[pl.ds(m_rows * m, m_rows), pl.ds(first, IN_TILE)],
                                                  z_sems.at[m % 2]))
                outs[m].start()
            for cp in outs[max(n_m - 2, 0):]:
                cp.wait()

        project(2 * chip)
        land(2)
        land(3)
        for n in range(4):
            from_sibling(n)
        project(2 * (2 * across_x[0] + across_x[1]))
        project(2 * (2 * across_y[0] + across_y[1]))
        land(4)
        land(5)
        from_sibling(4)
        from_sibling(5)
        project(2 * (2 * far[0] + far[1]))
        h_out.wait()
        for cp in direct + passed:
            cp.wait_send()

    return pl.pallas_call(
        body, name="w_in_gather_modulate_project",
        out_shape=(jax.ShapeDtypeStruct(full.shape, full.dtype), jax.ShapeDtypeStruct((1, 3 * D_MODEL), F32),
                   jax.ShapeDtypeStruct((N_DEV, D_MODEL), F32), jax.ShapeDtypeStruct((s, d), BF16),
                   jax.ShapeDtypeStruct((s, full.shape[0]), F32)),
        in_specs=[ANY, VMEM, ANY, VMEM, ANY, VMEM], out_specs=(ANY, VMEM, VMEM, ANY, ANY), input_output_aliases={0: 0},
        scratch_shapes=[pltpu.VMEM(w_ada.shape, F32), pltpu.VMEM((N_DEV, 1, D_MODEL), F32), pltpu.VMEM((N_DEV, 1, ADA_SHARD), F32),
                        pltpu.VMEM((N_CHIPS, 1, ADA_SHARD), F32), pltpu.VMEM((s, d), BF16), pltpu.VMEM((2, rows_at_once, d), F32),
                        pltpu.VMEM((IN_TILE, d), BF16), pltpu.VMEM((2, m_rows, IN_TILE), F32),
                        pltpu.SemaphoreType.DMA, pltpu.SemaphoreType.DMA((2,)), pltpu.SemaphoreType.DMA, pltpu.SemaphoreType.DMA((2,)),
                        pltpu.SemaphoreType.DMA,
                        pltpu.SemaphoreType.DMA((12,)), pltpu.SemaphoreType.DMA((12,)),
                        pltpu.SemaphoreType.DMA((N_DEV + 2,)), pltpu.SemaphoreType.DMA((N_DEV + 2,))],
        compiler_params=_params(),
    )(full, c, w_ada, b_ada, xs, norm_g)


def _w_in_grad_chip_partial(dz, h):
    t_rows, m = dz.shape
    n_tiles = m // IN_TILE

    def body(dz_hbm, h_hbm, p_hbm, h_theirs, h_mine, abuf, sendbuf, recvbuf, obuf, h_sems, a_sems, o_sems, send_sems, recv_sems):
        x, y, cc = _place()

        def a_tile(t, n):
            return pltpu.make_async_copy(dz_hbm.at[:, pl.ds(IN_TILE * t, IN_TILE)], abuf.at[n % 2], a_sems.at[n % 2])

        def half_cols(core):
            return h_hbm.at[:, pl.ds(pl.multiple_of(HALF * core, 128), HALF)]

        loads = [pltpu.make_async_copy(half_cols(1 - cc), h_theirs, h_sems.at[0]),
                 pltpu.make_async_copy(half_cols(cc), h_mine, h_sems.at[1])]
        for cp in loads:
            cp.start()
        a_tile(0, 0).start()
        loads[0].wait()
        sent = []
        for t in range(n_tiles):
            a_tile(t, t).wait()
            a_tile((t + 1) % n_tiles, t + 1).start()
            sendbuf[t] = _dot(abuf[t % 2], h_theirs[...], TN).astype(BF16)
            sent.append(pltpu.make_async_remote_copy(src_ref=sendbuf.at[t], dst_ref=recvbuf.at[t], send_sem=send_sems.at[t],
                                                     recv_sem=recv_sems.at[t], device_id=(x, y, 1 - cc), device_id_type=MESH))
            sent[t].start()
        loads[1].wait()
        outs = []
        for t in range(n_tiles):
            n = n_tiles + t
            a_tile(t, n).wait()
            if t + 1 < n_tiles:
                a_tile(t + 1, n + 1).start()
            sent[t].wait_recv()
            if t >= 2:
                outs[t - 2].wait()
            obuf[t % 2] = (_dot(abuf[n % 2], h_mine[...], TN) + recvbuf[t].astype(F32)).astype(BF16)
            outs.append(pltpu.make_async_copy(obuf.at[t % 2], p_hbm.at[pl.ds(IN_TILE * t, IN_TILE), :], o_sems.at[t % 2]))
            outs[t].start()
        for cp in outs[max(n_tiles - 2, 0):]:
            cp.wait()
        for cp in sent:
            cp.wait_send()

    return pl.pallas_call(
        body, name="w_in_grad_chip_partial", out_shape=jax.ShapeDtypeStruct((m, HALF), BF16),
        in_specs=[ANY, ANY], out_specs=ANY,
        scratch_shapes=[pltpu.VMEM((t_rows, HALF), BF16), pltpu.VMEM((t_rows, HALF), BF16), pltpu.VMEM((2, t_rows, IN_TILE), BF16),
                        pltpu.VMEM((n_tiles, IN_TILE, HALF), BF16), pltpu.VMEM((n_tiles, IN_TILE, HALF), BF16),
                        pltpu.VMEM((2, IN_TILE, HALF), BF16),
                        pltpu.SemaphoreType.DMA((2,)), pltpu.SemaphoreType.DMA((2,)), pltpu.SemaphoreType.DMA((2,)),
                        pltpu.SemaphoreType.DMA((n_tiles,)), pltpu.SemaphoreType.DMA((n_tiles,))],
        compiler_params=_params(),
    )(dz, h)


def _group_copies(whole, src_ref, land_ref, send_sems, recv_sems, first_sem):
    x, y, cc = _place()
    me = 4 * x + 2 * y + cc
    cps = []
    for k in range(1, N_DEV):
        px, py, pc = _flip(x, (k >> 2) & 1), _flip(y, (k >> 1) & 1), _flip(cc, k & 1)
        src = src_ref if whole else src_ref.at[4 * px + 2 * py + pc]
        cps.append(pltpu.make_async_remote_copy(src_ref=src, dst_ref=land_ref.at[me], send_sem=send_sems.at[first_sem + k - 1],
                                                recv_sem=recv_sems.at[first_sem + k - 1], device_id=(px, py, pc), device_id_type=MESH))
    return cps


def _chip_copies(pieces, p_ref, land_ref, send_sems, recv_sems):
    x, y, cc = _place()
    chip = 2 * x + y
    shard = land_ref.shape[1]
    others = _other_chips(x, y)
    cps = []
    for k, (j, r0, nr) in enumerate(pieces):
        px, py = others[j]
        cps.append(pltpu.make_async_remote_copy(
            src_ref=p_ref.at[pl.ds(shard * (2 * px + py) + r0, nr), :], dst_ref=land_ref.at[chip, pl.ds(r0, nr), :],
            send_sem=send_sems.at[k], recv_sem=recv_sems.at[k], device_id=(px, py, cc), device_id_type=MESH))
    return cps


def _owner_copies(p_ref, land_ref, send_sems, recv_sems):
    x, y, cc = _place()
    me = 4 * x + 2 * y + cc
    shard = land_ref.shape[1]
    cps = []
    for k in range(1, N_DEV):
        px, py, pc = _flip(x, (k >> 2) & 1), _flip(y, (k >> 1) & 1), _flip(cc, k & 1)
        block = p_ref.at[pl.ds(shard * (2 * px + py), shard), pl.ds(pl.multiple_of(HALF * pc, 128), HALF)]
        cps.append(pltpu.make_async_remote_copy(src_ref=block, dst_ref=land_ref.at[me], send_sem=send_sems.at[k - 1],
                                                recv_sem=recv_sems.at[k - 1], device_id=(px, py, pc), device_id_type=MESH))
    return cps


def _reduce_and_swap(slots, own, rows_tile, name):
    n_slots, r, hcols = slots.shape
    n_tiles = r // rows_tile
    per_device = n_slots == N_DEV

    def body(slots_hbm, own_hbm, f_hbm, sbuf, obuf, fbuf, in_sems, out_sems, send_sems, recv_sems):
        x, y, cc = _place()
        chip = 2 * x + y
        mine = 2 * chip + cc if per_device else chip
        my_cols = pl.ds(pl.multiple_of(hcols * cc, 128), hcols)

        def loads(t):
            rows = pl.ds(rows_tile * t, rows_tile)
            own_rows = pl.ds(r * chip + rows_tile * t, rows_tile)
            own_src = own_hbm.at[own_rows, my_cols] if per_device else own_hbm.at[own_rows, :]
            return ([pltpu.make_async_copy(slots_hbm.at[k, rows, :], sbuf.at[t % 2, k], in_sems.at[t % 2, k]) for k in range(n_slots)]
                    + [pltpu.make_async_copy(own_src, obuf.at[t % 2], in_sems.at[t % 2, n_slots])])

        def mine_out(t):
            return f_hbm.at[pl.ds(rows_tile * t, rows_tile), my_cols]

        for cp in loads(0):
            cp.start()
        local, remote = [], []
        for t in range(n_tiles):
            for cp in loads(t):
                cp.wait()
            if t + 1 < n_tiles:
                for cp in loads(t + 1):
                    cp.start()
            if t >= 2:
                local[t - 2].wait()
                remote[t - 2].wait_send()
            acc = jnp.zeros((rows_tile, hcols), F32)
            for k in range(n_slots):
                acc = acc + jnp.where(mine == k, obuf[t % 2], sbuf[t % 2, k]).astype(F32)
            fbuf[t % 2] = acc
            local.append(pltpu.make_async_copy(fbuf.at[t % 2], mine_out(t), out_sems.at[t % 2]))
            remote.append(pltpu.make_async_remote_copy(src_ref=fbuf.at[t % 2], dst_ref=mine_out(t), send_sem=send_sems.at[t],
                                                       recv_sem=recv_sems.at[t], device_id=(x, y, 1 - cc), device_id_type=MESH))
            local[t].start()
            remote[t].start()
        for t in range(max(n_tiles - 2, 0), n_tiles):
            local[t].wait()
            remote[t].wait_send()
        for t in range(n_tiles):
            remote[t].wait_recv()

    return pl.pallas_call(
        body, name=name, out_shape=jax.ShapeDtypeStruct((r, 2 * hcols), F32), in_specs=[ANY, ANY], out_specs=ANY,
        scratch_shapes=[pltpu.VMEM((2, n_slots, rows_tile, hcols), BF16), pltpu.VMEM((2, rows_tile, hcols), BF16),
                        pltpu.VMEM((2, rows_tile, hcols), F32), pltpu.SemaphoreType.DMA((2, n_slots + 1)),
                        pltpu.SemaphoreType.DMA((2,)), pltpu.SemaphoreType.DMA((n_tiles,)), pltpu.SemaphoreType.DMA((n_tiles,))],
        compiler_params=_params(),
    )(slots, own)


ROW_NORM_G, ROW_DMOD, ROW_FINAL_G, ROW_LN, ROW_SINKS, ROW_LOSS, ROW_SGU_B, MISC_ROWS = 0, 1, 4, 5, 6, 7, 8, 16


def _small_grads_sum(parts, wland, after):
    part_names = ["dng", "dshift", "dscale", "dgate", "dfg", "dlg", "dlb", "dsinks", "dsgu_b", "dsgu_w", "loss"]
    n_parts = len(part_names)
    n_peers = N_DEV - 1

    def body(*refs):
        p = dict(zip(part_names, refs[:n_parts]))
        wland_ref = refs[n_parts]
        tot_ref, gw_ref, dmod_ref = refs[n_parts + 2:n_parts + 5]
        misc, cmisc, dsend, dbuf, own_sem, send_sems, recv_sems = refs[n_parts + 5:]
        x, y, cc = _place()
        me = 4 * x + 2 * y + cc
        chip = 2 * x + y
        peers = [(_flip(x, (k >> 2) & 1), _flip(y, (k >> 1) & 1), _flip(cc, k & 1)) for k in range(1, N_DEV)]

        def remote(k, src, dst, peer):
            return pltpu.make_async_remote_copy(src_ref=src, dst_ref=dst, send_sem=send_sems.at[k], recv_sem=recv_sems.at[k],
                                                device_id=peer, device_id_type=MESH)

        dsend[:, 0:D_MODEL] = p["dshift"][...]
        dsend[:, D_MODEL:2 * D_MODEL] = p["dscale"][...]
        dsend[:, 2 * D_MODEL:3 * D_MODEL] = p["dgate"][...]

        def shard_cols(chip_index):
            return dsend.at[:, pl.ds(pl.multiple_of(ADA_SHARD * chip_index, 128), ADA_SHARD)]

        own_cols = pltpu.make_async_copy(shard_cols(chip), dbuf.at[me], own_sem)
        own_cols.start()
        gathers = [remote(k, shard_cols(2 * px + py), dbuf.at[me], (px, py, pc)) for k, (px, py, pc) in enumerate(peers)]

        misc[cc] = jnp.zeros((MISC_ROWS, D_MODEL), F32)
        misc[cc, ROW_NORM_G:ROW_NORM_G + 1, :] = p["dng"][...]
        misc[cc, ROW_DMOD:ROW_DMOD + 1, :] = p["dshift"][...]
        misc[cc, ROW_DMOD + 1:ROW_DMOD + 2, :] = p["dscale"][...]
        misc[cc, ROW_DMOD + 2:ROW_DMOD + 3, :] = p["dgate"][...]
        misc[cc, ROW_FINAL_G:ROW_FINAL_G + 1, :] = p["dfg"][...]
        misc[cc, ROW_LN:ROW_LN + 1, 0:D_SGU] = p["dlg"][...]
        misc[cc, ROW_LN:ROW_LN + 1, D_SGU:2 * D_SGU] = p["dlb"][...]
        misc[cc, ROW_SINKS:ROW_SINKS + 1, 0:N_Q_HEADS] = p["dsinks"][...]
        misc[cc, ROW_LOSS:ROW_LOSS + 1, 0:1] = p["loss"][...]
        misc[cc, ROW_SGU_B:ROW_SGU_B + SGU_GROUPS, 0:BLOCK] = p["dsgu_b"][...]
        pair = remote(n_peers, misc.at[cc], misc.at[cc], (x, y, 1 - cc))
        for cp in gathers + [pair]:
            cp.start()
        pair.wait()
        cmisc[chip] = misc[0] + misc[1]
        quad = [remote(n_peers + 1 + j, cmisc.at[chip], cmisc.at[chip], (_flip(x, bx), _flip(y, by), cc))
                for j, (bx, by) in enumerate(((0, 1), (1, 0), (1, 1)))]
        for cp in quad:
            cp.start()
        own_group = p["dsgu_w"][me]
        group = jnp.zeros((BLOCK, BLOCK), F32)
        for b in range(N_DEV):
            group = group + jnp.where(me == b, own_group, wland_ref[b])
        gw_ref[...] = group
        for cp in gathers + quad:
            cp.wait()

        tot = cmisc[0]
        for k in range(1, N_CHIPS):
            tot = tot + cmisc[k]
        tot_ref[...] = tot
        own_cols.wait()
        for b in range(N_DEV):
            dmod_ref[b:b + 1, :] = dbuf[b]

    operands = [parts[n] for n in part_names] + [wland]
    n_sems = n_peers + 1 + (N_CHIPS - 1)
    return pl.pallas_call(
        body, name="small_grads_sum",
        out_shape=(jax.ShapeDtypeStruct((MISC_ROWS, D_MODEL), F32), jax.ShapeDtypeStruct((BLOCK, BLOCK), F32),
                   jax.ShapeDtypeStruct((N_DEV, ADA_SHARD), F32)),
        in_specs=[VMEM] * len(operands) + [ANY], out_specs=(VMEM, VMEM, VMEM),
        scratch_shapes=[pltpu.VMEM((2, MISC_ROWS, D_MODEL), F32), pltpu.VMEM((N_CHIPS, MISC_ROWS, D_MODEL), F32),
                        pltpu.VMEM((1, 3 * D_MODEL), F32), pltpu.VMEM((N_DEV, 1, ADA_SHARD), F32),
                        pltpu.SemaphoreType.DMA, pltpu.SemaphoreType.DMA((n_sems,)), pltpu.SemaphoreType.DMA((n_sems,))],
        compiler_params=_params(),
    )(*operands, after)


def _small_adamw(tot, gw_own, gw_all, place, small):
    names = ["norm_g", "b_ada", "attn_sinks", "sgu_ln_g", "sgu_ln_b", "sgu_w", "sgu_b", "final_g"]

    def body(*refs):
        tot_ref, own_ref, all_ref, place_ref = refs[:4]
        me = 2 * place_ref[1] + place_ref[0]
        wmv = {n: refs[4 + 3 * i: 7 + 3 * i] for i, n in enumerate(names)}
        outs = refs[4 + 3 * len(names):]
        res = {n: outs[4 * i: 4 * i + 4] for i, n in enumerate(names)}
        outs[4 * len(names)][...] = tot_ref[ROW_LOSS:ROW_LOSS + 1, 0:1]

        def update(name, g, pick=lambda r: r[...], put=None):
            w_ref, m_ref, v_ref = wmv[name]
            delta, m2, v2 = _adamw(pick(w_ref), g, pick(m_ref), pick(v_ref))
            for ref, val in zip(res[name], (g, delta, m2, v2)):
                if put is None:
                    ref[...] = val
                else:
                    put(ref, val)

        update("norm_g", tot_ref[ROW_NORM_G:ROW_NORM_G + 1, :])
        update("final_g", tot_ref[ROW_FINAL_G:ROW_FINAL_G + 1, :])
        for t in range(3):
            cols = slice(D_MODEL * t, D_MODEL * (t + 1))

            def put_cols(ref, val, cols=cols):
                ref[:, cols] = val
            update("b_ada", tot_ref[ROW_DMOD + t:ROW_DMOD + t + 1, :], pick=lambda r, cols=cols: r[:, cols], put=put_cols)
        update("sgu_ln_g", tot_ref[ROW_LN:ROW_LN + 1, 0:D_SGU])
        update("sgu_ln_b", tot_ref[ROW_LN:ROW_LN + 1, D_SGU:2 * D_SGU])
        update("attn_sinks", tot_ref[ROW_SINKS:ROW_SINKS + 1, 0:N_Q_HEADS])
        update("sgu_b", tot_ref[ROW_SGU_B:ROW_SGU_B + SGU_GROUPS, 0:BLOCK])
        for g in range(SGU_GROUPS):
            def put_group(ref, val, g=g):
                ref[g] = val
            update("sgu_w", jnp.where(me == g, own_ref[...], all_ref[g]), pick=lambda r, g=g: r[g], put=put_group)

    shapes = {"norm_g": (1, D_MODEL), "b_ada": (1, 3 * D_MODEL), "attn_sinks": (1, N_Q_HEADS), "sgu_ln_g": (1, D_SGU),
              "sgu_ln_b": (1, D_SGU), "sgu_w": (SGU_GROUPS, BLOCK, BLOCK), "sgu_b": (SGU_GROUPS, BLOCK), "final_g": (1, D_MODEL)}
    operands = [tot, gw_own, gw_all, place]
    for n in names:
        operands += [a.reshape(shapes[n]) for a in small[n]]
    out_shape = []
    for n in names:
        out_shape += [jax.ShapeDtypeStruct(shapes[n], F32)] * 4
    out_shape.append(jax.ShapeDtypeStruct((1, 1), F32))
    outs = pl.pallas_call(
        body, name="small_adamw", out_shape=tuple(out_shape),
        in_specs=[VMEM, VMEM, VMEM, pl.BlockSpec(memory_space=pltpu.SMEM)] + [VMEM] * (len(operands) - 4),
        out_specs=tuple([VMEM] * len(out_shape)), compiler_params=_params(),
    )(*operands)
    return {n: outs[4 * i: 4 * i + 4] for i, n in enumerate(names)}, outs[4 * len(names)]


def _cast_into_gathered(w, tile, place, name, after):
    r, c = w.shape
    nblk = r // tile

    def body(place_ref, w_ref, after_ref, o_ref):
        o_ref[...] = w_ref[...].astype(BF16)

    return pl.pallas_call(
        body, name=name,
        grid_spec=pltpu.PrefetchScalarGridSpec(
            num_scalar_prefetch=1, grid=(nblk,), in_specs=[pl.BlockSpec((tile, c), lambda i, p: (i, 0)), ANY],
            out_specs=pl.BlockSpec((tile, c), lambda i, p: (p[1] * nblk + i, 0))),
        out_shape=jax.ShapeDtypeStruct((N_CHIPS * r, c), BF16),
        compiler_params=_params(("parallel",)),
    )(place, w, after)


def _in_proj_odd(h, win_t, z, after):
    s, d = h.shape
    tm = min(1024, s)
    n_odd = win_t.shape[0] // IN_TILE // 2

    def body(h_ref, w_ref, z_in, after_ref, z_ref):
        z_ref[...] = _dot(h_ref[...], w_ref[...], NT)

    return pl.pallas_call(
        body, name="in_proj_odd", grid=(s // tm, n_odd), out_shape=jax.ShapeDtypeStruct(z.shape, z.dtype),
        in_specs=[pl.BlockSpec((tm, d), lambda i, j: (i, 0)), pl.BlockSpec((IN_TILE, d), lambda i, j: (2 * j + 1, 0)), ANY, ANY],
        out_specs=pl.BlockSpec((tm, IN_TILE), lambda i, j: (i, 2 * j + 1)), input_output_aliases={2: 0},
        compiler_params=_params(("parallel", "parallel")),
    )(h, win_t, z, after)


def _matmul_nt(a, bt, tm, out_dtype, name, after):
    m, k = a.shape
    n = bt.shape[0]
    tm = min(tm, m)

    def body(a_ref, b_ref, after_ref, o_ref):
        o_ref[...] = _dot(a_ref[...], b_ref[...], NT).astype(out_dtype)

    return pl.pallas_call(
        body, name=name, grid=(m // tm,), out_shape=jax.ShapeDtypeStruct((m, n), out_dtype),
        in_specs=[pl.BlockSpec((tm, k), lambda i: (i, 0)), pl.BlockSpec((n, k), lambda i: (0, 0), pipeline_mode=pl.Buffered(1)), ANY],
        out_specs=pl.BlockSpec((tm, n), lambda i: (i, 0)),
        compiler_params=_params(("parallel",)),
    )(a, bt, after)


def _matmul_tn(a, b, tm, name):
    t, m = a.shape
    n = b.shape[1]
    tm = min(tm, m)

    def body(a_ref, b_ref, o_ref):
        o_ref[...] = _dot(a_ref[...], b_ref[...], TN).astype(BF16)

    return pl.pallas_call(
        body, name=name, grid=(m // tm,), out_shape=jax.ShapeDtypeStruct((m, n), BF16),
        in_specs=[pl.BlockSpec((t, tm), lambda i: (0, i)), pl.BlockSpec((t, n), lambda i: (0, 0))],
        out_specs=pl.BlockSpec((tm, n), lambda i: (i, 0)),
        compiler_params=_params(("parallel",)),
    )(a, b)


def _mixer_masks(n):
    key = lax.broadcasted_iota(jnp.int32, (2 * BLOCK, Q_PER_KV * BLOCK), 0)
    qry = lax.broadcasted_iota(jnp.int32, (2 * BLOCK, Q_PER_KV * BLOCK), 1) & (BLOCK - 1)
    valid = (key > qry) & (key <= qry + BLOCK) & ((key >= BLOCK) | (n > 0))
    r2 = lax.broadcasted_iota(jnp.int32, (BLOCK, BLOCK), 0)
    c2 = lax.broadcasted_iota(jnp.int32, (BLOCK, BLOCK), 1)
    return valid, r2 >= c2


def _layer_norm_parts(vs):
    mu = jnp.mean(vs, axis=-1, keepdims=True)
    xc = vs - mu
    rstd = lax.rsqrt(jnp.mean(xc * xc, axis=-1, keepdims=True) + EPS)
    return xc * rstd, rstd


def _stacked_queries(z_ref, hk, scale=None):
    q = jnp.concatenate([z_ref[:, _head_cols(Q0, Q_PER_KV * hk + g)] for g in range(Q_PER_KV)], axis=0)
    return (q if scale is None else q * scale).astype(BF16)


def _head_cols(base, h):
    return slice(base + HEAD_DIM * h, base + HEAD_DIM * (h + 1))


def _sink_row(sink_ref, hk):
    return jnp.concatenate([jnp.broadcast_to(sink_ref[:, h:h + 1], (1, BLOCK))
                            for h in range(Q_PER_KV * hk, Q_PER_KV * (hk + 1))], axis=1)


def _attn_probs(qs, kh, valid, sk, scale=None):
    s = _dot(kh, qs, NT)
    s = jnp.where(valid, s if scale is None else s * scale, -jnp.inf)
    m = jnp.maximum(jnp.max(s, axis=0, keepdims=True), sk)
    e = jnp.exp(s - m)
    es = jnp.exp(sk - m)
    inv = 1.0 / (jnp.sum(e, axis=0, keepdims=True) + es)
    return e * inv, es * inv


def _kv_ext(z_ref, kvp_ref):
    k_ext = jnp.concatenate([kvp_ref[:, 0:D_KV], z_ref[:, K0:K0 + D_KV]], axis=0).astype(BF16)
    v_ext = jnp.concatenate([kvp_ref[:, D_KV:2 * D_KV], z_ref[:, V0:V0 + D_KV]], axis=0).astype(BF16)
    return k_ext, v_ext


def _sgu_mixed(w_ref, bt_ref, vn, tril):
    ws = [jnp.where(tril, w_ref[g], 0.0).astype(BF16) for g in range(SGU_GROUPS)]
    mixed = jnp.concatenate([_dot(ws[g], vn[:, GROUP_DIM * g:GROUP_DIM * (g + 1)], NN) + bt_ref[:, g:g + 1]
                             for g in range(SGU_GROUPS)], axis=1)
    return ws, mixed


def _mixer_fwd(z, sinks, ln_g, ln_b, sgu_w, sgu_bt):
    s = z.shape[0]
    nb = s // BLOCK

    def body(z_ref, kvp_ref, sink_ref, lg_ref, lb_ref, w_ref, bt_ref, cat_ref, o_ref):
        n = pl.program_id(0)
        valid, tril = _mixer_masks(n)
        k_ext, v_ext = _kv_ext(z_ref, kvp_ref)
        for hk in range(2):
            kv = slice(HEAD_DIM * hk, HEAD_DIM * (hk + 1))
            p, _ = _attn_probs(_stacked_queries(z_ref, hk), k_ext[:, kv], valid, _sink_row(sink_ref, hk), ATTN_SCALE)
            o_t = _dot(v_ext[:, kv], p.astype(BF16), TN)
            for g in range(Q_PER_KV):
                o_ref[:, _head_cols(0, Q_PER_KV * hk + g)] = o_t[:, BLOCK * g:BLOCK * (g + 1)].T
        ga = z_ref[:, GA0:GA0 + D_ATTN]
        cat_ref[:, 0:D_ATTN] = (o_ref[...] * (ga * _sigmoid(ga))).astype(BF16)
        xhat, _ = _layer_norm_parts(z_ref[:, VS0:VS0 + D_SGU])
        vn = (xhat * lg_ref[...] + lb_ref[...]).astype(BF16)
        _, mixed = _sgu_mixed(w_ref, bt_ref, vn, tril)
        gt = z_ref[:, GS0:GS0 + D_SGU]
        cat_ref[:, D_ATTN:D_MODEL] = ((z_ref[:, U0:U0 + D_SGU] * mixed) * (gt * _sigmoid(gt))).astype(BF16)

    const2 = lambda n: (0, 0)
    return pl.pallas_call(
        body, name="mixer_fwd", grid=(nb,),
        out_shape=(jax.ShapeDtypeStruct((s, D_MODEL), BF16), jax.ShapeDtypeStruct((s, D_ATTN), F32)),
        in_specs=[pl.BlockSpec((BLOCK, D_IN), lambda n: (n, 0)),
                  pl.BlockSpec((BLOCK, 2 * D_KV), lambda n: (jnp.maximum(n - 1, 0), K0 // (2 * D_KV))),
                  pl.BlockSpec((1, N_Q_HEADS), const2), pl.BlockSpec((1, D_SGU), const2), pl.BlockSpec((1, D_SGU), const2),
                  pl.BlockSpec((SGU_GROUPS, BLOCK, BLOCK), lambda n: (0, 0, 0)), pl.BlockSpec((BLOCK, SGU_GROUPS), const2)],
        out_specs=(pl.BlockSpec((BLOCK, D_MODEL), lambda n: (n, 0)), pl.BlockSpec((BLOCK, D_ATTN), lambda n: (n, 0))),
        compiler_params=_params(("parallel",)),
    )(z, z, sinks, ln_g, ln_b, sgu_w, sgu_bt)


def _out_proj_loss(cat, wout, x, target, mod, final_g):
    s, d = x.shape
    tm = min(256, s)

    def body(cat_ref, w_ref, x_ref, t_ref, mod_ref, fg_ref, dx1_ref, dy_ref, dfg_ref, dgate_ref, loss_ref):
        @pl.when(pl.program_id(0) == 0)
        def _():
            dfg_ref[...] = jnp.zeros_like(dfg_ref)
            dgate_ref[...] = jnp.zeros_like(dgate_ref)
            loss_ref[...] = jnp.zeros_like(loss_ref)

        gate = mod_ref[:, 2 * d:3 * d]
        fg = fg_ref[...]
        yv = _dot(cat_ref[...], w_ref[...], NN)
        x1 = x_ref[...] + gate * yv
        r = lax.rsqrt(jnp.mean(x1 * x1, axis=-1, keepdims=True) + EPS)
        xr = x1 * r
        err = xr * fg - t_ref[...]
        dout = err * (1.0 / d)
        gy = dout * fg
        dx1 = r * gy - xr * (r * r * jnp.mean(gy * x1, axis=-1, keepdims=True))
        dx1_ref[...] = dx1
        dy_ref[...] = (gate * dx1).astype(BF16)
        dfg_ref[...] += jnp.sum(dout * xr, axis=0, keepdims=True)
        dgate_ref[...] += jnp.sum(dx1 * yv, axis=0, keepdims=True)
        loss_ref[...] += 0.5 * jnp.sum(jnp.mean(err * err, axis=-1, keepdims=True), axis=0, keepdims=True)

    row = lambda i: (i, 0)
    const = lambda i: (0, 0)
    return pl.pallas_call(
        body, name="out_proj_loss", grid=(s // tm,),
        out_shape=(jax.ShapeDtypeStruct((s, d), F32), jax.ShapeDtypeStruct((s, d), BF16), jax.ShapeDtypeStruct((1, d), F32),
                   jax.ShapeDtypeStruct((1, d), F32), jax.ShapeDtypeStruct((1, 1), F32)),
        in_specs=[pl.BlockSpec((tm, d), row), pl.BlockSpec((d, d), const), pl.BlockSpec((tm, d), row), pl.BlockSpec((tm, d), row),
                  pl.BlockSpec((1, 3 * d), const), pl.BlockSpec((1, d), const)],
        out_specs=(pl.BlockSpec((tm, d), row), pl.BlockSpec((tm, d), row), pl.BlockSpec((1, d), const), pl.BlockSpec((1, d), const),
                   pl.BlockSpec((1, 1), const)),
        compiler_params=_params(("arbitrary",)),
    )(cat, wout, x, target, mod, final_g)


def _mixer_bwd(z, dcat, o, sinks, ln_g, ln_b, sgu_w, sgu_bt):
    s = z.shape[0]
    nb = s // BLOCK

    def body(z_ref, kvp_ref, dc_ref, o_ref, sink_ref, lg_ref, lb_ref, w_ref, bt_ref,
             dz_ref, dsink_ref, dlg_ref, dlb_ref, dw_ref, db_ref, pend, dkv, carry, dsink_acc, dbt_acc):
        n = pl.program_id(0)

        @pl.when(n == 0)
        def _():
            pend[...] = jnp.zeros_like(pend)
            carry[...] = jnp.zeros_like(carry)
            dsink_acc[...] = jnp.zeros_like(dsink_acc)
            dbt_acc[...] = jnp.zeros_like(dbt_acc)
            dlg_ref[...] = jnp.zeros_like(dlg_ref)
            dlb_ref[...] = jnp.zeros_like(dlb_ref)
            dw_ref[...] = jnp.zeros_like(dw_ref)

        dz_ref[:, 0:K0] = pend[:, 0:K0]
        dz_ref[:, GA0:D_IN] = pend[:, GA0:D_IN]

        @pl.when(n == nb)
        def _():
            dkv[...] = jnp.zeros_like(dkv)

        @pl.when(n < nb)
        def _():
            valid, tril = _mixer_masks(n)
            k_ext, v_ext = _kv_ext(z_ref, kvp_ref)
            ga = z_ref[:, GA0:GA0 + D_ATTN]
            sg = _sigmoid(ga)
            datt = dc_ref[:, 0:D_ATTN]
            pend[:, GA0:GA0 + D_ATTN] = (datt * o_ref[...] * (sg * (1.0 + ga * (1.0 - sg)))).astype(BF16)
            do_all = (datt * (ga * sg)).astype(BF16)
            for hk in range(2):
                kv = slice(HEAD_DIM * hk, HEAD_DIM * (hk + 1))
                kh, vh = k_ext[:, kv], v_ext[:, kv]
                qs = _stacked_queries(z_ref, hk, ATTN_SCALE)
                p, p_sink = _attn_probs(qs, kh, valid, _sink_row(sink_ref, hk))
                do = jnp.concatenate([do_all[:, _head_cols(0, Q_PER_KV * hk + g)] for g in range(Q_PER_KV)], axis=0)
                dp = _dot(vh, do, NT)
                delta = jnp.sum(p * dp, axis=0, keepdims=True)
                ds = (p * (dp - delta)).astype(BF16)
                dsk = -p_sink * delta
                dq_t = _dot(kh, ds, TN) * ATTN_SCALE
                for g in range(Q_PER_KV):
                    h = Q_PER_KV * hk + g
                    dsink_acc[:, h:h + 1] += jnp.sum(dsk[:, BLOCK * g:BLOCK * (g + 1)], axis=1, keepdims=True)
                    pend[:, _head_cols(Q0, h)] = dq_t[:, BLOCK * g:BLOCK * (g + 1)].T.astype(BF16)
                dkv[:, kv] = _dot(ds, qs, NN)
                dkv[:, D_KV + HEAD_DIM * hk:D_KV + HEAD_DIM * (hk + 1)] = _dot(p.astype(BF16), do, NN)

            xhat, rstd = _layer_norm_parts(z_ref[:, VS0:VS0 + D_SGU])
            lg = lg_ref[...]
            vn = (xhat * lg + lb_ref[...]).astype(BF16)
            ws, mixed = _sgu_mixed(w_ref, bt_ref, vn, tril)
            u = z_ref[:, U0:U0 + D_SGU]
            gt = z_ref[:, GS0:GS0 + D_SGU]
            sg = _sigmoid(gt)
            dsgu = dc_ref[:, D_ATTN:D_MODEL]
            dgated = dsgu * (gt * sg)
            pend[:, U0:U0 + D_SGU] = (dgated * mixed).astype(BF16)
            pend[:, GS0:GS0 + D_SGU] = (dsgu * (u * mixed) * (sg * (1.0 + gt * (1.0 - sg)))).astype(BF16)
            dmixed = dgated * u
            dmb = dmixed.astype(BF16)
            dvn_parts = []
            for g in range(SGU_GROUPS):
                gs_ = slice(GROUP_DIM * g, GROUP_DIM * (g + 1))
                dw_ref[g] += jnp.where(tril, _dot(dmb[:, gs_], vn[:, gs_], NT), 0.0)
                dbt_acc[:, g:g + 1] += jnp.sum(dmixed[:, gs_], axis=-1, keepdims=True)
                dvn_parts.append(_dot(ws[g], dmb[:, gs_], TN))
            dvn = jnp.concatenate(dvn_parts, axis=1)
            dlg_ref[...] += jnp.sum(dvn * xhat, axis=0, keepdims=True)
            dlb_ref[...] += jnp.sum(dvn, axis=0, keepdims=True)
            dxh = dvn * lg
            pend[:, VS0:VS0 + D_SGU] = (rstd * (dxh - jnp.mean(dxh, axis=-1, keepdims=True)
                                                - xhat * jnp.mean(dxh * xhat, axis=-1, keepdims=True))).astype(BF16)

        dz_ref[:, K0:GA0] = (carry[...] + dkv[0:BLOCK, :]).astype(BF16)
        carry[...] = dkv[BLOCK:2 * BLOCK, :]

        @pl.when(n == nb)
        def _():
            dsink_ref[...] = dsink_acc[:, 0:N_Q_HEADS]
            db_ref[...] = dbt_acc[...].T[0:SGU_GROUPS, :]

    cur = lambda n: (jnp.minimum(n, nb - 1), 0)
    const2 = lambda n: (0, 0)
    return pl.pallas_call(
        body, name="mixer_bwd", grid=(nb + 1,),
        out_shape=(jax.ShapeDtypeStruct((s, D_IN), BF16), jax.ShapeDtypeStruct((1, N_Q_HEADS), F32),
                   jax.ShapeDtypeStruct((1, D_SGU), F32), jax.ShapeDtypeStruct((1, D_SGU), F32),
                   jax.ShapeDtypeStruct((SGU_GROUPS, BLOCK, BLOCK), F32), jax.ShapeDtypeStruct((SGU_GROUPS, BLOCK), F32)),
        in_specs=[pl.BlockSpec((BLOCK, D_IN), cur),
                  pl.BlockSpec((BLOCK, 2 * D_KV), lambda n: (jnp.maximum(jnp.minimum(n, nb - 1) - 1, 0), K0 // (2 * D_KV))),
                  pl.BlockSpec((BLOCK, D_MODEL), cur), pl.BlockSpec((BLOCK, D_ATTN), cur),
                  pl.BlockSpec((1, N_Q_HEADS), const2), pl.BlockSpec((1, D_SGU), const2), pl.BlockSpec((1, D_SGU), const2),
                  pl.BlockSpec((SGU_GROUPS, BLOCK, BLOCK), lambda n: (0, 0, 0)), pl.BlockSpec((BLOCK, SGU_GROUPS), const2)],
        out_specs=(pl.BlockSpec((BLOCK, D_IN), lambda n: (jnp.maximum(n - 1, 0), 0)),
                   pl.BlockSpec((1, N_Q_HEADS), const2), pl.BlockSpec((1, D_SGU), const2), pl.BlockSpec((1, D_SGU), const2),
                   pl.BlockSpec((SGU_GROUPS, BLOCK, BLOCK), lambda n: (0, 0, 0)), pl.BlockSpec((SGU_GROUPS, BLOCK), const2)),
        scratch_shapes=[pltpu.VMEM((BLOCK, D_IN), BF16), pltpu.VMEM((2 * BLOCK, 2 * D_KV), F32), pltpu.VMEM((BLOCK, 2 * D_KV), F32),
                        pltpu.VMEM((1, BLOCK), F32), pltpu.VMEM((BLOCK, BLOCK), F32)],
        compiler_params=_params(("arbitrary",)),
    )(z, z, dcat, o, sinks, ln_g, ln_b, sgu_w, sgu_bt)


def _in_proj_bwd(dz, win_t, x, dx1, norm_g, mod):
    s, d = x.shape
    kdim = dz.shape[1]
    tm = min(256, s)

    def body(dz_ref, w_ref, x_ref, dx1_ref, g_ref, mod_ref, gx_ref, dshift_ref, dscale_ref, dng_ref):
        @pl.when(pl.program_id(0) == 0)
        def _():
            dshift_ref[...] = jnp.zeros_like(dshift_ref)
            dscale_ref[...] = jnp.zeros_like(dscale_ref)
            dng_ref[...] = jnp.zeros_like(dng_ref)

        dh = _dot(dz_ref[...], w_ref[...], NN)
        xv = x_ref[...]
        ng = g_ref[...]
        r = lax.rsqrt(jnp.mean(xv * xv, axis=-1, keepdims=True) + EPS)
        xn = xv * r
        dshift_ref[...] += jnp.sum(dh, axis=0, keepdims=True)
        dscale_ref[...] += jnp.sum(dh * (xn * ng), axis=0, keepdims=True)
        dhs = dh * (1.0 + mod_ref[:, d:2 * d])
        dng_ref[...] += jnp.sum(dhs * xn, axis=0, keepdims=True)
        dxn = dhs * ng
        gx_ref[...] = dx1_ref[...] + r * dxn - xn * (r * r * jnp.mean(dxn * xv, axis=-1, keepdims=True))

    row = lambda i: (i, 0)
    const = lambda i: (0, 0)
    return pl.pallas_call(
        body, name="in_proj_bwd", grid=(s // tm,),
        out_shape=(jax.ShapeDtypeStruct((s, d), F32),) + (jax.ShapeDtypeStruct((1, d), F32),) * 3,
        in_specs=[pl.BlockSpec((tm, kdim), row), pl.BlockSpec((kdim, d), const, pipeline_mode=pl.Buffered(1)),
                  pl.BlockSpec((tm, d), row), pl.BlockSpec((tm, d), row), pl.BlockSpec((1, d), const), pl.BlockSpec((1, 3 * d), const)],
        out_specs=(pl.BlockSpec((tm, d), row),) + (pl.BlockSpec((1, d), const),) * 3,
        compiler_params=_params(("arbitrary",)),
    )(dz, win_t, x, dx1, norm_g, mod)


def _adamw_rows(w, g, m, v, tile, name, after):
    r, c = w.shape

    def body(w_ref, g_ref, m_ref, v_ref, after_ref, go_ref, d_ref, mo_ref, vo_ref):
        g_ = g_ref[...]
        delta, m2, v2 = _adamw(w_ref[...], g_, m_ref[...], v_ref[...])
        go_ref[...] = g_
        d_ref[...] = delta
        mo_ref[...] = m2
        vo_ref[...] = v2

    blk = pl.BlockSpec((tile, c), lambda i: (i, 0))
    return pl.pallas_call(
        body, name=name, grid=(r // tile,), out_shape=(jax.ShapeDtypeStruct((r, c), F32),) * 4,
        in_specs=[blk] * 4 + [ANY], out_specs=(blk,) * 4,
        compiler_params=_params(("parallel",)),
    )(w, g, m, v, after)


def _adamw_ada(cact, dmod, w, m, v, tile=256):
    r, c = w.shape

    def body(c_ref, d_ref, w_ref, m_ref, v_ref, go_ref, dl_ref, mo_ref, vo_ref):
        g_ = _dot(c_ref[...].astype(BF16), d_ref[...].astype(BF16), TN)
        delta, m2, v2 = _adamw(w_ref[...], g_, m_ref[...], v_ref[...])
        go_ref[...] = g_
        dl_ref[...] = delta
        mo_ref[...] = m2
        vo_ref[...] = v2

    blk = pl.BlockSpec((tile, c), lambda i: (i, 0))
    return pl.pallas_call(
        body, name="adamw_ada", grid=(r // tile,), out_shape=(jax.ShapeDtypeStruct((r, c), F32),) * 4,
        in_specs=[pl.BlockSpec((N_DEV, tile), lambda i: (0, i)), pl.BlockSpec((N_DEV, c), lambda i: (0, 0)), blk, blk, blk],
        out_specs=(blk,) * 4,
        compiler_params=_params(("parallel",)),
    )(cact, dmod, w, m, v)


def kernel(x, c, norm_g, w_ada, b_ada, w_in, attn_sinks, sgu_ln_g, sgu_ln_b, sgu_w, sgu_b, w_out, final_g, loss_target, m_norm_g, m_w_ada, m_b_ada, m_w_in, m_attn_sinks, m_sgu_ln_g, m_sgu_ln_b, m_sgu_w, m_sgu_b, m_w_out, m_final_g, v_norm_g, v_w_ada, v_b_ada, v_w_in, v_attn_sinks, v_sgu_ln_g, v_sgu_ln_b, v_sgu_w, v_sgu_b, v_w_out, v_final_g):
    xs, tgt = x[0], loss_target[0]
    core = lax.axis_index("c").astype(jnp.int32).reshape(1)
    chip = 2 * lax.axis_index("x") + lax.axis_index("y")
    place = jnp.stack([core[0], chip.astype(jnp.int32)])
    sgu_bt = sgu_b[0].T

    w_in_t, m_in_t, v_in_t = (jnp.swapaxes(a[0], 0, 1) for a in (w_in, m_w_in, v_w_in))
    win_t = _cast_into_gathered(w_in_t, IN_SHARD // 4, place, "cast_w_in", c)
    win_t, mod, cact, h, z = _gather_w_in_modulate_project(win_t, c, w_ada[0], b_ada, xs, norm_g)
    wout = _cast_into_gathered(w_out[0], OUT_SHARD // 2, place, "cast_w_out", win_t)
    sems, bufs, token = _split_start(_gather_copies, [wout], 3, "w_out_gather_start", win_t)
    z = _in_proj_odd(h, win_t, z, token)
    cat, o = _mixer_fwd(z, attn_sinks, sgu_ln_g, sgu_ln_b, sgu_w[0], sgu_bt)
    wout = _gather_finish(_split_wait(_gather_copies, sems, bufs, cat, "w_out_gather_wait")[0], "w_out_gather_finish")
    dx1, dy, dfg, dgate, loss = _out_proj_loss(cat, wout, xs, tgt, mod, final_g.reshape(1, D_MODEL))

    po = _matmul_tn(cat, dy, 512, "w_out_grad")
    sems, bufs, token = _split_start(_owner_copies, [po, lax.empty((N_DEV, OUT_SHARD, HALF), BF16)], N_DEV - 1,
                                     "w_out_grad_to_owners_start", cat)
    dcat = _matmul_nt(dy, wout, 256, F32, "out_proj_bwd", token)
    dz, dsinks, dlg, dlb, dsgu_w, dsgu_b = _mixer_bwd(z, dcat, o, attn_sinks, sgu_ln_g, sgu_ln_b, sgu_w[0], sgu_bt)
    po, slots_o = _split_wait(_owner_copies, sems, bufs, dz, "w_out_grad_to_owners_wait")
    go = _reduce_and_swap(slots_o, po, OUT_SHARD // 2, "w_out_grad_reduce")

    pw = _w_in_grad_chip_partial(dz, h)
    def first(p, land, g, gland, send_sems, recv_sems):
        return (_chip_copies([(0, 0, IN_SHARD), (1, 0, IN_SHARD), (2, 0, IN_SHARD // 4)], p, land, send_sems, recv_sems)
                + _group_copies(False, g, gland, send_sems, recv_sems, 3))

    def rest(p, land, t, gath, send_sems, recv_sems):
        return (_chip_copies([(2, IN_SHARD // 4, 3 * IN_SHARD // 4)], p, land, send_sems, recv_sems)
                + _group_copies(True, t, gath, send_sems, recv_sems, 1))

    groups = jax.ShapeDtypeStruct((N_DEV, BLOCK, BLOCK), F32)
    sems, bufs, token = _split_start(first, [pw, lax.empty((N_CHIPS, IN_SHARD, HALF), BF16), dsgu_w, lax.empty(groups.shape, F32)],
                                     3 + N_DEV - 1, "w_in_grad_to_chips_start", dz)
    gx, dshift, dscale, dng = _in_proj_bwd(dz, win_t, xs, dx1, norm_g + token[0:1, 0:1], mod)
    pw, land_w, dsgu_w, wland = _split_wait(first, sems, bufs, gx, "w_in_grad_to_chips_wait")

    parts = dict(dng=dng, dshift=dshift, dscale=dscale, dgate=dgate, dfg=dfg, dlg=dlg, dlb=dlb, dsinks=dsinks,
                 dsgu_b=dsgu_b, dsgu_w=dsgu_w, loss=loss)
    small = dict(norm_g=(norm_g, m_norm_g, v_norm_g), b_ada=(b_ada, m_b_ada, v_b_ada),
                 attn_sinks=(attn_sinks, m_attn_sinks, v_attn_sinks), sgu_ln_g=(sgu_ln_g, m_sgu_ln_g, v_sgu_ln_g),
                 sgu_ln_b=(sgu_ln_b, m_sgu_ln_b, v_sgu_ln_b), sgu_w=(sgu_w, m_sgu_w, v_sgu_w), sgu_b=(sgu_b, m_sgu_b, v_sgu_b),
                 final_g=(final_g, m_final_g, v_final_g))
    tot, gw_own, dmod_mine = _small_grads_sum(parts, wland, land_w)
    sems, bufs, token = _split_start(rest, [pw, land_w, gw_own, lax.empty(groups.shape, F32)], 1 + N_DEV - 1,
                                     "w_in_grad_rest_to_chips_start", tot)
    res = {}
    res["w_ada"] = _adamw_ada(cact, dmod_mine + token[0:1, 0:1], w_ada[0], m_w_ada[0], v_w_ada[0])
    res["w_out"] = _adamw_rows(w_out[0], go, m_w_out[0], v_w_out[0], 256, "adamw_w_out", res["w_ada"][1])

    pw, slots_w, gw_own, gw_all = _split_wait(rest, sems, bufs, res["w_out"][1], "w_in_grad_rest_to_chips_wait")
    small_res, total = _small_adamw(tot, gw_own, gw_all, place, small)
    res.update(small_res)
    gw_t = _reduce_and_swap(slots_w, pw, IN_SHARD // 4, "w_in_grad_reduce")
    res["w_in"] = tuple(jnp.swapaxes(a, 0, 1) for a in _adamw_rows(w_in_t, gw_t, m_in_t, v_in_t, IN_SHARD // 4, "adamw_w_in", gw_t))

    order = ["norm_g", "w_ada", "b_ada", "w_in", "attn_sinks", "sgu_ln_g", "sgu_ln_b", "sgu_w", "sgu_b", "w_out", "final_g"]
    like = dict(norm_g=norm_g, w_ada=w_ada, b_ada=b_ada, w_in=w_in, attn_sinks=attn_sinks, sgu_ln_g=sgu_ln_g, sgu_ln_b=sgu_ln_b,
                sgu_w=sgu_w, sgu_b=sgu_b, w_out=w_out, final_g=final_g)
    outs = [total[0, 0], gx[None]]
    for kind in range(4):
        outs += [res[n][kind].reshape(like[n].shape) for n in order]
    return tuple(outs)
```

```python
import functools
import math

import jax
import jax.numpy as jnp
from jax import lax
from jax.experimental import pallas as pl
from jax.experimental.pallas import tpu as pltpu

F32 = jnp.float32
BF16 = jnp.bfloat16

D_MODEL = 2048
D_ATTN = 1024
HEAD_DIM = 64
N_Q_HEADS = 16
Q_PER_KV = 8
D_KV = 128
BLOCK = 128
D_SGU = 1024
SGU_GROUPS = 8
GROUP_DIM = 128
D_IN = 5376
EPS = 1e-6
ATTN_SCALE = 1.0 / math.sqrt(HEAD_DIM)
N_CHIPS = 4
N_DEV = 8
IN_SHARD = D_IN // N_CHIPS
OUT_SHARD = D_MODEL // N_CHIPS
ADA_SHARD = 3 * D_MODEL // N_CHIPS
HALF = D_MODEL // 2
IN_TILE = 768

Q0, K0, V0, GA0, U0, VS0, GS0 = 0, 1024, 1152, 1280, 2304, 3328, 4352

ADAM_LR = 0.001
ADAM_B1 = 0.9
ADAM_B2 = 0.999
ADAM_EPS = 1e-08
ADAM_WD = 0.01
ADAM_STEP = 10

VMEM_LIMIT_V7X = 56 * 1024 * 1024

NN = (((1,), (0,)), ((), ()))
NT = (((1,), (1,)), ((), ()))
TN = (((0,), (0,)), ((), ()))
MESH = pl.DeviceIdType.MESH
ANY = pl.BlockSpec(memory_space=pl.ANY)
VMEM = pl.BlockSpec(memory_space=pltpu.VMEM)
HBM = pl.BlockSpec(memory_space=pltpu.HBM)
SEM = pl.BlockSpec(memory_space=pltpu.SEMAPHORE)
EFFECT = pltpu.SideEffectType.DATAFLOW_SIDE_EFFECTING


def _dot(a, b, dims):
    return lax.dot_general(a, b, dims, preferred_element_type=F32)


def _params(sem=None):
    return pltpu.CompilerParams(dimension_semantics=sem, vmem_limit_bytes=VMEM_LIMIT_V7X)


def _sigmoid(x):
    return jax.nn.sigmoid(x)


def _adamw(w, g, m, v):
    m = ADAM_B1 * m + (1.0 - ADAM_B1) * g
    v = ADAM_B2 * v + (1.0 - ADAM_B2) * (g * g)
    m_hat = m / (1.0 - ADAM_B1 ** ADAM_STEP)
    v_hat = v / (1.0 - ADAM_B2 ** ADAM_STEP)
    delta = -ADAM_LR * (m_hat / (jnp.sqrt(v_hat) + ADAM_EPS) + ADAM_WD * w)
    return delta, m, v


def _place():
    return lax.axis_index("x"), lax.axis_index("y"), lax.axis_index("c")


def _flip(p, bit):
    return 1 - p if bit else p


def _other_chips(x, y):
    return [(_flip(x, bx), _flip(y, by)) for bx, by in ((0, 1), (1, 0), (1, 1))]


def _split_start(copies, bufs, n_sems, name, after):
    n = len(bufs)

    def body(*refs):
        send_sems, recv_sems = refs[n + 1:n + 3]
        for cp in copies(*refs[:n], send_sems, recv_sems):
            cp.start()
        refs[-1][...] = jnp.zeros_like(refs[-1])

    bufs = [pltpu.with_memory_space_constraint(a, pltpu.HBM) for a in bufs]
    outs = pl.pallas_call(
        body, name=name,
        out_shape=(pltpu.SemaphoreType.DMA((n_sems,)), pltpu.SemaphoreType.DMA((n_sems,)))
        + tuple(pltpu.HBM(a.shape, a.dtype) for a in bufs) + (jax.ShapeDtypeStruct((8, 128), F32),),
        in_specs=[HBM] * n + [ANY], out_specs=(SEM, SEM) + (HBM,) * n + (VMEM,), input_output_aliases={k: k + 2 for k in range(n)},
        compiler_params=pltpu.CompilerParams(has_side_effects=EFFECT),
    )(*bufs, after)
    return outs[:2], list(outs[2:2 + n]), outs[-1]


def _split_wait(copies, sems, bufs, after, name):
    n = len(bufs)

    def body(*refs):
        send_sems, recv_sems = refs[n:n + 2]
        for cp in copies(*refs[:n], send_sems, recv_sems):
            cp.wait_send()
            cp.wait_recv()

    return pl.pallas_call(
        body, name=name, out_shape=tuple(pltpu.HBM(a.shape, a.dtype) for a in bufs),
        in_specs=[HBM] * n + [SEM, SEM, ANY], out_specs=(HBM,) * n, input_output_aliases={k: k for k in range(n)},
        compiler_params=pltpu.CompilerParams(has_side_effects=EFFECT),
    )(*bufs, *sems, after)


def _gather_copies(full_ref, send_sems, recv_sems):
    x, y, cc = _place()
    shard = full_ref.shape[0] // N_CHIPS
    half = shard // 2
    mine = full_ref.at[pl.ds(shard * (2 * x + y) + half * cc, half), :]
    return [pltpu.make_async_remote_copy(src_ref=mine, dst_ref=mine, send_sem=send_sems.at[j], recv_sem=recv_sems.at[j],
                                         device_id=(px, py, cc), device_id_type=MESH)
            for j, (px, py) in enumerate(_other_chips(x, y))]


def _gather_finish(full, name):
    shard = full.shape[0] // N_CHIPS
    half = shard // 2

    def body(full_in, full_ref, send_sems, recv_sems):
        x, y, cc = _place()
        cps = []
        for j, (px, py) in enumerate(_other_chips(x, y)):
            landed = full_ref.at[pl.ds(shard * (2 * px + py) + half * cc, half), :]
            cps.append(pltpu.make_async_remote_copy(src_ref=landed, dst_ref=landed, send_sem=send_sems.at[j], recv_sem=recv_sems.at[j],
                                                    device_id=(x, y, 1 - cc), device_id_type=MESH))
        for cp in cps:
            cp.start()
        for cp in cps:
            cp.wait()

    return pl.pallas_call(
        body, name=name, out_shape=jax.ShapeDtypeStruct(full.shape, full.dtype),
        in_specs=[ANY], out_specs=ANY, input_output_aliases={0: 0},
        scratch_shapes=[pltpu.SemaphoreType.DMA((3,)), pltpu.SemaphoreType.DMA((3,))],
        compiler_params=_params(),
    )(full)


def _gather_w_in_modulate_project(full, c, w_ada, b_ada, xs, norm_g):
    shard = full.shape[0] // N_CHIPS
    half = shard // 2
    quarter = half // 2
    kdim = w_ada.shape[0]
    kc = 256
    s, d = xs.shape
    rows_at_once = min(256, s)
    n_chunks = s // rows_at_once
    m_rows = min(512, s)
    n_m = s // m_rows

    def body(full_in, c_ref, w_hbm, b_ref, x_hbm, g_ref, full_ref, mod_ref, cact_ref, h_hbm, z_hbm,
             w_vmem, cbuf, pbuf, rbuf, h_vmem, xbuf, bbuf, zbuf, w_sem, x_sems, b_sem, z_sems, h_sem,
             send_sems, recv_sems, msend_sems, mrecv_sems):
        x, y, cc = _place()
        me = 4 * x + 2 * y + cc
        chip = 2 * x + y
        across_x, across_y, far = (1 - x, y), (x, 1 - y), (1 - x, 1 - y)

        def rows(chip_xy, core, q):
            return full_ref.at[pl.ds(shard * (2 * chip_xy[0] + chip_xy[1]) + half * core + quarter * q, quarter), :]

        def copy(k, piece, to):
            return pltpu.make_async_remote_copy(src_ref=piece, dst_ref=piece, send_sem=send_sems.at[k], recv_sem=recv_sems.at[k],
                                                device_id=to, device_id_type=MESH)

        def small(k, src, dst, to):
            return pltpu.make_async_remote_copy(src_ref=src, dst_ref=dst, send_sem=msend_sems.at[k], recv_sem=mrecv_sems.at[k],
                                                device_id=to, device_id_type=MESH)

        def x_chunk(i):
            return pltpu.make_async_copy(x_hbm.at[pl.ds(rows_at_once * i, rows_at_once), :], xbuf.at[i % 2], x_sems.at[i % 2])

        direct = [copy(0, rows((x, y), cc, 0), (*across_x, cc)), copy(2, rows((x, y), cc, 1), (*across_y, cc)),
                  copy(1, rows((x, y), cc, 1), (*across_x, cc)), copy(3, rows((x, y), cc, 0), (*across_y, cc))]
        landing = [(0, across_x, 0, (4, (*across_y, cc))), (2, across_y, 1, (5, (*across_x, cc))),
                   (1, across_x, 1, None), (3, across_y, 0, None), (4, far, 0, None), (5, far, 1, None)]
        passed = []

        def land(n):
            k, whose, q, onward = landing[n]
            piece = rows(whose, cc, q)
            copy(k, piece, (x, y, cc)).wait_recv()
            if onward is not None:
                passed.append(copy(onward[0], piece, onward[1]))
                passed[-1].start()
            passed.append(copy(6 + n, piece, (x, y, 1 - cc)))
            passed[-1].start()

        def from_sibling(n):
            k, whose, q, onward = landing[n]
            copy(6 + n, rows(whose, 1 - cc, q), (x, y, cc)).wait_recv()

        cbuf[me] = c_ref[...]
        gathers = [small(k - 1, cbuf.at[me], cbuf.at[me], (_flip(x, (k >> 2) & 1), _flip(y, (k >> 1) & 1), _flip(cc, k & 1)))
                   for k in range(1, N_DEV)]
        for cp in gathers:
            cp.start()
        direct[0].start()
        direct[1].start()
        w_load = pltpu.make_async_copy(w_hbm, w_vmem, w_sem)
        w_load.start()
        x_chunk(0).start()
        for cp in gathers:
            cp.wait()
        w_load.wait()
        for b in range(N_DEV):
            cv = cbuf[b]
            cact_ref[b:b + 1, :] = cv * _sigmoid(cv)
        acc = jnp.zeros((N_DEV, ADA_SHARD), F32)
        for k0 in range(0, kdim, kc):
            acc = acc + _dot(cact_ref[:, k0:k0 + kc].astype(BF16), w_vmem[k0:k0 + kc, :].astype(BF16), NN)
        for b in range(N_DEV):
            pbuf[b] = acc[b:b + 1, :]
        rbuf[chip] = pbuf[me]
        sends = [small(N_DEV - 1 + j, pbuf.at[4 * px + 2 * py + cc], rbuf.at[chip], (px, py, cc))
                 for j, (px, py) in enumerate(_other_chips(x, y))]
        for cp in sends:
            cp.start()
        direct[2].start()
        direct[3].start()
        land(0)
        land(1)
        for cp in sends:
            cp.wait()
        for k in range(N_CHIPS):
            cols = slice(ADA_SHARD * k, ADA_SHARD * (k + 1))
            mod_ref[:, cols] = rbuf[k] + b_ref[:, cols]

        for i in range(n_chunks):
            x_chunk(i).wait()
            if i + 1 < n_chunks:
                x_chunk(i + 1).start()
            xv = xbuf[i % 2]
            r = lax.rsqrt(jnp.mean(xv * xv, axis=-1, keepdims=True) + EPS)
            h_vmem[rows_at_once * i:rows_at_once * (i + 1), :] = (
                (xv * r * g_ref[...]) * (1.0 + mod_ref[:, d:2 * d]) + mod_ref[:, 0:d]).astype(BF16)
        h_out = pltpu.make_async_copy(h_vmem, h_hbm, h_sem)
        h_out.start()

        def project(tile):
            first = pl.multiple_of(IN_TILE * tile, 128)
            b_load = pltpu.make_async_copy(full_ref.at[pl.ds(first, IN_TILE), :], bbuf, b_sem)
            b_load.start()
            b_load.wait()
            outs = []
            for m in range(n_m):
                if m >= 2:
                    outs[m - 2].wait()
                zbuf[m % 2] = _dot(h_vmem[m_rows * m:m_rows * (m + 1), :], bbuf[...], NT)
                outs.append(pltpu.make_async_copy(zbuf.at[m % 2], z_hbm.at[pl.ds(m_rows * m, m_rows), pl.ds(first, IN_TILE)],
                                                  z_sems.at[m % 2]))
                outs[m].start()
            for cp in outs[max(n_m - 2, 0):]:
                cp.wait()

        project(2 * chip)
        land(2)
        land(3)
        for n in range(4):
            from_sibling(n)
        project(2 * (2 * across_x[0] + across_x[1]))
        project(2 * (2 * across_y[0] + across_y[1]))
        land(4)
        land(5)
        from_sibling(4)
        from_sibling(5)
        project(2 * (2 * far[0] + far[1]))
        h_out.wait()
        for cp in direct + passed:
            cp.wait_send()

    return pl.pallas_call(
        body, name="w_in_gather_modulate_project",
        out_shape=(jax.ShapeDtypeStruct(full.shape, full.dtype), jax.ShapeDtypeStruct((1, 3 * D_MODEL), F32),
                   jax.ShapeDtypeStruct((N_DEV, D_MODEL), F32), jax.ShapeDtypeStruct((s, d), BF16),
                   jax.ShapeDtypeStruct((s, full.shape[0]), F32)),
        in_specs=[ANY, VMEM, ANY, VMEM, ANY, VMEM], out_specs=(ANY, VMEM, VMEM, ANY, ANY), input_output_aliases={0: 0},
        scratch_shapes=[pltpu.VMEM(w_ada.shape, F32), pltpu.VMEM((N_DEV, 1, D_MODEL), F32), pltpu.VMEM((N_DEV, 1, ADA_SHARD), F32),
                        pltpu.VMEM((N_CHIPS, 1, ADA_SHARD), F32), pltpu.VMEM((s, d), BF16), pltpu.VMEM((2, rows_at_once, d), F32),
                        pltpu.VMEM((IN_TILE, d), BF16), pltpu.VMEM((2, m_rows, IN_TILE), F32),
                        pltpu.SemaphoreType.DMA, pltpu.SemaphoreType.DMA((2,)), pltpu.SemaphoreType.DMA, pltpu.SemaphoreType.DMA((2,)),
                        pltpu.SemaphoreType.DMA,
                        pltpu.SemaphoreType.DMA((12,)), pltpu.SemaphoreType.DMA((12,)),
                        pltpu.SemaphoreType.DMA((N_DEV + 2,)), pltpu.SemaphoreType.DMA((N_DEV + 2,))],
        compiler_params=_params(),
    )(full, c, w_ada, b_ada, xs, norm_g)


def _w_in_grad_chip_partial(dz, h):
    t_rows, m = dz.shape
    n_tiles = m // IN_TILE

    def body(dz_hbm, h_hbm, p_hbm, h_theirs, h_mine, abuf, sendbuf, recvbuf, obuf, h_sems, a_sems, o_sems, send_sems, recv_sems):
        x, y, cc = _place()

        def a_tile(t, n):
            return pltpu.make_async_copy(dz_hbm.at[:, pl.ds(IN_TILE * t, IN_TILE)], abuf.at[n % 2], a_sems.at[n % 2])

        def half_cols(core):
            return h_hbm.at[:, pl.ds(pl.multiple_of(HALF * core, 128), HALF)]

        loads = [pltpu.make_async_copy(half_cols(1 - cc), h_theirs, h_sems.at[0]),
                 pltpu.make_async_copy(half_cols(cc), h_mine, h_sems.at[1])]
        for cp in loads:
            cp.start()
        a_tile(0, 0).start()
        loads[0].wait()
        sent = []
        for t in range(n_tiles):
            a_tile(t, t).wait()
            a_tile((t + 1) % n_tiles, t + 1).start()
            sendbuf[t] = _dot(abuf[t % 2], h_theirs[...], TN).astype(BF16)
            sent.append(pltpu.make_async_remote_copy(src_ref=sendbuf.at[t], dst_ref=recvbuf.at[t], send_sem=send_sems.at[t],
                                                     recv_sem=recv_sems.at[t], device_id=(x, y, 1 - cc), device_id_type=MESH))
            sent[t].start()
        loads[1].wait()
        outs = []
        for t in range(n_tiles):
            n = n_tiles + t
            a_tile(t, n).wait()
            if t + 1 < n_tiles:
                a_tile(t + 1, n + 1).start()
            sent[t].wait_recv()
            if t >= 2:
                outs[t - 2].wait()
            obuf[t % 2] = (_dot(abuf[n % 2], h_mine[...], TN) + recvbuf[t].astype(F32)).astype(BF16)
            outs.append(pltpu.make_async_copy(obuf.at[t % 2], p_hbm.at[pl.ds(IN_TILE * t, IN_TILE), :], o_sems.at[t % 2]))
            outs[t].start()
        for cp in outs[max(n_tiles - 2, 0):]:
            cp.wait()
        for cp in sent:
            cp.wait_send()

    return pl.pallas_call(
        body, name="w_in_grad_chip_partial", out_shape=jax.ShapeDtypeStruct((m, HALF), BF16),
        in_specs=[ANY, ANY], out_specs=ANY,
        scratch_shapes=[pltpu.VMEM((t_rows, HALF), BF16), pltpu.VMEM((t_rows, HALF), BF16), pltpu.VMEM((2, t_rows, IN_TILE), BF16),
                        pltpu.VMEM((n_tiles, IN_TILE, HALF), BF16), pltpu.VMEM((n_tiles, IN_TILE, HALF), BF16),
                        pltpu.VMEM((2, IN_TILE, HALF), BF16),
                        pltpu.SemaphoreType.DMA((2,)), pltpu.SemaphoreType.DMA((2,)), pltpu.SemaphoreType.DMA((2,)),
                        pltpu.SemaphoreType.DMA((n_tiles,)), pltpu.SemaphoreType.DMA((n_tiles,))],
        compiler_params=_params(),
    )(dz, h)


def _group_copies(whole, src_ref, land_ref, send_sems, recv_sems, first_sem):
    x, y, cc = _place()
    me = 4 * x + 2 * y + cc
    cps = []
    for k in range(1, N_DEV):
        px, py, pc = _flip(x, (k >> 2) & 1), _flip(y, (k >> 1) & 1), _flip(cc, k & 1)
        src = src_ref if whole else src_ref.at[4 * px + 2 * py + pc]
        cps.append(pltpu.make_async_remote_copy(src_ref=src, dst_ref=land_ref.at[me], send_sem=send_sems.at[first_sem + k - 1],
                                                recv_sem=recv_sems.at[first_sem + k - 1], device_id=(px, py, pc), device_id_type=MESH))
    return cps


def _chip_copies(pieces, p_ref, land_ref, send_sems, recv_sems):
    x, y, cc = _place()
    chip = 2 * x + y
    shard = land_ref.shape[1]
    others = _other_chips(x, y)
    cps = []
    for k, (j, r0, nr) in enumerate(pieces):
        px, py = others[j]
        cps.append(pltpu.make_async_remote_copy(
            src_ref=p_ref.at[pl.ds(shard * (2 * px + py) + r0, nr), :], dst_ref=land_ref.at[chip, pl.ds(r0, nr), :],
            send_sem=send_sems.at[k], recv_sem=recv_sems.at[k], device_id=(px, py, cc), device_id_type=MESH))
    return cps


def _owner_copies(p_ref, land_ref, send_sems, recv_sems):
    x, y, cc = _place()
    me = 4 * x + 2 * y + cc
    shard = land_ref.shape[1]
    cps = []
    for k in range(1, N_DEV):
        px, py, pc = _flip(x, (k >> 2) & 1), _flip(y, (k >> 1) & 1), _flip(cc, k & 1)
        block = p_ref.at[pl.ds(shard * (2 * px + py), shard), pl.ds(pl.multiple_of(HALF * pc, 128), HALF)]
        cps.append(pltpu.make_async_remote_copy(src_ref=block, dst_ref=land_ref.at[me], send_sem=send_sems.at[k - 1],
                                                recv_sem=recv_sems.at[k - 1], device_id=(px, py, pc), device_id_type=MESH))
    return cps


def _reduce_and_swap(slots, own, rows_tile, name, after):
    n_slots, r, hcols = slots.shape
    n_tiles = r // rows_tile
    per_device = n_slots == N_DEV

    def body(slots_hbm, own_hbm, after_ref, f_hbm, sbuf, obuf, fbuf, in_sems, out_sems, send_sems, recv_sems):
        x, y, cc = _place()
        chip = 2 * x + y
        mine = 2 * chip + cc if per_device else chip
        my_cols = pl.ds(pl.multiple_of(hcols * cc, 128), hcols)

        def loads(t):
            rows = pl.ds(rows_tile * t, rows_tile)
            own_rows = pl.ds(r * chip + rows_tile * t, rows_tile)
            own_src = own_hbm.at[own_rows, my_cols] if per_device else own_hbm.at[own_rows, :]
            return ([pltpu.make_async_copy(slots_hbm.at[k, rows, :], sbuf.at[t % 2, k], in_sems.at[t % 2, k]) for k in range(n_slots)]
                    + [pltpu.make_async_copy(own_src, obuf.at[t % 2], in_sems.at[t % 2, n_slots])])

        def mine_out(t):
            return f_hbm.at[pl.ds(rows_tile * t, rows_tile), my_cols]

        for cp in loads(0):
            cp.start()
        local, remote = [], []
        for t in range(n_tiles):
            for cp in loads(t):
                cp.wait()
            if t + 1 < n_tiles:
                for cp in loads(t + 1):
                    cp.start()
            if t >= 2:
                local[t - 2].wait()
                remote[t - 2].wait_send()
            acc = jnp.zeros((rows_tile, hcols), F32)
            for k in range(n_slots):
                acc = acc + jnp.where(mine == k, obuf[t % 2], sbuf[t % 2, k]).astype(F32)
            fbuf[t % 2] = acc
            local.append(pltpu.make_async_copy(fbuf.at[t % 2], mine_out(t), out_sems.at[t % 2]))
            remote.append(pltpu.make_async_remote_copy(src_ref=fbuf.at[t % 2], dst_ref=mine_out(t), send_sem=send_sems.at[t],
                                                       recv_sem=recv_sems.at[t], device_id=(x, y, 1 - cc), device_id_type=MESH))
            local[t].start()
            remote[t].start()
        for t in range(max(n_tiles - 2, 0), n_tiles):
            local[t].wait()
            remote[t].wait_send()
        for t in range(n_tiles):
            remote[t].wait_recv()

    return pl.pallas_call(
        body, name=name, out_shape=jax.ShapeDtypeStruct((r, 2 * hcols), F32), in_specs=[ANY, ANY, ANY], out_specs=ANY,
        scratch_shapes=[pltpu.VMEM((2, n_slots, rows_tile, hcols), BF16), pltpu.VMEM((2, rows_tile, hcols), BF16),
                        pltpu.VMEM((2, rows_tile, hcols), F32), pltpu.SemaphoreType.DMA((2, n_slots + 1)),
                        pltpu.SemaphoreType.DMA((2,)), pltpu.SemaphoreType.DMA((n_tiles,)), pltpu.SemaphoreType.DMA((n_tiles,))],
        compiler_params=_params(),
    )(slots, own, after)


ROW_NORM_G, ROW_DMOD, ROW_FINAL_G, ROW_LN, ROW_SINKS, ROW_LOSS, ROW_SGU_B, MISC_ROWS = 0, 1, 4, 5, 6, 7, 8, 16


def _small_grads_sum(parts, wland, after):
    part_names = ["dng", "dshift", "dscale", "dgate", "dfg", "dlg", "dlb", "dsinks", "dsgu_b", "dsgu_w", "loss"]
    n_parts = len(part_names)
    n_peers = N_DEV - 1

    def body(*refs):
        p = dict(zip(part_names, refs[:n_parts]))
        wland_ref = refs[n_parts]
        tot_ref, gw_ref, dmod_ref = refs[n_parts + 2:n_parts + 5]
        misc, cmisc, dsend, dbuf, own_sem, send_sems, recv_sems = refs[n_parts + 5:]
        x, y, cc = _place()
        me = 4 * x + 2 * y + cc
        chip = 2 * x + y
        peers = [(_flip(x, (k >> 2) & 1), _flip(y, (k >> 1) & 1), _flip(cc, k & 1)) for k in range(1, N_DEV)]

        def remote(k, src, dst, peer):
            return pltpu.make_async_remote_copy(src_ref=src, dst_ref=dst, send_sem=send_sems.at[k], recv_sem=recv_sems.at[k],
                                                device_id=peer, device_id_type=MESH)

        dsend[:, 0:D_MODEL] = p["dshift"][...]
        dsend[:, D_MODEL:2 * D_MODEL] = p["dscale"][...]
        dsend[:, 2 * D_MODEL:3 * D_MODEL] = p["dgate"][...]

        def shard_cols(chip_index):
            return dsend.at[:, pl.ds(pl.multiple_of(ADA_SHARD * chip_index, 128), ADA_SHARD)]

        own_cols = pltpu.make_async_copy(shard_cols(chip), dbuf.at[me], own_sem)
        own_cols.start()
        gathers = [remote(k, shard_cols(2 * px + py), dbuf.at[me], (px, py, pc)) for k, (px, py, pc) in enumerate(peers)]

        misc[cc] = jnp.zeros((MISC_ROWS, D_MODEL), F32)
        misc[cc, ROW_NORM_G:ROW_NORM_G + 1, :] = p["dng"][...]
        misc[cc, ROW_DMOD:ROW_DMOD + 1, :] = p["dshift"][...]
        misc[cc, ROW_DMOD + 1:ROW_DMOD + 2, :] = p["dscale"][...]
        misc[cc, ROW_DMOD + 2:ROW_DMOD + 3, :] = p["dgate"][...]
        misc[cc, ROW_FINAL_G:ROW_FINAL_G + 1, :] = p["dfg"][...]
        misc[cc, ROW_LN:ROW_LN + 1, 0:D_SGU] = p["dlg"][...]
        misc[cc, ROW_LN:ROW_LN + 1, D_SGU:2 * D_SGU] = p["dlb"][...]
        misc[cc, ROW_SINKS:ROW_SINKS + 1, 0:N_Q_HEADS] = p["dsinks"][...]
        misc[cc, ROW_LOSS:ROW_LOSS + 1, 0:1] = p["loss"][...]
        misc[cc, ROW_SGU_B:ROW_SGU_B + SGU_GROUPS, 0:BLOCK] = p["dsgu_b"][...]
        pair = remote(n_peers, misc.at[cc], misc.at[cc], (x, y, 1 - cc))
        for cp in gathers + [pair]:
            cp.start()
        pair.wait()
        cmisc[chip] = misc[0] + misc[1]
        quad = [remote(n_peers + 1 + j, cmisc.at[chip], cmisc.at[chip], (_flip(x, bx), _flip(y, by), cc))
                for j, (bx, by) in enumerate(((0, 1), (1, 0), (1, 1)))]
        for cp in quad:
            cp.start()
        own_group = p["dsgu_w"][me]
        group = jnp.zeros((BLOCK, BLOCK), F32)
        for b in range(N_DEV):
            group = group + jnp.where(me == b, own_group, wland_ref[b])
        gw_ref[...] = group
        for cp in gathers + quad:
            cp.wait()

        tot = cmisc[0]
        for k in range(1, N_CHIPS):
            tot = tot + cmisc[k]
        tot_ref[...] = tot
        own_cols.wait()
        for b in range(N_DEV):
            dmod_ref[b:b + 1, :] = dbuf[b]

    operands = [parts[n] for n in part_names] + [wland]
    n_sems = n_peers + 1 + (N_CHIPS - 1)
    return pl.pallas_call(
        body, name="small_grads_sum",
        out_shape=(jax.ShapeDtypeStruct((MISC_ROWS, D_MODEL), F32), jax.ShapeDtypeStruct((BLOCK, BLOCK), F32),
                   jax.ShapeDtypeStruct((N_DEV, ADA_SHARD), F32)),
        in_specs=[VMEM] * len(operands) + [ANY], out_specs=(VMEM, VMEM, VMEM),
        scratch_shapes=[pltpu.VMEM((2, MISC_ROWS, D_MODEL), F32), pltpu.VMEM((N_CHIPS, MISC_ROWS, D_MODEL), F32),
                        pltpu.VMEM((1, 3 * D_MODEL), F32), pltpu.VMEM((N_DEV, 1, ADA_SHARD), F32),
                        pltpu.SemaphoreType.DMA, pltpu.SemaphoreType.DMA((n_sems,)), pltpu.SemaphoreType.DMA((n_sems,))],
        compiler_params=_params(),
    )(*operands, after)


def _small_adamw(tot, gw_own, gw_all, place, small):
    names = ["norm_g", "b_ada", "attn_sinks", "sgu_ln_g", "sgu_ln_b", "sgu_w", "sgu_b", "final_g"]

    def body(*refs):
        tot_ref, own_ref, all_ref, place_ref = refs[:4]
        me = 2 * place_ref[1] + place_ref[0]
        wmv = {n: refs[4 + 3 * i: 7 + 3 * i] for i, n in enumerate(names)}
        outs = refs[4 + 3 * len(names):]
        res = {n: outs[4 * i: 4 * i + 4] for i, n in enumerate(names)}
        outs[4 * len(names)][...] = tot_ref[ROW_LOSS:ROW_LOSS + 1, 0:1]

        def update(name, g, pick=lambda r: r[...], put=None):
            w_ref, m_ref, v_ref = wmv[name]
            delta, m2, v2 = _adamw(pick(w_ref), g, pick(m_ref), pick(v_ref))
            for ref, val in zip(res[name], (g, delta, m2, v2)):
                if put is None:
                    ref[...] = val
                else:
                    put(ref, val)

        update("norm_g", tot_ref[ROW_NORM_G:ROW_NORM_G + 1, :])
        update("final_g", tot_ref[ROW_FINAL_G:ROW_FINAL_G + 1, :])
        for t in range(3):
            cols = slice(D_MODEL * t, D_MODEL * (t + 1))

            def put_cols(ref, val, cols=cols):
                ref[:, cols] = val
            update("b_ada", tot_ref[ROW_DMOD + t:ROW_DMOD + t + 1, :], pick=lambda r, cols=cols: r[:, cols], put=put_cols)
        update("sgu_ln_g", tot_ref[ROW_LN:ROW_LN + 1, 0:D_SGU])
        update("sgu_ln_b", tot_ref[ROW_LN:ROW_LN + 1, D_SGU:2 * D_SGU])
        update("attn_sinks", tot_ref[ROW_SINKS:ROW_SINKS + 1, 0:N_Q_HEADS])
        update("sgu_b", tot_ref[ROW_SGU_B:ROW_SGU_B + SGU_GROUPS, 0:BLOCK])
        for g in range(SGU_GROUPS):
            def put_group(ref, val, g=g):
                ref[g] = val
            update("sgu_w", jnp.where(me == g, own_ref[...], all_ref[g]), pick=lambda r, g=g: r[g], put=put_group)

    shapes = {"norm_g": (1, D_MODEL), "b_ada": (1, 3 * D_MODEL), "attn_sinks": (1, N_Q_HEADS), "sgu_ln_g": (1, D_SGU),
              "sgu_ln_b": (1, D_SGU), "sgu_w": (SGU_GROUPS, BLOCK, BLOCK), "sgu_b": (SGU_GROUPS, BLOCK), "final_g": (1, D_MODEL)}
    operands = [tot, gw_own, gw_all, place]
    for n in names:
        operands += [a.reshape(shapes[n]) for a in small[n]]
    out_shape = []
    for n in names:
        out_shape += [jax.ShapeDtypeStruct(shapes[n], F32)] * 4
    out_shape.append(jax.ShapeDtypeStruct((1, 1), F32))
    outs = pl.pallas_call(
        body, name="small_adamw", out_shape=tuple(out_shape),
        in_specs=[VMEM, VMEM, VMEM, pl.BlockSpec(memory_space=pltpu.SMEM)] + [VMEM] * (len(operands) - 4),
        out_specs=tuple([VMEM] * len(out_shape)), compiler_params=_params(),
    )(*operands)
    return {n: outs[4 * i: 4 * i + 4] for i, n in enumerate(names)}, outs[4 * len(names)]


def _cast_into_gathered(w, tile, place, name, after):
    r, c = w.shape
    nblk = r // tile

    def body(place_ref, w_ref, after_ref, o_ref):
        o_ref[...] = w_ref[...].astype(BF16)

    return pl.pallas_call(
        body, name=name,
        grid_spec=pltpu.PrefetchScalarGridSpec(
            num_scalar_prefetch=1, grid=(nblk,), in_specs=[pl.BlockSpec((tile, c), lambda i, p: (i, 0)), ANY],
            out_specs=pl.BlockSpec((tile, c), lambda i, p: (p[1] * nblk + i, 0))),
        out_shape=jax.ShapeDtypeStruct((N_CHIPS * r, c), BF16),
        compiler_params=_params(("parallel",)),
    )(place, w, after)


def _in_proj_odd(h, win_t, z, after):
    s, d = h.shape
    tm = min(1024, s)
    n_odd = win_t.shape[0] // IN_TILE // 2

    def body(h_ref, w_ref, z_in, after_ref, z_ref):
        z_ref[...] = _dot(h_ref[...], w_ref[...], NT)

    return pl.pallas_call(
        body, name="in_proj_odd", grid=(s // tm, n_odd), out_shape=jax.ShapeDtypeStruct(z.shape, z.dtype),
        in_specs=[pl.BlockSpec((tm, d), lambda i, j: (i, 0)), pl.BlockSpec((IN_TILE, d), lambda i, j: (2 * j + 1, 0)), ANY, ANY],
        out_specs=pl.BlockSpec((tm, IN_TILE), lambda i, j: (i, 2 * j + 1)), input_output_aliases={2: 0},
        compiler_params=_params(("parallel", "parallel")),
    )(h, win_t, z, after)


def _matmul_nt(a, bt, tm, out_dtype, name, after):
    m, k = a.shape
    n = bt.shape[0]
    tm = min(tm, m)

    def body(a_ref, b_ref, after_ref, o_ref):
        o_ref[...] = _dot(a_ref[...], b_ref[...], NT).astype(out_dtype)

    return pl.pallas_call(
        body, name=name, grid=(m // tm,), out_shape=jax.ShapeDtypeStruct((m, n), out_dtype),
        in_specs=[pl.BlockSpec((tm, k), lambda i: (i, 0)), pl.BlockSpec((n, k), lambda i: (0, 0), pipeline_mode=pl.Buffered(1)), ANY],
        out_specs=pl.BlockSpec((tm, n), lambda i: (i, 0)),
        compiler_params=_params(("parallel",)),
    )(a, bt, after)


def _matmul_tn(a, b, tm, name):
    t, m = a.shape
    n = b.shape[1]
    tm = min(tm, m)

    def body(a_ref, b_ref, o_ref):
        o_ref[...] = _dot(a_ref[...], b_ref[...], TN).astype(BF16)

    return pl.pallas_call(
        body, name=name, grid=(m // tm,), out_shape=jax.ShapeDtypeStruct((m, n), BF16),
        in_specs=[pl.BlockSpec((t, tm), lambda i: (0, i)), pl.BlockSpec((t, n), lambda i: (0, 0))],
        out_specs=pl.BlockSpec((tm, n), lambda i: (i, 0)),
        compiler_params=_params(("parallel",)),
    )(a, b)


def _mixer_masks(n):
    key = lax.broadcasted_iota(jnp.int32, (2 * BLOCK, Q_PER_KV * BLOCK), 0)
    qry = lax.broadcasted_iota(jnp.int32, (2 * BLOCK, Q_PER_KV * BLOCK), 1) & (BLOCK - 1)
    valid = (key > qry) & (key <= qry + BLOCK) & ((key >= BLOCK) | (n > 0))
    r2 = lax.broadcasted_iota(jnp.int32, (BLOCK, BLOCK), 0)
    c2 = lax.broadcasted_iota(jnp.int32, (BLOCK, BLOCK), 1)
    return valid, r2 >= c2


def _layer_norm_parts(vs):
    mu = jnp.mean(vs, axis=-1, keepdims=True)
    xc = vs - mu
    rstd = lax.rsqrt(jnp.mean(xc * xc, axis=-1, keepdims=True) + EPS)
    return xc * rstd, rstd


def _stacked_queries(z_ref, hk, scale=None):
    q = jnp.concatenate([z_ref[:, _head_cols(Q0, Q_PER_KV * hk + g)] for g in range(Q_PER_KV)], axis=0)
    return (q if scale is None else q * scale).astype(BF16)


def _head_cols(base, h):
    return slice(base + HEAD_DIM * h, base + HEAD_DIM * (h + 1))


def _sink_row(sink_ref, hk):
    return jnp.concatenate([jnp.broadcast_to(sink_ref[:, h:h + 1], (1, BLOCK))
                            for h in range(Q_PER_KV * hk, Q_PER_KV * (hk + 1))], axis=1)


def _attn_probs(qs, kh, valid, sk, scale=None):
    s = _dot(kh, qs, NT)
    s = jnp.where(valid, s if scale is None else s * scale, -jnp.inf)
    m = jnp.maximum(jnp.max(s, axis=0, keepdims=True), sk)
    e = jnp.exp(s - m)
    es = jnp.exp(sk - m)
    inv = 1.0 / (jnp.sum(e, axis=0, keepdims=True) + es)
    return e * inv, es * inv


def _kv_ext(z_ref, kvp_ref):
    k_ext = jnp.concatenate([kvp_ref[:, 0:D_KV], z_ref[:, K0:K0 + D_KV]], axis=0).astype(BF16)
    v_ext = jnp.concatenate([kvp_ref[:, D_KV:2 * D_KV], z_ref[:, V0:V0 + D_KV]], axis=0).astype(BF16)
    return k_ext, v_ext


def _sgu_mixed(w_ref, bt_ref, vn, tril):
    ws = [jnp.where(tril, w_ref[g], 0.0).astype(BF16) for g in range(SGU_GROUPS)]
    mixed = jnp.concatenate([_dot(ws[g], vn[:, GROUP_DIM * g:GROUP_DIM * (g + 1)], NN) + bt_ref[:, g:g + 1]
                             for g in range(SGU_GROUPS)], axis=1)
    return ws, mixed


def _mixer_fwd(z, sinks, ln_g, ln_b, sgu_w, sgu_bt):
    s = z.shape[0]
    nb = s // BLOCK

    def body(z_ref, kvp_ref, sink_ref, lg_ref, lb_ref, w_ref, bt_ref, cat_ref, o_ref):
        n = pl.program_id(0)
        valid, tril = _mixer_masks(n)
        k_ext, v_ext = _kv_ext(z_ref, kvp_ref)
        for hk in range(2):
            kv = slice(HEAD_DIM * hk, HEAD_DIM * (hk + 1))
            p, _ = _attn_probs(_stacked_queries(z_ref, hk), k_ext[:, kv], valid, _sink_row(sink_ref, hk), ATTN_SCALE)
            o_t = _dot(v_ext[:, kv], p.astype(BF16), TN)
            for g in range(Q_PER_KV):
                o_ref[:, _head_cols(0, Q_PER_KV * hk + g)] = o_t[:, BLOCK * g:BLOCK * (g + 1)].T
        ga = z_ref[:, GA0:GA0 + D_ATTN]
        cat_ref[:, 0:D_ATTN] = (o_ref[...] * (ga * _sigmoid(ga))).astype(BF16)
        xhat, _ = _layer_norm_parts(z_ref[:, VS0:VS0 + D_SGU])
        vn = (xhat * lg_ref[...] + lb_ref[...]).astype(BF16)
        _, mixed = _sgu_mixed(w_ref, bt_ref, vn, tril)
        gt = z_ref[:, GS0:GS0 + D_SGU]
        cat_ref[:, D_ATTN:D_MODEL] = ((z_ref[:, U0:U0 + D_SGU] * mixed) * (gt * _sigmoid(gt))).astype(BF16)

    const2 = lambda n: (0, 0)
    return pl.pallas_call(
        body, name="mixer_fwd", grid=(nb,),
        out_shape=(jax.ShapeDtypeStruct((s, D_MODEL), BF16), jax.ShapeDtypeStruct((s, D_ATTN), F32)),
        in_specs=[pl.BlockSpec((BLOCK, D_IN), lambda n: (n, 0)),
                  pl.BlockSpec((BLOCK, 2 * D_KV), lambda n: (jnp.maximum(n - 1, 0), K0 // (2 * D_KV))),
                  pl.BlockSpec((1, N_Q_HEADS), const2), pl.BlockSpec((1, D_SGU), const2), pl.BlockSpec((1, D_SGU), const2),
                  pl.BlockSpec((SGU_GROUPS, BLOCK, BLOCK), lambda n: (0, 0, 0)), pl.BlockSpec((BLOCK, SGU_GROUPS), const2)],
        out_specs=(pl.BlockSpec((BLOCK, D_MODEL), lambda n: (n, 0)), pl.BlockSpec((BLOCK, D_ATTN), lambda n: (n, 0))),
        compiler_params=_params(("parallel",)),
    )(z, z, sinks, ln_g, ln_b, sgu_w, sgu_bt)


def _out_proj_loss(cat, wout, x, target, mod, final_g):
    s, d = x.shape
    tm = min(256, s)

    def body(cat_ref, w_ref, x_ref, t_ref, mod_ref, fg_ref, dx1_ref, dy_ref, dfg_ref, dgate_ref, loss_ref):
        @pl.when(pl.program_id(0) == 0)
        def _():
            dfg_ref[...] = jnp.zeros_like(dfg_ref)
            dgate_ref[...] = jnp.zeros_like(dgate_ref)
            loss_ref[...] = jnp.zeros_like(loss_ref)

        gate = mod_ref[:, 2 * d:3 * d]
        fg = fg_ref[...]
        yv = _dot(cat_ref[...], w_ref[...], NN)
        x1 = x_ref[...] + gate * yv
        r = lax.rsqrt(jnp.mean(x1 * x1, axis=-1, keepdims=True) + EPS)
        xr = x1 * r
        err = xr * fg - t_ref[...]
        dout = err * (1.0 / d)
        gy = dout * fg
        dx1 = r * gy - xr * (r * r * jnp.mean(gy * x1, axis=-1, keepdims=True))
        dx1_ref[...] = dx1
        dy_ref[...] = (gate * dx1).astype(BF16)
        dfg_ref[...] += jnp.sum(dout * xr, axis=0, keepdims=True)
        dgate_ref[...] += jnp.sum(dx1 * yv, axis=0, keepdims=True)
        loss_ref[...] += 0.5 * jnp.sum(jnp.mean(err * err, axis=-1, keepdims=True), axis=0, keepdims=True)

    row = lambda i: (i, 0)
    const = lambda i: (0, 0)
    return pl.pallas_call(
        body, name="out_proj_loss", grid=(s // tm,),
        out_shape=(jax.ShapeDtypeStruct((s, d), F32), jax.ShapeDtypeStruct((s, d), BF16), jax.ShapeDtypeStruct((1, d), F32),
                   jax.ShapeDtypeStruct((1, d), F32), jax.ShapeDtypeStruct((1, 1), F32)),
        in_specs=[pl.BlockSpec((tm, d), row), pl.BlockSpec((d, d), const), pl.BlockSpec((tm, d), row), pl.BlockSpec((tm, d), row),
                  pl.BlockSpec((1, 3 * d), const), pl.BlockSpec((1, d), const)],
        out_specs=(pl.BlockSpec((tm, d), row), pl.BlockSpec((tm, d), row), pl.BlockSpec((1, d), const), pl.BlockSpec((1, d), const),
                   pl.BlockSpec((1, 1), const)),
        compiler_params=_params(("arbitrary",)),
    )(cat, wout, x, target, mod, final_g)


def _mixer_bwd(z, dcat, o, sinks, ln_g, ln_b, sgu_w, sgu_bt):
    s = z.shape[0]
    nb = s // BLOCK

    def body(z_ref, kvp_ref, dc_ref, o_ref, sink_ref, lg_ref, lb_ref, w_ref, bt_ref,
             dz_ref, dsink_ref, dlg_ref, dlb_ref, dw_ref, db_ref, pend, dkv, carry, dsink_acc, dbt_acc):
        n = pl.program_id(0)

        @pl.when(n == 0)
        def _():
            pend[...] = jnp.zeros_like(pend)
            carry[...] = jnp.zeros_like(carry)
            dsink_acc[...] = jnp.zeros_like(dsink_acc)
            dbt_acc[...] = jnp.zeros_like(dbt_acc)
            dlg_ref[...] = jnp.zeros_like(dlg_ref)
            dlb_ref[...] = jnp.zeros_like(dlb_ref)
            dw_ref[...] = jnp.zeros_like(dw_ref)

        dz_ref[:, 0:K0] = pend[:, 0:K0]
        dz_ref[:, GA0:D_IN] = pend[:, GA0:D_IN]

        @pl.when(n == nb)
        def _():
            dkv[...] = jnp.zeros_like(dkv)

        @pl.when(n < nb)
        def _():
            valid, tril = _mixer_masks(n)
            k_ext, v_ext = _kv_ext(z_ref, kvp_ref)
            ga = z_ref[:, GA0:GA0 + D_ATTN]
            sg = _sigmoid(ga)
            datt = dc_ref[:, 0:D_ATTN]
            pend[:, GA0:GA0 + D_ATTN] = (datt * o_ref[...] * (sg * (1.0 + ga * (1.0 - sg)))).astype(BF16)
            do_all = (datt * (ga * sg)).astype(BF16)
            for hk in range(2):
                kv = slice(HEAD_DIM * hk, HEAD_DIM * (hk + 1))
                kh, vh = k_ext[:, kv], v_ext[:, kv]
                qs = _stacked_queries(z_ref, hk, ATTN_SCALE)
                p, p_sink = _attn_probs(qs, kh, valid, _sink_row(sink_ref, hk))
                do = jnp.concatenate([do_all[:, _head_cols(0, Q_PER_KV * hk + g)] for g in range(Q_PER_KV)], axis=0)
                dp = _dot(vh, do, NT)
                delta = jnp.sum(p * dp, axis=0, keepdims=True)
                ds = (p * (dp - delta)).astype(BF16)
                dsk = -p_sink * delta
                dq_t = _dot(kh, ds, TN) * ATTN_SCALE
                for g in range(Q_PER_KV):
                    h = Q_PER_KV * hk + g
                    dsink_acc[:, h:h + 1] += jnp.sum(dsk[:, BLOCK * g:BLOCK * (g + 1)], axis=1, keepdims=True)
                    pend[:, _head_cols(Q0, h)] = dq_t[:, BLOCK * g:BLOCK * (g + 1)].T.astype(BF16)
                dkv[:, kv] = _dot(ds, qs, NN)
                dkv[:, D_KV + HEAD_DIM * hk:D_KV + HEAD_DIM * (hk + 1)] = _dot(p.astype(BF16), do, NN)

            xhat, rstd = _layer_norm_parts(z_ref[:, VS0:VS0 + D_SGU])
            lg = lg_ref[...]
            vn = (xhat * lg + lb_ref[...]).astype(BF16)
            ws, mixed = _sgu_mixed(w_ref, bt_ref, vn, tril)
            u = z_ref[:, U0:U0 + D_SGU]
            gt = z_ref[:, GS0:GS0 + D_SGU]
            sg = _sigmoid(gt)
            dsgu = dc_ref[:, D_ATTN:D_MODEL]
            dgated = dsgu * (gt * sg)
            pend[:, U0:U0 + D_SGU] = (dgated * mixed).astype(BF16)
            pend[:, GS0:GS0 + D_SGU] = (dsgu * (u * mixed) * (sg * (1.0 + gt * (1.0 - sg)))).astype(BF16)
            dmixed = dgated * u
            dmb = dmixed.astype(BF16)
            dvn_parts = []
            for g in range(SGU_GROUPS):
                gs_ = slice(GROUP_DIM * g, GROUP_DIM * (g + 1))
                dw_ref[g] += jnp.where(tril, _dot(dmb[:, gs_], vn[:, gs_], NT), 0.0)
                dbt_acc[:, g:g + 1] += jnp.sum(dmixed[:, gs_], axis=-1, keepdims=True)
                dvn_parts.append(_dot(ws[g], dmb[:, gs_], TN))
            dvn = jnp.concatenate(dvn_parts, axis=1)
            dlg_ref[...] += jnp.sum(dvn * xhat, axis=0, keepdims=True)
            dlb_ref[...] += jnp.sum(dvn, axis=0, keepdims=True)
            dxh = dvn * lg
            pend[:, VS0:VS0 + D_SGU] = (rstd * (dxh - jnp.mean(dxh, axis=-1, keepdims=True)
                                                - xhat * jnp.mean(dxh * xhat, axis=-1, keepdims=True))).astype(BF16)

        dz_ref[:, K0:GA0] = (carry[...] + dkv[0:BLOCK, :]).astype(BF16)
        carry[...] = dkv[BLOCK:2 * BLOCK, :]

        @pl.when(n == nb)
        def _():
            dsink_ref[...] = dsink_acc[:, 0:N_Q_HEADS]
            db_ref[...] = dbt_acc[...].T[0:SGU_GROUPS, :]

    cur = lambda n: (jnp.minimum(n, nb - 1), 0)
    const2 = lambda n: (0, 0)
    return pl.pallas_call(
        body, name="mixer_bwd", grid=(nb + 1,),
        out_shape=(jax.ShapeDtypeStruct((s, D_IN), BF16), jax.ShapeDtypeStruct((1, N_Q_HEADS), F32),
                   jax.ShapeDtypeStruct((1, D_SGU), F32), jax.ShapeDtypeStruct((1, D_SGU), F32),
                   jax.ShapeDtypeStruct((SGU_GROUPS, BLOCK, BLOCK), F32), jax.ShapeDtypeStruct((SGU_GROUPS, BLOCK), F32)),
        in_specs=[pl.BlockSpec((BLOCK, D_IN), cur),
                  pl.BlockSpec((BLOCK, 2 * D_KV), lambda n: (jnp.maximum(jnp.minimum(n, nb - 1) - 1, 0), K0 // (2 * D_KV))),
                  pl.BlockSpec((BLOCK, D_MODEL), cur), pl.BlockSpec((BLOCK, D_ATTN), cur),
                  pl.BlockSpec((1, N_Q_HEADS), const2), pl.BlockSpec((1, D_SGU), const2), pl.BlockSpec((1, D_SGU), const2),
                  pl.BlockSpec((SGU_GROUPS, BLOCK, BLOCK), lambda n: (0, 0, 0)), pl.BlockSpec((BLOCK, SGU_GROUPS), const2)],
        out_specs=(pl.BlockSpec((BLOCK, D_IN), lambda n: (jnp.maximum(n - 1, 0), 0)),
                   pl.BlockSpec((1, N_Q_HEADS), const2), pl.BlockSpec((1, D_SGU), const2), pl.BlockSpec((1, D_SGU), const2),
                   pl.BlockSpec((SGU_GROUPS, BLOCK, BLOCK), lambda n: (0, 0, 0)), pl.BlockSpec((SGU_GROUPS, BLOCK), const2)),
        scratch_shapes=[pltpu.VMEM((BLOCK, D_IN), BF16), pltpu.VMEM((2 * BLOCK, 2 * D_KV), F32), pltpu.VMEM((BLOCK, 2 * D_KV), F32),
                        pltpu.VMEM((1, BLOCK), F32), pltpu.VMEM((BLOCK, BLOCK), F32)],
        compiler_params=_params(("arbitrary",)),
    )(z, z, dcat, o, sinks, ln_g, ln_b, sgu_w, sgu_bt)


def _in_proj_bwd(dz, win_t, x, dx1, norm_g, mod):
    s, d = x.shape
    kdim = dz.shape[1]
    tm = min(256, s)

    def body(dz_ref, w_ref, x_ref, dx1_ref, g_ref, mod_ref, gx_ref, dshift_ref, dscale_ref, dng_ref):
        @pl.when(pl.program_id(0) == 0)
        def _():
            dshift_ref[...] = jnp.zeros_like(dshift_ref)
            dscale_ref[...] = jnp.zeros_like(dscale_ref)
            dng_ref[...] = jnp.zeros_like(dng_ref)

        dh = _dot(dz_ref[...], w_ref[...], NN)
        xv = x_ref[...]
        ng = g_ref[...]
        r = lax.rsqrt(jnp.mean(xv * xv, axis=-1, keepdims=True) + EPS)
        xn = xv * r
        dshift_ref[...] += jnp.sum(dh, axis=0, keepdims=True)
        dscale_ref[...] += jnp.sum(dh * (xn * ng), axis=0, keepdims=True)
        dhs = dh * (1.0 + mod_ref[:, d:2 * d])
        dng_ref[...] += jnp.sum(dhs * xn, axis=0, keepdims=True)
        dxn = dhs * ng
        gx_ref[...] = dx1_ref[...] + r * dxn - xn * (r * r * jnp.mean(dxn * xv, axis=-1, keepdims=True))

    row = lambda i: (i, 0)
    const = lambda i: (0, 0)
    return pl.pallas_call(
        body, name="in_proj_bwd", grid=(s // tm,),
        out_shape=(jax.ShapeDtypeStruct((s, d), F32),) + (jax.ShapeDtypeStruct((1, d), F32),) * 3,
        in_specs=[pl.BlockSpec((tm, kdim), row), pl.BlockSpec((kdim, d), const, pipeline_mode=pl.Buffered(1)),
                  pl.BlockSpec((tm, d), row), pl.BlockSpec((tm, d), row), pl.BlockSpec((1, d), const), pl.BlockSpec((1, 3 * d), const)],
        out_specs=(pl.BlockSpec((tm, d), row),) + (pl.BlockSpec((1, d), const),) * 3,
        compiler_params=_params(("arbitrary",)),
    )(dz, win_t, x, dx1, norm_g, mod)


def _adamw_rows(w, g, m, v, tile, name, after):
    r, c = w.shape

    def body(w_ref, g_ref, m_ref, v_ref, after_ref, go_ref, d_ref, mo_ref, vo_ref):
        g_ = g_ref[...]
        delta, m2, v2 = _adamw(w_ref[...], g_, m_ref[...], v_ref[...])
        go_ref[...] = g_
        d_ref[...] = delta
        mo_ref[...] = m2
        vo_ref[...] = v2

    blk = pl.BlockSpec((tile, c), lambda i: (i, 0))
    return pl.pallas_call(
        body, name=name, grid=(r // tile,), out_shape=(jax.ShapeDtypeStruct((r, c), F32),) * 4,
        in_specs=[blk] * 4 + [ANY], out_specs=(blk,) * 4,
        compiler_params=_params(("parallel",)),
    )(w, g, m, v, after)


def _adamw_ada(cact, dmod, w, m, v, tile=256):
    r, c = w.shape

    def body(c_ref, d_ref, w_ref, m_ref, v_ref, go_ref, dl_ref, mo_ref, vo_ref):
        g_ = _dot(c_ref[...].astype(BF16), d_ref[...].astype(BF16), TN)
        delta, m2, v2 = _adamw(w_ref[...], g_, m_ref[...], v_ref[...])
        go_ref[...] = g_
        dl_ref[...] = delta
        mo_ref[...] = m2
        vo_ref[...] = v2

    blk = pl.BlockSpec((tile, c), lambda i: (i, 0))
    return pl.pallas_call(
        body, name="adamw_ada", grid=(r // tile,), out_shape=(jax.ShapeDtypeStruct((r, c), F32),) * 4,
        in_specs=[pl.BlockSpec((N_DEV, tile), lambda i: (0, i)), pl.BlockSpec((N_DEV, c), lambda i: (0, 0)), blk, blk, blk],
        out_specs=(blk,) * 4,
        compiler_params=_params(("parallel",)),
    )(cact, dmod, w, m, v)


def kernel(x, c, norm_g, w_ada, b_ada, w_in, attn_sinks, sgu_ln_g, sgu_ln_b, sgu_w, sgu_b, w_out, final_g, loss_target, m_norm_g, m_w_ada, m_b_ada, m_w_in, m_attn_sinks, m_sgu_ln_g, m_sgu_ln_b, m_sgu_w, m_sgu_b, m_w_out, m_final_g, v_norm_g, v_w_ada, v_b_ada, v_w_in, v_attn_sinks, v_sgu_ln_g, v_sgu_ln_b, v_sgu_w, v_sgu_b, v_w_out, v_final_g):
    xs, tgt = x[0], loss_target[0]
    core = lax.axis_index("c").astype(jnp.int32).reshape(1)
    chip = 2 * lax.axis_index("x") + lax.axis_index("y")
    place = jnp.stack([core[0], chip.astype(jnp.int32)])
    sgu_bt = sgu_b[0].T

    w_in_t, m_in_t, v_in_t = (jnp.swapaxes(a[0], 0, 1) for a in (w_in, m_w_in, v_w_in))
    win_t = _cast_into_gathered(w_in_t, IN_SHARD // 4, place, "cast_w_in", c)
    win_t, mod, cact, h, z = _gather_w_in_modulate_project(win_t, c, w_ada[0], b_ada, xs, norm_g)
    wout = _cast_into_gathered(w_out[0], OUT_SHARD // 2, place, "cast_w_out", win_t)
    sems, bufs, token = _split_start(_gather_copies, [wout], 3, "w_out_gather_start", win_t)
    z = _in_proj_odd(h, win_t, z, token)
    cat, o = _mixer_fwd(z, attn_sinks, sgu_ln_g, sgu_ln_b, sgu_w[0], sgu_bt)
    wout = _gather_finish(_split_wait(_gather_copies, sems, bufs, cat, "w_out_gather_wait")[0], "w_out_gather_finish")
    dx1, dy, dfg, dgate, loss = _out_proj_loss(cat, wout, xs, tgt, mod, final_g.reshape(1, D_MODEL))

    po = _matmul_tn(cat, dy, 512, "w_out_grad")
    sems, bufs, token = _split_start(_owner_copies, [po, lax.empty((N_DEV, OUT_SHARD, HALF), BF16)], N_DEV - 1,
                                     "w_out_grad_to_owners_start", cat)
    dcat = _matmul_nt(dy, wout, 256, F32, "out_proj_bwd", token)
    dz, dsinks, dlg, dlb, dsgu_w, dsgu_b = _mixer_bwd(z, dcat, o, attn_sinks, sgu_ln_g, sgu_ln_b, sgu_w[0], sgu_bt)
    po, slots_o = _split_wait(_owner_copies, sems, bufs, dz, "w_out_grad_to_owners_wait")

    pw = _w_in_grad_chip_partial(dz, h)
    def first(p, land, g, gland, send_sems, recv_sems):
        return (_chip_copies([(0, 0, IN_SHARD), (1, 0, IN_SHARD), (2, 0, IN_SHARD // 4)], p, land, send_sems, recv_sems)
                + _group_copies(False, g, gland, send_sems, recv_sems, 3))

    def rest(p, land, t, gath, send_sems, recv_sems):
        return (_chip_copies([(2, IN_SHARD // 4, 3 * IN_SHARD // 4)], p, land, send_sems, recv_sems)
                + _group_copies(True, t, gath, send_sems, recv_sems, 1))

    groups = jax.ShapeDtypeStruct((N_DEV, BLOCK, BLOCK), F32)
    sems, bufs, token = _split_start(first, [pw, lax.empty((N_CHIPS, IN_SHARD, HALF), BF16), dsgu_w, lax.empty(groups.shape, F32)],
                                     3 + N_DEV - 1, "w_in_grad_to_chips_start", dz)
    gx, dshift, dscale, dng = _in_proj_bwd(dz, win_t, xs, dx1, norm_g + token[0:1, 0:1], mod)
    go = _reduce_and_swap(slots_o, po, OUT_SHARD // 2, "w_out_grad_reduce", gx)
    pw, land_w, dsgu_w, wland = _split_wait(first, sems, bufs, go, "w_in_grad_to_chips_wait")

    parts = dict(dng=dng, dshift=dshift, dscale=dscale, dgate=dgate, dfg=dfg, dlg=dlg, dlb=dlb, dsinks=dsinks,
                 dsgu_b=dsgu_b, dsgu_w=dsgu_w, loss=loss)
    small = dict(norm_g=(norm_g, m_norm_g, v_norm_g), b_ada=(b_ada, m_b_ada, v_b_ada),
                 attn_sinks=(attn_sinks, m_attn_sinks, v_attn_sinks), sgu_ln_g=(sgu_ln_g, m_sgu_ln_g, v_sgu_ln_g),
                 sgu_ln_b=(sgu_ln_b, m_sgu_ln_b, v_sgu_ln_b), sgu_w=(sgu_w, m_sgu_w, v_sgu_w), sgu_b=(sgu_b, m_sgu_b, v_sgu_b),
                 final_g=(final_g, m_final_g, v_final_g))
    tot, gw_own, dmod_mine = _small_grads_sum(parts, wland, land_w)
    sems, bufs, token = _split_start(rest, [pw, land_w, gw_own, lax.empty(groups.shape, F32)], 1 + N_DEV - 1,
                                     "w_in_grad_rest_to_chips_start", tot)
    res = {}
    res["w_ada"] = _adamw_ada(cact, dmod_mine + token[0:1, 0:1], w_ada[0], m_w_ada[0], v_w_ada[0])
    res["w_out"] = _adamw_rows(w_out[0], go, m_w_out[0], v_w_out[0], 256, "adamw_w_out", res["w_ada"][1])

    pw, slots_w, gw_own, gw_all = _split_wait(rest, sems, bufs, res["w_out"][1], "w_in_grad_rest_to_chips_wait")
    small_res, total = _small_adamw(tot, gw_own, gw_all, place, small)
    res.update(small_res)
    gw_t = _reduce_and_swap(slots_w, pw, IN_SHARD // 4, "w_in_grad_reduce", total)
    res["w_in"] = tuple(jnp.swapaxes(a, 0, 1) for a in _adamw_rows(w_in_t, gw_t, m_in_t, v_in_t, IN_SHARD // 4, "adamw_w_in", gw_t))

    order = ["norm_g", "w_ada", "b_ada", "w_in", "attn_sinks", "sgu_ln_g", "sgu_ln_b", "sgu_w", "sgu_b", "w_out", "final_g"]
    like = dict(norm_g=norm_g, w_ada=w_ada, b_ada=b_ada, w_in=w_in, attn_sinks=attn_sinks, sgu_ln_g=sgu_ln_g, sgu_ln_b=sgu_ln_b,
                sgu_w=sgu_w, sgu_b=sgu_b, w_out=w_out, final_g=final_g)
    outs = [total[0, 0], gx[None]]
    for kind in range(4):
        outs += [res[n][kind].reshape(like[n].shape) for n in order]
    return tuple(outs)
```
